```python
import math
import jax
import jax.numpy as jnp
from jax import lax
import numpy as np

D_MODEL = 1024
BATCH = 16
SEQ = 256
DEPTH = 2
DEC_BATCH = 2
DEC_SEQ = 4096
PAST_LEN = 512

GRID_W = 64
Q_BLOCK = 128
ROPE_BASE = 10000.0
NORM_EPS = 1e-6

HY_WIDTH = 512
HY_BANDS = 16
HY_EMB_DIM = 1 + 2 * HY_BANDS
HY_FILTER_HIDDEN = 64
HY_SHORT = 3
HY_FAST_DECAY_PCT = 0.3
HY_SLOW_DECAY_PCT = 1.5
HY_DECAY_TARGET = 1e-2

MLA_HEADS = 8
MLA_NOPE = 64
MLA_ROPE = 32
MLA_QK = MLA_NOPE + MLA_ROPE
MLA_V = 64
MLA_Q_RANK = 768
MLA_KV_RANK = 256

EVEN_SPLITS = (3 * HY_WIDTH, 3 * HY_WIDTH + MLA_Q_RANK, 3 * HY_WIDTH + MLA_Q_RANK + MLA_KV_RANK)
EVEN_IN = EVEN_SPLITS[2] + MLA_ROPE
EVEN_MIX = HY_WIDTH + MLA_HEADS * MLA_V

DIFF_HEADS = 8
DIFF_DH = 64
DIFF_QK = 4 * DIFF_HEADS * DIFF_DH
ODD_IN = DIFF_QK + DIFF_HEADS * 2 * DIFF_DH
ODD_MIX = DIFF_HEADS * 2 * DIFF_DH

D_FF = 2816
N_EXPERTS = 8
TOP_K = 2
D_FF_EXPERT = 3584

kernel_name = 'hybrid_diffusion_prefix_step'


def _rms(x, g):
    xf = x.astype(jnp.float32)
    y = xf * lax.rsqrt(jnp.mean(jnp.square(xf), axis=-1, keepdims=True) + NORM_EPS)
    return (y * g.astype(jnp.float32)).astype(x.dtype)


def _modulate(h, shift, scale):
    return h * (1.0 + scale) + shift


def _ada_mod(cond, w_mod, b_mod):
    return (jax.nn.silu(cond) @ w_mod + b_mod)[:, None, :]


def _swiglu(h, w_gate, w_up, w_down):
    return (jax.nn.silu(h @ w_gate) * (h @ w_up)) @ w_down


def _axial_rope(x):
    L, R = x.shape[-2], x.shape[-1]
    rows = L // GRID_W
    rr, cc = jnp.meshgrid(jnp.arange(rows), jnp.arange(GRID_W), indexing='ij')
    row_pos = rr.reshape(-1).astype(jnp.float32)
    col_pos = cc.reshape(-1).astype(jnp.float32)
    half = R // 2
    inv_freq = ROPE_BASE ** (-jnp.arange(0, half, 2, dtype=jnp.float32) / half)

    def rot(xa, pos):
        ang = pos[:, None] * inv_freq[None, :]
        cos = jnp.cos(ang).astype(x.dtype)
        sin = jnp.sin(ang).astype(x.dtype)
        x1, x2 = xa[..., :half // 2], xa[..., half // 2:]
        return jnp.concatenate([x1 * cos - x2 * sin, x1 * sin + x2 * cos], axis=-1)

    return jnp.concatenate([rot(x[..., :half], row_pos), rot(x[..., half:], col_pos)], axis=-1)


def _map_query_blocks(fn, qs):
    b, h, lq = qs[0].shape[:3]
    nb = lq // Q_BLOCK
    blocks = tuple(jnp.moveaxis(q.reshape(b, h, nb, Q_BLOCK, q.shape[-1]), 2, 0) for q in qs)
    out = lax.map(fn, blocks)
    out = jnp.moveaxis(out, 0, 2)
    return out.reshape(b, h, lq, out.shape[-1])


def _softmax_attend(q, k, v):
    scale = q.shape[-1] ** -0.5

    def blk(qb):
        (qb,) = qb
        s = jnp.einsum('bhqd,bhkd->bhqk', qb, k, preferred_element_type=jnp.float32) * scale
        pr = jax.nn.softmax(s, axis=-1).astype(v.dtype)
        return jnp.einsum('bhqk,bhkd->bhqd', pr, v)

    return _map_query_blocks(blk, (q,))


def _diff_attend(q1, q2, k1, k2, v, lam):
    scale = q1.shape[-1] ** -0.5

    def blk(qb):
        q1b, q2b = qb
        a1 = jax.nn.softmax(jnp.einsum('bhqd,bhkd->bhqk', q1b, k1, preferred_element_type=jnp.float32) * scale, axis=-1)
        a2 = jax.nn.softmax(jnp.einsum('bhqd,bhkd->bhqk', q2b, k2, preferred_element_type=jnp.float32) * scale, axis=-1)
        a = (a1 - lam * a2).astype(v.dtype)
        return jnp.einsum('bhqk,bhkd->bhqd', a, v)

    return _map_query_blocks(blk, (q1, q2))


def _short_conv(z, w, b):
    L = z.shape[1]
    pad = HY_SHORT // 2
    zp = jnp.pad(z, ((0, 0), (pad, HY_SHORT - 1 - pad), (0, 0)))
    y = b
    for j in range(HY_SHORT):
        y = y + zp[:, j:j + L] * w[j]
    return y


def _hyena_filter(L, fw1, fb1, freq1, fw2, fb2, freq2, fw3):
    f32 = jnp.float32
    t01 = jnp.linspace(0.0, 1.0, L, dtype=f32)[:, None]
    w = 2.0 * math.pi * jnp.arange(L, dtype=f32)[:, None] / L
    f = jnp.linspace(1e-4, HY_BANDS - 1, HY_BANDS, dtype=f32)[None, :]
    z = jnp.concatenate([t01, jnp.cos(f * w), -jnp.sin(f * w)], axis=-1)
    h = jnp.sin(freq1.astype(f32) * (z @ fw1.astype(f32) + fb1.astype(f32)))
    h = jnp.sin(freq2.astype(f32) * (h @ fw2.astype(f32) + fb2.astype(f32)))
    h = h @ fw3.astype(f32)
    max_decay = math.log(HY_DECAY_TARGET) / HY_FAST_DECAY_PCT
    min_decay = math.log(HY_DECAY_TARGET) / HY_SLOW_DECAY_PCT
    deltas = jnp.abs(jnp.linspace(min_decay, max_decay, HY_WIDTH, dtype=f32))
    window = jnp.exp(-t01 * deltas[None, :])
    h_fwd = h[:, :HY_WIDTH] * window
    h_bwd = h[:, HY_WIDTH:] * window
    k = jnp.concatenate([h_fwd, jnp.zeros((1, HY_WIDTH), f32), h_bwd[:0:-1]], axis=0)
    return k / jnp.sum(jnp.abs(k), axis=0, keepdims=True)


def _bidir_fftconv(u, k, dbias):
    L = u.shape[1]
    uf32 = u.astype(jnp.float32)
    uf = jnp.fft.rfft(uf32, n=2 * L, axis=1)
    kf = jnp.fft.rfft(k, n=2 * L, axis=0)
    y = jnp.fft.irfft(uf * kf[None], n=2 * L, axis=1)[:, :L]
    return (y + uf32 * dbias.astype(jnp.float32)).astype(u.dtype)


def _hyena(z, p):
    L = z.shape[1]
    zc = _short_conv(z, p['hy_conv_w'], p['hy_conv_b'])
    x0, x1, v = jnp.split(zc, 3, axis=-1)
    k = _hyena_filter(L, p['hy_fw1'], p['hy_fb1'], p['hy_freq1'], p['hy_fw2'], p['hy_fb2'], p['hy_freq2'], p['hy_fw3'])
    return _bidir_fftconv(v * x1, k, p['hy_dbias']) * x0


def _mla_kv(ckv_n, kr, p):
    B, L, _ = ckv_n.shape
    kv = (ckv_n @ p['w_ukv']).reshape(B, L, MLA_HEADS, MLA_NOPE + MLA_V).transpose(0, 2, 1, 3)
    k_nope, v = kv[..., :MLA_NOPE], kv[..., MLA_NOPE:]
    k_r = jnp.broadcast_to(kr[:, None], (B, MLA_HEADS, L, MLA_ROPE))
    k = _rms(jnp.concatenate([k_nope, k_r], axis=-1), p['k_norm'])
    return k, v


def _rope_tail(t):
    return jnp.concatenate([t[..., :MLA_NOPE], _axial_rope(t[..., MLA_NOPE:])], axis=-1)


def _even_layer(x, mod, p, cache=None):
    shift1, scale1, gate1, shift2, scale2, gate2 = jnp.split(mod, 6, axis=-1)
    B, L, _ = x.shape
    h = _modulate(_rms(x, p['norm1']), shift1, scale1)
    z = h @ p['w_in']
    z_hy, cq, ckv, kr = jnp.split(z, EVEN_SPLITS, axis=-1)
    y_hy = _hyena(z_hy, p)
    q = (_rms(cq, p['qa_norm']) @ p['w_uq']).reshape(B, L, MLA_HEADS, MLA_QK).transpose(0, 2, 1, 3)
    q = _rms(q, p['q_norm'])
    ckv_n = _rms(ckv, p['kva_norm'])
    k, v = _mla_kv(ckv_n, kr, p)
    if cache is None:
        new_cache = (ckv_n, kr)
    else:
        q = _rope_tail(q)
        k = _rope_tail(k)
        k_ctx, v_ctx = _mla_kv(cache[0], cache[1], p)
        k = jnp.concatenate([k_ctx, k], axis=2)
        v = jnp.concatenate([v_ctx, v], axis=2)
        new_cache = None
    o = _softmax_attend(q, k, v).transpose(0, 2, 1, 3).reshape(B, L, MLA_HEADS * MLA_V)
    x = x + gate1 * (jnp.concatenate([y_hy, o], axis=-1) @ p['w_out'])
    h2 = _modulate(_rms(x, p['norm2']), shift2, scale2)
    x = x + gate2 * _swiglu(h2, p['ffn_w_gate'], p['ffn_w_up'], p['ffn_w_down'])
    return x, new_cache


def _moe(h, p):
    logits = jnp.einsum('bld,de->ble', h, p['w_router'], preferred_element_type=jnp.float32)
    top_v, top_i = lax.top_k(logits, TOP_K)
    top_w = jax.nn.softmax(top_v, axis=-1)
    gates = jnp.sum(jax.nn.one_hot(top_i, N_EXPERTS, dtype=jnp.float32) * top_w[..., None], axis=-2).astype(h.dtype)
    y = jnp.zeros_like(h)
    for e in range(N_EXPERTS):
        y = y + gates[..., e:e + 1] * _swiglu(h, p['moe_w_gate'][e], p['moe_w_up'][e], p['moe_w_down'][e])
    return y


def _odd_layer(x, mod, p, lambda_init, cache=None):
    shift1, scale1, gate1, shift2, scale2, gate2 = jnp.split(mod, 6, axis=-1)
    B, L, _ = x.shape
    h = _modulate(_rms(x, p['norm1']), shift1, scale1)
    z = h @ p['w_qkv']
    qk = z[..., :DIFF_QK].reshape(B, L, 4, DIFF_HEADS, DIFF_DH).transpose(2, 0, 3, 1, 4)
    v = z[..., DIFF_QK:].reshape(B, L, DIFF_HEADS, 2 * DIFF_DH).transpose(0, 2, 1, 3)
    q = _rms(qk[:2], p['q_norm'])
    k = _rms(qk[2:], p['k_norm'])
    if cache is None:
        new_cache = (jnp.concatenate([k[0], k[1]], axis=-1), v)
    else:
        q = _axial_rope(q)
        k = _axial_rope(k)
        k_ctx, v_ctx = cache
        k_ctx = jnp.stack([k_ctx[..., :DIFF_DH], k_ctx[..., DIFF_DH:]])
        k = jnp.concatenate([k_ctx, k], axis=3)
        v = jnp.concatenate([v_ctx, v], axis=2)
        new_cache = None
    f32 = jnp.float32
    lam = (jnp.exp(jnp.sum(p['lam_q1'].astype(f32) * p['lam_k1'].astype(f32)))
           - jnp.exp(jnp.sum(p['lam_q2'].astype(f32) * p['lam_k2'].astype(f32))) + lambda_init)
    o = _diff_attend(q[0], q[1], k[0], k[1], v, lam)
    o = _rms(o, p['subln']) * (1.0 - lambda_init)
    o = o.transpose(0, 2, 1, 3).reshape(B, L, ODD_MIX)
    x = x + gate1 * (o @ p['w_out'])
    h2 = _modulate(_rms(x, p['norm2']), shift2, scale2)
    x = x + gate2 * _moe(h2, p)
    return x, new_cache


def setup_inputs(seed: int = 0) -> dict:
    key = jax.random.key(seed)
    ks = iter(jax.random.split(key, 64))
    f32 = jnp.float32

    def nrm(shape, scale):
        return jax.random.normal(next(ks), shape, f32) * scale

    def gain(n):
        return 1.0 + 0.02 * jax.random.normal(next(ks), (n,), f32)

    D = D_MODEL
    return {
        'x_prompt': nrm((BATCH, SEQ, D), 1.0),
        'x_sample': nrm((DEC_BATCH, DEC_SEQ, D), 1.0),
        'cache_l0_ckv': nrm((DEC_BATCH, PAST_LEN, MLA_KV_RANK), 1.0),
        'cache_l0_krope': nrm((DEC_BATCH, PAST_LEN, MLA_ROPE), 1.0),
        'cache_l1_k': nrm((DEC_BATCH, DIFF_HEADS, PAST_LEN, 2 * DIFF_DH), 1.0),
        'cache_l1_v': nrm((DEC_BATCH, DIFF_HEADS, PAST_LEN, 2 * DIFF_DH), 1.0),
        'c': nrm((DEC_BATCH, D), 1.0),
        'c_ctx': nrm((D,), 1.0),
        'l0_w_mod': nrm((D, 6 * D), D ** -0.5),
        'l0_b_mod': nrm((6 * D,), 0.02),
        'l0_norm1': gain(D),
        'l0_norm2': gain(D),
        'l0_w_in': nrm((D, EVEN_IN), D ** -0.5),
        'l0_hy_conv_w': nrm((HY_SHORT, 3 * HY_WIDTH), 0.5),
        'l0_hy_conv_b': nrm((3 * HY_WIDTH,), 0.02),
        'l0_hy_fw1': nrm((HY_EMB_DIM, HY_FILTER_HIDDEN), HY_EMB_DIM ** -0.5),
        'l0_hy_fb1': nrm((HY_FILTER_HIDDEN,), 0.02),
        'l0_hy_freq1': gain(HY_FILTER_HIDDEN),
        'l0_hy_fw2': nrm((HY_FILTER_HIDDEN, HY_FILTER_HIDDEN), HY_FILTER_HIDDEN ** -0.5),
        'l0_hy_fb2': nrm((HY_FILTER_HIDDEN,), 0.02),
        'l0_hy_freq2': gain(HY_FILTER_HIDDEN),
        'l0_hy_fw3': nrm((HY_FILTER_HIDDEN, 2 * HY_WIDTH), HY_FILTER_HIDDEN ** -0.5),
        'l0_hy_dbias': nrm((HY_WIDTH,), 1.0),
        'l0_mla_qa_norm': gain(MLA_Q_RANK),
        'l0_mla_w_uq': nrm((MLA_Q_RANK, MLA_HEADS * MLA_QK), MLA_Q_RANK ** -0.5),
        'l0_mla_kva_norm': gain(MLA_KV_RANK),
        'l0_mla_w_ukv': nrm((MLA_KV_RANK, MLA_HEADS * (MLA_NOPE + MLA_V)), MLA_KV_RANK ** -0.5),
        'l0_mla_q_norm': gain(MLA_QK),
        'l0_mla_k_norm': gain(MLA_QK),
        'l0_w_out': nrm((EVEN_MIX, D), EVEN_MIX ** -0.5),
        'l0_ffn_w_gate': nrm((D, D_FF), D ** -0.5),
        'l0_ffn_w_up': nrm((D, D_FF), D ** -0.5),
        'l0_ffn_w_down': nrm((D_FF, D), D_FF ** -0.5),
        'l1_w_mod': nrm((D, 6 * D), D ** -0.5),
        'l1_b_mod': nrm((6 * D,), 0.02),
        'l1_norm1': gain(D),
        'l1_norm2': gain(D),
        'l1_w_qkv': nrm((D, ODD_IN), D ** -0.5),
        'l1_q_norm': gain(DIFF_DH),
        'l1_k_norm': gain(DIFF_DH),
        'l1_lam_q1': nrm((DIFF_DH,), 0.1),
        'l1_lam_k1': nrm((DIFF_DH,), 0.1),
        'l1_lam_q2': nrm((DIFF_DH,), 0.1),
        'l1_lam_k2': nrm((DIFF_DH,), 0.1),
        'l1_subln': gain(2 * DIFF_DH),
        'l1_w_out': nrm((ODD_MIX, D), ODD_MIX ** -0.5),
        'l1_w_router': nrm((D, N_EXPERTS), D ** -0.5),
        'l1_moe_w_gate': nrm((N_EXPERTS, D, D_FF_EXPERT), D ** -0.5),
        'l1_moe_w_up': nrm((N_EXPERTS, D, D_FF_EXPERT), D ** -0.5),
        'l1_moe_w_down': nrm((N_EXPERTS, D_FF_EXPERT, D), D_FF_EXPERT ** -0.5),
    }


def reference(x_prompt, x_sample, cache_l0_ckv, cache_l0_krope, cache_l1_k, cache_l1_v, c, c_ctx,
              l0_w_mod, l0_b_mod, l0_norm1, l0_norm2, l0_w_in, l0_hy_conv_w, l0_hy_conv_b,
              l0_hy_fw1, l0_hy_fb1, l0_hy_freq1, l0_hy_fw2, l0_hy_fb2, l0_hy_freq2, l0_hy_fw3, l0_hy_dbias,
              l0_mla_qa_norm, l0_mla_w_uq, l0_mla_kva_norm, l0_mla_w_ukv, l0_mla_q_norm, l0_mla_k_norm,
              l0_w_out, l0_ffn_w_gate, l0_ffn_w_up, l0_ffn_w_down,
              l1_w_mod, l1_b_mod, l1_norm1, l1_norm2, l1_w_qkv, l1_q_norm, l1_k_norm,
              l1_lam_q1, l1_lam_k1, l1_lam_q2, l1_lam_k2, l1_subln, l1_w_out,
              l1_w_router, l1_moe_w_gate, l1_moe_w_up, l1_moe_w_down):
    even = {
        'w_mod': l0_w_mod, 'b_mod': l0_b_mod, 'norm1': l0_norm1, 'norm2': l0_norm2, 'w_in': l0_w_in,
        'hy_conv_w': l0_hy_conv_w, 'hy_conv_b': l0_hy_conv_b, 'hy_fw1': l0_hy_fw1, 'hy_fb1': l0_hy_fb1,
        'hy_freq1': l0_hy_freq1, 'hy_fw2': l0_hy_fw2, 'hy_fb2': l0_hy_fb2, 'hy_freq2': l0_hy_freq2,
        'hy_fw3': l0_hy_fw3, 'hy_dbias': l0_hy_dbias, 'qa_norm': l0_mla_qa_norm, 'w_uq': l0_mla_w_uq,
        'kva_norm': l0_mla_kva_norm, 'w_ukv': l0_mla_w_ukv, 'q_norm': l0_mla_q_norm, 'k_norm': l0_mla_k_norm,
        'w_out': l0_w_out, 'ffn_w_gate': l0_ffn_w_gate, 'ffn_w_up': l0_ffn_w_up, 'ffn_w_down': l0_ffn_w_down,
    }
    odd = {
        'w_mod': l1_w_mod, 'b_mod': l1_b_mod, 'norm1': l1_norm1, 'norm2': l1_norm2, 'w_qkv': l1_w_qkv,
        'q_norm': l1_q_norm, 'k_norm': l1_k_norm, 'lam_q1': l1_lam_q1, 'lam_k1': l1_lam_k1,
        'lam_q2': l1_lam_q2, 'lam_k2': l1_lam_k2, 'subln': l1_subln, 'w_out': l1_w_out,
        'w_router': l1_w_router, 'moe_w_gate': l1_moe_w_gate, 'moe_w_up': l1_moe_w_up, 'moe_w_down': l1_moe_w_down,
    }
    layer_params = (even, odd)
    caches_in = ((cache_l0_ckv, cache_l0_krope), (cache_l1_k, cache_l1_v))
    xc = x_prompt
    xl = x_sample
    new_state = []
    for layer in range(DEPTH):
        p = layer_params[layer]
        mod_ctx = _ada_mod(c_ctx[None, :], p['w_mod'], p['b_mod'])
        mod_lat = _ada_mod(c, p['w_mod'], p['b_mod'])
        if layer % 2 == 0:
            xc, st = _even_layer(xc, mod_ctx, p)
            xl, _ = _even_layer(xl, mod_lat, p, caches_in[layer])
        else:
            lambda_init = 0.8 - 0.6 * math.exp(-0.3 * layer)
            xc, st = _odd_layer(xc, mod_ctx, p, lambda_init)
            xl, _ = _odd_layer(xl, mod_lat, p, lambda_init, caches_in[layer])
        new_state.append(st)
    (new_l0_ckv, new_l0_krope), (new_l1_k, new_l1_v) = new_state
    return (xc, xl, new_l0_ckv, new_l0_krope, new_l1_k, new_l1_v)
```

```python
import functools
import math

import numpy as np
import jax
import jax.numpy as jnp
from jax import lax
from jax.experimental import pallas as pl
from jax.experimental.pallas import tpu as pltpu

F32 = jnp.float32
BF16 = jnp.bfloat16

VMEM_LIMIT_BYTES = 56 * 1024 * 1024
LANES = 128
SUBLANES = 8

GRID_W = 64
ROPE_BASE = 10000.0
NORM_EPS = 1e-6
HY_WIDTH = 512
HY_BANDS = 16
HY_FAST_DECAY_PCT = 0.3
HY_SLOW_DECAY_PCT = 1.5
HY_DECAY_TARGET = 1e-2
MLA_HEADS = 8
MLA_NOPE = 64
MLA_ROPE = 32
MLA_QK = MLA_NOPE + MLA_ROPE
MLA_V = 64
MLA_Q_RANK = 768
MLA_KV_RANK = 256
DIFF_HEADS = 8
DIFF_DH = 64
N_EXPERTS = 8


def _params(*sem):
    return pltpu.CompilerParams(dimension_semantics=sem, vmem_limit_bytes=VMEM_LIMIT_BYTES)


def _dot(a, b):
    return jnp.dot(a, b, preferred_element_type=F32)


def _dot_nt(a, b):
    return lax.dot_general(a, b, (((1,), (1,)), ((), ())), preferred_element_type=F32)


def _split_bf16(a):
    hi = a.astype(BF16)
    lo = (a - hi.astype(F32)).astype(BF16)
    return hi, lo


def _dot_f32(a, b):
    ah, al = _split_bf16(a)
    bh, bl = _split_bf16(b)
    return _dot(ah, bh) + (_dot(al, bh) + _dot(ah, bl))


def _rms(x, g, n=None):
    n = x.shape[-1] if n is None else n
    ms = jnp.sum(x * x, axis=-1, keepdims=True) * (1.0 / n)
    return x * lax.rsqrt(ms + NORM_EPS) * g


def _norm_mod(x, g, shift, scale):
    return _rms(x, g) * (1.0 + scale) + shift


def _silu(x):
    return x / (1.0 + jnp.exp(-x))


def _adaln_body(c_ref, w_ref, b_ref, o_ref):
    o_ref[...] = _dot_f32(_silu(c_ref[...]), w_ref[...]) + b_ref[...]


def _adaln(cond8, w_mod, b_mod):
    d, n = w_mod.shape
    tn = n // 4
    out = pl.pallas_call(
        _adaln_body,
        grid=(n // tn,),
        in_specs=[pl.BlockSpec((SUBLANES, d), lambda j: (0, 0)),
                  pl.BlockSpec((d, tn), lambda j: (0, j)),
                  pl.BlockSpec((1, tn), lambda j: (0, j))],
        out_specs=pl.BlockSpec((SUBLANES, tn), lambda j: (0, j)),
        out_shape=jax.ShapeDtypeStruct((SUBLANES, n), F32),
        compiler_params=_params("arbitrary"),
        name="adaln",
    )(cond8, w_mod, b_mod.reshape(1, n))
    return out.reshape(SUBLANES, 6, d)


def _lin_in_body(x_ref, mod_ref, g_ref, w_ref, o_ref):
    h = _norm_mod(x_ref[...], g_ref[...], mod_ref[0:1, :], mod_ref[1:2, :])
    o_ref[...] = _dot(h.astype(BF16), w_ref[...])


def _lin_in(x, mod, g, w, *, tm, seg_len):
    t, d = x.shape
    n = w.shape[1]
    return pl.pallas_call(
        _lin_in_body,
        grid=(t // tm,),
        in_specs=[pl.BlockSpec((tm, d), lambda i: (i, 0)),
                  pl.BlockSpec((None, 6, d), lambda i: (i * tm // seg_len, 0, 0)),
                  pl.BlockSpec((1, d), lambda i: (0, 0)),
                  pl.BlockSpec((d, n), lambda i: (0, 0))],
        out_specs=pl.BlockSpec((tm, n), lambda i: (i, 0)),
        out_shape=jax.ShapeDtypeStruct((t, n), F32),
        compiler_params=_params("parallel"),
        name="lin_in",
    )(x, mod, g.reshape(1, d), w)


def _hy_pre_body(z_ref, zp_ref, zn_ref, w_ref, b_ref, u_ref, x0_ref, *, tm, lat_tiles, tiles_per_seq):
    i = pl.program_id(0)
    z = z_ref[...]
    in_lat = i < lat_tiles
    has_prev = jnp.logical_and(in_lat, i % tiles_per_seq != 0)
    has_next = jnp.logical_and(in_lat, i % tiles_per_seq != tiles_per_seq - 1)
    prev_row = jnp.where(has_prev, zp_ref[SUBLANES - 1:SUBLANES, :], 0.0)
    next_row = jnp.where(has_next, zn_ref[0:1, :], 0.0)
    rows = lax.broadcasted_iota(jnp.int32, z.shape, 0)
    z_m = jnp.where(rows == 0, prev_row, pltpu.roll(z, 1, 0))
    z_p = jnp.where(rows == tm - 1, next_row, pltpu.roll(z, tm - 1, 0))
    zc = b_ref[...] + z_m * w_ref[0:1, :] + z * w_ref[1:2, :] + z_p * w_ref[2:3, :]
    c = HY_WIDTH
    x0_ref[...] = zc[:, :c]
    u_ref[...] = zc[:, 2 * c:] * zc[:, c:2 * c]


def _hy_pre(z, conv_w, conv_b, *, t_lat, lat_seq, ctx_seq):
    t = z.shape[0]
    tm = ctx_seq
    c3 = 3 * HY_WIDTH
    nb8 = t // SUBLANES
    body = functools.partial(_hy_pre_body, tm=tm, lat_tiles=t_lat // tm, tiles_per_seq=lat_seq // tm)
    return pl.pallas_call(
        body,
        grid=(t // tm,),
        in_specs=[pl.BlockSpec((tm, c3), lambda i: (i, 0)),
                  pl.BlockSpec((SUBLANES, c3), lambda i: (jnp.maximum(i * (tm // SUBLANES) - 1, 0), 0)),
                  pl.BlockSpec((SUBLANES, c3), lambda i: (jnp.minimum((i + 1) * (tm // SUBLANES), nb8 - 1), 0)),
                  pl.BlockSpec((3, c3), lambda i: (0, 0)),
                  pl.BlockSpec((1, c3), lambda i: (0, 0))],
        out_specs=[pl.BlockSpec((tm, HY_WIDTH), lambda i: (i, 0)),
                   pl.BlockSpec((tm, HY_WIDTH), lambda i: (i, 0))],
        out_shape=[jax.ShapeDtypeStruct((t, HY_WIDTH), F32),
                   jax.ShapeDtypeStruct((t, HY_WIDTH), F32)],
        compiler_params=_params("parallel"),
        name="hy_pre",
    )(z, z, z, conv_w, conv_b.reshape(1, c3))


def _filter_embedding(seq):
    t01 = np.linspace(0.0, 1.0, seq)[:, None]
    w = 2.0 * math.pi * np.arange(seq)[:, None] / seq
    f = np.linspace(1e-4, HY_BANDS - 1, HY_BANDS)[None, :]
    z = np.concatenate([t01, np.cos(f * w), -np.sin(f * w)], axis=-1)
    z_rev = np.concatenate([z[:1], z[:0:-1]], axis=0)
    zz = np.concatenate([z, z_rev], axis=0)
    out = np.zeros((2 * seq, LANES), np.float32)
    out[:, :zz.shape[1]] = zz
    return out


def _filter_body(zz_ref, dl_ref, w1_ref, b1_ref, f1_ref, w2_ref, b2_ref, f2_ref, w3_ref,
                 k_ref, s_ref, *, tm, lat_tiles, ctx_tiles):
    i = pl.program_id(0)
    zz = zz_ref[...]
    h = jnp.sin(f1_ref[...] * (_dot_f32(zz, w1_ref[...]) + b1_ref[...]))
    h = jnp.sin(f2_ref[...] * (_dot_f32(h, w2_ref[...]) + b2_ref[...]))
    h = _dot_f32(h, w3_ref[...])
    is_bwd = jnp.logical_or(jnp.logical_and(i >= lat_tiles // 2, i < lat_tiles),
                            i >= lat_tiles + ctx_tiles // 2)
    first_bwd = jnp.logical_or(i == lat_tiles // 2, i == lat_tiles + ctx_tiles // 2)
    window = jnp.exp(-zz[:, 0:1] * dl_ref[...])
    k = jnp.where(is_bwd, h[:, HY_WIDTH:], h[:, :HY_WIDTH]) * window
    rows = lax.broadcasted_iota(jnp.int32, k.shape, 0)
    k = jnp.where(jnp.logical_and(first_bwd, rows == 0), 0.0, k)
    k_ref[...] = k
    s = jnp.sum(jnp.abs(k), axis=0, keepdims=True)

    @pl.when(i == 0)
    def _():
        s_ref[...] = jnp.zeros_like(s_ref)

    @pl.when(i < lat_tiles)
    def _():
        s_ref[0:1, :] += s

    @pl.when(i >= lat_tiles)
    def _():
        s_ref[1:2, :] += s


def _hyena_filters(p, *, lat_seq, ctx_seq):
    tm = ctx_seq
    zz = jnp.asarray(np.concatenate([_filter_embedding(lat_seq), _filter_embedding(ctx_seq)], axis=0))
    rows = zz.shape[0]
    max_decay = math.log(HY_DECAY_TARGET) / HY_FAST_DECAY_PCT
    min_decay = math.log(HY_DECAY_TARGET) / HY_SLOW_DECAY_PCT
    deltas = jnp.asarray(np.abs(np.linspace(min_decay, max_decay, HY_WIDTH))[None, :].astype(np.float32))
    emb, hid = p['hy_fw1'].shape

    def pad2(a, r, c):
        return jnp.zeros((r, c), F32).at[:a.shape[0], :a.shape[1]].set(a)

    w1 = pad2(p['hy_fw1'], LANES, LANES)
    b1 = pad2(p['hy_fb1'][None, :], 1, LANES)
    f1 = pad2(p['hy_freq1'][None, :], 1, LANES)
    w2 = pad2(p['hy_fw2'], LANES, LANES)
    b2 = pad2(p['hy_fb2'][None, :], 1, LANES)
    f2 = pad2(p['hy_freq2'][None, :], 1, LANES)
    w3 = pad2(p['hy_fw3'], LANES, 2 * HY_WIDTH)
    body = functools.partial(_filter_body, tm=tm, lat_tiles=2 * lat_seq // tm, ctx_tiles=2 * ctx_seq // tm)
    full = lambda i: (0, 0)
    return pl.pallas_call(
        body,
        grid=(rows // tm,),
        in_specs=[pl.BlockSpec((tm, LANES), lambda i: (i, 0)),
                  pl.BlockSpec((1, HY_WIDTH), full),
                  pl.BlockSpec((LANES, LANES), full), pl.BlockSpec((1, LANES), full), pl.BlockSpec((1, LANES), full),
                  pl.BlockSpec((LANES, LANES), full), pl.BlockSpec((1, LANES), full), pl.BlockSpec((1, LANES), full),
                  pl.BlockSpec((LANES, 2 * HY_WIDTH), full)],
        out_specs=[pl.BlockSpec((tm, HY_WIDTH), lambda i: (i, 0)),
                   pl.BlockSpec((SUBLANES, HY_WIDTH), full)],
        out_shape=[jax.ShapeDtypeStruct((rows, HY_WIDTH), F32),
                   jax.ShapeDtypeStruct((SUBLANES, HY_WIDTH), F32)],
        compiler_params=_params("arbitrary"),
        name="hy_filter",
    )(zz, deltas, w1, b1, f1, w2, b2, f2, w3)


def _stack_complex(z):
    return np.block([[z.real, -z.imag], [z.imag, z.real]])


def _dft_consts_two_level(seq, n1, n2):
    n = 2 * seq
    assert n1 * n2 == n
    a1 = np.arange(n1)
    f1_full = np.exp(-2j * np.pi * np.outer(a1, a1) / n1)
    f1_u = np.concatenate([f1_full.real, f1_full.imag], axis=0)[:, :n1 // 2]
    f1_k = np.concatenate([f1_full.real, f1_full.imag], axis=0)
    a2 = np.arange(n2)
    f = a1[:, None, None] + n1 * a2[None, :, None]
    z = np.exp(-2j * np.pi * (f * a2[None, None, :]) / n)
    mf = np.stack([_stack_complex(z[i]) for i in range(n1)])
    mi = np.stack([_stack_complex(np.conj(z[i]).T) for i in range(n1)])
    g = np.exp(2j * np.pi * np.outer(a1[:n1 // 2], a1) / n1) / n
    gc, gs = g.real, -g.imag
    as32 = lambda a: jnp.asarray(a.astype(np.float32))
    return as32(f1_u), as32(f1_k), as32(mf), as32(mi), as32(gc), as32(gs)


def _dft_consts_one_level(seq):
    n = 2 * seq
    a = np.arange(n)
    z = np.exp(-2j * np.pi * np.outer(a, a) / n)
    mf = np.concatenate([z.real, z.imag], axis=0)
    zi = np.exp(2j * np.pi * np.outer(a[:seq], a) / n) / n
    mi = np.concatenate([zi.real, -zi.imag], axis=1)
    as32 = lambda a: jnp.asarray(a.astype(np.float32))
    return as32(mf), as32(mi)


def _lmat_body(f_ref, x_ref, sc_ref, or_ref, oi_ref):
    x = (x_ref[...] * (1.0 / sc_ref[...])).astype(BF16)
    o = _dot(f_ref[...], x)
    h = o.shape[0] // 2
    or_ref[...] = o[:h]
    oi_ref[...] = o[h:]


def _lmat(f, x, scale_row, *, tn):
    g, k, cols = x.shape
    m2 = f.shape[0]
    m = m2 // 2
    return pl.pallas_call(
        _lmat_body,
        grid=(g, cols // tn),
        in_specs=[pl.BlockSpec((m2, k), lambda b, j: (0, 0)),
                  pl.BlockSpec((None, k, tn), lambda b, j: (b, 0, j)),
                  pl.BlockSpec((1, tn), lambda b, j: (0, 0))],
        out_specs=[pl.BlockSpec((None, m, tn), lambda b, j: (b, 0, j)),
                   pl.BlockSpec((None, m, tn), lambda b, j: (b, 0, j))],
        out_shape=[jax.ShapeDtypeStruct((g, m, cols), F32),
                   jax.ShapeDtypeStruct((g, m, cols), F32)],
        compiler_params=_params("parallel", "parallel"),
        name="hy_dft1",
    )(f.astype(BF16), x, scale_row)


def _spec_fwd_body(mf_ref, ar_ref, ai_ref, kr_ref, ki_ref):
    a = jnp.concatenate([ar_ref[...], ai_ref[...]], axis=0).astype(BF16)
    x = _dot(mf_ref[...], a)
    h = x.shape[0] // 2
    kr_ref[...] = x[:h]
    ki_ref[...] = x[h:]


def _spec_fwd(mf, ar, ai):
    n1, n2, c = ar.shape
    spec = pl.BlockSpec((None, n2, c), lambda i: (i, 0, 0))
    return pl.pallas_call(
        _spec_fwd_body,
        grid=(n1,),
        in_specs=[pl.BlockSpec((None, 2 * n2, 2 * n2), lambda i: (i, 0, 0)), spec, spec],
        out_specs=[spec, spec],
        out_shape=[jax.ShapeDtypeStruct((n1, n2, c), F32)] * 2,
        compiler_params=_params("parallel"),
        name="hy_spec_filter",
    )(mf, ar, ai)


def _spec_mul_body(mf_ref, mi_ref, kr_ref, ki_ref, ar_ref, ai_ref, br_ref, bi_ref):
    a = jnp.concatenate([ar_ref[...], ai_ref[...]], axis=0).astype(BF16)
    x = _dot(mf_ref[...], a)
    h = x.shape[0] // 2
    xr, xi = x[:h], x[h:]
    kr, ki = kr_ref[...], ki_ref[...]
    y = jnp.concatenate([xr * kr - xi * ki, xr * ki + xi * kr], axis=0).astype(BF16)
    b = _dot(mi_ref[...], y)
    br_ref[...] = b[:h]
    bi_ref[...] = b[h:]


def _spec_mul(mf, mi, kr, ki, ar, ai):
    nb, n1, n2, c = ar.shape
    mspec = pl.BlockSpec((None, 2 * n2, 2 * n2), lambda i, b: (i, 0, 0))
    kspec = pl.BlockSpec((None, n2, c), lambda i, b: (i, 0, 0))
    aspec = pl.BlockSpec((None, None, n2, c), lambda i, b: (b, i, 0, 0))
    return pl.pallas_call(
        _spec_mul_body,
        grid=(n1, nb),
        in_specs=[mspec, mspec, kspec, kspec, aspec, aspec],
        out_specs=[aspec, aspec],
        out_shape=[jax.ShapeDtypeStruct((nb, n1, n2, c), F32)] * 2,
        compiler_params=_params("parallel", "arbitrary"),
        name="hy_spec_mul",
    )(mf, mi, kr, ki, ar, ai)


def _idft1_body(gc_ref, gs_ref, br_ref, bi_ref, u_ref, x0_ref, db_ref, o_ref):
    y = _dot(gc_ref[...], br_ref[...].astype(BF16)) + _dot(gs_ref[...], bi_ref[...].astype(BF16))
    u = u_ref[...]
    o_ref[...] = (y + u * db_ref[...]) * x0_ref[...]


def _idft1(gc, gs, br, bi, u, x0, db_row, *, tn):
    nb, n1, cols = br.shape
    m = gc.shape[0]
    gspec = pl.BlockSpec((m, n1), lambda b, j: (0, 0))
    bspec = pl.BlockSpec((None, n1, tn), lambda b, j: (b, 0, j))
    uspec = pl.BlockSpec((None, m, tn), lambda b, j: (b, 0, j))
    return pl.pallas_call(
        _idft1_body,
        grid=(nb, cols // tn),
        in_specs=[gspec, gspec, bspec, bspec, uspec, uspec, pl.BlockSpec((1, tn), lambda b, j: (0, 0))],
        out_specs=uspec,
        out_shape=jax.ShapeDtypeStruct((nb, m, cols), F32),
        compiler_params=_params("parallel", "parallel"),
        name="hy_idft1",
    )(gc.astype(BF16), gs.astype(BF16), br, bi, u, x0, db_row)


def _ctx_filter_body(mf_ref, k_ref, sc_ref, kf_ref):
    kf_ref[...] = _dot(mf_ref[...], (k_ref[...] * (1.0 / sc_ref[...])).astype(BF16))


def _ctx_conv_body(mf_ref, mi_ref, kf_ref, u_ref, x0_ref, db_ref, o_ref):
    u = u_ref[...]
    x = _dot(mf_ref[...], u.astype(BF16))
    h = x.shape[0] // 2
    xr, xi = x[:h], x[h:]
    kr, ki = kf_ref[:h, :], kf_ref[h:, :]
    y = jnp.concatenate([xr * kr - xi * ki, xr * ki + xi * kr], axis=0).astype(BF16)
    o_ref[...] = (_dot(mi_ref[...], y) + u * db_ref[...]) * x0_ref[...]


def _hyena_ctx(k_raw, k_norm1, u, x0, dbias, *, seq):
    mf, mi = _dft_consts_one_level(seq)
    n = 2 * seq
    c = u.shape[1]
    nseq = u.shape[0] // seq
    kf = pl.pallas_call(
        _ctx_filter_body,
        out_shape=jax.ShapeDtypeStruct((2 * n, c), F32),
        compiler_params=_params(),
        name="hy_ctx_filter",
    )(mf.astype(BF16), k_raw, k_norm1)
    full = lambda s: (0, 0)
    return pl.pallas_call(
        _ctx_conv_body,
        grid=(nseq,),
        in_specs=[pl.BlockSpec((2 * n, seq), full),
                  pl.BlockSpec((seq, 2 * n), full),
                  pl.BlockSpec((2 * n, c), full),
                  pl.BlockSpec((seq, c), lambda s: (s, 0)),
                  pl.BlockSpec((seq, c), lambda s: (s, 0)),
                  pl.BlockSpec((1, c), full)],
        out_specs=pl.BlockSpec((seq, c), lambda s: (s, 0)),
        out_shape=jax.ShapeDtypeStruct(u.shape, F32),
        compiler_params=_params("parallel"),
        name="hy_ctx_conv",
    )(mf[:, :seq].astype(BF16), mi.astype(BF16), kf, u, x0, dbias)


def _hyena_lat(k_raw, k_norm1, u, x0, dbias, *, seq, nb):
    c = u.shape[1]
    n1, n2 = 64, 2 * seq // 64
    f1_u, f1_k, mf, mi, gc, gs = _dft_consts_two_level(seq, n1, n2)
    mf = mf.astype(BF16)
    mi = mi.astype(BF16)
    cols = n2 * c
    tn = 4096
    rep = tn // c
    ones_row = jnp.ones((1, tn), F32)
    akr, aki = _lmat(f1_k, k_raw.reshape(1, n1, cols), jnp.tile(k_norm1, (1, rep)), tn=tn)
    kr, ki = _spec_fwd(mf, akr.reshape(n1, n2, c), aki.reshape(n1, n2, c))
    ar, ai = _lmat(f1_u, u.reshape(nb, n1 // 2, cols), ones_row, tn=tn)
    br, bi = _spec_mul(mf, mi, kr, ki, ar.reshape(nb, n1, n2, c), ai.reshape(nb, n1, n2, c))
    y = _idft1(gc, gs, br.reshape(nb, n1, cols), bi.reshape(nb, n1, cols),
               u.reshape(nb, n1 // 2, cols), x0.reshape(nb, n1 // 2, cols),
               jnp.tile(dbias, (1, rep)), tn=tn)
    return y.reshape(nb * seq, c)


def _rope_tables(seq, rope_dims, lane_offsets, pad_rows):
    rows = seq // GRID_W
    rr, cc = np.meshgrid(np.arange(rows), np.arange(GRID_W), indexing='ij')
    pos = (rr.reshape(-1).astype(np.float64), cc.reshape(-1).astype(np.float64))
    half = rope_dims // 2
    q = half // 2
    inv_freq = ROPE_BASE ** (-np.arange(0, half, 2, dtype=np.float64) / half)
    cos_t = np.ones((seq + pad_rows, LANES), np.float64)
    sin_a = np.zeros((seq + pad_rows, LANES), np.float64)
    sin_b = np.zeros((seq + pad_rows, LANES), np.float64)
    for off in lane_offsets:
        for axis in range(2):
            ang = pos[axis][:, None] * inv_freq[None, :]
            base = off + axis * half
            cos_t[:seq, base:base + q] = np.cos(ang)
            cos_t[:seq, base + q:base + half] = np.cos(ang)
            sin_b[:seq, base:base + q] = -np.sin(ang)
            sin_a[:seq, base + q:base + half] = np.sin(ang)
    as32 = lambda a: jnp.asarray(a.astype(np.float32))
    return as32(cos_t), as32(sin_a), as32(sin_b)


def _rope(x, cos_t, sin_a, sin_b, shift):
    return x * cos_t + pltpu.roll(x, shift, 1) * sin_a + pltpu.roll(x, LANES - shift, 1) * sin_b


def _mla_q_body(cq_ref, ckv_ref, qa_ref, kva_ref, wuq_ref, qn_ref, cos_ref, sa_ref, sb_ref,
                q_ref, ckvn_ref, *, scale):
    ckvn_ref[...] = _rms(ckv_ref[...], kva_ref[...])
    cqn = _rms(cq_ref[...], qa_ref[...])
    q = _dot(cqn.astype(BF16), wuq_ref[...])
    cos_t, sin_a, sin_b = cos_ref[...], sa_ref[...], sb_ref[...]
    g = qn_ref[...]
    for h in range(MLA_HEADS):
        qh = _rms(q[:, h * LANES:(h + 1) * LANES], g, MLA_QK)
        qh = _rope(qh, cos_t, sin_a, sin_b, MLA_ROPE // 4)
        q_ref[:, h * LANES:(h + 1) * LANES] = (qh * scale).astype(BF16)


def _pad_heads(w, heads, width):
    k = w.shape[0]
    w3 = w.reshape(k, heads, width)
    return jnp.zeros((k, heads, LANES), w.dtype).at[:, :, :width].set(w3).reshape(k, heads * LANES)


def _mla_q(z, p, tables, *, tm, t_lat, lat_seq):
    t = z.shape[0]
    cos_t, sin_a, sin_b = tables
    wuq = _pad_heads(p['w_uq'], MLA_HEADS, MLA_QK).astype(BF16)
    qn = jnp.zeros((1, LANES), F32).at[0, :MLA_QK].set(p['q_norm'])
    pos_blocks = lat_seq // tm
    lat_tiles = t_lat // tm
    tspec = pl.BlockSpec((tm, LANES), lambda i: (jnp.where(i < lat_tiles, i % pos_blocks, pos_blocks), 0))
    full = lambda i: (0, 0)
    cq_blk = (3 * HY_WIDTH) // MLA_Q_RANK
    ckv_blk = (3 * HY_WIDTH + MLA_Q_RANK) // MLA_KV_RANK
    return pl.pallas_call(
        functools.partial(_mla_q_body, scale=MLA_QK ** -0.5),
        grid=(t // tm,),
        in_specs=[pl.BlockSpec((tm, MLA_Q_RANK), lambda i: (i, cq_blk)),
                  pl.BlockSpec((tm, MLA_KV_RANK), lambda i: (i, ckv_blk)),
                  pl.BlockSpec((1, MLA_Q_RANK), full),
                  pl.BlockSpec((1, MLA_KV_RANK), full),
                  pl.BlockSpec((MLA_Q_RANK, MLA_HEADS * LANES), full),
                  pl.BlockSpec((1, LANES), full),
                  tspec, tspec, tspec],
        out_specs=[pl.BlockSpec((tm, MLA_HEADS * LANES), lambda i: (i, 0)),
                   pl.BlockSpec((tm, MLA_KV_RANK), lambda i: (i, 0))],
        out_shape=[jax.ShapeDtypeStruct((t, MLA_HEADS * LANES), BF16),
                   jax.ShapeDtypeStruct((t, MLA_KV_RANK), F32)],
        compiler_params=_params("parallel"),
        name="mla_q",
    )(z, z, p['qa_norm'].reshape(1, -1), p['kva_norm'].reshape(1, -1), wuq, qn, cos_t, sin_a, sin_b)


def _mla_kv_body(ckvn_ref, kr_ref, wk_ref, wv_ref, kn_ref, cos_ref, sa_ref, sb_ref, k_ref, v_ref):
    c = ckvn_ref[...].astype(BF16)
    k = _dot(c, wk_ref[...])
    v_ref[...] = _dot(c, wv_ref[...]).astype(BF16)
    kr = pltpu.roll(kr_ref[...], MLA_NOPE, 1)
    cos_t, sin_a, sin_b = cos_ref[...], sa_ref[...], sb_ref[...]
    g = kn_ref[...]
    for h in range(MLA_HEADS):
        kh = _rms(k[:, h * LANES:(h + 1) * LANES] + kr, g, MLA_QK)
        kh = _rope(kh, cos_t, sin_a, sin_b, MLA_ROPE // 4)
        k_ref[:, h * LANES:(h + 1) * LANES] = kh.astype(BF16)


def _mla_kv(ckvn_rows, kr_rows, p, tables, *, tm, nb, past, lat_seq):
    r = ckvn_rows.shape[0]
    cos_t, sin_a, sin_b = tables
    w = p['w_ukv'].reshape(MLA_KV_RANK, MLA_HEADS, MLA_NOPE + MLA_V)
    wk = _pad_heads(w[:, :, :MLA_NOPE].reshape(MLA_KV_RANK, -1), MLA_HEADS, MLA_NOPE).astype(BF16)
    wv = w[:, :, MLA_NOPE:].reshape(MLA_KV_RANK, MLA_HEADS * MLA_V).astype(BF16)
    kn = jnp.zeros((1, LANES), F32).at[0, :MLA_QK].set(p['k_norm'])
    per_b = (past + lat_seq) // tm
    past_tiles = past // tm
    pos_blocks = lat_seq // tm
    lat_tiles = nb * per_b

    def tmap(i):
        j = i % per_b
        is_pos = jnp.logical_and(i < lat_tiles, j >= past_tiles)
        return (jnp.where(is_pos, j - past_tiles, pos_blocks), 0)

    tspec = pl.BlockSpec((tm, LANES), tmap)
    full = lambda i: (0, 0)
    return pl.pallas_call(
        _mla_kv_body,
        grid=(r // tm,),
        in_specs=[pl.BlockSpec((tm, MLA_KV_RANK), lambda i: (i, 0)),
                  pl.BlockSpec((tm, LANES), lambda i: (i, 0)),
                  pl.BlockSpec((MLA_KV_RANK, MLA_HEADS * LANES), full),
                  pl.BlockSpec((MLA_KV_RANK, MLA_HEADS * MLA_V), full),
                  pl.BlockSpec((1, LANES), full),
                  tspec, tspec, tspec],
        out_specs=[pl.BlockSpec((tm, MLA_HEADS * LANES), lambda i: (i, 0)),
                   pl.BlockSpec((tm, MLA_HEADS * MLA_V), lambda i: (i, 0))],
        out_shape=[jax.ShapeDtypeStruct((r, MLA_HEADS * LANES), BF16),
                   jax.ShapeDtypeStruct((r, MLA_HEADS * MLA_V), BF16)],
        compiler_params=_params("parallel"),
        name="mla_kv",
    )(ckvn_rows, kr_rows, wk, wv, kn, cos_t, sin_a, sin_b)


def _softmax_pv(q, ks, vs):
    ss = [_dot_nt(q, k) for k in ks]
    m = functools.reduce(jnp.maximum, [jnp.max(s, axis=-1, keepdims=True) for s in ss])
    ps = [jnp.exp(s - m) for s in ss]
    l = functools.reduce(lambda a, b: a + b, [jnp.sum(p, axis=-1, keepdims=True) for p in ps])
    r = functools.reduce(lambda a, b: a + b, [_dot(p.astype(BF16), v) for p, v in zip(ps, vs)])
    return r, l


def _mla_attn_body(*refs, n_seg):
    q_ref, o_ref = refs[0], refs[-1]
    k_refs = refs[1:1 + n_seg]
    v_refs = refs[1 + n_seg:1 + 2 * n_seg]
    vs = [v[...] for v in v_refs]
    outs = []
    for hh in range(2):
        sl = slice(hh * LANES, (hh + 1) * LANES)
        r, l = _softmax_pv(q_ref[:, sl], [k[:, sl] for k in k_refs], vs)
        outs.append(r / l)
    lane = lax.broadcasted_iota(jnp.int32, outs[0].shape, 1)
    o_ref[...] = jnp.where(lane < MLA_V, outs[0], outs[1]).astype(BF16)


def _mla_attn(q, k, v, *, tq, n_seq, seq_q, seq_k, q_row0, k_row0):
    hp = MLA_HEADS // 2
    nq = seq_q // tq
    qb0, kb0 = q_row0 // tq, k_row0 // seq_k
    return pl.pallas_call(
        functools.partial(_mla_attn_body, n_seg=1),
        grid=(n_seq, hp, nq),
        in_specs=[pl.BlockSpec((tq, 2 * LANES), lambda s, h, i: (qb0 + s * nq + i, h)),
                  pl.BlockSpec((seq_k, 2 * LANES), lambda s, h, i: (kb0 + s, h)),
                  pl.BlockSpec((seq_k, 2 * MLA_V), lambda s, h, i: (kb0 + s, h))],
        out_specs=pl.BlockSpec((tq, 2 * MLA_V), lambda s, h, i: (s * nq + i, h)),
        out_shape=jax.ShapeDtypeStruct((n_seq * seq_q, MLA_HEADS * MLA_V), BF16),
        compiler_params=_params("parallel", "parallel", "arbitrary"),
        name="mla_attn",
    )(q, k, v)


def _diff_attn_body(*refs, n_seg, lambda_init):
    q_ref, lam_ref, sub_ref, o_ref = refs[0], refs[1], refs[2], refs[-1]
    k_refs = refs[3:3 + n_seg]
    v_refs = refs[3 + n_seg:3 + 2 * n_seg]
    lp = lam_ref[...]
    lam = (jnp.exp(jnp.sum(lp[0:1] * lp[1:2], axis=-1, keepdims=True))
           - jnp.exp(jnp.sum(lp[2:3] * lp[3:4], axis=-1, keepdims=True)) + lambda_init)
    q = q_ref[...].astype(F32)
    ks = [k[...] for k in k_refs]
    vs = [v[...] for v in v_refs]
    lane = lax.broadcasted_iota(jnp.int32, q.shape, 1)
    q1 = jnp.where(lane < DIFF_DH, q, 0.0).astype(BF16)
    q2 = jnp.where(lane < DIFF_DH, 0.0, q).astype(BF16)
    r1, l1 = _softmax_pv(q1, ks, vs)
    r2, l2 = _softmax_pv(q2, ks, vs)
    o = r1 / l1 - (lam / l2) * r2
    o_ref[...] = (_rms(o, sub_ref[...]) * (1.0 - lambda_init)).astype(BF16)


def _diff_attn(q, k_new, v_new, k_cache, v_cache, lam_p, subln, *, tq, n_seq, seq_q, q_row0, lambda_init):
    nq = seq_q // tq
    qb0 = q_row0 // tq
    sb0 = q_row0 // seq_q
    d = 2 * DIFF_DH
    new_spec = pl.BlockSpec((seq_q, d), lambda s, h, i: (sb0 + s, h))
    if k_cache is None:
        n_seg, k_args, v_args, k_specs, v_specs = 1, [k_new], [v_new], [new_spec], [new_spec]
    else:
        past = k_cache.shape[2]
        c_spec = pl.BlockSpec((None, None, past, d), lambda s, h, i: (s, h, 0, 0))
        n_seg, k_args, v_args = 2, [k_cache, k_new], [v_cache, v_new]
        k_specs, v_specs = [c_spec, new_spec], [c_spec, new_spec]
    return pl.pallas_call(
        functools.partial(_diff_attn_body, n_seg=n_seg, lambda_init=lambda_init),
        grid=(n_seq, DIFF_HEADS, nq),
        in_specs=[pl.BlockSpec((tq, d), lambda s, h, i: (qb0 + s * nq + i, h)),
                  pl.BlockSpec((4, DIFF_DH), lambda s, h, i: (0, 0)),
                  pl.BlockSpec((1, d), lambda s, h, i: (0, 0))] + k_specs + v_specs,
        out_specs=pl.BlockSpec((tq, d), lambda s, h, i: (s * nq + i, h)),
        out_shape=jax.ShapeDtypeStruct((n_seq * seq_q, DIFF_HEADS * d), BF16),
        compiler_params=_params("parallel", "parallel", "arbitrary"),
        name="diff_attn",
    )(q, lam_p, subln.reshape(1, d), *k_args, *v_args)


def _proj_res_body(*refs, n_in, gate_row):
    x_ref, mod_ref, o_ref = refs[0], refs[1], refs[-1]
    acc = None
    for j in range(n_in):
        a_ref, w_ref = refs[2 + 2 * j], refs[3 + 2 * j]
        d = _dot(a_ref[...].astype(BF16), w_ref[...])
        acc = d if acc is None else acc + d
    o_ref[...] = x_ref[...] + mod_ref[gate_row:gate_row + 1, :] * acc


def _proj_res(x, mod, acts, ws, *, tm, seg_len, gate_row):
    t, d = x.shape
    in_specs = [pl.BlockSpec((tm, d), lambda i: (i, 0)),
                pl.BlockSpec((None, 6, d), lambda i: (i * tm // seg_len, 0, 0))]
    args = [x, mod]
    for a, w in zip(acts, ws):
        in_specs += [pl.BlockSpec((tm, a.shape[1]), lambda i: (i, 0)),
                     pl.BlockSpec(w.shape, lambda i: (0, 0))]
        args += [a, w]
    return pl.pallas_call(
        functools.partial(_proj_res_body, n_in=len(acts), gate_row=gate_row),
        grid=(t // tm,),
        in_specs=in_specs,
        out_specs=pl.BlockSpec((tm, d), lambda i: (i, 0)),
        out_shape=jax.ShapeDtypeStruct((t, d), F32),
        compiler_params=_params("parallel"),
        name="proj_res",
    )(*args)


def _ffn_body(x_ref, mod_ref, g_ref, wg_ref, wu_ref, wd_ref, o_ref, h_ref, acc_ref):
    f = pl.program_id(1)

    @pl.when(f == 0)
    def _():
        h_ref[...] = _norm_mod(x_ref[...], g_ref[...], mod_ref[3:4, :], mod_ref[4:5, :]).astype(BF16)
        acc_ref[...] = jnp.zeros_like(acc_ref)

    h = h_ref[...]
    a = _silu(_dot(h, wg_ref[...])) * _dot(h, wu_ref[...])
    acc_ref[...] += _dot(a.astype(BF16), wd_ref[...])

    @pl.when(f == pl.num_programs(1) - 1)
    def _():
        o_ref[...] = x_ref[...] + mod_ref[5:6, :] * acc_ref[...]


def _ffn(x, mod, g, wg, wu, wd, *, tm, tf, seg_len):
    t, d = x.shape
    ff = wg.shape[1]
    return pl.pallas_call(
        _ffn_body,
        grid=(t // tm, ff // tf),
        in_specs=[pl.BlockSpec((tm, d), lambda i, f: (i, 0)),
                  pl.BlockSpec((None, 6, d), lambda i, f: (i * tm // seg_len, 0, 0)),
                  pl.BlockSpec((1, d), lambda i, f: (0, 0)),
                  pl.BlockSpec((d, tf), lambda i, f: (0, f)),
                  pl.BlockSpec((d, tf), lambda i, f: (0, f)),
                  pl.BlockSpec((tf, d), lambda i, f: (f, 0))],
        out_specs=pl.BlockSpec((tm, d), lambda i, f: (i, 0)),
        out_shape=jax.ShapeDtypeStruct((t, d), F32),
        scratch_shapes=[pltpu.VMEM((tm, d), BF16), pltpu.VMEM((tm, d), F32)],
        compiler_params=_params("parallel", "arbitrary"),
        name="ffn",
    )(x, mod, g.reshape(1, d), wg, wu, wd)


def _group_ms(x, gmat):
    hi, lo = _split_bf16(x * x)
    return (_dot(hi, gmat) + _dot(lo, gmat)) * (1.0 / DIFF_DH)


def _qkv_body(x_ref, mod_ref, g_ref, w_ref, gm_ref, qn_ref, kn_ref, cos_ref, sa_ref, sb_ref,
              q_ref, k_ref, v_ref, kf_ref, vf_ref, *, scale):
    h = _norm_mod(x_ref[...], g_ref[...], mod_ref[0:1, :], mod_ref[1:2, :]).astype(BF16)
    z = _dot(h, w_ref[...])
    hd = DIFF_HEADS * 2 * DIFF_DH
    cos_t, sin_a, sin_b = cos_ref[...], sa_ref[...], sb_ref[...]
    gm = gm_ref[...]
    for hh in range(DIFF_HEADS):
        sl = slice(hh * LANES, (hh + 1) * LANES)
        qh = z[:, hh * LANES:(hh + 1) * LANES]
        qh = qh * lax.rsqrt(_group_ms(qh, gm) + NORM_EPS) * qn_ref[...]
        q_ref[:, sl] = (_rope(qh, cos_t, sin_a, sin_b, DIFF_DH // 4) * scale).astype(BF16)
        kh = z[:, hd + hh * LANES:hd + (hh + 1) * LANES]
        kh = kh * lax.rsqrt(_group_ms(kh, gm) + NORM_EPS) * kn_ref[...]
        kf_ref[:, sl] = kh
        k_ref[:, sl] = _rope(kh, cos_t, sin_a, sin_b, DIFF_DH // 4).astype(BF16)
    v = z[:, 2 * hd:]
    vf_ref[...] = v
    v_ref[...] = v.astype(BF16)


def _qkv(x, mod, p, tables, *, tm, seg_len, t_lat, lat_seq):
    t, d = x.shape
    hd = DIFF_HEADS * 2 * DIFF_DH
    cos_t, sin_a, sin_b = tables
    wqk = p['w_qkv'][:, :2 * hd].reshape(d, 2, 2, DIFF_HEADS, DIFF_DH)
    wqk = wqk.transpose(0, 1, 3, 2, 4).reshape(d, 2 * hd)
    w = jnp.concatenate([wqk, p['w_qkv'][:, 2 * hd:]], axis=1).astype(BF16)
    gi = np.arange(LANES) // DIFF_DH
    gmat = jnp.asarray((gi[:, None] == gi[None, :]).astype(np.float32)).astype(BF16)
    qn = jnp.tile(p['q_norm'], 2).reshape(1, LANES)
    kn = jnp.tile(p['k_norm'], 2).reshape(1, LANES)
    pos_blocks = lat_seq // tm
    lat_tiles = t_lat // tm
    ctx_tiles = (t - t_lat) // tm
    tspec = pl.BlockSpec((tm, LANES), lambda i: (jnp.where(i < lat_tiles, i % pos_blocks, pos_blocks), 0))
    full = lambda i: (0, 0)
    row = pl.BlockSpec((tm, hd), lambda i: (i, 0))
    fspec = pl.BlockSpec((tm, hd), lambda i: (jnp.where(i < lat_tiles, ctx_tiles, i - lat_tiles), 0))
    fshape = jax.ShapeDtypeStruct((t - t_lat + tm, hd), F32)
    return pl.pallas_call(
        functools.partial(_qkv_body, scale=DIFF_DH ** -0.5),
        grid=(t // tm,),
        in_specs=[pl.BlockSpec((tm, d), lambda i: (i, 0)),
                  pl.BlockSpec((None, 6, d), lambda i: (i * tm // seg_len, 0, 0)),
                  pl.BlockSpec((1, d), full),
                  pl.BlockSpec((d, 3 * hd), full),
                  pl.BlockSpec((LANES, LANES), full),
                  pl.BlockSpec((1, LANES), full), pl.BlockSpec((1, LANES), full),
                  tspec, tspec, tspec],
        out_specs=[row, row, row, fspec, fspec],
        out_shape=[jax.ShapeDtypeStruct((t, hd), BF16)] * 3 + [fshape, fshape],
        compiler_params=_params("arbitrary"),
        name="qkv",
    )(x, mod, p['norm1'].reshape(1, d), w, gmat, qn, kn, cos_t, sin_a, sin_b)


def _route(logits):
    lane = lax.broadcasted_iota(jnp.int32, logits.shape, 1)
    neg = jnp.float32(-jnp.inf)
    lg = jnp.where(lane < N_EXPERTS, logits, neg)
    m1 = jnp.max(lg, axis=-1, keepdims=True)
    i1 = jnp.min(jnp.where(lg == m1, lane, LANES), axis=-1, keepdims=True)
    lg2 = jnp.where(lane == i1, neg, lg)
    m2 = jnp.max(lg2, axis=-1, keepdims=True)
    i2 = jnp.min(jnp.where(lg2 == m2, lane, LANES), axis=-1, keepdims=True)
    e = jnp.exp(m2 - m1)
    w1 = 1.0 / (1.0 + e)
    w2 = e / (1.0 + e)
    return jnp.where(lane == i1, w1, 0.0) + jnp.where(lane == i2, w2, 0.0)


def _moe_dense_body(x_ref, mod_ref, g_ref, wr_ref, wg_ref, wu_ref, wd_ref, o_ref,
                    h_ref, gates_ref, acc_ref):
    e = pl.program_id(1)
    f = pl.program_id(2)
    first = jnp.logical_and(e == 0, f == 0)
    last = jnp.logical_and(e == pl.num_programs(1) - 1, f == pl.num_programs(2) - 1)

    @pl.when(first)
    def _():
        h = _norm_mod(x_ref[...], g_ref[...], mod_ref[3:4, :], mod_ref[4:5, :])
        h_ref[...] = h.astype(BF16)
        gates_ref[...] = _route(_dot_f32(h, wr_ref[...]))
        acc_ref[...] = jnp.zeros_like(acc_ref)

    h = h_ref[...]
    a = _silu(_dot(h, wg_ref[...].astype(BF16))) * _dot(h, wu_ref[...].astype(BF16))
    y = _dot(a.astype(BF16), wd_ref[...].astype(BF16))
    gates = gates_ref[...]
    lane = lax.broadcasted_iota(jnp.int32, gates.shape, 1)
    ge = jnp.sum(jnp.where(lane == e, gates, 0.0), axis=-1, keepdims=True)
    acc_ref[...] += ge * y

    @pl.when(last)
    def _():
        o_ref[...] = x_ref[...] + mod_ref[5:6, :] * acc_ref[...]


def _moe_dense(x, mod, g, w_router, wg, wu, wd, *, tm, tf, seg_len):
    t, d = x.shape
    ne, _, ff = wg.shape
    wr = jnp.zeros((d, LANES), F32).at[:, :ne].set(w_router)
    return pl.pallas_call(
        _moe_dense_body,
        grid=(t // tm, ne, ff // tf),
        in_specs=[pl.BlockSpec((tm, d), lambda i, e, f: (i, 0)),
                  pl.BlockSpec((None, 6, d), lambda i, e, f: (i * tm // seg_len, 0, 0)),
                  pl.BlockSpec((1, d), lambda i, e, f: (0, 0)),
                  pl.BlockSpec((d, LANES), lambda i, e, f: (0, 0)),
                  pl.BlockSpec((None, d, tf), lambda i, e, f: (e, 0, f)),
                  pl.BlockSpec((None, d, tf), lambda i, e, f: (e, 0, f)),
                  pl.BlockSpec((None, tf, d), lambda i, e, f: (e, f, 0))],
        out_specs=pl.BlockSpec((tm, d), lambda i, e, f: (i, 0)),
        out_shape=jax.ShapeDtypeStruct((t, d), F32),
        scratch_shapes=[pltpu.VMEM((tm, d), BF16), pltpu.VMEM((tm, LANES), F32), pltpu.VMEM((tm, d), F32)],
        compiler_params=_params("parallel", "arbitrary", "arbitrary"),
        name="moe_dense",
    )(x, mod, g.reshape(1, d), wr, wg, wu, wd)


def _even_layer(x, cond8, p, cache_ckv, cache_kr, *, nb, lat_seq, n_ctx, ctx_seq):
    t, d = x.shape
    t_lat = nb * lat_seq
    seg_len = lat_seq
    past = cache_ckv.shape[1]
    mod = _adaln(cond8, p['w_mod'], p['b_mod'])

    n_in = p['w_in'].shape[1]
    n_pad = -(-n_in // LANES) * LANES
    w_in = jnp.zeros((d, n_pad), BF16).at[:, :n_in].set(p['w_in'].astype(BF16))
    z = _lin_in(x, mod, p['norm1'], w_in, tm=512, seg_len=seg_len)

    u, x0 = _hy_pre(z, p['hy_conv_w'], p['hy_conv_b'], t_lat=t_lat, lat_seq=lat_seq, ctx_seq=ctx_seq)
    k_raw, k_sum = _hyena_filters(p, lat_seq=lat_seq, ctx_seq=ctx_seq)
    dbias = p['hy_dbias'].reshape(1, HY_WIDTH)
    y_lat = _hyena_lat(k_raw[:2 * lat_seq], k_sum[0:1], u[:t_lat], x0[:t_lat], dbias, seq=lat_seq, nb=nb)
    y_ctx = _hyena_ctx(k_raw[2 * lat_seq:], k_sum[1:2], u[t_lat:], x0[t_lat:], dbias, seq=ctx_seq)
    y_hy = jnp.concatenate([y_lat, y_ctx], axis=0)

    tm = 512
    tables = _rope_tables(lat_seq, MLA_ROPE, (MLA_NOPE,), tm)
    q, ckvn = _mla_q(z, p, tables, tm=tm, t_lat=t_lat, lat_seq=lat_seq)
    kr_col = 3 * HY_WIDTH + MLA_Q_RANK + MLA_KV_RANK
    kr = z[:, kr_col:kr_col + LANES]
    cache_kr_p = jnp.zeros((nb, past, LANES), F32).at[:, :, :MLA_ROPE].set(cache_kr)
    ckvn_rows = jnp.concatenate(
        [jnp.concatenate([cache_ckv, ckvn[:t_lat].reshape(nb, lat_seq, -1)], axis=1).reshape(nb * (past + lat_seq), -1),
         ckvn[t_lat:]], axis=0)
    kr_rows = jnp.concatenate(
        [jnp.concatenate([cache_kr_p, kr[:t_lat].reshape(nb, lat_seq, LANES)], axis=1).reshape(nb * (past + lat_seq), LANES),
         kr[t_lat:]], axis=0)
    k_all, v_all = _mla_kv(ckvn_rows, kr_rows, p, tables, tm=tm, nb=nb, past=past, lat_seq=lat_seq)
    o_lat = _mla_attn(q, k_all, v_all, tq=256, n_seq=nb, seq_q=lat_seq, seq_k=past + lat_seq,
                      q_row0=0, k_row0=0)
    o_ctx = _mla_attn(q, k_all, v_all, tq=ctx_seq, n_seq=n_ctx, seq_q=ctx_seq, seq_k=ctx_seq,
                      q_row0=t_lat, k_row0=nb * (past + lat_seq))
    o = jnp.concatenate([o_lat, o_ctx], axis=0)

    w_out = p['w_out'].astype(BF16)
    x = _proj_res(x, mod, [y_hy, o], [w_out[:HY_WIDTH], w_out[HY_WIDTH:]], tm=512, seg_len=seg_len, gate_row=2)
    x = _ffn(x, mod, p['norm2'], p['ffn_w_gate'].astype(BF16), p['ffn_w_up'].astype(BF16),
             p['ffn_w_down'].astype(BF16), tm=512, tf=1408, seg_len=seg_len)
    new_ckv = ckvn[t_lat:].reshape(n_ctx, ctx_seq, -1)
    new_kr = kr[t_lat:, :MLA_ROPE].reshape(n_ctx, ctx_seq, MLA_ROPE)
    return x, new_ckv, new_kr


def _odd_layer(x, cond8, p, cache_k, cache_v, lambda_init, *, nb, lat_seq, n_ctx, ctx_seq):
    t, d = x.shape
    t_lat = nb * lat_seq
    seg_len = lat_seq
    mod = _adaln(cond8, p['w_mod'], p['b_mod'])
    tm = 512
    tables = _rope_tables(lat_seq, DIFF_DH, (0, DIFF_DH), tm)
    q, k, v, kf, vf = _qkv(x, mod, p, tables, tm=tm, seg_len=seg_len, t_lat=t_lat, lat_seq=lat_seq)
    lam_p = jnp.stack([p['lam_q1'], p['lam_k1'], p['lam_q2'], p['lam_k2']])
    o_lat = _diff_attn(q, k, v, cache_k.astype(BF16), cache_v.astype(BF16), lam_p, p['subln'],
                       tq=256, n_seq=nb, seq_q=lat_seq, q_row0=0, lambda_init=lambda_init)
    o_ctx = _diff_attn(q, k, v, None, None, lam_p, p['subln'],
                       tq=ctx_seq, n_seq=n_ctx, seq_q=ctx_seq, q_row0=t_lat, lambda_init=lambda_init)
    o = jnp.concatenate([o_lat, o_ctx], axis=0)
    x = _proj_res(x, mod, [o], [p['w_out'].astype(BF16)], tm=512, seg_len=seg_len, gate_row=2)
    x = _moe_dense(x, mod, p['norm2'], p['w_router'], p['moe_w_gate'], p['moe_w_up'], p['moe_w_down'],
                   tm=1024, tf=512, seg_len=seg_len)
    hd = 2 * DIFF_DH
    t_ctx = t - t_lat
    new_k = kf[:t_ctx].reshape(n_ctx, ctx_seq, DIFF_HEADS, hd).transpose(0, 2, 1, 3)
    new_v = vf[:t_ctx].reshape(n_ctx, ctx_seq, DIFF_HEADS, hd).transpose(0, 2, 1, 3)
    return x, new_k, new_v


def kernel(x_prompt, x_sample, cache_l0_ckv, cache_l0_krope, cache_l1_k, cache_l1_v, c, c_ctx,
           l0_w_mod, l0_b_mod, l0_norm1, l0_norm2, l0_w_in, l0_hy_conv_w, l0_hy_conv_b,
           l0_hy_fw1, l0_hy_fb1, l0_hy_freq1, l0_hy_fw2, l0_hy_fb2, l0_hy_freq2, l0_hy_fw3, l0_hy_dbias,
           l0_mla_qa_norm, l0_mla_w_uq, l0_mla_kva_norm, l0_mla_w_ukv, l0_mla_q_norm, l0_mla_k_norm,
           l0_w_out, l0_ffn_w_gate, l0_ffn_w_up, l0_ffn_w_down,
           l1_w_mod, l1_b_mod, l1_norm1, l1_norm2, l1_w_qkv, l1_q_norm, l1_k_norm,
           l1_lam_q1, l1_lam_k1, l1_lam_q2, l1_lam_k2, l1_subln, l1_w_out,
           l1_w_router, l1_moe_w_gate, l1_moe_w_up, l1_moe_w_down):
    even = {
        'w_mod': l0_w_mod, 'b_mod': l0_b_mod, 'norm1': l0_norm1, 'norm2': l0_norm2, 'w_in': l0_w_in,
        'hy_conv_w': l0_hy_conv_w, 'hy_conv_b': l0_hy_conv_b, 'hy_fw1': l0_hy_fw1, 'hy_fb1': l0_hy_fb1,
        'hy_freq1': l0_hy_freq1, 'hy_fw2': l0_hy_fw2, 'hy_fb2': l0_hy_fb2, 'hy_freq2': l0_hy_freq2,
        'hy_fw3': l0_hy_fw3, 'hy_dbias': l0_hy_dbias, 'qa_norm': l0_mla_qa_norm, 'w_uq': l0_mla_w_uq,
        'kva_norm': l0_mla_kva_norm, 'w_ukv': l0_mla_w_ukv, 'q_norm': l0_mla_q_norm, 'k_norm': l0_mla_k_norm,
        'w_out': l0_w_out, 'ffn_w_gate': l0_ffn_w_gate, 'ffn_w_up': l0_ffn_w_up, 'ffn_w_down': l0_ffn_w_down,
    }
    odd = {
        'w_mod': l1_w_mod, 'b_mod': l1_b_mod, 'norm1': l1_norm1, 'norm2': l1_norm2, 'w_qkv': l1_w_qkv,
        'q_norm': l1_q_norm, 'k_norm': l1_k_norm, 'lam_q1': l1_lam_q1, 'lam_k1': l1_lam_k1,
        'lam_q2': l1_lam_q2, 'lam_k2': l1_lam_k2, 'subln': l1_subln, 'w_out': l1_w_out,
        'w_router': l1_w_router, 'moe_w_gate': l1_moe_w_gate, 'moe_w_up': l1_moe_w_up,
        'moe_w_down': l1_moe_w_down,
    }
    n_ctx, ctx_seq, d = x_prompt.shape
    nb, lat_seq, _ = x_sample.shape
    assert n_ctx * ctx_seq == lat_seq, "segment layout needs equally sized modulation segments"
    dims = dict(nb=nb, lat_seq=lat_seq, n_ctx=n_ctx, ctx_seq=ctx_seq)
    t_lat = nb * lat_seq
    x = jnp.concatenate([x_sample.reshape(t_lat, d), x_prompt.reshape(n_ctx * ctx_seq, d)], axis=0)
    cond8 = jnp.zeros((SUBLANES, d), F32).at[:nb].set(c).at[nb].set(c_ctx)

    x, new_l0_ckv, new_l0_krope = _even_layer(x, cond8, even, cache_l0_ckv, cache_l0_krope, **dims)
    lambda_init = 0.8 - 0.6 * math.exp(-0.3 * 1)
    x, new_l1_k, new_l1_v = _odd_layer(x, cond8, odd, cache_l1_k, cache_l1_v, lambda_init, **dims)

    y_sample = x[:t_lat].reshape(nb, lat_seq, d)
    y_prompt = x[t_lat:].reshape(n_ctx, ctx_seq, d)
    return (y_prompt, y_sample, new_l0_ckv, new_l0_krope, new_l1_k, new_l1_v)
```

```python
import functools
import math

import numpy as np
import jax
import jax.numpy as jnp
from jax import lax
from jax.experimental import pallas as pl
from jax.experimental.pallas import tpu as pltpu

F32 = jnp.float32
BF16 = jnp.bfloat16

VMEM_LIMIT_BYTES = 56 * 1024 * 1024
LANES = 128
SUBLANES = 8
LOG2_E = math.log2(math.e)

GRID_W = 64
ROPE_BASE = 10000.0
NORM_EPS = 1e-6
HY_WIDTH = 512
HY_BANDS = 16
HY_FAST_DECAY_PCT = 0.3
HY_SLOW_DECAY_PCT = 1.5
HY_DECAY_TARGET = 1e-2
MLA_HEADS = 8
MLA_NOPE = 64
MLA_ROPE = 32
MLA_QK = MLA_NOPE + MLA_ROPE
MLA_V = 64
MLA_Q_RANK = 768
MLA_KV_RANK = 256
DIFF_HEADS = 8
DIFF_DH = 64
N_EXPERTS = 8


def _params(*sem):
    return pltpu.CompilerParams(dimension_semantics=sem, vmem_limit_bytes=VMEM_LIMIT_BYTES)


def _dot(a, b):
    return jnp.dot(a, b, preferred_element_type=F32)


def _dot_nt(a, b):
    return lax.dot_general(a, b, (((1,), (1,)), ((), ())), preferred_element_type=F32)


def _split_bf16(a):
    hi = a.astype(BF16)
    lo = (a - hi.astype(F32)).astype(BF16)
    return hi, lo


def _dot_f32(a, b):
    ah, al = _split_bf16(a)
    bh, bl = _split_bf16(b)
    return _dot(ah, bh) + (_dot(al, bh) + _dot(ah, bl))


def _rms(x, g, n=None):
    n = x.shape[-1] if n is None else n
    ms = jnp.sum(x * x, axis=-1, keepdims=True) * (1.0 / n)
    return x * lax.rsqrt(ms + NORM_EPS) * g


def _norm_mod(x, g, shift, scale):
    return _rms(x, g) * (1.0 + scale) + shift


def _silu(x):
    return x / (1.0 + jnp.exp(-x))


def _adaln_body(c_ref, w_ref, b_ref, o_ref):
    o_ref[...] = _dot_f32(_silu(c_ref[...]), w_ref[...]) + b_ref[...]


def _adaln(cond8, w_mod, b_mod):
    d, n = w_mod.shape
    tn = n // 4
    out = pl.pallas_call(
        _adaln_body,
        grid=(n // tn,),
        in_specs=[pl.BlockSpec((SUBLANES, d), lambda j: (0, 0)),
                  pl.BlockSpec((d, tn), lambda j: (0, j)),
                  pl.BlockSpec((1, tn), lambda j: (0, j))],
        out_specs=pl.BlockSpec((SUBLANES, tn), lambda j: (0, j)),
        out_shape=jax.ShapeDtypeStruct((SUBLANES, n), F32),
        compiler_params=_params("arbitrary"),
        name="adaln",
    )(cond8, w_mod, b_mod.reshape(1, n))
    return out.reshape(SUBLANES, 6, d)


def _lin_in_body(x_ref, mod_ref, g_ref, w_ref, o_ref):
    h = _norm_mod(x_ref[...], g_ref[...], mod_ref[0:1, :], mod_ref[1:2, :])
    o_ref[...] = _dot(h.astype(BF16), w_ref[...])


def _lin_in(x, mod, g, w, *, tm, seg_len):
    t, d = x.shape
    n = w.shape[1]
    return pl.pallas_call(
        _lin_in_body,
        grid=(t // tm,),
        in_specs=[pl.BlockSpec((tm, d), lambda i: (i, 0)),
                  pl.BlockSpec((None, 6, d), lambda i: (i * tm // seg_len, 0, 0)),
                  pl.BlockSpec((1, d), lambda i: (0, 0)),
                  pl.BlockSpec((d, n), lambda i: (0, 0))],
        out_specs=pl.BlockSpec((tm, n), lambda i: (i, 0)),
        out_shape=jax.ShapeDtypeStruct((t, n), F32),
        compiler_params=_params("parallel"),
        name="lin_in",
    )(x, mod, g.reshape(1, d), w)


def _hy_pre_body(z_ref, zp_ref, zn_ref, w_ref, b_ref, u_ref, x0_ref, *, tm, lat_tiles, tiles_per_seq):
    i = pl.program_id(0)
    z = z_ref[...]
    in_lat = i < lat_tiles
    has_prev = jnp.logical_and(in_lat, i % tiles_per_seq != 0)
    has_next = jnp.logical_and(in_lat, i % tiles_per_seq != tiles_per_seq - 1)
    prev_row = jnp.where(has_prev, zp_ref[SUBLANES - 1:SUBLANES, :], 0.0)
    next_row = jnp.where(has_next, zn_ref[0:1, :], 0.0)
    rows = lax.broadcasted_iota(jnp.int32, z.shape, 0)
    z_m = jnp.where(rows == 0, prev_row, pltpu.roll(z, 1, 0))
    z_p = jnp.where(rows == tm - 1, next_row, pltpu.roll(z, tm - 1, 0))
    zc = b_ref[...] + z_m * w_ref[0:1, :] + z * w_ref[1:2, :] + z_p * w_ref[2:3, :]
    c = HY_WIDTH
    x0_ref[...] = zc[:, :c]
    u_ref[...] = zc[:, 2 * c:] * zc[:, c:2 * c]


def _hy_pre(z, conv_w, conv_b, *, t_lat, lat_seq, ctx_seq):
    t = z.shape[0]
    tm = ctx_seq
    c3 = 3 * HY_WIDTH
    nb8 = t // SUBLANES
    body = functools.partial(_hy_pre_body, tm=tm, lat_tiles=t_lat // tm, tiles_per_seq=lat_seq // tm)
    return pl.pallas_call(
        body,
        grid=(t // tm,),
        in_specs=[pl.BlockSpec((tm, c3), lambda i: (i, 0)),
                  pl.BlockSpec((SUBLANES, c3), lambda i: (jnp.maximum(i * (tm // SUBLANES) - 1, 0), 0)),
                  pl.BlockSpec((SUBLANES, c3), lambda i: (jnp.minimum((i + 1) * (tm // SUBLANES), nb8 - 1), 0)),
                  pl.BlockSpec((3, c3), lambda i: (0, 0)),
                  pl.BlockSpec((1, c3), lambda i: (0, 0))],
        out_specs=[pl.BlockSpec((tm, HY_WIDTH), lambda i: (i, 0)),
                   pl.BlockSpec((tm, HY_WIDTH), lambda i: (i, 0))],
        out_shape=[jax.ShapeDtypeStruct((t, HY_WIDTH), F32),
                   jax.ShapeDtypeStruct((t, HY_WIDTH), F32)],
        compiler_params=_params("parallel"),
        name="hy_pre",
    )(z, z, z, conv_w, conv_b.reshape(1, c3))


def _filter_embedding(seq):
    t01 = np.linspace(0.0, 1.0, seq)[:, None]
    w = 2.0 * math.pi * np.arange(seq)[:, None] / seq
    f = np.linspace(1e-4, HY_BANDS - 1, HY_BANDS)[None, :]
    z = np.concatenate([t01, np.cos(f * w), -np.sin(f * w)], axis=-1)
    z_rev = np.concatenate([z[:1], z[:0:-1]], axis=0)
    zz = np.concatenate([z, z_rev], axis=0)
    out = np.zeros((2 * seq, LANES), np.float32)
    out[:, :zz.shape[1]] = zz
    return out


def _filter_body(zz_ref, dl_ref, w1_ref, b1_ref, f1_ref, w2_ref, b2_ref, f2_ref, w3_ref,
                 k_ref, s_ref, *, tm, lat_tiles, ctx_tiles):
    i = pl.program_id(0)
    zz = zz_ref[...]
    h = jnp.sin(f1_ref[...] * (_dot_f32(zz, w1_ref[...]) + b1_ref[...]))
    h = jnp.sin(f2_ref[...] * (_dot_f32(h, w2_ref[...]) + b2_ref[...]))
    h = _dot_f32(h, w3_ref[...])
    is_bwd = jnp.logical_or(jnp.logical_and(i >= lat_tiles // 2, i < lat_tiles),
                            i >= lat_tiles + ctx_tiles // 2)
    first_bwd = jnp.logical_or(i == lat_tiles // 2, i == lat_tiles + ctx_tiles // 2)
    window = jnp.exp(-zz[:, 0:1] * dl_ref[...])
    k = jnp.where(is_bwd, h[:, HY_WIDTH:], h[:, :HY_WIDTH]) * window
    rows = lax.broadcasted_iota(jnp.int32, k.shape, 0)
    k = jnp.where(jnp.logical_and(first_bwd, rows == 0), 0.0, k)
    k_ref[...] = k
    s = jnp.sum(jnp.abs(k), axis=0, keepdims=True)

    @pl.when(i == 0)
    def _():
        s_ref[...] = jnp.zeros_like(s_ref)

    @pl.when(i < lat_tiles)
    def _():
        s_ref[0:1, :] += s

    @pl.when(i >= lat_tiles)
    def _():
        s_ref[1:2, :] += s


def _hyena_filters(p, *, lat_seq, ctx_seq):
    tm = ctx_seq
    zz = jnp.asarray(np.concatenate([_filter_embedding(lat_seq), _filter_embedding(ctx_seq)], axis=0))
    rows = zz.shape[0]
    max_decay = math.log(HY_DECAY_TARGET) / HY_FAST_DECAY_PCT
    min_decay = math.log(HY_DECAY_TARGET) / HY_SLOW_DECAY_PCT
    deltas = jnp.asarray(np.abs(np.linspace(min_decay, max_decay, HY_WIDTH))[None, :].astype(np.float32))
    emb, hid = p['hy_fw1'].shape

    def pad2(a, r, c):
        return jnp.zeros((r, c), F32).at[:a.shape[0], :a.shape[1]].set(a)

    w1 = pad2(p['hy_fw1'], LANES, LANES)
    b1 = pad2(p['hy_fb1'][None, :], 1, LANES)
    f1 = pad2(p['hy_freq1'][None, :], 1, LANES)
    w2 = pad2(p['hy_fw2'], LANES, LANES)
    b2 = pad2(p['hy_fb2'][None, :], 1, LANES)
    f2 = pad2(p['hy_freq2'][None, :], 1, LANES)
    w3 = pad2(p['hy_fw3'], LANES, 2 * HY_WIDTH)
    body = functools.partial(_filter_body, tm=tm, lat_tiles=2 * lat_seq // tm, ctx_tiles=2 * ctx_seq // tm)
    full = lambda i: (0, 0)
    return pl.pallas_call(
        body,
        grid=(rows // tm,),
        in_specs=[pl.BlockSpec((tm, LANES), lambda i: (i, 0)),
                  pl.BlockSpec((1, HY_WIDTH), full),
                  pl.BlockSpec((LANES, LANES), full), pl.BlockSpec((1, LANES), full), pl.BlockSpec((1, LANES), full),
                  pl.BlockSpec((LANES, LANES), full), pl.BlockSpec((1, LANES), full), pl.BlockSpec((1, LANES), full),
                  pl.BlockSpec((LANES, 2 * HY_WIDTH), full)],
        out_specs=[pl.BlockSpec((tm, HY_WIDTH), lambda i: (i, 0)),
                   pl.BlockSpec((SUBLANES, HY_WIDTH), full)],
        out_shape=[jax.ShapeDtypeStruct((rows, HY_WIDTH), F32),
                   jax.ShapeDtypeStruct((SUBLANES, HY_WIDTH), F32)],
        compiler_params=_params("arbitrary"),
        name="hy_filter",
    )(zz, deltas, w1, b1, f1, w2, b2, f2, w3)


def _stack_complex(z):
    return np.block([[z.real, -z.imag], [z.imag, z.real]])


def _dft_consts_two_level(seq, n1, n2):
    n = 2 * seq
    assert n1 * n2 == n
    a1 = np.arange(n1)
    f1_full = np.exp(-2j * np.pi * np.outer(a1, a1) / n1)
    f1_u = np.concatenate([f1_full.real, f1_full.imag], axis=0)[:, :n1 // 2]
    f1_k = np.concatenate([f1_full.real, f1_full.imag], axis=0)
    a2 = np.arange(n2)
    f = a1[:, None, None] + n1 * a2[None, :, None]
    z = np.exp(-2j * np.pi * (f * a2[None, None, :]) / n)
    mf = np.stack([_stack_complex(z[i]) for i in range(n1)])
    mi = np.stack([_stack_complex(np.conj(z[i]).T) for i in range(n1)])
    g = np.exp(2j * np.pi * np.outer(a1[:n1 // 2], a1) / n1) / n
    gc, gs = g.real, -g.imag
    as32 = lambda a: jnp.asarray(a.astype(np.float32))
    return as32(f1_u), as32(f1_k), as32(mf), as32(mi), as32(gc), as32(gs)


def _dft_consts_one_level(seq):
    n = 2 * seq
    a = np.arange(n)
    z = np.exp(-2j * np.pi * np.outer(a, a) / n)
    mf = np.concatenate([z.real, z.imag], axis=0)
    zi = np.exp(2j * np.pi * np.outer(a[:seq], a) / n) / n
    mi = np.concatenate([zi.real, -zi.imag], axis=1)
    as32 = lambda a: jnp.asarray(a.astype(np.float32))
    return as32(mf), as32(mi)


def _lmat_body(f_ref, x_ref, sc_ref, or_ref, oi_ref):
    x = (x_ref[...] * (1.0 / sc_ref[...])).astype(BF16)
    o = _dot(f_ref[...], x)
    h = o.shape[0] // 2
    or_ref[...] = o[:h]
    oi_ref[...] = o[h:]


def _lmat(f, x, scale_row, *, tn):
    g, k, cols = x.shape
    m2 = f.shape[0]
    m = m2 // 2
    return pl.pallas_call(
        _lmat_body,
        grid=(g, cols // tn),
        in_specs=[pl.BlockSpec((m2, k), lambda b, j: (0, 0)),
                  pl.BlockSpec((None, k, tn), lambda b, j: (b, 0, j)),
                  pl.BlockSpec((1, tn), lambda b, j: (0, 0))],
        out_specs=[pl.BlockSpec((None, m, tn), lambda b, j: (b, 0, j)),
                   pl.BlockSpec((None, m, tn), lambda b, j: (b, 0, j))],
        out_shape=[jax.ShapeDtypeStruct((g, m, cols), F32),
                   jax.ShapeDtypeStruct((g, m, cols), F32)],
        compiler_params=_params("parallel", "parallel"),
        name="hy_dft1",
    )(f.astype(BF16), x, scale_row)


def _spec_fwd_body(mf_ref, ar_ref, ai_ref, kr_ref, ki_ref):
    a = jnp.concatenate([ar_ref[...], ai_ref[...]], axis=0).astype(BF16)
    x = _dot(mf_ref[...], a)
    h = x.shape[0] // 2
    kr_ref[...] = x[:h]
    ki_ref[...] = x[h:]


def _spec_fwd(mf, ar, ai):
    n1, n2, c = ar.shape
    spec = pl.BlockSpec((None, n2, c), lambda i: (i, 0, 0))
    return pl.pallas_call(
        _spec_fwd_body,
        grid=(n1,),
        in_specs=[pl.BlockSpec((None, 2 * n2, 2 * n2), lambda i: (i, 0, 0)), spec, spec],
        out_specs=[spec, spec],
        out_shape=[jax.ShapeDtypeStruct((n1, n2, c), F32)] * 2,
        compiler_params=_params("parallel"),
        name="hy_spec_filter",
    )(mf, ar, ai)


def _spec_mul_body(mf_ref, mi_ref, kr_ref, ki_ref, ar_ref, ai_ref, br_ref, bi_ref):
    a = jnp.concatenate([ar_ref[...], ai_ref[...]], axis=0).astype(BF16)
    x = _dot(mf_ref[...], a)
    h = x.shape[0] // 2
    xr, xi = x[:h], x[h:]
    kr, ki = kr_ref[...], ki_ref[...]
    y = jnp.concatenate([xr * kr - xi * ki, xr * ki + xi * kr], axis=0).astype(BF16)
    b = _dot(mi_ref[...], y)
    br_ref[...] = b[:h]
    bi_ref[...] = b[h:]


def _spec_mul(mf, mi, kr, ki, ar, ai):
    nb, n1, n2, c = ar.shape
    mspec = pl.BlockSpec((None, 2 * n2, 2 * n2), lambda i, b: (i, 0, 0))
    kspec = pl.BlockSpec((None, n2, c), lambda i, b: (i, 0, 0))
    aspec = pl.BlockSpec((None, None, n2, c), lambda i, b: (b, i, 0, 0))
    return pl.pallas_call(
        _spec_mul_body,
        grid=(n1, nb),
        in_specs=[mspec, mspec, kspec, kspec, aspec, aspec],
        out_specs=[aspec, aspec],
        out_shape=[jax.ShapeDtypeStruct((nb, n1, n2, c), F32)] * 2,
        compiler_params=_params("parallel", "arbitrary"),
        name="hy_spec_mul",
    )(mf, mi, kr, ki, ar, ai)


def _idft1_body(gc_ref, gs_ref, br_ref, bi_ref, u_ref, x0_ref, db_ref, o_ref):
    y = _dot(gc_ref[...], br_ref[...].astype(BF16)) + _dot(gs_ref[...], bi_ref[...].astype(BF16))
    u = u_ref[...]
    o_ref[...] = (y + u * db_ref[...]) * x0_ref[...]


def _idft1(gc, gs, br, bi, u, x0, db_row, *, tn):
    nb, n1, cols = br.shape
    m = gc.shape[0]
    gspec = pl.BlockSpec((m, n1), lambda b, j: (0, 0))
    bspec = pl.BlockSpec((None, n1, tn), lambda b, j: (b, 0, j))
    uspec = pl.BlockSpec((None, m, tn), lambda b, j: (b, 0, j))
    return pl.pallas_call(
        _idft1_body,
        grid=(nb, cols // tn),
        in_specs=[gspec, gspec, bspec, bspec, uspec, uspec, pl.BlockSpec((1, tn), lambda b, j: (0, 0))],
        out_specs=uspec,
        out_shape=jax.ShapeDtypeStruct((nb, m, cols), F32),
        compiler_params=_params("parallel", "parallel"),
        name="hy_idft1",
    )(gc.astype(BF16), gs.astype(BF16), br, bi, u, x0, db_row)


def _ctx_filter_body(mf_ref, k_ref, sc_ref, kf_ref):
    kf_ref[...] = _dot(mf_ref[...], (k_ref[...] * (1.0 / sc_ref[...])).astype(BF16))


def _ctx_conv_body(mf_ref, mi_ref, kf_ref, u_ref, x0_ref, db_ref, o_ref):
    u = u_ref[...]
    x = _dot(mf_ref[...], u.astype(BF16))
    h = x.shape[0] // 2
    xr, xi = x[:h], x[h:]
    kr, ki = kf_ref[:h, :], kf_ref[h:, :]
    y = jnp.concatenate([xr * kr - xi * ki, xr * ki + xi * kr], axis=0).astype(BF16)
    o_ref[...] = (_dot(mi_ref[...], y) + u * db_ref[...]) * x0_ref[...]


def _hyena_ctx(k_raw, k_norm1, u, x0, dbias, *, seq):
    mf, mi = _dft_consts_one_level(seq)
    n = 2 * seq
    c = u.shape[1]
    nseq = u.shape[0] // seq
    kf = pl.pallas_call(
        _ctx_filter_body,
        out_shape=jax.ShapeDtypeStruct((2 * n, c), F32),
        compiler_params=_params(),
        name="hy_ctx_filter",
    )(mf.astype(BF16), k_raw, k_norm1)
    full = lambda s: (0, 0)
    return pl.pallas_call(
        _ctx_conv_body,
        grid=(nseq,),
        in_specs=[pl.BlockSpec((2 * n, seq), full),
                  pl.BlockSpec((seq, 2 * n), full),
                  pl.BlockSpec((2 * n, c), full),
                  pl.BlockSpec((seq, c), lambda s: (s, 0)),
                  pl.BlockSpec((seq, c), lambda s: (s, 0)),
                  pl.BlockSpec((1, c), full)],
        out_specs=pl.BlockSpec((seq, c), lambda s: (s, 0)),
        out_shape=jax.ShapeDtypeStruct(u.shape, F32),
        compiler_params=_params("parallel"),
        name="hy_ctx_conv",
    )(mf[:, :seq].astype(BF16), mi.astype(BF16), kf, u, x0, dbias)


def _hyena_lat(k_raw, k_norm1, u, x0, dbias, *, seq, nb):
    c = u.shape[1]
    n1, n2 = 64, 2 * seq // 64
    f1_u, f1_k, mf, mi, gc, gs = _dft_consts_two_level(seq, n1, n2)
    mf = mf.astype(BF16)
    mi = mi.astype(BF16)
    cols = n2 * c
    tn = 4096
    rep = tn // c
    ones_row = jnp.ones((1, tn), F32)
    akr, aki = _lmat(f1_k, k_raw.reshape(1, n1, cols), jnp.tile(k_norm1, (1, rep)), tn=tn)
    kr, ki = _spec_fwd(mf, akr.reshape(n1, n2, c), aki.reshape(n1, n2, c))
    ar, ai = _lmat(f1_u, u.reshape(nb, n1 // 2, cols), ones_row, tn=tn)
    br, bi = _spec_mul(mf, mi, kr, ki, ar.reshape(nb, n1, n2, c), ai.reshape(nb, n1, n2, c))
    y = _idft1(gc, gs, br.reshape(nb, n1, cols), bi.reshape(nb, n1, cols),
               u.reshape(nb, n1 // 2, cols), x0.reshape(nb, n1 // 2, cols),
               jnp.tile(dbias, (1, rep)), tn=tn)
    return y.reshape(nb * seq, c)


def _rope_tables(seq, rope_dims, lane_offsets, pad_rows):
    rows = seq // GRID_W
    rr, cc = np.meshgrid(np.arange(rows), np.arange(GRID_W), indexing='ij')
    pos = (rr.reshape(-1).astype(np.float64), cc.reshape(-1).astype(np.float64))
    half = rope_dims // 2
    q = half // 2
    inv_freq = ROPE_BASE ** (-np.arange(0, half, 2, dtype=np.float64) / half)
    cos_t = np.ones((seq + pad_rows, LANES), np.float64)
    sin_a = np.zeros((seq + pad_rows, LANES), np.float64)
    sin_b = np.zeros((seq + pad_rows, LANES), np.float64)
    for off in lane_offsets:
        for axis in range(2):
            ang = pos[axis][:, None] * inv_freq[None, :]
            base = off + axis * half
            cos_t[:seq, base:base + q] = np.cos(ang)
            cos_t[:seq, base + q:base + half] = np.cos(ang)
            sin_b[:seq, base:base + q] = -np.sin(ang)
            sin_a[:seq, base + q:base + half] = np.sin(ang)
    as32 = lambda a: jnp.asarray(a.astype(np.float32))
    return as32(cos_t), as32(sin_a), as32(sin_b)


def _rope(x, cos_t, sin_a, sin_b, shift):
    return x * cos_t + pltpu.roll(x, shift, 1) * sin_a + pltpu.roll(x, LANES - shift, 1) * sin_b


def _mla_q_body(cq_ref, ckv_ref, qa_ref, kva_ref, wuq_ref, qn_ref, cos_ref, sa_ref, sb_ref,
                q_ref, ckvn_ref, *, scale):
    ckvn_ref[...] = _rms(ckv_ref[...], kva_ref[...])
    cqn = _rms(cq_ref[...], qa_ref[...])
    q = _dot(cqn.astype(BF16), wuq_ref[...])
    cos_t, sin_a, sin_b = cos_ref[...], sa_ref[...], sb_ref[...]
    g = qn_ref[...]
    for h in range(MLA_HEADS):
        qh = _rms(q[:, h * LANES:(h + 1) * LANES], g, MLA_QK)
        qh = _rope(qh, cos_t, sin_a, sin_b, MLA_ROPE // 4)
        q_ref[:, h * LANES:(h + 1) * LANES] = (qh * scale).astype(BF16)


def _pad_heads(w, heads, width):
    k = w.shape[0]
    w3 = w.reshape(k, heads, width)
    return jnp.zeros((k, heads, LANES), w.dtype).at[:, :, :width].set(w3).reshape(k, heads * LANES)


def _mla_q(z, p, tables, *, tm, t_lat, lat_seq):
    t = z.shape[0]
    cos_t, sin_a, sin_b = tables
    wuq = _pad_heads(p['w_uq'], MLA_HEADS, MLA_QK).astype(BF16)
    qn = jnp.zeros((1, LANES), F32).at[0, :MLA_QK].set(p['q_norm'])
    pos_blocks = lat_seq // tm
    lat_tiles = t_lat // tm
    tspec = pl.BlockSpec((tm, LANES), lambda i: (jnp.where(i < lat_tiles, i % pos_blocks, pos_blocks), 0))
    full = lambda i: (0, 0)
    cq_blk = (3 * HY_WIDTH) // MLA_Q_RANK
    ckv_blk = (3 * HY_WIDTH + MLA_Q_RANK) // MLA_KV_RANK
    return pl.pallas_call(
        functools.partial(_mla_q_body, scale=MLA_QK ** -0.5 * LOG2_E),
        grid=(t // tm,),
        in_specs=[pl.BlockSpec((tm, MLA_Q_RANK), lambda i: (i, cq_blk)),
                  pl.BlockSpec((tm, MLA_KV_RANK), lambda i: (i, ckv_blk)),
                  pl.BlockSpec((1, MLA_Q_RANK), full),
                  pl.BlockSpec((1, MLA_KV_RANK), full),
                  pl.BlockSpec((MLA_Q_RANK, MLA_HEADS * LANES), full),
                  pl.BlockSpec((1, LANES), full),
                  tspec, tspec, tspec],
        out_specs=[pl.BlockSpec((tm, MLA_HEADS * LANES), lambda i: (i, 0)),
                   pl.BlockSpec((tm, MLA_KV_RANK), lambda i: (i, 0))],
        out_shape=[jax.ShapeDtypeStruct((t, MLA_HEADS * LANES), BF16),
                   jax.ShapeDtypeStruct((t, MLA_KV_RANK), F32)],
        compiler_params=_params("parallel"),
        name="mla_q",
    )(z, z, p['qa_norm'].reshape(1, -1), p['kva_norm'].reshape(1, -1), wuq, qn, cos_t, sin_a, sin_b)


def _mla_kv_body(ckvn_ref, kr_ref, wk_ref, wv_ref, kn_ref, cos_ref, sa_ref, sb_ref, k_ref, v_ref):
    c = ckvn_ref[...].astype(BF16)
    k = _dot(c, wk_ref[...])
    v_ref[...] = _dot(c, wv_ref[...]).astype(BF16)
    kr = pltpu.roll(kr_ref[...], MLA_NOPE, 1)
    cos_t, sin_a, sin_b = cos_ref[...], sa_ref[...], sb_ref[...]
    g = kn_ref[...]
    for h in range(MLA_HEADS):
        kh = _rms(k[:, h * LANES:(h + 1) * LANES] + kr, g, MLA_QK)
        kh = _rope(kh, cos_t, sin_a, sin_b, MLA_ROPE // 4)
        k_ref[:, h * LANES:(h + 1) * LANES] = kh.astype(BF16)


def _mla_kv(ckvn_rows, kr_rows, p, tables, *, tm, nb, past, lat_seq):
    r = ckvn_rows.shape[0]
    cos_t, sin_a, sin_b = tables
    w = p['w_ukv'].reshape(MLA_KV_RANK, MLA_HEADS, MLA_NOPE + MLA_V)
    wk = _pad_heads(w[:, :, :MLA_NOPE].reshape(MLA_KV_RANK, -1), MLA_HEADS, MLA_NOPE).astype(BF16)
    wv = w[:, :, MLA_NOPE:].reshape(MLA_KV_RANK, MLA_HEADS * MLA_V).astype(BF16)
    kn = jnp.zeros((1, LANES), F32).at[0, :MLA_QK].set(p['k_norm'])
    per_b = (past + lat_seq) // tm
    past_tiles = past // tm
    pos_blocks = lat_seq // tm
    lat_tiles = nb * per_b

    def tmap(i):
        j = i % per_b
        is_pos = jnp.logical_and(i < lat_tiles, j >= past_tiles)
        return (jnp.where(is_pos, j - past_tiles, pos_blocks), 0)

    tspec = pl.BlockSpec((tm, LANES), tmap)
    full = lambda i: (0, 0)
    return pl.pallas_call(
        _mla_kv_body,
        grid=(r // tm,),
        in_specs=[pl.BlockSpec((tm, MLA_KV_RANK), lambda i: (i, 0)),
                  pl.BlockSpec((tm, LANES), lambda i: (i, 0)),
                  pl.BlockSpec((MLA_KV_RANK, MLA_HEADS * LANES), full),
                  pl.BlockSpec((MLA_KV_RANK, MLA_HEADS * MLA_V), full),
                  pl.BlockSpec((1, LANES), full),
                  tspec, tspec, tspec],
        out_specs=[pl.BlockSpec((tm, MLA_HEADS * LANES), lambda i: (i, 0)),
                   pl.BlockSpec((tm, MLA_HEADS * MLA_V), lambda i: (i, 0))],
        out_shape=[jax.ShapeDtypeStruct((r, MLA_HEADS * LANES), BF16),
                   jax.ShapeDtypeStruct((r, MLA_HEADS * MLA_V), BF16)],
        compiler_params=_params("parallel"),
        name="mla_kv",
    )(ckvn_rows, kr_rows, wk, wv, kn, cos_t, sin_a, sin_b)


ATT_CHUNK = 512


def _fill_vaug(vaug_ref, v_refs):
    off = 0
    for v in v_refs:
        n = v.shape[0]
        vaug_ref[off:off + n, :LANES] = v[...]
        off += n
    vaug_ref[:, LANES:] = jnp.ones((vaug_ref.shape[0], LANES), BF16)


def _softmax_pv(q, k_of, vaug_ref, s_ref, n_keys):
    chunk = min(ATT_CHUNK, n_keys)
    m_lane = None
    for c in range(n_keys // chunk):
        rows = slice(c * chunk, (c + 1) * chunk)
        s = _dot_nt(q, k_of(rows))
        s_ref[:, rows] = s
        for j in range(chunk // LANES):
            blk = s[:, j * LANES:(j + 1) * LANES]
            m_lane = blk if m_lane is None else jnp.maximum(m_lane, blk)
    m = jnp.max(m_lane, axis=-1, keepdims=True)
    acc = None
    for c in range(n_keys // chunk):
        rows = slice(c * chunk, (c + 1) * chunk)
        p = jnp.exp2(s_ref[:, rows] - m).astype(BF16)
        d = _dot(p, vaug_ref[rows, :])
        acc = d if acc is None else acc + d
    return acc


def _mla_attn_body(q_ref, k_ref, v_ref, o_ref, vaug_ref, s_ref):
    @pl.when(pl.program_id(2) == 0)
    def _():
        _fill_vaug(vaug_ref, [v_ref])

    n_keys = k_ref.shape[0]
    outs = []
    for hh in range(2):
        sl = slice(hh * LANES, (hh + 1) * LANES)
        r = _softmax_pv(q_ref[:, sl], lambda rows: k_ref[rows, sl], vaug_ref, s_ref, n_keys)
        outs.append(r[:, :LANES] / r[:, LANES:])
    lane = lax.broadcasted_iota(jnp.int32, outs[0].shape, 1)
    o_ref[...] = jnp.where(lane < MLA_V, outs[0], outs[1]).astype(BF16)


def _mla_attn(q, k, v, *, tq, n_seq, seq_q, seq_k, q_row0, k_row0):
    hp = MLA_HEADS // 2
    nq = seq_q // tq
    qb0, kb0 = q_row0 // tq, k_row0 // seq_k
    return pl.pallas_call(
        _mla_attn_body,
        grid=(n_seq, hp, nq),
        in_specs=[pl.BlockSpec((tq, 2 * LANES), lambda s, h, i: (qb0 + s * nq + i, h)),
                  pl.BlockSpec((seq_k, 2 * LANES), lambda s, h, i: (kb0 + s, h)),
                  pl.BlockSpec((seq_k, 2 * MLA_V), lambda s, h, i: (kb0 + s, h))],
        out_specs=pl.BlockSpec((tq, 2 * MLA_V), lambda s, h, i: (s * nq + i, h)),
        out_shape=jax.ShapeDtypeStruct((n_seq * seq_q, MLA_HEADS * MLA_V), BF16),
        scratch_shapes=[pltpu.VMEM((seq_k, 2 * LANES), BF16), pltpu.VMEM((tq, seq_k), F32)],
        compiler_params=_params("arbitrary", "arbitrary", "arbitrary"),
        name="mla_attn",
    )(q, k, v)


def _diff_attn_body(*refs, n_seg, lambda_init):
    q_ref, lam_ref, sub_ref = refs[0], refs[1], refs[2]
    k_refs = refs[3:3 + n_seg]
    v_refs = refs[3 + n_seg:3 + 2 * n_seg]
    o_ref, kcat_ref, vaug_ref, s_ref = refs[3 + 2 * n_seg:]

    @pl.when(pl.program_id(2) == 0)
    def _():
        _fill_vaug(vaug_ref, v_refs)
        off = 0
        for k in k_refs:
            kcat_ref[off:off + k.shape[0], :] = k[...]
            off += k.shape[0]

    lp = lam_ref[...]
    lam = (jnp.exp(jnp.sum(lp[0:1] * lp[1:2], axis=-1, keepdims=True))
           - jnp.exp(jnp.sum(lp[2:3] * lp[3:4], axis=-1, keepdims=True)) + lambda_init)
    q = q_ref[...].astype(F32)
    lane = lax.broadcasted_iota(jnp.int32, q.shape, 1)
    q1 = jnp.where(lane < DIFF_DH, q, 0.0).astype(BF16)
    q2 = jnp.where(lane < DIFF_DH, 0.0, q).astype(BF16)
    n_keys = kcat_ref.shape[0]
    k_of = lambda rows: kcat_ref[rows, :]
    r1 = _softmax_pv(q1, k_of, vaug_ref, s_ref, n_keys)
    r2 = _softmax_pv(q2, k_of, vaug_ref, s_ref, n_keys)
    o = r1[:, :LANES] / r1[:, LANES:] - (lam / r2[:, LANES:]) * r2[:, :LANES]
    o_ref[...] = (_rms(o, sub_ref[...]) * (1.0 - lambda_init)).astype(BF16)


def _diff_attn(q, k_new, v_new, k_cache, v_cache, lam_p, subln, *, tq, n_seq, seq_q, q_row0, lambda_init):
    nq = seq_q // tq
    qb0 = q_row0 // tq
    sb0 = q_row0 // seq_q
    d = 2 * DIFF_DH
    new_spec = pl.BlockSpec((seq_q, d), lambda s, h, i: (sb0 + s, h))
    if k_cache is None:
        n_seg, k_args, v_args, k_specs, v_specs = 1, [k_new], [v_new], [new_spec], [new_spec]
        n_keys = seq_q
    else:
        past = k_cache.shape[2]
        c_spec = pl.BlockSpec((None, None, past, d), lambda s, h, i: (s, h, 0, 0))
        n_seg, k_args, v_args = 2, [k_cache, k_new], [v_cache, v_new]
        k_specs, v_specs = [c_spec, new_spec], [c_spec, new_spec]
        n_keys = past + seq_q
    return pl.pallas_call(
        functools.partial(_diff_attn_body, n_seg=n_seg, lambda_init=lambda_init),
        grid=(n_seq, DIFF_HEADS, nq),
        in_specs=[pl.BlockSpec((tq, d), lambda s, h, i: (qb0 + s * nq + i, h)),
                  pl.BlockSpec((4, DIFF_DH), lambda s, h, i: (0, 0)),
                  pl.BlockSpec((1, d), lambda s, h, i: (0, 0))] + k_specs + v_specs,
        out_specs=pl.BlockSpec((tq, d), lambda s, h, i: (s * nq + i, h)),
        out_shape=jax.ShapeDtypeStruct((n_seq * seq_q, DIFF_HEADS * d), BF16),
        scratch_shapes=[pltpu.VMEM((n_keys, d), BF16), pltpu.VMEM((n_keys, 2 * LANES), BF16),
                        pltpu.VMEM((tq, n_keys), F32)],
        compiler_params=_params("arbitrary", "arbitrary", "arbitrary"),
        name="diff_attn",
    )(q, lam_p, subln.reshape(1, d), *k_args, *v_args)


def _proj_res_body(*refs, n_in, gate_row):
    x_ref, mod_ref, o_ref = refs[0], refs[1], refs[-1]
    acc = None
    for j in range(n_in):
        a_ref, w_ref = refs[2 + 2 * j], refs[3 + 2 * j]
        d = _dot(a_ref[...].astype(BF16), w_ref[...])
        acc = d if acc is None else acc + d
    o_ref[...] = x_ref[...] + mod_ref[gate_row:gate_row + 1, :] * acc


def _proj_res(x, mod, acts, ws, *, tm, seg_len, gate_row):
    t, d = x.shape
    in_specs = [pl.BlockSpec((tm, d), lambda i: (i, 0)),
                pl.BlockSpec((None, 6, d), lambda i: (i * tm // seg_len, 0, 0))]
    args = [x, mod]
    for a, w in zip(acts, ws):
        in_specs += [pl.BlockSpec((tm, a.shape[1]), lambda i: (i, 0)),
                     pl.BlockSpec(w.shape, lambda i: (0, 0))]
        args += [a, w]
    return pl.pallas_call(
        functools.partial(_proj_res_body, n_in=len(acts), gate_row=gate_row),
        grid=(t // tm,),
        in_specs=in_specs,
        out_specs=pl.BlockSpec((tm, d), lambda i: (i, 0)),
        out_shape=jax.ShapeDtypeStruct((t, d), F32),
        compiler_params=_params("parallel"),
        name="proj_res",
    )(*args)


def _ffn_body(x_ref, mod_ref, g_ref, wg_ref, wu_ref, wd_ref, o_ref, h_ref, acc_ref):
    f = pl.program_id(1)

    @pl.when(f == 0)
    def _():
        h_ref[...] = _norm_mod(x_ref[...], g_ref[...], mod_ref[3:4, :], mod_ref[4:5, :]).astype(BF16)
        acc_ref[...] = jnp.zeros_like(acc_ref)

    h = h_ref[...]
    a = _silu(_dot(h, wg_ref[...])) * _dot(h, wu_ref[...])
    acc_ref[...] += _dot(a.astype(BF16), wd_ref[...])

    @pl.when(f == pl.num_programs(1) - 1)
    def _():
        o_ref[...] = x_ref[...] + mod_ref[5:6, :] * acc_ref[...]


def _ffn(x, mod, g, wg, wu, wd, *, tm, tf, seg_len):
    t, d = x.shape
    ff = wg.shape[1]
    return pl.pallas_call(
        _ffn_body,
        grid=(t // tm, ff // tf),
        in_specs=[pl.BlockSpec((tm, d), lambda i, f: (i, 0)),
                  pl.BlockSpec((None, 6, d), lambda i, f: (i * tm // seg_len, 0, 0)),
                  pl.BlockSpec((1, d), lambda i, f: (0, 0)),
                  pl.BlockSpec((d, tf), lambda i, f: (0, f)),
                  pl.BlockSpec((d, tf), lambda i, f: (0, f)),
                  pl.BlockSpec((tf, d), lambda i, f: (f, 0))],
        out_specs=pl.BlockSpec((tm, d), lambda i, f: (i, 0)),
        out_shape=jax.ShapeDtypeStruct((t, d), F32),
        scratch_shapes=[pltpu.VMEM((tm, d), BF16), pltpu.VMEM((tm, d), F32)],
        compiler_params=_params("parallel", "arbitrary"),
        name="ffn",
    )(x, mod, g.reshape(1, d), wg, wu, wd)


def _group_ms(x, gmat):
    hi, lo = _split_bf16(x * x)
    return (_dot(hi, gmat) + _dot(lo, gmat)) * (1.0 / DIFF_DH)


def _qkv_body(x_ref, mod_ref, g_ref, w_ref, gm_ref, qn_ref, kn_ref, cos_ref, sa_ref, sb_ref,
              q_ref, k_ref, v_ref, kf_ref, vf_ref, *, scale):
    h = _norm_mod(x_ref[...], g_ref[...], mod_ref[0:1, :], mod_ref[1:2, :]).astype(BF16)
    z = _dot(h, w_ref[...])
    hd = DIFF_HEADS * 2 * DIFF_DH
    cos_t, sin_a, sin_b = cos_ref[...], sa_ref[...], sb_ref[...]
    gm = gm_ref[...]
    for hh in range(DIFF_HEADS):
        sl = slice(hh * LANES, (hh + 1) * LANES)
        qh = z[:, hh * LANES:(hh + 1) * LANES]
        qh = qh * lax.rsqrt(_group_ms(qh, gm) + NORM_EPS) * qn_ref[...]
        q_ref[:, sl] = (_rope(qh, cos_t, sin_a, sin_b, DIFF_DH // 4) * scale).astype(BF16)
        kh = z[:, hd + hh * LANES:hd + (hh + 1) * LANES]
        kh = kh * lax.rsqrt(_group_ms(kh, gm) + NORM_EPS) * kn_ref[...]
        kf_ref[:, sl] = kh
        k_ref[:, sl] = _rope(kh, cos_t, sin_a, sin_b, DIFF_DH // 4).astype(BF16)
    v = z[:, 2 * hd:]
    vf_ref[...] = v
    v_ref[...] = v.astype(BF16)


def _qkv(x, mod, p, tables, *, tm, seg_len, t_lat, lat_seq):
    t, d = x.shape
    hd = DIFF_HEADS * 2 * DIFF_DH
    cos_t, sin_a, sin_b = tables
    wqk = p['w_qkv'][:, :2 * hd].reshape(d, 2, 2, DIFF_HEADS, DIFF_DH)
    wqk = wqk.transpose(0, 1, 3, 2, 4).reshape(d, 2 * hd)
    w = jnp.concatenate([wqk, p['w_qkv'][:, 2 * hd:]], axis=1).astype(BF16)
    gi = np.arange(LANES) // DIFF_DH
    gmat = jnp.asarray((gi[:, None] == gi[None, :]).astype(np.float32)).astype(BF16)
    qn = jnp.tile(p['q_norm'], 2).reshape(1, LANES)
    kn = jnp.tile(p['k_norm'], 2).reshape(1, LANES)
    pos_blocks = lat_seq // tm
    lat_tiles = t_lat // tm
    ctx_tiles = (t - t_lat) // tm
    tspec = pl.BlockSpec((tm, LANES), lambda i: (jnp.where(i < lat_tiles, i % pos_blocks, pos_blocks), 0))
    full = lambda i: (0, 0)
    row = pl.BlockSpec((tm, hd), lambda i: (i, 0))
    fspec = pl.BlockSpec((tm, hd), lambda i: (jnp.where(i < lat_tiles, ctx_tiles, i - lat_tiles), 0))
    fshape = jax.ShapeDtypeStruct((t - t_lat + tm, hd), F32)
    return pl.pallas_call(
        functools.partial(_qkv_body, scale=DIFF_DH ** -0.5 * LOG2_E),
        grid=(t // tm,),
        in_specs=[pl.BlockSpec((tm, d), lambda i: (i, 0)),
                  pl.BlockSpec((None, 6, d), lambda i: (i * tm // seg_len, 0, 0)),
                  pl.BlockSpec((1, d), full),
                  pl.BlockSpec((d, 3 * hd), full),
                  pl.BlockSpec((LANES, LANES), full),
                  pl.BlockSpec((1, LANES), full), pl.BlockSpec((1, LANES), full),
                  tspec, tspec, tspec],
        out_specs=[row, row, row, fspec, fspec],
        out_shape=[jax.ShapeDtypeStruct((t, hd), BF16)] * 3 + [fshape, fshape],
        compiler_params=_params("arbitrary"),
        name="qkv",
    )(x, mod, p['norm1'].reshape(1, d), w, gmat, qn, kn, cos_t, sin_a, sin_b)


def _route(logits):
    lane = lax.broadcasted_iota(jnp.int32, logits.shape, 1)
    neg = jnp.float32(-jnp.inf)
    lg = jnp.where(lane < N_EXPERTS, logits, neg)
    m1 = jnp.max(lg, axis=-1, keepdims=True)
    i1 = jnp.min(jnp.where(lg == m1, lane, LANES), axis=-1, keepdims=True)
    lg2 = jnp.where(lane == i1, neg, lg)
    m2 = jnp.max(lg2, axis=-1, keepdims=True)
    i2 = jnp.min(jnp.where(lg2 == m2, lane, LANES), axis=-1, keepdims=True)
    e = jnp.exp(m2 - m1)
    w1 = 1.0 / (1.0 + e)
    w2 = e / (1.0 + e)
    return jnp.where(lane == i1, w1, 0.0) + jnp.where(lane == i2, w2, 0.0)


MOE_BLOCK = 1024
MOE_SUB = 256
MOE_ROUTE_TM = 512


def _moe_route_body(x_ref, mod_ref, g_ref, wr_ref, h_ref, gates_ref, rank_ref, rank_t_ref,
                    carry_row, carry_col, *, tm):
    i = pl.program_id(0)

    @pl.when(i == 0)
    def _():
        carry_row[...] = jnp.zeros_like(carry_row)
        carry_col[...] = jnp.zeros_like(carry_col)

    h = _norm_mod(x_ref[...], g_ref[...], mod_ref[3:4, :], mod_ref[4:5, :])
    h_ref[...] = h.astype(BF16)
    gates = _route(_dot_f32(h, wr_ref[...]))
    gates_ref[...] = gates
    sel = jnp.where(gates != 0.0, 1.0, 0.0)
    sel_t = sel.T
    r = lax.broadcasted_iota(jnp.int32, (tm, tm), 0)
    c = lax.broadcasted_iota(jnp.int32, (tm, tm), 1)
    lower = jnp.where(c < r, 1.0, 0.0).astype(BF16)
    upper = jnp.where(r < c, 1.0, 0.0).astype(BF16)
    before = _dot(lower, sel.astype(BF16)) + carry_row[...]
    before_t = _dot(sel_t.astype(BF16), upper) + carry_col[...]
    rank_ref[...] = jnp.where(sel > 0.0, before, -1.0)
    rank_t = jnp.where(sel_t > 0.0, before_t, -1.0)
    for s in range(tm // MOE_SUB):
        rank_t_ref[s] = rank_t[:SUBLANES, s * MOE_SUB:(s + 1) * MOE_SUB]
    carry_row[...] += jnp.sum(sel, axis=0, keepdims=True)
    carry_col[...] += jnp.sum(sel_t, axis=1, keepdims=True)


def _moe_route(x, mod, g, w_router, *, seg_len):
    t, d = x.shape
    tm = MOE_ROUTE_TM
    ne = w_router.shape[1]
    assert ne <= SUBLANES
    wr = jnp.zeros((d, LANES), F32).at[:, :ne].set(w_router)
    sub = tm // MOE_SUB
    return pl.pallas_call(
        functools.partial(_moe_route_body, tm=tm),
        grid=(t // tm,),
        in_specs=[pl.BlockSpec((tm, d), lambda i: (i, 0)),
                  pl.BlockSpec((None, 6, d), lambda i: (i * tm // seg_len, 0, 0)),
                  pl.BlockSpec((1, d), lambda i: (0, 0)),
                  pl.BlockSpec((d, LANES), lambda i: (0, 0))],
        out_specs=[pl.BlockSpec((tm, d), lambda i: (i, 0)),
                   pl.BlockSpec((tm, LANES), lambda i: (i, 0)),
                   pl.BlockSpec((tm, LANES), lambda i: (i, 0)),
                   pl.BlockSpec((sub, SUBLANES, MOE_SUB), lambda i: (i, 0, 0))],
        out_shape=[jax.ShapeDtypeStruct((t, d), BF16),
                   jax.ShapeDtypeStruct((t, LANES), F32),
                   jax.ShapeDtypeStruct((t, LANES), F32),
                   jax.ShapeDtypeStruct((t // MOE_SUB, SUBLANES, MOE_SUB), F32)],
        scratch_shapes=[pltpu.VMEM((1, LANES), F32), pltpu.VMEM((LANES, 1), F32)],
        compiler_params=_params("arbitrary"),
        name="moe_route",
    )(x, mod, g.reshape(1, d), wr)


def _moe_plan(rank, ne, *, n_blocks):
    t = rank.shape[0]
    n_tiles = t // MOE_SUB
    per_blk = MOE_BLOCK // MOE_SUB
    n_sub = n_blocks * per_blk
    sel = (rank[:, :ne] >= 0.0).astype(jnp.int32)
    tile_cnt = sel.reshape(n_tiles, MOE_SUB, ne).sum(axis=1)
    tile_end = jnp.cumsum(tile_cnt, axis=0)
    tile_start = tile_end - tile_cnt
    cnt = tile_end[-1]
    nblk = (cnt + MOE_BLOCK - 1) // MOE_BLOCK
    bend = jnp.cumsum(nblk)
    bstart = bend - nblk
    e_last = jnp.max(jnp.where(cnt > 0, jnp.arange(ne), 0))
    b = jnp.arange(n_blocks)
    blk_valid = b < bend[-1]
    blk_e = jnp.minimum(jnp.sum(bend[None, :] <= b[:, None], axis=1), e_last).astype(jnp.int32)
    blk_r0 = (b - bstart[blk_e]) * MOE_BLOCK
    blk_rows = jnp.where(blk_valid, jnp.clip(cnt[blk_e] - blk_r0, 0, MOE_BLOCK), 0).astype(jnp.int32)
    j = jnp.arange(n_sub)
    sub_e = blk_e[j // per_blk]
    sub_r0 = blk_r0[j // per_blk] + (j % per_blk) * MOE_SUB
    sub_valid = jnp.logical_and(blk_valid[j // per_blk], sub_r0 < cnt[sub_e])
    ends = tile_end[:, sub_e]
    r1 = jnp.minimum(sub_r0 + MOE_SUB, cnt[sub_e])
    c_lo = jnp.sum(ends <= sub_r0[None, :], axis=0)
    c_hi = jnp.sum(ends < r1[None, :], axis=0)
    c_lo = jnp.where(sub_valid, c_lo, 1).astype(jnp.int32)
    c_hi = jnp.where(sub_valid, jnp.minimum(c_hi, n_tiles - 1), 0).astype(jnp.int32)
    base = (bstart * MOE_BLOCK).astype(jnp.int32)
    j0 = jnp.minimum((base[None, :] + tile_start) // MOE_SUB, n_sub - 2).astype(jnp.int32)
    return dict(blk_e=blk_e, blk_valid=blk_valid.astype(jnp.int32), blk_rows=blk_rows,
                sub_e=sub_e.astype(jnp.int32), sub_r0=sub_r0.astype(jnp.int32), c_lo=c_lo, c_hi=c_hi,
                base=base, j0=j0.reshape(-1))


def _moe_dispatch_body(e_ref, r0_ref, lo_ref, hi_ref, h_ref, rank_t_ref, xs_ref, acc_ref):
    j = pl.program_id(0)
    e = e_ref[j]
    rows = (r0_ref[j] + lax.broadcasted_iota(jnp.int32, (MOE_SUB, 1), 0)).astype(F32)
    sub = lax.broadcasted_iota(jnp.int32, (SUBLANES, MOE_SUB), 0)
    acc_ref[...] = jnp.zeros_like(acc_ref)

    def step(c, carry):
        rk = jnp.sum(jnp.where(sub == e, rank_t_ref[c], 0.0), axis=0, keepdims=True)
        onehot = jnp.where(rk == rows, 1.0, 0.0).astype(BF16)
        off = pl.multiple_of(c * MOE_SUB, MOE_SUB)
        acc_ref[...] += _dot(onehot, h_ref[pl.ds(off, MOE_SUB), :])
        return carry

    lax.fori_loop(lo_ref[j], hi_ref[j] + 1, step, 0)
    xs_ref[...] = acc_ref[...].astype(BF16)


def _moe_dispatch(h, rank_t, plan, *, n_sub):
    t, d = h.shape
    grid_spec = pltpu.PrefetchScalarGridSpec(
        num_scalar_prefetch=4,
        grid=(n_sub,),
        in_specs=[pl.BlockSpec((t, d), lambda j, *_: (0, 0), pipeline_mode=pl.Buffered(1)),
                  pl.BlockSpec(rank_t.shape, lambda j, *_: (0, 0, 0), pipeline_mode=pl.Buffered(1))],
        out_specs=pl.BlockSpec((MOE_SUB, d), lambda j, *_: (j, 0)),
        scratch_shapes=[pltpu.VMEM((MOE_SUB, d), F32)],
    )
    return pl.pallas_call(
        _moe_dispatch_body,
        grid_spec=grid_spec,
        out_shape=jax.ShapeDtypeStruct((n_sub * MOE_SUB, d), BF16),
        compiler_params=_params("arbitrary"),
        name="moe_dispatch",
    )(plan['sub_e'], plan['sub_r0'], plan['c_lo'], plan['c_hi'], h, rank_t)


def _moe_ffn_body(e_ref, valid_ref, rows_ref, xs_ref, wg_ref, wu_ref, wd_ref, y_ref, acc_ref):
    b = pl.program_id(0)
    f = pl.program_id(1)
    n_rows = rows_ref[b]
    last = f == pl.num_programs(1) - 1
    wg = wg_ref[...].astype(BF16)
    wu = wu_ref[...].astype(BF16)
    wd = wd_ref[...].astype(BF16)
    for s in range(MOE_BLOCK // MOE_SUB):
        sl = slice(s * MOE_SUB, (s + 1) * MOE_SUB)
        live = s * MOE_SUB < n_rows

        @pl.when(jnp.logical_and(live, f == 0))
        def _():
            acc_ref[sl, :] = jnp.zeros((MOE_SUB, acc_ref.shape[1]), F32)

        @pl.when(live)
        def _():
            h = xs_ref[sl, :]
            a = _silu(_dot(h, wg)) * _dot(h, wu)
            acc_ref[sl, :] += _dot(a.astype(BF16), wd)

        @pl.when(jnp.logical_and(live, last))
        def _():
            y_ref[sl, :] = acc_ref[sl, :].astype(BF16)

        @pl.when(jnp.logical_and(jnp.logical_not(live), last))
        def _():
            y_ref[sl, :] = jnp.zeros((MOE_SUB, y_ref.shape[1]), BF16)


def _moe_ffn(xs, wg, wu, wd, plan, *, n_blocks, tf):
    _, d = xs.shape
    ne, _, ff = wg.shape
    nf = ff // tf

    def w_in(b, f, e_ref, valid_ref, rows_ref):
        return (e_ref[b], 0, jnp.where(valid_ref[b] > 0, f, nf - 1))

    def w_down(b, f, e_ref, valid_ref, rows_ref):
        return (e_ref[b], jnp.where(valid_ref[b] > 0, f, nf - 1), 0)

    grid_spec = pltpu.PrefetchScalarGridSpec(
        num_scalar_prefetch=3,
        grid=(n_blocks, nf),
        in_specs=[pl.BlockSpec((MOE_BLOCK, d), lambda b, f, *_: (b, 0)),
                  pl.BlockSpec((None, d, tf), w_in),
                  pl.BlockSpec((None, d, tf), w_in),
                  pl.BlockSpec((None, tf, d), w_down)],
        out_specs=pl.BlockSpec((MOE_BLOCK, d), lambda b, f, *_: (b, 0)),
        scratch_shapes=[pltpu.VMEM((MOE_BLOCK, d), F32)],
    )
    return pl.pallas_call(
        _moe_ffn_body,
        grid_spec=grid_spec,
        out_shape=jax.ShapeDtypeStruct((n_blocks * MOE_BLOCK, d), BF16),
        compiler_params=_params("arbitrary", "arbitrary"),
        name="moe_ffn",
    )(plan['blk_e'], plan['blk_valid'], plan['blk_rows'], xs, wg, wu, wd)


def _moe_combine_body(j0_ref, base_ref, x_ref, mod_ref, gates_ref, rank_ref, ya_ref, yb_ref, o_ref, acc_ref,
                      *, ne):
    c = pl.program_id(0)
    e = pl.program_id(1)

    @pl.when(e == 0)
    def _():
        acc_ref[...] = jnp.zeros_like(acc_ref)

    lane = lax.broadcasted_iota(jnp.int32, gates_ref.shape, 1)
    pick = lane == e
    g = jnp.sum(jnp.where(pick, gates_ref[...], 0.0), axis=-1, keepdims=True)
    rk = jnp.sum(jnp.where(pick, rank_ref[...], 0.0), axis=-1, keepdims=True)
    shift = (base_ref[e] - j0_ref[c * ne + e] * MOE_SUB).astype(F32)
    loc = jnp.where(rk >= 0.0, rk + shift, -1.0)
    col = lax.broadcasted_iota(jnp.int32, (1, MOE_SUB), 1).astype(F32)
    qa = jnp.where(loc == col, 1.0, 0.0).astype(BF16)
    qb = jnp.where(loc == col + float(MOE_SUB), 1.0, 0.0).astype(BF16)
    acc_ref[...] += g * (_dot(qa, ya_ref[...]) + _dot(qb, yb_ref[...]))

    @pl.when(e == ne - 1)
    def _():
        o_ref[...] = x_ref[...] + mod_ref[5:6, :] * acc_ref[...]


def _moe_combine(x, mod, gates, rank, y, plan, *, ne, seg_len):
    t, d = x.shape
    tm = MOE_SUB
    grid_spec = pltpu.PrefetchScalarGridSpec(
        num_scalar_prefetch=2,
        grid=(t // tm, ne),
        in_specs=[pl.BlockSpec((tm, d), lambda c, e, *_: (c, 0)),
                  pl.BlockSpec((None, 6, d), lambda c, e, *_: (c * tm // seg_len, 0, 0)),
                  pl.BlockSpec((tm, LANES), lambda c, e, *_: (c, 0)),
                  pl.BlockSpec((tm, LANES), lambda c, e, *_: (c, 0)),
                  pl.BlockSpec((MOE_SUB, d), lambda c, e, j0, base: (j0[c * ne + e], 0)),
                  pl.BlockSpec((MOE_SUB, d), lambda c, e, j0, base: (j0[c * ne + e] + 1, 0))],
        out_specs=pl.BlockSpec((tm, d), lambda c, e, *_: (c, 0)),
        scratch_shapes=[pltpu.VMEM((tm, d), F32)],
    )
    return pl.pallas_call(
        functools.partial(_moe_combine_body, ne=ne),
        grid_spec=grid_spec,
        out_shape=jax.ShapeDtypeStruct((t, d), F32),
        compiler_params=_params("parallel", "arbitrary"),
        name="moe_combine",
    )(plan['j0'], plan['base'], x, mod, gates, rank, y, y)


def _moe(x, mod, g, w_router, wg, wu, wd, *, seg_len, top_k=2):
    t, d = x.shape
    ne = w_router.shape[1]
    n_blocks = t * top_k // MOE_BLOCK + ne
    h, gates, rank, rank_t = _moe_route(x, mod, g, w_router, seg_len=seg_len)
    plan = _moe_plan(rank, ne, n_blocks=n_blocks)
    xs = _moe_dispatch(h, rank_t, plan, n_sub=n_blocks * (MOE_BLOCK // MOE_SUB))
    y = _moe_ffn(xs, wg, wu, wd, plan, n_blocks=n_blocks, tf=512)
    return _moe_combine(x, mod, gates, rank, y, plan, ne=ne, seg_len=seg_len)


def _even_layer(x, cond8, p, cache_ckv, cache_kr, *, nb, lat_seq, n_ctx, ctx_seq):
    t, d = x.shape
    t_lat = nb * lat_seq
    seg_len = lat_seq
    past = cache_ckv.shape[1]
    mod = _adaln(cond8, p['w_mod'], p['b_mod'])

    n_in = p['w_in'].shape[1]
    n_pad = -(-n_in // LANES) * LANES
    w_in = jnp.zeros((d, n_pad), BF16).at[:, :n_in].set(p['w_in'].astype(BF16))
    z = _lin_in(x, mod, p['norm1'], w_in, tm=512, seg_len=seg_len)

    u, x0 = _hy_pre(z, p['hy_conv_w'], p['hy_conv_b'], t_lat=t_lat, lat_seq=lat_seq, ctx_seq=ctx_seq)
    k_raw, k_sum = _hyena_filters(p, lat_seq=lat_seq, ctx_seq=ctx_seq)
    dbias = p['hy_dbias'].reshape(1, HY_WIDTH)
    y_lat = _hyena_lat(k_raw[:2 * lat_seq], k_sum[0:1], u[:t_lat], x0[:t_lat], dbias, seq=lat_seq, nb=nb)
    y_ctx = _hyena_ctx(k_raw[2 * lat_seq:], k_sum[1:2], u[t_lat:], x0[t_lat:], dbias, seq=ctx_seq)
    y_hy = jnp.concatenate([y_lat, y_ctx], axis=0)

    tm = 512
    tables = _rope_tables(lat_seq, MLA_ROPE, (MLA_NOPE,), tm)
    q, ckvn = _mla_q(z, p, tables, tm=tm, t_lat=t_lat, lat_seq=lat_seq)
    kr_col = 3 * HY_WIDTH + MLA_Q_RANK + MLA_KV_RANK
    kr = z[:, kr_col:kr_col + LANES]
    cache_kr_p = jnp.zeros((nb, past, LANES), F32).at[:, :, :MLA_ROPE].set(cache_kr)
    ckvn_rows = jnp.concatenate(
        [jnp.concatenate([cache_ckv, ckvn[:t_lat].reshape(nb, lat_seq, -1)], axis=1).reshape(nb * (past + lat_seq), -1),
         ckvn[t_lat:]], axis=0)
    kr_rows = jnp.concatenate(
        [jnp.concatenate([cache_kr_p, kr[:t_lat].reshape(nb, lat_seq, LANES)], axis=1).reshape(nb * (past + lat_seq), LANES),
         kr[t_lat:]], axis=0)
    k_all, v_all = _mla_kv(ckvn_rows, kr_rows, p, tables, tm=tm, nb=nb, past=past, lat_seq=lat_seq)
    o_lat = _mla_attn(q, k_all, v_all, tq=256, n_seq=nb, seq_q=lat_seq, seq_k=past + lat_seq,
                      q_row0=0, k_row0=0)
    o_ctx = _mla_attn(q, k_all, v_all, tq=ctx_seq, n_seq=n_ctx, seq_q=ctx_seq, seq_k=ctx_seq,
                      q_row0=t_lat, k_row0=nb * (past + lat_seq))
    o = jnp.concatenate([o_lat, o_ctx], axis=0)

    w_out = p['w_out'].astype(BF16)
    x = _proj_res(x, mod, [y_hy, o], [w_out[:HY_WIDTH], w_out[HY_WIDTH:]], tm=512, seg_len=seg_len, gate_row=2)
    x = _ffn(x, mod, p['norm2'], p['ffn_w_gate'].astype(BF16), p['ffn_w_up'].astype(BF16),
             p['ffn_w_down'].astype(BF16), tm=512, tf=1408, seg_len=seg_len)
    new_ckv = ckvn[t_lat:].reshape(n_ctx, ctx_seq, -1)
    new_kr = kr[t_lat:, :MLA_ROPE].reshape(n_ctx, ctx_seq, MLA_ROPE)
    return x, new_ckv, new_kr


def _odd_layer(x, cond8, p, cache_k, cache_v, lambda_init, *, nb, lat_seq, n_ctx, ctx_seq):
    t, d = x.shape
    t_lat = nb * lat_seq
    seg_len = lat_seq
    mod = _adaln(cond8, p['w_mod'], p['b_mod'])
    tm = 512
    tables = _rope_tables(lat_seq, DIFF_DH, (0, DIFF_DH), tm)
    q, k, v, kf, vf = _qkv(x, mod, p, tables, tm=tm, seg_len=seg_len, t_lat=t_lat, lat_seq=lat_seq)
    lam_p = jnp.stack([p['lam_q1'], p['lam_k1'], p['lam_q2'], p['lam_k2']])
    o_lat = _diff_attn(q, k, v, cache_k.astype(BF16), cache_v.astype(BF16), lam_p, p['subln'],
                       tq=256, n_seq=nb, seq_q=lat_seq, q_row0=0, lambda_init=lambda_init)
    o_ctx = _diff_attn(q, k, v, None, None, lam_p, p['subln'],
                       tq=ctx_seq, n_seq=n_ctx, seq_q=ctx_seq, q_row0=t_lat, lambda_init=lambda_init)
    o = jnp.concatenate([o_lat, o_ctx], axis=0)
    x = _proj_res(x, mod, [o], [p['w_out'].astype(BF16)], tm=512, seg_len=seg_len, gate_row=2)
    x = _moe(x, mod, p['norm2'], p['w_router'], p['moe_w_gate'], p['moe_w_up'], p['moe_w_down'],
             seg_len=seg_len)
    hd = 2 * DIFF_DH
    t_ctx = t - t_lat
    new_k = kf[:t_ctx].reshape(n_ctx, ctx_seq, DIFF_HEADS, hd).transpose(0, 2, 1, 3)
    new_v = vf[:t_ctx].reshape(n_ctx, ctx_seq, DIFF_HEADS, hd).transpose(0, 2, 1, 3)
    return x, new_k, new_v


def kernel(x_prompt, x_sample, cache_l0_ckv, cache_l0_krope, cache_l1_k, cache_l1_v, c, c_ctx,
           l0_w_mod, l0_b_mod, l0_norm1, l0_norm2, l0_w_in, l0_hy_conv_w, l0_hy_conv_b,
           l0_hy_fw1, l0_hy_fb1, l0_hy_freq1, l0_hy_fw2, l0_hy_fb2, l0_hy_freq2, l0_hy_fw3, l0_hy_dbias,
           l0_mla_qa_norm, l0_mla_w_uq, l0_mla_kva_norm, l0_mla_w_ukv, l0_mla_q_norm, l0_mla_k_norm,
           l0_w_out, l0_ffn_w_gate, l0_ffn_w_up, l0_ffn_w_down,
           l1_w_mod, l1_b_mod, l1_norm1, l1_norm2, l1_w_qkv, l1_q_norm, l1_k_norm,
           l1_lam_q1, l1_lam_k1, l1_lam_q2, l1_lam_k2, l1_subln, l1_w_out,
           l1_w_router, l1_moe_w_gate, l1_moe_w_up, l1_moe_w_down):
    even = {
        'w_mod': l0_w_mod, 'b_mod': l0_b_mod, 'norm1': l0_norm1, 'norm2': l0_norm2, 'w_in': l0_w_in,
        'hy_conv_w': l0_hy_conv_w, 'hy_conv_b': l0_hy_conv_b, 'hy_fw1': l0_hy_fw1, 'hy_fb1': l0_hy_fb1,
        'hy_freq1': l0_hy_freq1, 'hy_fw2': l0_hy_fw2, 'hy_fb2': l0_hy_fb2, 'hy_freq2': l0_hy_freq2,
        'hy_fw3': l0_hy_fw3, 'hy_dbias': l0_hy_dbias, 'qa_norm': l0_mla_qa_norm, 'w_uq': l0_mla_w_uq,
        'kva_norm': l0_mla_kva_norm, 'w_ukv': l0_mla_w_ukv, 'q_norm': l0_mla_q_norm, 'k_norm': l0_mla_k_norm,
        'w_out': l0_w_out, 'ffn_w_gate': l0_ffn_w_gate, 'ffn_w_up': l0_ffn_w_up, 'ffn_w_down': l0_ffn_w_down,
    }
    odd = {
        'w_mod': l1_w_mod, 'b_mod': l1_b_mod, 'norm1': l1_norm1, 'norm2': l1_norm2, 'w_qkv': l1_w_qkv,
        'q_norm': l1_q_norm, 'k_norm': l1_k_norm, 'lam_q1': l1_lam_q1, 'lam_k1': l1_lam_k1,
        'lam_q2': l1_lam_q2, 'lam_k2': l1_lam_k2, 'subln': l1_subln, 'w_out': l1_w_out,
        'w_router': l1_w_router, 'moe_w_gate': l1_moe_w_gate, 'moe_w_up': l1_moe_w_up,
        'moe_w_down': l1_moe_w_down,
    }
    n_ctx, ctx_seq, d = x_prompt.shape
    nb, lat_seq, _ = x_sample.shape
    assert n_ctx * ctx_seq == lat_seq, "segment layout needs equally sized modulation segments"
    dims = dict(nb=nb, lat_seq=lat_seq, n_ctx=n_ctx, ctx_seq=ctx_seq)
    t_lat = nb * lat_seq
    x = jnp.concatenate([x_sample.reshape(t_lat, d), x_prompt.reshape(n_ctx * ctx_seq, d)], axis=0)
    cond8 = jnp.zeros((SUBLANES, d), F32).at[:nb].set(c).at[nb].set(c_ctx)

    x, new_l0_ckv, new_l0_krope = _even_layer(x, cond8, even, cache_l0_ckv, cache_l0_krope, **dims)
    lambda_init = 0.8 - 0.6 * math.exp(-0.3 * 1)
    x, new_l1_k, new_l1_v = _odd_layer(x, cond8, odd, cache_l1_k, cache_l1_v, lambda_init, **dims)

    y_sample = x[:t_lat].reshape(nb, lat_seq, d)
    y_prompt = x[t_lat:].reshape(n_ctx, ctx_seq, d)
    return (y_prompt, y_sample, new_l0_ckv, new_l0_krope, new_l1_k, new_l1_v)
```

```python
import functools
import math

import numpy as np
import jax
import jax.numpy as jnp
from jax import lax
from jax.experimental import pallas as pl
from jax.experimental.pallas import tpu as pltpu

F32 = jnp.float32
BF16 = jnp.bfloat16

VMEM_LIMIT_BYTES = 56 * 1024 * 1024
LANES = 128
SUBLANES = 8
LOG2_E = math.log2(math.e)

GRID_W = 64
ROPE_BASE = 10000.0
NORM_EPS = 1e-6
HY_WIDTH = 512
HY_BANDS = 16
HY_FAST_DECAY_PCT = 0.3
HY_SLOW_DECAY_PCT = 1.5
HY_DECAY_TARGET = 1e-2
MLA_HEADS = 8
MLA_NOPE = 64
MLA_ROPE = 32
MLA_QK = MLA_NOPE + MLA_ROPE
MLA_V = 64
MLA_Q_RANK = 768
MLA_KV_RANK = 256
DIFF_HEADS = 8
DIFF_DH = 64
N_EXPERTS = 8


def _params(*sem):
    return pltpu.CompilerParams(dimension_semantics=sem, vmem_limit_bytes=VMEM_LIMIT_BYTES)


def _dot(a, b):
    return jnp.dot(a, b, preferred_element_type=F32)


def _dot_nt(a, b):
    return lax.dot_general(a, b, (((1,), (1,)), ((), ())), preferred_element_type=F32)


def _split_bf16(a):
    hi = a.astype(BF16)
    lo = (a - hi.astype(F32)).astype(BF16)
    return hi, lo


def _dot_f32(a, b):
    ah, al = _split_bf16(a)
    bh, bl = _split_bf16(b)
    return _dot(ah, bh) + (_dot(al, bh) + _dot(ah, bl))


def _rms(x, g, n=None):
    n = x.shape[-1] if n is None else n
    ms = jnp.sum(x * x, axis=-1, keepdims=True) * (1.0 / n)
    return x * lax.rsqrt(ms + NORM_EPS) * g


def _norm_mod(x, g, shift, scale):
    return _rms(x, g) * (1.0 + scale) + shift


def _silu(x):
    return x / (1.0 + jnp.exp(-x))


def _adaln_body(c_ref, w_ref, b_ref, o_ref):
    o_ref[...] = _dot_f32(_silu(c_ref[...]), w_ref[...]) + b_ref[...]


def _adaln(cond8, w_mod, b_mod):
    d, n = w_mod.shape
    tn = n // 4
    out = pl.pallas_call(
        _adaln_body,
        grid=(n // tn,),
        in_specs=[pl.BlockSpec((SUBLANES, d), lambda j: (0, 0)),
                  pl.BlockSpec((d, tn), lambda j: (0, j)),
                  pl.BlockSpec((1, tn), lambda j: (0, j))],
        out_specs=pl.BlockSpec((SUBLANES, tn), lambda j: (0, j)),
        out_shape=jax.ShapeDtypeStruct((SUBLANES, n), F32),
        compiler_params=_params("arbitrary"),
        name="adaln",
    )(cond8, w_mod, b_mod.reshape(1, n))
    return out.reshape(SUBLANES, 6, d)


def _lin_in_body(x_ref, mod_ref, g_ref, w_ref, o_ref):
    h = _norm_mod(x_ref[...], g_ref[...], mod_ref[0:1, :], mod_ref[1:2, :])
    o_ref[...] = _dot(h.astype(BF16), w_ref[...])


def _lin_in(x, mod, g, w, *, tm, seg_len):
    t, d = x.shape
    n = w.shape[1]
    return pl.pallas_call(
        _lin_in_body,
        grid=(t // tm,),
        in_specs=[pl.BlockSpec((tm, d), lambda i: (i, 0)),
                  pl.BlockSpec((None, 6, d), lambda i: (i * tm // seg_len, 0, 0)),
                  pl.BlockSpec((1, d), lambda i: (0, 0)),
                  pl.BlockSpec((d, n), lambda i: (0, 0))],
        out_specs=pl.BlockSpec((tm, n), lambda i: (i, 0)),
        out_shape=jax.ShapeDtypeStruct((t, n), F32),
        compiler_params=_params("parallel"),
        name="lin_in",
    )(x, mod, g.reshape(1, d), w)


def _hy_pre_body(z_ref, zp_ref, zn_ref, w_ref, b_ref, u_ref, x0_ref, *, tm, lat_tiles, tiles_per_seq):
    i = pl.program_id(0)
    z = z_ref[...]
    in_lat = i < lat_tiles
    has_prev = jnp.logical_and(in_lat, i % tiles_per_seq != 0)
    has_next = jnp.logical_and(in_lat, i % tiles_per_seq != tiles_per_seq - 1)
    prev_row = jnp.where(has_prev, zp_ref[SUBLANES - 1:SUBLANES, :], 0.0)
    next_row = jnp.where(has_next, zn_ref[0:1, :], 0.0)
    rows = lax.broadcasted_iota(jnp.int32, z.shape, 0)
    z_m = jnp.where(rows == 0, prev_row, pltpu.roll(z, 1, 0))
    z_p = jnp.where(rows == tm - 1, next_row, pltpu.roll(z, tm - 1, 0))
    zc = b_ref[...] + z_m * w_ref[0:1, :] + z * w_ref[1:2, :] + z_p * w_ref[2:3, :]
    c = HY_WIDTH
    x0_ref[...] = zc[:, :c]
    u_ref[...] = zc[:, 2 * c:] * zc[:, c:2 * c]


def _hy_pre(z, conv_w, conv_b, *, t_lat, lat_seq, ctx_seq):
    t = z.shape[0]
    tm = ctx_seq
    c3 = 3 * HY_WIDTH
    nb8 = t // SUBLANES
    body = functools.partial(_hy_pre_body, tm=tm, lat_tiles=t_lat // tm, tiles_per_seq=lat_seq // tm)
    return pl.pallas_call(
        body,
        grid=(t // tm,),
        in_specs=[pl.BlockSpec((tm, c3), lambda i: (i, 0)),
                  pl.BlockSpec((SUBLANES, c3), lambda i: (jnp.maximum(i * (tm // SUBLANES) - 1, 0), 0)),
                  pl.BlockSpec((SUBLANES, c3), lambda i: (jnp.minimum((i + 1) * (tm // SUBLANES), nb8 - 1), 0)),
                  pl.BlockSpec((3, c3), lambda i: (0, 0)),
                  pl.BlockSpec((1, c3), lambda i: (0, 0))],
        out_specs=[pl.BlockSpec((tm, HY_WIDTH), lambda i: (i, 0)),
                   pl.BlockSpec((tm, HY_WIDTH), lambda i: (i, 0))],
        out_shape=[jax.ShapeDtypeStruct((t, HY_WIDTH), F32),
                   jax.ShapeDtypeStruct((t, HY_WIDTH), F32)],
        compiler_params=_params("parallel"),
        name="hy_pre",
    )(z, z, z, conv_w, conv_b.reshape(1, c3))


def _filter_embedding(seq):
    t01 = np.linspace(0.0, 1.0, seq)[:, None]
    w = 2.0 * math.pi * np.arange(seq)[:, None] / seq
    f = np.linspace(1e-4, HY_BANDS - 1, HY_BANDS)[None, :]
    z = np.concatenate([t01, np.cos(f * w), -np.sin(f * w)], axis=-1)
    z_rev = np.concatenate([z[:1], z[:0:-1]], axis=0)
    zz = np.concatenate([z, z_rev], axis=0)
    out = np.zeros((2 * seq, LANES), np.float32)
    out[:, :zz.shape[1]] = zz
    return out


def _filter_body(zz_ref, dl_ref, w1_ref, b1_ref, f1_ref, w2_ref, b2_ref, f2_ref, w3_ref,
                 k_ref, s_ref, *, tm, lat_tiles, ctx_tiles):
    i = pl.program_id(0)
    zz = zz_ref[...]
    h = jnp.sin(f1_ref[...] * (_dot_f32(zz, w1_ref[...]) + b1_ref[...]))
    h = jnp.sin(f2_ref[...] * (_dot_f32(h, w2_ref[...]) + b2_ref[...]))
    h = _dot_f32(h, w3_ref[...])
    is_bwd = jnp.logical_or(jnp.logical_and(i >= lat_tiles // 2, i < lat_tiles),
                            i >= lat_tiles + ctx_tiles // 2)
    first_bwd = jnp.logical_or(i == lat_tiles // 2, i == lat_tiles + ctx_tiles // 2)
    window = jnp.exp(-zz[:, 0:1] * dl_ref[...])
    k = jnp.where(is_bwd, h[:, HY_WIDTH:], h[:, :HY_WIDTH]) * window
    rows = lax.broadcasted_iota(jnp.int32, k.shape, 0)
    k = jnp.where(jnp.logical_and(first_bwd, rows == 0), 0.0, k)
    k_ref[...] = k
    s = jnp.sum(jnp.abs(k), axis=0, keepdims=True)

    @pl.when(i == 0)
    def _():
        s_ref[...] = jnp.zeros_like(s_ref)

    @pl.when(i < lat_tiles)
    def _():
        s_ref[0:1, :] += s

    @pl.when(i >= lat_tiles)
    def _():
        s_ref[1:2, :] += s


def _hyena_filters(p, *, lat_seq, ctx_seq):
    tm = ctx_seq
    zz = jnp.asarray(np.concatenate([_filter_embedding(lat_seq), _filter_embedding(ctx_seq)], axis=0))
    rows = zz.shape[0]
    max_decay = math.log(HY_DECAY_TARGET) / HY_FAST_DECAY_PCT
    min_decay = math.log(HY_DECAY_TARGET) / HY_SLOW_DECAY_PCT
    deltas = jnp.asarray(np.abs(np.linspace(min_decay, max_decay, HY_WIDTH))[None, :].astype(np.float32))
    emb, hid = p['hy_fw1'].shape

    def pad2(a, r, c):
        return jnp.zeros((r, c), F32).at[:a.shape[0], :a.shape[1]].set(a)

    w1 = pad2(p['hy_fw1'], LANES, LANES)
    b1 = pad2(p['hy_fb1'][None, :], 1, LANES)
    f1 = pad2(p['hy_freq1'][None, :], 1, LANES)
    w2 = pad2(p['hy_fw2'], LANES, LANES)
    b2 = pad2(p['hy_fb2'][None, :], 1, LANES)
    f2 = pad2(p['hy_freq2'][None, :], 1, LANES)
    w3 = pad2(p['hy_fw3'], LANES, 2 * HY_WIDTH)
    body = functools.partial(_filter_body, tm=tm, lat_tiles=2 * lat_seq // tm, ctx_tiles=2 * ctx_seq // tm)
    full = lambda i: (0, 0)
    return pl.pallas_call(
        body,
        grid=(rows // tm,),
        in_specs=[pl.BlockSpec((tm, LANES), lambda i: (i, 0)),
                  pl.BlockSpec((1, HY_WIDTH), full),
                  pl.BlockSpec((LANES, LANES), full), pl.BlockSpec((1, LANES), full), pl.BlockSpec((1, LANES), full),
                  pl.BlockSpec((LANES, LANES), full), pl.BlockSpec((1, LANES), full), pl.BlockSpec((1, LANES), full),
                  pl.BlockSpec((LANES, 2 * HY_WIDTH), full)],
        out_specs=[pl.BlockSpec((tm, HY_WIDTH), lambda i: (i, 0)),
                   pl.BlockSpec((SUBLANES, HY_WIDTH), full)],
        out_shape=[jax.ShapeDtypeStruct((rows, HY_WIDTH), F32),
                   jax.ShapeDtypeStruct((SUBLANES, HY_WIDTH), F32)],
        compiler_params=_params("arbitrary"),
        name="hy_filter",
    )(zz, deltas, w1, b1, f1, w2, b2, f2, w3)


def _stack_complex(z):
    return np.block([[z.real, -z.imag], [z.imag, z.real]])


def _dft_consts_two_level(seq, n1, n2):
    n = 2 * seq
    assert n1 * n2 == n
    a1 = np.arange(n1)
    f1_full = np.exp(-2j * np.pi * np.outer(a1, a1) / n1)
    f1_u = np.concatenate([f1_full.real, f1_full.imag], axis=0)[:, :n1 // 2]
    f1_k = np.concatenate([f1_full.real, f1_full.imag], axis=0)
    a2 = np.arange(n2)
    f = a1[:, None, None] + n1 * a2[None, :, None]
    z = np.exp(-2j * np.pi * (f * a2[None, None, :]) / n)
    mf = np.stack([_stack_complex(z[i]) for i in range(n1)])
    mi = np.stack([_stack_complex(np.conj(z[i]).T) for i in range(n1)])
    g = np.exp(2j * np.pi * np.outer(a1[:n1 // 2], a1) / n1) / n
    gc, gs = g.real, -g.imag
    as32 = lambda a: jnp.asarray(a.astype(np.float32))
    return as32(f1_u), as32(f1_k), as32(mf), as32(mi), as32(gc), as32(gs)


def _dft_consts_one_level(seq):
    n = 2 * seq
    a = np.arange(n)
    z = np.exp(-2j * np.pi * np.outer(a, a) / n)
    mf = np.concatenate([z.real, z.imag], axis=0)
    zi = np.exp(2j * np.pi * np.outer(a[:seq], a) / n) / n
    mi = np.concatenate([zi.real, -zi.imag], axis=1)
    as32 = lambda a: jnp.asarray(a.astype(np.float32))
    return as32(mf), as32(mi)


def _lmat_body(f_ref, x_ref, sc_ref, or_ref, oi_ref):
    x = (x_ref[...] * (1.0 / sc_ref[...])).astype(BF16)
    o = _dot(f_ref[...], x)
    h = o.shape[0] // 2
    or_ref[...] = o[:h]
    oi_ref[...] = o[h:]


def _lmat(f, x, scale_row, *, tn):
    g, k, cols = x.shape
    m2 = f.shape[0]
    m = m2 // 2
    return pl.pallas_call(
        _lmat_body,
        grid=(g, cols // tn),
        in_specs=[pl.BlockSpec((m2, k), lambda b, j: (0, 0)),
                  pl.BlockSpec((None, k, tn), lambda b, j: (b, 0, j)),
                  pl.BlockSpec((1, tn), lambda b, j: (0, 0))],
        out_specs=[pl.BlockSpec((None, m, tn), lambda b, j: (b, 0, j)),
                   pl.BlockSpec((None, m, tn), lambda b, j: (b, 0, j))],
        out_shape=[jax.ShapeDtypeStruct((g, m, cols), F32),
                   jax.ShapeDtypeStruct((g, m, cols), F32)],
        compiler_params=_params("parallel", "parallel"),
        name="hy_dft1",
    )(f.astype(BF16), x, scale_row)


def _spec_fwd_body(mf_ref, ar_ref, ai_ref, kr_ref, ki_ref):
    a = jnp.concatenate([ar_ref[...], ai_ref[...]], axis=0).astype(BF16)
    x = _dot(mf_ref[...], a)
    h = x.shape[0] // 2
    kr_ref[...] = x[:h]
    ki_ref[...] = x[h:]


def _spec_fwd(mf, ar, ai):
    n1, n2, c = ar.shape
    spec = pl.BlockSpec((None, n2, c), lambda i: (i, 0, 0))
    return pl.pallas_call(
        _spec_fwd_body,
        grid=(n1,),
        in_specs=[pl.BlockSpec((None, 2 * n2, 2 * n2), lambda i: (i, 0, 0)), spec, spec],
        out_specs=[spec, spec],
        out_shape=[jax.ShapeDtypeStruct((n1, n2, c), F32)] * 2,
        compiler_params=_params("parallel"),
        name="hy_spec_filter",
    )(mf, ar, ai)


def _spec_mul_body(mf_ref, mi_ref, kr_ref, ki_ref, ar_ref, ai_ref, br_ref, bi_ref):
    a = jnp.concatenate([ar_ref[...], ai_ref[...]], axis=0).astype(BF16)
    x = _dot(mf_ref[...], a)
    h = x.shape[0] // 2
    xr, xi = x[:h], x[h:]
    kr, ki = kr_ref[...], ki_ref[...]
    y = jnp.concatenate([xr * kr - xi * ki, xr * ki + xi * kr], axis=0).astype(BF16)
    b = _dot(mi_ref[...], y)
    br_ref[...] = b[:h]
    bi_ref[...] = b[h:]


def _spec_mul(mf, mi, kr, ki, ar, ai):
    nb, n1, n2, c = ar.shape
    mspec = pl.BlockSpec((None, 2 * n2, 2 * n2), lambda i, b: (i, 0, 0))
    kspec = pl.BlockSpec((None, n2, c), lambda i, b: (i, 0, 0))
    aspec = pl.BlockSpec((None, None, n2, c), lambda i, b: (b, i, 0, 0))
    return pl.pallas_call(
        _spec_mul_body,
        grid=(n1, nb),
        in_specs=[mspec, mspec, kspec, kspec, aspec, aspec],
        out_specs=[aspec, aspec],
        out_shape=[jax.ShapeDtypeStruct((nb, n1, n2, c), F32)] * 2,
        compiler_params=_params("parallel", "arbitrary"),
        name="hy_spec_mul",
    )(mf, mi, kr, ki, ar, ai)


def _idft1_body(gc_ref, gs_ref, br_ref, bi_ref, u_ref, x0_ref, db_ref, o_ref):
    y = _dot(gc_ref[...], br_ref[...].astype(BF16)) + _dot(gs_ref[...], bi_ref[...].astype(BF16))
    u = u_ref[...]
    o_ref[...] = (y + u * db_ref[...]) * x0_ref[...]


def _idft1(gc, gs, br, bi, u, x0, db_row, *, tn):
    nb, n1, cols = br.shape
    m = gc.shape[0]
    gspec = pl.BlockSpec((m, n1), lambda b, j: (0, 0))
    bspec = pl.BlockSpec((None, n1, tn), lambda b, j: (b, 0, j))
    uspec = pl.BlockSpec((None, m, tn), lambda b, j: (b, 0, j))
    return pl.pallas_call(
        _idft1_body,
        grid=(nb, cols // tn),
        in_specs=[gspec, gspec, bspec, bspec, uspec, uspec, pl.BlockSpec((1, tn), lambda b, j: (0, 0))],
        out_specs=uspec,
        out_shape=jax.ShapeDtypeStruct((nb, m, cols), F32),
        compiler_params=_params("parallel", "parallel"),
        name="hy_idft1",
    )(gc.astype(BF16), gs.astype(BF16), br, bi, u, x0, db_row)


def _ctx_filter_body(mf_ref, k_ref, sc_ref, kf_ref):
    kf_ref[...] = _dot(mf_ref[...], (k_ref[...] * (1.0 / sc_ref[...])).astype(BF16))


def _ctx_conv_body(mf_ref, mi_ref, kf_ref, u_ref, x0_ref, db_ref, o_ref):
    u = u_ref[...]
    x = _dot(mf_ref[...], u.astype(BF16))
    h = x.shape[0] // 2
    xr, xi = x[:h], x[h:]
    kr, ki = kf_ref[:h, :], kf_ref[h:, :]
    y = jnp.concatenate([xr * kr - xi * ki, xr * ki + xi * kr], axis=0).astype(BF16)
    o_ref[...] = (_dot(mi_ref[...], y) + u * db_ref[...]) * x0_ref[...]


def _hyena_ctx(k_raw, k_norm1, u, x0, dbias, *, seq):
    mf, mi = _dft_consts_one_level(seq)
    n = 2 * seq
    c = u.shape[1]
    nseq = u.shape[0] // seq
    kf = pl.pallas_call(
        _ctx_filter_body,
        out_shape=jax.ShapeDtypeStruct((2 * n, c), F32),
        compiler_params=_params(),
        name="hy_ctx_filter",
    )(mf.astype(BF16), k_raw, k_norm1)
    full = lambda s: (0, 0)
    return pl.pallas_call(
        _ctx_conv_body,
        grid=(nseq,),
        in_specs=[pl.BlockSpec((2 * n, seq), full),
                  pl.BlockSpec((seq, 2 * n), full),
                  pl.BlockSpec((2 * n, c), full),
                  pl.BlockSpec((seq, c), lambda s: (s, 0)),
                  pl.BlockSpec((seq, c), lambda s: (s, 0)),
                  pl.BlockSpec((1, c), full)],
        out_specs=pl.BlockSpec((seq, c), lambda s: (s, 0)),
        out_shape=jax.ShapeDtypeStruct(u.shape, F32),
        compiler_params=_params("parallel"),
        name="hy_ctx_conv",
    )(mf[:, :seq].astype(BF16), mi.astype(BF16), kf, u, x0, dbias)


def _hyena_lat(k_raw, k_norm1, u, x0, dbias, *, seq, nb):
    c = u.shape[1]
    n1, n2 = 64, 2 * seq // 64
    f1_u, f1_k, mf, mi, gc, gs = _dft_consts_two_level(seq, n1, n2)
    mf = mf.astype(BF16)
    mi = mi.astype(BF16)
    cols = n2 * c
    tn = 4096
    rep = tn // c
    ones_row = jnp.ones((1, tn), F32)
    akr, aki = _lmat(f1_k, k_raw.reshape(1, n1, cols), jnp.tile(k_norm1, (1, rep)), tn=tn)
    kr, ki = _spec_fwd(mf, akr.reshape(n1, n2, c), aki.reshape(n1, n2, c))
    ar, ai = _lmat(f1_u, u.reshape(nb, n1 // 2, cols), ones_row, tn=tn)
    br, bi = _spec_mul(mf, mi, kr, ki, ar.reshape(nb, n1, n2, c), ai.reshape(nb, n1, n2, c))
    y = _idft1(gc, gs, br.reshape(nb, n1, cols), bi.reshape(nb, n1, cols),
               u.reshape(nb, n1 // 2, cols), x0.reshape(nb, n1 // 2, cols),
               jnp.tile(dbias, (1, rep)), tn=tn)
    return y.reshape(nb * seq, c)


def _rope_tables(seq, rope_dims, lane_offsets, pad_rows):
    rows = seq // GRID_W
    rr, cc = np.meshgrid(np.arange(rows), np.arange(GRID_W), indexing='ij')
    pos = (rr.reshape(-1).astype(np.float64), cc.reshape(-1).astype(np.float64))
    half = rope_dims // 2
    q = half // 2
    inv_freq = ROPE_BASE ** (-np.arange(0, half, 2, dtype=np.float64) / half)
    cos_t = np.ones((seq + pad_rows, LANES), np.float64)
    sin_a = np.zeros((seq + pad_rows, LANES), np.float64)
    sin_b = np.zeros((seq + pad_rows, LANES), np.float64)
    for off in lane_offsets:
        for axis in range(2):
            ang = pos[axis][:, None] * inv_freq[None, :]
            base = off + axis * half
            cos_t[:seq, base:base + q] = np.cos(ang)
            cos_t[:seq, base + q:base + half] = np.cos(ang)
            sin_b[:seq, base:base + q] = -np.sin(ang)
            sin_a[:seq, base + q:base + half] = np.sin(ang)
    as32 = lambda a: jnp.asarray(a.astype(np.float32))
    return as32(cos_t), as32(sin_a), as32(sin_b)


def _rope(x, cos_t, sin_a, sin_b, shift):
    return x * cos_t + pltpu.roll(x, shift, 1) * sin_a + pltpu.roll(x, LANES - shift, 1) * sin_b


def _mla_q_body(cq_ref, ckv_ref, qa_ref, kva_ref, wuq_ref, qn_ref, cos_ref, sa_ref, sb_ref,
                q_ref, ckvn_ref, *, scale):
    ckvn_ref[...] = _rms(ckv_ref[...], kva_ref[...])
    cqn = _rms(cq_ref[...], qa_ref[...])
    q = _dot(cqn.astype(BF16), wuq_ref[...])
    cos_t, sin_a, sin_b = cos_ref[...], sa_ref[...], sb_ref[...]
    g = qn_ref[...]
    for h in range(MLA_HEADS):
        qh = _rms(q[:, h * LANES:(h + 1) * LANES], g, MLA_QK)
        qh = _rope(qh, cos_t, sin_a, sin_b, MLA_ROPE // 4)
        q_ref[:, h * LANES:(h + 1) * LANES] = (qh * scale).astype(BF16)


def _pad_heads(w, heads, width):
    k = w.shape[0]
    w3 = w.reshape(k, heads, width)
    return jnp.zeros((k, heads, LANES), w.dtype).at[:, :, :width].set(w3).reshape(k, heads * LANES)


def _mla_q(z, p, tables, *, tm, t_lat, lat_seq):
    t = z.shape[0]
    cos_t, sin_a, sin_b = tables
    wuq = _pad_heads(p['w_uq'], MLA_HEADS, MLA_QK).astype(BF16)
    qn = jnp.zeros((1, LANES), F32).at[0, :MLA_QK].set(p['q_norm'])
    pos_blocks = lat_seq // tm
    lat_tiles = t_lat // tm
    tspec = pl.BlockSpec((tm, LANES), lambda i: (jnp.where(i < lat_tiles, i % pos_blocks, pos_blocks), 0))
    full = lambda i: (0, 0)
    cq_blk = (3 * HY_WIDTH) // MLA_Q_RANK
    ckv_blk = (3 * HY_WIDTH + MLA_Q_RANK) // MLA_KV_RANK
    return pl.pallas_call(
        functools.partial(_mla_q_body, scale=MLA_QK ** -0.5 * LOG2_E),
        grid=(t // tm,),
        in_specs=[pl.BlockSpec((tm, MLA_Q_RANK), lambda i: (i, cq_blk)),
                  pl.BlockSpec((tm, MLA_KV_RANK), lambda i: (i, ckv_blk)),
                  pl.BlockSpec((1, MLA_Q_RANK), full),
                  pl.BlockSpec((1, MLA_KV_RANK), full),
                  pl.BlockSpec((MLA_Q_RANK, MLA_HEADS * LANES), full),
                  pl.BlockSpec((1, LANES), full),
                  tspec, tspec, tspec],
        out_specs=[pl.BlockSpec((tm, MLA_HEADS * LANES), lambda i: (i, 0)),
                   pl.BlockSpec((tm, MLA_KV_RANK), lambda i: (i, 0))],
        out_shape=[jax.ShapeDtypeStruct((t, MLA_HEADS * LANES), BF16),
                   jax.ShapeDtypeStruct((t, MLA_KV_RANK), F32)],
        compiler_params=_params("parallel"),
        name="mla_q",
    )(z, z, p['qa_norm'].reshape(1, -1), p['kva_norm'].reshape(1, -1), wuq, qn, cos_t, sin_a, sin_b)


def _mla_kv_body(ckvn_ref, kr_ref, wk_ref, wv_ref, kn_ref, cos_ref, sa_ref, sb_ref, k_ref, v_ref):
    c = ckvn_ref[...].astype(BF16)
    k = _dot(c, wk_ref[...])
    v_ref[...] = _dot(c, wv_ref[...]).astype(BF16)
    kr = pltpu.roll(kr_ref[...], MLA_NOPE, 1)
    cos_t, sin_a, sin_b = cos_ref[...], sa_ref[...], sb_ref[...]
    g = kn_ref[...]
    for h in range(MLA_HEADS):
        kh = _rms(k[:, h * LANES:(h + 1) * LANES] + kr, g, MLA_QK)
        kh = _rope(kh, cos_t, sin_a, sin_b, MLA_ROPE // 4)
        k_ref[:, h * LANES:(h + 1) * LANES] = kh.astype(BF16)


def _mla_kv(ckvn_rows, kr_rows, p, tables, *, tm, nb, past, lat_seq):
    r = ckvn_rows.shape[0]
    cos_t, sin_a, sin_b = tables
    w = p['w_ukv'].reshape(MLA_KV_RANK, MLA_HEADS, MLA_NOPE + MLA_V)
    wk = _pad_heads(w[:, :, :MLA_NOPE].reshape(MLA_KV_RANK, -1), MLA_HEADS, MLA_NOPE).astype(BF16)
    wv = w[:, :, MLA_NOPE:].reshape(MLA_KV_RANK, MLA_HEADS * MLA_V).astype(BF16)
    kn = jnp.zeros((1, LANES), F32).at[0, :MLA_QK].set(p['k_norm'])
    per_b = (past + lat_seq) // tm
    past_tiles = past // tm
    pos_blocks = lat_seq // tm
    lat_tiles = nb * per_b

    def tmap(i):
        j = i % per_b
        is_pos = jnp.logical_and(i < lat_tiles, j >= past_tiles)
        return (jnp.where(is_pos, j - past_tiles, pos_blocks), 0)

    tspec = pl.BlockSpec((tm, LANES), tmap)
    full = lambda i: (0, 0)
    return pl.pallas_call(
        _mla_kv_body,
        grid=(r // tm,),
        in_specs=[pl.BlockSpec((tm, MLA_KV_RANK), lambda i: (i, 0)),
                  pl.BlockSpec((tm, LANES), lambda i: (i, 0)),
                  pl.BlockSpec((MLA_KV_RANK, MLA_HEADS * LANES), full),
                  pl.BlockSpec((MLA_KV_RANK, MLA_HEADS * MLA_V), full),
                  pl.BlockSpec((1, LANES), full),
                  tspec, tspec, tspec],
        out_specs=[pl.BlockSpec((tm, MLA_HEADS * LANES), lambda i: (i, 0)),
                   pl.BlockSpec((tm, MLA_HEADS * MLA_V), lambda i: (i, 0))],
        out_shape=[jax.ShapeDtypeStruct((r, MLA_HEADS * LANES), BF16),
                   jax.ShapeDtypeStruct((r, MLA_HEADS * MLA_V), BF16)],
        compiler_params=_params("parallel"),
        name="mla_kv",
    )(ckvn_rows, kr_rows, wk, wv, kn, cos_t, sin_a, sin_b)


ATT_CHUNK = 512
ATT_UNIT_ROWS = 256
ATT_TQ = 512


def _fill_vaug(vaug_ref, v_refs):
    off = 0
    for v in v_refs:
        n = v.shape[0]
        vaug_ref[off:off + n, :LANES] = v[...]
        off += n
    vaug_ref[:, LANES:] = jnp.ones((vaug_ref.shape[0], LANES), BF16)


def _softmax_pv(units, vaug_ref, s_ref, n_keys):
    chunk = min(ATT_CHUNK, n_keys)
    chunks = [slice(c * chunk, (c + 1) * chunk) for c in range(n_keys // chunk)]

    def scores(u, rows, m_lane):
        q, k_of = units[u]
        s = _dot_nt(q, k_of(rows))
        s_ref[u % 2, :, rows] = s
        for j in range(chunk // LANES):
            blk = s[:, j * LANES:(j + 1) * LANES]
            m_lane = blk if m_lane is None else jnp.maximum(m_lane, blk)
        return m_lane

    def values(u, rows, m, acc):
        p = jnp.exp2(s_ref[u % 2, :, rows] - m).astype(BF16)
        d = _dot(p, vaug_ref[rows, :])
        return d if acc is None else acc + d

    outs = []
    m_lane = None
    for rows in chunks:
        m_lane = scores(0, rows, m_lane)
    for u in range(len(units)):
        m = jnp.max(m_lane, axis=-1, keepdims=True)
        acc, m_lane = None, None
        for rows in chunks:
            acc = values(u, rows, m, acc)
            if u + 1 < len(units):
                m_lane = scores(u + 1, rows, m_lane)
        outs.append(acc)
    return outs


def _mla_attn_body(prev_ref, q_ref, k_ref, v_ref, o_ref, vaug_ref, s_ref):
    @pl.when(pl.program_id(2) == 0)
    def _():
        _fill_vaug(vaug_ref, [v_ref])

    n_keys = k_ref.shape[0]
    tq = q_ref.shape[0]
    ur = s_ref.shape[1]
    units = []
    for r0 in range(0, tq, ur):
        for hh in range(2):
            sl = slice(hh * LANES, (hh + 1) * LANES)
            units.append((q_ref[r0:r0 + ur, sl], lambda rows, sl=sl: k_ref[rows, sl]))
    res = [r[:, :LANES] / r[:, LANES:] for r in _softmax_pv(units, vaug_ref, s_ref, n_keys)]
    lane = lax.broadcasted_iota(jnp.int32, res[0].shape, 1)
    for i, r0 in enumerate(range(0, tq, ur)):
        o_ref[r0:r0 + ur, :] = jnp.where(lane < MLA_V, res[2 * i], res[2 * i + 1]).astype(BF16)


def _mla_attn(prev, q, k, v, *, tq, n_seq, seq_q, seq_k, q_row0, k_row0):
    hp = MLA_HEADS // 2
    nq = seq_q // tq
    qb0, kb0 = q_row0 // tq, k_row0 // seq_k
    return pl.pallas_call(
        _mla_attn_body,
        grid=(n_seq, hp, nq),
        in_specs=[pl.BlockSpec(memory_space=pl.ANY),
                  pl.BlockSpec((tq, 2 * LANES), lambda s, h, i: (qb0 + s * nq + i, h)),
                  pl.BlockSpec((seq_k, 2 * LANES), lambda s, h, i: (kb0 + s, h)),
                  pl.BlockSpec((seq_k, 2 * MLA_V), lambda s, h, i: (kb0 + s, h))],
        out_specs=pl.BlockSpec((tq, 2 * MLA_V), lambda s, h, i: (qb0 + s * nq + i, h)),
        out_shape=jax.ShapeDtypeStruct(prev.shape, prev.dtype),
        input_output_aliases={0: 0},
        scratch_shapes=[pltpu.VMEM((seq_k, 2 * LANES), BF16),
                        pltpu.VMEM((2, min(tq, ATT_UNIT_ROWS), seq_k), F32)],
        compiler_params=_params("arbitrary", "arbitrary", "arbitrary"),
        name="mla_attn",
    )(prev, q, k, v)


def _diff_attn_body(prev_ref, *refs, n_seg, lambda_init):
    q_ref, lam_ref, sub_ref = refs[0], refs[1], refs[2]
    k_refs = refs[3:3 + n_seg]
    v_refs = refs[3 + n_seg:3 + 2 * n_seg]
    o_ref, kcat_ref, vaug_ref, s_ref = refs[3 + 2 * n_seg:]

    @pl.when(pl.program_id(2) == 0)
    def _():
        _fill_vaug(vaug_ref, v_refs)
        off = 0
        for k in k_refs:
            kcat_ref[off:off + k.shape[0], :] = k[...]
            off += k.shape[0]

    lp = lam_ref[...]
    lam = (jnp.exp(jnp.sum(lp[0:1] * lp[1:2], axis=-1, keepdims=True))
           - jnp.exp(jnp.sum(lp[2:3] * lp[3:4], axis=-1, keepdims=True)) + lambda_init)
    n_keys = kcat_ref.shape[0]
    k_of = lambda rows: kcat_ref[rows, :]
    tq = q_ref.shape[0]
    ur = s_ref.shape[1]
    units = []
    for r0 in range(0, tq, ur):
        q = q_ref[r0:r0 + ur, :].astype(F32)
        lane = lax.broadcasted_iota(jnp.int32, q.shape, 1)
        units.append((jnp.where(lane < DIFF_DH, q, 0.0).astype(BF16), k_of))
        units.append((jnp.where(lane < DIFF_DH, 0.0, q).astype(BF16), k_of))
    res = _softmax_pv(units, vaug_ref, s_ref, n_keys)
    for i, r0 in enumerate(range(0, tq, ur)):
        r1, r2 = res[2 * i], res[2 * i + 1]
        o = r1[:, :LANES] / r1[:, LANES:] - (lam / r2[:, LANES:]) * r2[:, :LANES]
        o_ref[r0:r0 + ur, :] = (_rms(o, sub_ref[...]) * (1.0 - lambda_init)).astype(BF16)


def _diff_attn(prev, q, k_new, v_new, k_cache, v_cache, lam_p, subln, *, tq, n_seq, seq_q, q_row0, lambda_init):
    nq = seq_q // tq
    qb0 = q_row0 // tq
    sb0 = q_row0 // seq_q
    d = 2 * DIFF_DH
    new_spec = pl.BlockSpec((seq_q, d), lambda s, h, i: (sb0 + s, h))
    if k_cache is None:
        n_seg, k_args, v_args, k_specs, v_specs = 1, [k_new], [v_new], [new_spec], [new_spec]
        n_keys = seq_q
    else:
        past = k_cache.shape[2]
        c_spec = pl.BlockSpec((None, None, past, d), lambda s, h, i: (s, h, 0, 0))
        n_seg, k_args, v_args = 2, [k_cache, k_new], [v_cache, v_new]
        k_specs, v_specs = [c_spec, new_spec], [c_spec, new_spec]
        n_keys = past + seq_q
    return pl.pallas_call(
        functools.partial(_diff_attn_body, n_seg=n_seg, lambda_init=lambda_init),
        grid=(n_seq, DIFF_HEADS, nq),
        in_specs=[pl.BlockSpec(memory_space=pl.ANY),
                  pl.BlockSpec((tq, d), lambda s, h, i: (qb0 + s * nq + i, h)),
                  pl.BlockSpec((4, DIFF_DH), lambda s, h, i: (0, 0)),
                  pl.BlockSpec((1, d), lambda s, h, i: (0, 0))] + k_specs + v_specs,
        out_specs=pl.BlockSpec((tq, d), lambda s, h, i: (qb0 + s * nq + i, h)),
        out_shape=jax.ShapeDtypeStruct(prev.shape, prev.dtype),
        input_output_aliases={0: 0},
        scratch_shapes=[pltpu.VMEM((n_keys, d), BF16), pltpu.VMEM((n_keys, 2 * LANES), BF16),
                        pltpu.VMEM((2, min(tq, ATT_UNIT_ROWS), n_keys), F32)],
        compiler_params=_params("arbitrary", "arbitrary", "arbitrary"),
        name="diff_attn",
    )(prev, q, lam_p, subln.reshape(1, d), *k_args, *v_args)


def _proj_res_body(*refs, n_in, gate_row):
    x_ref, mod_ref, o_ref = refs[0], refs[1], refs[-1]
    acc = None
    for j in range(n_in):
        a_ref, w_ref = refs[2 + 2 * j], refs[3 + 2 * j]
        d = _dot(a_ref[...].astype(BF16), w_ref[...])
        acc = d if acc is None else acc + d
    o_ref[...] = x_ref[...] + mod_ref[gate_row:gate_row + 1, :] * acc


def _proj_res(x, mod, acts, ws, *, tm, seg_len, gate_row):
    t, d = x.shape
    in_specs = [pl.BlockSpec((tm, d), lambda i: (i, 0)),
                pl.BlockSpec((None, 6, d), lambda i: (i * tm // seg_len, 0, 0))]
    args = [x, mod]
    for a, w in zip(acts, ws):
        in_specs += [pl.BlockSpec((tm, a.shape[1]), lambda i: (i, 0)),
                     pl.BlockSpec(w.shape, lambda i: (0, 0))]
        args += [a, w]
    return pl.pallas_call(
        functools.partial(_proj_res_body, n_in=len(acts), gate_row=gate_row),
        grid=(t // tm,),
        in_specs=in_specs,
        out_specs=pl.BlockSpec((tm, d), lambda i: (i, 0)),
        out_shape=jax.ShapeDtypeStruct((t, d), F32),
        compiler_params=_params("parallel"),
        name="proj_res",
    )(*args)


def _ffn_body(x_ref, mod_ref, g_ref, wg_ref, wu_ref, wd_ref, o_ref, h_ref, acc_ref):
    f = pl.program_id(1)

    @pl.when(f == 0)
    def _():
        h_ref[...] = _norm_mod(x_ref[...], g_ref[...], mod_ref[3:4, :], mod_ref[4:5, :]).astype(BF16)
        acc_ref[...] = jnp.zeros_like(acc_ref)

    h = h_ref[...]
    a = _silu(_dot(h, wg_ref[...])) * _dot(h, wu_ref[...])
    acc_ref[...] += _dot(a.astype(BF16), wd_ref[...])

    @pl.when(f == pl.num_programs(1) - 1)
    def _():
        o_ref[...] = x_ref[...] + mod_ref[5:6, :] * acc_ref[...]


def _ffn(x, mod, g, wg, wu, wd, *, tm, tf, seg_len):
    t, d = x.shape
    ff = wg.shape[1]
    return pl.pallas_call(
        _ffn_body,
        grid=(t // tm, ff // tf),
        in_specs=[pl.BlockSpec((tm, d), lambda i, f: (i, 0)),
                  pl.BlockSpec((None, 6, d), lambda i, f: (i * tm // seg_len, 0, 0)),
                  pl.BlockSpec((1, d), lambda i, f: (0, 0)),
                  pl.BlockSpec((d, tf), lambda i, f: (0, f)),
                  pl.BlockSpec((d, tf), lambda i, f: (0, f)),
                  pl.BlockSpec((tf, d), lambda i, f: (f, 0))],
        out_specs=pl.BlockSpec((tm, d), lambda i, f: (i, 0)),
        out_shape=jax.ShapeDtypeStruct((t, d), F32),
        scratch_shapes=[pltpu.VMEM((tm, d), BF16), pltpu.VMEM((tm, d), F32)],
        compiler_params=_params("parallel", "arbitrary"),
        name="ffn",
    )(x, mod, g.reshape(1, d), wg, wu, wd)


def _group_ms(x, gmat):
    hi, lo = _split_bf16(x * x)
    return (_dot(hi, gmat) + _dot(lo, gmat)) * (1.0 / DIFF_DH)


def _qkv_body(x_ref, mod_ref, g_ref, w_ref, gm_ref, qn_ref, kn_ref, cos_ref, sa_ref, sb_ref,
              q_ref, k_ref, v_ref, kf_ref, vf_ref, *, scale):
    h = _norm_mod(x_ref[...], g_ref[...], mod_ref[0:1, :], mod_ref[1:2, :]).astype(BF16)
    z = _dot(h, w_ref[...])
    hd = DIFF_HEADS * 2 * DIFF_DH
    cos_t, sin_a, sin_b = cos_ref[...], sa_ref[...], sb_ref[...]
    gm = gm_ref[...]
    for hh in range(DIFF_HEADS):
        sl = slice(hh * LANES, (hh + 1) * LANES)
        qh = z[:, hh * LANES:(hh + 1) * LANES]
        qh = qh * lax.rsqrt(_group_ms(qh, gm) + NORM_EPS) * qn_ref[...]
        q_ref[:, sl] = (_rope(qh, cos_t, sin_a, sin_b, DIFF_DH // 4) * scale).astype(BF16)
        kh = z[:, hd + hh * LANES:hd + (hh + 1) * LANES]
        kh = kh * lax.rsqrt(_group_ms(kh, gm) + NORM_EPS) * kn_ref[...]
        kf_ref[:, sl] = kh
        k_ref[:, sl] = _rope(kh, cos_t, sin_a, sin_b, DIFF_DH // 4).astype(BF16)
    v = z[:, 2 * hd:]
    vf_ref[...] = v
    v_ref[...] = v.astype(BF16)


def _qkv(x, mod, p, tables, *, tm, seg_len, t_lat, lat_seq):
    t, d = x.shape
    hd = DIFF_HEADS * 2 * DIFF_DH
    cos_t, sin_a, sin_b = tables
    wqk = p['w_qkv'][:, :2 * hd].reshape(d, 2, 2, DIFF_HEADS, DIFF_DH)
    wqk = wqk.transpose(0, 1, 3, 2, 4).reshape(d, 2 * hd)
    w = jnp.concatenate([wqk, p['w_qkv'][:, 2 * hd:]], axis=1).astype(BF16)
    gi = np.arange(LANES) // DIFF_DH
    gmat = jnp.asarray((gi[:, None] == gi[None, :]).astype(np.float32)).astype(BF16)
    qn = jnp.tile(p['q_norm'], 2).reshape(1, LANES)
    kn = jnp.tile(p['k_norm'], 2).reshape(1, LANES)
    pos_blocks = lat_seq // tm
    lat_tiles = t_lat // tm
    ctx_tiles = (t - t_lat) // tm
    tspec = pl.BlockSpec((tm, LANES), lambda i: (jnp.where(i < lat_tiles, i % pos_blocks, pos_blocks), 0))
    full = lambda i: (0, 0)
    row = pl.BlockSpec((tm, hd), lambda i: (i, 0))
    fspec = pl.BlockSpec((tm, hd), lambda i: (jnp.where(i < lat_tiles, ctx_tiles, i - lat_tiles), 0))
    fshape = jax.ShapeDtypeStruct((t - t_lat + tm, hd), F32)
    return pl.pallas_call(
        functools.partial(_qkv_body, scale=DIFF_DH ** -0.5 * LOG2_E),
        grid=(t // tm,),
        in_specs=[pl.BlockSpec((tm, d), lambda i: (i, 0)),
                  pl.BlockSpec((None, 6, d), lambda i: (i * tm // seg_len, 0, 0)),
                  pl.BlockSpec((1, d), full),
                  pl.BlockSpec((d, 3 * hd), full),
                  pl.BlockSpec((LANES, LANES), full),
                  pl.BlockSpec((1, LANES), full), pl.BlockSpec((1, LANES), full),
                  tspec, tspec, tspec],
        out_specs=[row, row, row, fspec, fspec],
        out_shape=[jax.ShapeDtypeStruct((t, hd), BF16)] * 3 + [fshape, fshape],
        compiler_params=_params("arbitrary"),
        name="qkv",
    )(x, mod, p['norm1'].reshape(1, d), w, gmat, qn, kn, cos_t, sin_a, sin_b)


def _route(logits):
    lane = lax.broadcasted_iota(jnp.int32, logits.shape, 1)
    neg = jnp.float32(-jnp.inf)
    lg = jnp.where(lane < N_EXPERTS, logits, neg)
    m1 = jnp.max(lg, axis=-1, keepdims=True)
    i1 = jnp.min(jnp.where(lg == m1, lane, LANES), axis=-1, keepdims=True)
    lg2 = jnp.where(lane == i1, neg, lg)
    m2 = jnp.max(lg2, axis=-1, keepdims=True)
    i2 = jnp.min(jnp.where(lg2 == m2, lane, LANES), axis=-1, keepdims=True)
    e = jnp.exp(m2 - m1)
    w1 = 1.0 / (1.0 + e)
    w2 = e / (1.0 + e)
    return jnp.where(lane == i1, w1, 0.0) + jnp.where(lane == i2, w2, 0.0)


MOE_BLOCK = 1024
MOE_SUB = 256
MOE_ROUTE_TM = 512


def _moe_route_body(x_ref, mod_ref, g_ref, wr_ref, h_ref, gates_ref, rank_ref, rank_t_ref,
                    carry_row, carry_col, *, tm):
    i = pl.program_id(0)

    @pl.when(i == 0)
    def _():
        carry_row[...] = jnp.zeros_like(carry_row)
        carry_col[...] = jnp.zeros_like(carry_col)

    h = _norm_mod(x_ref[...], g_ref[...], mod_ref[3:4, :], mod_ref[4:5, :])
    h_ref[...] = h.astype(BF16)
    gates = _route(_dot_f32(h, wr_ref[...]))
    gates_ref[...] = gates
    sel = jnp.where(gates != 0.0, 1.0, 0.0)
    sel_t = sel.T
    r = lax.broadcasted_iota(jnp.int32, (tm, tm), 0)
    c = lax.broadcasted_iota(jnp.int32, (tm, tm), 1)
    lower = jnp.where(c < r, 1.0, 0.0).astype(BF16)
    upper = jnp.where(r < c, 1.0, 0.0).astype(BF16)
    before = _dot(lower, sel.astype(BF16)) + carry_row[...]
    before_t = _dot(sel_t.astype(BF16), upper) + carry_col[...]
    rank_ref[...] = jnp.where(sel > 0.0, before, -1.0)
    rank_t = jnp.where(sel_t > 0.0, before_t, -1.0)
    for s in range(tm // MOE_SUB):
        rank_t_ref[s] = rank_t[:SUBLANES, s * MOE_SUB:(s + 1) * MOE_SUB]
    carry_row[...] += jnp.sum(sel, axis=0, keepdims=True)
    carry_col[...] += jnp.sum(sel_t, axis=1, keepdims=True)


def _moe_route(x, mod, g, w_router, *, seg_len):
    t, d = x.shape
    tm = MOE_ROUTE_TM
    ne = w_router.shape[1]
    assert ne <= SUBLANES
    wr = jnp.zeros((d, LANES), F32).at[:, :ne].set(w_router)
    sub = tm // MOE_SUB
    return pl.pallas_call(
        functools.partial(_moe_route_body, tm=tm),
        grid=(t // tm,),
        in_specs=[pl.BlockSpec((tm, d), lambda i: (i, 0)),
                  pl.BlockSpec((None, 6, d), lambda i: (i * tm // seg_len, 0, 0)),
                  pl.BlockSpec((1, d), lambda i: (0, 0)),
                  pl.BlockSpec((d, LANES), lambda i: (0, 0))],
        out_specs=[pl.BlockSpec((tm, d), lambda i: (i, 0)),
                   pl.BlockSpec((tm, LANES), lambda i: (i, 0)),
                   pl.BlockSpec((tm, LANES), lambda i: (i, 0)),
                   pl.BlockSpec((sub, SUBLANES, MOE_SUB), lambda i: (i, 0, 0))],
        out_shape=[jax.ShapeDtypeStruct((t, d), BF16),
                   jax.ShapeDtypeStruct((t, LANES), F32),
                   jax.ShapeDtypeStruct((t, LANES), F32),
                   jax.ShapeDtypeStruct((t // MOE_SUB, SUBLANES, MOE_SUB), F32)],
        scratch_shapes=[pltpu.VMEM((1, LANES), F32), pltpu.VMEM((LANES, 1), F32)],
        compiler_params=_params("arbitrary"),
        name="moe_route",
    )(x, mod, g.reshape(1, d), wr)


def _moe_plan(rank, ne, *, n_blocks):
    t = rank.shape[0]
    n_tiles = t // MOE_SUB
    per_blk = MOE_BLOCK // MOE_SUB
    n_sub = n_blocks * per_blk
    sel = (rank[:, :ne] >= 0.0).astype(jnp.int32)
    tile_cnt = sel.reshape(n_tiles, MOE_SUB, ne).sum(axis=1)
    tile_end = jnp.cumsum(tile_cnt, axis=0)
    tile_start = tile_end - tile_cnt
    cnt = tile_end[-1]
    nblk = (cnt + MOE_BLOCK - 1) // MOE_BLOCK
    bend = jnp.cumsum(nblk)
    bstart = bend - nblk
    e_last = jnp.max(jnp.where(cnt > 0, jnp.arange(ne), 0))
    b = jnp.arange(n_blocks)
    blk_valid = b < bend[-1]
    blk_e = jnp.minimum(jnp.sum(bend[None, :] <= b[:, None], axis=1), e_last).astype(jnp.int32)
    blk_r0 = (b - bstart[blk_e]) * MOE_BLOCK
    blk_rows = jnp.where(blk_valid, jnp.clip(cnt[blk_e] - blk_r0, 0, MOE_BLOCK), 0).astype(jnp.int32)
    j = jnp.arange(n_sub)
    sub_e = blk_e[j // per_blk]
    sub_r0 = blk_r0[j // per_blk] + (j % per_blk) * MOE_SUB
    sub_valid = jnp.logical_and(blk_valid[j // per_blk], sub_r0 < cnt[sub_e])
    ends = tile_end[:, sub_e]
    r1 = jnp.minimum(sub_r0 + MOE_SUB, cnt[sub_e])
    c_lo = jnp.sum(ends <= sub_r0[None, :], axis=0)
    c_hi = jnp.sum(ends < r1[None, :], axis=0)
    c_lo = jnp.where(sub_valid, c_lo, 1).astype(jnp.int32)
    c_hi = jnp.where(sub_valid, jnp.minimum(c_hi, n_tiles - 1), 0).astype(jnp.int32)
    base = (bstart * MOE_BLOCK).astype(jnp.int32)
    j0 = jnp.minimum((base[None, :] + tile_start) // MOE_SUB, n_sub - 2).astype(jnp.int32)
    return dict(blk_e=blk_e, blk_valid=blk_valid.astype(jnp.int32), blk_rows=blk_rows,
                sub_e=sub_e.astype(jnp.int32), sub_r0=sub_r0.astype(jnp.int32), c_lo=c_lo, c_hi=c_hi,
                base=base, j0=j0.reshape(-1))


def _moe_dispatch_body(e_ref, r0_ref, lo_ref, hi_ref, h_ref, rank_t_ref, xs_ref, acc_ref):
    j = pl.program_id(0)
    e = e_ref[j]
    rows = (r0_ref[j] + lax.broadcasted_iota(jnp.int32, (MOE_SUB, 1), 0)).astype(F32)
    sub = lax.broadcasted_iota(jnp.int32, (SUBLANES, MOE_SUB), 0)
    acc_ref[...] = jnp.zeros_like(acc_ref)

    def step(c, carry):
        rk = jnp.sum(jnp.where(sub == e, rank_t_ref[c], 0.0), axis=0, keepdims=True)
        onehot = jnp.where(rk == rows, 1.0, 0.0).astype(BF16)
        off = pl.multiple_of(c * MOE_SUB, MOE_SUB)
        acc_ref[...] += _dot(onehot, h_ref[pl.ds(off, MOE_SUB), :])
        return carry

    lax.fori_loop(lo_ref[j], hi_ref[j] + 1, step, 0)
    xs_ref[...] = acc_ref[...].astype(BF16)


def _moe_dispatch(h, rank_t, plan, *, n_sub):
    t, d = h.shape
    grid_spec = pltpu.PrefetchScalarGridSpec(
        num_scalar_prefetch=4,
        grid=(n_sub,),
        in_specs=[pl.BlockSpec((t, d), lambda j, *_: (0, 0), pipeline_mode=pl.Buffered(1)),
                  pl.BlockSpec(rank_t.shape, lambda j, *_: (0, 0, 0), pipeline_mode=pl.Buffered(1))],
        out_specs=pl.BlockSpec((MOE_SUB, d), lambda j, *_: (j, 0)),
        scratch_shapes=[pltpu.VMEM((MOE_SUB, d), F32)],
    )
    return pl.pallas_call(
        _moe_dispatch_body,
        grid_spec=grid_spec,
        out_shape=jax.ShapeDtypeStruct((n_sub * MOE_SUB, d), BF16),
        compiler_params=_params("arbitrary"),
        name="moe_dispatch",
    )(plan['sub_e'], plan['sub_r0'], plan['c_lo'], plan['c_hi'], h, rank_t)


def _moe_ffn_body(e_ref, valid_ref, rows_ref, xs_ref, wg_ref, wu_ref, wd_ref, y_ref, acc_ref):
    b = pl.program_id(0)
    f = pl.program_id(1)
    n_rows = rows_ref[b]
    last = f == pl.num_programs(1) - 1
    wg = wg_ref[...].astype(BF16)
    wu = wu_ref[...].astype(BF16)
    wd = wd_ref[...].astype(BF16)
    full = n_rows == MOE_BLOCK

    def swiglu(h):
        a = _silu(_dot(h, wg)) * _dot(h, wu)
        return _dot(a.astype(BF16), wd)

    @pl.when(jnp.logical_and(full, f == 0))
    def _():
        acc_ref[...] = swiglu(xs_ref[...])

    @pl.when(jnp.logical_and(full, f > 0))
    def _():
        acc_ref[...] += swiglu(xs_ref[...])

    @pl.when(jnp.logical_and(full, last))
    def _():
        y_ref[...] = acc_ref[...].astype(BF16)

    for s in range(MOE_BLOCK // MOE_SUB):
        sl = slice(s * MOE_SUB, (s + 1) * MOE_SUB)
        live = jnp.logical_and(jnp.logical_not(full), s * MOE_SUB < n_rows)
        dead = jnp.logical_and(jnp.logical_not(full), s * MOE_SUB >= n_rows)

        @pl.when(jnp.logical_and(live, f == 0))
        def _():
            acc_ref[sl, :] = jnp.zeros((MOE_SUB, acc_ref.shape[1]), F32)

        @pl.when(live)
        def _():
            acc_ref[sl, :] += swiglu(xs_ref[sl, :])

        @pl.when(jnp.logical_and(live, last))
        def _():
            y_ref[sl, :] = acc_ref[sl, :].astype(BF16)

        @pl.when(jnp.logical_and(dead, last))
        def _():
            y_ref[sl, :] = jnp.zeros((MOE_SUB, y_ref.shape[1]), BF16)


def _moe_ffn(xs, wg, wu, wd, plan, *, n_blocks, tf):
    _, d = xs.shape
    ne, _, ff = wg.shape
    nf = ff // tf

    def w_in(b, f, e_ref, valid_ref, rows_ref):
        return (e_ref[b], 0, jnp.where(valid_ref[b] > 0, f, nf - 1))

    def w_down(b, f, e_ref, valid_ref, rows_ref):
        return (e_ref[b], jnp.where(valid_ref[b] > 0, f, nf - 1), 0)

    grid_spec = pltpu.PrefetchScalarGridSpec(
        num_scalar_prefetch=3,
        grid=(n_blocks, nf),
        in_specs=[pl.BlockSpec((MOE_BLOCK, d), lambda b, f, *_: (b, 0)),
                  pl.BlockSpec((None, d, tf), w_in),
                  pl.BlockSpec((None, d, tf), w_in),
                  pl.BlockSpec((None, tf, d), w_down)],
        out_specs=pl.BlockSpec((MOE_BLOCK, d), lambda b, f, *_: (b, 0)),
        scratch_shapes=[pltpu.VMEM((MOE_BLOCK, d), F32)],
    )
    return pl.pallas_call(
        _moe_ffn_body,
        grid_spec=grid_spec,
        out_shape=jax.ShapeDtypeStruct((n_blocks * MOE_BLOCK, d), BF16),
        compiler_params=_params("arbitrary", "arbitrary"),
        name="moe_ffn",
    )(plan['blk_e'], plan['blk_valid'], plan['blk_rows'], xs, wg, wu, wd)


def _moe_combine_body(j0_ref, base_ref, x_ref, mod_ref, gates_ref, rank_ref, *rest, ne, split_tiles):
    y_refs, o_ref, o2_ref = rest[:2 * ne], rest[2 * ne], rest[2 * ne + 1]
    c = pl.program_id(0)
    gates = gates_ref[...]
    rank = rank_ref[...]
    lane = lax.broadcasted_iota(jnp.int32, gates.shape, 1)
    col = lax.broadcasted_iota(jnp.int32, (1, MOE_SUB), 1).astype(F32)
    acc = None
    for e in range(ne):
        pick = lane == e
        g = jnp.sum(jnp.where(pick, gates, 0.0), axis=-1, keepdims=True)
        rk = jnp.sum(jnp.where(pick, rank, 0.0), axis=-1, keepdims=True)
        shift = (base_ref[e] - j0_ref[c * ne + e] * MOE_SUB).astype(F32)
        loc = jnp.where(rk >= 0.0, rk + shift, -1.0)
        qa = jnp.where(loc == col, 1.0, 0.0).astype(BF16)
        qb = jnp.where(loc == col + float(MOE_SUB), 1.0, 0.0).astype(BF16)
        contrib = g * (_dot(qa, y_refs[2 * e][...]) + _dot(qb, y_refs[2 * e + 1][...]))
        acc = contrib if acc is None else acc + contrib
    out = x_ref[...] + mod_ref[5:6, :] * acc

    @pl.when(c < split_tiles)
    def _():
        o_ref[...] = out

    @pl.when(c >= split_tiles)
    def _():
        o2_ref[...] = out


def _moe_combine(x, mod, gates, rank, y, plan, *, ne, seg_len, t_split):
    t, d = x.shape
    tm = MOE_SUB
    split_tiles = t_split // tm
    y_specs = []
    for e in range(ne):
        y_specs += [pl.BlockSpec((MOE_SUB, d), lambda c, j0, base, e=e: (j0[c * ne + e], 0)),
                    pl.BlockSpec((MOE_SUB, d), lambda c, j0, base, e=e: (j0[c * ne + e] + 1, 0))]
    grid_spec = pltpu.PrefetchScalarGridSpec(
        num_scalar_prefetch=2,
        grid=(t // tm,),
        in_specs=[pl.BlockSpec((tm, d), lambda c, *_: (c, 0)),
                  pl.BlockSpec((None, 6, d), lambda c, *_: (c * tm // seg_len, 0, 0)),
                  pl.BlockSpec((tm, LANES), lambda c, *_: (c, 0)),
                  pl.BlockSpec((tm, LANES), lambda c, *_: (c, 0))] + y_specs,
        out_specs=[pl.BlockSpec((tm, d), lambda c, *_: (jnp.minimum(c, split_tiles - 1), 0)),
                   pl.BlockSpec((tm, d), lambda c, *_: (jnp.maximum(c - split_tiles, 0), 0))],
    )
    return pl.pallas_call(
        functools.partial(_moe_combine_body, ne=ne, split_tiles=split_tiles),
        grid_spec=grid_spec,
        out_shape=[jax.ShapeDtypeStruct((t_split, d), F32), jax.ShapeDtypeStruct((t - t_split, d), F32)],
        compiler_params=_params("arbitrary"),
        name="moe_combine",
    )(plan['j0'], plan['base'], x, mod, gates, rank, *([y] * (2 * ne)))


def _moe(x, mod, g, w_router, wg, wu, wd, *, seg_len, t_split, top_k=2):
    t, d = x.shape
    ne = w_router.shape[1]
    n_blocks = t * top_k // MOE_BLOCK + ne
    h, gates, rank, rank_t = _moe_route(x, mod, g, w_router, seg_len=seg_len)
    plan = _moe_plan(rank, ne, n_blocks=n_blocks)
    xs = _moe_dispatch(h, rank_t, plan, n_sub=n_blocks * (MOE_BLOCK // MOE_SUB))
    y = _moe_ffn(xs, wg, wu, wd, plan, n_blocks=n_blocks, tf=512)
    return _moe_combine(x, mod, gates, rank, y, plan, ne=ne, seg_len=seg_len, t_split=t_split)


def _even_layer(x, cond8, p, cache_ckv, cache_kr, *, nb, lat_seq, n_ctx, ctx_seq):
    t, d = x.shape
    t_lat = nb * lat_seq
    seg_len = lat_seq
    past = cache_ckv.shape[1]
    mod = _adaln(cond8, p['w_mod'], p['b_mod'])

    n_in = p['w_in'].shape[1]
    n_pad = -(-n_in // LANES) * LANES
    w_in = jnp.zeros((d, n_pad), BF16).at[:, :n_in].set(p['w_in'].astype(BF16))
    z = _lin_in(x, mod, p['norm1'], w_in, tm=512, seg_len=seg_len)

    u, x0 = _hy_pre(z, p['hy_conv_w'], p['hy_conv_b'], t_lat=t_lat, lat_seq=lat_seq, ctx_seq=ctx_seq)
    k_raw, k_sum = _hyena_filters(p, lat_seq=lat_seq, ctx_seq=ctx_seq)
    dbias = p['hy_dbias'].reshape(1, HY_WIDTH)
    y_lat = _hyena_lat(k_raw[:2 * lat_seq], k_sum[0:1], u[:t_lat], x0[:t_lat], dbias, seq=lat_seq, nb=nb)
    y_ctx = _hyena_ctx(k_raw[2 * lat_seq:], k_sum[1:2], u[t_lat:], x0[t_lat:], dbias, seq=ctx_seq)
    y_hy = jnp.concatenate([y_lat, y_ctx], axis=0)

    tm = 512
    tables = _rope_tables(lat_seq, MLA_ROPE, (MLA_NOPE,), tm)
    q, ckvn = _mla_q(z, p, tables, tm=tm, t_lat=t_lat, lat_seq=lat_seq)
    kr_col = 3 * HY_WIDTH + MLA_Q_RANK + MLA_KV_RANK
    kr = z[:, kr_col:kr_col + LANES]
    cache_kr_p = jnp.zeros((nb, past, LANES), F32).at[:, :, :MLA_ROPE].set(cache_kr)
    ckvn_rows = jnp.concatenate(
        [jnp.concatenate([cache_ckv, ckvn[:t_lat].reshape(nb, lat_seq, -1)], axis=1).reshape(nb * (past + lat_seq), -1),
         ckvn[t_lat:]], axis=0)
    kr_rows = jnp.concatenate(
        [jnp.concatenate([cache_kr_p, kr[:t_lat].reshape(nb, lat_seq, LANES)], axis=1).reshape(nb * (past + lat_seq), LANES),
         kr[t_lat:]], axis=0)
    k_all, v_all = _mla_kv(ckvn_rows, kr_rows, p, tables, tm=tm, nb=nb, past=past, lat_seq=lat_seq)
    o = jnp.zeros((t, MLA_HEADS * MLA_V), BF16)
    o = _mla_attn(o, q, k_all, v_all, tq=ATT_TQ, n_seq=nb, seq_q=lat_seq, seq_k=past + lat_seq,
                  q_row0=0, k_row0=0)
    o = _mla_attn(o, q, k_all, v_all, tq=ctx_seq, n_seq=n_ctx, seq_q=ctx_seq, seq_k=ctx_seq,
                  q_row0=t_lat, k_row0=nb * (past + lat_seq))

    w_out = p['w_out'].astype(BF16)
    x = _proj_res(x, mod, [y_hy, o], [w_out[:HY_WIDTH], w_out[HY_WIDTH:]], tm=512, seg_len=seg_len, gate_row=2)
    x = _ffn(x, mod, p['norm2'], p['ffn_w_gate'].astype(BF16), p['ffn_w_up'].astype(BF16),
             p['ffn_w_down'].astype(BF16), tm=512, tf=1408, seg_len=seg_len)
    new_ckv = ckvn[t_lat:].reshape(n_ctx, ctx_seq, -1)
    new_kr = kr[t_lat:, :MLA_ROPE].reshape(n_ctx, ctx_seq, MLA_ROPE)
    return x, new_ckv, new_kr


def _odd_layer(x, cond8, p, cache_k, cache_v, lambda_init, *, nb, lat_seq, n_ctx, ctx_seq):
    t, d = x.shape
    t_lat = nb * lat_seq
    seg_len = lat_seq
    mod = _adaln(cond8, p['w_mod'], p['b_mod'])
    tm = 512
    tables = _rope_tables(lat_seq, DIFF_DH, (0, DIFF_DH), tm)
    q, k, v, kf, vf = _qkv(x, mod, p, tables, tm=tm, seg_len=seg_len, t_lat=t_lat, lat_seq=lat_seq)
    lam_p = jnp.stack([p['lam_q1'], p['lam_k1'], p['lam_q2'], p['lam_k2']])
    o = jnp.zeros((t, DIFF_HEADS * 2 * DIFF_DH), BF16)
    o = _diff_attn(o, q, k, v, cache_k.astype(BF16), cache_v.astype(BF16), lam_p, p['subln'],
                   tq=ATT_TQ, n_seq=nb, seq_q=lat_seq, q_row0=0, lambda_init=lambda_init)
    o = _diff_attn(o, q, k, v, None, None, lam_p, p['subln'],
                   tq=ctx_seq, n_seq=n_ctx, seq_q=ctx_seq, q_row0=t_lat, lambda_init=lambda_init)
    x = _proj_res(x, mod, [o], [p['w_out'].astype(BF16)], tm=512, seg_len=seg_len, gate_row=2)
    x_lat, x_ctx = _moe(x, mod, p['norm2'], p['w_router'], p['moe_w_gate'], p['moe_w_up'], p['moe_w_down'],
                        seg_len=seg_len, t_split=t_lat)
    hd = 2 * DIFF_DH
    t_ctx = t - t_lat
    new_k = kf[:t_ctx].reshape(n_ctx, ctx_seq, DIFF_HEADS, hd).transpose(0, 2, 1, 3)
    new_v = vf[:t_ctx].reshape(n_ctx, ctx_seq, DIFF_HEADS, hd).transpose(0, 2, 1, 3)
    return x_lat, x_ctx, new_k, new_v


def kernel(x_prompt, x_sample, cache_l0_ckv, cache_l0_krope, cache_l1_k, cache_l1_v, c, c_ctx,
           l0_w_mod, l0_b_mod, l0_norm1, l0_norm2, l0_w_in, l0_hy_conv_w, l0_hy_conv_b,
           l0_hy_fw1, l0_hy_fb1, l0_hy_freq1, l0_hy_fw2, l0_hy_fb2, l0_hy_freq2, l0_hy_fw3, l0_hy_dbias,
           l0_mla_qa_norm, l0_mla_w_uq, l0_mla_kva_norm, l0_mla_w_ukv, l0_mla_q_norm, l0_mla_k_norm,
           l0_w_out, l0_ffn_w_gate, l0_ffn_w_up, l0_ffn_w_down,
           l1_w_mod, l1_b_mod, l1_norm1, l1_norm2, l1_w_qkv, l1_q_norm, l1_k_norm,
           l1_lam_q1, l1_lam_k1, l1_lam_q2, l1_lam_k2, l1_subln, l1_w_out,
           l1_w_router, l1_moe_w_gate, l1_moe_w_up, l1_moe_w_down):
    even = {
        'w_mod': l0_w_mod, 'b_mod': l0_b_mod, 'norm1': l0_norm1, 'norm2': l0_norm2, 'w_in': l0_w_in,
        'hy_conv_w': l0_hy_conv_w, 'hy_conv_b': l0_hy_conv_b, 'hy_fw1': l0_hy_fw1, 'hy_fb1': l0_hy_fb1,
        'hy_freq1': l0_hy_freq1, 'hy_fw2': l0_hy_fw2, 'hy_fb2': l0_hy_fb2, 'hy_freq2': l0_hy_freq2,
        'hy_fw3': l0_hy_fw3, 'hy_dbias': l0_hy_dbias, 'qa_norm': l0_mla_qa_norm, 'w_uq': l0_mla_w_uq,
        'kva_norm': l0_mla_kva_norm, 'w_ukv': l0_mla_w_ukv, 'q_norm': l0_mla_q_norm, 'k_norm': l0_mla_k_norm,
        'w_out': l0_w_out, 'ffn_w_gate': l0_ffn_w_gate, 'ffn_w_up': l0_ffn_w_up, 'ffn_w_down': l0_ffn_w_down,
    }
    odd = {
        'w_mod': l1_w_mod, 'b_mod': l1_b_mod, 'norm1': l1_norm1, 'norm2': l1_norm2, 'w_qkv': l1_w_qkv,
        'q_norm': l1_q_norm, 'k_norm': l1_k_norm, 'lam_q1': l1_lam_q1, 'lam_k1': l1_lam_k1,
        'lam_q2': l1_lam_q2, 'lam_k2': l1_lam_k2, 'subln': l1_subln, 'w_out': l1_w_out,
        'w_router': l1_w_router, 'moe_w_gate': l1_moe_w_gate, 'moe_w_up': l1_moe_w_up,
        'moe_w_down': l1_moe_w_down,
    }
    n_ctx, ctx_seq, d = x_prompt.shape
    nb, lat_seq, _ = x_sample.shape
    assert n_ctx * ctx_seq == lat_seq, "segment layout needs equally sized modulation segments"
    dims = dict(nb=nb, lat_seq=lat_seq, n_ctx=n_ctx, ctx_seq=ctx_seq)
    t_lat = nb * lat_seq
    x = jnp.concatenate([x_sample.reshape(t_lat, d), x_prompt.reshape(n_ctx * ctx_seq, d)], axis=0)
    cond8 = jnp.zeros((SUBLANES, d), F32).at[:nb].set(c).at[nb].set(c_ctx)

    x, new_l0_ckv, new_l0_krope = _even_layer(x, cond8, even, cache_l0_ckv, cache_l0_krope, **dims)
    lambda_init = 0.8 - 0.6 * math.exp(-0.3 * 1)
    x_lat, x_ctx, new_l1_k, new_l1_v = _odd_layer(x, cond8, odd, cache_l1_k, cache_l1_v, lambda_init, **dims)

    y_sample = x_lat.reshape(nb, lat_seq, d)
    y_prompt = x_ctx.reshape(n_ctx, ctx_seq, d)
    return (y_prompt, y_sample, new_l0_ckv, new_l0_krope, new_l1_k, new_l1_v)
```

```python
import functools
import math

import numpy as np
import jax
import jax.numpy as jnp
from jax import lax
from jax.experimental import pallas as pl
from jax.experimental.pallas import tpu as pltpu

F32 = jnp.float32
BF16 = jnp.bfloat16

VMEM_LIMIT_BYTES = 56 * 1024 * 1024
LANES = 128
SUBLANES = 8
LOG2_E = math.log2(math.e)

GRID_W = 64
ROPE_BASE = 10000.0
NORM_EPS = 1e-6
HY_WIDTH = 512
HY_BANDS = 16
HY_FAST_DECAY_PCT = 0.3
HY_SLOW_DECAY_PCT = 1.5
HY_DECAY_TARGET = 1e-2
MLA_HEADS = 8
MLA_NOPE = 64
MLA_ROPE = 32
MLA_QK = MLA_NOPE + MLA_ROPE
MLA_V = 64
MLA_Q_RANK = 768
MLA_KV_RANK = 256
DIFF_HEADS = 8
DIFF_DH = 64
N_EXPERTS = 8


def _params(*sem):
    return pltpu.CompilerParams(dimension_semantics=sem, vmem_limit_bytes=VMEM_LIMIT_BYTES)


def _dot(a, b):
    return jnp.dot(a, b, preferred_element_type=F32)


def _dot_nt(a, b):
    return lax.dot_general(a, b, (((1,), (1,)), ((), ())), preferred_element_type=F32)


def _split_bf16(a):
    hi = a.astype(BF16)
    lo = (a - hi.astype(F32)).astype(BF16)
    return hi, lo


def _dot_f32(a, b):
    ah, al = _split_bf16(a)
    bh, bl = _split_bf16(b)
    return _dot(ah, bh) + (_dot(al, bh) + _dot(ah, bl))


def _rms(x, g, n=None):
    n = x.shape[-1] if n is None else n
    ms = jnp.sum(x * x, axis=-1, keepdims=True) * (1.0 / n)
    return x * lax.rsqrt(ms + NORM_EPS) * g


def _norm_mod(x, g, shift, scale):
    return _rms(x, g) * (1.0 + scale) + shift


def _silu(x):
    return x / (1.0 + jnp.exp(-x))


def _adaln_body(c_ref, w_ref, b_ref, o_ref):
    o_ref[...] = _dot_f32(_silu(c_ref[...]), w_ref[...]) + b_ref[...]


def _adaln(cond8, w_mod, b_mod):
    d, n = w_mod.shape
    tn = n // 4
    out = pl.pallas_call(
        _adaln_body,
        grid=(n // tn,),
        in_specs=[pl.BlockSpec((SUBLANES, d), lambda j: (0, 0)),
                  pl.BlockSpec((d, tn), lambda j: (0, j)),
                  pl.BlockSpec((1, tn), lambda j: (0, j))],
        out_specs=pl.BlockSpec((SUBLANES, tn), lambda j: (0, j)),
        out_shape=jax.ShapeDtypeStruct((SUBLANES, n), F32),
        compiler_params=_params("arbitrary"),
        name="adaln",
    )(cond8, w_mod, b_mod.reshape(1, n))
    return out.reshape(SUBLANES, 6, d)


def _two_part_specs(a, b, tm):
    na = a.shape[0] // tm
    cols = a.shape[1]
    return (pl.BlockSpec((tm, cols), lambda i: (jnp.minimum(i, na - 1), 0)),
            pl.BlockSpec((tm, cols), lambda i: (jnp.maximum(i - na, 0), 0)), na)


def _lin_in_body(xa_ref, xb_ref, mod_ref, g_ref, w_ref, o_ref, *, na):
    x = jnp.where(pl.program_id(0) < na, xa_ref[...], xb_ref[...])
    h = _norm_mod(x, g_ref[...], mod_ref[0:1, :], mod_ref[1:2, :])
    o_ref[...] = _dot(h.astype(BF16), w_ref[...])


def _lin_in(xa, xb, mod, g, w, *, tm, seg_len):
    d = xa.shape[1]
    t = xa.shape[0] + xb.shape[0]
    n = w.shape[1]
    spec_a, spec_b, na = _two_part_specs(xa, xb, tm)
    return pl.pallas_call(
        functools.partial(_lin_in_body, na=na),
        grid=(t // tm,),
        in_specs=[spec_a, spec_b,
                  pl.BlockSpec((None, 6, d), lambda i: (i * tm // seg_len, 0, 0)),
                  pl.BlockSpec((1, d), lambda i: (0, 0)),
                  pl.BlockSpec((d, n), lambda i: (0, 0))],
        out_specs=pl.BlockSpec((tm, n), lambda i: (i, 0)),
        out_shape=jax.ShapeDtypeStruct((t, n), F32),
        compiler_params=_params("arbitrary"),
        name="lin_in",
    )(xa, xb, mod, g.reshape(1, d), w)


def _hy_pre_body(z_ref, zp_ref, zn_ref, w_ref, b_ref, u_ref, x0_ref, *, tm, lat_tiles, tiles_per_seq):
    i = pl.program_id(0)
    z = z_ref[...]
    in_lat = i < lat_tiles
    has_prev = jnp.logical_and(in_lat, i % tiles_per_seq != 0)
    has_next = jnp.logical_and(in_lat, i % tiles_per_seq != tiles_per_seq - 1)
    prev_row = jnp.where(has_prev, zp_ref[SUBLANES - 1:SUBLANES, :], 0.0)
    next_row = jnp.where(has_next, zn_ref[0:1, :], 0.0)
    rows = lax.broadcasted_iota(jnp.int32, z.shape, 0)
    z_m = jnp.where(rows == 0, prev_row, pltpu.roll(z, 1, 0))
    z_p = jnp.where(rows == tm - 1, next_row, pltpu.roll(z, tm - 1, 0))
    zc = b_ref[...] + z_m * w_ref[0:1, :] + z * w_ref[1:2, :] + z_p * w_ref[2:3, :]
    c = HY_WIDTH
    x0_ref[...] = zc[:, :c]
    u_ref[...] = zc[:, 2 * c:] * zc[:, c:2 * c]


def _hy_pre(z, conv_w, conv_b, *, t_lat, lat_seq, ctx_seq):
    t = z.shape[0]
    tm = ctx_seq
    c3 = 3 * HY_WIDTH
    nb8 = t // SUBLANES
    body = functools.partial(_hy_pre_body, tm=tm, lat_tiles=t_lat // tm, tiles_per_seq=lat_seq // tm)
    return pl.pallas_call(
        body,
        grid=(t // tm,),
        in_specs=[pl.BlockSpec((tm, c3), lambda i: (i, 0)),
                  pl.BlockSpec((SUBLANES, c3), lambda i: (jnp.maximum(i * (tm // SUBLANES) - 1, 0), 0)),
                  pl.BlockSpec((SUBLANES, c3), lambda i: (jnp.minimum((i + 1) * (tm // SUBLANES), nb8 - 1), 0)),
                  pl.BlockSpec((3, c3), lambda i: (0, 0)),
                  pl.BlockSpec((1, c3), lambda i: (0, 0))],
        out_specs=[pl.BlockSpec((tm, HY_WIDTH), lambda i: (i, 0)),
                   pl.BlockSpec((tm, HY_WIDTH), lambda i: (i, 0))],
        out_shape=[jax.ShapeDtypeStruct((t, HY_WIDTH), F32),
                   jax.ShapeDtypeStruct((t, HY_WIDTH), F32)],
        compiler_params=_params("parallel"),
        name="hy_pre",
    )(z, z, z, conv_w, conv_b.reshape(1, c3))


def _filter_embedding(seq):
    t01 = np.linspace(0.0, 1.0, seq)[:, None]
    w = 2.0 * math.pi * np.arange(seq)[:, None] / seq
    f = np.linspace(1e-4, HY_BANDS - 1, HY_BANDS)[None, :]
    z = np.concatenate([t01, np.cos(f * w), -np.sin(f * w)], axis=-1)
    z_rev = np.concatenate([z[:1], z[:0:-1]], axis=0)
    zz = np.concatenate([z, z_rev], axis=0)
    out = np.zeros((2 * seq, LANES), np.float32)
    out[:, :zz.shape[1]] = zz
    return out


def _filter_body(zz_ref, dl_ref, w1_ref, b1_ref, f1_ref, w2_ref, b2_ref, f2_ref, w3_ref,
                 k_ref, s_ref, *, tm, lat_tiles, ctx_tiles):
    i = pl.program_id(0)
    zz = zz_ref[...]
    h = jnp.sin(f1_ref[...] * (_dot_f32(zz, w1_ref[...]) + b1_ref[...]))
    h = jnp.sin(f2_ref[...] * (_dot_f32(h, w2_ref[...]) + b2_ref[...]))
    h = _dot_f32(h, w3_ref[...])
    is_bwd = jnp.logical_or(jnp.logical_and(i >= lat_tiles // 2, i < lat_tiles),
                            i >= lat_tiles + ctx_tiles // 2)
    first_bwd = jnp.logical_or(i == lat_tiles // 2, i == lat_tiles + ctx_tiles // 2)
    window = jnp.exp(-zz[:, 0:1] * dl_ref[...])
    k = jnp.where(is_bwd, h[:, HY_WIDTH:], h[:, :HY_WIDTH]) * window
    rows = lax.broadcasted_iota(jnp.int32, k.shape, 0)
    k = jnp.where(jnp.logical_and(first_bwd, rows == 0), 0.0, k)
    k_ref[...] = k
    s = jnp.sum(jnp.abs(k), axis=0, keepdims=True)

    @pl.when(i == 0)
    def _():
        s_ref[...] = jnp.zeros_like(s_ref)

    @pl.when(i < lat_tiles)
    def _():
        s_ref[0:1, :] += s

    @pl.when(i >= lat_tiles)
    def _():
        s_ref[1:2, :] += s


def _hyena_filters(p, *, lat_seq, ctx_seq):
    tm = ctx_seq
    zz = jnp.asarray(np.concatenate([_filter_embedding(lat_seq), _filter_embedding(ctx_seq)], axis=0))
    rows = zz.shape[0]
    max_decay = math.log(HY_DECAY_TARGET) / HY_FAST_DECAY_PCT
    min_decay = math.log(HY_DECAY_TARGET) / HY_SLOW_DECAY_PCT
    deltas = jnp.asarray(np.abs(np.linspace(min_decay, max_decay, HY_WIDTH))[None, :].astype(np.float32))
    emb, hid = p['hy_fw1'].shape

    def pad2(a, r, c):
        return jnp.zeros((r, c), F32).at[:a.shape[0], :a.shape[1]].set(a)

    w1 = pad2(p['hy_fw1'], LANES, LANES)
    b1 = pad2(p['hy_fb1'][None, :], 1, LANES)
    f1 = pad2(p['hy_freq1'][None, :], 1, LANES)
    w2 = pad2(p['hy_fw2'], LANES, LANES)
    b2 = pad2(p['hy_fb2'][None, :], 1, LANES)
    f2 = pad2(p['hy_freq2'][None, :], 1, LANES)
    w3 = pad2(p['hy_fw3'], LANES, 2 * HY_WIDTH)
    body = functools.partial(_filter_body, tm=tm, lat_tiles=2 * lat_seq // tm, ctx_tiles=2 * ctx_seq // tm)
    full = lambda i: (0, 0)
    return pl.pallas_call(
        body,
        grid=(rows // tm,),
        in_specs=[pl.BlockSpec((tm, LANES), lambda i: (i, 0)),
                  pl.BlockSpec((1, HY_WIDTH), full),
                  pl.BlockSpec((LANES, LANES), full), pl.BlockSpec((1, LANES), full), pl.BlockSpec((1, LANES), full),
                  pl.BlockSpec((LANES, LANES), full), pl.BlockSpec((1, LANES), full), pl.BlockSpec((1, LANES), full),
                  pl.BlockSpec((LANES, 2 * HY_WIDTH), full)],
        out_specs=[pl.BlockSpec((tm, HY_WIDTH), lambda i: (i, 0)),
                   pl.BlockSpec((SUBLANES, HY_WIDTH), full)],
        out_shape=[jax.ShapeDtypeStruct((rows, HY_WIDTH), F32),
                   jax.ShapeDtypeStruct((SUBLANES, HY_WIDTH), F32)],
        compiler_params=_params("arbitrary"),
        name="hy_filter",
    )(zz, deltas, w1, b1, f1, w2, b2, f2, w3)


def _stack_complex(z):
    return np.block([[z.real, -z.imag], [z.imag, z.real]])


def _dft_consts_two_level(seq, n1, n2):
    n = 2 * seq
    assert n1 * n2 == n
    a1 = np.arange(n1)
    f1_full = np.exp(-2j * np.pi * np.outer(a1, a1) / n1)
    f1_u = np.concatenate([f1_full.real, f1_full.imag], axis=0)[:, :n1 // 2]
    f1_k = np.concatenate([f1_full.real, f1_full.imag], axis=0)
    a2 = np.arange(n2)
    f = a1[:, None, None] + n1 * a2[None, :, None]
    z = np.exp(-2j * np.pi * (f * a2[None, None, :]) / n)
    mf = np.stack([_stack_complex(z[i]) for i in range(n1)])
    mi = np.stack([_stack_complex(np.conj(z[i]).T) for i in range(n1)])
    g = np.exp(2j * np.pi * np.outer(a1[:n1 // 2], a1) / n1) / n
    gc, gs = g.real, -g.imag
    as32 = lambda a: jnp.asarray(a.astype(np.float32))
    return as32(f1_u), as32(f1_k), as32(mf), as32(mi), as32(gc), as32(gs)


def _dft_consts_one_level(seq):
    n = 2 * seq
    a = np.arange(n)
    z = np.exp(-2j * np.pi * np.outer(a, a) / n)
    mf = np.concatenate([z.real, z.imag], axis=0)
    zi = np.exp(2j * np.pi * np.outer(a[:seq], a) / n) / n
    mi = np.concatenate([zi.real, -zi.imag], axis=1)
    as32 = lambda a: jnp.asarray(a.astype(np.float32))
    return as32(mf), as32(mi)


def _lmat_body(f_ref, x_ref, sc_ref, or_ref, oi_ref):
    x = (x_ref[...] * (1.0 / sc_ref[...])).astype(BF16)
    o = _dot(f_ref[...], x)
    h = o.shape[0] // 2
    or_ref[...] = o[:h].astype(BF16)
    oi_ref[...] = o[h:].astype(BF16)


def _lmat(f, x, scale_row, *, tn):
    g, k, cols = x.shape
    m2 = f.shape[0]
    m = m2 // 2
    return pl.pallas_call(
        _lmat_body,
        grid=(g, cols // tn),
        in_specs=[pl.BlockSpec((m2, k), lambda b, j: (0, 0)),
                  pl.BlockSpec((None, k, tn), lambda b, j: (b, 0, j)),
                  pl.BlockSpec((1, tn), lambda b, j: (0, 0))],
        out_specs=[pl.BlockSpec((None, m, tn), lambda b, j: (b, 0, j)),
                   pl.BlockSpec((None, m, tn), lambda b, j: (b, 0, j))],
        out_shape=[jax.ShapeDtypeStruct((g, m, cols), BF16),
                   jax.ShapeDtypeStruct((g, m, cols), BF16)],
        compiler_params=_params("parallel", "parallel"),
        name="hy_dft1",
    )(f.astype(BF16), x, scale_row)


SPEC_GROUP = 4


def _spec_fwd_body(mf_ref, ar_ref, ai_ref, kr_ref, ki_ref):
    for g in range(SPEC_GROUP):
        a = jnp.concatenate([ar_ref[g], ai_ref[g]], axis=0)
        x = _dot(mf_ref[g], a)
        h = x.shape[0] // 2
        kr_ref[g] = x[:h]
        ki_ref[g] = x[h:]


def _spec_fwd(mf, ar, ai):
    n1, n2, c = ar.shape
    spec = pl.BlockSpec((SPEC_GROUP, n2, c), lambda i: (i, 0, 0))
    return pl.pallas_call(
        _spec_fwd_body,
        grid=(n1 // SPEC_GROUP,),
        in_specs=[pl.BlockSpec((SPEC_GROUP, 2 * n2, 2 * n2), lambda i: (i, 0, 0)), spec, spec],
        out_specs=[spec, spec],
        out_shape=[jax.ShapeDtypeStruct((n1, n2, c), F32)] * 2,
        compiler_params=_params("parallel"),
        name="hy_spec_filter",
    )(mf, ar, ai)


def _spec_mul_body(mf_ref, mi_ref, kr_ref, ki_ref, ar_ref, ai_ref, br_ref, bi_ref):
    for g in range(SPEC_GROUP):
        a = jnp.concatenate([ar_ref[g], ai_ref[g]], axis=0)
        x = _dot(mf_ref[g], a)
        h = x.shape[0] // 2
        xr, xi = x[:h], x[h:]
        kr, ki = kr_ref[g], ki_ref[g]
        y = jnp.concatenate([xr * kr - xi * ki, xr * ki + xi * kr], axis=0).astype(BF16)
        b = _dot(mi_ref[g], y)
        br_ref[g] = b[:h].astype(BF16)
        bi_ref[g] = b[h:].astype(BF16)


def _spec_mul(mf, mi, kr, ki, ar, ai):
    nb, n1, n2, c = ar.shape
    mspec = pl.BlockSpec((SPEC_GROUP, 2 * n2, 2 * n2), lambda i, b: (i, 0, 0))
    kspec = pl.BlockSpec((SPEC_GROUP, n2, c), lambda i, b: (i, 0, 0))
    aspec = pl.BlockSpec((None, SPEC_GROUP, n2, c), lambda i, b: (b, i, 0, 0))
    return pl.pallas_call(
        _spec_mul_body,
        grid=(n1 // SPEC_GROUP, nb),
        in_specs=[mspec, mspec, kspec, kspec, aspec, aspec],
        out_specs=[aspec, aspec],
        out_shape=[jax.ShapeDtypeStruct((nb, n1, n2, c), BF16)] * 2,
        compiler_params=_params("parallel", "arbitrary"),
        name="hy_spec_mul",
    )(mf, mi, kr, ki, ar, ai)


def _idft1_body(gc_ref, gs_ref, br_ref, bi_ref, o_ref):
    o_ref[...] = _dot(gc_ref[...], br_ref[...]) + _dot(gs_ref[...], bi_ref[...])


def _idft1(gc, gs, br, bi, *, tn):
    nb, n1, cols = br.shape
    m = gc.shape[0]
    gspec = pl.BlockSpec((m, n1), lambda b, j: (0, 0))
    bspec = pl.BlockSpec((None, n1, tn), lambda b, j: (b, 0, j))
    return pl.pallas_call(
        _idft1_body,
        grid=(nb, cols // tn),
        in_specs=[gspec, gspec, bspec, bspec],
        out_specs=pl.BlockSpec((None, m, tn), lambda b, j: (b, 0, j)),
        out_shape=jax.ShapeDtypeStruct((nb, m, cols), F32),
        compiler_params=_params("parallel", "parallel"),
        name="hy_idft1",
    )(gc.astype(BF16), gs.astype(BF16), br, bi)


def _ctx_filter_body(mf_ref, k_ref, sc_ref, kf_ref):
    kf_ref[...] = _dot(mf_ref[...], (k_ref[...] * (1.0 / sc_ref[...])).astype(BF16))


def _ctx_conv_body(mf_ref, mi_ref, kf_ref, u_ref, o_ref):
    x = _dot(mf_ref[...], u_ref[...].astype(BF16))
    h = x.shape[0] // 2
    xr, xi = x[:h], x[h:]
    kr, ki = kf_ref[:h, :], kf_ref[h:, :]
    y = jnp.concatenate([xr * kr - xi * ki, xr * ki + xi * kr], axis=0).astype(BF16)
    o_ref[...] = _dot(mi_ref[...], y)


def _hyena_ctx(k_raw, k_norm1, u, *, seq, nseq, u_row0):
    mf, mi = _dft_consts_one_level(seq)
    n = 2 * seq
    c = u.shape[1]
    kf = pl.pallas_call(
        _ctx_filter_body,
        out_shape=jax.ShapeDtypeStruct((2 * n, c), F32),
        compiler_params=_params(),
        name="hy_ctx_filter",
    )(mf.astype(BF16), k_raw, k_norm1)
    full = lambda s: (0, 0)
    return pl.pallas_call(
        _ctx_conv_body,
        grid=(nseq,),
        in_specs=[pl.BlockSpec((2 * n, seq), full),
                  pl.BlockSpec((seq, 2 * n), full),
                  pl.BlockSpec((2 * n, c), full),
                  pl.BlockSpec((seq, c), lambda s: (u_row0 // seq + s, 0))],
        out_specs=pl.BlockSpec((seq, c), lambda s: (s, 0)),
        out_shape=jax.ShapeDtypeStruct((nseq * seq, c), F32),
        compiler_params=_params("parallel"),
        name="hy_ctx_conv",
    )(mf[:, :seq].astype(BF16), mi.astype(BF16), kf, u)


def _hyena_lat(k_raw, k_norm1, u, *, seq, nb):
    c = u.shape[1]
    n1, n2 = 64, 2 * seq // 64
    f1_u, f1_k, mf, mi, gc, gs = _dft_consts_two_level(seq, n1, n2)
    mf = mf.astype(BF16)
    mi = mi.astype(BF16)
    cols = n2 * c
    tn = 4096
    rep = tn // c
    ones_row = jnp.ones((1, tn), F32)
    akr, aki = _lmat(f1_k, k_raw.reshape(1, n1, cols), jnp.tile(k_norm1, (1, rep)), tn=tn)
    kr, ki = _spec_fwd(mf, akr.reshape(n1, n2, c), aki.reshape(n1, n2, c))
    ar, ai = _lmat(f1_u, u.reshape(nb, n1 // 2, cols), ones_row, tn=tn)
    br, bi = _spec_mul(mf, mi, kr, ki, ar.reshape(nb, n1, n2, c), ai.reshape(nb, n1, n2, c))
    y = _idft1(gc, gs, br.reshape(nb, n1, cols), bi.reshape(nb, n1, cols), tn=tn)
    return y.reshape(nb * seq, c)


def _rope_tables(seq, rope_dims, lane_offsets, pad_rows):
    rows = seq // GRID_W
    rr, cc = np.meshgrid(np.arange(rows), np.arange(GRID_W), indexing='ij')
    pos = (rr.reshape(-1).astype(np.float64), cc.reshape(-1).astype(np.float64))
    half = rope_dims // 2
    q = half // 2
    inv_freq = ROPE_BASE ** (-np.arange(0, half, 2, dtype=np.float64) / half)
    cos_t = np.ones((seq + pad_rows, LANES), np.float64)
    sin_a = np.zeros((seq + pad_rows, LANES), np.float64)
    sin_b = np.zeros((seq + pad_rows, LANES), np.float64)
    for off in lane_offsets:
        for axis in range(2):
            ang = pos[axis][:, None] * inv_freq[None, :]
            base = off + axis * half
            cos_t[:seq, base:base + q] = np.cos(ang)
            cos_t[:seq, base + q:base + half] = np.cos(ang)
            sin_b[:seq, base:base + q] = -np.sin(ang)
            sin_a[:seq, base + q:base + half] = np.sin(ang)
    as32 = lambda a: jnp.asarray(a.astype(np.float32))
    return as32(cos_t), as32(sin_a), as32(sin_b)


def _rope(x, cos_t, sin_a, sin_b, shift):
    return x * cos_t + pltpu.roll(x, shift, 1) * sin_a + pltpu.roll(x, LANES - shift, 1) * sin_b


def _mla_q_body(cq_ref, ckv_ref, qa_ref, kva_ref, wuq_ref, qn_ref, cos_ref, sa_ref, sb_ref,
                q_ref, ckvn_ref, *, scale):
    ckvn_ref[...] = _rms(ckv_ref[...], kva_ref[...])
    cqn = _rms(cq_ref[...], qa_ref[...])
    q = _dot(cqn.astype(BF16), wuq_ref[...])
    cos_t, sin_a, sin_b = cos_ref[...], sa_ref[...], sb_ref[...]
    g = qn_ref[...]
    for h in range(MLA_HEADS):
        qh = _rms(q[:, h * LANES:(h + 1) * LANES], g, MLA_QK)
        qh = _rope(qh, cos_t, sin_a, sin_b, MLA_ROPE // 4)
        q_ref[:, h * LANES:(h + 1) * LANES] = (qh * scale).astype(BF16)


def _pad_heads(w, heads, width):
    k = w.shape[0]
    w3 = w.reshape(k, heads, width)
    return jnp.zeros((k, heads, LANES), w.dtype).at[:, :, :width].set(w3).reshape(k, heads * LANES)


def _mla_q(z, p, tables, *, tm, t_lat, lat_seq):
    t = z.shape[0]
    cos_t, sin_a, sin_b = tables
    wuq = _pad_heads(p['w_uq'], MLA_HEADS, MLA_QK).astype(BF16)
    qn = jnp.zeros((1, LANES), F32).at[0, :MLA_QK].set(p['q_norm'])
    pos_blocks = lat_seq // tm
    lat_tiles = t_lat // tm
    tspec = pl.BlockSpec((tm, LANES), lambda i: (jnp.where(i < lat_tiles, i % pos_blocks, pos_blocks), 0))
    full = lambda i: (0, 0)
    cq_blk = (3 * HY_WIDTH) // MLA_Q_RANK
    ckv_blk = (3 * HY_WIDTH + MLA_Q_RANK) // MLA_KV_RANK
    return pl.pallas_call(
        functools.partial(_mla_q_body, scale=MLA_QK ** -0.5 * LOG2_E),
        grid=(t // tm,),
        in_specs=[pl.BlockSpec((tm, MLA_Q_RANK), lambda i: (i, cq_blk)),
                  pl.BlockSpec((tm, MLA_KV_RANK), lambda i: (i, ckv_blk)),
                  pl.BlockSpec((1, MLA_Q_RANK), full),
                  pl.BlockSpec((1, MLA_KV_RANK), full),
                  pl.BlockSpec((MLA_Q_RANK, MLA_HEADS * LANES), full),
                  pl.BlockSpec((1, LANES), full),
                  tspec, tspec, tspec],
        out_specs=[pl.BlockSpec((tm, MLA_HEADS * LANES), lambda i: (i, 0)),
                   pl.BlockSpec((tm, MLA_KV_RANK), lambda i: (i, 0))],
        out_shape=[jax.ShapeDtypeStruct((t, MLA_HEADS * LANES), BF16),
                   jax.ShapeDtypeStruct((t, MLA_KV_RANK), F32)],
        compiler_params=_params("parallel"),
        name="mla_q",
    )(z, z, p['qa_norm'].reshape(1, -1), p['kva_norm'].reshape(1, -1), wuq, qn, cos_t, sin_a, sin_b)


def _mla_kv_body(ckvn_ref, kr_ref, wk_ref, wv_ref, kn_ref, cos_ref, sa_ref, sb_ref, k_ref, v_ref):
    c = ckvn_ref[...].astype(BF16)
    k = _dot(c, wk_ref[...])
    v_ref[...] = _dot(c, wv_ref[...]).astype(BF16)
    kr = pltpu.roll(kr_ref[...], MLA_NOPE, 1)
    cos_t, sin_a, sin_b = cos_ref[...], sa_ref[...], sb_ref[...]
    g = kn_ref[...]
    for h in range(MLA_HEADS):
        kh = _rms(k[:, h * LANES:(h + 1) * LANES] + kr, g, MLA_QK)
        kh = _rope(kh, cos_t, sin_a, sin_b, MLA_ROPE // 4)
        k_ref[:, h * LANES:(h + 1) * LANES] = kh.astype(BF16)


def _mla_kv(ckvn_rows, kr_rows, p, tables, *, tm, nb, past, lat_seq):
    r = ckvn_rows.shape[0]
    cos_t, sin_a, sin_b = tables
    w = p['w_ukv'].reshape(MLA_KV_RANK, MLA_HEADS, MLA_NOPE + MLA_V)
    wk = _pad_heads(w[:, :, :MLA_NOPE].reshape(MLA_KV_RANK, -1), MLA_HEADS, MLA_NOPE).astype(BF16)
    wv = w[:, :, MLA_NOPE:].reshape(MLA_KV_RANK, MLA_HEADS * MLA_V).astype(BF16)
    kn = jnp.zeros((1, LANES), F32).at[0, :MLA_QK].set(p['k_norm'])
    per_b = (past + lat_seq) // tm
    past_tiles = past // tm
    pos_blocks = lat_seq // tm
    lat_tiles = nb * per_b

    def tmap(i):
        j = i % per_b
        is_pos = jnp.logical_and(i < lat_tiles, j >= past_tiles)
        return (jnp.where(is_pos, j - past_tiles, pos_blocks), 0)

    tspec = pl.BlockSpec((tm, LANES), tmap)
    full = lambda i: (0, 0)
    return pl.pallas_call(
        _mla_kv_body,
        grid=(r // tm,),
        in_specs=[pl.BlockSpec((tm, MLA_KV_RANK), lambda i: (i, 0)),
                  pl.BlockSpec((tm, LANES), lambda i: (i, 0)),
                  pl.BlockSpec((MLA_KV_RANK, MLA_HEADS * LANES), full),
                  pl.BlockSpec((MLA_KV_RANK, MLA_HEADS * MLA_V), full),
                  pl.BlockSpec((1, LANES), full),
                  tspec, tspec, tspec],
        out_specs=[pl.BlockSpec((tm, MLA_HEADS * LANES), lambda i: (i, 0)),
                   pl.BlockSpec((tm, MLA_HEADS * MLA_V), lambda i: (i, 0))],
        out_shape=[jax.ShapeDtypeStruct((r, MLA_HEADS * LANES), BF16),
                   jax.ShapeDtypeStruct((r, MLA_HEADS * MLA_V), BF16)],
        compiler_params=_params("parallel"),
        name="mla_kv",
    )(ckvn_rows, kr_rows, wk, wv, kn, cos_t, sin_a, sin_b)


ATT_CHUNK = 512
ATT_UNIT_ROWS = 256
ATT_TQ = 1024


def _fill_vaug(vaug_ref, v_refs):
    off = 0
    for v in v_refs:
        n = v.shape[0]
        vaug_ref[off:off + n, :LANES] = v[...]
        off += n
    vaug_ref[:, LANES:] = jnp.ones((vaug_ref.shape[0], LANES), BF16)


def _softmax_pv(units, vaug_ref, s_ref, n_keys):
    chunk = min(ATT_CHUNK, n_keys)
    chunks = [slice(c * chunk, (c + 1) * chunk) for c in range(n_keys // chunk)]

    def scores(u, rows, m_lane):
        q, k_of = units[u]
        s = _dot_nt(q, k_of(rows))
        s_ref[u % 2, :, rows] = s
        for j in range(chunk // LANES):
            blk = s[:, j * LANES:(j + 1) * LANES]
            m_lane = blk if m_lane is None else jnp.maximum(m_lane, blk)
        return m_lane

    def values(u, rows, m, acc):
        p = jnp.exp2(s_ref[u % 2, :, rows] - m).astype(BF16)
        d = _dot(p, vaug_ref[rows, :])
        return d if acc is None else acc + d

    outs = []
    m_lane = None
    for rows in chunks:
        m_lane = scores(0, rows, m_lane)
    for u in range(len(units)):
        m = jnp.max(m_lane, axis=-1, keepdims=True)
        acc, m_lane = None, None
        for rows in chunks:
            acc = values(u, rows, m, acc)
            if u + 1 < len(units):
                m_lane = scores(u + 1, rows, m_lane)
        outs.append(acc)
    return outs


def _mla_attn_body(prev_ref, q_ref, k_ref, v_ref, o_ref, vaug_ref, s_ref):
    @pl.when(pl.program_id(2) == 0)
    def _():
        _fill_vaug(vaug_ref, [v_ref])

    n_keys = k_ref.shape[0]
    tq = q_ref.shape[0]
    ur = s_ref.shape[1]
    units = []
    for r0 in range(0, tq, ur):
        for hh in range(2):
            sl = slice(hh * LANES, (hh + 1) * LANES)
            units.append((q_ref[r0:r0 + ur, sl], lambda rows, sl=sl: k_ref[rows, sl]))
    res = [r[:, :LANES] / r[:, LANES:] for r in _softmax_pv(units, vaug_ref, s_ref, n_keys)]
    lane = lax.broadcasted_iota(jnp.int32, res[0].shape, 1)
    for i, r0 in enumerate(range(0, tq, ur)):
        o_ref[r0:r0 + ur, :] = jnp.where(lane < MLA_V, res[2 * i], res[2 * i + 1]).astype(BF16)


def _mla_attn(prev, q, k, v, *, tq, n_seq, seq_q, seq_k, q_row0, k_row0):
    hp = MLA_HEADS // 2
    nq = seq_q // tq
    qb0, kb0 = q_row0 // tq, k_row0 // seq_k
    return pl.pallas_call(
        _mla_attn_body,
        grid=(n_seq, hp, nq),
        in_specs=[pl.BlockSpec(memory_space=pl.ANY),
                  pl.BlockSpec((tq, 2 * LANES), lambda s, h, i: (qb0 + s * nq + i, h)),
                  pl.BlockSpec((seq_k, 2 * LANES), lambda s, h, i: (kb0 + s, h)),
                  pl.BlockSpec((seq_k, 2 * MLA_V), lambda s, h, i: (kb0 + s, h))],
        out_specs=pl.BlockSpec((tq, 2 * MLA_V), lambda s, h, i: (qb0 + s * nq + i, h)),
        out_shape=jax.ShapeDtypeStruct(prev.shape, prev.dtype),
        input_output_aliases={0: 0},
        scratch_shapes=[pltpu.VMEM((seq_k, 2 * LANES), BF16),
                        pltpu.VMEM((2, min(tq, ATT_UNIT_ROWS), seq_k), F32)],
        compiler_params=_params("arbitrary", "arbitrary", "arbitrary"),
        name="mla_attn",
    )(prev, q, k, v)


def _diff_attn_body(prev_ref, *refs, n_seg, lambda_init):
    q_ref, lam_ref, sub_ref = refs[0], refs[1], refs[2]
    k_refs = refs[3:3 + n_seg]
    v_refs = refs[3 + n_seg:3 + 2 * n_seg]
    o_ref, kcat_ref, vaug_ref, s_ref = refs[3 + 2 * n_seg:]

    @pl.when(pl.program_id(2) == 0)
    def _():
        _fill_vaug(vaug_ref, v_refs)
        off = 0
        for k in k_refs:
            kcat_ref[off:off + k.shape[0], :] = k[...]
            off += k.shape[0]

    lp = lam_ref[...]
    lam = (jnp.exp(jnp.sum(lp[0:1] * lp[1:2], axis=-1, keepdims=True))
           - jnp.exp(jnp.sum(lp[2:3] * lp[3:4], axis=-1, keepdims=True)) + lambda_init)
    n_keys = kcat_ref.shape[0]
    k_of = lambda rows: kcat_ref[rows, :]
    tq = q_ref.shape[0]
    ur = s_ref.shape[1]
    units = []
    for r0 in range(0, tq, ur):
        q = q_ref[r0:r0 + ur, :].astype(F32)
        lane = lax.broadcasted_iota(jnp.int32, q.shape, 1)
        units.append((jnp.where(lane < DIFF_DH, q, 0.0).astype(BF16), k_of))
        units.append((jnp.where(lane < DIFF_DH, 0.0, q).astype(BF16), k_of))
    res = _softmax_pv(units, vaug_ref, s_ref, n_keys)
    for i, r0 in enumerate(range(0, tq, ur)):
        r1, r2 = res[2 * i], res[2 * i + 1]
        o = r1[:, :LANES] / r1[:, LANES:] - (lam / r2[:, LANES:]) * r2[:, :LANES]
        o_ref[r0:r0 + ur, :] = (_rms(o, sub_ref[...]) * (1.0 - lambda_init)).astype(BF16)


def _diff_attn(prev, q, k_new, v_new, k_cache, v_cache, lam_p, subln, *, tq, n_seq, seq_q, q_row0, lambda_init):
    nq = seq_q // tq
    qb0 = q_row0 // tq
    sb0 = q_row0 // seq_q
    d = 2 * DIFF_DH
    new_spec = pl.BlockSpec((seq_q, d), lambda s, h, i: (sb0 + s, h))
    if k_cache is None:
        n_seg, k_args, v_args, k_specs, v_specs = 1, [k_new], [v_new], [new_spec], [new_spec]
        n_keys = seq_q
    else:
        past = k_cache.shape[2]
        c_spec = pl.BlockSpec((None, None, past, d), lambda s, h, i: (s, h, 0, 0))
        n_seg, k_args, v_args = 2, [k_cache, k_new], [v_cache, v_new]
        k_specs, v_specs = [c_spec, new_spec], [c_spec, new_spec]
        n_keys = past + seq_q
    return pl.pallas_call(
        functools.partial(_diff_attn_body, n_seg=n_seg, lambda_init=lambda_init),
        grid=(n_seq, DIFF_HEADS, nq),
        in_specs=[pl.BlockSpec(memory_space=pl.ANY),
                  pl.BlockSpec((tq, d), lambda s, h, i: (qb0 + s * nq + i, h)),
                  pl.BlockSpec((4, DIFF_DH), lambda s, h, i: (0, 0)),
                  pl.BlockSpec((1, d), lambda s, h, i: (0, 0))] + k_specs + v_specs,
        out_specs=pl.BlockSpec((tq, d), lambda s, h, i: (qb0 + s * nq + i, h)),
        out_shape=jax.ShapeDtypeStruct(prev.shape, prev.dtype),
        input_output_aliases={0: 0},
        scratch_shapes=[pltpu.VMEM((n_keys, d), BF16), pltpu.VMEM((n_keys, 2 * LANES), BF16),
                        pltpu.VMEM((2, min(tq, ATT_UNIT_ROWS), n_keys), F32)],
        compiler_params=_params("arbitrary", "arbitrary", "arbitrary"),
        name="diff_attn",
    )(prev, q, lam_p, subln.reshape(1, d), *k_args, *v_args)


def _proj_res_body(*refs, n_in, gate_row):
    x_ref, mod_ref, o_ref = refs[0], refs[1], refs[-1]
    acc = None
    for j in range(n_in):
        a_ref, w_ref = refs[2 + 2 * j], refs[3 + 2 * j]
        d = _dot(a_ref[...].astype(BF16), w_ref[...])
        acc = d if acc is None else acc + d
    o_ref[...] = x_ref[...] + mod_ref[gate_row:gate_row + 1, :] * acc


def _proj_res(x, mod, acts, ws, *, tm, seg_len, gate_row):
    t, d = x.shape
    in_specs = [pl.BlockSpec((tm, d), lambda i: (i, 0)),
                pl.BlockSpec((None, 6, d), lambda i: (i * tm // seg_len, 0, 0))]
    args = [x, mod]
    for a, w in zip(acts, ws):
        in_specs += [pl.BlockSpec((tm, a.shape[1]), lambda i: (i, 0)),
                     pl.BlockSpec(w.shape, lambda i: (0, 0))]
        args += [a, w]
    return pl.pallas_call(
        functools.partial(_proj_res_body, n_in=len(acts), gate_row=gate_row),
        grid=(t // tm,),
        in_specs=in_specs,
        out_specs=pl.BlockSpec((tm, d), lambda i: (i, 0)),
        out_shape=jax.ShapeDtypeStruct((t, d), F32),
        compiler_params=_params("parallel"),
        name="proj_res",
    )(*args)


def _even_out_body(xa_ref, xb_ref, mod_ref, ca_ref, cb_ref, u_ref, x0_ref, db_ref, o_ref, wa_ref, wb_ref,
                   out_ref, *, na):
    first = pl.program_id(0) < na
    x = jnp.where(first, xa_ref[...], xb_ref[...])
    conv = jnp.where(first, ca_ref[...], cb_ref[...])
    y_hy = (conv + u_ref[...] * db_ref[...]) * x0_ref[...]
    acc = _dot(y_hy.astype(BF16), wa_ref[...]) + _dot(o_ref[...], wb_ref[...])
    out_ref[...] = x + mod_ref[2:3, :] * acc


def _even_out(xa, xb, mod, conv_a, conv_b, u, x0, dbias, o, w_hy, w_att, *, tm, seg_len):
    d = xa.shape[1]
    t = xa.shape[0] + xb.shape[0]
    c = u.shape[1]
    xa_spec, xb_spec, na = _two_part_specs(xa, xb, tm)
    ca_spec, cb_spec, na_c = _two_part_specs(conv_a, conv_b, tm)
    assert na == na_c
    row = lambda cols: pl.BlockSpec((tm, cols), lambda i: (i, 0))
    full = lambda a: pl.BlockSpec(a.shape, lambda i: (0, 0))
    return pl.pallas_call(
        functools.partial(_even_out_body, na=na),
        grid=(t // tm,),
        in_specs=[xa_spec, xb_spec,
                  pl.BlockSpec((None, 6, d), lambda i: (i * tm // seg_len, 0, 0)),
                  ca_spec, cb_spec, row(c), row(c), full(dbias), row(o.shape[1]), full(w_hy), full(w_att)],
        out_specs=row(d),
        out_shape=jax.ShapeDtypeStruct((t, d), F32),
        compiler_params=_params("arbitrary"),
        name="even_out",
    )(xa, xb, mod, conv_a, conv_b, u, x0, dbias, o, w_hy, w_att)


def _ffn_body(x_ref, mod_ref, g_ref, wg_ref, wu_ref, wd_ref, o_ref, h_ref, acc_ref):
    f = pl.program_id(1)

    @pl.when(f == 0)
    def _():
        h_ref[...] = _norm_mod(x_ref[...], g_ref[...], mod_ref[3:4, :], mod_ref[4:5, :]).astype(BF16)
        acc_ref[...] = jnp.zeros_like(acc_ref)

    h = h_ref[...]
    a = _silu(_dot(h, wg_ref[...])) * _dot(h, wu_ref[...])
    acc_ref[...] += _dot(a.astype(BF16), wd_ref[...])

    @pl.when(f == pl.num_programs(1) - 1)
    def _():
        o_ref[...] = x_ref[...] + mod_ref[5:6, :] * acc_ref[...]


def _ffn(x, mod, g, wg, wu, wd, *, tm, tf, seg_len):
    t, d = x.shape
    ff = wg.shape[1]
    return pl.pallas_call(
        _ffn_body,
        grid=(t // tm, ff // tf),
        in_specs=[pl.BlockSpec((tm, d), lambda i, f: (i, 0)),
                  pl.BlockSpec((None, 6, d), lambda i, f: (i * tm // seg_len, 0, 0)),
                  pl.BlockSpec((1, d), lambda i, f: (0, 0)),
                  pl.BlockSpec((d, tf), lambda i, f: (0, f)),
                  pl.BlockSpec((d, tf), lambda i, f: (0, f)),
                  pl.BlockSpec((tf, d), lambda i, f: (f, 0))],
        out_specs=pl.BlockSpec((tm, d), lambda i, f: (i, 0)),
        out_shape=jax.ShapeDtypeStruct((t, d), F32),
        scratch_shapes=[pltpu.VMEM((tm, d), BF16), pltpu.VMEM((tm, d), F32)],
        compiler_params=_params("parallel", "arbitrary"),
        name="ffn",
    )(x, mod, g.reshape(1, d), wg, wu, wd)


def _group_ms(x, gmat):
    hi, lo = _split_bf16(x * x)
    return (_dot(hi, gmat) + _dot(lo, gmat)) * (1.0 / DIFF_DH)


def _qkv_body(x_ref, mod_ref, g_ref, w_ref, gm_ref, qn_ref, kn_ref, cos_ref, sa_ref, sb_ref,
              q_ref, k_ref, v_ref, kf_ref, vf_ref, *, scale):
    h = _norm_mod(x_ref[...], g_ref[...], mod_ref[0:1, :], mod_ref[1:2, :]).astype(BF16)
    z = _dot(h, w_ref[...])
    hd = DIFF_HEADS * 2 * DIFF_DH
    cos_t, sin_a, sin_b = cos_ref[...], sa_ref[...], sb_ref[...]
    gm = gm_ref[...]
    for hh in range(DIFF_HEADS):
        sl = slice(hh * LANES, (hh + 1) * LANES)
        qh = z[:, hh * LANES:(hh + 1) * LANES]
        qh = qh * lax.rsqrt(_group_ms(qh, gm) + NORM_EPS) * qn_ref[...]
        q_ref[:, sl] = (_rope(qh, cos_t, sin_a, sin_b, DIFF_DH // 4) * scale).astype(BF16)
        kh = z[:, hd + hh * LANES:hd + (hh + 1) * LANES]
        kh = kh * lax.rsqrt(_group_ms(kh, gm) + NORM_EPS) * kn_ref[...]
        kf_ref[:, sl] = kh
        k_ref[:, sl] = _rope(kh, cos_t, sin_a, sin_b, DIFF_DH // 4).astype(BF16)
    v = z[:, 2 * hd:]
    vf_ref[...] = v
    v_ref[...] = v.astype(BF16)


def _qkv(x, mod, p, tables, *, tm, seg_len, t_lat, lat_seq):
    t, d = x.shape
    hd = DIFF_HEADS * 2 * DIFF_DH
    cos_t, sin_a, sin_b = tables
    wqk = p['w_qkv'][:, :2 * hd].reshape(d, 2, 2, DIFF_HEADS, DIFF_DH)
    wqk = wqk.transpose(0, 1, 3, 2, 4).reshape(d, 2 * hd)
    w = jnp.concatenate([wqk, p['w_qkv'][:, 2 * hd:]], axis=1).astype(BF16)
    gi = np.arange(LANES) // DIFF_DH
    gmat = jnp.asarray((gi[:, None] == gi[None, :]).astype(np.float32)).astype(BF16)
    qn = jnp.tile(p['q_norm'], 2).reshape(1, LANES)
    kn = jnp.tile(p['k_norm'], 2).reshape(1, LANES)
    pos_blocks = lat_seq // tm
    lat_tiles = t_lat // tm
    ctx_tiles = (t - t_lat) // tm
    tspec = pl.BlockSpec((tm, LANES), lambda i: (jnp.where(i < lat_tiles, i % pos_blocks, pos_blocks), 0))
    full = lambda i: (0, 0)
    row = pl.BlockSpec((tm, hd), lambda i: (i, 0))
    fspec = pl.BlockSpec((tm, hd), lambda i: (jnp.where(i < lat_tiles, ctx_tiles, i - lat_tiles), 0))
    fshape = jax.ShapeDtypeStruct((t - t_lat + tm, hd), F32)
    return pl.pallas_call(
        functools.partial(_qkv_body, scale=DIFF_DH ** -0.5 * LOG2_E),
        grid=(t // tm,),
        in_specs=[pl.BlockSpec((tm, d), lambda i: (i, 0)),
                  pl.BlockSpec((None, 6, d), lambda i: (i * tm // seg_len, 0, 0)),
                  pl.BlockSpec((1, d), full),
                  pl.BlockSpec((d, 3 * hd), full),
                  pl.BlockSpec((LANES, LANES), full),
                  pl.BlockSpec((1, LANES), full), pl.BlockSpec((1, LANES), full),
                  tspec, tspec, tspec],
        out_specs=[row, row, row, fspec, fspec],
        out_shape=[jax.ShapeDtypeStruct((t, hd), BF16)] * 3 + [fshape, fshape],
        compiler_params=_params("arbitrary"),
        name="qkv",
    )(x, mod, p['norm1'].reshape(1, d), w, gmat, qn, kn, cos_t, sin_a, sin_b)


def _route(logits):
    lane = lax.broadcasted_iota(jnp.int32, logits.shape, 1)
    neg = jnp.float32(-jnp.inf)
    lg = jnp.where(lane < N_EXPERTS, logits, neg)
    m1 = jnp.max(lg, axis=-1, keepdims=True)
    i1 = jnp.min(jnp.where(lg == m1, lane, LANES), axis=-1, keepdims=True)
    lg2 = jnp.where(lane == i1, neg, lg)
    m2 = jnp.max(lg2, axis=-1, keepdims=True)
    i2 = jnp.min(jnp.where(lg2 == m2, lane, LANES), axis=-1, keepdims=True)
    e = jnp.exp(m2 - m1)
    w1 = 1.0 / (1.0 + e)
    w2 = e / (1.0 + e)
    return jnp.where(lane == i1, w1, 0.0) + jnp.where(lane == i2, w2, 0.0)


MOE_BLOCK = 1024
MOE_SUB = 256
MOE_ROUTE_TM = 512


def _moe_route_body(x_ref, mod_ref, g_ref, wr_ref, h_ref, gates_ref, rank_ref, rank_t_ref,
                    carry_row, carry_col, *, tm):
    i = pl.program_id(0)

    @pl.when(i == 0)
    def _():
        carry_row[...] = jnp.zeros_like(carry_row)
        carry_col[...] = jnp.zeros_like(carry_col)

    h = _norm_mod(x_ref[...], g_ref[...], mod_ref[3:4, :], mod_ref[4:5, :])
    h_ref[...] = h.astype(BF16)
    gates = _route(_dot_f32(h, wr_ref[...]))
    gates_ref[...] = gates
    sel = jnp.where(gates != 0.0, 1.0, 0.0)
    sel_t = sel.T
    r = lax.broadcasted_iota(jnp.int32, (tm, tm), 0)
    c = lax.broadcasted_iota(jnp.int32, (tm, tm), 1)
    lower = jnp.where(c < r, 1.0, 0.0).astype(BF16)
    upper = jnp.where(r < c, 1.0, 0.0).astype(BF16)
    before = _dot(lower, sel.astype(BF16)) + carry_row[...]
    before_t = _dot(sel_t.astype(BF16), upper) + carry_col[...]
    rank_ref[...] = jnp.where(sel > 0.0, before, -1.0)
    rank_t = jnp.where(sel_t > 0.0, before_t, -1.0)
    for s in range(tm // MOE_SUB):
        rank_t_ref[s] = rank_t[:SUBLANES, s * MOE_SUB:(s + 1) * MOE_SUB]
    carry_row[...] += jnp.sum(sel, axis=0, keepdims=True)
    carry_col[...] += jnp.sum(sel_t, axis=1, keepdims=True)


def _moe_route(x, mod, g, w_router, *, seg_len):
    t, d = x.shape
    tm = MOE_ROUTE_TM
    ne = w_router.shape[1]
    assert ne <= SUBLANES
    wr = jnp.zeros((d, LANES), F32).at[:, :ne].set(w_router)
    sub = tm // MOE_SUB
    return pl.pallas_call(
        functools.partial(_moe_route_body, tm=tm),
        grid=(t // tm,),
        in_specs=[pl.BlockSpec((tm, d), lambda i: (i, 0)),
                  pl.BlockSpec((None, 6, d), lambda i: (i * tm // seg_len, 0, 0)),
                  pl.BlockSpec((1, d), lambda i: (0, 0)),
                  pl.BlockSpec((d, LANES), lambda i: (0, 0))],
        out_specs=[pl.BlockSpec((tm, d), lambda i: (i, 0)),
                   pl.BlockSpec((tm, LANES), lambda i: (i, 0)),
                   pl.BlockSpec((tm, LANES), lambda i: (i, 0)),
                   pl.BlockSpec((sub, SUBLANES, MOE_SUB), lambda i: (i, 0, 0))],
        out_shape=[jax.ShapeDtypeStruct((t, d), BF16),
                   jax.ShapeDtypeStruct((t, LANES), F32),
                   jax.ShapeDtypeStruct((t, LANES), F32),
                   jax.ShapeDtypeStruct((t // MOE_SUB, SUBLANES, MOE_SUB), F32)],
        scratch_shapes=[pltpu.VMEM((1, LANES), F32), pltpu.VMEM((LANES, 1), F32)],
        compiler_params=_params("arbitrary"),
        name="moe_route",
    )(x, mod, g.reshape(1, d), wr)


def _moe_plan(rank, ne, *, n_blocks):
    t = rank.shape[0]
    n_tiles = t // MOE_SUB
    per_blk = MOE_BLOCK // MOE_SUB
    n_sub = n_blocks * per_blk
    sel = (rank[:, :ne] >= 0.0).astype(jnp.int32)
    tile_cnt = sel.reshape(n_tiles, MOE_SUB, ne).sum(axis=1)
    tile_end = jnp.cumsum(tile_cnt, axis=0)
    tile_start = tile_end - tile_cnt
    cnt = tile_end[-1]
    nblk = (cnt + MOE_BLOCK - 1) // MOE_BLOCK
    bend = jnp.cumsum(nblk)
    bstart = bend - nblk
    e_last = jnp.max(jnp.where(cnt > 0, jnp.arange(ne), 0))
    b = jnp.arange(n_blocks)
    blk_valid = b < bend[-1]
    blk_e = jnp.minimum(jnp.sum(bend[None, :] <= b[:, None], axis=1), e_last).astype(jnp.int32)
    blk_r0 = (b - bstart[blk_e]) * MOE_BLOCK
    blk_rows = jnp.where(blk_valid, jnp.clip(cnt[blk_e] - blk_r0, 0, MOE_BLOCK), 0).astype(jnp.int32)
    j = jnp.arange(n_sub)
    sub_e = blk_e[j // per_blk]
    sub_r0 = blk_r0[j // per_blk] + (j % per_blk) * MOE_SUB
    sub_valid = jnp.logical_and(blk_valid[j // per_blk], sub_r0 < cnt[sub_e])
    ends = tile_end[:, sub_e]
    r1 = jnp.minimum(sub_r0 + MOE_SUB, cnt[sub_e])
    c_lo = jnp.sum(ends <= sub_r0[None, :], axis=0)
    c_hi = jnp.sum(ends < r1[None, :], axis=0)
    c_lo = jnp.where(sub_valid, c_lo, 1).astype(jnp.int32)
    c_hi = jnp.where(sub_valid, jnp.minimum(c_hi, n_tiles - 1), 0).astype(jnp.int32)
    base = (bstart * MOE_BLOCK).astype(jnp.int32)
    j0 = jnp.minimum((base[None, :] + tile_start) // MOE_SUB, n_sub - 2).astype(jnp.int32)
    return dict(blk_e=blk_e, blk_valid=blk_valid.astype(jnp.int32), blk_rows=blk_rows,
                sub_e=sub_e.astype(jnp.int32), sub_r0=sub_r0.astype(jnp.int32), c_lo=c_lo, c_hi=c_hi,
                base=base, j0=j0.reshape(-1))


def _moe_dispatch_body(e_ref, r0_ref, lo_ref, hi_ref, h_ref, rank_t_ref, xs_ref, acc_ref):
    j = pl.program_id(0)
    e = e_ref[j]
    rows = (r0_ref[j] + lax.broadcasted_iota(jnp.int32, (MOE_SUB, 1), 0)).astype(F32)
    sub = lax.broadcasted_iota(jnp.int32, (SUBLANES, MOE_SUB), 0)
    acc_ref[...] = jnp.zeros_like(acc_ref)

    def step(c, carry):
        rk = jnp.sum(jnp.where(sub == e, rank_t_ref[c], 0.0), axis=0, keepdims=True)
        onehot = jnp.where(rk == rows, 1.0, 0.0).astype(BF16)
        off = pl.multiple_of(c * MOE_SUB, MOE_SUB)
        acc_ref[...] += _dot(onehot, h_ref[pl.ds(off, MOE_SUB), :])
        return carry

    lax.fori_loop(lo_ref[j], hi_ref[j] + 1, step, 0)
    xs_ref[...] = acc_ref[...].astype(BF16)


def _moe_dispatch(h, rank_t, plan, *, n_sub):
    t, d = h.shape
    grid_spec = pltpu.PrefetchScalarGridSpec(
        num_scalar_prefetch=4,
        grid=(n_sub,),
        in_specs=[pl.BlockSpec((t, d), lambda j, *_: (0, 0), pipeline_mode=pl.Buffered(1)),
                  pl.BlockSpec(rank_t.shape, lambda j, *_: (0, 0, 0), pipeline_mode=pl.Buffered(1))],
        out_specs=pl.BlockSpec((MOE_SUB, d), lambda j, *_: (j, 0)),
        scratch_shapes=[pltpu.VMEM((MOE_SUB, d), F32)],
    )
    return pl.pallas_call(
        _moe_dispatch_body,
        grid_spec=grid_spec,
        out_shape=jax.ShapeDtypeStruct((n_sub * MOE_SUB, d), BF16),
        compiler_params=_params("arbitrary"),
        name="moe_dispatch",
    )(plan['sub_e'], plan['sub_r0'], plan['c_lo'], plan['c_hi'], h, rank_t)


def _moe_ffn_body(e_ref, valid_ref, rows_ref, xs_ref, wg_ref, wu_ref, wd_ref, y_ref, acc_ref):
    b = pl.program_id(0)
    f = pl.program_id(1)
    n_rows = rows_ref[b]
    last = f == pl.num_programs(1) - 1
    wg = wg_ref[...].astype(BF16)
    wu = wu_ref[...].astype(BF16)
    wd = wd_ref[...].astype(BF16)
    full = n_rows == MOE_BLOCK

    def swiglu(h):
        a = _silu(_dot(h, wg)) * _dot(h, wu)
        return _dot(a.astype(BF16), wd)

    @pl.when(jnp.logical_and(full, f == 0))
    def _():
        acc_ref[...] = swiglu(xs_ref[...])

    @pl.when(jnp.logical_and(full, f > 0))
    def _():
        acc_ref[...] += swiglu(xs_ref[...])

    @pl.when(jnp.logical_and(full, last))
    def _():
        y_ref[...] = acc_ref[...].astype(BF16)

    for s in range(MOE_BLOCK // MOE_SUB):
        sl = slice(s * MOE_SUB, (s + 1) * MOE_SUB)
        live = jnp.logical_and(jnp.logical_not(full), s * MOE_SUB < n_rows)
        dead = jnp.logical_and(jnp.logical_not(full), s * MOE_SUB >= n_rows)

        @pl.when(jnp.logical_and(live, f == 0))
        def _():
            acc_ref[sl, :] = jnp.zeros((MOE_SUB, acc_ref.shape[1]), F32)

        @pl.when(live)
        def _():
            acc_ref[sl, :] += swiglu(xs_ref[sl, :])

        @pl.when(jnp.logical_and(live, last))
        def _():
            y_ref[sl, :] = acc_ref[sl, :].astype(BF16)

        @pl.when(jnp.logical_and(dead, last))
        def _():
            y_ref[sl, :] = jnp.zeros((MOE_SUB, y_ref.shape[1]), BF16)


def _moe_ffn(xs, wg, wu, wd, plan, *, n_blocks, tf):
    _, d = xs.shape
    ne, _, ff = wg.shape
    nf = ff // tf

    def w_in(b, f, e_ref, valid_ref, rows_ref):
        return (e_ref[b], 0, jnp.where(valid_ref[b] > 0, f, nf - 1))

    def w_down(b, f, e_ref, valid_ref, rows_ref):
        return (e_ref[b], jnp.where(valid_ref[b] > 0, f, nf - 1), 0)

    grid_spec = pltpu.PrefetchScalarGridSpec(
        num_scalar_prefetch=3,
        grid=(n_blocks, nf),
        in_specs=[pl.BlockSpec((MOE_BLOCK, d), lambda b, f, *_: (b, 0)),
                  pl.BlockSpec((None, d, tf), w_in),
                  pl.BlockSpec((None, d, tf), w_in),
                  pl.BlockSpec((None, tf, d), w_down)],
        out_specs=pl.BlockSpec((MOE_BLOCK, d), lambda b, f, *_: (b, 0)),
        scratch_shapes=[pltpu.VMEM((MOE_BLOCK, d), F32)],
    )
    return pl.pallas_call(
        _moe_ffn_body,
        grid_spec=grid_spec,
        out_shape=jax.ShapeDtypeStruct((n_blocks * MOE_BLOCK, d), BF16),
        compiler_params=_params("arbitrary", "arbitrary"),
        name="moe_ffn",
    )(plan['blk_e'], plan['blk_valid'], plan['blk_rows'], xs, wg, wu, wd)


def _moe_combine_body(j0_ref, base_ref, x_ref, mod_ref, gates_ref, rank_ref, *rest, ne, split_tiles):
    y_refs, o_ref, o2_ref = rest[:2 * ne], rest[2 * ne], rest[2 * ne + 1]
    c = pl.program_id(0)
    gates = gates_ref[...]
    rank = rank_ref[...]
    lane = lax.broadcasted_iota(jnp.int32, gates.shape, 1)
    col = lax.broadcasted_iota(jnp.int32, (1, MOE_SUB), 1).astype(F32)
    acc = None
    for e in range(ne):
        pick = lane == e
        g = jnp.sum(jnp.where(pick, gates, 0.0), axis=-1, keepdims=True)
        rk = jnp.sum(jnp.where(pick, rank, 0.0), axis=-1, keepdims=True)
        shift = (base_ref[e] - j0_ref[c * ne + e] * MOE_SUB).astype(F32)
        loc = jnp.where(rk >= 0.0, rk + shift, -1.0)
        qa = jnp.where(loc == col, 1.0, 0.0).astype(BF16)
        qb = jnp.where(loc == col + float(MOE_SUB), 1.0, 0.0).astype(BF16)
        contrib = g * (_dot(qa, y_refs[2 * e][...]) + _dot(qb, y_refs[2 * e + 1][...]))
        acc = contrib if acc is None else acc + contrib
    out = x_ref[...] + mod_ref[5:6, :] * acc

    @pl.when(c < split_tiles)
    def _():
        o_ref[...] = out

    @pl.when(c >= split_tiles)
    def _():
        o2_ref[...] = out


def _moe_combine(x, mod, gates, rank, y, plan, *, ne, seg_len, t_split):
    t, d = x.shape
    tm = MOE_SUB
    split_tiles = t_split // tm
    y_specs = []
    for e in range(ne):
        y_specs += [pl.BlockSpec((MOE_SUB, d), lambda c, j0, base, e=e: (j0[c * ne + e], 0)),
                    pl.BlockSpec((MOE_SUB, d), lambda c, j0, base, e=e: (j0[c * ne + e] + 1, 0))]
    grid_spec = pltpu.PrefetchScalarGridSpec(
        num_scalar_prefetch=2,
        grid=(t // tm,),
        in_specs=[pl.BlockSpec((tm, d), lambda c, *_: (c, 0)),
                  pl.BlockSpec((None, 6, d), lambda c, *_: (c * tm // seg_len, 0, 0)),
                  pl.BlockSpec((tm, LANES), lambda c, *_: (c, 0)),
                  pl.BlockSpec((tm, LANES), lambda c, *_: (c, 0))] + y_specs,
        out_specs=[pl.BlockSpec((tm, d), lambda c, *_: (jnp.minimum(c, split_tiles - 1), 0)),
                   pl.BlockSpec((tm, d), lambda c, *_: (jnp.maximum(c - split_tiles, 0), 0))],
    )
    return pl.pallas_call(
        functools.partial(_moe_combine_body, ne=ne, split_tiles=split_tiles),
        grid_spec=grid_spec,
        out_shape=[jax.ShapeDtypeStruct((t_split, d), F32), jax.ShapeDtypeStruct((t - t_split, d), F32)],
        compiler_params=_params("arbitrary"),
        name="moe_combine",
    )(plan['j0'], plan['base'], x, mod, gates, rank, *([y] * (2 * ne)))


def _moe(x, mod, g, w_router, wg, wu, wd, *, seg_len, t_split, top_k=2):
    t, d = x.shape
    ne = w_router.shape[1]
    n_blocks = t * top_k // MOE_BLOCK + ne
    h, gates, rank, rank_t = _moe_route(x, mod, g, w_router, seg_len=seg_len)
    plan = _moe_plan(rank, ne, n_blocks=n_blocks)
    xs = _moe_dispatch(h, rank_t, plan, n_sub=n_blocks * (MOE_BLOCK // MOE_SUB))
    y = _moe_ffn(xs, wg, wu, wd, plan, n_blocks=n_blocks, tf=512)
    return _moe_combine(x, mod, gates, rank, y, plan, ne=ne, seg_len=seg_len, t_split=t_split)


def _even_layer(x_lat, x_ctx, cond8, p, cache_ckv, cache_kr, *, nb, lat_seq, n_ctx, ctx_seq):
    d = x_lat.shape[1]
    t_lat = nb * lat_seq
    t = t_lat + x_ctx.shape[0]
    seg_len = lat_seq
    past = cache_ckv.shape[1]
    mod = _adaln(cond8, p['w_mod'], p['b_mod'])

    n_in = p['w_in'].shape[1]
    n_pad = -(-n_in // LANES) * LANES
    w_in = jnp.zeros((d, n_pad), BF16).at[:, :n_in].set(p['w_in'].astype(BF16))
    z = _lin_in(x_lat, x_ctx, mod, p['norm1'], w_in, tm=512, seg_len=seg_len)

    u, x0 = _hy_pre(z, p['hy_conv_w'], p['hy_conv_b'], t_lat=t_lat, lat_seq=lat_seq, ctx_seq=ctx_seq)
    k_raw, k_sum = _hyena_filters(p, lat_seq=lat_seq, ctx_seq=ctx_seq)
    dbias = p['hy_dbias'].reshape(1, HY_WIDTH)
    conv_lat = _hyena_lat(k_raw[:2 * lat_seq], k_sum[0:1], u[:t_lat], seq=lat_seq, nb=nb)
    conv_ctx = _hyena_ctx(k_raw[2 * lat_seq:], k_sum[1:2], u, seq=ctx_seq, nseq=n_ctx, u_row0=t_lat)

    tm = 512
    tables = _rope_tables(lat_seq, MLA_ROPE, (MLA_NOPE,), tm)
    q, ckvn = _mla_q(z, p, tables, tm=tm, t_lat=t_lat, lat_seq=lat_seq)
    kr_col = 3 * HY_WIDTH + MLA_Q_RANK + MLA_KV_RANK
    kr = z[:, kr_col:kr_col + LANES]
    cache_kr_p = jnp.zeros((nb, past, LANES), F32).at[:, :, :MLA_ROPE].set(cache_kr)
    ckvn_rows = jnp.concatenate(
        [jnp.concatenate([cache_ckv, ckvn[:t_lat].reshape(nb, lat_seq, -1)], axis=1).reshape(nb * (past + lat_seq), -1),
         ckvn[t_lat:]], axis=0)
    kr_rows = jnp.concatenate(
        [jnp.concatenate([cache_kr_p, kr[:t_lat].reshape(nb, lat_seq, LANES)], axis=1).reshape(nb * (past + lat_seq), LANES),
         kr[t_lat:]], axis=0)
    k_all, v_all = _mla_kv(ckvn_rows, kr_rows, p, tables, tm=tm, nb=nb, past=past, lat_seq=lat_seq)
    o = jnp.zeros((t, MLA_HEADS * MLA_V), BF16)
    o = _mla_attn(o, q, k_all, v_all, tq=ATT_TQ, n_seq=nb, seq_q=lat_seq, seq_k=past + lat_seq,
                  q_row0=0, k_row0=0)
    o = _mla_attn(o, q, k_all, v_all, tq=ctx_seq, n_seq=n_ctx, seq_q=ctx_seq, seq_k=ctx_seq,
                  q_row0=t_lat, k_row0=nb * (past + lat_seq))

    w_out = p['w_out'].astype(BF16)
    x = _even_out(x_lat, x_ctx, mod, conv_lat, conv_ctx, u, x0, dbias, o, w_out[:HY_WIDTH], w_out[HY_WIDTH:],
                  tm=512, seg_len=seg_len)
    x = _ffn(x, mod, p['norm2'], p['ffn_w_gate'].astype(BF16), p['ffn_w_up'].astype(BF16),
             p['ffn_w_down'].astype(BF16), tm=512, tf=1408, seg_len=seg_len)
    new_ckv = ckvn[t_lat:].reshape(n_ctx, ctx_seq, -1)
    new_kr = kr[t_lat:, :MLA_ROPE].reshape(n_ctx, ctx_seq, MLA_ROPE)
    return x, new_ckv, new_kr


def _odd_layer(x, cond8, p, cache_k, cache_v, lambda_init, *, nb, lat_seq, n_ctx, ctx_seq):
    t, d = x.shape
    t_lat = nb * lat_seq
    seg_len = lat_seq
    mod = _adaln(cond8, p['w_mod'], p['b_mod'])
    tm = 512
    tables = _rope_tables(lat_seq, DIFF_DH, (0, DIFF_DH), tm)
    q, k, v, kf, vf = _qkv(x, mod, p, tables, tm=tm, seg_len=seg_len, t_lat=t_lat, lat_seq=lat_seq)
    lam_p = jnp.stack([p['lam_q1'], p['lam_k1'], p['lam_q2'], p['lam_k2']])
    o = jnp.zeros((t, DIFF_HEADS * 2 * DIFF_DH), BF16)
    o = _diff_attn(o, q, k, v, cache_k.astype(BF16), cache_v.astype(BF16), lam_p, p['subln'],
                   tq=ATT_TQ, n_seq=nb, seq_q=lat_seq, q_row0=0, lambda_init=lambda_init)
    o = _diff_attn(o, q, k, v, None, None, lam_p, p['subln'],
                   tq=ctx_seq, n_seq=n_ctx, seq_q=ctx_seq, q_row0=t_lat, lambda_init=lambda_init)
    x = _proj_res(x, mod, [o], [p['w_out'].astype(BF16)], tm=512, seg_len=seg_len, gate_row=2)
    x_lat, x_ctx = _moe(x, mod, p['norm2'], p['w_router'], p['moe_w_gate'], p['moe_w_up'], p['moe_w_down'],
                        seg_len=seg_len, t_split=t_lat)
    hd = 2 * DIFF_DH
    t_ctx = t - t_lat
    new_k = kf[:t_ctx].reshape(n_ctx, ctx_seq, DIFF_HEADS, hd).transpose(0, 2, 1, 3)
    new_v = vf[:t_ctx].reshape(n_ctx, ctx_seq, DIFF_HEADS, hd).transpose(0, 2, 1, 3)
    return x_lat, x_ctx, new_k, new_v


def kernel(x_prompt, x_sample, cache_l0_ckv, cache_l0_krope, cache_l1_k, cache_l1_v, c, c_ctx,
           l0_w_mod, l0_b_mod, l0_norm1, l0_norm2, l0_w_in, l0_hy_conv_w, l0_hy_conv_b,
           l0_hy_fw1, l0_hy_fb1, l0_hy_freq1, l0_hy_fw2, l0_hy_fb2, l0_hy_freq2, l0_hy_fw3, l0_hy_dbias,
           l0_mla_qa_norm, l0_mla_w_uq, l0_mla_kva_norm, l0_mla_w_ukv, l0_mla_q_norm, l0_mla_k_norm,
           l0_w_out, l0_ffn_w_gate, l0_ffn_w_up, l0_ffn_w_down,
           l1_w_mod, l1_b_mod, l1_norm1, l1_norm2, l1_w_qkv, l1_q_norm, l1_k_norm,
           l1_lam_q1, l1_lam_k1, l1_lam_q2, l1_lam_k2, l1_subln, l1_w_out,
           l1_w_router, l1_moe_w_gate, l1_moe_w_up, l1_moe_w_down):
    even = {
        'w_mod': l0_w_mod, 'b_mod': l0_b_mod, 'norm1': l0_norm1, 'norm2': l0_norm2, 'w_in': l0_w_in,
        'hy_conv_w': l0_hy_conv_w, 'hy_conv_b': l0_hy_conv_b, 'hy_fw1': l0_hy_fw1, 'hy_fb1': l0_hy_fb1,
        'hy_freq1': l0_hy_freq1, 'hy_fw2': l0_hy_fw2, 'hy_fb2': l0_hy_fb2, 'hy_freq2': l0_hy_freq2,
        'hy_fw3': l0_hy_fw3, 'hy_dbias': l0_hy_dbias, 'qa_norm': l0_mla_qa_norm, 'w_uq': l0_mla_w_uq,
        'kva_norm': l0_mla_kva_norm, 'w_ukv': l0_mla_w_ukv, 'q_norm': l0_mla_q_norm, 'k_norm': l0_mla_k_norm,
        'w_out': l0_w_out, 'ffn_w_gate': l0_ffn_w_gate, 'ffn_w_up': l0_ffn_w_up, 'ffn_w_down': l0_ffn_w_down,
    }
    odd = {
        'w_mod': l1_w_mod, 'b_mod': l1_b_mod, 'norm1': l1_norm1, 'norm2': l1_norm2, 'w_qkv': l1_w_qkv,
        'q_norm': l1_q_norm, 'k_norm': l1_k_norm, 'lam_q1': l1_lam_q1, 'lam_k1': l1_lam_k1,
        'lam_q2': l1_lam_q2, 'lam_k2': l1_lam_k2, 'subln': l1_subln, 'w_out': l1_w_out,
        'w_router': l1_w_router, 'moe_w_gate': l1_moe_w_gate, 'moe_w_up': l1_moe_w_up,
        'moe_w_down': l1_moe_w_down,
    }
    n_ctx, ctx_seq, d = x_prompt.shape
    nb, lat_seq, _ = x_sample.shape
    assert n_ctx * ctx_seq == lat_seq, "segment layout needs equally sized modulation segments"
    dims = dict(nb=nb, lat_seq=lat_seq, n_ctx=n_ctx, ctx_seq=ctx_seq)
    t_lat = nb * lat_seq
    cond8 = jnp.zeros((SUBLANES, d), F32).at[:nb].set(c).at[nb].set(c_ctx)

    x, new_l0_ckv, new_l0_krope = _even_layer(x_sample.reshape(t_lat, d), x_prompt.reshape(n_ctx * ctx_seq, d),
                                              cond8, even, cache_l0_ckv, cache_l0_krope, **dims)
    lambda_init = 0.8 - 0.6 * math.exp(-0.3 * 1)
    x_lat, x_ctx, new_l1_k, new_l1_v = _odd_layer(x, cond8, odd, cache_l1_k, cache_l1_v, lambda_init, **dims)

    y_sample = x_lat.reshape(nb, lat_seq, d)
    y_prompt = x_ctx.reshape(n_ctx, ctx_seq, d)
    return (y_prompt, y_sample, new_l0_ckv, new_l0_krope, new_l1_k, new_l1_v)
```

```python
import functools
import math

import numpy as np
import jax
import jax.numpy as jnp
from jax import lax
from jax.experimental import pallas as pl
from jax.experimental.pallas import tpu as pltpu

F32 = jnp.float32
BF16 = jnp.bfloat16

VMEM_LIMIT_BYTES = 56 * 1024 * 1024
LANES = 128
SUBLANES = 8
LOG2_E = math.log2(math.e)

GRID_W = 64
ROPE_BASE = 10000.0
NORM_EPS = 1e-6
HY_WIDTH = 512
HY_BANDS = 16
HY_FAST_DECAY_PCT = 0.3
HY_SLOW_DECAY_PCT = 1.5
HY_DECAY_TARGET = 1e-2
MLA_HEADS = 8
MLA_NOPE = 64
MLA_ROPE = 32
MLA_QK = MLA_NOPE + MLA_ROPE
MLA_V = 64
MLA_Q_RANK = 768
MLA_KV_RANK = 256
DIFF_HEADS = 8
DIFF_DH = 64
N_EXPERTS = 8


def _params(*sem):
    return pltpu.CompilerParams(dimension_semantics=sem, vmem_limit_bytes=VMEM_LIMIT_BYTES)


def _dot(a, b):
    return jnp.dot(a, b, preferred_element_type=F32)


def _dot_nt(a, b):
    return lax.dot_general(a, b, (((1,), (1,)), ((), ())), preferred_element_type=F32)


def _split_bf16(a):
    hi = a.astype(BF16)
    lo = (a - hi.astype(F32)).astype(BF16)
    return hi, lo


def _dot_f32(a, b):
    ah, al = _split_bf16(a)
    bh, bl = _split_bf16(b)
    return _dot(ah, bh) + (_dot(al, bh) + _dot(ah, bl))


def _rms(x, g, n=None):
    n = x.shape[-1] if n is None else n
    ms = jnp.sum(x * x, axis=-1, keepdims=True) * (1.0 / n)
    return x * lax.rsqrt(ms + NORM_EPS) * g


def _norm_mod(x, g, shift, scale):
    return _rms(x, g) * (1.0 + scale) + shift


def _silu(x):
    return x / (1.0 + jnp.exp(-x))


def _adaln_body(c_ref, w_ref, b_ref, o_ref):
    o_ref[...] = _dot_f32(_silu(c_ref[...]), w_ref[...]) + b_ref[...]


def _adaln(cond8, w_mod, b_mod):
    d, n = w_mod.shape
    tn = n // 4
    out = pl.pallas_call(
        _adaln_body,
        grid=(n // tn,),
        in_specs=[pl.BlockSpec((SUBLANES, d), lambda j: (0, 0)),
                  pl.BlockSpec((d, tn), lambda j: (0, j)),
                  pl.BlockSpec((1, tn), lambda j: (0, j))],
        out_specs=pl.BlockSpec((SUBLANES, tn), lambda j: (0, j)),
        out_shape=jax.ShapeDtypeStruct((SUBLANES, n), F32),
        compiler_params=_params("arbitrary"),
        name="adaln",
    )(cond8, w_mod, b_mod.reshape(1, n))
    return out.reshape(SUBLANES, 6, d)


def _two_part_specs(a, b, tm):
    na = a.shape[0] // tm
    cols = a.shape[1]
    return (pl.BlockSpec((tm, cols), lambda i: (jnp.minimum(i, na - 1), 0)),
            pl.BlockSpec((tm, cols), lambda i: (jnp.maximum(i - na, 0), 0)), na)


def _lin_in_body(xa_ref, xb_ref, mod_ref, g_ref, w_ref, o_ref, *, na):
    x = jnp.where(pl.program_id(0) < na, xa_ref[...], xb_ref[...])
    h = _norm_mod(x, g_ref[...], mod_ref[0:1, :], mod_ref[1:2, :])
    o_ref[...] = _dot(h.astype(BF16), w_ref[...])


def _lin_in(xa, xb, mod, g, w, *, tm, seg_len):
    d = xa.shape[1]
    t = xa.shape[0] + xb.shape[0]
    n = w.shape[1]
    spec_a, spec_b, na = _two_part_specs(xa, xb, tm)
    return pl.pallas_call(
        functools.partial(_lin_in_body, na=na),
        grid=(t // tm,),
        in_specs=[spec_a, spec_b,
                  pl.BlockSpec((None, 6, d), lambda i: (i * tm // seg_len, 0, 0)),
                  pl.BlockSpec((1, d), lambda i: (0, 0)),
                  pl.BlockSpec((d, n), lambda i: (0, 0))],
        out_specs=pl.BlockSpec((tm, n), lambda i: (i, 0)),
        out_shape=jax.ShapeDtypeStruct((t, n), F32),
        compiler_params=_params("arbitrary"),
        name="lin_in",
    )(xa, xb, mod, g.reshape(1, d), w)


def _hy_pre_body(z_ref, zp_ref, zn_ref, w_ref, b_ref, u_ref, x0_ref, *, tm, lat_tiles, tiles_per_seq):
    i = pl.program_id(0)
    z = z_ref[...]
    in_lat = i < lat_tiles
    has_prev = jnp.logical_and(in_lat, i % tiles_per_seq != 0)
    has_next = jnp.logical_and(in_lat, i % tiles_per_seq != tiles_per_seq - 1)
    prev_row = jnp.where(has_prev, zp_ref[SUBLANES - 1:SUBLANES, :], 0.0)
    next_row = jnp.where(has_next, zn_ref[0:1, :], 0.0)
    rows = lax.broadcasted_iota(jnp.int32, z.shape, 0)
    z_m = jnp.where(rows == 0, prev_row, pltpu.roll(z, 1, 0))
    z_p = jnp.where(rows == tm - 1, next_row, pltpu.roll(z, tm - 1, 0))
    zc = b_ref[...] + z_m * w_ref[0:1, :] + z * w_ref[1:2, :] + z_p * w_ref[2:3, :]
    c = HY_WIDTH
    x0_ref[...] = zc[:, :c]
    u_ref[...] = zc[:, 2 * c:] * zc[:, c:2 * c]


def _hy_pre(z, conv_w, conv_b, *, t_lat, lat_seq, ctx_seq):
    t = z.shape[0]
    tm = ctx_seq
    c3 = 3 * HY_WIDTH
    nb8 = t // SUBLANES
    body = functools.partial(_hy_pre_body, tm=tm, lat_tiles=t_lat // tm, tiles_per_seq=lat_seq // tm)
    return pl.pallas_call(
        body,
        grid=(t // tm,),
        in_specs=[pl.BlockSpec((tm, c3), lambda i: (i, 0)),
                  pl.BlockSpec((SUBLANES, c3), lambda i: (jnp.maximum(i * (tm // SUBLANES) - 1, 0), 0)),
                  pl.BlockSpec((SUBLANES, c3), lambda i: (jnp.minimum((i + 1) * (tm // SUBLANES), nb8 - 1), 0)),
                  pl.BlockSpec((3, c3), lambda i: (0, 0)),
                  pl.BlockSpec((1, c3), lambda i: (0, 0))],
        out_specs=[pl.BlockSpec((tm, HY_WIDTH), lambda i: (i, 0)),
                   pl.BlockSpec((tm, HY_WIDTH), lambda i: (i, 0))],
        out_shape=[jax.ShapeDtypeStruct((t, HY_WIDTH), F32),
                   jax.ShapeDtypeStruct((t, HY_WIDTH), F32)],
        compiler_params=_params("parallel"),
        name="hy_pre",
    )(z, z, z, conv_w, conv_b.reshape(1, c3))


def _filter_embedding(seq):
    t01 = np.linspace(0.0, 1.0, seq)[:, None]
    w = 2.0 * math.pi * np.arange(seq)[:, None] / seq
    f = np.linspace(1e-4, HY_BANDS - 1, HY_BANDS)[None, :]
    z = np.concatenate([t01, np.cos(f * w), -np.sin(f * w)], axis=-1)
    z_rev = np.concatenate([z[:1], z[:0:-1]], axis=0)
    zz = np.concatenate([z, z_rev], axis=0)
    out = np.zeros((2 * seq, LANES), np.float32)
    out[:, :zz.shape[1]] = zz
    return out


def _filter_body(zz_ref, dl_ref, w1_ref, b1_ref, f1_ref, w2_ref, b2_ref, f2_ref, w3_ref,
                 k_ref, s_ref, *, tm, lat_tiles, ctx_tiles):
    i = pl.program_id(0)
    zz = zz_ref[...]
    h = jnp.sin(f1_ref[...] * (_dot_f32(zz, w1_ref[...]) + b1_ref[...]))
    h = jnp.sin(f2_ref[...] * (_dot_f32(h, w2_ref[...]) + b2_ref[...]))
    h = _dot_f32(h, w3_ref[...])
    is_bwd = jnp.logical_or(jnp.logical_and(i >= lat_tiles // 2, i < lat_tiles),
                            i >= lat_tiles + ctx_tiles // 2)
    first_bwd = jnp.logical_or(i == lat_tiles // 2, i == lat_tiles + ctx_tiles // 2)
    window = jnp.exp(-zz[:, 0:1] * dl_ref[...])
    k = jnp.where(is_bwd, h[:, HY_WIDTH:], h[:, :HY_WIDTH]) * window
    rows = lax.broadcasted_iota(jnp.int32, k.shape, 0)
    k = jnp.where(jnp.logical_and(first_bwd, rows == 0), 0.0, k)
    k_ref[...] = k
    s = jnp.sum(jnp.abs(k), axis=0, keepdims=True)

    @pl.when(i == 0)
    def _():
        s_ref[...] = jnp.zeros_like(s_ref)

    @pl.when(i < lat_tiles)
    def _():
        s_ref[0:1, :] += s

    @pl.when(i >= lat_tiles)
    def _():
        s_ref[1:2, :] += s


def _hyena_filters(p, *, lat_seq, ctx_seq):
    tm = ctx_seq
    zz = jnp.asarray(np.concatenate([_filter_embedding(lat_seq), _filter_embedding(ctx_seq)], axis=0))
    rows = zz.shape[0]
    max_decay = math.log(HY_DECAY_TARGET) / HY_FAST_DECAY_PCT
    min_decay = math.log(HY_DECAY_TARGET) / HY_SLOW_DECAY_PCT
    deltas = jnp.asarray(np.abs(np.linspace(min_decay, max_decay, HY_WIDTH))[None, :].astype(np.float32))
    emb, hid = p['hy_fw1'].shape

    def pad2(a, r, c):
        return jnp.zeros((r, c), F32).at[:a.shape[0], :a.shape[1]].set(a)

    w1 = pad2(p['hy_fw1'], LANES, LANES)
    b1 = pad2(p['hy_fb1'][None, :], 1, LANES)
    f1 = pad2(p['hy_freq1'][None, :], 1, LANES)
    w2 = pad2(p['hy_fw2'], LANES, LANES)
    b2 = pad2(p['hy_fb2'][None, :], 1, LANES)
    f2 = pad2(p['hy_freq2'][None, :], 1, LANES)
    w3 = pad2(p['hy_fw3'], LANES, 2 * HY_WIDTH)
    body = functools.partial(_filter_body, tm=tm, lat_tiles=2 * lat_seq // tm, ctx_tiles=2 * ctx_seq // tm)
    full = lambda i: (0, 0)
    return pl.pallas_call(
        body,
        grid=(rows // tm,),
        in_specs=[pl.BlockSpec((tm, LANES), lambda i: (i, 0)),
                  pl.BlockSpec((1, HY_WIDTH), full),
                  pl.BlockSpec((LANES, LANES), full), pl.BlockSpec((1, LANES), full), pl.BlockSpec((1, LANES), full),
                  pl.BlockSpec((LANES, LANES), full), pl.BlockSpec((1, LANES), full), pl.BlockSpec((1, LANES), full),
                  pl.BlockSpec((LANES, 2 * HY_WIDTH), full)],
        out_specs=[pl.BlockSpec((tm, HY_WIDTH), lambda i: (i, 0)),
                   pl.BlockSpec((SUBLANES, HY_WIDTH), full)],
        out_shape=[jax.ShapeDtypeStruct((rows, HY_WIDTH), F32),
                   jax.ShapeDtypeStruct((SUBLANES, HY_WIDTH), F32)],
        compiler_params=_params("arbitrary"),
        name="hy_filter",
    )(zz, deltas, w1, b1, f1, w2, b2, f2, w3)


def _stack_complex(z):
    return np.block([[z.real, -z.imag], [z.imag, z.real]])


def _dft_consts_two_level(seq, n1, n2):
    n = 2 * seq
    assert n1 * n2 == n
    a1 = np.arange(n1)
    f1_full = np.exp(-2j * np.pi * np.outer(a1, a1) / n1)
    f1_u = np.concatenate([f1_full.real, f1_full.imag], axis=0)[:, :n1 // 2]
    f1_k = np.concatenate([f1_full.real, f1_full.imag], axis=0)
    a2 = np.arange(n2)
    f = a1[:, None, None] + n1 * a2[None, :, None]
    z = np.exp(-2j * np.pi * (f * a2[None, None, :]) / n)
    mf = np.stack([_stack_complex(z[i]) for i in range(n1)])
    mi = np.stack([_stack_complex(np.conj(z[i]).T) for i in range(n1)])
    g = np.exp(2j * np.pi * np.outer(a1[:n1 // 2], a1) / n1) / n
    gc, gs = g.real, -g.imag
    as32 = lambda a: jnp.asarray(a.astype(np.float32))
    return as32(f1_u), as32(f1_k), as32(mf), as32(mi), as32(gc), as32(gs)


def _dft_consts_one_level(seq):
    n = 2 * seq
    a = np.arange(n)
    z = np.exp(-2j * np.pi * np.outer(a, a) / n)
    mf = np.concatenate([z.real, z.imag], axis=0)
    zi = np.exp(2j * np.pi * np.outer(a[:seq], a) / n) / n
    mi = np.concatenate([zi.real, -zi.imag], axis=1)
    as32 = lambda a: jnp.asarray(a.astype(np.float32))
    return as32(mf), as32(mi)


def _lmat_body(f_ref, x_ref, sc_ref, or_ref, oi_ref):
    x = (x_ref[...] * (1.0 / sc_ref[...])).astype(BF16)
    o = _dot(f_ref[...], x)
    h = o.shape[0] // 2
    or_ref[...] = o[:h].astype(BF16)
    oi_ref[...] = o[h:].astype(BF16)


def _lmat(f, x, scale_row, *, tn):
    g, k, cols = x.shape
    m2 = f.shape[0]
    m = m2 // 2
    return pl.pallas_call(
        _lmat_body,
        grid=(g, cols // tn),
        in_specs=[pl.BlockSpec((m2, k), lambda b, j: (0, 0)),
                  pl.BlockSpec((None, k, tn), lambda b, j: (b, 0, j)),
                  pl.BlockSpec((1, tn), lambda b, j: (0, 0))],
        out_specs=[pl.BlockSpec((None, m, tn), lambda b, j: (b, 0, j)),
                   pl.BlockSpec((None, m, tn), lambda b, j: (b, 0, j))],
        out_shape=[jax.ShapeDtypeStruct((g, m, cols), BF16),
                   jax.ShapeDtypeStruct((g, m, cols), BF16)],
        compiler_params=_params("parallel", "parallel"),
        name="hy_dft1",
    )(f.astype(BF16), x, scale_row)


SPEC_GROUP = 4


def _spec_fwd_body(mf_ref, ar_ref, ai_ref, kr_ref, ki_ref):
    for g in range(SPEC_GROUP):
        a = jnp.concatenate([ar_ref[g], ai_ref[g]], axis=0)
        x = _dot(mf_ref[g], a)
        h = x.shape[0] // 2
        kr_ref[g] = x[:h]
        ki_ref[g] = x[h:]


def _spec_fwd(mf, ar, ai):
    n1, n2, c = ar.shape
    spec = pl.BlockSpec((SPEC_GROUP, n2, c), lambda i: (i, 0, 0))
    return pl.pallas_call(
        _spec_fwd_body,
        grid=(n1 // SPEC_GROUP,),
        in_specs=[pl.BlockSpec((SPEC_GROUP, 2 * n2, 2 * n2), lambda i: (i, 0, 0)), spec, spec],
        out_specs=[spec, spec],
        out_shape=[jax.ShapeDtypeStruct((n1, n2, c), F32)] * 2,
        compiler_params=_params("parallel"),
        name="hy_spec_filter",
    )(mf, ar, ai)


def _spec_mul_body(mf_ref, mi_ref, kr_ref, ki_ref, ar_ref, ai_ref, br_ref, bi_ref):
    for g in range(SPEC_GROUP):
        a = jnp.concatenate([ar_ref[g], ai_ref[g]], axis=0)
        x = _dot(mf_ref[g], a)
        h = x.shape[0] // 2
        xr, xi = x[:h], x[h:]
        kr, ki = kr_ref[g], ki_ref[g]
        y = jnp.concatenate([xr * kr - xi * ki, xr * ki + xi * kr], axis=0).astype(BF16)
        b = _dot(mi_ref[g], y)
        br_ref[g] = b[:h].astype(BF16)
        bi_ref[g] = b[h:].astype(BF16)


def _spec_mul(mf, mi, kr, ki, ar, ai):
    nb, n1, n2, c = ar.shape
    mspec = pl.BlockSpec((SPEC_GROUP, 2 * n2, 2 * n2), lambda i, b: (i, 0, 0))
    kspec = pl.BlockSpec((SPEC_GROUP, n2, c), lambda i, b: (i, 0, 0))
    aspec = pl.BlockSpec((None, SPEC_GROUP, n2, c), lambda i, b: (b, i, 0, 0))
    return pl.pallas_call(
        _spec_mul_body,
        grid=(n1 // SPEC_GROUP, nb),
        in_specs=[mspec, mspec, kspec, kspec, aspec, aspec],
        out_specs=[aspec, aspec],
        out_shape=[jax.ShapeDtypeStruct((nb, n1, n2, c), BF16)] * 2,
        compiler_params=_params("parallel", "arbitrary"),
        name="hy_spec_mul",
    )(mf, mi, kr, ki, ar, ai)


def _idft1_body(gc_ref, gs_ref, br_ref, bi_ref, o_ref):
    o_ref[...] = _dot(gc_ref[...], br_ref[...]) + _dot(gs_ref[...], bi_ref[...])


def _idft1(gc, gs, br, bi, *, tn):
    nb, n1, cols = br.shape
    m = gc.shape[0]
    gspec = pl.BlockSpec((m, n1), lambda b, j: (0, 0))
    bspec = pl.BlockSpec((None, n1, tn), lambda b, j: (b, 0, j))
    return pl.pallas_call(
        _idft1_body,
        grid=(nb, cols // tn),
        in_specs=[gspec, gspec, bspec, bspec],
        out_specs=pl.BlockSpec((None, m, tn), lambda b, j: (b, 0, j)),
        out_shape=jax.ShapeDtypeStruct((nb, m, cols), F32),
        compiler_params=_params("parallel", "parallel"),
        name="hy_idft1",
    )(gc.astype(BF16), gs.astype(BF16), br, bi)


def _ctx_filter_body(mf_ref, k_ref, sc_ref, kf_ref):
    kf_ref[...] = _dot(mf_ref[...], (k_ref[...] * (1.0 / sc_ref[...])).astype(BF16))


def _ctx_conv_body(mf_ref, mi_ref, kf_ref, u_ref, o_ref):
    x = _dot(mf_ref[...], u_ref[...].astype(BF16))
    h = x.shape[0] // 2
    xr, xi = x[:h], x[h:]
    kr, ki = kf_ref[:h, :], kf_ref[h:, :]
    y = jnp.concatenate([xr * kr - xi * ki, xr * ki + xi * kr], axis=0).astype(BF16)
    o_ref[...] = _dot(mi_ref[...], y)


def _hyena_ctx(k_raw, k_norm1, u, *, seq, nseq, u_row0):
    mf, mi = _dft_consts_one_level(seq)
    n = 2 * seq
    c = u.shape[1]
    kf = pl.pallas_call(
        _ctx_filter_body,
        out_shape=jax.ShapeDtypeStruct((2 * n, c), F32),
        compiler_params=_params(),
        name="hy_ctx_filter",
    )(mf.astype(BF16), k_raw, k_norm1)
    full = lambda s: (0, 0)
    return pl.pallas_call(
        _ctx_conv_body,
        grid=(nseq,),
        in_specs=[pl.BlockSpec((2 * n, seq), full),
                  pl.BlockSpec((seq, 2 * n), full),
                  pl.BlockSpec((2 * n, c), full),
                  pl.BlockSpec((seq, c), lambda s: (u_row0 // seq + s, 0))],
        out_specs=pl.BlockSpec((seq, c), lambda s: (s, 0)),
        out_shape=jax.ShapeDtypeStruct((nseq * seq, c), F32),
        compiler_params=_params("parallel"),
        name="hy_ctx_conv",
    )(mf[:, :seq].astype(BF16), mi.astype(BF16), kf, u)


def _hyena_lat(k_raw, k_norm1, u, *, seq, nb):
    c = u.shape[1]
    n1, n2 = 64, 2 * seq // 64
    f1_u, f1_k, mf, mi, gc, gs = _dft_consts_two_level(seq, n1, n2)
    mf = mf.astype(BF16)
    mi = mi.astype(BF16)
    cols = n2 * c
    tn = 4096
    rep = tn // c
    ones_row = jnp.ones((1, tn), F32)
    akr, aki = _lmat(f1_k, k_raw.reshape(1, n1, cols), jnp.tile(k_norm1, (1, rep)), tn=tn)
    kr, ki = _spec_fwd(mf, akr.reshape(n1, n2, c), aki.reshape(n1, n2, c))
    ar, ai = _lmat(f1_u, u.reshape(nb, n1 // 2, cols), ones_row, tn=tn)
    br, bi = _spec_mul(mf, mi, kr, ki, ar.reshape(nb, n1, n2, c), ai.reshape(nb, n1, n2, c))
    y = _idft1(gc, gs, br.reshape(nb, n1, cols), bi.reshape(nb, n1, cols), tn=tn)
    return y.reshape(nb * seq, c)


def _rope_tables(seq, rope_dims, lane_offsets, pad_rows):
    rows = seq // GRID_W
    rr, cc = np.meshgrid(np.arange(rows), np.arange(GRID_W), indexing='ij')
    pos = (rr.reshape(-1).astype(np.float64), cc.reshape(-1).astype(np.float64))
    half = rope_dims // 2
    q = half // 2
    inv_freq = ROPE_BASE ** (-np.arange(0, half, 2, dtype=np.float64) / half)
    cos_t = np.ones((seq + pad_rows, LANES), np.float64)
    sin_a = np.zeros((seq + pad_rows, LANES), np.float64)
    sin_b = np.zeros((seq + pad_rows, LANES), np.float64)
    for off in lane_offsets:
        for axis in range(2):
            ang = pos[axis][:, None] * inv_freq[None, :]
            base = off + axis * half
            cos_t[:seq, base:base + q] = np.cos(ang)
            cos_t[:seq, base + q:base + half] = np.cos(ang)
            sin_b[:seq, base:base + q] = -np.sin(ang)
            sin_a[:seq, base + q:base + half] = np.sin(ang)
    as32 = lambda a: jnp.asarray(a.astype(np.float32))
    return as32(cos_t), as32(sin_a), as32(sin_b)


def _rope(x, cos_t, sin_a, sin_b, shift):
    return x * cos_t + pltpu.roll(x, shift, 1) * sin_a + pltpu.roll(x, LANES - shift, 1) * sin_b


def _fold_gain(tables, gain, shift, scale):
    cos_t, sin_a, sin_b = tables
    g = gain.reshape(1, LANES) * scale
    return cos_t * g, sin_a * jnp.roll(g, shift, axis=1), sin_b * jnp.roll(g, -shift, axis=1)


def _rms_rope(x, n, cos_t, sin_a, sin_b, shift):
    r = lax.rsqrt(jnp.sum(x * x, axis=-1, keepdims=True) * (1.0 / n) + NORM_EPS)
    return r * _rope(x, cos_t, sin_a, sin_b, shift)


def _mla_q_body(cq_ref, ckv_ref, qa_ref, kva_ref, wuq_ref, cos_ref, sa_ref, sb_ref, q_ref, ckvn_ref):
    ckvn_ref[...] = _rms(ckv_ref[...], kva_ref[...])
    cqn = _rms(cq_ref[...], qa_ref[...])
    q = _dot(cqn.astype(BF16), wuq_ref[...])
    cos_t, sin_a, sin_b = cos_ref[...], sa_ref[...], sb_ref[...]
    for h in range(MLA_HEADS):
        sl = slice(h * LANES, (h + 1) * LANES)
        q_ref[:, sl] = _rms_rope(q[:, sl], MLA_QK, cos_t, sin_a, sin_b, MLA_ROPE // 4).astype(BF16)


def _pad_heads(w, heads, width):
    k = w.shape[0]
    w3 = w.reshape(k, heads, width)
    return jnp.zeros((k, heads, LANES), w.dtype).at[:, :, :width].set(w3).reshape(k, heads * LANES)


def _mla_q(z, p, tables, *, tm, t_lat, lat_seq):
    t = z.shape[0]
    wuq = _pad_heads(p['w_uq'], MLA_HEADS, MLA_QK).astype(BF16)
    qn = jnp.zeros((LANES,), F32).at[:MLA_QK].set(p['q_norm'])
    cos_t, sin_a, sin_b = _fold_gain(tables, qn, MLA_ROPE // 4, MLA_QK ** -0.5 * LOG2_E)
    pos_blocks = lat_seq // tm
    lat_tiles = t_lat // tm
    tspec = pl.BlockSpec((tm, LANES), lambda i: (jnp.where(i < lat_tiles, i % pos_blocks, pos_blocks), 0))
    full = lambda i: (0, 0)
    cq_blk = (3 * HY_WIDTH) // MLA_Q_RANK
    ckv_blk = (3 * HY_WIDTH + MLA_Q_RANK) // MLA_KV_RANK
    return pl.pallas_call(
        _mla_q_body,
        grid=(t // tm,),
        in_specs=[pl.BlockSpec((tm, MLA_Q_RANK), lambda i: (i, cq_blk)),
                  pl.BlockSpec((tm, MLA_KV_RANK), lambda i: (i, ckv_blk)),
                  pl.BlockSpec((1, MLA_Q_RANK), full),
                  pl.BlockSpec((1, MLA_KV_RANK), full),
                  pl.BlockSpec((MLA_Q_RANK, MLA_HEADS * LANES), full),
                  tspec, tspec, tspec],
        out_specs=[pl.BlockSpec((tm, MLA_HEADS * LANES), lambda i: (i, 0)),
                   pl.BlockSpec((tm, MLA_KV_RANK), lambda i: (i, 0))],
        out_shape=[jax.ShapeDtypeStruct((t, MLA_HEADS * LANES), BF16),
                   jax.ShapeDtypeStruct((t, MLA_KV_RANK), F32)],
        compiler_params=_params("parallel"),
        name="mla_q",
    )(z, z, p['qa_norm'].reshape(1, -1), p['kva_norm'].reshape(1, -1), wuq, cos_t, sin_a, sin_b)


def _mla_kv_body(ckvn_ref, kr_ref, wk_ref, wv_ref, cos_ref, sa_ref, sb_ref, k_ref, v_ref):
    c = ckvn_ref[...].astype(BF16)
    k = _dot(c, wk_ref[...])
    v_ref[...] = _dot(c, wv_ref[...]).astype(BF16)
    kr = pltpu.roll(kr_ref[...], MLA_NOPE, 1)
    cos_t, sin_a, sin_b = cos_ref[...], sa_ref[...], sb_ref[...]
    for h in range(MLA_HEADS):
        sl = slice(h * LANES, (h + 1) * LANES)
        k_ref[:, sl] = _rms_rope(k[:, sl] + kr, MLA_QK, cos_t, sin_a, sin_b, MLA_ROPE // 4).astype(BF16)


def _mla_kv(ckvn_rows, kr_rows, p, tables, *, tm, nb, past, lat_seq):
    r = ckvn_rows.shape[0]
    w = p['w_ukv'].reshape(MLA_KV_RANK, MLA_HEADS, MLA_NOPE + MLA_V)
    wk = _pad_heads(w[:, :, :MLA_NOPE].reshape(MLA_KV_RANK, -1), MLA_HEADS, MLA_NOPE).astype(BF16)
    wv = w[:, :, MLA_NOPE:].reshape(MLA_KV_RANK, MLA_HEADS * MLA_V).astype(BF16)
    kn = jnp.zeros((LANES,), F32).at[:MLA_QK].set(p['k_norm'])
    cos_t, sin_a, sin_b = _fold_gain(tables, kn, MLA_ROPE // 4, 1.0)
    per_b = (past + lat_seq) // tm
    past_tiles = past // tm
    pos_blocks = lat_seq // tm
    lat_tiles = nb * per_b

    def tmap(i):
        j = i % per_b
        is_pos = jnp.logical_and(i < lat_tiles, j >= past_tiles)
        return (jnp.where(is_pos, j - past_tiles, pos_blocks), 0)

    tspec = pl.BlockSpec((tm, LANES), tmap)
    full = lambda i: (0, 0)
    return pl.pallas_call(
        _mla_kv_body,
        grid=(r // tm,),
        in_specs=[pl.BlockSpec((tm, MLA_KV_RANK), lambda i: (i, 0)),
                  pl.BlockSpec((tm, LANES), lambda i: (i, 0)),
                  pl.BlockSpec((MLA_KV_RANK, MLA_HEADS * LANES), full),
                  pl.BlockSpec((MLA_KV_RANK, MLA_HEADS * MLA_V), full),
                  tspec, tspec, tspec],
        out_specs=[pl.BlockSpec((tm, MLA_HEADS * LANES), lambda i: (i, 0)),
                   pl.BlockSpec((tm, MLA_HEADS * MLA_V), lambda i: (i, 0))],
        out_shape=[jax.ShapeDtypeStruct((r, MLA_HEADS * LANES), BF16),
                   jax.ShapeDtypeStruct((r, MLA_HEADS * MLA_V), BF16)],
        compiler_params=_params("parallel"),
        name="mla_kv",
    )(ckvn_rows, kr_rows, wk, wv, cos_t, sin_a, sin_b)


ATT_CHUNK = 512
ATT_UNIT_ROWS = 256
ATT_TQ = 1024


def _fill_vaug(vaug_ref, g, v_blocks):
    off = 0
    for v in v_blocks:
        n = v.shape[0]
        vaug_ref[g, off:off + n, :LANES] = v
        off += n
    vaug_ref[g, :, LANES:] = jnp.ones((vaug_ref.shape[1], LANES), BF16)


def _softmax_pv(units, s_ref, n_keys):
    chunk = min(ATT_CHUNK, n_keys)
    chunks = [slice(c * chunk, (c + 1) * chunk) for c in range(n_keys // chunk)]

    def scores(u, rows, m_lane):
        s = _dot_nt(units[u][0], units[u][1](rows))
        s_ref[u % 2, :, rows] = s
        for j in range(chunk // LANES):
            blk = s[:, j * LANES:(j + 1) * LANES]
            m_lane = blk if m_lane is None else jnp.maximum(m_lane, blk)
        return m_lane

    def values(u, rows, m, acc):
        p = jnp.exp2(s_ref[u % 2, :, rows] - m).astype(BF16)
        d = _dot(p, units[u][2](rows))
        return d if acc is None else acc + d

    outs = []
    m_lane = None
    for rows in chunks:
        m_lane = scores(0, rows, m_lane)
    for u in range(len(units)):
        m = jnp.max(m_lane, axis=-1, keepdims=True)
        acc, m_lane = None, None
        for rows in chunks:
            acc = values(u, rows, m, acc)
            if u + 1 < len(units):
                m_lane = scores(u + 1, rows, m_lane)
        outs.append(acc)
    return outs


def _mla_attn_body(prev_ref, q_ref, k_ref, v_ref, o_ref, vaug_ref, s_ref):
    pairs = vaug_ref.shape[0]

    @pl.when(pl.program_id(2) == 0)
    def _():
        for g in range(pairs):
            _fill_vaug(vaug_ref, g, [v_ref[:, g * LANES:(g + 1) * LANES]])

    n_keys = k_ref.shape[0]
    tq = q_ref.shape[0]
    ur = s_ref.shape[1]
    units, slots = [], []
    for r0 in range(0, tq, ur):
        for g in range(pairs):
            slots.append((r0, g))
            for hh in range(2):
                sl = slice((2 * g + hh) * LANES, (2 * g + hh + 1) * LANES)
                units.append((q_ref[r0:r0 + ur, sl], lambda rows, sl=sl: k_ref[rows, sl],
                              lambda rows, g=g: vaug_ref[g, rows, :]))
    res = [r[:, :LANES] / r[:, LANES:] for r in _softmax_pv(units, s_ref, n_keys)]
    lane = lax.broadcasted_iota(jnp.int32, res[0].shape, 1)
    for i, (r0, g) in enumerate(slots):
        o_ref[r0:r0 + ur, g * LANES:(g + 1) * LANES] = jnp.where(
            lane < MLA_V, res[2 * i], res[2 * i + 1]).astype(BF16)


def _mla_attn(prev, q, k, v, *, tq, pairs, n_seq, seq_q, seq_k, q_row0, k_row0):
    hp = MLA_HEADS // 2 // pairs
    nq = seq_q // tq
    qb0, kb0 = q_row0 // tq, k_row0 // seq_k
    return pl.pallas_call(
        _mla_attn_body,
        grid=(n_seq, hp, nq),
        in_specs=[pl.BlockSpec(memory_space=pl.ANY),
                  pl.BlockSpec((tq, pairs * 2 * LANES), lambda s, h, i: (qb0 + s * nq + i, h)),
                  pl.BlockSpec((seq_k, pairs * 2 * LANES), lambda s, h, i: (kb0 + s, h)),
                  pl.BlockSpec((seq_k, pairs * 2 * MLA_V), lambda s, h, i: (kb0 + s, h))],
        out_specs=pl.BlockSpec((tq, pairs * 2 * MLA_V), lambda s, h, i: (qb0 + s * nq + i, h)),
        out_shape=jax.ShapeDtypeStruct(prev.shape, prev.dtype),
        input_output_aliases={0: 0},
        scratch_shapes=[pltpu.VMEM((pairs, seq_k, 2 * LANES), BF16),
                        pltpu.VMEM((2, min(tq, ATT_UNIT_ROWS), seq_k), F32)],
        compiler_params=_params("arbitrary", "arbitrary", "arbitrary"),
        name="mla_attn",
    )(prev, q, k, v)


def _diff_attn_body(prev_ref, *refs, n_seg, lambda_init):
    q_ref, lam_ref, sub_ref = refs[0], refs[1], refs[2]
    k_refs = refs[3:3 + n_seg]
    v_refs = refs[3 + n_seg:3 + 2 * n_seg]
    o_ref, kcat_ref, vaug_ref, s_ref = refs[3 + 2 * n_seg:]
    heads = vaug_ref.shape[0]

    def head_block(ref, g):
        return ref[g] if len(ref.shape) == 3 else ref[:, g * LANES:(g + 1) * LANES]

    @pl.when(pl.program_id(2) == 0)
    def _():
        for g in range(heads):
            _fill_vaug(vaug_ref, g, [head_block(v, g) for v in v_refs])
            off = 0
            for k in k_refs:
                kcat_ref[g, off:off + k.shape[-2], :] = head_block(k, g)
                off += k.shape[-2]

    lp = lam_ref[...]
    lam = (jnp.exp(jnp.sum(lp[0:1] * lp[1:2], axis=-1, keepdims=True))
           - jnp.exp(jnp.sum(lp[2:3] * lp[3:4], axis=-1, keepdims=True)) + lambda_init)
    n_keys = kcat_ref.shape[1]
    tq = q_ref.shape[0]
    ur = s_ref.shape[1]
    units, slots = [], []
    for r0 in range(0, tq, ur):
        for g in range(heads):
            slots.append((r0, g))
            q = q_ref[r0:r0 + ur, g * LANES:(g + 1) * LANES].astype(F32)
            lane = lax.broadcasted_iota(jnp.int32, q.shape, 1)
            k_of = lambda rows, g=g: kcat_ref[g, rows, :]
            v_of = lambda rows, g=g: vaug_ref[g, rows, :]
            units.append((jnp.where(lane < DIFF_DH, q, 0.0).astype(BF16), k_of, v_of))
            units.append((jnp.where(lane < DIFF_DH, 0.0, q).astype(BF16), k_of, v_of))
    res = _softmax_pv(units, s_ref, n_keys)
    for i, (r0, g) in enumerate(slots):
        r1, r2 = res[2 * i], res[2 * i + 1]
        o = r1[:, :LANES] / r1[:, LANES:] - (lam / r2[:, LANES:]) * r2[:, :LANES]
        o_ref[r0:r0 + ur, g * LANES:(g + 1) * LANES] = (
            _rms(o, sub_ref[...]) * (1.0 - lambda_init)).astype(BF16)


def _diff_attn(prev, q, k_new, v_new, k_cache, v_cache, lam_p, subln, *, tq, heads, n_seq, seq_q, q_row0,
               lambda_init):
    nq = seq_q // tq
    qb0 = q_row0 // tq
    sb0 = q_row0 // seq_q
    d = 2 * DIFF_DH
    new_spec = pl.BlockSpec((seq_q, heads * d), lambda s, h, i: (sb0 + s, h))
    if k_cache is None:
        n_seg, k_args, v_args, k_specs, v_specs = 1, [k_new], [v_new], [new_spec], [new_spec]
        n_keys = seq_q
    else:
        past = k_cache.shape[2]
        c_spec = pl.BlockSpec((None, heads, past, d), lambda s, h, i: (s, h, 0, 0))
        n_seg, k_args, v_args = 2, [k_cache, k_new], [v_cache, v_new]
        k_specs, v_specs = [c_spec, new_spec], [c_spec, new_spec]
        n_keys = past + seq_q
    return pl.pallas_call(
        functools.partial(_diff_attn_body, n_seg=n_seg, lambda_init=lambda_init),
        grid=(n_seq, DIFF_HEADS // heads, nq),
        in_specs=[pl.BlockSpec(memory_space=pl.ANY),
                  pl.BlockSpec((tq, heads * d), lambda s, h, i: (qb0 + s * nq + i, h)),
                  pl.BlockSpec((4, DIFF_DH), lambda s, h, i: (0, 0)),
                  pl.BlockSpec((1, d), lambda s, h, i: (0, 0))] + k_specs + v_specs,
        out_specs=pl.BlockSpec((tq, heads * d), lambda s, h, i: (qb0 + s * nq + i, h)),
        out_shape=jax.ShapeDtypeStruct(prev.shape, prev.dtype),
        input_output_aliases={0: 0},
        scratch_shapes=[pltpu.VMEM((heads, n_keys, d), BF16), pltpu.VMEM((heads, n_keys, 2 * LANES), BF16),
                        pltpu.VMEM((2, min(tq, ATT_UNIT_ROWS), n_keys), F32)],
        compiler_params=_params("arbitrary", "arbitrary", "arbitrary"),
        name="diff_attn",
    )(prev, q, lam_p, subln.reshape(1, d), *k_args, *v_args)


def _proj_res_body(*refs, n_in, gate_row):
    x_ref, mod_ref, o_ref = refs[0], refs[1], refs[-1]
    acc = None
    for j in range(n_in):
        a_ref, w_ref = refs[2 + 2 * j], refs[3 + 2 * j]
        d = _dot(a_ref[...].astype(BF16), w_ref[...])
        acc = d if acc is None else acc + d
    o_ref[...] = x_ref[...] + mod_ref[gate_row:gate_row + 1, :] * acc


def _proj_res(x, mod, acts, ws, *, tm, seg_len, gate_row):
    t, d = x.shape
    in_specs = [pl.BlockSpec((tm, d), lambda i: (i, 0)),
                pl.BlockSpec((None, 6, d), lambda i: (i * tm // seg_len, 0, 0))]
    args = [x, mod]
    for a, w in zip(acts, ws):
        in_specs += [pl.BlockSpec((tm, a.shape[1]), lambda i: (i, 0)),
                     pl.BlockSpec(w.shape, lambda i: (0, 0))]
        args += [a, w]
    return pl.pallas_call(
        functools.partial(_proj_res_body, n_in=len(acts), gate_row=gate_row),
        grid=(t // tm,),
        in_specs=in_specs,
        out_specs=pl.BlockSpec((tm, d), lambda i: (i, 0)),
        out_shape=jax.ShapeDtypeStruct((t, d), F32),
        compiler_params=_params("parallel"),
        name="proj_res",
    )(*args)


def _even_out_body(xa_ref, xb_ref, mod_ref, ca_ref, cb_ref, u_ref, x0_ref, db_ref, o_ref, wa_ref, wb_ref,
                   out_ref, *, na):
    first = pl.program_id(0) < na
    x = jnp.where(first, xa_ref[...], xb_ref[...])
    conv = jnp.where(first, ca_ref[...], cb_ref[...])
    y_hy = (conv + u_ref[...] * db_ref[...]) * x0_ref[...]
    acc = _dot(y_hy.astype(BF16), wa_ref[...]) + _dot(o_ref[...], wb_ref[...])
    out_ref[...] = x + mod_ref[2:3, :] * acc


def _even_out(xa, xb, mod, conv_a, conv_b, u, x0, dbias, o, w_hy, w_att, *, tm, seg_len):
    d = xa.shape[1]
    t = xa.shape[0] + xb.shape[0]
    c = u.shape[1]
    xa_spec, xb_spec, na = _two_part_specs(xa, xb, tm)
    ca_spec, cb_spec, na_c = _two_part_specs(conv_a, conv_b, tm)
    assert na == na_c
    row = lambda cols: pl.BlockSpec((tm, cols), lambda i: (i, 0))
    full = lambda a: pl.BlockSpec(a.shape, lambda i: (0, 0))
    return pl.pallas_call(
        functools.partial(_even_out_body, na=na),
        grid=(t // tm,),
        in_specs=[xa_spec, xb_spec,
                  pl.BlockSpec((None, 6, d), lambda i: (i * tm // seg_len, 0, 0)),
                  ca_spec, cb_spec, row(c), row(c), full(dbias), row(o.shape[1]), full(w_hy), full(w_att)],
        out_specs=row(d),
        out_shape=jax.ShapeDtypeStruct((t, d), F32),
        compiler_params=_params("arbitrary"),
        name="even_out",
    )(xa, xb, mod, conv_a, conv_b, u, x0, dbias, o, w_hy, w_att)


def _ffn_body(x_ref, mod_ref, g_ref, wg_ref, wu_ref, wd_ref, o_ref, h_ref, acc_ref):
    f = pl.program_id(1)

    @pl.when(f == 0)
    def _():
        h_ref[...] = _norm_mod(x_ref[...], g_ref[...], mod_ref[3:4, :], mod_ref[4:5, :]).astype(BF16)
        acc_ref[...] = jnp.zeros_like(acc_ref)

    h = h_ref[...]
    a = _silu(_dot(h, wg_ref[...])) * _dot(h, wu_ref[...])
    acc_ref[...] += _dot(a.astype(BF16), wd_ref[...])

    @pl.when(f == pl.num_programs(1) - 1)
    def _():
        o_ref[...] = x_ref[...] + mod_ref[5:6, :] * acc_ref[...]


def _ffn(x, mod, g, wg, wu, wd, *, tm, tf, seg_len):
    t, d = x.shape
    ff = wg.shape[1]
    return pl.pallas_call(
        _ffn_body,
        grid=(t // tm, ff // tf),
        in_specs=[pl.BlockSpec((tm, d), lambda i, f: (i, 0)),
                  pl.BlockSpec((None, 6, d), lambda i, f: (i * tm // seg_len, 0, 0)),
                  pl.BlockSpec((1, d), lambda i, f: (0, 0)),
                  pl.BlockSpec((d, tf), lambda i, f: (0, f)),
                  pl.BlockSpec((d, tf), lambda i, f: (0, f)),
                  pl.BlockSpec((tf, d), lambda i, f: (f, 0))],
        out_specs=pl.BlockSpec((tm, d), lambda i, f: (i, 0)),
        out_shape=jax.ShapeDtypeStruct((t, d), F32),
        scratch_shapes=[pltpu.VMEM((tm, d), BF16), pltpu.VMEM((tm, d), F32)],
        compiler_params=_params("parallel", "arbitrary"),
        name="ffn",
    )(x, mod, g.reshape(1, d), wg, wu, wd)


def _group_ms(x, gmat):
    hi, lo = _split_bf16(x * x)
    return (_dot(hi, gmat) + _dot(lo, gmat)) * (1.0 / DIFF_DH)


def _qkv_body(x_ref, mod_ref, g_ref, w_ref, gm_ref, kn_ref, qc_ref, qa_ref, qb_ref, kc_ref, ka_ref, kb_ref,
              q_ref, k_ref, v_ref, kf_ref, vf_ref, *, lat_tiles, seq):
    h = _norm_mod(x_ref[...], g_ref[...], mod_ref[0:1, :], mod_ref[1:2, :]).astype(BF16)
    z = _dot(h, w_ref[...])
    hd = DIFF_HEADS * 2 * DIFF_DH
    tm = z.shape[0]
    gm = gm_ref[...]
    shift = DIFF_DH // 4
    is_ctx = pl.program_id(0) >= lat_tiles
    for hh in range(DIFF_HEADS):
        sl = slice(hh * LANES, (hh + 1) * LANES)
        qh = z[:, hh * LANES:(hh + 1) * LANES]
        rq = lax.rsqrt(_group_ms(qh, gm) + NORM_EPS)
        q_ref[:, sl] = (rq * _rope(qh, qc_ref[...], qa_ref[...], qb_ref[...], shift)).astype(BF16)
        kh = z[:, hd + hh * LANES:hd + (hh + 1) * LANES]
        rk = lax.rsqrt(_group_ms(kh, gm) + NORM_EPS)
        k_ref[:, sl] = (rk * _rope(kh, kc_ref[...], ka_ref[...], kb_ref[...], shift)).astype(BF16)

        @pl.when(is_ctx)
        def _():
            kn = kh * rk * kn_ref[...]
            vh = z[:, 2 * hd + hh * LANES:2 * hd + (hh + 1) * LANES]
            for s in range(tm // seq):
                kf_ref[s, hh] = kn[s * seq:(s + 1) * seq, :]
                vf_ref[s, hh] = vh[s * seq:(s + 1) * seq, :]
    v_ref[...] = z[:, 2 * hd:].astype(BF16)


def _qkv(x, mod, p, tables, *, tm, seg_len, t_lat, lat_seq, n_ctx, ctx_seq):
    t, d = x.shape
    hd = DIFF_HEADS * 2 * DIFF_DH
    wqk = p['w_qkv'][:, :2 * hd].reshape(d, 2, 2, DIFF_HEADS, DIFF_DH)
    wqk = wqk.transpose(0, 1, 3, 2, 4).reshape(d, 2 * hd)
    w = jnp.concatenate([wqk, p['w_qkv'][:, 2 * hd:]], axis=1).astype(BF16)
    gi = np.arange(LANES) // DIFF_DH
    gmat = jnp.asarray((gi[:, None] == gi[None, :]).astype(np.float32)).astype(BF16)
    qn = jnp.tile(p['q_norm'], 2)
    kn = jnp.tile(p['k_norm'], 2)
    q_tabs = _fold_gain(tables, qn, DIFF_DH // 4, DIFF_DH ** -0.5 * LOG2_E)
    k_tabs = _fold_gain(tables, kn, DIFF_DH // 4, 1.0)
    pos_blocks = lat_seq // tm
    lat_tiles = t_lat // tm
    seq_per_tile = tm // ctx_seq
    tspec = pl.BlockSpec((tm, LANES), lambda i: (jnp.where(i < lat_tiles, i % pos_blocks, pos_blocks), 0))
    full = lambda i: (0, 0)
    row = pl.BlockSpec((tm, hd), lambda i: (i, 0))
    fspec = pl.BlockSpec((seq_per_tile, DIFF_HEADS, ctx_seq, LANES),
                         lambda i: (jnp.maximum(i - lat_tiles, 0), 0, 0, 0))
    fshape = jax.ShapeDtypeStruct((n_ctx, DIFF_HEADS, ctx_seq, LANES), F32)
    return pl.pallas_call(
        functools.partial(_qkv_body, lat_tiles=lat_tiles, seq=ctx_seq),
        grid=(t // tm,),
        in_specs=[pl.BlockSpec((tm, d), lambda i: (i, 0)),
                  pl.BlockSpec((None, 6, d), lambda i: (i * tm // seg_len, 0, 0)),
                  pl.BlockSpec((1, d), full),
                  pl.BlockSpec((d, 3 * hd), full),
                  pl.BlockSpec((LANES, LANES), full),
                  pl.BlockSpec((1, LANES), full),
                  tspec, tspec, tspec, tspec, tspec, tspec],
        out_specs=[row, row, row, fspec, fspec],
        out_shape=[jax.ShapeDtypeStruct((t, hd), BF16)] * 3 + [fshape, fshape],
        compiler_params=_params("arbitrary"),
        name="qkv",
    )(x, mod, p['norm1'].reshape(1, d), w, gmat, kn.reshape(1, LANES), *q_tabs, *k_tabs)


def _route(logits):
    lane = lax.broadcasted_iota(jnp.int32, logits.shape, 1)
    neg = jnp.float32(-jnp.inf)
    lg = jnp.where(lane < N_EXPERTS, logits, neg)
    m1 = jnp.max(lg, axis=-1, keepdims=True)
    i1 = jnp.min(jnp.where(lg == m1, lane, LANES), axis=-1, keepdims=True)
    lg2 = jnp.where(lane == i1, neg, lg)
    m2 = jnp.max(lg2, axis=-1, keepdims=True)
    i2 = jnp.min(jnp.where(lg2 == m2, lane, LANES), axis=-1, keepdims=True)
    e = jnp.exp(m2 - m1)
    w1 = 1.0 / (1.0 + e)
    w2 = e / (1.0 + e)
    return jnp.where(lane == i1, w1, 0.0) + jnp.where(lane == i2, w2, 0.0)


MOE_BLOCK = 1024
MOE_SUB = 256
MOE_ROUTE_TM = 512


def _moe_route_body(x_ref, mod_ref, g_ref, wr_ref, h_ref, gates_ref, rank_ref, rank_t_ref,
                    carry_row, carry_col, *, tm):
    i = pl.program_id(0)

    @pl.when(i == 0)
    def _():
        carry_row[...] = jnp.zeros_like(carry_row)
        carry_col[...] = jnp.zeros_like(carry_col)

    h = _norm_mod(x_ref[...], g_ref[...], mod_ref[3:4, :], mod_ref[4:5, :])
    h_ref[...] = h.astype(BF16)
    gates = _route(_dot_f32(h, wr_ref[...]))
    gates_ref[...] = gates
    sel = jnp.where(gates != 0.0, 1.0, 0.0)
    sel_t = sel.T
    r = lax.broadcasted_iota(jnp.int32, (tm, tm), 0)
    c = lax.broadcasted_iota(jnp.int32, (tm, tm), 1)
    lower = jnp.where(c < r, 1.0, 0.0).astype(BF16)
    upper = jnp.where(r < c, 1.0, 0.0).astype(BF16)
    before = _dot(lower, sel.astype(BF16)) + carry_row[...]
    before_t = _dot(sel_t.astype(BF16), upper) + carry_col[...]
    rank_ref[...] = jnp.where(sel > 0.0, before, -1.0)
    rank_t = jnp.where(sel_t > 0.0, before_t, -1.0)
    for s in range(tm // MOE_SUB):
        rank_t_ref[s] = rank_t[:SUBLANES, s * MOE_SUB:(s + 1) * MOE_SUB]
    carry_row[...] += jnp.sum(sel, axis=0, keepdims=True)
    carry_col[...] += jnp.sum(sel_t, axis=1, keepdims=True)


def _moe_route(x, mod, g, w_router, *, seg_len):
    t, d = x.shape
    tm = MOE_ROUTE_TM
    ne = w_router.shape[1]
    assert ne <= SUBLANES
    wr = jnp.zeros((d, LANES), F32).at[:, :ne].set(w_router)
    sub = tm // MOE_SUB
    return pl.pallas_call(
        functools.partial(_moe_route_body, tm=tm),
        grid=(t // tm,),
        in_specs=[pl.BlockSpec((tm, d), lambda i: (i, 0)),
                  pl.BlockSpec((None, 6, d), lambda i: (i * tm // seg_len, 0, 0)),
                  pl.BlockSpec((1, d), lambda i: (0, 0)),
                  pl.BlockSpec((d, LANES), lambda i: (0, 0))],
        out_specs=[pl.BlockSpec((tm, d), lambda i: (i, 0)),
                   pl.BlockSpec((tm, LANES), lambda i: (i, 0)),
                   pl.BlockSpec((tm, LANES), lambda i: (i, 0)),
                   pl.BlockSpec((sub, SUBLANES, MOE_SUB), lambda i: (i, 0, 0))],
        out_shape=[jax.ShapeDtypeStruct((t, d), BF16),
                   jax.ShapeDtypeStruct((t, LANES), F32),
                   jax.ShapeDtypeStruct((t, LANES), F32),
                   jax.ShapeDtypeStruct((t // MOE_SUB, SUBLANES, MOE_SUB), F32)],
        scratch_shapes=[pltpu.VMEM((1, LANES), F32), pltpu.VMEM((LANES, 1), F32)],
        compiler_params=_params("arbitrary"),
        name="moe_route",
    )(x, mod, g.reshape(1, d), wr)


def _moe_plan(rank, ne, *, n_blocks):
    t = rank.shape[0]
    n_tiles = t // MOE_SUB
    per_blk = MOE_BLOCK // MOE_SUB
    n_sub = n_blocks * per_blk
    sel = (rank[:, :ne] >= 0.0).astype(jnp.int32)
    tile_cnt = sel.reshape(n_tiles, MOE_SUB, ne).sum(axis=1)
    tile_end = jnp.cumsum(tile_cnt, axis=0)
    tile_start = tile_end - tile_cnt
    cnt = tile_end[-1]
    nblk = (cnt + MOE_BLOCK - 1) // MOE_BLOCK
    bend = jnp.cumsum(nblk)
    bstart = bend - nblk
    e_last = jnp.max(jnp.where(cnt > 0, jnp.arange(ne), 0))
    b = jnp.arange(n_blocks)
    blk_valid = b < bend[-1]
    blk_e = jnp.minimum(jnp.sum(bend[None, :] <= b[:, None], axis=1), e_last).astype(jnp.int32)
    blk_r0 = (b - bstart[blk_e]) * MOE_BLOCK
    blk_rows = jnp.where(blk_valid, jnp.clip(cnt[blk_e] - blk_r0, 0, MOE_BLOCK), 0).astype(jnp.int32)
    j = jnp.arange(n_sub)
    sub_e = blk_e[j // per_blk]
    sub_r0 = blk_r0[j // per_blk] + (j % per_blk) * MOE_SUB
    sub_valid = jnp.logical_and(blk_valid[j // per_blk], sub_r0 < cnt[sub_e])
    ends = tile_end[:, sub_e]
    r1 = jnp.minimum(sub_r0 + MOE_SUB, cnt[sub_e])
    c_lo = jnp.sum(ends <= sub_r0[None, :], axis=0)
    c_hi = jnp.sum(ends < r1[None, :], axis=0)
    c_lo = jnp.where(sub_valid, c_lo, 1).astype(jnp.int32)
    c_hi = jnp.where(sub_valid, jnp.minimum(c_hi, n_tiles - 1), 0).astype(jnp.int32)
    base = (bstart * MOE_BLOCK).astype(jnp.int32)
    j0 = jnp.minimum((base[None, :] + tile_start) // MOE_SUB, n_sub - 2).astype(jnp.int32)
    return dict(blk_e=blk_e, blk_valid=blk_valid.astype(jnp.int32), blk_rows=blk_rows,
                sub_e=sub_e.astype(jnp.int32), sub_r0=sub_r0.astype(jnp.int32), c_lo=c_lo, c_hi=c_hi,
                base=base, j0=j0.reshape(-1))


def _moe_dispatch_body(e_ref, r0_ref, lo_ref, hi_ref, h_ref, rank_t_ref, xs_ref, acc_ref):
    j = pl.program_id(0)
    e = e_ref[j]
    rows = (r0_ref[j] + lax.broadcasted_iota(jnp.int32, (MOE_SUB, 1), 0)).astype(F32)
    sub = lax.broadcasted_iota(jnp.int32, (SUBLANES, MOE_SUB), 0)
    acc_ref[...] = jnp.zeros_like(acc_ref)

    def step(c, carry):
        rk = jnp.sum(jnp.where(sub == e, rank_t_ref[c], 0.0), axis=0, keepdims=True)
        onehot = jnp.where(rk == rows, 1.0, 0.0).astype(BF16)
        off = pl.multiple_of(c * MOE_SUB, MOE_SUB)
        acc_ref[...] += _dot(onehot, h_ref[pl.ds(off, MOE_SUB), :])
        return carry

    lax.fori_loop(lo_ref[j], hi_ref[j] + 1, step, 0)
    xs_ref[...] = acc_ref[...].astype(BF16)


def _moe_dispatch(h, rank_t, plan, *, n_sub):
    t, d = h.shape
    grid_spec = pltpu.PrefetchScalarGridSpec(
        num_scalar_prefetch=4,
        grid=(n_sub,),
        in_specs=[pl.BlockSpec((t, d), lambda j, *_: (0, 0), pipeline_mode=pl.Buffered(1)),
                  pl.BlockSpec(rank_t.shape, lambda j, *_: (0, 0, 0), pipeline_mode=pl.Buffered(1))],
        out_specs=pl.BlockSpec((MOE_SUB, d), lambda j, *_: (j, 0)),
        scratch_shapes=[pltpu.VMEM((MOE_SUB, d), F32)],
    )
    return pl.pallas_call(
        _moe_dispatch_body,
        grid_spec=grid_spec,
        out_shape=jax.ShapeDtypeStruct((n_sub * MOE_SUB, d), BF16),
        compiler_params=_params("arbitrary"),
        name="moe_dispatch",
    )(plan['sub_e'], plan['sub_r0'], plan['c_lo'], plan['c_hi'], h, rank_t)


def _moe_ffn_body(e_ref, valid_ref, rows_ref, xs_ref, wg_ref, wu_ref, wd_ref, y_ref, acc_ref):
    b = pl.program_id(0)
    f = pl.program_id(1)
    n_rows = rows_ref[b]
    last = f == pl.num_programs(1) - 1
    wg = wg_ref[...].astype(BF16)
    wu = wu_ref[...].astype(BF16)
    wd = wd_ref[...].astype(BF16)
    full = n_rows == MOE_BLOCK

    def swiglu(h):
        a = _silu(_dot(h, wg)) * _dot(h, wu)
        return _dot(a.astype(BF16), wd)

    @pl.when(jnp.logical_and(full, f == 0))
    def _():
        acc_ref[...] = swiglu(xs_ref[...])

    @pl.when(jnp.logical_and(full, f > 0))
    def _():
        acc_ref[...] += swiglu(xs_ref[...])

    @pl.when(jnp.logical_and(full, last))
    def _():
        y_ref[...] = acc_ref[...].astype(BF16)

    for s in range(MOE_BLOCK // MOE_SUB):
        sl = slice(s * MOE_SUB, (s + 1) * MOE_SUB)
        live = jnp.logical_and(jnp.logical_not(full), s * MOE_SUB < n_rows)
        dead = jnp.logical_and(jnp.logical_not(full), s * MOE_SUB >= n_rows)

        @pl.when(jnp.logical_and(live, f == 0))
        def _():
            acc_ref[sl, :] = jnp.zeros((MOE_SUB, acc_ref.shape[1]), F32)

        @pl.when(live)
        def _():
            acc_ref[sl, :] += swiglu(xs_ref[sl, :])

        @pl.when(jnp.logical_and(live, last))
        def _():
            y_ref[sl, :] = acc_ref[sl, :].astype(BF16)

        @pl.when(jnp.logical_and(dead, last))
        def _():
            y_ref[sl, :] = jnp.zeros((MOE_SUB, y_ref.shape[1]), BF16)


def _moe_ffn(xs, wg, wu, wd, plan, *, n_blocks, tf):
    _, d = xs.shape
    ne, _, ff = wg.shape
    nf = ff // tf

    def w_in(b, f, e_ref, valid_ref, rows_ref):
        return (e_ref[b], 0, jnp.where(valid_ref[b] > 0, f, nf - 1))

    def w_down(b, f, e_ref, valid_ref, rows_ref):
        return (e_ref[b], jnp.where(valid_ref[b] > 0, f, nf - 1), 0)

    grid_spec = pltpu.PrefetchScalarGridSpec(
        num_scalar_prefetch=3,
        grid=(n_blocks, nf),
        in_specs=[pl.BlockSpec((MOE_BLOCK, d), lambda b, f, *_: (b, 0)),
                  pl.BlockSpec((None, d, tf), w_in),
                  pl.BlockSpec((None, d, tf), w_in),
                  pl.BlockSpec((None, tf, d), w_down)],
        out_specs=pl.BlockSpec((MOE_BLOCK, d), lambda b, f, *_: (b, 0)),
        scratch_shapes=[pltpu.VMEM((MOE_BLOCK, d), F32)],
    )
    return pl.pallas_call(
        _moe_ffn_body,
        grid_spec=grid_spec,
        out_shape=jax.ShapeDtypeStruct((n_blocks * MOE_BLOCK, d), BF16),
        compiler_params=_params("arbitrary", "arbitrary"),
        name="moe_ffn",
    )(plan['blk_e'], plan['blk_valid'], plan['blk_rows'], xs, wg, wu, wd)


def _moe_combine_body(j0_ref, base_ref, x_ref, mod_ref, gates_ref, rank_ref, *rest, ne, split_tiles):
    y_refs, o_ref, o2_ref = rest[:2 * ne], rest[2 * ne], rest[2 * ne + 1]
    c = pl.program_id(0)
    gates = gates_ref[...]
    rank = rank_ref[...]
    lane = lax.broadcasted_iota(jnp.int32, gates.shape, 1)
    col = lax.broadcasted_iota(jnp.int32, (1, MOE_SUB), 1).astype(F32)
    acc = None
    for e in range(ne):
        pick = lane == e
        g = jnp.sum(jnp.where(pick, gates, 0.0), axis=-1, keepdims=True)
        rk = jnp.sum(jnp.where(pick, rank, 0.0), axis=-1, keepdims=True)
        shift = (base_ref[e] - j0_ref[c * ne + e] * MOE_SUB).astype(F32)
        loc = jnp.where(rk >= 0.0, rk + shift, -1.0)
        qa = jnp.where(loc == col, 1.0, 0.0).astype(BF16)
        qb = jnp.where(loc == col + float(MOE_SUB), 1.0, 0.0).astype(BF16)
        contrib = g * (_dot(qa, y_refs[2 * e][...]) + _dot(qb, y_refs[2 * e + 1][...]))
        acc = contrib if acc is None else acc + contrib
    out = x_ref[...] + mod_ref[5:6, :] * acc

    @pl.when(c < split_tiles)
    def _():
        o_ref[...] = out

    @pl.when(c >= split_tiles)
    def _():
        o2_ref[...] = out


def _moe_combine(x, mod, gates, rank, y, plan, *, ne, seg_len, t_split):
    t, d = x.shape
    tm = MOE_SUB
    split_tiles = t_split // tm
    y_specs = []
    for e in range(ne):
        y_specs += [pl.BlockSpec((MOE_SUB, d), lambda c, j0, base, e=e: (j0[c * ne + e], 0)),
                    pl.BlockSpec((MOE_SUB, d), lambda c, j0, base, e=e: (j0[c * ne + e] + 1, 0))]
    grid_spec = pltpu.PrefetchScalarGridSpec(
        num_scalar_prefetch=2,
        grid=(t // tm,),
        in_specs=[pl.BlockSpec((tm, d), lambda c, *_: (c, 0)),
                  pl.BlockSpec((None, 6, d), lambda c, *_: (c * tm // seg_len, 0, 0)),
                  pl.BlockSpec((tm, LANES), lambda c, *_: (c, 0)),
                  pl.BlockSpec((tm, LANES), lambda c, *_: (c, 0))] + y_specs,
        out_specs=[pl.BlockSpec((tm, d), lambda c, *_: (jnp.minimum(c, split_tiles - 1), 0)),
                   pl.BlockSpec((tm, d), lambda c, *_: (jnp.maximum(c - split_tiles, 0), 0))],
    )
    return pl.pallas_call(
        functools.partial(_moe_combine_body, ne=ne, split_tiles=split_tiles),
        grid_spec=grid_spec,
        out_shape=[jax.ShapeDtypeStruct((t_split, d), F32), jax.ShapeDtypeStruct((t - t_split, d), F32)],
        compiler_params=_params("arbitrary"),
        name="moe_combine",
    )(plan['j0'], plan['base'], x, mod, gates, rank, *([y] * (2 * ne)))


def _moe(x, mod, g, w_router, wg, wu, wd, *, seg_len, t_split, top_k=2):
    t, d = x.shape
    ne = w_router.shape[1]
    n_blocks = t * top_k // MOE_BLOCK + ne
    h, gates, rank, rank_t = _moe_route(x, mod, g, w_router, seg_len=seg_len)
    plan = _moe_plan(rank, ne, n_blocks=n_blocks)
    xs = _moe_dispatch(h, rank_t, plan, n_sub=n_blocks * (MOE_BLOCK // MOE_SUB))
    y = _moe_ffn(xs, wg, wu, wd, plan, n_blocks=n_blocks, tf=512)
    return _moe_combine(x, mod, gates, rank, y, plan, ne=ne, seg_len=seg_len, t_split=t_split)


def _even_layer(x_lat, x_ctx, cond8, p, cache_ckv, cache_kr, *, nb, lat_seq, n_ctx, ctx_seq):
    d = x_lat.shape[1]
    t_lat = nb * lat_seq
    t = t_lat + x_ctx.shape[0]
    seg_len = lat_seq
    past = cache_ckv.shape[1]
    mod = _adaln(cond8, p['w_mod'], p['b_mod'])

    n_in = p['w_in'].shape[1]
    n_pad = -(-n_in // LANES) * LANES
    w_in = jnp.zeros((d, n_pad), BF16).at[:, :n_in].set(p['w_in'].astype(BF16))
    z = _lin_in(x_lat, x_ctx, mod, p['norm1'], w_in, tm=512, seg_len=seg_len)

    u, x0 = _hy_pre(z, p['hy_conv_w'], p['hy_conv_b'], t_lat=t_lat, lat_seq=lat_seq, ctx_seq=ctx_seq)
    k_raw, k_sum = _hyena_filters(p, lat_seq=lat_seq, ctx_seq=ctx_seq)
    dbias = p['hy_dbias'].reshape(1, HY_WIDTH)
    conv_lat = _hyena_lat(k_raw[:2 * lat_seq], k_sum[0:1], u[:t_lat], seq=lat_seq, nb=nb)
    conv_ctx = _hyena_ctx(k_raw[2 * lat_seq:], k_sum[1:2], u, seq=ctx_seq, nseq=n_ctx, u_row0=t_lat)

    tm = 512
    tables = _rope_tables(lat_seq, MLA_ROPE, (MLA_NOPE,), tm)
    q, ckvn = _mla_q(z, p, tables, tm=tm, t_lat=t_lat, lat_seq=lat_seq)
    kr_col = 3 * HY_WIDTH + MLA_Q_RANK + MLA_KV_RANK
    kr = z[:, kr_col:kr_col + LANES]
    cache_kr_p = jnp.zeros((nb, past, LANES), F32).at[:, :, :MLA_ROPE].set(cache_kr)
    ckvn_rows = jnp.concatenate(
        [jnp.concatenate([cache_ckv, ckvn[:t_lat].reshape(nb, lat_seq, -1)], axis=1).reshape(nb * (past + lat_seq), -1),
         ckvn[t_lat:]], axis=0)
    kr_rows = jnp.concatenate(
        [jnp.concatenate([cache_kr_p, kr[:t_lat].reshape(nb, lat_seq, LANES)], axis=1).reshape(nb * (past + lat_seq), LANES),
         kr[t_lat:]], axis=0)
    k_all, v_all = _mla_kv(ckvn_rows, kr_rows, p, tables, tm=tm, nb=nb, past=past, lat_seq=lat_seq)
    o = jnp.zeros((t, MLA_HEADS * MLA_V), BF16)
    o = _mla_attn(o, q, k_all, v_all, tq=ATT_TQ, pairs=1, n_seq=nb, seq_q=lat_seq, seq_k=past + lat_seq,
                  q_row0=0, k_row0=0)
    o = _mla_attn(o, q, k_all, v_all, tq=ctx_seq, pairs=MLA_HEADS // 2, n_seq=n_ctx, seq_q=ctx_seq,
                  seq_k=ctx_seq, q_row0=t_lat, k_row0=nb * (past + lat_seq))

    w_out = p['w_out'].astype(BF16)
    x = _even_out(x_lat, x_ctx, mod, conv_lat, conv_ctx, u, x0, dbias, o, w_out[:HY_WIDTH], w_out[HY_WIDTH:],
                  tm=512, seg_len=seg_len)
    x = _ffn(x, mod, p['norm2'], p['ffn_w_gate'].astype(BF16), p['ffn_w_up'].astype(BF16),
             p['ffn_w_down'].astype(BF16), tm=512, tf=1408, seg_len=seg_len)
    new_ckv = ckvn[t_lat:].reshape(n_ctx, ctx_seq, -1)
    new_kr = kr[t_lat:, :MLA_ROPE].reshape(n_ctx, ctx_seq, MLA_ROPE)
    return x, new_ckv, new_kr


def _odd_layer(x, cond8, p, cache_k, cache_v, lambda_init, *, nb, lat_seq, n_ctx, ctx_seq):
    t, d = x.shape
    t_lat = nb * lat_seq
    seg_len = lat_seq
    mod = _adaln(cond8, p['w_mod'], p['b_mod'])
    tm = 512
    tables = _rope_tables(lat_seq, DIFF_DH, (0, DIFF_DH), tm)
    q, k, v, new_k, new_v = _qkv(x, mod, p, tables, tm=tm, seg_len=seg_len, t_lat=t_lat, lat_seq=lat_seq,
                                 n_ctx=n_ctx, ctx_seq=ctx_seq)
    lam_p = jnp.stack([p['lam_q1'], p['lam_k1'], p['lam_q2'], p['lam_k2']])
    o = jnp.zeros((t, DIFF_HEADS * 2 * DIFF_DH), BF16)
    o = _diff_attn(o, q, k, v, cache_k.astype(BF16), cache_v.astype(BF16), lam_p, p['subln'],
                   tq=ATT_TQ, heads=1, n_seq=nb, seq_q=lat_seq, q_row0=0, lambda_init=lambda_init)
    o = _diff_attn(o, q, k, v, None, None, lam_p, p['subln'],
                   tq=ctx_seq, heads=DIFF_HEADS, n_seq=n_ctx, seq_q=ctx_seq, q_row0=t_lat,
                   lambda_init=lambda_init)
    x = _proj_res(x, mod, [o], [p['w_out'].astype(BF16)], tm=512, seg_len=seg_len, gate_row=2)
    x_lat, x_ctx = _moe(x, mod, p['norm2'], p['w_router'], p['moe_w_gate'], p['moe_w_up'], p['moe_w_down'],
                        seg_len=seg_len, t_split=t_lat)
    return x_lat, x_ctx, new_k, new_v


def kernel(x_prompt, x_sample, cache_l0_ckv, cache_l0_krope, cache_l1_k, cache_l1_v, c, c_ctx,
           l0_w_mod, l0_b_mod, l0_norm1, l0_norm2, l0_w_in, l0_hy_conv_w, l0_hy_conv_b,
           l0_hy_fw1, l0_hy_fb1, l0_hy_freq1, l0_hy_fw2, l0_hy_fb2, l0_hy_freq2, l0_hy_fw3, l0_hy_dbias,
           l0_mla_qa_norm, l0_mla_w_uq, l0_mla_kva_norm, l0_mla_w_ukv, l0_mla_q_norm, l0_mla_k_norm,
           l0_w_out, l0_ffn_w_gate, l0_ffn_w_up, l0_ffn_w_down,
           l1_w_mod, l1_b_mod, l1_norm1, l1_norm2, l1_w_qkv, l1_q_norm, l1_k_norm,
           l1_lam_q1, l1_lam_k1, l1_lam_q2, l1_lam_k2, l1_subln, l1_w_out,
           l1_w_router, l1_moe_w_gate, l1_moe_w_up, l1_moe_w_down):
    even = {
        'w_mod': l0_w_mod, 'b_mod': l0_b_mod, 'norm1': l0_norm1, 'norm2': l0_norm2, 'w_in': l0_w_in,
        'hy_conv_w': l0_hy_conv_w, 'hy_conv_b': l0_hy_conv_b, 'hy_fw1': l0_hy_fw1, 'hy_fb1': l0_hy_fb1,
        'hy_freq1': l0_hy_freq1, 'hy_fw2': l0_hy_fw2, 'hy_fb2': l0_hy_fb2, 'hy_freq2': l0_hy_freq2,
        'hy_fw3': l0_hy_fw3, 'hy_dbias': l0_hy_dbias, 'qa_norm': l0_mla_qa_norm, 'w_uq': l0_mla_w_uq,
        'kva_norm': l0_mla_kva_norm, 'w_ukv': l0_mla_w_ukv, 'q_norm': l0_mla_q_norm, 'k_norm': l0_mla_k_norm,
        'w_out': l0_w_out, 'ffn_w_gate': l0_ffn_w_gate, 'ffn_w_up': l0_ffn_w_up, 'ffn_w_down': l0_ffn_w_down,
    }
    odd = {
        'w_mod': l1_w_mod, 'b_mod': l1_b_mod, 'norm1': l1_norm1, 'norm2': l1_norm2, 'w_qkv': l1_w_qkv,
        'q_norm': l1_q_norm, 'k_norm': l1_k_norm, 'lam_q1': l1_lam_q1, 'lam_k1': l1_lam_k1,
        'lam_q2': l1_lam_q2, 'lam_k2': l1_lam_k2, 'subln': l1_subln, 'w_out': l1_w_out,
        'w_router': l1_w_router, 'moe_w_gate': l1_moe_w_gate, 'moe_w_up': l1_moe_w_up,
        'moe_w_down': l1_moe_w_down,
    }
    n_ctx, ctx_seq, d = x_prompt.shape
    nb, lat_seq, _ = x_sample.shape
    assert n_ctx * ctx_seq == lat_seq, "segment layout needs equally sized modulation segments"
    dims = dict(nb=nb, lat_seq=lat_seq, n_ctx=n_ctx, ctx_seq=ctx_seq)
    t_lat = nb * lat_seq
    cond8 = jnp.zeros((SUBLANES, d), F32).at[:nb].set(c).at[nb].set(c_ctx)

    x, new_l0_ckv, new_l0_krope = _even_layer(x_sample.reshape(t_lat, d), x_prompt.reshape(n_ctx * ctx_seq, d),
                                              cond8, even, cache_l0_ckv, cache_l0_krope, **dims)
    lambda_init = 0.8 - 0.6 * math.exp(-0.3 * 1)
    x_lat, x_ctx, new_l1_k, new_l1_v = _odd_layer(x, cond8, odd, cache_l1_k, cache_l1_v, lambda_init, **dims)

    y_sample = x_lat.reshape(nb, lat_seq, d)
    y_prompt = x_ctx.reshape(n_ctx, ctx_seq, d)
    return (y_prompt, y_sample, new_l0_ckv, new_l0_krope, new_l1_k, new_l1_v)
```

```python
import functools
import math

import numpy as np
import jax
import jax.numpy as jnp
from jax import lax
from jax.experimental import pallas as pl
from jax.experimental.pallas import tpu as pltpu

F32 = jnp.float32
BF16 = jnp.bfloat16

VMEM_LIMIT_BYTES = 56 * 1024 * 1024
LANES = 128
SUBLANES = 8
LOG2_E = math.log2(math.e)

GRID_W = 64
ROPE_BASE = 10000.0
NORM_EPS = 1e-6
HY_WIDTH = 512
HY_BANDS = 16
HY_FAST_DECAY_PCT = 0.3
HY_SLOW_DECAY_PCT = 1.5
HY_DECAY_TARGET = 1e-2
MLA_HEADS = 8
MLA_NOPE = 64
MLA_ROPE = 32
MLA_QK = MLA_NOPE + MLA_ROPE
MLA_V = 64
MLA_Q_RANK = 768
MLA_KV_RANK = 256
DIFF_HEADS = 8
DIFF_DH = 64
N_EXPERTS = 8


def _params(*sem):
    return pltpu.CompilerParams(dimension_semantics=sem, vmem_limit_bytes=VMEM_LIMIT_BYTES)


def _dot(a, b):
    return jnp.dot(a, b, preferred_element_type=F32)


def _dot_nt(a, b):
    return lax.dot_general(a, b, (((1,), (1,)), ((), ())), preferred_element_type=F32)


def _split_bf16(a):
    hi = a.astype(BF16)
    lo = (a - hi.astype(F32)).astype(BF16)
    return hi, lo


def _dot_f32(a, b):
    ah, al = _split_bf16(a)
    bh, bl = _split_bf16(b)
    return _dot(ah, bh) + (_dot(al, bh) + _dot(ah, bl))


def _rms(x, g, n=None):
    n = x.shape[-1] if n is None else n
    ms = jnp.sum(x * x, axis=-1, keepdims=True) * (1.0 / n)
    return x * lax.rsqrt(ms + NORM_EPS) * g


def _norm_mod(x, g, shift, scale):
    return _rms(x, g) * (1.0 + scale) + shift


def _silu(x):
    return x / (1.0 + jnp.exp(-x))


def _adaln_body(c_ref, w_ref, b_ref, o_ref):
    o_ref[...] = _dot_f32(_silu(c_ref[...]), w_ref[...]) + b_ref[...]


def _adaln(cond8, w_mod, b_mod):
    d, n = w_mod.shape
    tn = n // 4
    out = pl.pallas_call(
        _adaln_body,
        grid=(n // tn,),
        in_specs=[pl.BlockSpec((SUBLANES, d), lambda j: (0, 0)),
                  pl.BlockSpec((d, tn), lambda j: (0, j)),
                  pl.BlockSpec((1, tn), lambda j: (0, j))],
        out_specs=pl.BlockSpec((SUBLANES, tn), lambda j: (0, j)),
        out_shape=jax.ShapeDtypeStruct((SUBLANES, n), F32),
        compiler_params=_params("arbitrary"),
        name="adaln",
    )(cond8, w_mod, b_mod.reshape(1, n))
    return out.reshape(SUBLANES, 6, d)


def _two_part_specs(a, b, tm):
    na = a.shape[0] // tm
    cols = a.shape[1]
    return (pl.BlockSpec((tm, cols), lambda i: (jnp.minimum(i, na - 1), 0)),
            pl.BlockSpec((tm, cols), lambda i: (jnp.maximum(i - na, 0), 0)), na)


def _hy_pre_body(z_ref, zp_ref, zn_ref, w_ref, b_ref, u_ref, x0_ref, *, tm, lat_tiles, tiles_per_seq):
    i = pl.program_id(0)
    z = z_ref[...]
    in_lat = i < lat_tiles
    has_prev = jnp.logical_and(in_lat, i % tiles_per_seq != 0)
    has_next = jnp.logical_and(in_lat, i % tiles_per_seq != tiles_per_seq - 1)
    prev_row = jnp.where(has_prev, zp_ref[SUBLANES - 1:SUBLANES, :], 0.0)
    next_row = jnp.where(has_next, zn_ref[0:1, :], 0.0)
    rows = lax.broadcasted_iota(jnp.int32, z.shape, 0)
    z_m = jnp.where(rows == 0, prev_row, pltpu.roll(z, 1, 0))
    z_p = jnp.where(rows == tm - 1, next_row, pltpu.roll(z, tm - 1, 0))
    zc = b_ref[...] + z_m * w_ref[0:1, :] + z * w_ref[1:2, :] + z_p * w_ref[2:3, :]
    c = HY_WIDTH
    x0_ref[...] = zc[:, :c]
    u_ref[...] = zc[:, 2 * c:] * zc[:, c:2 * c]


def _hy_pre(z, conv_w, conv_b, *, t_lat, lat_seq, ctx_seq):
    t = z.shape[0]
    tm = ctx_seq
    c3 = 3 * HY_WIDTH
    nb8 = t // SUBLANES
    body = functools.partial(_hy_pre_body, tm=tm, lat_tiles=t_lat // tm, tiles_per_seq=lat_seq // tm)
    return pl.pallas_call(
        body,
        grid=(t // tm,),
        in_specs=[pl.BlockSpec((tm, c3), lambda i: (i, 0)),
                  pl.BlockSpec((SUBLANES, c3), lambda i: (jnp.maximum(i * (tm // SUBLANES) - 1, 0), 0)),
                  pl.BlockSpec((SUBLANES, c3), lambda i: (jnp.minimum((i + 1) * (tm // SUBLANES), nb8 - 1), 0)),
                  pl.BlockSpec((3, c3), lambda i: (0, 0)),
                  pl.BlockSpec((1, c3), lambda i: (0, 0))],
        out_specs=[pl.BlockSpec((tm, HY_WIDTH), lambda i: (i, 0)),
                   pl.BlockSpec((tm, HY_WIDTH), lambda i: (i, 0))],
        out_shape=[jax.ShapeDtypeStruct((t, HY_WIDTH), F32),
                   jax.ShapeDtypeStruct((t, HY_WIDTH), F32)],
        compiler_params=_params("parallel"),
        name="hy_pre",
    )(z, z, z, conv_w, conv_b.reshape(1, c3))


def _filter_embedding(seq):
    t01 = np.linspace(0.0, 1.0, seq)[:, None]
    w = 2.0 * math.pi * np.arange(seq)[:, None] / seq
    f = np.linspace(1e-4, HY_BANDS - 1, HY_BANDS)[None, :]
    z = np.concatenate([t01, np.cos(f * w), -np.sin(f * w)], axis=-1)
    z_rev = np.concatenate([z[:1], z[:0:-1]], axis=0)
    zz = np.concatenate([z, z_rev], axis=0)
    out = np.zeros((2 * seq, LANES), np.float32)
    out[:, :zz.shape[1]] = zz
    return out


def _filter_body(zz_ref, dl_ref, w1_ref, b1_ref, f1_ref, w2_ref, b2_ref, f2_ref, w3_ref,
                 k_ref, s_ref, *, tm, lat_tiles, ctx_tiles):
    i = pl.program_id(0)
    zz = zz_ref[...]
    h = jnp.sin(f1_ref[...] * (_dot_f32(zz, w1_ref[...]) + b1_ref[...]))
    h = jnp.sin(f2_ref[...] * (_dot_f32(h, w2_ref[...]) + b2_ref[...]))
    h = _dot_f32(h, w3_ref[...])
    is_bwd = jnp.logical_or(jnp.logical_and(i >= lat_tiles // 2, i < lat_tiles),
                            i >= lat_tiles + ctx_tiles // 2)
    first_bwd = jnp.logical_or(i == lat_tiles // 2, i == lat_tiles + ctx_tiles // 2)
    window = jnp.exp(-zz[:, 0:1] * dl_ref[...])
    k = jnp.where(is_bwd, h[:, HY_WIDTH:], h[:, :HY_WIDTH]) * window
    rows = lax.broadcasted_iota(jnp.int32, k.shape, 0)
    k = jnp.where(jnp.logical_and(first_bwd, rows == 0), 0.0, k)
    k_ref[...] = k
    s = jnp.sum(jnp.abs(k), axis=0, keepdims=True)

    @pl.when(i == 0)
    def _():
        s_ref[...] = jnp.zeros_like(s_ref)

    @pl.when(i < lat_tiles)
    def _():
        s_ref[0:1, :] += s

    @pl.when(i >= lat_tiles)
    def _():
        s_ref[1:2, :] += s


def _hyena_filters(p, *, lat_seq, ctx_seq):
    tm = ctx_seq
    zz = jnp.asarray(np.concatenate([_filter_embedding(lat_seq), _filter_embedding(ctx_seq)], axis=0))
    rows = zz.shape[0]
    max_decay = math.log(HY_DECAY_TARGET) / HY_FAST_DECAY_PCT
    min_decay = math.log(HY_DECAY_TARGET) / HY_SLOW_DECAY_PCT
    deltas = jnp.asarray(np.abs(np.linspace(min_decay, max_decay, HY_WIDTH))[None, :].astype(np.float32))
    emb, hid = p['hy_fw1'].shape

    def pad2(a, r, c):
        return jnp.zeros((r, c), F32).at[:a.shape[0], :a.shape[1]].set(a)

    w1 = pad2(p['hy_fw1'], LANES, LANES)
    b1 = pad2(p['hy_fb1'][None, :], 1, LANES)
    f1 = pad2(p['hy_freq1'][None, :], 1, LANES)
    w2 = pad2(p['hy_fw2'], LANES, LANES)
    b2 = pad2(p['hy_fb2'][None, :], 1, LANES)
    f2 = pad2(p['hy_freq2'][None, :], 1, LANES)
    w3 = pad2(p['hy_fw3'], LANES, 2 * HY_WIDTH)
    body = functools.partial(_filter_body, tm=tm, lat_tiles=2 * lat_seq // tm, ctx_tiles=2 * ctx_seq // tm)
    full = lambda i: (0, 0)
    return pl.pallas_call(
        body,
        grid=(rows // tm,),
        in_specs=[pl.BlockSpec((tm, LANES), lambda i: (i, 0)),
                  pl.BlockSpec((1, HY_WIDTH), full),
                  pl.BlockSpec((LANES, LANES), full), pl.BlockSpec((1, LANES), full), pl.BlockSpec((1, LANES), full),
                  pl.BlockSpec((LANES, LANES), full), pl.BlockSpec((1, LANES), full), pl.BlockSpec((1, LANES), full),
                  pl.BlockSpec((LANES, 2 * HY_WIDTH), full)],
        out_specs=[pl.BlockSpec((tm, HY_WIDTH), lambda i: (i, 0)),
                   pl.BlockSpec((SUBLANES, HY_WIDTH), full)],
        out_shape=[jax.ShapeDtypeStruct((rows, HY_WIDTH), F32),
                   jax.ShapeDtypeStruct((SUBLANES, HY_WIDTH), F32)],
        compiler_params=_params("arbitrary"),
        name="hy_filter",
    )(zz, deltas, w1, b1, f1, w2, b2, f2, w3)


def _stack_complex(z):
    return np.block([[z.real, -z.imag], [z.imag, z.real]])


def _dft_consts_two_level(seq, n1, n2):
    n = 2 * seq
    assert n1 * n2 == n
    a1 = np.arange(n1)
    f1_full = np.exp(-2j * np.pi * np.outer(a1, a1) / n1)
    f1_u = np.concatenate([f1_full.real, f1_full.imag], axis=0)[:, :n1 // 2]
    f1_k = np.concatenate([f1_full.real, f1_full.imag], axis=0)
    a2 = np.arange(n2)
    f = a1[:, None, None] + n1 * a2[None, :, None]
    z = np.exp(-2j * np.pi * (f * a2[None, None, :]) / n)
    mf = np.stack([_stack_complex(z[i]) for i in range(n1)])
    mi = np.stack([_stack_complex(np.conj(z[i]).T) for i in range(n1)])
    g = np.exp(2j * np.pi * np.outer(a1[:n1 // 2], a1) / n1) / n
    gc, gs = g.real, -g.imag
    as32 = lambda a: jnp.asarray(a.astype(np.float32))
    return as32(f1_u), as32(f1_k), as32(mf), as32(mi), as32(gc), as32(gs)


def _dft_consts_one_level(seq):
    n = 2 * seq
    a = np.arange(n)
    z = np.exp(-2j * np.pi * np.outer(a, a) / n)
    mf = np.concatenate([z.real, z.imag], axis=0)
    zi = np.exp(2j * np.pi * np.outer(a[:seq], a) / n) / n
    mi = np.concatenate([zi.real, -zi.imag], axis=1)
    as32 = lambda a: jnp.asarray(a.astype(np.float32))
    return as32(mf), as32(mi)


def _lmat_body(f_ref, x_ref, sc_ref, or_ref, oi_ref):
    x = (x_ref[...] * (1.0 / sc_ref[...])).astype(BF16)
    o = _dot(f_ref[...], x)
    h = o.shape[0] // 2
    or_ref[...] = o[:h].astype(BF16)
    oi_ref[...] = o[h:].astype(BF16)


def _lmat(f, x, scale_row, *, tn):
    g, k, cols = x.shape
    m2 = f.shape[0]
    m = m2 // 2
    return pl.pallas_call(
        _lmat_body,
        grid=(g, cols // tn),
        in_specs=[pl.BlockSpec((m2, k), lambda b, j: (0, 0)),
                  pl.BlockSpec((None, k, tn), lambda b, j: (b, 0, j)),
                  pl.BlockSpec((1, tn), lambda b, j: (0, 0))],
        out_specs=[pl.BlockSpec((None, m, tn), lambda b, j: (b, 0, j)),
                   pl.BlockSpec((None, m, tn), lambda b, j: (b, 0, j))],
        out_shape=[jax.ShapeDtypeStruct((g, m, cols), BF16),
                   jax.ShapeDtypeStruct((g, m, cols), BF16)],
        compiler_params=_params("parallel", "parallel"),
        name="hy_dft1",
    )(f.astype(BF16), x, scale_row)


SPEC_GROUP = 4


def _spec_fwd_body(mf_ref, ar_ref, ai_ref, kr_ref, ki_ref):
    for g in range(SPEC_GROUP):
        a = jnp.concatenate([ar_ref[g], ai_ref[g]], axis=0)
        x = _dot(mf_ref[g], a)
        h = x.shape[0] // 2
        kr_ref[g] = x[:h]
        ki_ref[g] = x[h:]


def _spec_fwd(mf, ar, ai):
    n1, n2, c = ar.shape
    spec = pl.BlockSpec((SPEC_GROUP, n2, c), lambda i: (i, 0, 0))
    return pl.pallas_call(
        _spec_fwd_body,
        grid=(n1 // SPEC_GROUP,),
        in_specs=[pl.BlockSpec((SPEC_GROUP, 2 * n2, 2 * n2), lambda i: (i, 0, 0)), spec, spec],
        out_specs=[spec, spec],
        out_shape=[jax.ShapeDtypeStruct((n1, n2, c), F32)] * 2,
        compiler_params=_params("parallel"),
        name="hy_spec_filter",
    )(mf, ar, ai)


def _spec_mul_body(mf_ref, mi_ref, kr_ref, ki_ref, ar_ref, ai_ref, br_ref, bi_ref):
    for g in range(SPEC_GROUP):
        a = jnp.concatenate([ar_ref[g], ai_ref[g]], axis=0)
        x = _dot(mf_ref[g], a)
        h = x.shape[0] // 2
        xr, xi = x[:h], x[h:]
        kr, ki = kr_ref[g], ki_ref[g]
        y = jnp.concatenate([xr * kr - xi * ki, xr * ki + xi * kr], axis=0).astype(BF16)
        b = _dot(mi_ref[g], y)
        br_ref[g] = b[:h].astype(BF16)
        bi_ref[g] = b[h:].astype(BF16)


def _spec_mul(mf, mi, kr, ki, ar, ai):
    nb, n1, n2, c = ar.shape
    mspec = pl.BlockSpec((SPEC_GROUP, 2 * n2, 2 * n2), lambda i, b: (i, 0, 0))
    kspec = pl.BlockSpec((SPEC_GROUP, n2, c), lambda i, b: (i, 0, 0))
    aspec = pl.BlockSpec((None, SPEC_GROUP, n2, c), lambda i, b: (b, i, 0, 0))
    return pl.pallas_call(
        _spec_mul_body,
        grid=(n1 // SPEC_GROUP, nb),
        in_specs=[mspec, mspec, kspec, kspec, aspec, aspec],
        out_specs=[aspec, aspec],
        out_shape=[jax.ShapeDtypeStruct((nb, n1, n2, c), BF16)] * 2,
        compiler_params=_params("parallel", "arbitrary"),
        name="hy_spec_mul",
    )(mf, mi, kr, ki, ar, ai)


def _idft1_body(gc_ref, gs_ref, br_ref, bi_ref, o_ref):
    o_ref[...] = _dot(gc_ref[...], br_ref[...]) + _dot(gs_ref[...], bi_ref[...])


def _idft1(gc, gs, br, bi, *, tn):
    nb, n1, cols = br.shape
    m = gc.shape[0]
    gspec = pl.BlockSpec((m, n1), lambda b, j: (0, 0))
    bspec = pl.BlockSpec((None, n1, tn), lambda b, j: (b, 0, j))
    return pl.pallas_call(
        _idft1_body,
        grid=(nb, cols // tn),
        in_specs=[gspec, gspec, bspec, bspec],
        out_specs=pl.BlockSpec((None, m, tn), lambda b, j: (b, 0, j)),
        out_shape=jax.ShapeDtypeStruct((nb, m, cols), F32),
        compiler_params=_params("parallel", "parallel"),
        name="hy_idft1",
    )(gc.astype(BF16), gs.astype(BF16), br, bi)


def _ctx_filter_body(mf_ref, k_ref, sc_ref, kf_ref):
    kf_ref[...] = _dot(mf_ref[...], (k_ref[...] * (1.0 / sc_ref[...])).astype(BF16))


def _ctx_conv_body(mf_ref, mi_ref, kf_ref, u_ref, o_ref):
    x = _dot(mf_ref[...], u_ref[...].astype(BF16))
    h = x.shape[0] // 2
    xr, xi = x[:h], x[h:]
    kr, ki = kf_ref[:h, :], kf_ref[h:, :]
    y = jnp.concatenate([xr * kr - xi * ki, xr * ki + xi * kr], axis=0).astype(BF16)
    o_ref[...] = _dot(mi_ref[...], y)


def _hyena_ctx(k_raw, k_norm1, u, *, seq, nseq, u_row0):
    mf, mi = _dft_consts_one_level(seq)
    n = 2 * seq
    c = u.shape[1]
    kf = pl.pallas_call(
        _ctx_filter_body,
        out_shape=jax.ShapeDtypeStruct((2 * n, c), F32),
        compiler_params=_params(),
        name="hy_ctx_filter",
    )(mf.astype(BF16), k_raw, k_norm1)
    full = lambda s: (0, 0)
    return pl.pallas_call(
        _ctx_conv_body,
        grid=(nseq,),
        in_specs=[pl.BlockSpec((2 * n, seq), full),
                  pl.BlockSpec((seq, 2 * n), full),
                  pl.BlockSpec((2 * n, c), full),
                  pl.BlockSpec((seq, c), lambda s: (u_row0 // seq + s, 0))],
        out_specs=pl.BlockSpec((seq, c), lambda s: (s, 0)),
        out_shape=jax.ShapeDtypeStruct((nseq * seq, c), F32),
        compiler_params=_params("parallel"),
        name="hy_ctx_conv",
    )(mf[:, :seq].astype(BF16), mi.astype(BF16), kf, u)


def _hyena_lat(k_raw, k_norm1, u, *, seq, nb):
    c = u.shape[1]
    n1, n2 = 64, 2 * seq // 64
    f1_u, f1_k, mf, mi, gc, gs = _dft_consts_two_level(seq, n1, n2)
    mf = mf.astype(BF16)
    mi = mi.astype(BF16)
    cols = n2 * c
    tn = 4096
    rep = tn // c
    ones_row = jnp.ones((1, tn), F32)
    akr, aki = _lmat(f1_k, k_raw.reshape(1, n1, cols), jnp.tile(k_norm1, (1, rep)), tn=tn)
    kr, ki = _spec_fwd(mf, akr.reshape(n1, n2, c), aki.reshape(n1, n2, c))
    ar, ai = _lmat(f1_u, u.reshape(nb, n1 // 2, cols), ones_row, tn=tn)
    br, bi = _spec_mul(mf, mi, kr, ki, ar.reshape(nb, n1, n2, c), ai.reshape(nb, n1, n2, c))
    y = _idft1(gc, gs, br.reshape(nb, n1, cols), bi.reshape(nb, n1, cols), tn=tn)
    return y.reshape(nb * seq, c)


def _rope_tables(seq, rope_dims, lane_offsets, pad_rows):
    rows = seq // GRID_W
    rr, cc = np.meshgrid(np.arange(rows), np.arange(GRID_W), indexing='ij')
    pos = (rr.reshape(-1).astype(np.float64), cc.reshape(-1).astype(np.float64))
    half = rope_dims // 2
    q = half // 2
    inv_freq = ROPE_BASE ** (-np.arange(0, half, 2, dtype=np.float64) / half)
    cos_t = np.ones((seq + pad_rows, LANES), np.float64)
    sin_a = np.zeros((seq + pad_rows, LANES), np.float64)
    sin_b = np.zeros((seq + pad_rows, LANES), np.float64)
    for off in lane_offsets:
        for axis in range(2):
            ang = pos[axis][:, None] * inv_freq[None, :]
            base = off + axis * half
            cos_t[:seq, base:base + q] = np.cos(ang)
            cos_t[:seq, base + q:base + half] = np.cos(ang)
            sin_b[:seq, base:base + q] = -np.sin(ang)
            sin_a[:seq, base + q:base + half] = np.sin(ang)
    as32 = lambda a: jnp.asarray(a.astype(np.float32))
    return as32(cos_t), as32(sin_a), as32(sin_b)


def _rope(x, cos_t, sin_a, sin_b, shift):
    return x * cos_t + pltpu.roll(x, shift, 1) * sin_a + pltpu.roll(x, LANES - shift, 1) * sin_b


def _even_in_body(xa_ref, xb_ref, mod_ref, g_ref, w_ref, qa_ref, kva_ref, wuq_ref, qn_ref,
                  cos_ref, sa_ref, sb_ref, zhy_ref, kr_ref, q_ref, ckvn_ref, zq_ref, *, na, scale):
    i = pl.program_id(0)
    n = pl.num_programs(0) - 1

    @pl.when(i == 0)
    def _():
        zq_ref[1] = jnp.zeros(zq_ref.shape[1:], F32)

    c_q = MLA_Q_RANK
    zp = zq_ref[(i + 1) % 2]
    ckvn_ref[...] = _rms(zp[:, c_q:], kva_ref[...])
    cqn = _rms(zp[:, :c_q], qa_ref[...])
    q = _dot(cqn.astype(BF16), wuq_ref[...])
    cos_t, sin_a, sin_b = cos_ref[...], sa_ref[...], sb_ref[...]
    g = qn_ref[...]
    for hh in range(MLA_HEADS):
        sl = slice(hh * LANES, (hh + 1) * LANES)
        qh = _rope(_rms(q[:, sl], g, MLA_QK), cos_t, sin_a, sin_b, MLA_ROPE // 4)
        q_ref[:, sl] = (qh * scale).astype(BF16)

    x = jnp.where(jnp.minimum(i, n - 1) < na, xa_ref[...], xb_ref[...])
    h = _norm_mod(x, g_ref[...], mod_ref[0:1, :], mod_ref[1:2, :])
    z = _dot(h.astype(BF16), w_ref[...])
    c_hy = 3 * HY_WIDTH
    c_kv = c_hy + MLA_Q_RANK + MLA_KV_RANK
    zhy_ref[...] = z[:, :c_hy]
    kr_ref[...] = z[:, c_kv:]
    zq_ref[i % 2] = z[:, c_hy:c_kv]


def _pad_heads(w, heads, width):
    k = w.shape[0]
    w3 = w.reshape(k, heads, width)
    return jnp.zeros((k, heads, LANES), w.dtype).at[:, :, :width].set(w3).reshape(k, heads * LANES)


def _even_in(xa, xb, mod, p, tables, *, tm, seg_len, t_lat, lat_seq):
    d = xa.shape[1]
    t = xa.shape[0] + xb.shape[0]
    n_in = p['w_in'].shape[1]
    c_kv = 3 * HY_WIDTH + MLA_Q_RANK + MLA_KV_RANK
    assert c_kv % LANES == 0 and n_in - c_kv == MLA_ROPE
    n_pad = c_kv + LANES
    w_in = jnp.zeros((d, n_pad), BF16).at[:, :n_in].set(p['w_in'].astype(BF16))
    wuq = _pad_heads(p['w_uq'], MLA_HEADS, MLA_QK).astype(BF16)
    cos_t, sin_a, sin_b = tables
    qn = jnp.zeros((1, LANES), F32).at[0, :MLA_QK].set(p['q_norm'])
    pos_blocks = lat_seq // tm
    lat_tiles = t_lat // tm
    n = t // tm
    na = xa.shape[0] // tm
    cur = lambda i: jnp.minimum(i, n - 1)
    prev = lambda i: jnp.maximum(i - 1, 0)

    def tmap(i):
        j = prev(i)
        return (jnp.where(j < lat_tiles, j % pos_blocks, pos_blocks), 0)

    tspec = pl.BlockSpec((tm, LANES), tmap)
    full = lambda i: (0, 0)
    row_cur = lambda cols: pl.BlockSpec((tm, cols), lambda i: (cur(i), 0))
    row_prev = lambda cols: pl.BlockSpec((tm, cols), lambda i: (prev(i), 0))
    return pl.pallas_call(
        functools.partial(_even_in_body, na=na, scale=MLA_QK ** -0.5 * LOG2_E),
        grid=(n + 1,),
        in_specs=[pl.BlockSpec((tm, d), lambda i: (jnp.minimum(cur(i), na - 1), 0)),
                  pl.BlockSpec((tm, d), lambda i: (jnp.maximum(cur(i) - na, 0), 0)),
                  pl.BlockSpec((None, 6, d), lambda i: (cur(i) * tm // seg_len, 0, 0)),
                  pl.BlockSpec((1, d), full),
                  pl.BlockSpec((d, n_pad), full),
                  pl.BlockSpec((1, MLA_Q_RANK), full),
                  pl.BlockSpec((1, MLA_KV_RANK), full),
                  pl.BlockSpec((MLA_Q_RANK, MLA_HEADS * LANES), full),
                  pl.BlockSpec((1, LANES), full),
                  tspec, tspec, tspec],
        out_specs=[row_cur(3 * HY_WIDTH), row_cur(LANES), row_prev(MLA_HEADS * LANES), row_prev(MLA_KV_RANK)],
        out_shape=[jax.ShapeDtypeStruct((t, 3 * HY_WIDTH), F32),
                   jax.ShapeDtypeStruct((t, LANES), F32),
                   jax.ShapeDtypeStruct((t, MLA_HEADS * LANES), BF16),
                   jax.ShapeDtypeStruct((t, MLA_KV_RANK), F32)],
        scratch_shapes=[pltpu.VMEM((2, tm, MLA_Q_RANK + MLA_KV_RANK), F32)],
        compiler_params=_params("arbitrary"),
        name="even_in",
    )(xa, xb, mod, p['norm1'].reshape(1, d), w_in, p['qa_norm'].reshape(1, -1), p['kva_norm'].reshape(1, -1),
      wuq, qn, cos_t, sin_a, sin_b)


def _mla_kv_body(ckvn_ref, kr_ref, wk_ref, wv_ref, kn_ref, cos_ref, sa_ref, sb_ref, k_ref, v_ref):
    c = ckvn_ref[...].astype(BF16)
    k = _dot(c, wk_ref[...])
    v_ref[...] = _dot(c, wv_ref[...]).astype(BF16)
    kr = pltpu.roll(kr_ref[...], MLA_NOPE, 1)
    cos_t, sin_a, sin_b = cos_ref[...], sa_ref[...], sb_ref[...]
    g = kn_ref[...]
    for h in range(MLA_HEADS):
        sl = slice(h * LANES, (h + 1) * LANES)
        kh = _rope(_rms(k[:, sl] + kr, g, MLA_QK), cos_t, sin_a, sin_b, MLA_ROPE // 4)
        k_ref[:, sl] = kh.astype(BF16)


def _mla_kv(ckvn_rows, kr_rows, p, tables, *, tm, nb, past, lat_seq):
    r = ckvn_rows.shape[0]
    w = p['w_ukv'].reshape(MLA_KV_RANK, MLA_HEADS, MLA_NOPE + MLA_V)
    wk = _pad_heads(w[:, :, :MLA_NOPE].reshape(MLA_KV_RANK, -1), MLA_HEADS, MLA_NOPE).astype(BF16)
    wv = w[:, :, MLA_NOPE:].reshape(MLA_KV_RANK, MLA_HEADS * MLA_V).astype(BF16)
    cos_t, sin_a, sin_b = tables
    kn = jnp.zeros((1, LANES), F32).at[0, :MLA_QK].set(p['k_norm'])
    per_b = (past + lat_seq) // tm
    past_tiles = past // tm
    pos_blocks = lat_seq // tm
    lat_tiles = nb * per_b

    def tmap(i):
        j = i % per_b
        is_pos = jnp.logical_and(i < lat_tiles, j >= past_tiles)
        return (jnp.where(is_pos, j - past_tiles, pos_blocks), 0)

    tspec = pl.BlockSpec((tm, LANES), tmap)
    full = lambda i: (0, 0)
    return pl.pallas_call(
        _mla_kv_body,
        grid=(r // tm,),
        in_specs=[pl.BlockSpec((tm, MLA_KV_RANK), lambda i: (i, 0)),
                  pl.BlockSpec((tm, LANES), lambda i: (i, 0)),
                  pl.BlockSpec((MLA_KV_RANK, MLA_HEADS * LANES), full),
                  pl.BlockSpec((MLA_KV_RANK, MLA_HEADS * MLA_V), full),
                  pl.BlockSpec((1, LANES), full),
                  tspec, tspec, tspec],
        out_specs=[pl.BlockSpec((tm, MLA_HEADS * LANES), lambda i: (i, 0)),
                   pl.BlockSpec((tm, MLA_HEADS * MLA_V), lambda i: (i, 0))],
        out_shape=[jax.ShapeDtypeStruct((r, MLA_HEADS * LANES), BF16),
                   jax.ShapeDtypeStruct((r, MLA_HEADS * MLA_V), BF16)],
        compiler_params=_params("parallel"),
        name="mla_kv",
    )(ckvn_rows, kr_rows, wk, wv, kn, cos_t, sin_a, sin_b)


ATT_CHUNK = 512
ATT_UNIT_ROWS = 256
ATT_TQ = 1024


def _fill_vaug(vaug_ref, g, v_blocks):
    off = 0
    for v in v_blocks:
        n = v.shape[0]
        vaug_ref[g, off:off + n, :LANES] = v
        off += n
    vaug_ref[g, :, LANES:] = jnp.ones((vaug_ref.shape[1], LANES), BF16)


def _softmax_pv(units, s_ref, n_keys):
    chunk = min(ATT_CHUNK, n_keys)
    chunks = [slice(c * chunk, (c + 1) * chunk) for c in range(n_keys // chunk)]

    def scores(u, rows, m_lane):
        s = _dot_nt(units[u][0], units[u][1](rows))
        s_ref[u % 2, :, rows] = s
        for j in range(chunk // LANES):
            blk = s[:, j * LANES:(j + 1) * LANES]
            m_lane = blk if m_lane is None else jnp.maximum(m_lane, blk)
        return m_lane

    def values(u, rows, m, acc):
        p = jnp.exp2(s_ref[u % 2, :, rows] - m).astype(BF16)
        d = _dot(p, units[u][2](rows))
        return d if acc is None else acc + d

    outs = []
    m_lane = None
    for rows in chunks:
        m_lane = scores(0, rows, m_lane)
    for u in range(len(units)):
        m = jnp.max(m_lane, axis=-1, keepdims=True)
        acc, m_lane = None, None
        for rows in chunks:
            acc = values(u, rows, m, acc)
            if u + 1 < len(units):
                m_lane = scores(u + 1, rows, m_lane)
        outs.append(acc)
    return outs


def _mla_attn_body(prev_ref, q_ref, k_ref, v_ref, o_ref, vaug_ref, s_ref):
    pairs = vaug_ref.shape[0]

    @pl.when(pl.program_id(2) == 0)
    def _():
        for g in range(pairs):
            _fill_vaug(vaug_ref, g, [v_ref[:, g * LANES:(g + 1) * LANES]])

    n_keys = k_ref.shape[0]
    tq = q_ref.shape[0]
    ur = s_ref.shape[1]
    units, slots = [], []
    for r0 in range(0, tq, ur):
        for g in range(pairs):
            slots.append((r0, g))
            for hh in range(2):
                sl = slice((2 * g + hh) * LANES, (2 * g + hh + 1) * LANES)
                units.append((q_ref[r0:r0 + ur, sl], lambda rows, sl=sl: k_ref[rows, sl],
                              lambda rows, g=g: vaug_ref[g, rows, :]))
    res = [r[:, :LANES] / r[:, LANES:] for r in _softmax_pv(units, s_ref, n_keys)]
    lane = lax.broadcasted_iota(jnp.int32, res[0].shape, 1)
    for i, (r0, g) in enumerate(slots):
        o_ref[r0:r0 + ur, g * LANES:(g + 1) * LANES] = jnp.where(
            lane < MLA_V, res[2 * i], res[2 * i + 1]).astype(BF16)


def _mla_attn(prev, q, k, v, *, tq, pairs, n_seq, seq_q, seq_k, q_row0, k_row0):
    hp = MLA_HEADS // 2 // pairs
    nq = seq_q // tq
    qb0, kb0 = q_row0 // tq, k_row0 // seq_k
    return pl.pallas_call(
        _mla_attn_body,
        grid=(n_seq, hp, nq),
        in_specs=[pl.BlockSpec(memory_space=pl.ANY),
                  pl.BlockSpec((tq, pairs * 2 * LANES), lambda s, h, i: (qb0 + s * nq + i, h)),
                  pl.BlockSpec((seq_k, pairs * 2 * LANES), lambda s, h, i: (kb0 + s, h)),
                  pl.BlockSpec((seq_k, pairs * 2 * MLA_V), lambda s, h, i: (kb0 + s, h))],
        out_specs=pl.BlockSpec((tq, pairs * 2 * MLA_V), lambda s, h, i: (qb0 + s * nq + i, h)),
        out_shape=jax.ShapeDtypeStruct(prev.shape, prev.dtype),
        input_output_aliases={0: 0},
        scratch_shapes=[pltpu.VMEM((pairs, seq_k, 2 * LANES), BF16),
                        pltpu.VMEM((2, min(tq, ATT_UNIT_ROWS), seq_k), F32)],
        compiler_params=_params("arbitrary", "arbitrary", "arbitrary"),
        name="mla_attn",
    )(prev, q, k, v)


def _diff_attn_body(prev_ref, *refs, n_seg, lambda_init):
    q_ref, lam_ref, sub_ref = refs[0], refs[1], refs[2]
    k_refs = refs[3:3 + n_seg]
    v_refs = refs[3 + n_seg:3 + 2 * n_seg]
    o_ref, kcat_ref, vaug_ref, s_ref = refs[3 + 2 * n_seg:]
    heads = vaug_ref.shape[0]

    def head_block(ref, g):
        return ref[g] if len(ref.shape) == 3 else ref[:, g * LANES:(g + 1) * LANES]

    @pl.when(pl.program_id(2) == 0)
    def _():
        for g in range(heads):
            _fill_vaug(vaug_ref, g, [head_block(v, g) for v in v_refs])
            off = 0
            for k in k_refs:
                kcat_ref[g, off:off + k.shape[-2], :] = head_block(k, g)
                off += k.shape[-2]

    lp = lam_ref[...]
    lam = (jnp.exp(jnp.sum(lp[0:1] * lp[1:2], axis=-1, keepdims=True))
           - jnp.exp(jnp.sum(lp[2:3] * lp[3:4], axis=-1, keepdims=True)) + lambda_init)
    n_keys = kcat_ref.shape[1]
    tq = q_ref.shape[0]
    ur = s_ref.shape[1]
    units, slots = [], []
    for r0 in range(0, tq, ur):
        for g in range(heads):
            slots.append((r0, g))
            q = q_ref[r0:r0 + ur, g * LANES:(g + 1) * LANES].astype(F32)
            lane = lax.broadcasted_iota(jnp.int32, q.shape, 1)
            k_of = lambda rows, g=g: kcat_ref[g, rows, :]
            v_of = lambda rows, g=g: vaug_ref[g, rows, :]
            units.append((jnp.where(lane < DIFF_DH, q, 0.0).astype(BF16), k_of, v_of))
            units.append((jnp.where(lane < DIFF_DH, 0.0, q).astype(BF16), k_of, v_of))
    res = _softmax_pv(units, s_ref, n_keys)
    for i, (r0, g) in enumerate(slots):
        r1, r2 = res[2 * i], res[2 * i + 1]
        o = r1[:, :LANES] / r1[:, LANES:] - (lam / r2[:, LANES:]) * r2[:, :LANES]
        o_ref[r0:r0 + ur, g * LANES:(g + 1) * LANES] = (
            _rms(o, sub_ref[...]) * (1.0 - lambda_init)).astype(BF16)


def _diff_attn(prev, q, k_new, v_new, k_cache, v_cache, lam_p, subln, *, tq, heads, n_seq, seq_q, q_row0,
               lambda_init):
    nq = seq_q // tq
    qb0 = q_row0 // tq
    sb0 = q_row0 // seq_q
    d = 2 * DIFF_DH
    new_spec = pl.BlockSpec((seq_q, heads * d), lambda s, h, i: (sb0 + s, h))
    if k_cache is None:
        n_seg, k_args, v_args, k_specs, v_specs = 1, [k_new], [v_new], [new_spec], [new_spec]
        n_keys = seq_q
    else:
        past = k_cache.shape[2]
        c_spec = pl.BlockSpec((None, heads, past, d), lambda s, h, i: (s, h, 0, 0))
        n_seg, k_args, v_args = 2, [k_cache, k_new], [v_cache, v_new]
        k_specs, v_specs = [c_spec, new_spec], [c_spec, new_spec]
        n_keys = past + seq_q
    return pl.pallas_call(
        functools.partial(_diff_attn_body, n_seg=n_seg, lambda_init=lambda_init),
        grid=(n_seq, DIFF_HEADS // heads, nq),
        in_specs=[pl.BlockSpec(memory_space=pl.ANY),
                  pl.BlockSpec((tq, heads * d), lambda s, h, i: (qb0 + s * nq + i, h)),
                  pl.BlockSpec((4, DIFF_DH), lambda s, h, i: (0, 0)),
                  pl.BlockSpec((1, d), lambda s, h, i: (0, 0))] + k_specs + v_specs,
        out_specs=pl.BlockSpec((tq, heads * d), lambda s, h, i: (qb0 + s * nq + i, h)),
        out_shape=jax.ShapeDtypeStruct(prev.shape, prev.dtype),
        input_output_aliases={0: 0},
        scratch_shapes=[pltpu.VMEM((heads, n_keys, d), BF16), pltpu.VMEM((heads, n_keys, 2 * LANES), BF16),
                        pltpu.VMEM((2, min(tq, ATT_UNIT_ROWS), n_keys), F32)],
        compiler_params=_params("arbitrary", "arbitrary", "arbitrary"),
        name="diff_attn",
    )(prev, q, lam_p, subln.reshape(1, d), *k_args, *v_args)


def _proj_res_body(*refs, n_in, gate_row):
    x_ref, mod_ref, o_ref = refs[0], refs[1], refs[-1]
    acc = None
    for j in range(n_in):
        a_ref, w_ref = refs[2 + 2 * j], refs[3 + 2 * j]
        d = _dot(a_ref[...].astype(BF16), w_ref[...])
        acc = d if acc is None else acc + d
    o_ref[...] = x_ref[...] + mod_ref[gate_row:gate_row + 1, :] * acc


def _proj_res(x, mod, acts, ws, *, tm, seg_len, gate_row):
    t, d = x.shape
    in_specs = [pl.BlockSpec((tm, d), lambda i: (i, 0)),
                pl.BlockSpec((None, 6, d), lambda i: (i * tm // seg_len, 0, 0))]
    args = [x, mod]
    for a, w in zip(acts, ws):
        in_specs += [pl.BlockSpec((tm, a.shape[1]), lambda i: (i, 0)),
                     pl.BlockSpec(w.shape, lambda i: (0, 0))]
        args += [a, w]
    return pl.pallas_call(
        functools.partial(_proj_res_body, n_in=len(acts), gate_row=gate_row),
        grid=(t // tm,),
        in_specs=in_specs,
        out_specs=pl.BlockSpec((tm, d), lambda i: (i, 0)),
        out_shape=jax.ShapeDtypeStruct((t, d), F32),
        compiler_params=_params("parallel"),
        name="proj_res",
    )(*args)


def _even_out_body(xa_ref, xb_ref, mod_ref, ca_ref, cb_ref, u_ref, x0_ref, db_ref, o_ref, wa_ref, wb_ref,
                   out_ref, *, na):
    first = pl.program_id(0) < na
    x = jnp.where(first, xa_ref[...], xb_ref[...])
    conv = jnp.where(first, ca_ref[...], cb_ref[...])
    y_hy = (conv + u_ref[...] * db_ref[...]) * x0_ref[...]
    acc = _dot(y_hy.astype(BF16), wa_ref[...]) + _dot(o_ref[...], wb_ref[...])
    out_ref[...] = x + mod_ref[2:3, :] * acc


def _even_out(xa, xb, mod, conv_a, conv_b, u, x0, dbias, o, w_hy, w_att, *, tm, seg_len):
    d = xa.shape[1]
    t = xa.shape[0] + xb.shape[0]
    c = u.shape[1]
    xa_spec, xb_spec, na = _two_part_specs(xa, xb, tm)
    ca_spec, cb_spec, na_c = _two_part_specs(conv_a, conv_b, tm)
    assert na == na_c
    row = lambda cols: pl.BlockSpec((tm, cols), lambda i: (i, 0))
    full = lambda a: pl.BlockSpec(a.shape, lambda i: (0, 0))
    return pl.pallas_call(
        functools.partial(_even_out_body, na=na),
        grid=(t // tm,),
        in_specs=[xa_spec, xb_spec,
                  pl.BlockSpec((None, 6, d), lambda i: (i * tm // seg_len, 0, 0)),
                  ca_spec, cb_spec, row(c), row(c), full(dbias), row(o.shape[1]), full(w_hy), full(w_att)],
        out_specs=row(d),
        out_shape=jax.ShapeDtypeStruct((t, d), F32),
        compiler_params=_params("arbitrary"),
        name="even_out",
    )(xa, xb, mod, conv_a, conv_b, u, x0, dbias, o, w_hy, w_att)


def _ffn_body(x_ref, mod_ref, g_ref, wg_ref, wu_ref, wd_ref, o_ref, h_ref, acc_ref):
    f = pl.program_id(1)

    @pl.when(f == 0)
    def _():
        h_ref[...] = _norm_mod(x_ref[...], g_ref[...], mod_ref[3:4, :], mod_ref[4:5, :]).astype(BF16)
        acc_ref[...] = jnp.zeros_like(acc_ref)

    h = h_ref[...]
    a = _silu(_dot(h, wg_ref[...])) * _dot(h, wu_ref[...])
    acc_ref[...] += _dot(a.astype(BF16), wd_ref[...])

    @pl.when(f == pl.num_programs(1) - 1)
    def _():
        o_ref[...] = x_ref[...] + mod_ref[5:6, :] * acc_ref[...]


def _ffn(x, mod, g, wg, wu, wd, *, tm, tf, seg_len):
    t, d = x.shape
    ff = wg.shape[1]
    return pl.pallas_call(
        _ffn_body,
        grid=(t // tm, ff // tf),
        in_specs=[pl.BlockSpec((tm, d), lambda i, f: (i, 0)),
                  pl.BlockSpec((None, 6, d), lambda i, f: (i * tm // seg_len, 0, 0)),
                  pl.BlockSpec((1, d), lambda i, f: (0, 0)),
                  pl.BlockSpec((d, tf), lambda i, f: (0, f)),
                  pl.BlockSpec((d, tf), lambda i, f: (0, f)),
                  pl.BlockSpec((tf, d), lambda i, f: (f, 0))],
        out_specs=pl.BlockSpec((tm, d), lambda i, f: (i, 0)),
        out_shape=jax.ShapeDtypeStruct((t, d), F32),
        scratch_shapes=[pltpu.VMEM((tm, d), BF16), pltpu.VMEM((tm, d), F32)],
        compiler_params=_params("parallel", "arbitrary"),
        name="ffn",
    )(x, mod, g.reshape(1, d), wg, wu, wd)


def _group_ms(x, gmat):
    hi, lo = _split_bf16(x * x)
    return (_dot(hi, gmat) + _dot(lo, gmat)) * (1.0 / DIFF_DH)


def _qkv_body(x_ref, mod_ref, g_ref, w_ref, gm_ref, qn_ref, kn_ref, cos_ref, sa_ref, sb_ref,
              q_ref, k_ref, v_ref, kf_ref, vf_ref, *, scale, seq):
    h = _norm_mod(x_ref[...], g_ref[...], mod_ref[0:1, :], mod_ref[1:2, :]).astype(BF16)
    z = _dot(h, w_ref[...])
    hd = DIFF_HEADS * 2 * DIFF_DH
    tm = z.shape[0]
    cos_t, sin_a, sin_b = cos_ref[...], sa_ref[...], sb_ref[...]
    gm = gm_ref[...]
    shift = DIFF_DH // 4
    for hh in range(DIFF_HEADS):
        sl = slice(hh * LANES, (hh + 1) * LANES)
        qh = z[:, hh * LANES:(hh + 1) * LANES]
        qh = qh * lax.rsqrt(_group_ms(qh, gm) + NORM_EPS) * qn_ref[...]
        q_ref[:, sl] = (_rope(qh, cos_t, sin_a, sin_b, shift) * scale).astype(BF16)
        kh = z[:, hd + hh * LANES:hd + (hh + 1) * LANES]
        kh = kh * lax.rsqrt(_group_ms(kh, gm) + NORM_EPS) * kn_ref[...]
        k_ref[:, sl] = _rope(kh, cos_t, sin_a, sin_b, shift).astype(BF16)
        vh = z[:, 2 * hd + hh * LANES:2 * hd + (hh + 1) * LANES]
        for s in range(tm // seq):
            kf_ref[s, hh] = kh[s * seq:(s + 1) * seq, :]
            vf_ref[s, hh] = vh[s * seq:(s + 1) * seq, :]
    v_ref[...] = z[:, 2 * hd:].astype(BF16)


def _qkv(x, mod, p, tables, *, tm, seg_len, t_lat, lat_seq, n_ctx, ctx_seq):
    t, d = x.shape
    hd = DIFF_HEADS * 2 * DIFF_DH
    wqk = p['w_qkv'][:, :2 * hd].reshape(d, 2, 2, DIFF_HEADS, DIFF_DH)
    wqk = wqk.transpose(0, 1, 3, 2, 4).reshape(d, 2 * hd)
    w = jnp.concatenate([wqk, p['w_qkv'][:, 2 * hd:]], axis=1).astype(BF16)
    gi = np.arange(LANES) // DIFF_DH
    gmat = jnp.asarray((gi[:, None] == gi[None, :]).astype(np.float32)).astype(BF16)
    cos_t, sin_a, sin_b = tables
    qn = jnp.tile(p['q_norm'], 2).reshape(1, LANES)
    kn = jnp.tile(p['k_norm'], 2).reshape(1, LANES)
    pos_blocks = lat_seq // tm
    lat_tiles = t_lat // tm
    seq_per_tile = tm // ctx_seq
    tspec = pl.BlockSpec((tm, LANES), lambda i: (jnp.where(i < lat_tiles, i % pos_blocks, pos_blocks), 0))
    full = lambda i: (0, 0)
    row = pl.BlockSpec((tm, hd), lambda i: (i, 0))
    fspec = pl.BlockSpec((seq_per_tile, DIFF_HEADS, ctx_seq, LANES),
                         lambda i: (jnp.maximum(i - lat_tiles, 0), 0, 0, 0))
    fshape = jax.ShapeDtypeStruct((n_ctx, DIFF_HEADS, ctx_seq, LANES), F32)
    return pl.pallas_call(
        functools.partial(_qkv_body, scale=DIFF_DH ** -0.5 * LOG2_E, seq=ctx_seq),
        grid=(t // tm,),
        in_specs=[pl.BlockSpec((tm, d), lambda i: (i, 0)),
                  pl.BlockSpec((None, 6, d), lambda i: (i * tm // seg_len, 0, 0)),
                  pl.BlockSpec((1, d), full),
                  pl.BlockSpec((d, 3 * hd), full),
                  pl.BlockSpec((LANES, LANES), full),
                  pl.BlockSpec((1, LANES), full), pl.BlockSpec((1, LANES), full),
                  tspec, tspec, tspec],
        out_specs=[row, row, row, fspec, fspec],
        out_shape=[jax.ShapeDtypeStruct((t, hd), BF16)] * 3 + [fshape, fshape],
        compiler_params=_params("arbitrary"),
        name="qkv",
    )(x, mod, p['norm1'].reshape(1, d), w, gmat, qn, kn, cos_t, sin_a, sin_b)


def _route(logits):
    lane = lax.broadcasted_iota(jnp.int32, logits.shape, 1)
    neg = jnp.float32(-jnp.inf)
    lg = jnp.where(lane < N_EXPERTS, logits, neg)
    m1 = jnp.max(lg, axis=-1, keepdims=True)
    i1 = jnp.min(jnp.where(lg == m1, lane, LANES), axis=-1, keepdims=True)
    lg2 = jnp.where(lane == i1, neg, lg)
    m2 = jnp.max(lg2, axis=-1, keepdims=True)
    i2 = jnp.min(jnp.where(lg2 == m2, lane, LANES), axis=-1, keepdims=True)
    e = jnp.exp(m2 - m1)
    w1 = 1.0 / (1.0 + e)
    w2 = e / (1.0 + e)
    return jnp.where(lane == i1, w1, 0.0) + jnp.where(lane == i2, w2, 0.0)


MOE_BLOCK = 1024
MOE_SUB = 256
MOE_ROUTE_TM = 512
MOE_WINDOW = 5


def _moe_route_body(x_ref, mod_ref, g_ref, wr_ref, h_ref, gates_ref, rank_ref, rank_t_ref,
                    carry_row, carry_col, *, tm):
    i = pl.program_id(0)

    @pl.when(i == 0)
    def _():
        carry_row[...] = jnp.zeros_like(carry_row)
        carry_col[...] = jnp.zeros_like(carry_col)

    h = _norm_mod(x_ref[...], g_ref[...], mod_ref[3:4, :], mod_ref[4:5, :])
    h_ref[...] = h.astype(BF16)
    gates = _route(_dot_f32(h, wr_ref[...]))
    gates_ref[...] = gates
    sel = jnp.where(gates != 0.0, 1.0, 0.0)
    sel_t = sel.T
    r = lax.broadcasted_iota(jnp.int32, (tm, tm), 0)
    c = lax.broadcasted_iota(jnp.int32, (tm, tm), 1)
    lower = jnp.where(c < r, 1.0, 0.0).astype(BF16)
    upper = jnp.where(r < c, 1.0, 0.0).astype(BF16)
    before = _dot(lower, sel.astype(BF16)) + carry_row[...]
    before_t = _dot(sel_t.astype(BF16), upper) + carry_col[...]
    rank_ref[...] = jnp.where(sel > 0.0, before, -1.0)
    rank_t = jnp.where(sel_t > 0.0, before_t, -1.0)
    for s in range(tm // MOE_SUB):
        rank_t_ref[s] = rank_t[:SUBLANES, s * MOE_SUB:(s + 1) * MOE_SUB]
    carry_row[...] += jnp.sum(sel, axis=0, keepdims=True)
    carry_col[...] += jnp.sum(sel_t, axis=1, keepdims=True)


def _moe_route(x, mod, g, w_router, *, seg_len):
    t, d = x.shape
    tm = MOE_ROUTE_TM
    ne = w_router.shape[1]
    assert ne <= SUBLANES
    wr = jnp.zeros((d, LANES), F32).at[:, :ne].set(w_router)
    sub = tm // MOE_SUB
    return pl.pallas_call(
        functools.partial(_moe_route_body, tm=tm),
        grid=(t // tm,),
        in_specs=[pl.BlockSpec((tm, d), lambda i: (i, 0)),
                  pl.BlockSpec((None, 6, d), lambda i: (i * tm // seg_len, 0, 0)),
                  pl.BlockSpec((1, d), lambda i: (0, 0)),
                  pl.BlockSpec((d, LANES), lambda i: (0, 0))],
        out_specs=[pl.BlockSpec((tm, d), lambda i: (i, 0)),
                   pl.BlockSpec((tm, LANES), lambda i: (i, 0)),
                   pl.BlockSpec((tm, LANES), lambda i: (i, 0)),
                   pl.BlockSpec((sub, SUBLANES, MOE_SUB), lambda i: (i, 0, 0))],
        out_shape=[jax.ShapeDtypeStruct((t, d), BF16),
                   jax.ShapeDtypeStruct((t, LANES), F32),
                   jax.ShapeDtypeStruct((t, LANES), F32),
                   jax.ShapeDtypeStruct((t // MOE_SUB, SUBLANES, MOE_SUB), F32)],
        scratch_shapes=[pltpu.VMEM((1, LANES), F32), pltpu.VMEM((LANES, 1), F32)],
        compiler_params=_params("arbitrary"),
        name="moe_route",
    )(x, mod, g.reshape(1, d), wr)


def _moe_plan(rank, ne, *, n_blocks):
    t = rank.shape[0]
    n_tiles = t // MOE_SUB
    per_blk = MOE_BLOCK // MOE_SUB
    n_sub = n_blocks * per_blk
    sel = (rank[:, :ne] >= 0.0).astype(jnp.int32)
    tile_cnt = sel.reshape(n_tiles, MOE_SUB, ne).sum(axis=1)
    tile_end = jnp.cumsum(tile_cnt, axis=0)
    tile_start = tile_end - tile_cnt
    cnt = tile_end[-1]
    nblk = (cnt + MOE_BLOCK - 1) // MOE_BLOCK
    bend = jnp.cumsum(nblk)
    bstart = bend - nblk
    e_last = jnp.max(jnp.where(cnt > 0, jnp.arange(ne), 0))
    b = jnp.arange(n_blocks)
    blk_valid = b < bend[-1]
    blk_e = jnp.minimum(jnp.sum(bend[None, :] <= b[:, None], axis=1), e_last).astype(jnp.int32)
    blk_r0 = (b - bstart[blk_e]) * MOE_BLOCK
    blk_rows = jnp.where(blk_valid, jnp.clip(cnt[blk_e] - blk_r0, 0, MOE_BLOCK), 0).astype(jnp.int32)
    j = jnp.arange(n_sub)
    sub_e = blk_e[j // per_blk]
    sub_r0 = blk_r0[j // per_blk] + (j % per_blk) * MOE_SUB
    sub_valid = jnp.logical_and(blk_valid[j // per_blk], sub_r0 < cnt[sub_e])
    ends = tile_end[:, sub_e]
    r1 = jnp.minimum(sub_r0 + MOE_SUB, cnt[sub_e])
    c_lo = jnp.sum(ends <= sub_r0[None, :], axis=0)
    c_hi = jnp.sum(ends < r1[None, :], axis=0)
    c_lo = jnp.where(sub_valid, c_lo, 1).astype(jnp.int32)
    c_hi = jnp.where(sub_valid, jnp.minimum(c_hi, n_tiles - 1), 0).astype(jnp.int32)
    base = (bstart * MOE_BLOCK).astype(jnp.int32)
    j0 = jnp.minimum((base[None, :] + tile_start) // MOE_SUB, n_sub - 2).astype(jnp.int32)
    return dict(blk_e=blk_e, blk_valid=blk_valid.astype(jnp.int32), blk_rows=blk_rows,
                sub_e=sub_e.astype(jnp.int32), sub_r0=sub_r0.astype(jnp.int32), c_lo=c_lo, c_hi=c_hi,
                base=base, j0=j0.reshape(-1))


def _moe_dispatch_body(e_ref, r0_ref, lo_ref, hi_ref, h_ref, rank_t_ref, xs_ref, acc_ref):
    j = pl.program_id(0)
    e = e_ref[j]
    rows = (r0_ref[j] + lax.broadcasted_iota(jnp.int32, (MOE_SUB, 1), 0)).astype(F32)
    sub = lax.broadcasted_iota(jnp.int32, (SUBLANES, MOE_SUB), 0)
    n_tiles = rank_t_ref.shape[0]
    lo, hi = lo_ref[j], hi_ref[j]
    acc_ref[...] = jnp.zeros_like(acc_ref)

    def step(w, carry):
        first = lo + w * MOE_WINDOW
        c0 = jnp.minimum(first, n_tiles - MOE_WINDOW)
        pieces = []
        for i in range(MOE_WINDOW):
            c = c0 + i
            rk = jnp.sum(jnp.where(sub == e, rank_t_ref[c], 0.0), axis=0, keepdims=True)
            rk = jnp.where(c >= first, rk, -1.0)
            pieces.append(jnp.where(rk == rows, 1.0, 0.0).astype(BF16))
        onehot = jnp.concatenate(pieces, axis=1)
        off = pl.multiple_of(c0 * MOE_SUB, MOE_SUB)
        acc_ref[...] += _dot(onehot, h_ref[pl.ds(off, MOE_WINDOW * MOE_SUB), :])
        return carry

    lax.fori_loop(0, (hi - lo + MOE_WINDOW) // MOE_WINDOW, step, 0)
    xs_ref[...] = acc_ref[...].astype(BF16)


def _moe_dispatch(h, rank_t, plan, *, n_sub):
    t, d = h.shape
    grid_spec = pltpu.PrefetchScalarGridSpec(
        num_scalar_prefetch=4,
        grid=(n_sub,),
        in_specs=[pl.BlockSpec((t, d), lambda j, *_: (0, 0), pipeline_mode=pl.Buffered(1)),
                  pl.BlockSpec(rank_t.shape, lambda j, *_: (0, 0, 0), pipeline_mode=pl.Buffered(1))],
        out_specs=pl.BlockSpec((MOE_SUB, d), lambda j, *_: (j, 0)),
        scratch_shapes=[pltpu.VMEM((MOE_SUB, d), F32)],
    )
    return pl.pallas_call(
        _moe_dispatch_body,
        grid_spec=grid_spec,
        out_shape=jax.ShapeDtypeStruct((n_sub * MOE_SUB, d), BF16),
        compiler_params=_params("arbitrary"),
        name="moe_dispatch",
    )(plan['sub_e'], plan['sub_r0'], plan['c_lo'], plan['c_hi'], h, rank_t)


def _moe_ffn_body(e_ref, valid_ref, rows_ref, xs_ref, wg_ref, wu_ref, wd_ref, y_ref, acc_ref):
    b = pl.program_id(0)
    f = pl.program_id(1)
    n_rows = rows_ref[b]
    last = f == pl.num_programs(1) - 1
    wg = wg_ref[...].astype(BF16)
    wu = wu_ref[...].astype(BF16)
    wd = wd_ref[...].astype(BF16)
    full = n_rows == MOE_BLOCK

    def swiglu(h):
        a = _silu(_dot(h, wg)) * _dot(h, wu)
        return _dot(a.astype(BF16), wd)

    @pl.when(jnp.logical_and(full, f == 0))
    def _():
        acc_ref[...] = swiglu(xs_ref[...])

    @pl.when(jnp.logical_and(full, f > 0))
    def _():
        acc_ref[...] += swiglu(xs_ref[...])

    @pl.when(jnp.logical_and(full, last))
    def _():
        y_ref[...] = acc_ref[...].astype(BF16)

    for s in range(MOE_BLOCK // MOE_SUB):
        sl = slice(s * MOE_SUB, (s + 1) * MOE_SUB)
        live = jnp.logical_and(jnp.logical_not(full), s * MOE_SUB < n_rows)
        dead = jnp.logical_and(jnp.logical_not(full), s * MOE_SUB >= n_rows)

        @pl.when(jnp.logical_and(live, f == 0))
        def _():
            acc_ref[sl, :] = jnp.zeros((MOE_SUB, acc_ref.shape[1]), F32)

        @pl.when(live)
        def _():
            acc_ref[sl, :] += swiglu(xs_ref[sl, :])

        @pl.when(jnp.logical_and(live, last))
        def _():
            y_ref[sl, :] = acc_ref[sl, :].astype(BF16)

        @pl.when(jnp.logical_and(dead, last))
        def _():
            y_ref[sl, :] = jnp.zeros((MOE_SUB, y_ref.shape[1]), BF16)


def _moe_ffn(xs, wg, wu, wd, plan, *, n_blocks, tf):
    _, d = xs.shape
    ne, _, ff = wg.shape
    nf = ff // tf

    def w_in(b, f, e_ref, valid_ref, rows_ref):
        return (e_ref[b], 0, jnp.where(valid_ref[b] > 0, f, nf - 1))

    def w_down(b, f, e_ref, valid_ref, rows_ref):
        return (e_ref[b], jnp.where(valid_ref[b] > 0, f, nf - 1), 0)

    grid_spec = pltpu.PrefetchScalarGridSpec(
        num_scalar_prefetch=3,
        grid=(n_blocks, nf),
        in_specs=[pl.BlockSpec((MOE_BLOCK, d), lambda b, f, *_: (b, 0)),
                  pl.BlockSpec((None, d, tf), w_in),
                  pl.BlockSpec((None, d, tf), w_in),
                  pl.BlockSpec((None, tf, d), w_down)],
        out_specs=pl.BlockSpec((MOE_BLOCK, d), lambda b, f, *_: (b, 0)),
        scratch_shapes=[pltpu.VMEM((MOE_BLOCK, d), F32)],
    )
    return pl.pallas_call(
        _moe_ffn_body,
        grid_spec=grid_spec,
        out_shape=jax.ShapeDtypeStruct((n_blocks * MOE_BLOCK, d), BF16),
        compiler_params=_params("arbitrary", "arbitrary"),
        name="moe_ffn",
    )(plan['blk_e'], plan['blk_valid'], plan['blk_rows'], xs, wg, wu, wd)


def _moe_combine_body(j0_ref, base_ref, x_ref, mod_ref, gates_ref, rank_ref, *rest, ne, split_tiles):
    y_refs, o_ref, o2_ref = rest[:2 * ne], rest[2 * ne], rest[2 * ne + 1]
    c = pl.program_id(0)
    gates = gates_ref[...]
    rank = rank_ref[...]
    lane = lax.broadcasted_iota(jnp.int32, gates.shape, 1)
    col = lax.broadcasted_iota(jnp.int32, (1, MOE_SUB), 1).astype(F32)
    acc = None
    for e in range(ne):
        pick = lane == e
        g = jnp.sum(jnp.where(pick, gates, 0.0), axis=-1, keepdims=True)
        rk = jnp.sum(jnp.where(pick, rank, 0.0), axis=-1, keepdims=True)
        shift = (base_ref[e] - j0_ref[c * ne + e] * MOE_SUB).astype(F32)
        loc = jnp.where(rk >= 0.0, rk + shift, -1.0)
        qa = jnp.where(loc == col, 1.0, 0.0).astype(BF16)
        qb = jnp.where(loc == col + float(MOE_SUB), 1.0, 0.0).astype(BF16)
        contrib = g * (_dot(qa, y_refs[2 * e][...]) + _dot(qb, y_refs[2 * e + 1][...]))
        acc = contrib if acc is None else acc + contrib
    out = x_ref[...] + mod_ref[5:6, :] * acc

    @pl.when(c < split_tiles)
    def _():
        o_ref[...] = out

    @pl.when(c >= split_tiles)
    def _():
        o2_ref[...] = out


def _moe_combine(x, mod, gates, rank, y, plan, *, ne, seg_len, t_split):
    t, d = x.shape
    tm = MOE_SUB
    split_tiles = t_split // tm
    y_specs = []
    for e in range(ne):
        y_specs += [pl.BlockSpec((MOE_SUB, d), lambda c, j0, base, e=e: (j0[c * ne + e], 0)),
                    pl.BlockSpec((MOE_SUB, d), lambda c, j0, base, e=e: (j0[c * ne + e] + 1, 0))]
    grid_spec = pltpu.PrefetchScalarGridSpec(
        num_scalar_prefetch=2,
        grid=(t // tm,),
        in_specs=[pl.BlockSpec((tm, d), lambda c, *_: (c, 0)),
                  pl.BlockSpec((None, 6, d), lambda c, *_: (c * tm // seg_len, 0, 0)),
                  pl.BlockSpec((tm, LANES), lambda c, *_: (c, 0)),
                  pl.BlockSpec((tm, LANES), lambda c, *_: (c, 0))] + y_specs,
        out_specs=[pl.BlockSpec((tm, d), lambda c, *_: (jnp.minimum(c, split_tiles - 1), 0)),
                   pl.BlockSpec((tm, d), lambda c, *_: (jnp.maximum(c - split_tiles, 0), 0))],
    )
    return pl.pallas_call(
        functools.partial(_moe_combine_body, ne=ne, split_tiles=split_tiles),
        grid_spec=grid_spec,
        out_shape=[jax.ShapeDtypeStruct((t_split, d), F32), jax.ShapeDtypeStruct((t - t_split, d), F32)],
        compiler_params=_params("arbitrary"),
        name="moe_combine",
    )(plan['j0'], plan['base'], x, mod, gates, rank, *([y] * (2 * ne)))


def _moe(x, mod, g, w_router, wg, wu, wd, *, seg_len, t_split, top_k=2):
    t, d = x.shape
    ne = w_router.shape[1]
    n_blocks = t * top_k // MOE_BLOCK + ne
    h, gates, rank, rank_t = _moe_route(x, mod, g, w_router, seg_len=seg_len)
    plan = _moe_plan(rank, ne, n_blocks=n_blocks)
    xs = _moe_dispatch(h, rank_t, plan, n_sub=n_blocks * (MOE_BLOCK // MOE_SUB))
    y = _moe_ffn(xs, wg, wu, wd, plan, n_blocks=n_blocks, tf=512)
    return _moe_combine(x, mod, gates, rank, y, plan, ne=ne, seg_len=seg_len, t_split=t_split)


def _even_layer(x_lat, x_ctx, cond8, p, cache_ckv, cache_kr, *, nb, lat_seq, n_ctx, ctx_seq):
    d = x_lat.shape[1]
    t_lat = nb * lat_seq
    t = t_lat + x_ctx.shape[0]
    seg_len = lat_seq
    past = cache_ckv.shape[1]
    mod = _adaln(cond8, p['w_mod'], p['b_mod'])

    tm = 512
    tables = _rope_tables(lat_seq, MLA_ROPE, (MLA_NOPE,), tm)
    z_hy, kr, q, ckvn = _even_in(x_lat, x_ctx, mod, p, tables, tm=tm, seg_len=seg_len, t_lat=t_lat,
                                 lat_seq=lat_seq)

    u, x0 = _hy_pre(z_hy, p['hy_conv_w'], p['hy_conv_b'], t_lat=t_lat, lat_seq=lat_seq, ctx_seq=ctx_seq)
    k_raw, k_sum = _hyena_filters(p, lat_seq=lat_seq, ctx_seq=ctx_seq)
    dbias = p['hy_dbias'].reshape(1, HY_WIDTH)
    conv_lat = _hyena_lat(k_raw[:2 * lat_seq], k_sum[0:1], u[:t_lat], seq=lat_seq, nb=nb)
    conv_ctx = _hyena_ctx(k_raw[2 * lat_seq:], k_sum[1:2], u, seq=ctx_seq, nseq=n_ctx, u_row0=t_lat)

    cache_kr_p = jnp.zeros((nb, past, LANES), F32).at[:, :, :MLA_ROPE].set(cache_kr)
    ckvn_rows = jnp.concatenate(
        [jnp.concatenate([cache_ckv, ckvn[:t_lat].reshape(nb, lat_seq, -1)], axis=1).reshape(nb * (past + lat_seq), -1),
         ckvn[t_lat:]], axis=0)
    kr_rows = jnp.concatenate(
        [jnp.concatenate([cache_kr_p, kr[:t_lat].reshape(nb, lat_seq, LANES)], axis=1).reshape(nb * (past + lat_seq), LANES),
         kr[t_lat:]], axis=0)
    k_all, v_all = _mla_kv(ckvn_rows, kr_rows, p, tables, tm=tm, nb=nb, past=past, lat_seq=lat_seq)
    o = jnp.zeros((t, MLA_HEADS * MLA_V), BF16)
    o = _mla_attn(o, q, k_all, v_all, tq=ATT_TQ, pairs=1, n_seq=nb, seq_q=lat_seq, seq_k=past + lat_seq,
                  q_row0=0, k_row0=0)
    o = _mla_attn(o, q, k_all, v_all, tq=ctx_seq, pairs=MLA_HEADS // 2, n_seq=n_ctx, seq_q=ctx_seq,
                  seq_k=ctx_seq, q_row0=t_lat, k_row0=nb * (past + lat_seq))

    w_out = p['w_out'].astype(BF16)
    x = _even_out(x_lat, x_ctx, mod, conv_lat, conv_ctx, u, x0, dbias, o, w_out[:HY_WIDTH], w_out[HY_WIDTH:],
                  tm=512, seg_len=seg_len)
    x = _ffn(x, mod, p['norm2'], p['ffn_w_gate'].astype(BF16), p['ffn_w_up'].astype(BF16),
             p['ffn_w_down'].astype(BF16), tm=512, tf=1408, seg_len=seg_len)
    new_ckv = ckvn[t_lat:].reshape(n_ctx, ctx_seq, -1)
    new_kr = kr[t_lat:, :MLA_ROPE].reshape(n_ctx, ctx_seq, MLA_ROPE)
    return x, new_ckv, new_kr


def _odd_layer(x, cond8, p, cache_k, cache_v, lambda_init, *, nb, lat_seq, n_ctx, ctx_seq):
    t, d = x.shape
    t_lat = nb * lat_seq
    seg_len = lat_seq
    mod = _adaln(cond8, p['w_mod'], p['b_mod'])
    tm = 512
    tables = _rope_tables(lat_seq, DIFF_DH, (0, DIFF_DH), tm)
    q, k, v, new_k, new_v = _qkv(x, mod, p, tables, tm=tm, seg_len=seg_len, t_lat=t_lat, lat_seq=lat_seq,
                                 n_ctx=n_ctx, ctx_seq=ctx_seq)
    lam_p = jnp.stack([p['lam_q1'], p['lam_k1'], p['lam_q2'], p['lam_k2']])
    o = jnp.zeros((t, DIFF_HEADS * 2 * DIFF_DH), BF16)
    o = _diff_attn(o, q, k, v, cache_k.astype(BF16), cache_v.astype(BF16), lam_p, p['subln'],
                   tq=ATT_TQ, heads=1, n_seq=nb, seq_q=lat_seq, q_row0=0, lambda_init=lambda_init)
    o = _diff_attn(o, q, k, v, None, None, lam_p, p['subln'],
                   tq=ctx_seq, heads=DIFF_HEADS, n_seq=n_ctx, seq_q=ctx_seq, q_row0=t_lat,
                   lambda_init=lambda_init)
    x = _proj_res(x, mod, [o], [p['w_out'].astype(BF16)], tm=512, seg_len=seg_len, gate_row=2)
    x_lat, x_ctx = _moe(x, mod, p['norm2'], p['w_router'], p['moe_w_gate'], p['moe_w_up'], p['moe_w_down'],
                        seg_len=seg_len, t_split=t_lat)
    return x_lat, x_ctx, new_k, new_v


def kernel(x_prompt, x_sample, cache_l0_ckv, cache_l0_krope, cache_l1_k, cache_l1_v, c, c_ctx,
           l0_w_mod, l0_b_mod, l0_norm1, l0_norm2, l0_w_in, l0_hy_conv_w, l0_hy_conv_b,
           l0_hy_fw1, l0_hy_fb1, l0_hy_freq1, l0_hy_fw2, l0_hy_fb2, l0_hy_freq2, l0_hy_fw3, l0_hy_dbias,
           l0_mla_qa_norm, l0_mla_w_uq, l0_mla_kva_norm, l0_mla_w_ukv, l0_mla_q_norm, l0_mla_k_norm,
           l0_w_out, l0_ffn_w_gate, l0_ffn_w_up, l0_ffn_w_down,
           l1_w_mod, l1_b_mod, l1_norm1, l1_norm2, l1_w_qkv, l1_q_norm, l1_k_norm,
           l1_lam_q1, l1_lam_k1, l1_lam_q2, l1_lam_k2, l1_subln, l1_w_out,
           l1_w_router, l1_moe_w_gate, l1_moe_w_up, l1_moe_w_down):
    even = {
        'w_mod': l0_w_mod, 'b_mod': l0_b_mod, 'norm1': l0_norm1, 'norm2': l0_norm2, 'w_in': l0_w_in,
        'hy_conv_w': l0_hy_conv_w, 'hy_conv_b': l0_hy_conv_b, 'hy_fw1': l0_hy_fw1, 'hy_fb1': l0_hy_fb1,
        'hy_freq1': l0_hy_freq1, 'hy_fw2': l0_hy_fw2, 'hy_fb2': l0_hy_fb2, 'hy_freq2': l0_hy_freq2,
        'hy_fw3': l0_hy_fw3, 'hy_dbias': l0_hy_dbias, 'qa_norm': l0_mla_qa_norm, 'w_uq': l0_mla_w_uq,
        'kva_norm': l0_mla_kva_norm, 'w_ukv': l0_mla_w_ukv, 'q_norm': l0_mla_q_norm, 'k_norm': l0_mla_k_norm,
        'w_out': l0_w_out, 'ffn_w_gate': l0_ffn_w_gate, 'ffn_w_up': l0_ffn_w_up, 'ffn_w_down': l0_ffn_w_down,
    }
    odd = {
        'w_mod': l1_w_mod, 'b_mod': l1_b_mod, 'norm1': l1_norm1, 'norm2': l1_norm2, 'w_qkv': l1_w_qkv,
        'q_norm': l1_q_norm, 'k_norm': l1_k_norm, 'lam_q1': l1_lam_q1, 'lam_k1': l1_lam_k1,
        'lam_q2': l1_lam_q2, 'lam_k2': l1_lam_k2, 'subln': l1_subln, 'w_out': l1_w_out,
        'w_router': l1_w_router, 'moe_w_gate': l1_moe_w_gate, 'moe_w_up': l1_moe_w_up,
        'moe_w_down': l1_moe_w_down,
    }
    n_ctx, ctx_seq, d = x_prompt.shape
    nb, lat_seq, _ = x_sample.shape
    assert n_ctx * ctx_seq == lat_seq, "segment layout needs equally sized modulation segments"
    dims = dict(nb=nb, lat_seq=lat_seq, n_ctx=n_ctx, ctx_seq=ctx_seq)
    t_lat = nb * lat_seq
    cond8 = jnp.zeros((SUBLANES, d), F32).at[:nb].set(c).at[nb].set(c_ctx)

    x, new_l0_ckv, new_l0_krope = _even_layer(x_sample.reshape(t_lat, d), x_prompt.reshape(n_ctx * ctx_seq, d),
                                              cond8, even, cache_l0_ckv, cache_l0_krope, **dims)
    lambda_init = 0.8 - 0.6 * math.exp(-0.3 * 1)
    x_lat, x_ctx, new_l1_k, new_l1_v = _odd_layer(x, cond8, odd, cache_l1_k, cache_l1_v, lambda_init, **dims)

    y_sample = x_lat.reshape(nb, lat_seq, d)
    y_prompt = x_ctx.reshape(n_ctx, ctx_seq, d)
    return (y_prompt, y_sample, new_l0_ckv, new_l0_krope, new_l1_k, new_l1_v)
```

```python
import functools
import math

import numpy as np
import jax
import jax.numpy as jnp
from jax import lax
from jax.experimental import pallas as pl
from jax.experimental.pallas import tpu as pltpu

F32 = jnp.float32
BF16 = jnp.bfloat16

VMEM_LIMIT_BYTES = 56 * 1024 * 1024
LANES = 128
SUBLANES = 8
LOG2_E = math.log2(math.e)

GRID_W = 64
ROPE_BASE = 10000.0
NORM_EPS = 1e-6
HY_WIDTH = 512
HY_BANDS = 16
HY_FAST_DECAY_PCT = 0.3
HY_SLOW_DECAY_PCT = 1.5
HY_DECAY_TARGET = 1e-2
MLA_HEADS = 8
MLA_NOPE = 64
MLA_ROPE = 32
MLA_QK = MLA_NOPE + MLA_ROPE
MLA_V = 64
MLA_Q_RANK = 768
MLA_KV_RANK = 256
DIFF_HEADS = 8
DIFF_DH = 64
N_EXPERTS = 8


def _params(*sem):
    return pltpu.CompilerParams(dimension_semantics=sem, vmem_limit_bytes=VMEM_LIMIT_BYTES)


def _dot(a, b):
    return jnp.dot(a, b, preferred_element_type=F32)


def _dot_nt(a, b):
    return lax.dot_general(a, b, (((1,), (1,)), ((), ())), preferred_element_type=F32)


def _split_bf16(a):
    hi = a.astype(BF16)
    lo = (a - hi.astype(F32)).astype(BF16)
    return hi, lo


def _dot_f32(a, b):
    ah, al = _split_bf16(a)
    bh, bl = _split_bf16(b)
    return _dot(ah, bh) + (_dot(al, bh) + _dot(ah, bl))


def _rms(x, g, n=None):
    n = x.shape[-1] if n is None else n
    ms = jnp.sum(x * x, axis=-1, keepdims=True) * (1.0 / n)
    return x * lax.rsqrt(ms + NORM_EPS) * g


def _norm_mod(x, g, shift, scale):
    return _rms(x, g) * (1.0 + scale) + shift


def _silu(x):
    return x / (1.0 + jnp.exp(-x))


def _adaln_body(c_ref, w_ref, b_ref, o_ref):
    o_ref[...] = _dot_f32(_silu(c_ref[...]), w_ref[...]) + b_ref[...]


def _adaln(cond8, w_mod, b_mod):
    d, n = w_mod.shape
    tn = n // 4
    out = pl.pallas_call(
        _adaln_body,
        grid=(n // tn,),
        in_specs=[pl.BlockSpec((SUBLANES, d), lambda j: (0, 0)),
                  pl.BlockSpec((d, tn), lambda j: (0, j)),
                  pl.BlockSpec((1, tn), lambda j: (0, j))],
        out_specs=pl.BlockSpec((SUBLANES, tn), lambda j: (0, j)),
        out_shape=jax.ShapeDtypeStruct((SUBLANES, n), F32),
        compiler_params=_params("arbitrary"),
        name="adaln",
    )(cond8, w_mod, b_mod.reshape(1, n))
    return out.reshape(SUBLANES, 6, d)


def _two_part_specs(a, b, tm):
    na = a.shape[0] // tm
    cols = a.shape[1]
    return (pl.BlockSpec((tm, cols), lambda i: (jnp.minimum(i, na - 1), 0)),
            pl.BlockSpec((tm, cols), lambda i: (jnp.maximum(i - na, 0), 0)), na)


def _hy_pre_body(z_ref, zp_ref, zn_ref, w_ref, b_ref, u_ref, x0_ref, *, tm, lat_tiles, tiles_per_seq):
    i = pl.program_id(0)
    z = z_ref[...]
    in_lat = i < lat_tiles
    has_prev = jnp.logical_and(in_lat, i % tiles_per_seq != 0)
    has_next = jnp.logical_and(in_lat, i % tiles_per_seq != tiles_per_seq - 1)
    prev_row = jnp.where(has_prev, zp_ref[SUBLANES - 1:SUBLANES, :], 0.0)
    next_row = jnp.where(has_next, zn_ref[0:1, :], 0.0)
    rows = lax.broadcasted_iota(jnp.int32, z.shape, 0)
    z_m = jnp.where(rows == 0, prev_row, pltpu.roll(z, 1, 0))
    z_p = jnp.where(rows == tm - 1, next_row, pltpu.roll(z, tm - 1, 0))
    zc = b_ref[...] + z_m * w_ref[0:1, :] + z * w_ref[1:2, :] + z_p * w_ref[2:3, :]
    c = HY_WIDTH
    x0_ref[...] = zc[:, :c]
    u_ref[...] = zc[:, 2 * c:] * zc[:, c:2 * c]


def _hy_pre(z, conv_w, conv_b, *, t_lat, lat_seq, ctx_seq):
    t = z.shape[0]
    tm = ctx_seq
    c3 = 3 * HY_WIDTH
    nb8 = t // SUBLANES
    body = functools.partial(_hy_pre_body, tm=tm, lat_tiles=t_lat // tm, tiles_per_seq=lat_seq // tm)
    return pl.pallas_call(
        body,
        grid=(t // tm,),
        in_specs=[pl.BlockSpec((tm, c3), lambda i: (i, 0)),
                  pl.BlockSpec((SUBLANES, c3), lambda i: (jnp.maximum(i * (tm // SUBLANES) - 1, 0), 0)),
                  pl.BlockSpec((SUBLANES, c3), lambda i: (jnp.minimum((i + 1) * (tm // SUBLANES), nb8 - 1), 0)),
                  pl.BlockSpec((3, c3), lambda i: (0, 0)),
                  pl.BlockSpec((1, c3), lambda i: (0, 0))],
        out_specs=[pl.BlockSpec((tm, HY_WIDTH), lambda i: (i, 0)),
                   pl.BlockSpec((tm, HY_WIDTH), lambda i: (i, 0))],
        out_shape=[jax.ShapeDtypeStruct((t, HY_WIDTH), F32),
                   jax.ShapeDtypeStruct((t, HY_WIDTH), F32)],
        compiler_params=_params("parallel"),
        name="hy_pre",
    )(z, z, z, conv_w, conv_b.reshape(1, c3))


def _filter_embedding(seq):
    t01 = np.linspace(0.0, 1.0, seq)[:, None]
    w = 2.0 * math.pi * np.arange(seq)[:, None] / seq
    f = np.linspace(1e-4, HY_BANDS - 1, HY_BANDS)[None, :]
    z = np.concatenate([t01, np.cos(f * w), -np.sin(f * w)], axis=-1)
    z_rev = np.concatenate([z[:1], z[:0:-1]], axis=0)
    zz = np.concatenate([z, z_rev], axis=0)
    out = np.zeros((2 * seq, LANES), np.float32)
    out[:, :zz.shape[1]] = zz
    return out


def _filter_body(zz_ref, dl_ref, w1_ref, b1_ref, f1_ref, w2_ref, b2_ref, f2_ref, w3_ref,
                 k_ref, s_ref, *, tm, lat_tiles, ctx_tiles):
    i = pl.program_id(0)
    zz = zz_ref[...]
    h = jnp.sin(f1_ref[...] * (_dot_f32(zz, w1_ref[...]) + b1_ref[...]))
    h = jnp.sin(f2_ref[...] * (_dot_f32(h, w2_ref[...]) + b2_ref[...]))
    h = _dot_f32(h, w3_ref[...])
    is_bwd = jnp.logical_or(jnp.logical_and(i >= lat_tiles // 2, i < lat_tiles),
                            i >= lat_tiles + ctx_tiles // 2)
    first_bwd = jnp.logical_or(i == lat_tiles // 2, i == lat_tiles + ctx_tiles // 2)
    window = jnp.exp(-zz[:, 0:1] * dl_ref[...])
    k = jnp.where(is_bwd, h[:, HY_WIDTH:], h[:, :HY_WIDTH]) * window
    rows = lax.broadcasted_iota(jnp.int32, k.shape, 0)
    k = jnp.where(jnp.logical_and(first_bwd, rows == 0), 0.0, k)
    k_ref[...] = k
    s = jnp.sum(jnp.abs(k), axis=0, keepdims=True)

    @pl.when(i == 0)
    def _():
        s_ref[...] = jnp.zeros_like(s_ref)

    @pl.when(i < lat_tiles)
    def _():
        s_ref[0:1, :] += s

    @pl.when(i >= lat_tiles)
    def _():
        s_ref[1:2, :] += s


def _hyena_filters(p, *, lat_seq, ctx_seq):
    tm = ctx_seq
    zz = jnp.asarray(np.concatenate([_filter_embedding(lat_seq), _filter_embedding(ctx_seq)], axis=0))
    rows = zz.shape[0]
    max_decay = math.log(HY_DECAY_TARGET) / HY_FAST_DECAY_PCT
    min_decay = math.log(HY_DECAY_TARGET) / HY_SLOW_DECAY_PCT
    deltas = jnp.asarray(np.abs(np.linspace(min_decay, max_decay, HY_WIDTH))[None, :].astype(np.float32))
    emb, hid = p['hy_fw1'].shape

    def pad2(a, r, c):
        return jnp.zeros((r, c), F32).at[:a.shape[0], :a.shape[1]].set(a)

    w1 = pad2(p['hy_fw1'], LANES, LANES)
    b1 = pad2(p['hy_fb1'][None, :], 1, LANES)
    f1 = pad2(p['hy_freq1'][None, :], 1, LANES)
    w2 = pad2(p['hy_fw2'], LANES, LANES)
    b2 = pad2(p['hy_fb2'][None, :], 1, LANES)
    f2 = pad2(p['hy_freq2'][None, :], 1, LANES)
    w3 = pad2(p['hy_fw3'], LANES, 2 * HY_WIDTH)
    body = functools.partial(_filter_body, tm=tm, lat_tiles=2 * lat_seq // tm, ctx_tiles=2 * ctx_seq // tm)
    full = lambda i: (0, 0)
    return pl.pallas_call(
        body,
        grid=(rows // tm,),
        in_specs=[pl.BlockSpec((tm, LANES), lambda i: (i, 0)),
                  pl.BlockSpec((1, HY_WIDTH), full),
                  pl.BlockSpec((LANES, LANES), full), pl.BlockSpec((1, LANES), full), pl.BlockSpec((1, LANES), full),
                  pl.BlockSpec((LANES, LANES), full), pl.BlockSpec((1, LANES), full), pl.BlockSpec((1, LANES), full),
                  pl.BlockSpec((LANES, 2 * HY_WIDTH), full)],
        out_specs=[pl.BlockSpec((tm, HY_WIDTH), lambda i: (i, 0)),
                   pl.BlockSpec((SUBLANES, HY_WIDTH), full)],
        out_shape=[jax.ShapeDtypeStruct((rows, HY_WIDTH), F32),
                   jax.ShapeDtypeStruct((SUBLANES, HY_WIDTH), F32)],
        compiler_params=_params("arbitrary"),
        name="hy_filter",
    )(zz, deltas, w1, b1, f1, w2, b2, f2, w3)


def _stack_complex(z):
    return np.block([[z.real, -z.imag], [z.imag, z.real]])


def _dft_consts_two_level(seq, n1, n2):
    n = 2 * seq
    assert n1 * n2 == n
    a1 = np.arange(n1)
    f1_full = np.exp(-2j * np.pi * np.outer(a1, a1) / n1)
    f1_u = np.concatenate([f1_full.real, f1_full.imag], axis=0)[:, :n1 // 2]
    f1_k = np.concatenate([f1_full.real, f1_full.imag], axis=0)
    a2 = np.arange(n2)
    f = a1[:, None, None] + n1 * a2[None, :, None]
    z = np.exp(-2j * np.pi * (f * a2[None, None, :]) / n)
    mf = np.stack([_stack_complex(z[i]) for i in range(n1)])
    mi = np.stack([_stack_complex(np.conj(z[i]).T) for i in range(n1)])
    g = np.exp(2j * np.pi * np.outer(a1[:n1 // 2], a1) / n1) / n
    gc, gs = g.real, -g.imag
    as32 = lambda a: jnp.asarray(a.astype(np.float32))
    return as32(f1_u), as32(f1_k), as32(mf), as32(mi), as32(gc), as32(gs)


def _dft_consts_one_level(seq):
    n = 2 * seq
    a = np.arange(n)
    z = np.exp(-2j * np.pi * np.outer(a, a) / n)
    mf = np.concatenate([z.real, z.imag], axis=0)
    zi = np.exp(2j * np.pi * np.outer(a[:seq], a) / n) / n
    mi = np.concatenate([zi.real, -zi.imag], axis=1)
    as32 = lambda a: jnp.asarray(a.astype(np.float32))
    return as32(mf), as32(mi)


def _lmat_body(f_ref, x_ref, sc_ref, or_ref, oi_ref):
    x = (x_ref[...] * (1.0 / sc_ref[...])).astype(BF16)
    o = _dot(f_ref[...], x)
    h = o.shape[0] // 2
    or_ref[...] = o[:h].astype(BF16)
    oi_ref[...] = o[h:].astype(BF16)


def _lmat(f, x, scale_row, *, tn):
    g, k, cols = x.shape
    m2 = f.shape[0]
    m = m2 // 2
    return pl.pallas_call(
        _lmat_body,
        grid=(g, cols // tn),
        in_specs=[pl.BlockSpec((m2, k), lambda b, j: (0, 0)),
                  pl.BlockSpec((None, k, tn), lambda b, j: (b, 0, j)),
                  pl.BlockSpec((1, tn), lambda b, j: (0, 0))],
        out_specs=[pl.BlockSpec((None, m, tn), lambda b, j: (b, 0, j)),
                   pl.BlockSpec((None, m, tn), lambda b, j: (b, 0, j))],
        out_shape=[jax.ShapeDtypeStruct((g, m, cols), BF16),
                   jax.ShapeDtypeStruct((g, m, cols), BF16)],
        compiler_params=_params("parallel", "parallel"),
        name="hy_dft1",
    )(f.astype(BF16), x, scale_row)


SPEC_GROUP = 4


def _spec_fwd_body(mf_ref, ar_ref, ai_ref, kr_ref, ki_ref):
    for g in range(SPEC_GROUP):
        a = jnp.concatenate([ar_ref[g], ai_ref[g]], axis=0)
        x = _dot(mf_ref[g], a)
        h = x.shape[0] // 2
        kr_ref[g] = x[:h]
        ki_ref[g] = x[h:]


def _spec_fwd(mf, ar, ai):
    n1, n2, c = ar.shape
    spec = pl.BlockSpec((SPEC_GROUP, n2, c), lambda i: (i, 0, 0))
    return pl.pallas_call(
        _spec_fwd_body,
        grid=(n1 // SPEC_GROUP,),
        in_specs=[pl.BlockSpec((SPEC_GROUP, 2 * n2, 2 * n2), lambda i: (i, 0, 0)), spec, spec],
        out_specs=[spec, spec],
        out_shape=[jax.ShapeDtypeStruct((n1, n2, c), F32)] * 2,
        compiler_params=_params("parallel"),
        name="hy_spec_filter",
    )(mf, ar, ai)


def _spec_mul_body(mf_ref, mi_ref, kr_ref, ki_ref, ar_ref, ai_ref, br_ref, bi_ref):
    for g in range(SPEC_GROUP):
        a = jnp.concatenate([ar_ref[g], ai_ref[g]], axis=0)
        x = _dot(mf_ref[g], a)
        h = x.shape[0] // 2
        xr, xi = x[:h], x[h:]
        kr, ki = kr_ref[g], ki_ref[g]
        y = jnp.concatenate([xr * kr - xi * ki, xr * ki + xi * kr], axis=0).astype(BF16)
        b = _dot(mi_ref[g], y)
        br_ref[g] = b[:h].astype(BF16)
        bi_ref[g] = b[h:].astype(BF16)


def _spec_mul(mf, mi, kr, ki, ar, ai):
    nb, n1, n2, c = ar.shape
    mspec = pl.BlockSpec((SPEC_GROUP, 2 * n2, 2 * n2), lambda i, b: (i, 0, 0))
    kspec = pl.BlockSpec((SPEC_GROUP, n2, c), lambda i, b: (i, 0, 0))
    aspec = pl.BlockSpec((None, SPEC_GROUP, n2, c), lambda i, b: (b, i, 0, 0))
    return pl.pallas_call(
        _spec_mul_body,
        grid=(n1 // SPEC_GROUP, nb),
        in_specs=[mspec, mspec, kspec, kspec, aspec, aspec],
        out_specs=[aspec, aspec],
        out_shape=[jax.ShapeDtypeStruct((nb, n1, n2, c), BF16)] * 2,
        compiler_params=_params("parallel", "arbitrary"),
        name="hy_spec_mul",
    )(mf, mi, kr, ki, ar, ai)


def _idft1_body(gc_ref, gs_ref, br_ref, bi_ref, o_ref):
    o_ref[...] = _dot(gc_ref[...], br_ref[...]) + _dot(gs_ref[...], bi_ref[...])


def _idft1(gc, gs, br, bi, *, tn):
    nb, n1, cols = br.shape
    m = gc.shape[0]
    gspec = pl.BlockSpec((m, n1), lambda b, j: (0, 0))
    bspec = pl.BlockSpec((None, n1, tn), lambda b, j: (b, 0, j))
    return pl.pallas_call(
        _idft1_body,
        grid=(nb, cols // tn),
        in_specs=[gspec, gspec, bspec, bspec],
        out_specs=pl.BlockSpec((None, m, tn), lambda b, j: (b, 0, j)),
        out_shape=jax.ShapeDtypeStruct((nb, m, cols), F32),
        compiler_params=_params("parallel", "parallel"),
        name="hy_idft1",
    )(gc.astype(BF16), gs.astype(BF16), br, bi)


def _ctx_filter_body(mf_ref, k_ref, sc_ref, kf_ref):
    kf_ref[...] = _dot(mf_ref[...], (k_ref[...] * (1.0 / sc_ref[...])).astype(BF16))


def _ctx_conv_body(mf_ref, mi_ref, kf_ref, u_ref, o_ref):
    x = _dot(mf_ref[...], u_ref[...].astype(BF16))
    h = x.shape[0] // 2
    xr, xi = x[:h], x[h:]
    kr, ki = kf_ref[:h, :], kf_ref[h:, :]
    y = jnp.concatenate([xr * kr - xi * ki, xr * ki + xi * kr], axis=0).astype(BF16)
    o_ref[...] = _dot(mi_ref[...], y)


def _hyena_ctx(k_raw, k_norm1, u, *, seq, nseq, u_row0):
    mf, mi = _dft_consts_one_level(seq)
    n = 2 * seq
    c = u.shape[1]
    kf = pl.pallas_call(
        _ctx_filter_body,
        out_shape=jax.ShapeDtypeStruct((2 * n, c), F32),
        compiler_params=_params(),
        name="hy_ctx_filter",
    )(mf.astype(BF16), k_raw, k_norm1)
    full = lambda s: (0, 0)
    return pl.pallas_call(
        _ctx_conv_body,
        grid=(nseq,),
        in_specs=[pl.BlockSpec((2 * n, seq), full),
                  pl.BlockSpec((seq, 2 * n), full),
                  pl.BlockSpec((2 * n, c), full),
                  pl.BlockSpec((seq, c), lambda s: (u_row0 // seq + s, 0))],
        out_specs=pl.BlockSpec((seq, c), lambda s: (s, 0)),
        out_shape=jax.ShapeDtypeStruct((nseq * seq, c), F32),
        compiler_params=_params("parallel"),
        name="hy_ctx_conv",
    )(mf[:, :seq].astype(BF16), mi.astype(BF16), kf, u)


def _hyena_lat(k_raw, k_norm1, u, *, seq, nb):
    c = u.shape[1]
    n1, n2 = 64, 2 * seq // 64
    f1_u, f1_k, mf, mi, gc, gs = _dft_consts_two_level(seq, n1, n2)
    mf = mf.astype(BF16)
    mi = mi.astype(BF16)
    cols = n2 * c
    tn = 4096
    rep = tn // c
    ones_row = jnp.ones((1, tn), F32)
    akr, aki = _lmat(f1_k, k_raw.reshape(1, n1, cols), jnp.tile(k_norm1, (1, rep)), tn=tn)
    kr, ki = _spec_fwd(mf, akr.reshape(n1, n2, c), aki.reshape(n1, n2, c))
    ar, ai = _lmat(f1_u, u.reshape(nb, n1 // 2, cols), ones_row, tn=tn)
    br, bi = _spec_mul(mf, mi, kr, ki, ar.reshape(nb, n1, n2, c), ai.reshape(nb, n1, n2, c))
    y = _idft1(gc, gs, br.reshape(nb, n1, cols), bi.reshape(nb, n1, cols), tn=tn)
    return y.reshape(nb * seq, c)


def _rope_tables(seq, rope_dims, lane_offsets, pad_rows):
    rows = seq // GRID_W
    rr, cc = np.meshgrid(np.arange(rows), np.arange(GRID_W), indexing='ij')
    pos = (rr.reshape(-1).astype(np.float64), cc.reshape(-1).astype(np.float64))
    half = rope_dims // 2
    q = half // 2
    inv_freq = ROPE_BASE ** (-np.arange(0, half, 2, dtype=np.float64) / half)
    cos_t = np.ones((seq + pad_rows, LANES), np.float64)
    sin_a = np.zeros((seq + pad_rows, LANES), np.float64)
    sin_b = np.zeros((seq + pad_rows, LANES), np.float64)
    for off in lane_offsets:
        for axis in range(2):
            ang = pos[axis][:, None] * inv_freq[None, :]
            base = off + axis * half
            cos_t[:seq, base:base + q] = np.cos(ang)
            cos_t[:seq, base + q:base + half] = np.cos(ang)
            sin_b[:seq, base:base + q] = -np.sin(ang)
            sin_a[:seq, base + q:base + half] = np.sin(ang)
    as32 = lambda a: jnp.asarray(a.astype(np.float32))
    return as32(cos_t), as32(sin_a), as32(sin_b)


def _rope(x, cos_t, sin_a, sin_b, shift):
    return x * cos_t + pltpu.roll(x, shift, 1) * sin_a + pltpu.roll(x, LANES - shift, 1) * sin_b


def _even_in_body(xa_ref, xb_ref, mod_ref, g_ref, w_ref, qa_ref, kva_ref, wuq_ref, qn_ref,
                  cos_ref, sa_ref, sb_ref, zhy_ref, kr_ref, q_ref, ckvn_ref, zq_ref, *, na, scale):
    i = pl.program_id(0)
    n = pl.num_programs(0) - 1

    @pl.when(i == 0)
    def _():
        zq_ref[1] = jnp.zeros(zq_ref.shape[1:], F32)

    c_q = MLA_Q_RANK
    zp = zq_ref[(i + 1) % 2]
    ckvn_ref[...] = _rms(zp[:, c_q:], kva_ref[...])
    cqn = _rms(zp[:, :c_q], qa_ref[...])
    q = _dot(cqn.astype(BF16), wuq_ref[...])
    cos_t, sin_a, sin_b = cos_ref[...], sa_ref[...], sb_ref[...]
    g = qn_ref[...]
    for hh in range(MLA_HEADS):
        sl = slice(hh * LANES, (hh + 1) * LANES)
        qh = _rope(_rms(q[:, sl], g, MLA_QK), cos_t, sin_a, sin_b, MLA_ROPE // 4)
        q_ref[:, sl] = (qh * scale).astype(BF16)

    x = jnp.where(jnp.minimum(i, n - 1) < na, xa_ref[...], xb_ref[...])
    h = _norm_mod(x, g_ref[...], mod_ref[0:1, :], mod_ref[1:2, :])
    z = _dot(h.astype(BF16), w_ref[...])
    c_hy = 3 * HY_WIDTH
    c_kv = c_hy + MLA_Q_RANK + MLA_KV_RANK
    zhy_ref[...] = z[:, :c_hy]
    kr_ref[...] = z[:, c_kv:]
    zq_ref[i % 2] = z[:, c_hy:c_kv]


def _pad_heads(w, heads, width):
    k = w.shape[0]
    w3 = w.reshape(k, heads, width)
    return jnp.zeros((k, heads, LANES), w.dtype).at[:, :, :width].set(w3).reshape(k, heads * LANES)


def _even_in(xa, xb, mod, p, tables, *, tm, seg_len, t_lat, lat_seq):
    d = xa.shape[1]
    t = xa.shape[0] + xb.shape[0]
    n_in = p['w_in'].shape[1]
    c_kv = 3 * HY_WIDTH + MLA_Q_RANK + MLA_KV_RANK
    assert c_kv % LANES == 0 and n_in - c_kv == MLA_ROPE
    n_pad = c_kv + LANES
    w_in = jnp.zeros((d, n_pad), BF16).at[:, :n_in].set(p['w_in'].astype(BF16))
    wuq = _pad_heads(p['w_uq'], MLA_HEADS, MLA_QK).astype(BF16)
    cos_t, sin_a, sin_b = tables
    qn = jnp.zeros((1, LANES), F32).at[0, :MLA_QK].set(p['q_norm'])
    pos_blocks = lat_seq // tm
    lat_tiles = t_lat // tm
    n = t // tm
    na = xa.shape[0] // tm
    cur = lambda i: jnp.minimum(i, n - 1)
    prev = lambda i: jnp.maximum(i - 1, 0)

    def tmap(i):
        j = prev(i)
        return (jnp.where(j < lat_tiles, j % pos_blocks, pos_blocks), 0)

    tspec = pl.BlockSpec((tm, LANES), tmap)
    full = lambda i: (0, 0)
    row_cur = lambda cols: pl.BlockSpec((tm, cols), lambda i: (cur(i), 0))
    row_prev = lambda cols: pl.BlockSpec((tm, cols), lambda i: (prev(i), 0))
    return pl.pallas_call(
        functools.partial(_even_in_body, na=na, scale=MLA_QK ** -0.5 * LOG2_E),
        grid=(n + 1,),
        in_specs=[pl.BlockSpec((tm, d), lambda i: (jnp.minimum(cur(i), na - 1), 0)),
                  pl.BlockSpec((tm, d), lambda i: (jnp.maximum(cur(i) - na, 0), 0)),
                  pl.BlockSpec((None, 6, d), lambda i: (cur(i) * tm // seg_len, 0, 0)),
                  pl.BlockSpec((1, d), full),
                  pl.BlockSpec((d, n_pad), full),
                  pl.BlockSpec((1, MLA_Q_RANK), full),
                  pl.BlockSpec((1, MLA_KV_RANK), full),
                  pl.BlockSpec((MLA_Q_RANK, MLA_HEADS * LANES), full),
                  pl.BlockSpec((1, LANES), full),
                  tspec, tspec, tspec],
        out_specs=[row_cur(3 * HY_WIDTH), row_cur(LANES), row_prev(MLA_HEADS * LANES), row_prev(MLA_KV_RANK)],
        out_shape=[jax.ShapeDtypeStruct((t, 3 * HY_WIDTH), F32),
                   jax.ShapeDtypeStruct((t, LANES), F32),
                   jax.ShapeDtypeStruct((t, MLA_HEADS * LANES), BF16),
                   jax.ShapeDtypeStruct((t, MLA_KV_RANK), F32)],
        scratch_shapes=[pltpu.VMEM((2, tm, MLA_Q_RANK + MLA_KV_RANK), F32)],
        compiler_params=_params("arbitrary"),
        name="even_in",
    )(xa, xb, mod, p['norm1'].reshape(1, d), w_in, p['qa_norm'].reshape(1, -1), p['kva_norm'].reshape(1, -1),
      wuq, qn, cos_t, sin_a, sin_b)


def _mla_kv_body(ckvn_ref, kr_ref, wk_ref, wv_ref, kn_ref, cos_ref, sa_ref, sb_ref, k_ref, v_ref):
    c = ckvn_ref[...].astype(BF16)
    k = _dot(c, wk_ref[...])
    v_ref[...] = _dot(c, wv_ref[...]).astype(BF16)
    kr = pltpu.roll(kr_ref[...], MLA_NOPE, 1)
    cos_t, sin_a, sin_b = cos_ref[...], sa_ref[...], sb_ref[...]
    g = kn_ref[...]
    for h in range(MLA_HEADS):
        sl = slice(h * LANES, (h + 1) * LANES)
        kh = _rope(_rms(k[:, sl] + kr, g, MLA_QK), cos_t, sin_a, sin_b, MLA_ROPE // 4)
        k_ref[:, sl] = kh.astype(BF16)


def _mla_kv(ckvn_rows, kr_rows, p, tables, *, tm, nb, past, lat_seq):
    r = ckvn_rows.shape[0]
    w = p['w_ukv'].reshape(MLA_KV_RANK, MLA_HEADS, MLA_NOPE + MLA_V)
    wk = _pad_heads(w[:, :, :MLA_NOPE].reshape(MLA_KV_RANK, -1), MLA_HEADS, MLA_NOPE).astype(BF16)
    wv = w[:, :, MLA_NOPE:].reshape(MLA_KV_RANK, MLA_HEADS * MLA_V).astype(BF16)
    cos_t, sin_a, sin_b = tables
    kn = jnp.zeros((1, LANES), F32).at[0, :MLA_QK].set(p['k_norm'])
    per_b = (past + lat_seq) // tm
    past_tiles = past // tm
    pos_blocks = lat_seq // tm
    lat_tiles = nb * per_b

    def tmap(i):
        j = i % per_b
        is_pos = jnp.logical_and(i < lat_tiles, j >= past_tiles)
        return (jnp.where(is_pos, j - past_tiles, pos_blocks), 0)

    tspec = pl.BlockSpec((tm, LANES), tmap)
    full = lambda i: (0, 0)
    return pl.pallas_call(
        _mla_kv_body,
        grid=(r // tm,),
        in_specs=[pl.BlockSpec((tm, MLA_KV_RANK), lambda i: (i, 0)),
                  pl.BlockSpec((tm, LANES), lambda i: (i, 0)),
                  pl.BlockSpec((MLA_KV_RANK, MLA_HEADS * LANES), full),
                  pl.BlockSpec((MLA_KV_RANK, MLA_HEADS * MLA_V), full),
                  pl.BlockSpec((1, LANES), full),
                  tspec, tspec, tspec],
        out_specs=[pl.BlockSpec((tm, MLA_HEADS * LANES), lambda i: (i, 0)),
                   pl.BlockSpec((tm, MLA_HEADS * MLA_V), lambda i: (i, 0))],
        out_shape=[jax.ShapeDtypeStruct((r, MLA_HEADS * LANES), BF16),
                   jax.ShapeDtypeStruct((r, MLA_HEADS * MLA_V), BF16)],
        compiler_params=_params("parallel"),
        name="mla_kv",
    )(ckvn_rows, kr_rows, wk, wv, kn, cos_t, sin_a, sin_b)


ATT_CHUNK = 512
ATT_UNIT_ROWS = 256
ATT_TQ = 1024


def _fill_vaug(vaug_ref, g, v_blocks):
    off = 0
    for v in v_blocks:
        n = v.shape[0]
        vaug_ref[g, off:off + n, :LANES] = v
        off += n
    vaug_ref[g, :, LANES:] = jnp.ones((vaug_ref.shape[1], LANES), BF16)


def _softmax_pv(units, s_ref, n_keys):
    chunk = min(ATT_CHUNK, n_keys)
    chunks = [slice(c * chunk, (c + 1) * chunk) for c in range(n_keys // chunk)]

    def scores(u, rows, m_lane):
        s = _dot_nt(units[u][0], units[u][1](rows))
        s_ref[u % 2, :, rows] = s
        for j in range(chunk // LANES):
            blk = s[:, j * LANES:(j + 1) * LANES]
            m_lane = blk if m_lane is None else jnp.maximum(m_lane, blk)
        return m_lane

    def values(u, rows, m, acc):
        p = jnp.exp2(s_ref[u % 2, :, rows] - m).astype(BF16)
        d = _dot(p, units[u][2](rows))
        return d if acc is None else acc + d

    outs = []
    m_lane = None
    for rows in chunks:
        m_lane = scores(0, rows, m_lane)
    for u in range(len(units)):
        m = jnp.max(m_lane, axis=-1, keepdims=True)
        acc, m_lane = None, None
        for rows in chunks:
            acc = values(u, rows, m, acc)
            if u + 1 < len(units):
                m_lane = scores(u + 1, rows, m_lane)
        outs.append(acc)
    return outs


def _mla_attn_body(prev_ref, q_ref, k_ref, v_ref, o_ref, vaug_ref, s_ref):
    pairs = vaug_ref.shape[0]

    @pl.when(pl.program_id(2) == 0)
    def _():
        for g in range(pairs):
            _fill_vaug(vaug_ref, g, [v_ref[:, g * LANES:(g + 1) * LANES]])

    n_keys = k_ref.shape[0]
    tq = q_ref.shape[0]
    ur = s_ref.shape[1]
    units, slots = [], []
    for r0 in range(0, tq, ur):
        for g in range(pairs):
            slots.append((r0, g))
            for hh in range(2):
                sl = slice((2 * g + hh) * LANES, (2 * g + hh + 1) * LANES)
                units.append((q_ref[r0:r0 + ur, sl], lambda rows, sl=sl: k_ref[rows, sl],
                              lambda rows, g=g: vaug_ref[g, rows, :]))
    res = [r[:, :LANES] / r[:, LANES:] for r in _softmax_pv(units, s_ref, n_keys)]
    lane = lax.broadcasted_iota(jnp.int32, res[0].shape, 1)
    for i, (r0, g) in enumerate(slots):
        o_ref[r0:r0 + ur, g * LANES:(g + 1) * LANES] = jnp.where(
            lane < MLA_V, res[2 * i], res[2 * i + 1]).astype(BF16)


def _mla_attn(prev, q, k, v, *, tq, pairs, n_seq, seq_q, seq_k, q_row0, k_row0):
    hp = MLA_HEADS // 2 // pairs
    nq = seq_q // tq
    qb0, kb0 = q_row0 // tq, k_row0 // seq_k
    return pl.pallas_call(
        _mla_attn_body,
        grid=(n_seq, hp, nq),
        in_specs=[pl.BlockSpec(memory_space=pl.ANY),
                  pl.BlockSpec((tq, pairs * 2 * LANES), lambda s, h, i: (qb0 + s * nq + i, h)),
                  pl.BlockSpec((seq_k, pairs * 2 * LANES), lambda s, h, i: (kb0 + s, h)),
                  pl.BlockSpec((seq_k, pairs * 2 * MLA_V), lambda s, h, i: (kb0 + s, h))],
        out_specs=pl.BlockSpec((tq, pairs * 2 * MLA_V), lambda s, h, i: (qb0 + s * nq + i, h)),
        out_shape=jax.ShapeDtypeStruct(prev.shape, prev.dtype),
        input_output_aliases={0: 0},
        scratch_shapes=[pltpu.VMEM((pairs, seq_k, 2 * LANES), BF16),
                        pltpu.VMEM((2, min(tq, ATT_UNIT_ROWS), seq_k), F32)],
        compiler_params=_params("arbitrary", "arbitrary", "arbitrary"),
        name="mla_attn",
    )(prev, q, k, v)


def _diff_attn_body(prev_ref, *refs, n_seg, lambda_init):
    q_ref, lam_ref, sub_ref = refs[0], refs[1], refs[2]
    k_refs = refs[3:3 + n_seg]
    v_refs = refs[3 + n_seg:3 + 2 * n_seg]
    o_ref, kcat_ref, vaug_ref, s_ref = refs[3 + 2 * n_seg:]
    heads = vaug_ref.shape[0]

    def head_block(ref, g):
        return ref[g] if len(ref.shape) == 3 else ref[:, g * LANES:(g + 1) * LANES]

    @pl.when(pl.program_id(2) == 0)
    def _():
        for g in range(heads):
            _fill_vaug(vaug_ref, g, [head_block(v, g) for v in v_refs])
            off = 0
            for k in k_refs:
                kcat_ref[g, off:off + k.shape[-2], :] = head_block(k, g)
                off += k.shape[-2]

    lp = lam_ref[...]
    lam = (jnp.exp(jnp.sum(lp[0:1] * lp[1:2], axis=-1, keepdims=True))
           - jnp.exp(jnp.sum(lp[2:3] * lp[3:4], axis=-1, keepdims=True)) + lambda_init)
    n_keys = kcat_ref.shape[1]
    tq = q_ref.shape[0]
    ur = s_ref.shape[1]
    units, slots = [], []
    for r0 in range(0, tq, ur):
        for g in range(heads):
            slots.append((r0, g))
            q = q_ref[r0:r0 + ur, g * LANES:(g + 1) * LANES].astype(F32)
            lane = lax.broadcasted_iota(jnp.int32, q.shape, 1)
            k_of = lambda rows, g=g: kcat_ref[g, rows, :]
            v_of = lambda rows, g=g: vaug_ref[g, rows, :]
            units.append((jnp.where(lane < DIFF_DH, q, 0.0).astype(BF16), k_of, v_of))
            units.append((jnp.where(lane < DIFF_DH, 0.0, q).astype(BF16), k_of, v_of))
    res = _softmax_pv(units, s_ref, n_keys)
    for i, (r0, g) in enumerate(slots):
        r1, r2 = res[2 * i], res[2 * i + 1]
        o = r1[:, :LANES] / r1[:, LANES:] - (lam / r2[:, LANES:]) * r2[:, :LANES]
        o_ref[r0:r0 + ur, g * LANES:(g + 1) * LANES] = (
            _rms(o, sub_ref[...]) * (1.0 - lambda_init)).astype(BF16)


def _diff_attn(prev, q, k_new, v_new, k_cache, v_cache, lam_p, subln, *, tq, heads, n_seq, seq_q, q_row0,
               lambda_init):
    nq = seq_q // tq
    qb0 = q_row0 // tq
    sb0 = q_row0 // seq_q
    d = 2 * DIFF_DH
    new_spec = pl.BlockSpec((seq_q, heads * d), lambda s, h, i: (sb0 + s, h))
    if k_cache is None:
        n_seg, k_args, v_args, k_specs, v_specs = 1, [k_new], [v_new], [new_spec], [new_spec]
        n_keys = seq_q
    else:
        past = k_cache.shape[2]
        c_spec = pl.BlockSpec((None, heads, past, d), lambda s, h, i: (s, h, 0, 0))
        n_seg, k_args, v_args = 2, [k_cache, k_new], [v_cache, v_new]
        k_specs, v_specs = [c_spec, new_spec], [c_spec, new_spec]
        n_keys = past + seq_q
    return pl.pallas_call(
        functools.partial(_diff_attn_body, n_seg=n_seg, lambda_init=lambda_init),
        grid=(n_seq, DIFF_HEADS // heads, nq),
        in_specs=[pl.BlockSpec(memory_space=pl.ANY),
                  pl.BlockSpec((tq, heads * d), lambda s, h, i: (qb0 + s * nq + i, h)),
                  pl.BlockSpec((4, DIFF_DH), lambda s, h, i: (0, 0)),
                  pl.BlockSpec((1, d), lambda s, h, i: (0, 0))] + k_specs + v_specs,
        out_specs=pl.BlockSpec((tq, heads * d), lambda s, h, i: (qb0 + s * nq + i, h)),
        out_shape=jax.ShapeDtypeStruct(prev.shape, prev.dtype),
        input_output_aliases={0: 0},
        scratch_shapes=[pltpu.VMEM((heads, n_keys, d), BF16), pltpu.VMEM((heads, n_keys, 2 * LANES), BF16),
                        pltpu.VMEM((2, min(tq, ATT_UNIT_ROWS), n_keys), F32)],
        compiler_params=_params("arbitrary", "arbitrary", "arbitrary"),
        name="diff_attn",
    )(prev, q, lam_p, subln.reshape(1, d), *k_args, *v_args)


def _even_out_body(xa_ref, xb_ref, mod_ref, ca_ref, cb_ref, u_ref, x0_ref, db_ref, o_ref, wa_ref, wb_ref,
                   out_ref, *, na):
    first = pl.program_id(0) < na
    x = jnp.where(first, xa_ref[...], xb_ref[...])
    conv = jnp.where(first, ca_ref[...], cb_ref[...])
    y_hy = (conv + u_ref[...] * db_ref[...]) * x0_ref[...]
    acc = _dot(y_hy.astype(BF16), wa_ref[...]) + _dot(o_ref[...], wb_ref[...])
    out_ref[...] = x + mod_ref[2:3, :] * acc


def _even_out(xa, xb, mod, conv_a, conv_b, u, x0, dbias, o, w_hy, w_att, *, tm, seg_len):
    d = xa.shape[1]
    t = xa.shape[0] + xb.shape[0]
    c = u.shape[1]
    xa_spec, xb_spec, na = _two_part_specs(xa, xb, tm)
    ca_spec, cb_spec, na_c = _two_part_specs(conv_a, conv_b, tm)
    assert na == na_c
    row = lambda cols: pl.BlockSpec((tm, cols), lambda i: (i, 0))
    full = lambda a: pl.BlockSpec(a.shape, lambda i: (0, 0))
    return pl.pallas_call(
        functools.partial(_even_out_body, na=na),
        grid=(t // tm,),
        in_specs=[xa_spec, xb_spec,
                  pl.BlockSpec((None, 6, d), lambda i: (i * tm // seg_len, 0, 0)),
                  ca_spec, cb_spec, row(c), row(c), full(dbias), row(o.shape[1]), full(w_hy), full(w_att)],
        out_specs=row(d),
        out_shape=jax.ShapeDtypeStruct((t, d), F32),
        compiler_params=_params("arbitrary"),
        name="even_out",
    )(xa, xb, mod, conv_a, conv_b, u, x0, dbias, o, w_hy, w_att)


def _ffn_body(x_ref, mod_ref, g_ref, wg_ref, wu_ref, wd_ref, o_ref, h_ref, acc_ref):
    f = pl.program_id(1)

    @pl.when(f == 0)
    def _():
        h_ref[...] = _norm_mod(x_ref[...], g_ref[...], mod_ref[3:4, :], mod_ref[4:5, :]).astype(BF16)
        acc_ref[...] = jnp.zeros_like(acc_ref)

    h = h_ref[...]
    a = _silu(_dot(h, wg_ref[...])) * _dot(h, wu_ref[...])
    acc_ref[...] += _dot(a.astype(BF16), wd_ref[...])

    @pl.when(f == pl.num_programs(1) - 1)
    def _():
        o_ref[...] = x_ref[...] + mod_ref[5:6, :] * acc_ref[...]


def _ffn(x, mod, g, wg, wu, wd, *, tm, tf, seg_len):
    t, d = x.shape
    ff = wg.shape[1]
    return pl.pallas_call(
        _ffn_body,
        grid=(t // tm, ff // tf),
        in_specs=[pl.BlockSpec((tm, d), lambda i, f: (i, 0)),
                  pl.BlockSpec((None, 6, d), lambda i, f: (i * tm // seg_len, 0, 0)),
                  pl.BlockSpec((1, d), lambda i, f: (0, 0)),
                  pl.BlockSpec((d, tf), lambda i, f: (0, f)),
                  pl.BlockSpec((d, tf), lambda i, f: (0, f)),
                  pl.BlockSpec((tf, d), lambda i, f: (f, 0))],
        out_specs=pl.BlockSpec((tm, d), lambda i, f: (i, 0)),
        out_shape=jax.ShapeDtypeStruct((t, d), F32),
        scratch_shapes=[pltpu.VMEM((tm, d), BF16), pltpu.VMEM((tm, d), F32)],
        compiler_params=_params("parallel", "arbitrary"),
        name="ffn",
    )(x, mod, g.reshape(1, d), wg, wu, wd)


def _group_ms(x, gmat):
    hi, lo = _split_bf16(x * x)
    return (_dot(hi, gmat) + _dot(lo, gmat)) * (1.0 / DIFF_DH)


def _qkv_body(x_ref, mod_ref, g_ref, w_ref, gm_ref, qn_ref, kn_ref, cos_ref, sa_ref, sb_ref,
              q_ref, k_ref, v_ref, kf_ref, vf_ref, *, scale, seq):
    h = _norm_mod(x_ref[...], g_ref[...], mod_ref[0:1, :], mod_ref[1:2, :]).astype(BF16)
    z = _dot(h, w_ref[...])
    hd = DIFF_HEADS * 2 * DIFF_DH
    tm = z.shape[0]
    cos_t, sin_a, sin_b = cos_ref[...], sa_ref[...], sb_ref[...]
    gm = gm_ref[...]
    shift = DIFF_DH // 4
    for hh in range(DIFF_HEADS):
        sl = slice(hh * LANES, (hh + 1) * LANES)
        qh = z[:, hh * LANES:(hh + 1) * LANES]
        qh = qh * lax.rsqrt(_group_ms(qh, gm) + NORM_EPS) * qn_ref[...]
        q_ref[:, sl] = (_rope(qh, cos_t, sin_a, sin_b, shift) * scale).astype(BF16)
        kh = z[:, hd + hh * LANES:hd + (hh + 1) * LANES]
        kh = kh * lax.rsqrt(_group_ms(kh, gm) + NORM_EPS) * kn_ref[...]
        k_ref[:, sl] = _rope(kh, cos_t, sin_a, sin_b, shift).astype(BF16)
        vh = z[:, 2 * hd + hh * LANES:2 * hd + (hh + 1) * LANES]
        for s in range(tm // seq):
            kf_ref[s, hh] = kh[s * seq:(s + 1) * seq, :]
            vf_ref[s, hh] = vh[s * seq:(s + 1) * seq, :]
    v_ref[...] = z[:, 2 * hd:].astype(BF16)


def _qkv(x, mod, p, tables, *, tm, seg_len, t_lat, lat_seq, n_ctx, ctx_seq):
    t, d = x.shape
    hd = DIFF_HEADS * 2 * DIFF_DH
    wqk = p['w_qkv'][:, :2 * hd].reshape(d, 2, 2, DIFF_HEADS, DIFF_DH)
    wqk = wqk.transpose(0, 1, 3, 2, 4).reshape(d, 2 * hd)
    w = jnp.concatenate([wqk, p['w_qkv'][:, 2 * hd:]], axis=1).astype(BF16)
    gi = np.arange(LANES) // DIFF_DH
    gmat = jnp.asarray((gi[:, None] == gi[None, :]).astype(np.float32)).astype(BF16)
    cos_t, sin_a, sin_b = tables
    qn = jnp.tile(p['q_norm'], 2).reshape(1, LANES)
    kn = jnp.tile(p['k_norm'], 2).reshape(1, LANES)
    pos_blocks = lat_seq // tm
    lat_tiles = t_lat // tm
    seq_per_tile = tm // ctx_seq
    tspec = pl.BlockSpec((tm, LANES), lambda i: (jnp.where(i < lat_tiles, i % pos_blocks, pos_blocks), 0))
    full = lambda i: (0, 0)
    row = pl.BlockSpec((tm, hd), lambda i: (i, 0))
    fspec = pl.BlockSpec((seq_per_tile, DIFF_HEADS, ctx_seq, LANES),
                         lambda i: (jnp.maximum(i - lat_tiles, 0), 0, 0, 0))
    fshape = jax.ShapeDtypeStruct((n_ctx, DIFF_HEADS, ctx_seq, LANES), F32)
    return pl.pallas_call(
        functools.partial(_qkv_body, scale=DIFF_DH ** -0.5 * LOG2_E, seq=ctx_seq),
        grid=(t // tm,),
        in_specs=[pl.BlockSpec((tm, d), lambda i: (i, 0)),
                  pl.BlockSpec((None, 6, d), lambda i: (i * tm // seg_len, 0, 0)),
                  pl.BlockSpec((1, d), full),
                  pl.BlockSpec((d, 3 * hd), full),
                  pl.BlockSpec((LANES, LANES), full),
                  pl.BlockSpec((1, LANES), full), pl.BlockSpec((1, LANES), full),
                  tspec, tspec, tspec],
        out_specs=[row, row, row, fspec, fspec],
        out_shape=[jax.ShapeDtypeStruct((t, hd), BF16)] * 3 + [fshape, fshape],
        compiler_params=_params("arbitrary"),
        name="qkv",
    )(x, mod, p['norm1'].reshape(1, d), w, gmat, qn, kn, cos_t, sin_a, sin_b)


def _route(logits):
    lane = lax.broadcasted_iota(jnp.int32, logits.shape, 1)
    neg = jnp.float32(-jnp.inf)
    lg = jnp.where(lane < N_EXPERTS, logits, neg)
    m1 = jnp.max(lg, axis=-1, keepdims=True)
    i1 = jnp.min(jnp.where(lg == m1, lane, LANES), axis=-1, keepdims=True)
    lg2 = jnp.where(lane == i1, neg, lg)
    m2 = jnp.max(lg2, axis=-1, keepdims=True)
    i2 = jnp.min(jnp.where(lg2 == m2, lane, LANES), axis=-1, keepdims=True)
    e = jnp.exp(m2 - m1)
    w1 = 1.0 / (1.0 + e)
    w2 = e / (1.0 + e)
    return jnp.where(lane == i1, w1, 0.0) + jnp.where(lane == i2, w2, 0.0)


MOE_BLOCK = 1024
MOE_SUB = 256
MOE_ROUTE_TM = 512
MOE_WINDOW = 5
MOE_TF = 896
MOE_CMB = 256
MOE_CMB_BLOCKS = MOE_SUB // MOE_CMB + 1


def _moe_route_body(x_ref, o_ref, wo_ref, mod_ref, g_ref, wr_ref, xo_ref, h_ref, gates_ref, rank_ref, rank_t_ref,
                    carry_row, carry_col, *, tm):
    i = pl.program_id(0)

    @pl.when(i == 0)
    def _():
        carry_row[...] = jnp.zeros_like(carry_row)
        carry_col[...] = jnp.zeros_like(carry_col)

    x = x_ref[...] + mod_ref[2:3, :] * _dot(o_ref[...], wo_ref[...])
    xo_ref[...] = x
    h = _norm_mod(x, g_ref[...], mod_ref[3:4, :], mod_ref[4:5, :])
    h_ref[...] = h.astype(BF16)
    gates = _route(_dot_f32(h, wr_ref[...]))
    gates_ref[...] = gates
    sel = jnp.where(gates != 0.0, 1.0, 0.0)
    sel_t = sel.T
    r = lax.broadcasted_iota(jnp.int32, (tm, tm), 0)
    c = lax.broadcasted_iota(jnp.int32, (tm, tm), 1)
    lower = jnp.where(c < r, 1.0, 0.0).astype(BF16)
    upper = jnp.where(r < c, 1.0, 0.0).astype(BF16)
    before = _dot(lower, sel.astype(BF16)) + carry_row[...]
    before_t = _dot(sel_t.astype(BF16), upper) + carry_col[...]
    rank_ref[...] = jnp.where(sel > 0.0, before, -1.0)
    rank_t = jnp.where(sel_t > 0.0, before_t, -1.0)
    for s in range(tm // MOE_SUB):
        rank_t_ref[s] = rank_t[:SUBLANES, s * MOE_SUB:(s + 1) * MOE_SUB]
    carry_row[...] += jnp.sum(sel, axis=0, keepdims=True)
    carry_col[...] += jnp.sum(sel_t, axis=1, keepdims=True)


def _moe_route(x, o, w_out, mod, g, w_router, *, seg_len):
    t, d = x.shape
    tm = MOE_ROUTE_TM
    ne = w_router.shape[1]
    assert ne <= SUBLANES
    wr = jnp.zeros((d, LANES), F32).at[:, :ne].set(w_router)
    sub = tm // MOE_SUB
    row = lambda cols: pl.BlockSpec((tm, cols), lambda i: (i, 0))
    full = lambda a: pl.BlockSpec(a.shape, lambda i: (0, 0))
    return pl.pallas_call(
        functools.partial(_moe_route_body, tm=tm),
        grid=(t // tm,),
        in_specs=[row(d), row(o.shape[1]), full(w_out),
                  pl.BlockSpec((None, 6, d), lambda i: (i * tm // seg_len, 0, 0)),
                  pl.BlockSpec((1, d), lambda i: (0, 0)),
                  full(wr)],
        out_specs=[row(d), row(d), row(LANES), row(LANES),
                   pl.BlockSpec((sub, SUBLANES, MOE_SUB), lambda i: (i, 0, 0))],
        out_shape=[jax.ShapeDtypeStruct((t, d), F32),
                   jax.ShapeDtypeStruct((t, d), BF16),
                   jax.ShapeDtypeStruct((t, LANES), F32),
                   jax.ShapeDtypeStruct((t, LANES), F32),
                   jax.ShapeDtypeStruct((t // MOE_SUB, SUBLANES, MOE_SUB), F32)],
        scratch_shapes=[pltpu.VMEM((1, LANES), F32), pltpu.VMEM((LANES, 1), F32)],
        compiler_params=_params("arbitrary"),
        name="moe_route",
    )(x, o, w_out, mod, g.reshape(1, d), wr)


def _moe_plan(rank, ne, *, n_blocks):
    t = rank.shape[0]
    n_tiles = t // MOE_SUB
    per_blk = MOE_BLOCK // MOE_SUB
    n_sub = n_blocks * per_blk
    sel = (rank[:, :ne] >= 0.0).astype(jnp.int32)
    tile_cnt = sel.reshape(n_tiles, MOE_SUB, ne).sum(axis=1)
    tile_end = jnp.cumsum(tile_cnt, axis=0)
    tile_start = tile_end - tile_cnt
    cnt = tile_end[-1]
    nblk = (cnt + MOE_BLOCK - 1) // MOE_BLOCK
    bend = jnp.cumsum(nblk)
    bstart = bend - nblk
    e_last = jnp.max(jnp.where(cnt > 0, jnp.arange(ne), 0))
    b = jnp.arange(n_blocks)
    blk_valid = b < bend[-1]
    blk_e = jnp.minimum(jnp.sum(bend[None, :] <= b[:, None], axis=1), e_last).astype(jnp.int32)
    blk_r0 = (b - bstart[blk_e]) * MOE_BLOCK
    blk_rows = jnp.where(blk_valid, jnp.clip(cnt[blk_e] - blk_r0, 0, MOE_BLOCK), 0).astype(jnp.int32)
    j = jnp.arange(n_sub)
    sub_e = blk_e[j // per_blk]
    sub_r0 = blk_r0[j // per_blk] + (j % per_blk) * MOE_SUB
    sub_valid = jnp.logical_and(blk_valid[j // per_blk], sub_r0 < cnt[sub_e])
    ends = tile_end[:, sub_e]
    r1 = jnp.minimum(sub_r0 + MOE_SUB, cnt[sub_e])
    c_lo = jnp.sum(ends <= sub_r0[None, :], axis=0)
    c_hi = jnp.sum(ends < r1[None, :], axis=0)
    c_lo = jnp.where(sub_valid, c_lo, 1).astype(jnp.int32)
    c_hi = jnp.where(sub_valid, jnp.minimum(c_hi, n_tiles - 1), 0).astype(jnp.int32)
    base = (bstart * MOE_BLOCK).astype(jnp.int32)
    n_cmb = n_sub * (MOE_SUB // MOE_CMB)
    j0 = jnp.minimum((base[None, :] + tile_start) // MOE_CMB, n_cmb - MOE_CMB_BLOCKS).astype(jnp.int32)
    return dict(blk_e=blk_e, blk_valid=blk_valid.astype(jnp.int32), blk_rows=blk_rows,
                sub_e=sub_e.astype(jnp.int32), sub_r0=sub_r0.astype(jnp.int32), c_lo=c_lo, c_hi=c_hi,
                base=base, j0=j0.reshape(-1))


def _moe_dispatch_body(e_ref, r0_ref, lo_ref, hi_ref, h_ref, rank_t_ref, xs_ref, acc_ref):
    j = pl.program_id(0)
    e = e_ref[j]
    rows = (r0_ref[j] + lax.broadcasted_iota(jnp.int32, (MOE_SUB, 1), 0)).astype(F32)
    sub = lax.broadcasted_iota(jnp.int32, (SUBLANES, MOE_SUB), 0)
    n_tiles = rank_t_ref.shape[0]
    lo, hi = lo_ref[j], hi_ref[j]
    acc_ref[...] = jnp.zeros_like(acc_ref)

    def step(w, carry):
        first = lo + w * MOE_WINDOW
        c0 = jnp.minimum(first, n_tiles - MOE_WINDOW)
        pieces = []
        for i in range(MOE_WINDOW):
            c = c0 + i
            rk = jnp.sum(jnp.where(sub == e, rank_t_ref[c], 0.0), axis=0, keepdims=True)
            rk = jnp.where(c >= first, rk, -1.0)
            pieces.append(jnp.where(rk == rows, 1.0, 0.0).astype(BF16))
        onehot = jnp.concatenate(pieces, axis=1)
        off = pl.multiple_of(c0 * MOE_SUB, MOE_SUB)
        acc_ref[...] += _dot(onehot, h_ref[pl.ds(off, MOE_WINDOW * MOE_SUB), :])
        return carry

    lax.fori_loop(0, (hi - lo + MOE_WINDOW) // MOE_WINDOW, step, 0)
    xs_ref[...] = acc_ref[...].astype(BF16)


def _moe_dispatch(h, rank_t, plan, *, n_sub):
    t, d = h.shape
    grid_spec = pltpu.PrefetchScalarGridSpec(
        num_scalar_prefetch=4,
        grid=(n_sub,),
        in_specs=[pl.BlockSpec((t, d), lambda j, *_: (0, 0), pipeline_mode=pl.Buffered(1)),
                  pl.BlockSpec(rank_t.shape, lambda j, *_: (0, 0, 0), pipeline_mode=pl.Buffered(1))],
        out_specs=pl.BlockSpec((MOE_SUB, d), lambda j, *_: (j, 0)),
        scratch_shapes=[pltpu.VMEM((MOE_SUB, d), F32)],
    )
    return pl.pallas_call(
        _moe_dispatch_body,
        grid_spec=grid_spec,
        out_shape=jax.ShapeDtypeStruct((n_sub * MOE_SUB, d), BF16),
        compiler_params=_params("arbitrary"),
        name="moe_dispatch",
    )(plan['sub_e'], plan['sub_r0'], plan['c_lo'], plan['c_hi'], h, rank_t)


def _moe_ffn_body(e_ref, valid_ref, rows_ref, xs_ref, wg_ref, wu_ref, wd_ref, y_ref, acc_ref):
    b = pl.program_id(0)
    f = pl.program_id(1)
    n_rows = rows_ref[b]
    last = f == pl.num_programs(1) - 1
    wg = wg_ref[...].astype(BF16)
    wu = wu_ref[...].astype(BF16)
    wd = wd_ref[...].astype(BF16)
    full = n_rows == MOE_BLOCK

    def swiglu(h):
        a = _silu(_dot(h, wg)) * _dot(h, wu)
        return _dot(a.astype(BF16), wd)

    @pl.when(jnp.logical_and(full, f == 0))
    def _():
        acc_ref[...] = swiglu(xs_ref[...])

    @pl.when(jnp.logical_and(full, f > 0))
    def _():
        acc_ref[...] += swiglu(xs_ref[...])

    @pl.when(jnp.logical_and(full, last))
    def _():
        y_ref[...] = acc_ref[...].astype(BF16)

    for s in range(MOE_BLOCK // MOE_SUB):
        sl = slice(s * MOE_SUB, (s + 1) * MOE_SUB)
        live = jnp.logical_and(jnp.logical_not(full), s * MOE_SUB < n_rows)
        dead = jnp.logical_and(jnp.logical_not(full), s * MOE_SUB >= n_rows)

        @pl.when(jnp.logical_and(live, f == 0))
        def _():
            acc_ref[sl, :] = jnp.zeros((MOE_SUB, acc_ref.shape[1]), F32)

        @pl.when(live)
        def _():
            acc_ref[sl, :] += swiglu(xs_ref[sl, :])

        @pl.when(jnp.logical_and(live, last))
        def _():
            y_ref[sl, :] = acc_ref[sl, :].astype(BF16)

        @pl.when(jnp.logical_and(dead, last))
        def _():
            y_ref[sl, :] = jnp.zeros((MOE_SUB, y_ref.shape[1]), BF16)


def _moe_ffn(xs, wg, wu, wd, plan, *, n_blocks, tf):
    _, d = xs.shape
    ne, _, ff = wg.shape
    nf = ff // tf

    def w_in(b, f, e_ref, valid_ref, rows_ref):
        return (e_ref[b], 0, jnp.where(valid_ref[b] > 0, f, nf - 1))

    def w_down(b, f, e_ref, valid_ref, rows_ref):
        return (e_ref[b], jnp.where(valid_ref[b] > 0, f, nf - 1), 0)

    grid_spec = pltpu.PrefetchScalarGridSpec(
        num_scalar_prefetch=3,
        grid=(n_blocks, nf),
        in_specs=[pl.BlockSpec((MOE_BLOCK, d), lambda b, f, *_: (b, 0)),
                  pl.BlockSpec((None, d, tf), w_in),
                  pl.BlockSpec((None, d, tf), w_in),
                  pl.BlockSpec((None, tf, d), w_down)],
        out_specs=pl.BlockSpec((MOE_BLOCK, d), lambda b, f, *_: (b, 0)),
        scratch_shapes=[pltpu.VMEM((MOE_BLOCK, d), F32)],
    )
    return pl.pallas_call(
        _moe_ffn_body,
        grid_spec=grid_spec,
        out_shape=jax.ShapeDtypeStruct((n_blocks * MOE_BLOCK, d), BF16),
        compiler_params=_params("arbitrary", "arbitrary"),
        name="moe_ffn",
    )(plan['blk_e'], plan['blk_valid'], plan['blk_rows'], xs, wg, wu, wd)


def _moe_combine_body(j0_ref, base_ref, x_ref, mod_ref, gates_ref, rank_ref, *rest, ne, split_tiles):
    nblk = MOE_CMB_BLOCKS
    y_refs, o_ref, o2_ref = rest[:nblk * ne], rest[nblk * ne], rest[nblk * ne + 1]
    c = pl.program_id(0)
    gates = gates_ref[...]
    rank = rank_ref[...]
    lane = lax.broadcasted_iota(jnp.int32, gates.shape, 1)
    col = lax.broadcasted_iota(jnp.int32, (1, MOE_CMB), 1).astype(F32)
    acc = None
    for e in range(ne):
        pick = lane == e
        g = jnp.sum(jnp.where(pick, gates, 0.0), axis=-1, keepdims=True)
        rk = jnp.sum(jnp.where(pick, rank, 0.0), axis=-1, keepdims=True)
        shift = (base_ref[e] - j0_ref[c * ne + e] * MOE_CMB).astype(F32)
        loc = jnp.where(rk >= 0.0, rk + shift, -1.0)
        picked = None
        for b in range(nblk):
            onehot = jnp.where(loc == col + float(b * MOE_CMB), 1.0, 0.0).astype(BF16)
            d = _dot(onehot, y_refs[nblk * e + b][...])
            picked = d if picked is None else picked + d
        contrib = g * picked
        acc = contrib if acc is None else acc + contrib
    out = x_ref[...] + mod_ref[5:6, :] * acc

    @pl.when(c < split_tiles)
    def _():
        o_ref[...] = out

    @pl.when(c >= split_tiles)
    def _():
        o2_ref[...] = out


def _moe_combine(x, mod, gates, rank, y, plan, *, ne, seg_len, t_split):
    t, d = x.shape
    tm = MOE_SUB
    split_tiles = t_split // tm
    y_specs = []
    for e in range(ne):
        for b in range(MOE_CMB_BLOCKS):
            y_specs.append(pl.BlockSpec((MOE_CMB, d), lambda c, j0, base, e=e, b=b: (j0[c * ne + e] + b, 0)))
    grid_spec = pltpu.PrefetchScalarGridSpec(
        num_scalar_prefetch=2,
        grid=(t // tm,),
        in_specs=[pl.BlockSpec((tm, d), lambda c, *_: (c, 0)),
                  pl.BlockSpec((None, 6, d), lambda c, *_: (c * tm // seg_len, 0, 0)),
                  pl.BlockSpec((tm, LANES), lambda c, *_: (c, 0)),
                  pl.BlockSpec((tm, LANES), lambda c, *_: (c, 0))] + y_specs,
        out_specs=[pl.BlockSpec((tm, d), lambda c, *_: (jnp.minimum(c, split_tiles - 1), 0)),
                   pl.BlockSpec((tm, d), lambda c, *_: (jnp.maximum(c - split_tiles, 0), 0))],
    )
    return pl.pallas_call(
        functools.partial(_moe_combine_body, ne=ne, split_tiles=split_tiles),
        grid_spec=grid_spec,
        out_shape=[jax.ShapeDtypeStruct((t_split, d), F32), jax.ShapeDtypeStruct((t - t_split, d), F32)],
        compiler_params=_params("arbitrary"),
        name="moe_combine",
    )(plan['j0'], plan['base'], x, mod, gates, rank, *([y] * (MOE_CMB_BLOCKS * ne)))


def _attn_out_moe(x, o, w_out, mod, g, w_router, wg, wu, wd, *, seg_len, t_split, top_k=2):
    t, d = x.shape
    ne = w_router.shape[1]
    n_blocks = t * top_k // MOE_BLOCK + ne
    x, h, gates, rank, rank_t = _moe_route(x, o, w_out, mod, g, w_router, seg_len=seg_len)
    plan = _moe_plan(rank, ne, n_blocks=n_blocks)
    xs = _moe_dispatch(h, rank_t, plan, n_sub=n_blocks * (MOE_BLOCK // MOE_SUB))
    y = _moe_ffn(xs, wg, wu, wd, plan, n_blocks=n_blocks, tf=MOE_TF)
    return _moe_combine(x, mod, gates, rank, y, plan, ne=ne, seg_len=seg_len, t_split=t_split)


def _even_layer(x_lat, x_ctx, cond8, p, cache_ckv, cache_kr, *, nb, lat_seq, n_ctx, ctx_seq):
    d = x_lat.shape[1]
    t_lat = nb * lat_seq
    t = t_lat + x_ctx.shape[0]
    seg_len = lat_seq
    past = cache_ckv.shape[1]
    mod = _adaln(cond8, p['w_mod'], p['b_mod'])

    tm = 512
    tables = _rope_tables(lat_seq, MLA_ROPE, (MLA_NOPE,), tm)
    z_hy, kr, q, ckvn = _even_in(x_lat, x_ctx, mod, p, tables, tm=tm, seg_len=seg_len, t_lat=t_lat,
                                 lat_seq=lat_seq)

    u, x0 = _hy_pre(z_hy, p['hy_conv_w'], p['hy_conv_b'], t_lat=t_lat, lat_seq=lat_seq, ctx_seq=ctx_seq)
    k_raw, k_sum = _hyena_filters(p, lat_seq=lat_seq, ctx_seq=ctx_seq)
    dbias = p['hy_dbias'].reshape(1, HY_WIDTH)
    conv_lat = _hyena_lat(k_raw[:2 * lat_seq], k_sum[0:1], u[:t_lat], seq=lat_seq, nb=nb)
    conv_ctx = _hyena_ctx(k_raw[2 * lat_seq:], k_sum[1:2], u, seq=ctx_seq, nseq=n_ctx, u_row0=t_lat)

    cache_kr_p = jnp.zeros((nb, past, LANES), F32).at[:, :, :MLA_ROPE].set(cache_kr)
    ckvn_rows = jnp.concatenate(
        [jnp.concatenate([cache_ckv, ckvn[:t_lat].reshape(nb, lat_seq, -1)], axis=1).reshape(nb * (past + lat_seq), -1),
         ckvn[t_lat:]], axis=0)
    kr_rows = jnp.concatenate(
        [jnp.concatenate([cache_kr_p, kr[:t_lat].reshape(nb, lat_seq, LANES)], axis=1).reshape(nb * (past + lat_seq), LANES),
         kr[t_lat:]], axis=0)
    k_all, v_all = _mla_kv(ckvn_rows, kr_rows, p, tables, tm=tm, nb=nb, past=past, lat_seq=lat_seq)
    o = jnp.zeros((t, MLA_HEADS * MLA_V), BF16)
    o = _mla_attn(o, q, k_all, v_all, tq=ATT_TQ, pairs=1, n_seq=nb, seq_q=lat_seq, seq_k=past + lat_seq,
                  q_row0=0, k_row0=0)
    o = _mla_attn(o, q, k_all, v_all, tq=ctx_seq, pairs=MLA_HEADS // 2, n_seq=n_ctx, seq_q=ctx_seq,
                  seq_k=ctx_seq, q_row0=t_lat, k_row0=nb * (past + lat_seq))

    w_out = p['w_out'].astype(BF16)
    x = _even_out(x_lat, x_ctx, mod, conv_lat, conv_ctx, u, x0, dbias, o, w_out[:HY_WIDTH], w_out[HY_WIDTH:],
                  tm=512, seg_len=seg_len)
    x = _ffn(x, mod, p['norm2'], p['ffn_w_gate'].astype(BF16), p['ffn_w_up'].astype(BF16),
             p['ffn_w_down'].astype(BF16), tm=512, tf=1408, seg_len=seg_len)
    new_ckv = ckvn[t_lat:].reshape(n_ctx, ctx_seq, -1)
    new_kr = kr[t_lat:, :MLA_ROPE].reshape(n_ctx, ctx_seq, MLA_ROPE)
    return x, new_ckv, new_kr


def _odd_layer(x, cond8, p, cache_k, cache_v, lambda_init, *, nb, lat_seq, n_ctx, ctx_seq):
    t, d = x.shape
    t_lat = nb * lat_seq
    seg_len = lat_seq
    mod = _adaln(cond8, p['w_mod'], p['b_mod'])
    tm = 512
    tables = _rope_tables(lat_seq, DIFF_DH, (0, DIFF_DH), tm)
    q, k, v, new_k, new_v = _qkv(x, mod, p, tables, tm=tm, seg_len=seg_len, t_lat=t_lat, lat_seq=lat_seq,
                                 n_ctx=n_ctx, ctx_seq=ctx_seq)
    lam_p = jnp.stack([p['lam_q1'], p['lam_k1'], p['lam_q2'], p['lam_k2']])
    o = jnp.zeros((t, DIFF_HEADS * 2 * DIFF_DH), BF16)
    o = _diff_attn(o, q, k, v, cache_k.astype(BF16), cache_v.astype(BF16), lam_p, p['subln'],
                   tq=ATT_TQ, heads=1, n_seq=nb, seq_q=lat_seq, q_row0=0, lambda_init=lambda_init)
    o = _diff_attn(o, q, k, v, None, None, lam_p, p['subln'],
                   tq=ctx_seq, heads=DIFF_HEADS, n_seq=n_ctx, seq_q=ctx_seq, q_row0=t_lat,
                   lambda_init=lambda_init)
    x_lat, x_ctx = _attn_out_moe(x, o, p['w_out'].astype(BF16), mod, p['norm2'], p['w_router'],
                                 p['moe_w_gate'], p['moe_w_up'], p['moe_w_down'], seg_len=seg_len, t_split=t_lat)
    return x_lat, x_ctx, new_k, new_v


def kernel(x_prompt, x_sample, cache_l0_ckv, cache_l0_krope, cache_l1_k, cache_l1_v, c, c_ctx,
           l0_w_mod, l0_b_mod, l0_norm1, l0_norm2, l0_w_in, l0_hy_conv_w, l0_hy_conv_b,
           l0_hy_fw1, l0_hy_fb1, l0_hy_freq1, l0_hy_fw2, l0_hy_fb2, l0_hy_freq2, l0_hy_fw3, l0_hy_dbias,
           l0_mla_qa_norm, l0_mla_w_uq, l0_mla_kva_norm, l0_mla_w_ukv, l0_mla_q_norm, l0_mla_k_norm,
           l0_w_out, l0_ffn_w_gate, l0_ffn_w_up, l0_ffn_w_down,
           l1_w_mod, l1_b_mod, l1_norm1, l1_norm2, l1_w_qkv, l1_q_norm, l1_k_norm,
           l1_lam_q1, l1_lam_k1, l1_lam_q2, l1_lam_k2, l1_subln, l1_w_out,
           l1_w_router, l1_moe_w_gate, l1_moe_w_up, l1_moe_w_down):
    even = {
        'w_mod': l0_w_mod, 'b_mod': l0_b_mod, 'norm1': l0_norm1, 'norm2': l0_norm2, 'w_in': l0_w_in,
        'hy_conv_w': l0_hy_conv_w, 'hy_conv_b': l0_hy_conv_b, 'hy_fw1': l0_hy_fw1, 'hy_fb1': l0_hy_fb1,
        'hy_freq1': l0_hy_freq1, 'hy_fw2': l0_hy_fw2, 'hy_fb2': l0_hy_fb2, 'hy_freq2': l0_hy_freq2,
        'hy_fw3': l0_hy_fw3, 'hy_dbias': l0_hy_dbias, 'qa_norm': l0_mla_qa_norm, 'w_uq': l0_mla_w_uq,
        'kva_norm': l0_mla_kva_norm, 'w_ukv': l0_mla_w_ukv, 'q_norm': l0_mla_q_norm, 'k_norm': l0_mla_k_norm,
        'w_out': l0_w_out, 'ffn_w_gate': l0_ffn_w_gate, 'ffn_w_up': l0_ffn_w_up, 'ffn_w_down': l0_ffn_w_down,
    }
    odd = {
        'w_mod': l1_w_mod, 'b_mod': l1_b_mod, 'norm1': l1_norm1, 'norm2': l1_norm2, 'w_qkv': l1_w_qkv,
        'q_norm': l1_q_norm, 'k_norm': l1_k_norm, 'lam_q1': l1_lam_q1, 'lam_k1': l1_lam_k1,
        'lam_q2': l1_lam_q2, 'lam_k2': l1_lam_k2, 'subln': l1_subln, 'w_out': l1_w_out,
        'w_router': l1_w_router, 'moe_w_gate': l1_moe_w_gate, 'moe_w_up': l1_moe_w_up,
        'moe_w_down': l1_moe_w_down,
    }
    n_ctx, ctx_seq, d = x_prompt.shape
    nb, lat_seq, _ = x_sample.shape
    assert n_ctx * ctx_seq == lat_seq, "segment layout needs equally sized modulation segments"
    dims = dict(nb=nb, lat_seq=lat_seq, n_ctx=n_ctx, ctx_seq=ctx_seq)
    t_lat = nb * lat_seq
    cond8 = jnp.zeros((SUBLANES, d), F32).at[:nb].set(c).at[nb].set(c_ctx)

    x, new_l0_ckv, new_l0_krope = _even_layer(x_sample.reshape(t_lat, d), x_prompt.reshape(n_ctx * ctx_seq, d),
                                              cond8, even, cache_l0_ckv, cache_l0_krope, **dims)
    lambda_init = 0.8 - 0.6 * math.exp(-0.3 * 1)
    x_lat, x_ctx, new_l1_k, new_l1_v = _odd_layer(x, cond8, odd, cache_l1_k, cache_l1_v, lambda_init, **dims)

    y_sample = x_lat.reshape(nb, lat_seq, d)
    y_prompt = x_ctx.reshape(n_ctx, ctx_seq, d)
    return (y_prompt, y_sample, new_l0_ckv, new_l0_krope, new_l1_k, new_l1_v)
```

```python
import functools
import math

import numpy as np
import jax
import jax.numpy as jnp
from jax import lax
from jax.experimental import pallas as pl
from jax.experimental.pallas import tpu as pltpu

F32 = jnp.float32
BF16 = jnp.bfloat16

VMEM_LIMIT_BYTES = 56 * 1024 * 1024
LANES = 128
SUBLANES = 8
LOG2_E = math.log2(math.e)

GRID_W = 64
ROPE_BASE = 10000.0
NORM_EPS = 1e-6
HY_WIDTH = 512
HY_BANDS = 16
HY_FAST_DECAY_PCT = 0.3
HY_SLOW_DECAY_PCT = 1.5
HY_DECAY_TARGET = 1e-2
MLA_HEADS = 8
MLA_NOPE = 64
MLA_ROPE = 32
MLA_QK = MLA_NOPE + MLA_ROPE
MLA_V = 64
MLA_Q_RANK = 768
MLA_KV_RANK = 256
DIFF_HEADS = 8
DIFF_DH = 64
N_EXPERTS = 8


def _params(*sem):
    return pltpu.CompilerParams(dimension_semantics=sem, vmem_limit_bytes=VMEM_LIMIT_BYTES)


def _dot(a, b):
    return jnp.dot(a, b, preferred_element_type=F32)


def _dot_nt(a, b):
    return lax.dot_general(a, b, (((1,), (1,)), ((), ())), preferred_element_type=F32)


def _split_bf16(a):
    hi = a.astype(BF16)
    lo = (a - hi.astype(F32)).astype(BF16)
    return hi, lo


def _dot_f32(a, b):
    ah, al = _split_bf16(a)
    bh, bl = _split_bf16(b)
    return _dot(ah, bh) + (_dot(al, bh) + _dot(ah, bl))


def _rms(x, g, n=None):
    n = x.shape[-1] if n is None else n
    ms = jnp.sum(x * x, axis=-1, keepdims=True) * (1.0 / n)
    return x * lax.rsqrt(ms + NORM_EPS) * g


def _norm_mod(x, g, shift, scale):
    return _rms(x, g) * (1.0 + scale) + shift


def _silu(x):
    return x / (1.0 + jnp.exp(-x))


def _adaln_body(c_ref, w_ref, b_ref, o_ref):
    o_ref[...] = _dot_f32(_silu(c_ref[...]), w_ref[...]) + b_ref[...]


def _adaln(cond8, w_mod, b_mod):
    d, n = w_mod.shape
    tn = n // 4
    out = pl.pallas_call(
        _adaln_body,
        grid=(n // tn,),
        in_specs=[pl.BlockSpec((SUBLANES, d), lambda j: (0, 0)),
                  pl.BlockSpec((d, tn), lambda j: (0, j)),
                  pl.BlockSpec((1, tn), lambda j: (0, j))],
        out_specs=pl.BlockSpec((SUBLANES, tn), lambda j: (0, j)),
        out_shape=jax.ShapeDtypeStruct((SUBLANES, n), F32),
        compiler_params=_params("arbitrary"),
        name="adaln",
    )(cond8, w_mod, b_mod.reshape(1, n))
    return out.reshape(SUBLANES, 6, d)


def _two_part_specs(a, b, tm):
    na = a.shape[0] // tm
    cols = a.shape[1]
    return (pl.BlockSpec((tm, cols), lambda i: (jnp.minimum(i, na - 1), 0)),
            pl.BlockSpec((tm, cols), lambda i: (jnp.maximum(i - na, 0), 0)), na)


def _hy_pre_body(z_ref, zp_ref, zn_ref, w_ref, b_ref, u_ref, x0_ref, *, tm, lat_tiles, tiles_per_seq):
    i = pl.program_id(0)
    z = z_ref[...]
    in_lat = i < lat_tiles
    has_prev = jnp.logical_and(in_lat, i % tiles_per_seq != 0)
    has_next = jnp.logical_and(in_lat, i % tiles_per_seq != tiles_per_seq - 1)
    prev_row = jnp.where(has_prev, zp_ref[SUBLANES - 1:SUBLANES, :], 0.0)
    next_row = jnp.where(has_next, zn_ref[0:1, :], 0.0)
    rows = lax.broadcasted_iota(jnp.int32, z.shape, 0)
    z_m = jnp.where(rows == 0, prev_row, pltpu.roll(z, 1, 0))
    z_p = jnp.where(rows == tm - 1, next_row, pltpu.roll(z, tm - 1, 0))
    zc = b_ref[...] + z_m * w_ref[0:1, :] + z * w_ref[1:2, :] + z_p * w_ref[2:3, :]
    c = HY_WIDTH
    x0_ref[...] = zc[:, :c]
    u_ref[...] = zc[:, 2 * c:] * zc[:, c:2 * c]


def _hy_pre(z, conv_w, conv_b, *, t_lat, lat_seq, ctx_seq):
    t = z.shape[0]
    tm = ctx_seq
    c3 = 3 * HY_WIDTH
    nb8 = t // SUBLANES
    body = functools.partial(_hy_pre_body, tm=tm, lat_tiles=t_lat // tm, tiles_per_seq=lat_seq // tm)
    return pl.pallas_call(
        body,
        grid=(t // tm,),
        in_specs=[pl.BlockSpec((tm, c3), lambda i: (i, 0)),
                  pl.BlockSpec((SUBLANES, c3), lambda i: (jnp.maximum(i * (tm // SUBLANES) - 1, 0), 0)),
                  pl.BlockSpec((SUBLANES, c3), lambda i: (jnp.minimum((i + 1) * (tm // SUBLANES), nb8 - 1), 0)),
                  pl.BlockSpec((3, c3), lambda i: (0, 0)),
                  pl.BlockSpec((1, c3), lambda i: (0, 0))],
        out_specs=[pl.BlockSpec((tm, HY_WIDTH), lambda i: (i, 0)),
                   pl.BlockSpec((tm, HY_WIDTH), lambda i: (i, 0))],
        out_shape=[jax.ShapeDtypeStruct((t, HY_WIDTH), F32),
                   jax.ShapeDtypeStruct((t, HY_WIDTH), F32)],
        compiler_params=_params("parallel"),
        name="hy_pre",
    )(z, z, z, conv_w, conv_b.reshape(1, c3))


def _filter_embedding(seq):
    t01 = np.linspace(0.0, 1.0, seq)[:, None]
    w = 2.0 * math.pi * np.arange(seq)[:, None] / seq
    f = np.linspace(1e-4, HY_BANDS - 1, HY_BANDS)[None, :]
    z = np.concatenate([t01, np.cos(f * w), -np.sin(f * w)], axis=-1)
    z_rev = np.concatenate([z[:1], z[:0:-1]], axis=0)
    zz = np.concatenate([z, z_rev], axis=0)
    out = np.zeros((2 * seq, LANES), np.float32)
    out[:, :zz.shape[1]] = zz
    return out


def _filter_body(zz_ref, dl_ref, w1_ref, b1_ref, f1_ref, w2_ref, b2_ref, f2_ref, w3_ref,
                 k_ref, s_ref, *, tm, lat_tiles, ctx_tiles):
    i = pl.program_id(0)
    zz = zz_ref[...]
    h = jnp.sin(f1_ref[...] * (_dot_f32(zz, w1_ref[...]) + b1_ref[...]))
    h = jnp.sin(f2_ref[...] * (_dot_f32(h, w2_ref[...]) + b2_ref[...]))
    h = _dot_f32(h, w3_ref[...])
    is_bwd = jnp.logical_or(jnp.logical_and(i >= lat_tiles // 2, i < lat_tiles),
                            i >= lat_tiles + ctx_tiles // 2)
    first_bwd = jnp.logical_or(i == lat_tiles // 2, i == lat_tiles + ctx_tiles // 2)
    window = jnp.exp(-zz[:, 0:1] * dl_ref[...])
    k = jnp.where(is_bwd, h[:, HY_WIDTH:], h[:, :HY_WIDTH]) * window
    rows = lax.broadcasted_iota(jnp.int32, k.shape, 0)
    k = jnp.where(jnp.logical_and(first_bwd, rows == 0), 0.0, k)
    k_ref[...] = k
    s = jnp.sum(jnp.abs(k), axis=0, keepdims=True)

    @pl.when(i == 0)
    def _():
        s_ref[...] = jnp.zeros_like(s_ref)

    @pl.when(i < lat_tiles)
    def _():
        s_ref[0:1, :] += s

    @pl.when(i >= lat_tiles)
    def _():
        s_ref[1:2, :] += s


def _hyena_filters(p, *, lat_seq, ctx_seq):
    tm = ctx_seq
    zz = jnp.asarray(np.concatenate([_filter_embedding(lat_seq), _filter_embedding(ctx_seq)], axis=0))
    rows = zz.shape[0]
    max_decay = math.log(HY_DECAY_TARGET) / HY_FAST_DECAY_PCT
    min_decay = math.log(HY_DECAY_TARGET) / HY_SLOW_DECAY_PCT
    deltas = jnp.asarray(np.abs(np.linspace(min_decay, max_decay, HY_WIDTH))[None, :].astype(np.float32))
    emb, hid = p['hy_fw1'].shape

    def pad2(a, r, c):
        return jnp.zeros((r, c), F32).at[:a.shape[0], :a.shape[1]].set(a)

    w1 = pad2(p['hy_fw1'], LANES, LANES)
    b1 = pad2(p['hy_fb1'][None, :], 1, LANES)
    f1 = pad2(p['hy_freq1'][None, :], 1, LANES)
    w2 = pad2(p['hy_fw2'], LANES, LANES)
    b2 = pad2(p['hy_fb2'][None, :], 1, LANES)
    f2 = pad2(p['hy_freq2'][None, :], 1, LANES)
    w3 = pad2(p['hy_fw3'], LANES, 2 * HY_WIDTH)
    body = functools.partial(_filter_body, tm=tm, lat_tiles=2 * lat_seq // tm, ctx_tiles=2 * ctx_seq // tm)
    full = lambda i: (0, 0)
    return pl.pallas_call(
        body,
        grid=(rows // tm,),
        in_specs=[pl.BlockSpec((tm, LANES), lambda i: (i, 0)),
                  pl.BlockSpec((1, HY_WIDTH), full),
                  pl.BlockSpec((LANES, LANES), full), pl.BlockSpec((1, LANES), full), pl.BlockSpec((1, LANES), full),
                  pl.BlockSpec((LANES, LANES), full), pl.BlockSpec((1, LANES), full), pl.BlockSpec((1, LANES), full),
                  pl.BlockSpec((LANES, 2 * HY_WIDTH), full)],
        out_specs=[pl.BlockSpec((tm, HY_WIDTH), lambda i: (i, 0)),
                   pl.BlockSpec((SUBLANES, HY_WIDTH), full)],
        out_shape=[jax.ShapeDtypeStruct((rows, HY_WIDTH), F32),
                   jax.ShapeDtypeStruct((SUBLANES, HY_WIDTH), F32)],
        compiler_params=_params("arbitrary"),
        name="hy_filter",
    )(zz, deltas, w1, b1, f1, w2, b2, f2, w3)


def _stack_complex(z):
    return np.block([[z.real, -z.imag], [z.imag, z.real]])


def _dft_consts_two_level(seq, n1, n2):
    n = 2 * seq
    assert n1 * n2 == n
    a1 = np.arange(n1)
    f1_full = np.exp(-2j * np.pi * np.outer(a1, a1) / n1)
    f1_u = np.concatenate([f1_full.real, f1_full.imag], axis=0)[:, :n1 // 2]
    f1_k = np.concatenate([f1_full.real, f1_full.imag], axis=0)
    a2 = np.arange(n2)
    f = a1[:, None, None] + n1 * a2[None, :, None]
    z = np.exp(-2j * np.pi * (f * a2[None, None, :]) / n)
    mf = np.stack([_stack_complex(z[i]) for i in range(n1)])
    mi = np.stack([_stack_complex(np.conj(z[i]).T) for i in range(n1)])
    g = np.exp(2j * np.pi * np.outer(a1[:n1 // 2], a1) / n1) / n
    g1 = np.concatenate([_kron_rows(g.real), _kron_rows(-g.imag)], axis=1)
    as32 = lambda a: jnp.asarray(a.astype(np.float32))
    return as32(_kron_rows(f1_u)), as32(_kron_rows(f1_k)), as32(mf), as32(mi), as32(g1)


def _dft_consts_one_level(seq):
    n = 2 * seq
    a = np.arange(n)
    z = np.exp(-2j * np.pi * np.outer(a, a) / n)
    mf = np.concatenate([z.real, z.imag], axis=0)
    zi = np.exp(2j * np.pi * np.outer(a[:seq], a) / n) / n
    mi = np.concatenate([zi.real, -zi.imag], axis=1)
    as32 = lambda a: jnp.asarray(a.astype(np.float32))
    return as32(mf), as32(mi)


HY_ROWS = 16


def _kron_rows(m):
    return np.kron(m, np.eye(HY_ROWS))


def _dft1_body(m_ref, x_ref, sc_ref, or_ref, oi_ref):
    k, r, c = x_ref.shape
    x = (x_ref[...] * (1.0 / sc_ref[...])).reshape(k * r, c).astype(BF16)
    o = _dot(m_ref[...], x)
    h = o.shape[0] // 2
    or_ref[...] = o[:h].reshape(or_ref.shape).astype(BF16)
    oi_ref[...] = o[h:].reshape(oi_ref.shape).astype(BF16)


def _dft1(m, x, scale_row, *, n1, groups):
    _, k, n2, c = x.shape
    g = groups
    ospec = pl.BlockSpec((None, n1, HY_ROWS, c), lambda b, j: (b, 0, j, 0))
    return pl.pallas_call(
        _dft1_body,
        grid=(g, n2 // HY_ROWS),
        in_specs=[pl.BlockSpec(m.shape, lambda b, j: (0, 0)),
                  pl.BlockSpec((None, k, HY_ROWS, c), lambda b, j: (b, 0, j, 0)),
                  pl.BlockSpec((1, c), lambda b, j: (0, 0))],
        out_specs=[ospec, ospec],
        out_shape=[jax.ShapeDtypeStruct((g, n1, n2, c), BF16)] * 2,
        compiler_params=_params("parallel", "parallel"),
        name="hy_dft1",
    )(m, x, scale_row)


SPEC_GROUP = 4


def _spec_fwd_body(mf_ref, ar_ref, ai_ref, kr_ref, ki_ref):
    for g in range(SPEC_GROUP):
        a = jnp.concatenate([ar_ref[g], ai_ref[g]], axis=0)
        x = _dot(mf_ref[g], a)
        h = x.shape[0] // 2
        kr_ref[g] = x[:h]
        ki_ref[g] = x[h:]


def _spec_fwd(mf, ar, ai):
    n1, n2, c = ar.shape
    spec = pl.BlockSpec((SPEC_GROUP, n2, c), lambda i: (i, 0, 0))
    return pl.pallas_call(
        _spec_fwd_body,
        grid=(n1 // SPEC_GROUP,),
        in_specs=[pl.BlockSpec((SPEC_GROUP, 2 * n2, 2 * n2), lambda i: (i, 0, 0)), spec, spec],
        out_specs=[spec, spec],
        out_shape=[jax.ShapeDtypeStruct((n1, n2, c), F32)] * 2,
        compiler_params=_params("parallel"),
        name="hy_spec_filter",
    )(mf, ar, ai)


def _spec_mul_body(mf_ref, mi_ref, kr_ref, ki_ref, ar_ref, ai_ref, br_ref, bi_ref):
    for g in range(SPEC_GROUP):
        a = jnp.concatenate([ar_ref[g], ai_ref[g]], axis=0)
        x = _dot(mf_ref[g], a)
        h = x.shape[0] // 2
        xr, xi = x[:h], x[h:]
        kr, ki = kr_ref[g], ki_ref[g]
        y = jnp.concatenate([xr * kr - xi * ki, xr * ki + xi * kr], axis=0).astype(BF16)
        b = _dot(mi_ref[g], y)
        br_ref[g] = b[:h].astype(BF16)
        bi_ref[g] = b[h:].astype(BF16)


def _spec_mul(mf, mi, kr, ki, ar, ai):
    nb, n1, n2, c = ar.shape
    mspec = pl.BlockSpec((SPEC_GROUP, 2 * n2, 2 * n2), lambda i, b: (i, 0, 0))
    kspec = pl.BlockSpec((SPEC_GROUP, n2, c), lambda i, b: (i, 0, 0))
    aspec = pl.BlockSpec((None, SPEC_GROUP, n2, c), lambda i, b: (b, i, 0, 0))
    return pl.pallas_call(
        _spec_mul_body,
        grid=(n1 // SPEC_GROUP, nb),
        in_specs=[mspec, mspec, kspec, kspec, aspec, aspec],
        out_specs=[aspec, aspec],
        out_shape=[jax.ShapeDtypeStruct((nb, n1, n2, c), BF16)] * 2,
        compiler_params=_params("parallel", "arbitrary"),
        name="hy_spec_mul",
    )(mf, mi, kr, ki, ar, ai)


def _idft1_body(m_ref, br_ref, bi_ref, o_ref):
    n1, r, c = br_ref.shape
    b = jnp.concatenate([br_ref[...].reshape(n1 * r, c), bi_ref[...].reshape(n1 * r, c)], axis=0)
    o_ref[...] = _dot(m_ref[...], b).reshape(o_ref.shape)


def _idft1(m, br, bi, *, n_out):
    nb, n1, n2, c = br.shape
    bspec = pl.BlockSpec((None, n1, HY_ROWS, c), lambda b, j: (b, 0, j, 0))
    return pl.pallas_call(
        _idft1_body,
        grid=(nb, n2 // HY_ROWS),
        in_specs=[pl.BlockSpec(m.shape, lambda b, j: (0, 0)), bspec, bspec],
        out_specs=pl.BlockSpec((None, n_out, HY_ROWS, c), lambda b, j: (b, 0, j, 0)),
        out_shape=jax.ShapeDtypeStruct((nb, n_out, n2, c), F32),
        compiler_params=_params("parallel", "parallel"),
        name="hy_idft1",
    )(m, br, bi)


def _ctx_filter_body(mf_ref, k_ref, sc_ref, kf_ref):
    kf_ref[...] = _dot(mf_ref[...], (k_ref[...] * (1.0 / sc_ref[...])).astype(BF16))


def _ctx_conv_body(mf_ref, mi_ref, kf_ref, u_ref, o_ref):
    x = _dot(mf_ref[...], u_ref[...].astype(BF16))
    h = x.shape[0] // 2
    xr, xi = x[:h], x[h:]
    kr, ki = kf_ref[:h, :], kf_ref[h:, :]
    y = jnp.concatenate([xr * kr - xi * ki, xr * ki + xi * kr], axis=0).astype(BF16)
    o_ref[...] = _dot(mi_ref[...], y)


def _hyena_ctx(k_raw, k_norm1, u, *, seq, nseq, u_row0):
    mf, mi = _dft_consts_one_level(seq)
    n = 2 * seq
    c = u.shape[1]
    kf = pl.pallas_call(
        _ctx_filter_body,
        out_shape=jax.ShapeDtypeStruct((2 * n, c), F32),
        compiler_params=_params(),
        name="hy_ctx_filter",
    )(mf.astype(BF16), k_raw, k_norm1)
    full = lambda s: (0, 0)
    return pl.pallas_call(
        _ctx_conv_body,
        grid=(nseq,),
        in_specs=[pl.BlockSpec((2 * n, seq), full),
                  pl.BlockSpec((seq, 2 * n), full),
                  pl.BlockSpec((2 * n, c), full),
                  pl.BlockSpec((seq, c), lambda s: (u_row0 // seq + s, 0))],
        out_specs=pl.BlockSpec((seq, c), lambda s: (s, 0)),
        out_shape=jax.ShapeDtypeStruct((nseq * seq, c), F32),
        compiler_params=_params("parallel"),
        name="hy_ctx_conv",
    )(mf[:, :seq].astype(BF16), mi.astype(BF16), kf, u)


def _hyena_lat(k_raw, k_norm1, u, *, seq, nb):
    c = u.shape[1]
    n1, n2 = 64, 2 * seq // 64
    f1_u, f1_k, mf, mi, g1 = _dft_consts_two_level(seq, n1, n2)
    mf = mf.astype(BF16)
    mi = mi.astype(BF16)
    ones_row = jnp.ones((1, c), F32)
    akr, aki = _dft1(f1_k.astype(BF16), k_raw.reshape(-1, n1, n2, c), k_norm1, n1=n1, groups=1)
    kr, ki = _spec_fwd(mf, akr[0], aki[0])
    ar, ai = _dft1(f1_u.astype(BF16), u.reshape(-1, n1 // 2, n2, c), ones_row, n1=n1, groups=nb)
    br, bi = _spec_mul(mf, mi, kr, ki, ar, ai)
    y = _idft1(g1.astype(BF16), br, bi, n_out=n1 // 2)
    return y.reshape(nb * seq, c)


def _rope_tables(seq, rope_dims, lane_offsets, pad_rows):
    rows = seq // GRID_W
    rr, cc = np.meshgrid(np.arange(rows), np.arange(GRID_W), indexing='ij')
    pos = (rr.reshape(-1).astype(np.float64), cc.reshape(-1).astype(np.float64))
    half = rope_dims // 2
    q = half // 2
    inv_freq = ROPE_BASE ** (-np.arange(0, half, 2, dtype=np.float64) / half)
    cos_t = np.ones((seq + pad_rows, LANES), np.float64)
    sin_a = np.zeros((seq + pad_rows, LANES), np.float64)
    sin_b = np.zeros((seq + pad_rows, LANES), np.float64)
    for off in lane_offsets:
        for axis in range(2):
            ang = pos[axis][:, None] * inv_freq[None, :]
            base = off + axis * half
            cos_t[:seq, base:base + q] = np.cos(ang)
            cos_t[:seq, base + q:base + half] = np.cos(ang)
            sin_b[:seq, base:base + q] = -np.sin(ang)
            sin_a[:seq, base + q:base + half] = np.sin(ang)
    as32 = lambda a: jnp.asarray(a.astype(np.float32))
    return as32(cos_t), as32(sin_a), as32(sin_b)


def _rope(x, cos_t, sin_a, sin_b, shift):
    return x * cos_t + pltpu.roll(x, shift, 1) * sin_a + pltpu.roll(x, LANES - shift, 1) * sin_b


def _even_in_body(xa_ref, xb_ref, mod_ref, g_ref, w_ref, qa_ref, kva_ref, wuq_ref, qn_ref,
                  cos_ref, sa_ref, sb_ref, zhy_ref, kr_ref, q_ref, ckvn_ref, zq_ref, *, na, scale):
    i = pl.program_id(0)
    n = pl.num_programs(0) - 1

    @pl.when(i == 0)
    def _():
        zq_ref[1] = jnp.zeros(zq_ref.shape[1:], F32)

    c_q = MLA_Q_RANK
    zp = zq_ref[(i + 1) % 2]
    ckvn_ref[...] = _rms(zp[:, c_q:], kva_ref[...])
    cqn = _rms(zp[:, :c_q], qa_ref[...])
    q = _dot(cqn.astype(BF16), wuq_ref[...])
    cos_t, sin_a, sin_b = cos_ref[...], sa_ref[...], sb_ref[...]
    g = qn_ref[...]
    for hh in range(MLA_HEADS):
        sl = slice(hh * LANES, (hh + 1) * LANES)
        qh = _rope(_rms(q[:, sl], g, MLA_QK), cos_t, sin_a, sin_b, MLA_ROPE // 4)
        q_ref[:, sl] = (qh * scale).astype(BF16)

    x = jnp.where(jnp.minimum(i, n - 1) < na, xa_ref[...], xb_ref[...])
    h = _norm_mod(x, g_ref[...], mod_ref[0:1, :], mod_ref[1:2, :])
    z = _dot(h.astype(BF16), w_ref[...])
    c_hy = 3 * HY_WIDTH
    c_kv = c_hy + MLA_Q_RANK + MLA_KV_RANK
    zhy_ref[...] = z[:, :c_hy]
    kr_ref[...] = z[:, c_kv:]
    zq_ref[i % 2] = z[:, c_hy:c_kv]


def _pad_heads(w, heads, width):
    k = w.shape[0]
    w3 = w.reshape(k, heads, width)
    return jnp.zeros((k, heads, LANES), w.dtype).at[:, :, :width].set(w3).reshape(k, heads * LANES)


def _even_in(xa, xb, mod, p, tables, *, tm, seg_len, t_lat, lat_seq):
    d = xa.shape[1]
    t = xa.shape[0] + xb.shape[0]
    n_in = p['w_in'].shape[1]
    c_kv = 3 * HY_WIDTH + MLA_Q_RANK + MLA_KV_RANK
    assert c_kv % LANES == 0 and n_in - c_kv == MLA_ROPE
    n_pad = c_kv + LANES
    w_in = jnp.zeros((d, n_pad), BF16).at[:, :n_in].set(p['w_in'].astype(BF16))
    wuq = _pad_heads(p['w_uq'], MLA_HEADS, MLA_QK).astype(BF16)
    cos_t, sin_a, sin_b = tables
    qn = jnp.zeros((1, LANES), F32).at[0, :MLA_QK].set(p['q_norm'])
    pos_blocks = lat_seq // tm
    lat_tiles = t_lat // tm
    n = t // tm
    na = xa.shape[0] // tm
    cur = lambda i: jnp.minimum(i, n - 1)
    prev = lambda i: jnp.maximum(i - 1, 0)

    def tmap(i):
        j = prev(i)
        return (jnp.where(j < lat_tiles, j % pos_blocks, pos_blocks), 0)

    tspec = pl.BlockSpec((tm, LANES), tmap)
    full = lambda i: (0, 0)
    row_cur = lambda cols: pl.BlockSpec((tm, cols), lambda i: (cur(i), 0))
    row_prev = lambda cols: pl.BlockSpec((tm, cols), lambda i: (prev(i), 0))
    return pl.pallas_call(
        functools.partial(_even_in_body, na=na, scale=MLA_QK ** -0.5 * LOG2_E),
        grid=(n + 1,),
        in_specs=[pl.BlockSpec((tm, d), lambda i: (jnp.minimum(cur(i), na - 1), 0)),
                  pl.BlockSpec((tm, d), lambda i: (jnp.maximum(cur(i) - na, 0), 0)),
                  pl.BlockSpec((None, 6, d), lambda i: (cur(i) * tm // seg_len, 0, 0)),
                  pl.BlockSpec((1, d), full),
                  pl.BlockSpec((d, n_pad), full),
                  pl.BlockSpec((1, MLA_Q_RANK), full),
                  pl.BlockSpec((1, MLA_KV_RANK), full),
                  pl.BlockSpec((MLA_Q_RANK, MLA_HEADS * LANES), full),
                  pl.BlockSpec((1, LANES), full),
                  tspec, tspec, tspec],
        out_specs=[row_cur(3 * HY_WIDTH), row_cur(LANES), row_prev(MLA_HEADS * LANES), row_prev(MLA_KV_RANK)],
        out_shape=[jax.ShapeDtypeStruct((t, 3 * HY_WIDTH), F32),
                   jax.ShapeDtypeStruct((t, LANES), F32),
                   jax.ShapeDtypeStruct((t, MLA_HEADS * LANES), BF16),
                   jax.ShapeDtypeStruct((t, MLA_KV_RANK), F32)],
        scratch_shapes=[pltpu.VMEM((2, tm, MLA_Q_RANK + MLA_KV_RANK), F32)],
        compiler_params=_params("arbitrary"),
        name="even_in",
    )(xa, xb, mod, p['norm1'].reshape(1, d), w_in, p['qa_norm'].reshape(1, -1), p['kva_norm'].reshape(1, -1),
      wuq, qn, cos_t, sin_a, sin_b)


def _mla_kv_body(ckvn_ref, kr_ref, wk_ref, wv_ref, kn_ref, cos_ref, sa_ref, sb_ref, k_ref, v_ref):
    c = ckvn_ref[...].astype(BF16)
    k = _dot(c, wk_ref[...])
    v_ref[...] = _dot(c, wv_ref[...]).astype(BF16)
    kr = pltpu.roll(kr_ref[...], MLA_NOPE, 1)
    cos_t, sin_a, sin_b = cos_ref[...], sa_ref[...], sb_ref[...]
    g = kn_ref[...]
    for h in range(MLA_HEADS):
        sl = slice(h * LANES, (h + 1) * LANES)
        kh = _rope(_rms(k[:, sl] + kr, g, MLA_QK), cos_t, sin_a, sin_b, MLA_ROPE // 4)
        k_ref[:, sl] = kh.astype(BF16)


def _mla_kv(ckvn_rows, kr_rows, p, tables, *, tm, nb, past, lat_seq):
    r = ckvn_rows.shape[0]
    w = p['w_ukv'].reshape(MLA_KV_RANK, MLA_HEADS, MLA_NOPE + MLA_V)
    wk = _pad_heads(w[:, :, :MLA_NOPE].reshape(MLA_KV_RANK, -1), MLA_HEADS, MLA_NOPE).astype(BF16)
    wv = w[:, :, MLA_NOPE:].reshape(MLA_KV_RANK, MLA_HEADS * MLA_V).astype(BF16)
    cos_t, sin_a, sin_b = tables
    kn = jnp.zeros((1, LANES), F32).at[0, :MLA_QK].set(p['k_norm'])
    per_b = (past + lat_seq) // tm
    past_tiles = past // tm
    pos_blocks = lat_seq // tm
    lat_tiles = nb * per_b

    def tmap(i):
        j = i % per_b
        is_pos = jnp.logical_and(i < lat_tiles, j >= past_tiles)
        return (jnp.where(is_pos, j - past_tiles, pos_blocks), 0)

    tspec = pl.BlockSpec((tm, LANES), tmap)
    full = lambda i: (0, 0)
    return pl.pallas_call(
        _mla_kv_body,
        grid=(r // tm,),
        in_specs=[pl.BlockSpec((tm, MLA_KV_RANK), lambda i: (i, 0)),
                  pl.BlockSpec((tm, LANES), lambda i: (i, 0)),
                  pl.BlockSpec((MLA_KV_RANK, MLA_HEADS * LANES), full),
                  pl.BlockSpec((MLA_KV_RANK, MLA_HEADS * MLA_V), full),
                  pl.BlockSpec((1, LANES), full),
                  tspec, tspec, tspec],
        out_specs=[pl.BlockSpec((tm, MLA_HEADS * LANES), lambda i: (i, 0)),
                   pl.BlockSpec((tm, MLA_HEADS * MLA_V), lambda i: (i, 0))],
        out_shape=[jax.ShapeDtypeStruct((r, MLA_HEADS * LANES), BF16),
                   jax.ShapeDtypeStruct((r, MLA_HEADS * MLA_V), BF16)],
        compiler_params=_params("parallel"),
        name="mla_kv",
    )(ckvn_rows, kr_rows, wk, wv, kn, cos_t, sin_a, sin_b)


ATT_CHUNK = 512
ATT_UNIT_ROWS = 256
ATT_TQ = 1024


def _fill_vaug(vaug_ref, g, v_blocks):
    off = 0
    for v in v_blocks:
        n = v.shape[0]
        vaug_ref[g, off:off + n, :LANES] = v
        off += n
    vaug_ref[g, :, LANES:] = jnp.ones((vaug_ref.shape[1], LANES), BF16)


def _softmax_pv(units, s_ref, n_keys):
    chunk = min(ATT_CHUNK, n_keys)
    chunks = [slice(c * chunk, (c + 1) * chunk) for c in range(n_keys // chunk)]

    def scores(u, rows, m_lane):
        s = _dot_nt(units[u][0], units[u][1](rows))
        s_ref[u % 2, :, rows] = s
        for j in range(chunk // LANES):
            blk = s[:, j * LANES:(j + 1) * LANES]
            m_lane = blk if m_lane is None else jnp.maximum(m_lane, blk)
        return m_lane

    def values(u, rows, m, acc):
        p = jnp.exp2(s_ref[u % 2, :, rows] - m).astype(BF16)
        d = _dot(p, units[u][2](rows))
        return d if acc is None else acc + d

    outs = []
    m_lane = None
    for rows in chunks:
        m_lane = scores(0, rows, m_lane)
    for u in range(len(units)):
        m = jnp.max(m_lane, axis=-1, keepdims=True)
        acc, m_lane = None, None
        for rows in chunks:
            acc = values(u, rows, m, acc)
            if u + 1 < len(units):
                m_lane = scores(u + 1, rows, m_lane)
        outs.append(acc)
    return outs


def _mla_attn_body(prev_ref, q_ref, k_ref, v_ref, o_ref, vaug_ref, s_ref):
    pairs = vaug_ref.shape[0]

    @pl.when(pl.program_id(2) == 0)
    def _():
        for g in range(pairs):
            _fill_vaug(vaug_ref, g, [v_ref[:, g * LANES:(g + 1) * LANES]])

    n_keys = k_ref.shape[0]
    tq = q_ref.shape[0]
    ur = s_ref.shape[1]
    units, slots = [], []
    for r0 in range(0, tq, ur):
        for g in range(pairs):
            slots.append((r0, g))
            for hh in range(2):
                sl = slice((2 * g + hh) * LANES, (2 * g + hh + 1) * LANES)
                units.append((q_ref[r0:r0 + ur, sl], lambda rows, sl=sl: k_ref[rows, sl],
                              lambda rows, g=g: vaug_ref[g, rows, :]))
    res = [r[:, :LANES] / r[:, LANES:] for r in _softmax_pv(units, s_ref, n_keys)]
    lane = lax.broadcasted_iota(jnp.int32, res[0].shape, 1)
    for i, (r0, g) in enumerate(slots):
        o_ref[r0:r0 + ur, g * LANES:(g + 1) * LANES] = jnp.where(
            lane < MLA_V, res[2 * i], res[2 * i + 1]).astype(BF16)


def _mla_attn(prev, q, k, v, *, tq, pairs, n_seq, seq_q, seq_k, q_row0, k_row0):
    hp = MLA_HEADS // 2 // pairs
    nq = seq_q // tq
    qb0, kb0 = q_row0 // tq, k_row0 // seq_k
    return pl.pallas_call(
        _mla_attn_body,
        grid=(n_seq, hp, nq),
        in_specs=[pl.BlockSpec(memory_space=pl.ANY),
                  pl.BlockSpec((tq, pairs * 2 * LANES), lambda s, h, i: (qb0 + s * nq + i, h)),
                  pl.BlockSpec((seq_k, pairs * 2 * LANES), lambda s, h, i: (kb0 + s, h)),
                  pl.BlockSpec((seq_k, pairs * 2 * MLA_V), lambda s, h, i: (kb0 + s, h))],
        out_specs=pl.BlockSpec((tq, pairs * 2 * MLA_V), lambda s, h, i: (qb0 + s * nq + i, h)),
        out_shape=jax.ShapeDtypeStruct(prev.shape, prev.dtype),
        input_output_aliases={0: 0},
        scratch_shapes=[pltpu.VMEM((pairs, seq_k, 2 * LANES), BF16),
                        pltpu.VMEM((2, min(tq, ATT_UNIT_ROWS), seq_k), F32)],
        compiler_params=_params("arbitrary", "arbitrary", "arbitrary"),
        name="mla_attn",
    )(prev, q, k, v)


def _diff_attn_body(prev_ref, *refs, n_seg, lambda_init):
    q_ref, lam_ref, sub_ref = refs[0], refs[1], refs[2]
    k_refs = refs[3:3 + n_seg]
    v_refs = refs[3 + n_seg:3 + 2 * n_seg]
    o_ref, kcat_ref, vaug_ref, s_ref = refs[3 + 2 * n_seg:]
    heads = vaug_ref.shape[0]

    def head_block(ref, g):
        return ref[g] if len(ref.shape) == 3 else ref[:, g * LANES:(g + 1) * LANES]

    @pl.when(pl.program_id(2) == 0)
    def _():
        for g in range(heads):
            _fill_vaug(vaug_ref, g, [head_block(v, g) for v in v_refs])
            off = 0
            for k in k_refs:
                kcat_ref[g, off:off + k.shape[-2], :] = head_block(k, g)
                off += k.shape[-2]

    lp = lam_ref[...]
    lam = (jnp.exp(jnp.sum(lp[0:1] * lp[1:2], axis=-1, keepdims=True))
           - jnp.exp(jnp.sum(lp[2:3] * lp[3:4], axis=-1, keepdims=True)) + lambda_init)
    n_keys = kcat_ref.shape[1]
    tq = q_ref.shape[0]
    ur = s_ref.shape[1]
    units, slots = [], []
    for r0 in range(0, tq, ur):
        for g in range(heads):
            slots.append((r0, g))
            q = q_ref[r0:r0 + ur, g * LANES:(g + 1) * LANES].astype(F32)
            lane = lax.broadcasted_iota(jnp.int32, q.shape, 1)
            k_of = lambda rows, g=g: kcat_ref[g, rows, :]
            v_of = lambda rows, g=g: vaug_ref[g, rows, :]
            units.append((jnp.where(lane < DIFF_DH, q, 0.0).astype(BF16), k_of, v_of))
            units.append((jnp.where(lane < DIFF_DH, 0.0, q).astype(BF16), k_of, v_of))
    res = _softmax_pv(units, s_ref, n_keys)
    for i, (r0, g) in enumerate(slots):
        r1, r2 = res[2 * i], res[2 * i + 1]
        o = r1[:, :LANES] / r1[:, LANES:] - (lam / r2[:, LANES:]) * r2[:, :LANES]
        o_ref[r0:r0 + ur, g * LANES:(g + 1) * LANES] = (
            _rms(o, sub_ref[...]) * (1.0 - lambda_init)).astype(BF16)


def _diff_attn(prev, q, k_new, v_new, k_cache, v_cache, lam_p, subln, *, tq, heads, n_seq, seq_q, q_row0,
               lambda_init):
    nq = seq_q // tq
    qb0 = q_row0 // tq
    sb0 = q_row0 // seq_q
    d = 2 * DIFF_DH
    new_spec = pl.BlockSpec((seq_q, heads * d), lambda s, h, i: (sb0 + s, h))
    if k_cache is None:
        n_seg, k_args, v_args, k_specs, v_specs = 1, [k_new], [v_new], [new_spec], [new_spec]
        n_keys = seq_q
    else:
        past = k_cache.shape[2]
        c_spec = pl.BlockSpec((None, heads, past, d), lambda s, h, i: (s, h, 0, 0))
        n_seg, k_args, v_args = 2, [k_cache, k_new], [v_cache, v_new]
        k_specs, v_specs = [c_spec, new_spec], [c_spec, new_spec]
        n_keys = past + seq_q
    return pl.pallas_call(
        functools.partial(_diff_attn_body, n_seg=n_seg, lambda_init=lambda_init),
        grid=(n_seq, DIFF_HEADS // heads, nq),
        in_specs=[pl.BlockSpec(memory_space=pl.ANY),
                  pl.BlockSpec((tq, heads * d), lambda s, h, i: (qb0 + s * nq + i, h)),
                  pl.BlockSpec((4, DIFF_DH), lambda s, h, i: (0, 0)),
                  pl.BlockSpec((1, d), lambda s, h, i: (0, 0))] + k_specs + v_specs,
        out_specs=pl.BlockSpec((tq, heads * d), lambda s, h, i: (qb0 + s * nq + i, h)),
        out_shape=jax.ShapeDtypeStruct(prev.shape, prev.dtype),
        input_output_aliases={0: 0},
        scratch_shapes=[pltpu.VMEM((heads, n_keys, d), BF16), pltpu.VMEM((heads, n_keys, 2 * LANES), BF16),
                        pltpu.VMEM((2, min(tq, ATT_UNIT_ROWS), n_keys), F32)],
        compiler_params=_params("arbitrary", "arbitrary", "arbitrary"),
        name="diff_attn",
    )(prev, q, lam_p, subln.reshape(1, d), *k_args, *v_args)


def _even_out_body(xa_ref, xb_ref, mod_ref, ca_ref, cb_ref, u_ref, x0_ref, db_ref, o_ref, wa_ref, wb_ref,
                   out_ref, *, na):
    first = pl.program_id(0) < na
    x = jnp.where(first, xa_ref[...], xb_ref[...])
    conv = jnp.where(first, ca_ref[...], cb_ref[...])
    y_hy = (conv + u_ref[...] * db_ref[...]) * x0_ref[...]
    acc = _dot(y_hy.astype(BF16), wa_ref[...]) + _dot(o_ref[...], wb_ref[...])
    out_ref[...] = x + mod_ref[2:3, :] * acc


def _even_out(xa, xb, mod, conv_a, conv_b, u, x0, dbias, o, w_hy, w_att, *, tm, seg_len):
    d = xa.shape[1]
    t = xa.shape[0] + xb.shape[0]
    c = u.shape[1]
    xa_spec, xb_spec, na = _two_part_specs(xa, xb, tm)
    ca_spec, cb_spec, na_c = _two_part_specs(conv_a, conv_b, tm)
    assert na == na_c
    row = lambda cols: pl.BlockSpec((tm, cols), lambda i: (i, 0))
    full = lambda a: pl.BlockSpec(a.shape, lambda i: (0, 0))
    return pl.pallas_call(
        functools.partial(_even_out_body, na=na),
        grid=(t // tm,),
        in_specs=[xa_spec, xb_spec,
                  pl.BlockSpec((None, 6, d), lambda i: (i * tm // seg_len, 0, 0)),
                  ca_spec, cb_spec, row(c), row(c), full(dbias), row(o.shape[1]), full(w_hy), full(w_att)],
        out_specs=row(d),
        out_shape=jax.ShapeDtypeStruct((t, d), F32),
        compiler_params=_params("arbitrary"),
        name="even_out",
    )(xa, xb, mod, conv_a, conv_b, u, x0, dbias, o, w_hy, w_att)


def _ffn_body(x_ref, mod_ref, g_ref, wg_ref, wu_ref, wd_ref, o_ref, h_ref, acc_ref):
    f = pl.program_id(1)

    @pl.when(f == 0)
    def _():
        h_ref[...] = _norm_mod(x_ref[...], g_ref[...], mod_ref[3:4, :], mod_ref[4:5, :]).astype(BF16)
        acc_ref[...] = jnp.zeros_like(acc_ref)

    h = h_ref[...]
    a = _silu(_dot(h, wg_ref[...])) * _dot(h, wu_ref[...])
    acc_ref[...] += _dot(a.astype(BF16), wd_ref[...])

    @pl.when(f == pl.num_programs(1) - 1)
    def _():
        o_ref[...] = x_ref[...] + mod_ref[5:6, :] * acc_ref[...]


def _ffn(x, mod, g, wg, wu, wd, *, tm, tf, seg_len):
    t, d = x.shape
    ff = wg.shape[1]
    return pl.pallas_call(
        _ffn_body,
        grid=(t // tm, ff // tf),
        in_specs=[pl.BlockSpec((tm, d), lambda i, f: (i, 0)),
                  pl.BlockSpec((None, 6, d), lambda i, f: (i * tm // seg_len, 0, 0)),
                  pl.BlockSpec((1, d), lambda i, f: (0, 0)),
                  pl.BlockSpec((d, tf), lambda i, f: (0, f)),
                  pl.BlockSpec((d, tf), lambda i, f: (0, f)),
                  pl.BlockSpec((tf, d), lambda i, f: (f, 0))],
        out_specs=pl.BlockSpec((tm, d), lambda i, f: (i, 0)),
        out_shape=jax.ShapeDtypeStruct((t, d), F32),
        scratch_shapes=[pltpu.VMEM((tm, d), BF16), pltpu.VMEM((tm, d), F32)],
        compiler_params=_params("parallel", "arbitrary"),
        name="ffn",
    )(x, mod, g.reshape(1, d), wg, wu, wd)


def _group_ms(x, gmat):
    hi, lo = _split_bf16(x * x)
    return (_dot(hi, gmat) + _dot(lo, gmat)) * (1.0 / DIFF_DH)


def _qkv_body(x_ref, mod_ref, g_ref, w_ref, gm_ref, qn_ref, kn_ref, cos_ref, sa_ref, sb_ref,
              q_ref, k_ref, v_ref, kf_ref, vf_ref, *, scale, seq):
    h = _norm_mod(x_ref[...], g_ref[...], mod_ref[0:1, :], mod_ref[1:2, :]).astype(BF16)
    z = _dot(h, w_ref[...])
    hd = DIFF_HEADS * 2 * DIFF_DH
    tm = z.shape[0]
    cos_t, sin_a, sin_b = cos_ref[...], sa_ref[...], sb_ref[...]
    gm = gm_ref[...]
    shift = DIFF_DH // 4
    for hh in range(DIFF_HEADS):
        sl = slice(hh * LANES, (hh + 1) * LANES)
        qh = z[:, hh * LANES:(hh + 1) * LANES]
        qh = qh * lax.rsqrt(_group_ms(qh, gm) + NORM_EPS) * qn_ref[...]
        q_ref[:, sl] = (_rope(qh, cos_t, sin_a, sin_b, shift) * scale).astype(BF16)
        kh = z[:, hd + hh * LANES:hd + (hh + 1) * LANES]
        kh = kh * lax.rsqrt(_group_ms(kh, gm) + NORM_EPS) * kn_ref[...]
        k_ref[:, sl] = _rope(kh, cos_t, sin_a, sin_b, shift).astype(BF16)
        vh = z[:, 2 * hd + hh * LANES:2 * hd + (hh + 1) * LANES]
        for s in range(tm // seq):
            kf_ref[s, hh] = kh[s * seq:(s + 1) * seq, :]
            vf_ref[s, hh] = vh[s * seq:(s + 1) * seq, :]
    v_ref[...] = z[:, 2 * hd:].astype(BF16)


def _qkv(x, mod, p, tables, *, tm, seg_len, t_lat, lat_seq, n_ctx, ctx_seq):
    t, d = x.shape
    hd = DIFF_HEADS * 2 * DIFF_DH
    wqk = p['w_qkv'][:, :2 * hd].reshape(d, 2, 2, DIFF_HEADS, DIFF_DH)
    wqk = wqk.transpose(0, 1, 3, 2, 4).reshape(d, 2 * hd)
    w = jnp.concatenate([wqk, p['w_qkv'][:, 2 * hd:]], axis=1).astype(BF16)
    gi = np.arange(LANES) // DIFF_DH
    gmat = jnp.asarray((gi[:, None] == gi[None, :]).astype(np.float32)).astype(BF16)
    cos_t, sin_a, sin_b = tables
    qn = jnp.tile(p['q_norm'], 2).reshape(1, LANES)
    kn = jnp.tile(p['k_norm'], 2).reshape(1, LANES)
    pos_blocks = lat_seq // tm
    lat_tiles = t_lat // tm
    seq_per_tile = tm // ctx_seq
    tspec = pl.BlockSpec((tm, LANES), lambda i: (jnp.where(i < lat_tiles, i % pos_blocks, pos_blocks), 0))
    full = lambda i: (0, 0)
    row = pl.BlockSpec((tm, hd), lambda i: (i, 0))
    fspec = pl.BlockSpec((seq_per_tile, DIFF_HEADS, ctx_seq, LANES),
                         lambda i: (jnp.maximum(i - lat_tiles, 0), 0, 0, 0))
    fshape = jax.ShapeDtypeStruct((n_ctx, DIFF_HEADS, ctx_seq, LANES), F32)
    return pl.pallas_call(
        functools.partial(_qkv_body, scale=DIFF_DH ** -0.5 * LOG2_E, seq=ctx_seq),
        grid=(t // tm,),
        in_specs=[pl.BlockSpec((tm, d), lambda i: (i, 0)),
                  pl.BlockSpec((None, 6, d), lambda i: (i * tm // seg_len, 0, 0)),
                  pl.BlockSpec((1, d), full),
                  pl.BlockSpec((d, 3 * hd), full),
                  pl.BlockSpec((LANES, LANES), full),
                  pl.BlockSpec((1, LANES), full), pl.BlockSpec((1, LANES), full),
                  tspec, tspec, tspec],
        out_specs=[row, row, row, fspec, fspec],
        out_shape=[jax.ShapeDtypeStruct((t, hd), BF16)] * 3 + [fshape, fshape],
        compiler_params=_params("arbitrary"),
        name="qkv",
    )(x, mod, p['norm1'].reshape(1, d), w, gmat, qn, kn, cos_t, sin_a, sin_b)


def _route(logits):
    lane = lax.broadcasted_iota(jnp.int32, logits.shape, 1)
    neg = jnp.float32(-jnp.inf)
    lg = jnp.where(lane < N_EXPERTS, logits, neg)
    m1 = jnp.max(lg, axis=-1, keepdims=True)
    i1 = jnp.min(jnp.where(lg == m1, lane, LANES), axis=-1, keepdims=True)
    lg2 = jnp.where(lane == i1, neg, lg)
    m2 = jnp.max(lg2, axis=-1, keepdims=True)
    i2 = jnp.min(jnp.where(lg2 == m2, lane, LANES), axis=-1, keepdims=True)
    e = jnp.exp(m2 - m1)
    w1 = 1.0 / (1.0 + e)
    w2 = e / (1.0 + e)
    return jnp.where(lane == i1, w1, 0.0) + jnp.where(lane == i2, w2, 0.0)


MOE_BLOCK = 1024
MOE_SUB = 256
MOE_ROUTE_TM = 512
MOE_WINDOW = 5
MOE_TF = 512
MOE_CMB = 256
MOE_CMB_BLOCKS = MOE_SUB // MOE_CMB + 1


def _moe_route_body(x_ref, o_ref, wo_ref, mod_ref, g_ref, wr_ref, xo_ref, h_ref, gates_ref, rank_ref, rank_t_ref,
                    carry_row, carry_col, *, tm):
    i = pl.program_id(0)

    @pl.when(i == 0)
    def _():
        carry_row[...] = jnp.zeros_like(carry_row)
        carry_col[...] = jnp.zeros_like(carry_col)

    x = x_ref[...] + mod_ref[2:3, :] * _dot(o_ref[...], wo_ref[...])
    xo_ref[...] = x
    h = _norm_mod(x, g_ref[...], mod_ref[3:4, :], mod_ref[4:5, :])
    h_ref[...] = h.astype(BF16)
    gates = _route(_dot_f32(h, wr_ref[...]))
    gates_ref[...] = gates
    sel = jnp.where(gates != 0.0, 1.0, 0.0)
    sel_t = sel.T
    r = lax.broadcasted_iota(jnp.int32, (tm, tm), 0)
    c = lax.broadcasted_iota(jnp.int32, (tm, tm), 1)
    lower = jnp.where(c < r, 1.0, 0.0).astype(BF16)
    upper = jnp.where(r < c, 1.0, 0.0).astype(BF16)
    before = _dot(lower, sel.astype(BF16)) + carry_row[...]
    before_t = _dot(sel_t.astype(BF16), upper) + carry_col[...]
    rank_ref[...] = jnp.where(sel > 0.0, before, -1.0)
    rank_t = jnp.where(sel_t > 0.0, before_t, -1.0)
    for s in range(tm // MOE_SUB):
        rank_t_ref[s] = rank_t[:SUBLANES, s * MOE_SUB:(s + 1) * MOE_SUB]
    carry_row[...] += jnp.sum(sel, axis=0, keepdims=True)
    carry_col[...] += jnp.sum(sel_t, axis=1, keepdims=True)


def _moe_route(x, o, w_out, mod, g, w_router, *, seg_len):
    t, d = x.shape
    tm = MOE_ROUTE_TM
    ne = w_router.shape[1]
    assert ne <= SUBLANES
    wr = jnp.zeros((d, LANES), F32).at[:, :ne].set(w_router)
    sub = tm // MOE_SUB
    row = lambda cols: pl.BlockSpec((tm, cols), lambda i: (i, 0))
    full = lambda a: pl.BlockSpec(a.shape, lambda i: (0, 0))
    return pl.pallas_call(
        functools.partial(_moe_route_body, tm=tm),
        grid=(t // tm,),
        in_specs=[row(d), row(o.shape[1]), full(w_out),
                  pl.BlockSpec((None, 6, d), lambda i: (i * tm // seg_len, 0, 0)),
                  pl.BlockSpec((1, d), lambda i: (0, 0)),
                  full(wr)],
        out_specs=[row(d), row(d), row(LANES), row(LANES),
                   pl.BlockSpec((sub, SUBLANES, MOE_SUB), lambda i: (i, 0, 0))],
        out_shape=[jax.ShapeDtypeStruct((t, d), F32),
                   jax.ShapeDtypeStruct((t, d), BF16),
                   jax.ShapeDtypeStruct((t, LANES), F32),
                   jax.ShapeDtypeStruct((t, LANES), F32),
                   jax.ShapeDtypeStruct((t // MOE_SUB, SUBLANES, MOE_SUB), F32)],
        scratch_shapes=[pltpu.VMEM((1, LANES), F32), pltpu.VMEM((LANES, 1), F32)],
        compiler_params=_params("arbitrary"),
        name="moe_route",
    )(x, o, w_out, mod, g.reshape(1, d), wr)


def _moe_plan(rank, ne, *, n_blocks):
    t = rank.shape[0]
    n_tiles = t // MOE_SUB
    per_blk = MOE_BLOCK // MOE_SUB
    n_sub = n_blocks * per_blk
    sel = (rank[:, :ne] >= 0.0).astype(jnp.int32)
    tile_cnt = sel.reshape(n_tiles, MOE_SUB, ne).sum(axis=1)
    tile_end = jnp.cumsum(tile_cnt, axis=0)
    tile_start = tile_end - tile_cnt
    cnt = tile_end[-1]
    nblk = (cnt + MOE_BLOCK - 1) // MOE_BLOCK
    bend = jnp.cumsum(nblk)
    bstart = bend - nblk
    e_last = jnp.max(jnp.where(cnt > 0, jnp.arange(ne), 0))
    b = jnp.arange(n_blocks)
    blk_valid = b < bend[-1]
    blk_e = jnp.minimum(jnp.sum(bend[None, :] <= b[:, None], axis=1), e_last).astype(jnp.int32)
    blk_r0 = (b - bstart[blk_e]) * MOE_BLOCK
    blk_rows = jnp.where(blk_valid, jnp.clip(cnt[blk_e] - blk_r0, 0, MOE_BLOCK), 0).astype(jnp.int32)
    j = jnp.arange(n_sub)
    sub_e = blk_e[j // per_blk]
    sub_r0 = blk_r0[j // per_blk] + (j % per_blk) * MOE_SUB
    sub_valid = jnp.logical_and(blk_valid[j // per_blk], sub_r0 < cnt[sub_e])
    ends = tile_end[:, sub_e]
    r1 = jnp.minimum(sub_r0 + MOE_SUB, cnt[sub_e])
    c_lo = jnp.sum(ends <= sub_r0[None, :], axis=0)
    c_hi = jnp.sum(ends < r1[None, :], axis=0)
    c_lo = jnp.where(sub_valid, c_lo, 1).astype(jnp.int32)
    c_hi = jnp.where(sub_valid, jnp.minimum(c_hi, n_tiles - 1), 0).astype(jnp.int32)
    base = (bstart * MOE_BLOCK).astype(jnp.int32)
    n_cmb = n_sub * (MOE_SUB // MOE_CMB)
    j0 = jnp.minimum((base[None, :] + tile_start) // MOE_CMB, n_cmb - MOE_CMB_BLOCKS).astype(jnp.int32)
    return dict(blk_e=blk_e, blk_valid=blk_valid.astype(jnp.int32), blk_rows=blk_rows,
                sub_e=sub_e.astype(jnp.int32), sub_r0=sub_r0.astype(jnp.int32), c_lo=c_lo, c_hi=c_hi,
                base=base, j0=j0.reshape(-1))


def _moe_dispatch_body(e_ref, r0_ref, lo_ref, hi_ref, h_ref, rank_t_ref, xs_ref, acc_ref):
    j = pl.program_id(0)
    e = e_ref[j]
    rows = (r0_ref[j] + lax.broadcasted_iota(jnp.int32, (MOE_SUB, 1), 0)).astype(F32)
    sub = lax.broadcasted_iota(jnp.int32, (SUBLANES, MOE_SUB), 0)
    n_tiles = rank_t_ref.shape[0]
    lo, hi = lo_ref[j], hi_ref[j]
    acc_ref[...] = jnp.zeros_like(acc_ref)

    def step(w, carry):
        first = lo + w * MOE_WINDOW
        c0 = jnp.minimum(first, n_tiles - MOE_WINDOW)
        pieces = []
        for i in range(MOE_WINDOW):
            c = c0 + i
            rk = jnp.sum(jnp.where(sub == e, rank_t_ref[c], 0.0), axis=0, keepdims=True)
            rk = jnp.where(c >= first, rk, -1.0)
            pieces.append(jnp.where(rk == rows, 1.0, 0.0).astype(BF16))
        onehot = jnp.concatenate(pieces, axis=1)
        off = pl.multiple_of(c0 * MOE_SUB, MOE_SUB)
        acc_ref[...] += _dot(onehot, h_ref[pl.ds(off, MOE_WINDOW * MOE_SUB), :])
        return carry

    lax.fori_loop(0, (hi - lo + MOE_WINDOW) // MOE_WINDOW, step, 0)
    xs_ref[...] = acc_ref[...].astype(BF16)


def _moe_dispatch(h, rank_t, plan, *, n_sub):
    t, d = h.shape
    grid_spec = pltpu.PrefetchScalarGridSpec(
        num_scalar_prefetch=4,
        grid=(n_sub,),
        in_specs=[pl.BlockSpec((t, d), lambda j, *_: (0, 0), pipeline_mode=pl.Buffered(1)),
                  pl.BlockSpec(rank_t.shape, lambda j, *_: (0, 0, 0), pipeline_mode=pl.Buffered(1))],
        out_specs=pl.BlockSpec((MOE_SUB, d), lambda j, *_: (j, 0)),
        scratch_shapes=[pltpu.VMEM((MOE_SUB, d), F32)],
    )
    return pl.pallas_call(
        _moe_dispatch_body,
        grid_spec=grid_spec,
        out_shape=jax.ShapeDtypeStruct((n_sub * MOE_SUB, d), BF16),
        compiler_params=_params("arbitrary"),
        name="moe_dispatch",
    )(plan['sub_e'], plan['sub_r0'], plan['c_lo'], plan['c_hi'], h, rank_t)


def _moe_ffn_body(e_ref, valid_ref, rows_ref, xs_ref, wg_ref, wu_ref, wd_ref, y_ref, acc_ref):
    b = pl.program_id(0)
    f = pl.program_id(1)
    n_rows = rows_ref[b]
    last = f == pl.num_programs(1) - 1
    wg = wg_ref[...].astype(BF16)
    wu = wu_ref[...].astype(BF16)
    wd = wd_ref[...].astype(BF16)
    full = n_rows == MOE_BLOCK

    def swiglu(h):
        a = _silu(_dot(h, wg)) * _dot(h, wu)
        return _dot(a.astype(BF16), wd)

    @pl.when(jnp.logical_and(full, f == 0))
    def _():
        acc_ref[...] = swiglu(xs_ref[...])

    @pl.when(jnp.logical_and(full, f > 0))
    def _():
        acc_ref[...] += swiglu(xs_ref[...])

    @pl.when(jnp.logical_and(full, last))
    def _():
        y_ref[...] = acc_ref[...].astype(BF16)

    for s in range(MOE_BLOCK // MOE_SUB):
        sl = slice(s * MOE_SUB, (s + 1) * MOE_SUB)
        live = jnp.logical_and(jnp.logical_not(full), s * MOE_SUB < n_rows)
        dead = jnp.logical_and(jnp.logical_not(full), s * MOE_SUB >= n_rows)

        @pl.when(jnp.logical_and(live, f == 0))
        def _():
            acc_ref[sl, :] = jnp.zeros((MOE_SUB, acc_ref.shape[1]), F32)

        @pl.when(live)
        def _():
            acc_ref[sl, :] += swiglu(xs_ref[sl, :])

        @pl.when(jnp.logical_and(live, last))
        def _():
            y_ref[sl, :] = acc_ref[sl, :].astype(BF16)

        @pl.when(jnp.logical_and(dead, last))
        def _():
            y_ref[sl, :] = jnp.zeros((MOE_SUB, y_ref.shape[1]), BF16)


def _moe_ffn(xs, wg, wu, wd, plan, *, n_blocks, tf):
    _, d = xs.shape
    ne, _, ff = wg.shape
    nf = ff // tf

    def w_in(b, f, e_ref, valid_ref, rows_ref):
        return (e_ref[b], 0, jnp.where(valid_ref[b] > 0, f, nf - 1))

    def w_down(b, f, e_ref, valid_ref, rows_ref):
        return (e_ref[b], jnp.where(valid_ref[b] > 0, f, nf - 1), 0)

    grid_spec = pltpu.PrefetchScalarGridSpec(
        num_scalar_prefetch=3,
        grid=(n_blocks, nf),
        in_specs=[pl.BlockSpec((MOE_BLOCK, d), lambda b, f, *_: (b, 0)),
                  pl.BlockSpec((None, d, tf), w_in),
                  pl.BlockSpec((None, d, tf), w_in),
                  pl.BlockSpec((None, tf, d), w_down)],
        out_specs=pl.BlockSpec((MOE_BLOCK, d), lambda b, f, *_: (b, 0)),
        scratch_shapes=[pltpu.VMEM((MOE_BLOCK, d), F32)],
    )
    return pl.pallas_call(
        _moe_ffn_body,
        grid_spec=grid_spec,
        out_shape=jax.ShapeDtypeStruct((n_blocks * MOE_BLOCK, d), BF16),
        compiler_params=_params("arbitrary", "arbitrary"),
        name="moe_ffn",
    )(plan['blk_e'], plan['blk_valid'], plan['blk_rows'], xs, wg, wu, wd)


def _moe_combine_body(j0_ref, base_ref, x_ref, mod_ref, gates_ref, rank_ref, *rest, ne, split_tiles):
    nblk = MOE_CMB_BLOCKS
    y_refs, o_ref, o2_ref = rest[:nblk * ne], rest[nblk * ne], rest[nblk * ne + 1]
    c = pl.program_id(0)
    gates = gates_ref[...]
    rank = rank_ref[...]
    lane = lax.broadcasted_iota(jnp.int32, gates.shape, 1)
    col = lax.broadcasted_iota(jnp.int32, (1, MOE_CMB), 1).astype(F32)
    acc = None
    for e in range(ne):
        pick = lane == e
        g = jnp.sum(jnp.where(pick, gates, 0.0), axis=-1, keepdims=True)
        rk = jnp.sum(jnp.where(pick, rank, 0.0), axis=-1, keepdims=True)
        shift = (base_ref[e] - j0_ref[c * ne + e] * MOE_CMB).astype(F32)
        loc = jnp.where(rk >= 0.0, rk + shift, -1.0)
        picked = None
        for b in range(nblk):
            onehot = jnp.where(loc == col + float(b * MOE_CMB), 1.0, 0.0).astype(BF16)
            d = _dot(onehot, y_refs[nblk * e + b][...])
            picked = d if picked is None else picked + d
        contrib = g * picked
        acc = contrib if acc is None else acc + contrib
    out = x_ref[...] + mod_ref[5:6, :] * acc

    @pl.when(c < split_tiles)
    def _():
        o_ref[...] = out

    @pl.when(c >= split_tiles)
    def _():
        o2_ref[...] = out


def _moe_combine(x, mod, gates, rank, y, plan, *, ne, seg_len, t_split):
    t, d = x.shape
    tm = MOE_SUB
    split_tiles = t_split // tm
    y_specs = []
    for e in range(ne):
        for b in range(MOE_CMB_BLOCKS):
            y_specs.append(pl.BlockSpec((MOE_CMB, d), lambda c, j0, base, e=e, b=b: (j0[c * ne + e] + b, 0)))
    grid_spec = pltpu.PrefetchScalarGridSpec(
        num_scalar_prefetch=2,
        grid=(t // tm,),
        in_specs=[pl.BlockSpec((tm, d), lambda c, *_: (c, 0)),
                  pl.BlockSpec((None, 6, d), lambda c, *_: (c * tm // seg_len, 0, 0)),
                  pl.BlockSpec((tm, LANES), lambda c, *_: (c, 0)),
                  pl.BlockSpec((tm, LANES), lambda c, *_: (c, 0))] + y_specs,
        out_specs=[pl.BlockSpec((tm, d), lambda c, *_: (jnp.minimum(c, split_tiles - 1), 0)),
                   pl.BlockSpec((tm, d), lambda c, *_: (jnp.maximum(c - split_tiles, 0), 0))],
    )
    return pl.pallas_call(
        functools.partial(_moe_combine_body, ne=ne, split_tiles=split_tiles),
        grid_spec=grid_spec,
        out_shape=[jax.ShapeDtypeStruct((t_split, d), F32), jax.ShapeDtypeStruct((t - t_split, d), F32)],
        compiler_params=_params("arbitrary"),
        name="moe_combine",
    )(plan['j0'], plan['base'], x, mod, gates, rank, *([y] * (MOE_CMB_BLOCKS * ne)))


def _attn_out_moe(x, o, w_out, mod, g, w_router, wg, wu, wd, *, seg_len, t_split, top_k=2):
    t, d = x.shape
    ne = w_router.shape[1]
    n_blocks = t * top_k // MOE_BLOCK + ne
    x, h, gates, rank, rank_t = _moe_route(x, o, w_out, mod, g, w_router, seg_len=seg_len)
    plan = _moe_plan(rank, ne, n_blocks=n_blocks)
    xs = _moe_dispatch(h, rank_t, plan, n_sub=n_blocks * (MOE_BLOCK // MOE_SUB))
    y = _moe_ffn(xs, wg, wu, wd, plan, n_blocks=n_blocks, tf=MOE_TF)
    return _moe_combine(x, mod, gates, rank, y, plan, ne=ne, seg_len=seg_len, t_split=t_split)


def _even_layer(x_lat, x_ctx, cond8, p, cache_ckv, cache_kr, *, nb, lat_seq, n_ctx, ctx_seq):
    d = x_lat.shape[1]
    t_lat = nb * lat_seq
    t = t_lat + x_ctx.shape[0]
    seg_len = lat_seq
    past = cache_ckv.shape[1]
    mod = _adaln(cond8, p['w_mod'], p['b_mod'])

    tm = 512
    tables = _rope_tables(lat_seq, MLA_ROPE, (MLA_NOPE,), tm)
    z_hy, kr, q, ckvn = _even_in(x_lat, x_ctx, mod, p, tables, tm=tm, seg_len=seg_len, t_lat=t_lat,
                                 lat_seq=lat_seq)

    u, x0 = _hy_pre(z_hy, p['hy_conv_w'], p['hy_conv_b'], t_lat=t_lat, lat_seq=lat_seq, ctx_seq=ctx_seq)
    k_raw, k_sum = _hyena_filters(p, lat_seq=lat_seq, ctx_seq=ctx_seq)
    dbias = p['hy_dbias'].reshape(1, HY_WIDTH)
    conv_lat = _hyena_lat(k_raw[:2 * lat_seq], k_sum[0:1], u, seq=lat_seq, nb=nb)
    conv_ctx = _hyena_ctx(k_raw[2 * lat_seq:], k_sum[1:2], u, seq=ctx_seq, nseq=n_ctx, u_row0=t_lat)

    cache_kr_p = jnp.zeros((nb, past, LANES), F32).at[:, :, :MLA_ROPE].set(cache_kr)
    ckvn_rows = jnp.concatenate(
        [jnp.concatenate([cache_ckv, ckvn[:t_lat].reshape(nb, lat_seq, -1)], axis=1).reshape(nb * (past + lat_seq), -1),
         ckvn[t_lat:]], axis=0)
    kr_rows = jnp.concatenate(
        [jnp.concatenate([cache_kr_p, kr[:t_lat].reshape(nb, lat_seq, LANES)], axis=1).reshape(nb * (past + lat_seq), LANES),
         kr[t_lat:]], axis=0)
    k_all, v_all = _mla_kv(ckvn_rows, kr_rows, p, tables, tm=tm, nb=nb, past=past, lat_seq=lat_seq)
    o = jnp.zeros((t, MLA_HEADS * MLA_V), BF16)
    o = _mla_attn(o, q, k_all, v_all, tq=ATT_TQ, pairs=1, n_seq=nb, seq_q=lat_seq, seq_k=past + lat_seq,
                  q_row0=0, k_row0=0)
    o = _mla_attn(o, q, k_all, v_all, tq=ctx_seq, pairs=MLA_HEADS // 2, n_seq=n_ctx, seq_q=ctx_seq,
                  seq_k=ctx_seq, q_row0=t_lat, k_row0=nb * (past + lat_seq))

    w_out = p['w_out'].astype(BF16)
    x = _even_out(x_lat, x_ctx, mod, conv_lat, conv_ctx, u, x0, dbias, o, w_out[:HY_WIDTH], w_out[HY_WIDTH:],
                  tm=512, seg_len=seg_len)
    x = _ffn(x, mod, p['norm2'], p['ffn_w_gate'].astype(BF16), p['ffn_w_up'].astype(BF16),
             p['ffn_w_down'].astype(BF16), tm=512, tf=1408, seg_len=seg_len)
    new_ckv = ckvn[t_lat:].reshape(n_ctx, ctx_seq, -1)
    new_kr = kr[t_lat:, :MLA_ROPE].reshape(n_ctx, ctx_seq, MLA_ROPE)
    return x, new_ckv, new_kr


def _odd_layer(x, cond8, p, cache_k, cache_v, lambda_init, *, nb, lat_seq, n_ctx, ctx_seq):
    t, d = x.shape
    t_lat = nb * lat_seq
    seg_len = lat_seq
    mod = _adaln(cond8, p['w_mod'], p['b_mod'])
    tm = 512
    tables = _rope_tables(lat_seq, DIFF_DH, (0, DIFF_DH), tm)
    q, k, v, new_k, new_v = _qkv(x, mod, p, tables, tm=tm, seg_len=seg_len, t_lat=t_lat, lat_seq=lat_seq,
                                 n_ctx=n_ctx, ctx_seq=ctx_seq)
    lam_p = jnp.stack([p['lam_q1'], p['lam_k1'], p['lam_q2'], p['lam_k2']])
    o = jnp.zeros((t, DIFF_HEADS * 2 * DIFF_DH), BF16)
    o = _diff_attn(o, q, k, v, cache_k.astype(BF16), cache_v.astype(BF16), lam_p, p['subln'],
                   tq=ATT_TQ, heads=1, n_seq=nb, seq_q=lat_seq, q_row0=0, lambda_init=lambda_init)
    o = _diff_attn(o, q, k, v, None, None, lam_p, p['subln'],
                   tq=ctx_seq, heads=DIFF_HEADS, n_seq=n_ctx, seq_q=ctx_seq, q_row0=t_lat,
                   lambda_init=lambda_init)
    x_lat, x_ctx = _attn_out_moe(x, o, p['w_out'].astype(BF16), mod, p['norm2'], p['w_router'],
                                 p['moe_w_gate'], p['moe_w_up'], p['moe_w_down'], seg_len=seg_len, t_split=t_lat)
    return x_lat, x_ctx, new_k, new_v


def kernel(x_prompt, x_sample, cache_l0_ckv, cache_l0_krope, cache_l1_k, cache_l1_v, c, c_ctx,
           l0_w_mod, l0_b_mod, l0_norm1, l0_norm2, l0_w_in, l0_hy_conv_w, l0_hy_conv_b,
           l0_hy_fw1, l0_hy_fb1, l0_hy_freq1, l0_hy_fw2, l0_hy_fb2, l0_hy_freq2, l0_hy_fw3, l0_hy_dbias,
           l0_mla_qa_norm, l0_mla_w_uq, l0_mla_kva_norm, l0_mla_w_ukv, l0_mla_q_norm, l0_mla_k_norm,
           l0_w_out, l0_ffn_w_gate, l0_ffn_w_up, l0_ffn_w_down,
           l1_w_mod, l1_b_mod, l1_norm1, l1_norm2, l1_w_qkv, l1_q_norm, l1_k_norm,
           l1_lam_q1, l1_lam_k1, l1_lam_q2, l1_lam_k2, l1_subln, l1_w_out,
           l1_w_router, l1_moe_w_gate, l1_moe_w_up, l1_moe_w_down):
    even = {
        'w_mod': l0_w_mod, 'b_mod': l0_b_mod, 'norm1': l0_norm1, 'norm2': l0_norm2, 'w_in': l0_w_in,
        'hy_conv_w': l0_hy_conv_w, 'hy_conv_b': l0_hy_conv_b, 'hy_fw1': l0_hy_fw1, 'hy_fb1': l0_hy_fb1,
        'hy_freq1': l0_hy_freq1, 'hy_fw2': l0_hy_fw2, 'hy_fb2': l0_hy_fb2, 'hy_freq2': l0_hy_freq2,
        'hy_fw3': l0_hy_fw3, 'hy_dbias': l0_hy_dbias, 'qa_norm': l0_mla_qa_norm, 'w_uq': l0_mla_w_uq,
        'kva_norm': l0_mla_kva_norm, 'w_ukv': l0_mla_w_ukv, 'q_norm': l0_mla_q_norm, 'k_norm': l0_mla_k_norm,
        'w_out': l0_w_out, 'ffn_w_gate': l0_ffn_w_gate, 'ffn_w_up': l0_ffn_w_up, 'ffn_w_down': l0_ffn_w_down,
    }
    odd = {
        'w_mod': l1_w_mod, 'b_mod': l1_b_mod, 'norm1': l1_norm1, 'norm2': l1_norm2, 'w_qkv': l1_w_qkv,
        'q_norm': l1_q_norm, 'k_norm': l1_k_norm, 'lam_q1': l1_lam_q1, 'lam_k1': l1_lam_k1,
        'lam_q2': l1_lam_q2, 'lam_k2': l1_lam_k2, 'subln': l1_subln, 'w_out': l1_w_out,
        'w_router': l1_w_router, 'moe_w_gate': l1_moe_w_gate, 'moe_w_up': l1_moe_w_up,
        'moe_w_down': l1_moe_w_down,
    }
    n_ctx, ctx_seq, d = x_prompt.shape
    nb, lat_seq, _ = x_sample.shape
    assert n_ctx * ctx_seq == lat_seq, "segment layout needs equally sized modulation segments"
    dims = dict(nb=nb, lat_seq=lat_seq, n_ctx=n_ctx, ctx_seq=ctx_seq)
    t_lat = nb * lat_seq
    cond8 = jnp.zeros((SUBLANES, d), F32).at[:nb].set(c).at[nb].set(c_ctx)

    x, new_l0_ckv, new_l0_krope = _even_layer(x_sample.reshape(t_lat, d), x_prompt.reshape(n_ctx * ctx_seq, d),
                                              cond8, even, cache_l0_ckv, cache_l0_krope, **dims)
    lambda_init = 0.8 - 0.6 * math.exp(-0.3 * 1)
    x_lat, x_ctx, new_l1_k, new_l1_v = _odd_layer(x, cond8, odd, cache_l1_k, cache_l1_v, lambda_init, **dims)

    y_sample = x_lat.reshape(nb, lat_seq, d)
    y_prompt = x_ctx.reshape(n_ctx, ctx_seq, d)
    return (y_prompt, y_sample, new_l0_ckv, new_l0_krope, new_l1_k, new_l1_v)
```

```python
import functools
import math

import numpy as np
import jax
import jax.numpy as jnp
from jax import lax
from jax.experimental import pallas as pl
from jax.experimental.pallas import tpu as pltpu

F32 = jnp.float32
BF16 = jnp.bfloat16

VMEM_LIMIT_BYTES = 56 * 1024 * 1024
LANES = 128
SUBLANES = 8
LOG2_E = math.log2(math.e)

GRID_W = 64
ROPE_BASE = 10000.0
NORM_EPS = 1e-6
HY_WIDTH = 512
HY_BANDS = 16
HY_FAST_DECAY_PCT = 0.3
HY_SLOW_DECAY_PCT = 1.5
HY_DECAY_TARGET = 1e-2
MLA_HEADS = 8
MLA_NOPE = 64
MLA_ROPE = 32
MLA_QK = MLA_NOPE + MLA_ROPE
MLA_V = 64
MLA_Q_RANK = 768
MLA_KV_RANK = 256
DIFF_HEADS = 8
DIFF_DH = 64
N_EXPERTS = 8


def _params(*sem):
    return pltpu.CompilerParams(dimension_semantics=sem, vmem_limit_bytes=VMEM_LIMIT_BYTES)


def _dot(a, b):
    return jnp.dot(a, b, preferred_element_type=F32)


def _dot_nt(a, b):
    return lax.dot_general(a, b, (((1,), (1,)), ((), ())), preferred_element_type=F32)


def _split_bf16(a):
    hi = a.astype(BF16)
    lo = (a - hi.astype(F32)).astype(BF16)
    return hi, lo


def _dot_f32(a, b):
    ah, al = _split_bf16(a)
    bh, bl = _split_bf16(b)
    return _dot(ah, bh) + (_dot(al, bh) + _dot(ah, bl))


def _rms(x, g, n=None):
    n = x.shape[-1] if n is None else n
    ms = jnp.sum(x * x, axis=-1, keepdims=True) * (1.0 / n)
    return x * lax.rsqrt(ms + NORM_EPS) * g


def _norm_mod(x, g, shift, scale):
    return _rms(x, g) * (1.0 + scale) + shift


def _silu(x):
    return x / (1.0 + jnp.exp(-x))


def _adaln_body(c_ref, w_ref, b_ref, o_ref):
    o_ref[...] = _dot_f32(_silu(c_ref[...]), w_ref[...]) + b_ref[...]


def _adaln(cond8, w_mod, b_mod):
    d, n = w_mod.shape
    tn = n // 4
    out = pl.pallas_call(
        _adaln_body,
        grid=(n // tn,),
        in_specs=[pl.BlockSpec((SUBLANES, d), lambda j: (0, 0)),
                  pl.BlockSpec((d, tn), lambda j: (0, j)),
                  pl.BlockSpec((1, tn), lambda j: (0, j))],
        out_specs=pl.BlockSpec((SUBLANES, tn), lambda j: (0, j)),
        out_shape=jax.ShapeDtypeStruct((SUBLANES, n), F32),
        compiler_params=_params("arbitrary"),
        name="adaln",
    )(cond8, w_mod, b_mod.reshape(1, n))
    return out.reshape(SUBLANES, 6, d)


def _two_part_specs(a, b, tm):
    na = a.shape[0] // tm
    cols = a.shape[1]
    return (pl.BlockSpec((tm, cols), lambda i: (jnp.minimum(i, na - 1), 0)),
            pl.BlockSpec((tm, cols), lambda i: (jnp.maximum(i - na, 0), 0)), na)


def _hy_pre_body(z_ref, zp_ref, zn_ref, w_ref, b_ref, u_ref, x0_ref, *, tm, lat_tiles, tiles_per_seq):
    i = pl.program_id(0)
    z = z_ref[...]
    in_lat = i < lat_tiles
    has_prev = jnp.logical_and(in_lat, i % tiles_per_seq != 0)
    has_next = jnp.logical_and(in_lat, i % tiles_per_seq != tiles_per_seq - 1)
    prev_row = jnp.where(has_prev, zp_ref[SUBLANES - 1:SUBLANES, :], 0.0)
    next_row = jnp.where(has_next, zn_ref[0:1, :], 0.0)
    rows = lax.broadcasted_iota(jnp.int32, z.shape, 0)
    z_m = jnp.where(rows == 0, prev_row, pltpu.roll(z, 1, 0))
    z_p = jnp.where(rows == tm - 1, next_row, pltpu.roll(z, tm - 1, 0))
    zc = b_ref[...] + z_m * w_ref[0:1, :] + z * w_ref[1:2, :] + z_p * w_ref[2:3, :]
    c = HY_WIDTH
    x0_ref[...] = zc[:, :c]
    u_ref[...] = zc[:, 2 * c:] * zc[:, c:2 * c]


def _hy_pre(z, conv_w, conv_b, *, t_lat, lat_seq, ctx_seq):
    t = z.shape[0]
    tm = ctx_seq
    c3 = 3 * HY_WIDTH
    nb8 = t // SUBLANES
    body = functools.partial(_hy_pre_body, tm=tm, lat_tiles=t_lat // tm, tiles_per_seq=lat_seq // tm)
    return pl.pallas_call(
        body,
        grid=(t // tm,),
        in_specs=[pl.BlockSpec((tm, c3), lambda i: (i, 0)),
                  pl.BlockSpec((SUBLANES, c3), lambda i: (jnp.maximum(i * (tm // SUBLANES) - 1, 0), 0)),
                  pl.BlockSpec((SUBLANES, c3), lambda i: (jnp.minimum((i + 1) * (tm // SUBLANES), nb8 - 1), 0)),
                  pl.BlockSpec((3, c3), lambda i: (0, 0)),
                  pl.BlockSpec((1, c3), lambda i: (0, 0))],
        out_specs=[pl.BlockSpec((tm, HY_WIDTH), lambda i: (i, 0)),
                   pl.BlockSpec((tm, HY_WIDTH), lambda i: (i, 0))],
        out_shape=[jax.ShapeDtypeStruct((t, HY_WIDTH), F32),
                   jax.ShapeDtypeStruct((t, HY_WIDTH), F32)],
        compiler_params=_params("parallel"),
        name="hy_pre",
    )(z, z, z, conv_w, conv_b.reshape(1, c3))


def _filter_embedding(seq):
    t01 = np.linspace(0.0, 1.0, seq)[:, None]
    w = 2.0 * math.pi * np.arange(seq)[:, None] / seq
    f = np.linspace(1e-4, HY_BANDS - 1, HY_BANDS)[None, :]
    z = np.concatenate([t01, np.cos(f * w), -np.sin(f * w)], axis=-1)
    z_rev = np.concatenate([z[:1], z[:0:-1]], axis=0)
    zz = np.concatenate([z, z_rev], axis=0)
    out = np.zeros((2 * seq, LANES), np.float32)
    out[:, :zz.shape[1]] = zz
    return out


def _filter_body(zz_ref, dl_ref, w1_ref, b1_ref, f1_ref, w2_ref, b2_ref, f2_ref, w3_ref,
                 kl_ref, kc_ref, s_ref, *, tm, lat_tiles, ctx_tiles):
    i = pl.program_id(0)
    zz = zz_ref[...]
    h = jnp.sin(f1_ref[...] * (_dot_f32(zz, w1_ref[...]) + b1_ref[...]))
    h = jnp.sin(f2_ref[...] * (_dot_f32(h, w2_ref[...]) + b2_ref[...]))
    h = _dot_f32(h, w3_ref[...])
    is_bwd = jnp.logical_or(jnp.logical_and(i >= lat_tiles // 2, i < lat_tiles),
                            i >= lat_tiles + ctx_tiles // 2)
    first_bwd = jnp.logical_or(i == lat_tiles // 2, i == lat_tiles + ctx_tiles // 2)
    window = jnp.exp(-zz[:, 0:1] * dl_ref[...])
    k = jnp.where(is_bwd, h[:, HY_WIDTH:], h[:, :HY_WIDTH]) * window
    rows = lax.broadcasted_iota(jnp.int32, k.shape, 0)
    k = jnp.where(jnp.logical_and(first_bwd, rows == 0), 0.0, k)
    s = jnp.sum(jnp.abs(k), axis=0, keepdims=True)

    @pl.when(i == 0)
    def _():
        s_ref[...] = jnp.zeros_like(s_ref)

    @pl.when(i < lat_tiles)
    def _():
        kl_ref[...] = k
        s_ref[0:1, :] += s

    @pl.when(i >= lat_tiles)
    def _():
        kc_ref[...] = k
        s_ref[1:2, :] += s


def _hyena_filters(p, *, lat_seq, ctx_seq):
    tm = ctx_seq
    zz = jnp.asarray(np.concatenate([_filter_embedding(lat_seq), _filter_embedding(ctx_seq)], axis=0))
    rows = zz.shape[0]
    max_decay = math.log(HY_DECAY_TARGET) / HY_FAST_DECAY_PCT
    min_decay = math.log(HY_DECAY_TARGET) / HY_SLOW_DECAY_PCT
    deltas = jnp.asarray(np.abs(np.linspace(min_decay, max_decay, HY_WIDTH))[None, :].astype(np.float32))
    emb, hid = p['hy_fw1'].shape

    def pad2(a, r, c):
        return jnp.zeros((r, c), F32).at[:a.shape[0], :a.shape[1]].set(a)

    w1 = pad2(p['hy_fw1'], LANES, LANES)
    b1 = pad2(p['hy_fb1'][None, :], 1, LANES)
    f1 = pad2(p['hy_freq1'][None, :], 1, LANES)
    w2 = pad2(p['hy_fw2'], LANES, LANES)
    b2 = pad2(p['hy_fb2'][None, :], 1, LANES)
    f2 = pad2(p['hy_freq2'][None, :], 1, LANES)
    w3 = pad2(p['hy_fw3'], LANES, 2 * HY_WIDTH)
    lat_tiles = 2 * lat_seq // tm
    body = functools.partial(_filter_body, tm=tm, lat_tiles=lat_tiles, ctx_tiles=2 * ctx_seq // tm)
    full = lambda i: (0, 0)
    return pl.pallas_call(
        body,
        grid=(rows // tm,),
        in_specs=[pl.BlockSpec((tm, LANES), lambda i: (i, 0)),
                  pl.BlockSpec((1, HY_WIDTH), full),
                  pl.BlockSpec((LANES, LANES), full), pl.BlockSpec((1, LANES), full), pl.BlockSpec((1, LANES), full),
                  pl.BlockSpec((LANES, LANES), full), pl.BlockSpec((1, LANES), full), pl.BlockSpec((1, LANES), full),
                  pl.BlockSpec((LANES, 2 * HY_WIDTH), full)],
        out_specs=[pl.BlockSpec((tm, HY_WIDTH), lambda i: (jnp.minimum(i, lat_tiles - 1), 0)),
                   pl.BlockSpec((tm, HY_WIDTH), lambda i: (jnp.maximum(i - lat_tiles, 0), 0)),
                   pl.BlockSpec((SUBLANES, HY_WIDTH), full)],
        out_shape=[jax.ShapeDtypeStruct((2 * lat_seq, HY_WIDTH), F32),
                   jax.ShapeDtypeStruct((2 * ctx_seq, HY_WIDTH), F32),
                   jax.ShapeDtypeStruct((SUBLANES, HY_WIDTH), F32)],
        compiler_params=_params("arbitrary"),
        name="hy_filter",
    )(zz, deltas, w1, b1, f1, w2, b2, f2, w3)


def _stack_complex(z):
    return np.block([[z.real, -z.imag], [z.imag, z.real]])


def _dft_consts_two_level(seq, n1, n2):
    n = 2 * seq
    assert n1 * n2 == n
    a1 = np.arange(n1)
    f1_full = np.exp(-2j * np.pi * np.outer(a1, a1) / n1)
    f1_u = np.concatenate([f1_full.real, f1_full.imag], axis=0)[:, :n1 // 2]
    f1_k = np.concatenate([f1_full.real, f1_full.imag], axis=0)
    a2 = np.arange(n2)
    f = a1[:, None, None] + n1 * a2[None, :, None]
    z = np.exp(-2j * np.pi * (f * a2[None, None, :]) / n)
    mf = np.stack([_stack_complex(z[i]) for i in range(n1)])
    mi = np.stack([_stack_complex(np.conj(z[i]).T) for i in range(n1)])
    g = np.exp(2j * np.pi * np.outer(a1[:n1 // 2], a1) / n1) / n
    g1 = np.concatenate([_kron_rows(g.real), _kron_rows(-g.imag)], axis=1)
    as32 = lambda a: jnp.asarray(a.astype(np.float32))
    return as32(_kron_rows(f1_u)), as32(_kron_rows(f1_k)), as32(mf), as32(mi), as32(g1)


def _dft_consts_one_level(seq):
    n = 2 * seq
    a = np.arange(n)
    z = np.exp(-2j * np.pi * np.outer(a, a) / n)
    mf = np.concatenate([z.real, z.imag], axis=0)
    zi = np.exp(2j * np.pi * np.outer(a[:seq], a) / n) / n
    mi = np.concatenate([zi.real, -zi.imag], axis=1)
    as32 = lambda a: jnp.asarray(a.astype(np.float32))
    return as32(mf), as32(mi)


HY_ROWS = 16


def _kron_rows(m):
    return np.kron(m, np.eye(HY_ROWS))


def _dft1_body(m_ref, x_ref, sc_ref, or_ref, oi_ref):
    k, r, c = x_ref.shape
    x = (x_ref[...] * (1.0 / sc_ref[...])).reshape(k * r, c).astype(BF16)
    o = _dot(m_ref[...], x)
    h = o.shape[0] // 2
    or_ref[...] = o[:h].reshape(or_ref.shape).astype(BF16)
    oi_ref[...] = o[h:].reshape(oi_ref.shape).astype(BF16)


def _dft1(m, x, scale_row, *, n1, groups):
    _, k, n2, c = x.shape
    g = groups
    ospec = pl.BlockSpec((None, n1, HY_ROWS, c), lambda b, j: (b, 0, j, 0))
    return pl.pallas_call(
        _dft1_body,
        grid=(g, n2 // HY_ROWS),
        in_specs=[pl.BlockSpec(m.shape, lambda b, j: (0, 0)),
                  pl.BlockSpec((None, k, HY_ROWS, c), lambda b, j: (b, 0, j, 0)),
                  pl.BlockSpec((1, c), lambda b, j: (0, 0))],
        out_specs=[ospec, ospec],
        out_shape=[jax.ShapeDtypeStruct((g, n1, n2, c), BF16)] * 2,
        compiler_params=_params("parallel", "parallel"),
        name="hy_dft1",
    )(m, x, scale_row)


SPEC_GROUP = 8


def _spec_fwd_body(mf_ref, ar_ref, ai_ref, kr_ref, ki_ref):
    for g in range(SPEC_GROUP):
        a = jnp.concatenate([ar_ref[g], ai_ref[g]], axis=0)
        x = _dot(mf_ref[g], a)
        h = x.shape[0] // 2
        kr_ref[g] = x[:h]
        ki_ref[g] = x[h:]


def _spec_fwd(mf, ar, ai):
    n1, n2, c = ar.shape
    spec = pl.BlockSpec((SPEC_GROUP, n2, c), lambda i: (i, 0, 0))
    return pl.pallas_call(
        _spec_fwd_body,
        grid=(n1 // SPEC_GROUP,),
        in_specs=[pl.BlockSpec((SPEC_GROUP, 2 * n2, 2 * n2), lambda i: (i, 0, 0)), spec, spec],
        out_specs=[spec, spec],
        out_shape=[jax.ShapeDtypeStruct((n1, n2, c), F32)] * 2,
        compiler_params=_params("parallel"),
        name="hy_spec_filter",
    )(mf, ar, ai)


def _spec_mul_body(mf_ref, mi_ref, kr_ref, ki_ref, ar_ref, ai_ref, br_ref, bi_ref):
    for g in range(SPEC_GROUP):
        a = jnp.concatenate([ar_ref[g], ai_ref[g]], axis=0)
        x = _dot(mf_ref[g], a)
        h = x.shape[0] // 2
        xr, xi = x[:h], x[h:]
        kr, ki = kr_ref[g], ki_ref[g]
        y = jnp.concatenate([xr * kr - xi * ki, xr * ki + xi * kr], axis=0).astype(BF16)
        b = _dot(mi_ref[g], y)
        br_ref[g] = b[:h].astype(BF16)
        bi_ref[g] = b[h:].astype(BF16)


def _spec_mul(mf, mi, kr, ki, ar, ai):
    nb, n1, n2, c = ar.shape
    mspec = pl.BlockSpec((SPEC_GROUP, 2 * n2, 2 * n2), lambda i, b: (i, 0, 0))
    kspec = pl.BlockSpec((SPEC_GROUP, n2, c), lambda i, b: (i, 0, 0))
    aspec = pl.BlockSpec((None, SPEC_GROUP, n2, c), lambda i, b: (b, i, 0, 0))
    return pl.pallas_call(
        _spec_mul_body,
        grid=(n1 // SPEC_GROUP, nb),
        in_specs=[mspec, mspec, kspec, kspec, aspec, aspec],
        out_specs=[aspec, aspec],
        out_shape=[jax.ShapeDtypeStruct((nb, n1, n2, c), BF16)] * 2,
        compiler_params=_params("parallel", "arbitrary"),
        name="hy_spec_mul",
    )(mf, mi, kr, ki, ar, ai)


def _idft1_body(m_ref, br_ref, bi_ref, o_ref):
    n1, r, c = br_ref.shape
    b = jnp.concatenate([br_ref[...].reshape(n1 * r, c), bi_ref[...].reshape(n1 * r, c)], axis=0)
    o_ref[...] = _dot(m_ref[...], b).reshape(o_ref.shape)


def _idft1(m, br, bi, *, n_out):
    nb, n1, n2, c = br.shape
    bspec = pl.BlockSpec((None, n1, HY_ROWS, c), lambda b, j: (b, 0, j, 0))
    return pl.pallas_call(
        _idft1_body,
        grid=(nb, n2 // HY_ROWS),
        in_specs=[pl.BlockSpec(m.shape, lambda b, j: (0, 0)), bspec, bspec],
        out_specs=pl.BlockSpec((None, n_out, HY_ROWS, c), lambda b, j: (b, 0, j, 0)),
        out_shape=jax.ShapeDtypeStruct((nb, n_out, n2, c), F32),
        compiler_params=_params("parallel", "parallel"),
        name="hy_idft1",
    )(m, br, bi)


def _ctx_filter_body(mf_ref, k_ref, sc_ref, kf_ref):
    kf_ref[...] = _dot(mf_ref[...], (k_ref[...] * (1.0 / sc_ref[...])).astype(BF16))


def _ctx_conv_body(mf_ref, mi_ref, kf_ref, u_ref, o_ref):
    x = _dot(mf_ref[...], u_ref[...].astype(BF16))
    h = x.shape[0] // 2
    xr, xi = x[:h], x[h:]
    kr, ki = kf_ref[:h, :], kf_ref[h:, :]
    y = jnp.concatenate([xr * kr - xi * ki, xr * ki + xi * kr], axis=0).astype(BF16)
    o_ref[...] = _dot(mi_ref[...], y)


def _hyena_ctx(k_raw, k_norm1, u, *, seq, nseq, u_row0):
    mf, mi = _dft_consts_one_level(seq)
    n = 2 * seq
    c = u.shape[1]
    kf = pl.pallas_call(
        _ctx_filter_body,
        out_shape=jax.ShapeDtypeStruct((2 * n, c), F32),
        compiler_params=_params(),
        name="hy_ctx_filter",
    )(mf.astype(BF16), k_raw, k_norm1)
    full = lambda s: (0, 0)
    return pl.pallas_call(
        _ctx_conv_body,
        grid=(nseq,),
        in_specs=[pl.BlockSpec((2 * n, seq), full),
                  pl.BlockSpec((seq, 2 * n), full),
                  pl.BlockSpec((2 * n, c), full),
                  pl.BlockSpec((seq, c), lambda s: (u_row0 // seq + s, 0))],
        out_specs=pl.BlockSpec((seq, c), lambda s: (s, 0)),
        out_shape=jax.ShapeDtypeStruct((nseq * seq, c), F32),
        compiler_params=_params("parallel"),
        name="hy_ctx_conv",
    )(mf[:, :seq].astype(BF16), mi.astype(BF16), kf, u)


def _hyena_lat(k_raw, k_norm1, u, *, seq, nb):
    c = u.shape[1]
    n1, n2 = 64, 2 * seq // 64
    f1_u, f1_k, mf, mi, g1 = _dft_consts_two_level(seq, n1, n2)
    mf = mf.astype(BF16)
    mi = mi.astype(BF16)
    ones_row = jnp.ones((1, c), F32)
    akr, aki = _dft1(f1_k.astype(BF16), k_raw.reshape(-1, n1, n2, c), k_norm1, n1=n1, groups=1)
    kr, ki = _spec_fwd(mf, akr[0], aki[0])
    ar, ai = _dft1(f1_u.astype(BF16), u.reshape(-1, n1 // 2, n2, c), ones_row, n1=n1, groups=nb)
    br, bi = _spec_mul(mf, mi, kr, ki, ar, ai)
    y = _idft1(g1.astype(BF16), br, bi, n_out=n1 // 2)
    return y.reshape(nb * seq, c)


def _rope_tables(seq, rope_dims, lane_offsets, pad_rows):
    rows = seq // GRID_W
    rr, cc = np.meshgrid(np.arange(rows), np.arange(GRID_W), indexing='ij')
    pos = (rr.reshape(-1).astype(np.float64), cc.reshape(-1).astype(np.float64))
    half = rope_dims // 2
    q = half // 2
    inv_freq = ROPE_BASE ** (-np.arange(0, half, 2, dtype=np.float64) / half)
    cos_t = np.ones((seq + pad_rows, LANES), np.float64)
    sin_a = np.zeros((seq + pad_rows, LANES), np.float64)
    sin_b = np.zeros((seq + pad_rows, LANES), np.float64)
    for off in lane_offsets:
        for axis in range(2):
            ang = pos[axis][:, None] * inv_freq[None, :]
            base = off + axis * half
            cos_t[:seq, base:base + q] = np.cos(ang)
            cos_t[:seq, base + q:base + half] = np.cos(ang)
            sin_b[:seq, base:base + q] = -np.sin(ang)
            sin_a[:seq, base + q:base + half] = np.sin(ang)
    as32 = lambda a: jnp.asarray(a.astype(np.float32))
    return as32(cos_t), as32(sin_a), as32(sin_b)


def _rope(x, cos_t, sin_a, sin_b, shift):
    return x * cos_t + pltpu.roll(x, shift, 1) * sin_a + pltpu.roll(x, LANES - shift, 1) * sin_b


def _even_in_body(xa_ref, xb_ref, mod_ref, g_ref, w_ref, qa_ref, kva_ref, wuq_ref, qn_ref,
                  cos_ref, sa_ref, sb_ref, zhy_ref, kr_ref, q_ref, ckvn_ref, zq_ref, *, na, scale):
    i = pl.program_id(0)
    n = pl.num_programs(0) - 1

    @pl.when(i == 0)
    def _():
        zq_ref[1] = jnp.zeros(zq_ref.shape[1:], F32)

    c_q = MLA_Q_RANK
    zp = zq_ref[(i + 1) % 2]
    ckvn_ref[...] = _rms(zp[:, c_q:], kva_ref[...])
    cqn = _rms(zp[:, :c_q], qa_ref[...])
    q = _dot(cqn.astype(BF16), wuq_ref[...])
    cos_t, sin_a, sin_b = cos_ref[...], sa_ref[...], sb_ref[...]
    g = qn_ref[...]
    for hh in range(MLA_HEADS):
        sl = slice(hh * LANES, (hh + 1) * LANES)
        qh = _rope(_rms(q[:, sl], g, MLA_QK), cos_t, sin_a, sin_b, MLA_ROPE // 4)
        q_ref[:, sl] = (qh * scale).astype(BF16)

    x = jnp.where(jnp.minimum(i, n - 1) < na, xa_ref[...], xb_ref[...])
    h = _norm_mod(x, g_ref[...], mod_ref[0:1, :], mod_ref[1:2, :])
    z = _dot(h.astype(BF16), w_ref[...])
    c_hy = 3 * HY_WIDTH
    c_kv = c_hy + MLA_Q_RANK + MLA_KV_RANK
    zhy_ref[...] = z[:, :c_hy]
    kr_ref[...] = z[:, c_kv:]
    zq_ref[i % 2] = z[:, c_hy:c_kv]


def _pad_heads(w, heads, width):
    k = w.shape[0]
    w3 = w.reshape(k, heads, width)
    return jnp.zeros((k, heads, LANES), w.dtype).at[:, :, :width].set(w3).reshape(k, heads * LANES)


def _even_in(xa, xb, mod, p, tables, *, tm, seg_len, t_lat, lat_seq):
    d = xa.shape[1]
    t = xa.shape[0] + xb.shape[0]
    n_in = p['w_in'].shape[1]
    c_kv = 3 * HY_WIDTH + MLA_Q_RANK + MLA_KV_RANK
    assert c_kv % LANES == 0 and n_in - c_kv == MLA_ROPE
    n_pad = c_kv + LANES
    w_in = jnp.zeros((d, n_pad), BF16).at[:, :n_in].set(p['w_in'].astype(BF16))
    wuq = _pad_heads(p['w_uq'], MLA_HEADS, MLA_QK).astype(BF16)
    cos_t, sin_a, sin_b = tables
    qn = jnp.zeros((1, LANES), F32).at[0, :MLA_QK].set(p['q_norm'])
    pos_blocks = lat_seq // tm
    lat_tiles = t_lat // tm
    n = t // tm
    na = xa.shape[0] // tm
    cur = lambda i: jnp.minimum(i, n - 1)
    prev = lambda i: jnp.maximum(i - 1, 0)

    def tmap(i):
        j = prev(i)
        return (jnp.where(j < lat_tiles, j % pos_blocks, pos_blocks), 0)

    tspec = pl.BlockSpec((tm, LANES), tmap)
    full = lambda i: (0, 0)
    row_cur = lambda cols: pl.BlockSpec((tm, cols), lambda i: (cur(i), 0))
    row_prev = lambda cols: pl.BlockSpec((tm, cols), lambda i: (prev(i), 0))
    return pl.pallas_call(
        functools.partial(_even_in_body, na=na, scale=MLA_QK ** -0.5 * LOG2_E),
        grid=(n + 1,),
        in_specs=[pl.BlockSpec((tm, d), lambda i: (jnp.minimum(cur(i), na - 1), 0)),
                  pl.BlockSpec((tm, d), lambda i: (jnp.maximum(cur(i) - na, 0), 0)),
                  pl.BlockSpec((None, 6, d), lambda i: (cur(i) * tm // seg_len, 0, 0)),
                  pl.BlockSpec((1, d), full),
                  pl.BlockSpec((d, n_pad), full),
                  pl.BlockSpec((1, MLA_Q_RANK), full),
                  pl.BlockSpec((1, MLA_KV_RANK), full),
                  pl.BlockSpec((MLA_Q_RANK, MLA_HEADS * LANES), full),
                  pl.BlockSpec((1, LANES), full),
                  tspec, tspec, tspec],
        out_specs=[row_cur(3 * HY_WIDTH), row_cur(LANES), row_prev(MLA_HEADS * LANES), row_prev(MLA_KV_RANK)],
        out_shape=[jax.ShapeDtypeStruct((t, 3 * HY_WIDTH), F32),
                   jax.ShapeDtypeStruct((t, LANES), F32),
                   jax.ShapeDtypeStruct((t, MLA_HEADS * LANES), BF16),
                   jax.ShapeDtypeStruct((t, MLA_KV_RANK), F32)],
        scratch_shapes=[pltpu.VMEM((2, tm, MLA_Q_RANK + MLA_KV_RANK), F32)],
        compiler_params=_params("arbitrary"),
        name="even_in",
    )(xa, xb, mod, p['norm1'].reshape(1, d), w_in, p['qa_norm'].reshape(1, -1), p['kva_norm'].reshape(1, -1),
      wuq, qn, cos_t, sin_a, sin_b)


def _mla_kv_body(ckvn_ref, kr_ref, wk_ref, wv_ref, kn_ref, cos_ref, sa_ref, sb_ref, k_ref, v_ref):
    c = ckvn_ref[...].astype(BF16)
    k = _dot(c, wk_ref[...])
    v_ref[...] = _dot(c, wv_ref[...]).astype(BF16)
    kr = pltpu.roll(kr_ref[...], MLA_NOPE, 1)
    cos_t, sin_a, sin_b = cos_ref[...], sa_ref[...], sb_ref[...]
    g = kn_ref[...]
    for h in range(MLA_HEADS):
        sl = slice(h * LANES, (h + 1) * LANES)
        kh = _rope(_rms(k[:, sl] + kr, g, MLA_QK), cos_t, sin_a, sin_b, MLA_ROPE // 4)
        k_ref[:, sl] = kh.astype(BF16)


def _mla_kv(ckvn_rows, kr_rows, p, tables, *, tm, nb, past, lat_seq):
    r = ckvn_rows.shape[0]
    w = p['w_ukv'].reshape(MLA_KV_RANK, MLA_HEADS, MLA_NOPE + MLA_V)
    wk = _pad_heads(w[:, :, :MLA_NOPE].reshape(MLA_KV_RANK, -1), MLA_HEADS, MLA_NOPE).astype(BF16)
    wv = w[:, :, MLA_NOPE:].reshape(MLA_KV_RANK, MLA_HEADS * MLA_V).astype(BF16)
    cos_t, sin_a, sin_b = tables
    kn = jnp.zeros((1, LANES), F32).at[0, :MLA_QK].set(p['k_norm'])
    per_b = (past + lat_seq) // tm
    past_tiles = past // tm
    pos_blocks = lat_seq // tm
    lat_tiles = nb * per_b

    def tmap(i):
        j = i % per_b
        is_pos = jnp.logical_and(i < lat_tiles, j >= past_tiles)
        return (jnp.where(is_pos, j - past_tiles, pos_blocks), 0)

    tspec = pl.BlockSpec((tm, LANES), tmap)
    full = lambda i: (0, 0)
    return pl.pallas_call(
        _mla_kv_body,
        grid=(r // tm,),
        in_specs=[pl.BlockSpec((tm, MLA_KV_RANK), lambda i: (i, 0)),
                  pl.BlockSpec((tm, LANES), lambda i: (i, 0)),
                  pl.BlockSpec((MLA_KV_RANK, MLA_HEADS * LANES), full),
                  pl.BlockSpec((MLA_KV_RANK, MLA_HEADS * MLA_V), full),
                  pl.BlockSpec((1, LANES), full),
                  tspec, tspec, tspec],
        out_specs=[pl.BlockSpec((tm, MLA_HEADS * LANES), lambda i: (i, 0)),
                   pl.BlockSpec((tm, MLA_HEADS * MLA_V), lambda i: (i, 0))],
        out_shape=[jax.ShapeDtypeStruct((r, MLA_HEADS * LANES), BF16),
                   jax.ShapeDtypeStruct((r, MLA_HEADS * MLA_V), BF16)],
        compiler_params=_params("parallel"),
        name="mla_kv",
    )(ckvn_rows, kr_rows, wk, wv, kn, cos_t, sin_a, sin_b)


ATT_CHUNK = 512
ATT_UNIT_ROWS = 256
ATT_TQ = 1024


def _fill_vaug(vaug_ref, g, v_blocks):
    off = 0
    for v in v_blocks:
        n = v.shape[0]
        vaug_ref[g, off:off + n, :LANES] = v
        off += n
    vaug_ref[g, :, LANES:] = jnp.ones((vaug_ref.shape[1], LANES), BF16)


def _softmax_pv(units, s_ref, n_keys):
    chunk = min(ATT_CHUNK, n_keys)
    chunks = [slice(c * chunk, (c + 1) * chunk) for c in range(n_keys // chunk)]

    def scores(u, rows, m_lane):
        s = _dot_nt(units[u][0], units[u][1](rows))
        s_ref[u % 2, :, rows] = s
        for j in range(chunk // LANES):
            blk = s[:, j * LANES:(j + 1) * LANES]
            m_lane = blk if m_lane is None else jnp.maximum(m_lane, blk)
        return m_lane

    def values(u, rows, m, acc):
        p = jnp.exp2(s_ref[u % 2, :, rows] - m).astype(BF16)
        d = _dot(p, units[u][2](rows))
        return d if acc is None else acc + d

    outs = []
    m_lane = None
    for rows in chunks:
        m_lane = scores(0, rows, m_lane)
    for u in range(len(units)):
        m = jnp.max(m_lane, axis=-1, keepdims=True)
        acc, m_lane = None, None
        for rows in chunks:
            acc = values(u, rows, m, acc)
            if u + 1 < len(units):
                m_lane = scores(u + 1, rows, m_lane)
        outs.append(acc)
    return outs


def _mla_attn_body(prev_ref, q_ref, k_ref, v_ref, o_ref, vaug_ref, s_ref):
    pairs = vaug_ref.shape[0]

    @pl.when(pl.program_id(2) == 0)
    def _():
        for g in range(pairs):
            _fill_vaug(vaug_ref, g, [v_ref[:, g * LANES:(g + 1) * LANES]])

    n_keys = k_ref.shape[0]
    tq = q_ref.shape[0]
    ur = s_ref.shape[1]
    units, slots = [], []
    for r0 in range(0, tq, ur):
        for g in range(pairs):
            slots.append((r0, g))
            for hh in range(2):
                sl = slice((2 * g + hh) * LANES, (2 * g + hh + 1) * LANES)
                units.append((q_ref[r0:r0 + ur, sl], lambda rows, sl=sl: k_ref[rows, sl],
                              lambda rows, g=g: vaug_ref[g, rows, :]))
    res = [r[:, :LANES] / r[:, LANES:] for r in _softmax_pv(units, s_ref, n_keys)]
    lane = lax.broadcasted_iota(jnp.int32, res[0].shape, 1)
    for i, (r0, g) in enumerate(slots):
        o_ref[r0:r0 + ur, g * LANES:(g + 1) * LANES] = jnp.where(
            lane < MLA_V, res[2 * i], res[2 * i + 1]).astype(BF16)


def _mla_attn(prev, q, k, v, *, tq, pairs, n_seq, seq_q, seq_k, q_row0, k_row0):
    hp = MLA_HEADS // 2 // pairs
    nq = seq_q // tq
    qb0, kb0 = q_row0 // tq, k_row0 // seq_k
    return pl.pallas_call(
        _mla_attn_body,
        grid=(n_seq, hp, nq),
        in_specs=[pl.BlockSpec(memory_space=pl.ANY),
                  pl.BlockSpec((tq, pairs * 2 * LANES), lambda s, h, i: (qb0 + s * nq + i, h)),
                  pl.BlockSpec((seq_k, pairs * 2 * LANES), lambda s, h, i: (kb0 + s, h)),
                  pl.BlockSpec((seq_k, pairs * 2 * MLA_V), lambda s, h, i: (kb0 + s, h))],
        out_specs=pl.BlockSpec((tq, pairs * 2 * MLA_V), lambda s, h, i: (qb0 + s * nq + i, h)),
        out_shape=jax.ShapeDtypeStruct(prev.shape, prev.dtype),
        input_output_aliases={0: 0},
        scratch_shapes=[pltpu.VMEM((pairs, seq_k, 2 * LANES), BF16),
                        pltpu.VMEM((2, min(tq, ATT_UNIT_ROWS), seq_k), F32)],
        compiler_params=_params("arbitrary", "arbitrary", "arbitrary"),
        name="mla_attn",
    )(prev, q, k, v)


def _diff_attn_body(prev_ref, *refs, n_seg, lambda_init):
    q_ref, lam_ref, sub_ref = refs[0], refs[1], refs[2]
    k_refs = refs[3:3 + n_seg]
    v_refs = refs[3 + n_seg:3 + 2 * n_seg]
    o_ref, kcat_ref, vaug_ref, s_ref = refs[3 + 2 * n_seg:]
    heads = vaug_ref.shape[0]

    def head_block(ref, g):
        return ref[g] if len(ref.shape) == 3 else ref[:, g * LANES:(g + 1) * LANES]

    @pl.when(pl.program_id(2) == 0)
    def _():
        for g in range(heads):
            _fill_vaug(vaug_ref, g, [head_block(v, g) for v in v_refs])
            off = 0
            for k in k_refs:
                kcat_ref[g, off:off + k.shape[-2], :] = head_block(k, g)
                off += k.shape[-2]

    lp = lam_ref[...]
    lam = (jnp.exp(jnp.sum(lp[0:1] * lp[1:2], axis=-1, keepdims=True))
           - jnp.exp(jnp.sum(lp[2:3] * lp[3:4], axis=-1, keepdims=True)) + lambda_init)
    n_keys = kcat_ref.shape[1]
    tq = q_ref.shape[0]
    ur = s_ref.shape[1]
    units, slots = [], []
    for r0 in range(0, tq, ur):
        for g in range(heads):
            slots.append((r0, g))
            q = q_ref[r0:r0 + ur, g * LANES:(g + 1) * LANES].astype(F32)
            lane = lax.broadcasted_iota(jnp.int32, q.shape, 1)
            k_of = lambda rows, g=g: kcat_ref[g, rows, :]
            v_of = lambda rows, g=g: vaug_ref[g, rows, :]
            units.append((jnp.where(lane < DIFF_DH, q, 0.0).astype(BF16), k_of, v_of))
            units.append((jnp.where(lane < DIFF_DH, 0.0, q).astype(BF16), k_of, v_of))
    res = _softmax_pv(units, s_ref, n_keys)
    for i, (r0, g) in enumerate(slots):
        r1, r2 = res[2 * i], res[2 * i + 1]
        o = r1[:, :LANES] / r1[:, LANES:] - (lam / r2[:, LANES:]) * r2[:, :LANES]
        o_ref[r0:r0 + ur, g * LANES:(g + 1) * LANES] = (
            _rms(o, sub_ref[...]) * (1.0 - lambda_init)).astype(BF16)


def _diff_attn(prev, q, k_new, v_new, k_cache, v_cache, lam_p, subln, *, tq, heads, n_seq, seq_q, q_row0,
               lambda_init):
    nq = seq_q // tq
    qb0 = q_row0 // tq
    sb0 = q_row0 // seq_q
    d = 2 * DIFF_DH
    new_spec = pl.BlockSpec((seq_q, heads * d), lambda s, h, i: (sb0 + s, h))
    if k_cache is None:
        n_seg, k_args, v_args, k_specs, v_specs = 1, [k_new], [v_new], [new_spec], [new_spec]
        n_keys = seq_q
    else:
        past = k_cache.shape[2]
        c_spec = pl.BlockSpec((None, heads, past, d), lambda s, h, i: (s, h, 0, 0))
        n_seg, k_args, v_args = 2, [k_cache, k_new], [v_cache, v_new]
        k_specs, v_specs = [c_spec, new_spec], [c_spec, new_spec]
        n_keys = past + seq_q
    return pl.pallas_call(
        functools.partial(_diff_attn_body, n_seg=n_seg, lambda_init=lambda_init),
        grid=(n_seq, DIFF_HEADS // heads, nq),
        in_specs=[pl.BlockSpec(memory_space=pl.ANY),
                  pl.BlockSpec((tq, heads * d), lambda s, h, i: (qb0 + s * nq + i, h)),
                  pl.BlockSpec((4, DIFF_DH), lambda s, h, i: (0, 0)),
                  pl.BlockSpec((1, d), lambda s, h, i: (0, 0))] + k_specs + v_specs,
        out_specs=pl.BlockSpec((tq, heads * d), lambda s, h, i: (qb0 + s * nq + i, h)),
        out_shape=jax.ShapeDtypeStruct(prev.shape, prev.dtype),
        input_output_aliases={0: 0},
        scratch_shapes=[pltpu.VMEM((heads, n_keys, d), BF16), pltpu.VMEM((heads, n_keys, 2 * LANES), BF16),
                        pltpu.VMEM((2, min(tq, ATT_UNIT_ROWS), n_keys), F32)],
        compiler_params=_params("arbitrary", "arbitrary", "arbitrary"),
        name="diff_attn",
    )(prev, q, lam_p, subln.reshape(1, d), *k_args, *v_args)


def _even_out_body(xa_ref, xb_ref, mod_ref, ca_ref, cb_ref, u_ref, x0_ref, db_ref, o_ref, wa_ref, wb_ref,
                   out_ref, *, na):
    first = pl.program_id(0) < na
    x = jnp.where(first, xa_ref[...], xb_ref[...])
    conv = jnp.where(first, ca_ref[...], cb_ref[...])
    y_hy = (conv + u_ref[...] * db_ref[...]) * x0_ref[...]
    acc = _dot(y_hy.astype(BF16), wa_ref[...]) + _dot(o_ref[...], wb_ref[...])
    out_ref[...] = x + mod_ref[2:3, :] * acc


def _even_out(xa, xb, mod, conv_a, conv_b, u, x0, dbias, o, w_hy, w_att, *, tm, seg_len):
    d = xa.shape[1]
    t = xa.shape[0] + xb.shape[0]
    c = u.shape[1]
    xa_spec, xb_spec, na = _two_part_specs(xa, xb, tm)
    ca_spec, cb_spec, na_c = _two_part_specs(conv_a, conv_b, tm)
    assert na == na_c
    row = lambda cols: pl.BlockSpec((tm, cols), lambda i: (i, 0))
    full = lambda a: pl.BlockSpec(a.shape, lambda i: (0, 0))
    return pl.pallas_call(
        functools.partial(_even_out_body, na=na),
        grid=(t // tm,),
        in_specs=[xa_spec, xb_spec,
                  pl.BlockSpec((None, 6, d), lambda i: (i * tm // seg_len, 0, 0)),
                  ca_spec, cb_spec, row(c), row(c), full(dbias), row(o.shape[1]), full(w_hy), full(w_att)],
        out_specs=row(d),
        out_shape=jax.ShapeDtypeStruct((t, d), F32),
        compiler_params=_params("arbitrary"),
        name="even_out",
    )(xa, xb, mod, conv_a, conv_b, u, x0, dbias, o, w_hy, w_att)


def _ffn_body(x_ref, mod_ref, g_ref, wg_ref, wu_ref, wd_ref, o_ref, h_ref, acc_ref):
    f = pl.program_id(1)

    @pl.when(f == 0)
    def _():
        h_ref[...] = _norm_mod(x_ref[...], g_ref[...], mod_ref[3:4, :], mod_ref[4:5, :]).astype(BF16)
        acc_ref[...] = jnp.zeros_like(acc_ref)

    h = h_ref[...]
    a = _silu(_dot(h, wg_ref[...])) * _dot(h, wu_ref[...])
    acc_ref[...] += _dot(a.astype(BF16), wd_ref[...])

    @pl.when(f == pl.num_programs(1) - 1)
    def _():
        o_ref[...] = x_ref[...] + mod_ref[5:6, :] * acc_ref[...]


def _ffn(x, mod, g, wg, wu, wd, *, tm, tf, seg_len):
    t, d = x.shape
    ff = wg.shape[1]
    return pl.pallas_call(
        _ffn_body,
        grid=(t // tm, ff // tf),
        in_specs=[pl.BlockSpec((tm, d), lambda i, f: (i, 0)),
                  pl.BlockSpec((None, 6, d), lambda i, f: (i * tm // seg_len, 0, 0)),
                  pl.BlockSpec((1, d), lambda i, f: (0, 0)),
                  pl.BlockSpec((d, tf), lambda i, f: (0, f)),
                  pl.BlockSpec((d, tf), lambda i, f: (0, f)),
                  pl.BlockSpec((tf, d), lambda i, f: (f, 0))],
        out_specs=pl.BlockSpec((tm, d), lambda i, f: (i, 0)),
        out_shape=jax.ShapeDtypeStruct((t, d), F32),
        scratch_shapes=[pltpu.VMEM((tm, d), BF16), pltpu.VMEM((tm, d), F32)],
        compiler_params=_params("parallel", "arbitrary"),
        name="ffn",
    )(x, mod, g.reshape(1, d), wg, wu, wd)


def _group_ms(x, gmat):
    hi, lo = _split_bf16(x * x)
    return (_dot(hi, gmat) + _dot(lo, gmat)) * (1.0 / DIFF_DH)


def _qkv_body(x_ref, mod_ref, g_ref, w_ref, gm_ref, qn_ref, kn_ref, cos_ref, sa_ref, sb_ref,
              q_ref, k_ref, v_ref, kf_ref, vf_ref, *, scale, seq):
    h = _norm_mod(x_ref[...], g_ref[...], mod_ref[0:1, :], mod_ref[1:2, :]).astype(BF16)
    z = _dot(h, w_ref[...])
    hd = DIFF_HEADS * 2 * DIFF_DH
    tm = z.shape[0]
    cos_t, sin_a, sin_b = cos_ref[...], sa_ref[...], sb_ref[...]
    gm = gm_ref[...]
    shift = DIFF_DH // 4
    for hh in range(DIFF_HEADS):
        sl = slice(hh * LANES, (hh + 1) * LANES)
        qh = z[:, hh * LANES:(hh + 1) * LANES]
        qh = qh * lax.rsqrt(_group_ms(qh, gm) + NORM_EPS) * qn_ref[...]
        q_ref[:, sl] = (_rope(qh, cos_t, sin_a, sin_b, shift) * scale).astype(BF16)
        kh = z[:, hd + hh * LANES:hd + (hh + 1) * LANES]
        kh = kh * lax.rsqrt(_group_ms(kh, gm) + NORM_EPS) * kn_ref[...]
        k_ref[:, sl] = _rope(kh, cos_t, sin_a, sin_b, shift).astype(BF16)
        vh = z[:, 2 * hd + hh * LANES:2 * hd + (hh + 1) * LANES]
        for s in range(tm // seq):
            kf_ref[s, hh] = kh[s * seq:(s + 1) * seq, :]
            vf_ref[s, hh] = vh[s * seq:(s + 1) * seq, :]
    v_ref[...] = z[:, 2 * hd:].astype(BF16)


def _qkv(x, mod, p, tables, *, tm, seg_len, t_lat, lat_seq, n_ctx, ctx_seq):
    t, d = x.shape
    hd = DIFF_HEADS * 2 * DIFF_DH
    wqk = p['w_qkv'][:, :2 * hd].reshape(d, 2, 2, DIFF_HEADS, DIFF_DH)
    wqk = wqk.transpose(0, 1, 3, 2, 4).reshape(d, 2 * hd)
    w = jnp.concatenate([wqk, p['w_qkv'][:, 2 * hd:]], axis=1).astype(BF16)
    gi = np.arange(LANES) // DIFF_DH
    gmat = jnp.asarray((gi[:, None] == gi[None, :]).astype(np.float32)).astype(BF16)
    cos_t, sin_a, sin_b = tables
    qn = jnp.tile(p['q_norm'], 2).reshape(1, LANES)
    kn = jnp.tile(p['k_norm'], 2).reshape(1, LANES)
    pos_blocks = lat_seq // tm
    lat_tiles = t_lat // tm
    seq_per_tile = tm // ctx_seq
    tspec = pl.BlockSpec((tm, LANES), lambda i: (jnp.where(i < lat_tiles, i % pos_blocks, pos_blocks), 0))
    full = lambda i: (0, 0)
    row = pl.BlockSpec((tm, hd), lambda i: (i, 0))
    fspec = pl.BlockSpec((seq_per_tile, DIFF_HEADS, ctx_seq, LANES),
                         lambda i: (jnp.maximum(i - lat_tiles, 0), 0, 0, 0))
    fshape = jax.ShapeDtypeStruct((n_ctx, DIFF_HEADS, ctx_seq, LANES), F32)
    return pl.pallas_call(
        functools.partial(_qkv_body, scale=DIFF_DH ** -0.5 * LOG2_E, seq=ctx_seq),
        grid=(t // tm,),
        in_specs=[pl.BlockSpec((tm, d), lambda i: (i, 0)),
                  pl.BlockSpec((None, 6, d), lambda i: (i * tm // seg_len, 0, 0)),
                  pl.BlockSpec((1, d), full),
                  pl.BlockSpec((d, 3 * hd), full),
                  pl.BlockSpec((LANES, LANES), full),
                  pl.BlockSpec((1, LANES), full), pl.BlockSpec((1, LANES), full),
                  tspec, tspec, tspec],
        out_specs=[row, row, row, fspec, fspec],
        out_shape=[jax.ShapeDtypeStruct((t, hd), BF16)] * 3 + [fshape, fshape],
        compiler_params=_params("arbitrary"),
        name="qkv",
    )(x, mod, p['norm1'].reshape(1, d), w, gmat, qn, kn, cos_t, sin_a, sin_b)


def _route(logits):
    lane = lax.broadcasted_iota(jnp.int32, logits.shape, 1)
    neg = jnp.float32(-jnp.inf)
    lg = jnp.where(lane < N_EXPERTS, logits, neg)
    m1 = jnp.max(lg, axis=-1, keepdims=True)
    i1 = jnp.min(jnp.where(lg == m1, lane, LANES), axis=-1, keepdims=True)
    lg2 = jnp.where(lane == i1, neg, lg)
    m2 = jnp.max(lg2, axis=-1, keepdims=True)
    i2 = jnp.min(jnp.where(lg2 == m2, lane, LANES), axis=-1, keepdims=True)
    e = jnp.exp(m2 - m1)
    w1 = 1.0 / (1.0 + e)
    w2 = e / (1.0 + e)
    return jnp.where(lane == i1, w1, 0.0) + jnp.where(lane == i2, w2, 0.0)


MOE_BLOCK = 1024
MOE_SUB = 256
MOE_ROUTE_TM = 512
MOE_WINDOW = 5
MOE_TF = 512
MOE_CMB = 256
MOE_CMB_BLOCKS = MOE_SUB // MOE_CMB + 1


def _moe_route_body(x_ref, o_ref, wo_ref, mod_ref, g_ref, wr_ref, xo_ref, h_ref, gates_ref, rank_ref, rank_t_ref,
                    carry_row, carry_col, *, tm):
    i = pl.program_id(0)

    @pl.when(i == 0)
    def _():
        carry_row[...] = jnp.zeros_like(carry_row)
        carry_col[...] = jnp.zeros_like(carry_col)

    x = x_ref[...] + mod_ref[2:3, :] * _dot(o_ref[...], wo_ref[...])
    xo_ref[...] = x
    h = _norm_mod(x, g_ref[...], mod_ref[3:4, :], mod_ref[4:5, :])
    h_ref[...] = h.astype(BF16)
    gates = _route(_dot_f32(h, wr_ref[...]))
    gates_ref[...] = gates
    sel = jnp.where(gates != 0.0, 1.0, 0.0)
    sel_t = sel.T
    r = lax.broadcasted_iota(jnp.int32, (tm, tm), 0)
    c = lax.broadcasted_iota(jnp.int32, (tm, tm), 1)
    lower = jnp.where(c < r, 1.0, 0.0).astype(BF16)
    upper = jnp.where(r < c, 1.0, 0.0).astype(BF16)
    before = _dot(lower, sel.astype(BF16)) + carry_row[...]
    before_t = _dot(sel_t.astype(BF16), upper) + carry_col[...]
    rank_ref[...] = jnp.where(sel > 0.0, before, -1.0)
    rank_t = jnp.where(sel_t > 0.0, before_t, -1.0)
    for s in range(tm // MOE_SUB):
        rank_t_ref[s] = rank_t[:SUBLANES, s * MOE_SUB:(s + 1) * MOE_SUB]
    carry_row[...] += jnp.sum(sel, axis=0, keepdims=True)
    carry_col[...] += jnp.sum(sel_t, axis=1, keepdims=True)


def _moe_route(x, o, w_out, mod, g, w_router, *, seg_len):
    t, d = x.shape
    tm = MOE_ROUTE_TM
    ne = w_router.shape[1]
    assert ne <= SUBLANES
    wr = jnp.zeros((d, LANES), F32).at[:, :ne].set(w_router)
    sub = tm // MOE_SUB
    row = lambda cols: pl.BlockSpec((tm, cols), lambda i: (i, 0))
    full = lambda a: pl.BlockSpec(a.shape, lambda i: (0, 0))
    return pl.pallas_call(
        functools.partial(_moe_route_body, tm=tm),
        grid=(t // tm,),
        in_specs=[row(d), row(o.shape[1]), full(w_out),
                  pl.BlockSpec((None, 6, d), lambda i: (i * tm // seg_len, 0, 0)),
                  pl.BlockSpec((1, d), lambda i: (0, 0)),
                  full(wr)],
        out_specs=[row(d), row(d), row(LANES), row(LANES),
                   pl.BlockSpec((sub, SUBLANES, MOE_SUB), lambda i: (i, 0, 0))],
        out_shape=[jax.ShapeDtypeStruct((t, d), F32),
                   jax.ShapeDtypeStruct((t, d), BF16),
                   jax.ShapeDtypeStruct((t, LANES), F32),
                   jax.ShapeDtypeStruct((t, LANES), F32),
                   jax.ShapeDtypeStruct((t // MOE_SUB, SUBLANES, MOE_SUB), F32)],
        scratch_shapes=[pltpu.VMEM((1, LANES), F32), pltpu.VMEM((LANES, 1), F32)],
        compiler_params=_params("arbitrary"),
        name="moe_route",
    )(x, o, w_out, mod, g.reshape(1, d), wr)


def _moe_plan(rank, ne, *, n_blocks):
    t = rank.shape[0]
    n_tiles = t // MOE_SUB
    per_blk = MOE_BLOCK // MOE_SUB
    n_sub = n_blocks * per_blk
    sel = (rank[:, :ne] >= 0.0).astype(jnp.int32)
    tile_cnt = sel.reshape(n_tiles, MOE_SUB, ne).sum(axis=1)
    tile_end = jnp.cumsum(tile_cnt, axis=0)
    tile_start = tile_end - tile_cnt
    cnt = tile_end[-1]
    nblk = (cnt + MOE_BLOCK - 1) // MOE_BLOCK
    bend = jnp.cumsum(nblk)
    bstart = bend - nblk
    e_last = jnp.max(jnp.where(cnt > 0, jnp.arange(ne), 0))
    b = jnp.arange(n_blocks)
    blk_valid = b < bend[-1]
    blk_e = jnp.minimum(jnp.sum(bend[None, :] <= b[:, None], axis=1), e_last).astype(jnp.int32)
    blk_r0 = (b - bstart[blk_e]) * MOE_BLOCK
    blk_rows = jnp.where(blk_valid, jnp.clip(cnt[blk_e] - blk_r0, 0, MOE_BLOCK), 0).astype(jnp.int32)
    j = jnp.arange(n_sub)
    sub_e = blk_e[j // per_blk]
    sub_r0 = blk_r0[j // per_blk] + (j % per_blk) * MOE_SUB
    sub_valid = jnp.logical_and(blk_valid[j // per_blk], sub_r0 < cnt[sub_e])
    ends = tile_end[:, sub_e]
    r1 = jnp.minimum(sub_r0 + MOE_SUB, cnt[sub_e])
    c_lo = jnp.sum(ends <= sub_r0[None, :], axis=0)
    c_hi = jnp.sum(ends < r1[None, :], axis=0)
    c_lo = jnp.where(sub_valid, c_lo, 1).astype(jnp.int32)
    c_hi = jnp.where(sub_valid, jnp.minimum(c_hi, n_tiles - 1), 0).astype(jnp.int32)
    base = (bstart * MOE_BLOCK).astype(jnp.int32)
    n_cmb = n_sub * (MOE_SUB // MOE_CMB)
    j0 = jnp.minimum((base[None, :] + tile_start) // MOE_CMB, n_cmb - MOE_CMB_BLOCKS).astype(jnp.int32)
    return dict(blk_e=blk_e, blk_valid=blk_valid.astype(jnp.int32), blk_rows=blk_rows,
                sub_e=sub_e.astype(jnp.int32), sub_r0=sub_r0.astype(jnp.int32), c_lo=c_lo, c_hi=c_hi,
                base=base, j0=j0.reshape(-1))


def _moe_dispatch_body(e_ref, r0_ref, lo_ref, hi_ref, h_ref, rank_t_ref, xs_ref, acc_ref):
    j = pl.program_id(0)
    e = e_ref[j]
    rows = (r0_ref[j] + lax.broadcasted_iota(jnp.int32, (MOE_SUB, 1), 0)).astype(F32)
    sub = lax.broadcasted_iota(jnp.int32, (SUBLANES, MOE_SUB), 0)
    n_tiles = rank_t_ref.shape[0]
    lo, hi = lo_ref[j], hi_ref[j]
    acc_ref[...] = jnp.zeros_like(acc_ref)

    def step(w, carry):
        first = lo + w * MOE_WINDOW
        c0 = jnp.minimum(first, n_tiles - MOE_WINDOW)
        pieces = []
        for i in range(MOE_WINDOW):
            c = c0 + i
            rk = jnp.sum(jnp.where(sub == e, rank_t_ref[c], 0.0), axis=0, keepdims=True)
            rk = jnp.where(c >= first, rk, -1.0)
            pieces.append(jnp.where(rk == rows, 1.0, 0.0).astype(BF16))
        onehot = jnp.concatenate(pieces, axis=1)
        off = pl.multiple_of(c0 * MOE_SUB, MOE_SUB)
        acc_ref[...] += _dot(onehot, h_ref[pl.ds(off, MOE_WINDOW * MOE_SUB), :])
        return carry

    lax.fori_loop(0, (hi - lo + MOE_WINDOW) // MOE_WINDOW, step, 0)
    xs_ref[...] = acc_ref[...].astype(BF16)


def _moe_dispatch(h, rank_t, plan, *, n_sub):
    t, d = h.shape
    grid_spec = pltpu.PrefetchScalarGridSpec(
        num_scalar_prefetch=4,
        grid=(n_sub,),
        in_specs=[pl.BlockSpec((t, d), lambda j, *_: (0, 0), pipeline_mode=pl.Buffered(1)),
                  pl.BlockSpec(rank_t.shape, lambda j, *_: (0, 0, 0), pipeline_mode=pl.Buffered(1))],
        out_specs=pl.BlockSpec((MOE_SUB, d), lambda j, *_: (j, 0)),
        scratch_shapes=[pltpu.VMEM((MOE_SUB, d), F32)],
    )
    return pl.pallas_call(
        _moe_dispatch_body,
        grid_spec=grid_spec,
        out_shape=jax.ShapeDtypeStruct((n_sub * MOE_SUB, d), BF16),
        compiler_params=_params("arbitrary"),
        name="moe_dispatch",
    )(plan['sub_e'], plan['sub_r0'], plan['c_lo'], plan['c_hi'], h, rank_t)


def _moe_ffn_body(e_ref, valid_ref, rows_ref, xs_ref, wg_ref, wu_ref, wd_ref, y_ref, acc_ref):
    b = pl.program_id(0)
    f = pl.program_id(1)
    n_rows = rows_ref[b]
    last = f == pl.num_programs(1) - 1
    wg = wg_ref[...].astype(BF16)
    wu = wu_ref[...].astype(BF16)
    wd = wd_ref[...].astype(BF16)
    full = n_rows == MOE_BLOCK

    def swiglu(h):
        a = _silu(_dot(h, wg)) * _dot(h, wu)
        return _dot(a.astype(BF16), wd)

    @pl.when(jnp.logical_and(full, f == 0))
    def _():
        acc_ref[...] = swiglu(xs_ref[...])

    @pl.when(jnp.logical_and(full, f > 0))
    def _():
        acc_ref[...] += swiglu(xs_ref[...])

    @pl.when(jnp.logical_and(full, last))
    def _():
        y_ref[...] = acc_ref[...].astype(BF16)

    for s in range(MOE_BLOCK // MOE_SUB):
        sl = slice(s * MOE_SUB, (s + 1) * MOE_SUB)
        live = jnp.logical_and(jnp.logical_not(full), s * MOE_SUB < n_rows)
        dead = jnp.logical_and(jnp.logical_not(full), s * MOE_SUB >= n_rows)

        @pl.when(jnp.logical_and(live, f == 0))
        def _():
            acc_ref[sl, :] = jnp.zeros((MOE_SUB, acc_ref.shape[1]), F32)

        @pl.when(live)
        def _():
            acc_ref[sl, :] += swiglu(xs_ref[sl, :])

        @pl.when(jnp.logical_and(live, last))
        def _():
            y_ref[sl, :] = acc_ref[sl, :].astype(BF16)

        @pl.when(jnp.logical_and(dead, last))
        def _():
            y_ref[sl, :] = jnp.zeros((MOE_SUB, y_ref.shape[1]), BF16)


def _moe_ffn(xs, wg, wu, wd, plan, *, n_blocks, tf):
    _, d = xs.shape
    ne, _, ff = wg.shape
    nf = ff // tf

    def w_in(b, f, e_ref, valid_ref, rows_ref):
        return (e_ref[b], 0, jnp.where(valid_ref[b] > 0, f, nf - 1))

    def w_down(b, f, e_ref, valid_ref, rows_ref):
        return (e_ref[b], jnp.where(valid_ref[b] > 0, f, nf - 1), 0)

    grid_spec = pltpu.PrefetchScalarGridSpec(
        num_scalar_prefetch=3,
        grid=(n_blocks, nf),
        in_specs=[pl.BlockSpec((MOE_BLOCK, d), lambda b, f, *_: (b, 0)),
                  pl.BlockSpec((None, d, tf), w_in),
                  pl.BlockSpec((None, d, tf), w_in),
                  pl.BlockSpec((None, tf, d), w_down)],
        out_specs=pl.BlockSpec((MOE_BLOCK, d), lambda b, f, *_: (b, 0)),
        scratch_shapes=[pltpu.VMEM((MOE_BLOCK, d), F32)],
    )
    return pl.pallas_call(
        _moe_ffn_body,
        grid_spec=grid_spec,
        out_shape=jax.ShapeDtypeStruct((n_blocks * MOE_BLOCK, d), BF16),
        compiler_params=_params("arbitrary", "arbitrary"),
        name="moe_ffn",
    )(plan['blk_e'], plan['blk_valid'], plan['blk_rows'], xs, wg, wu, wd)


def _moe_combine_body(j0_ref, base_ref, x_ref, mod_ref, gates_ref, rank_ref, *rest, ne, split_tiles):
    nblk = MOE_CMB_BLOCKS
    y_refs, o_ref, o2_ref = rest[:nblk * ne], rest[nblk * ne], rest[nblk * ne + 1]
    c = pl.program_id(0)
    gates = gates_ref[...]
    rank = rank_ref[...]
    lane = lax.broadcasted_iota(jnp.int32, gates.shape, 1)
    col = lax.broadcasted_iota(jnp.int32, (1, MOE_CMB), 1).astype(F32)
    acc = None
    for e in range(ne):
        pick = lane == e
        g = jnp.sum(jnp.where(pick, gates, 0.0), axis=-1, keepdims=True)
        rk = jnp.sum(jnp.where(pick, rank, 0.0), axis=-1, keepdims=True)
        shift = (base_ref[e] - j0_ref[c * ne + e] * MOE_CMB).astype(F32)
        loc = jnp.where(rk >= 0.0, rk + shift, -1.0)
        picked = None
        for b in range(nblk):
            onehot = jnp.where(loc == col + float(b * MOE_CMB), 1.0, 0.0).astype(BF16)
            d = _dot(onehot, y_refs[nblk * e + b][...])
            picked = d if picked is None else picked + d
        contrib = g * picked
        acc = contrib if acc is None else acc + contrib
    out = x_ref[...] + mod_ref[5:6, :] * acc

    @pl.when(c < split_tiles)
    def _():
        o_ref[...] = out

    @pl.when(c >= split_tiles)
    def _():
        o2_ref[...] = out


def _moe_combine(x, mod, gates, rank, y, plan, *, ne, seg_len, t_split):
    t, d = x.shape
    tm = MOE_SUB
    split_tiles = t_split // tm
    y_specs = []
    for e in range(ne):
        for b in range(MOE_CMB_BLOCKS):
            y_specs.append(pl.BlockSpec((MOE_CMB, d), lambda c, j0, base, e=e, b=b: (j0[c * ne + e] + b, 0)))
    grid_spec = pltpu.PrefetchScalarGridSpec(
        num_scalar_prefetch=2,
        grid=(t // tm,),
        in_specs=[pl.BlockSpec((tm, d), lambda c, *_: (c, 0)),
                  pl.BlockSpec((None, 6, d), lambda c, *_: (c * tm // seg_len, 0, 0)),
                  pl.BlockSpec((tm, LANES), lambda c, *_: (c, 0)),
                  pl.BlockSpec((tm, LANES), lambda c, *_: (c, 0))] + y_specs,
        out_specs=[pl.BlockSpec((tm, d), lambda c, *_: (jnp.minimum(c, split_tiles - 1), 0)),
                   pl.BlockSpec((tm, d), lambda c, *_: (jnp.maximum(c - split_tiles, 0), 0))],
    )
    return pl.pallas_call(
        functools.partial(_moe_combine_body, ne=ne, split_tiles=split_tiles),
        grid_spec=grid_spec,
        out_shape=[jax.ShapeDtypeStruct((t_split, d), F32), jax.ShapeDtypeStruct((t - t_split, d), F32)],
        compiler_params=_params("arbitrary"),
        name="moe_combine",
    )(plan['j0'], plan['base'], x, mod, gates, rank, *([y] * (MOE_CMB_BLOCKS * ne)))


def _attn_out_moe(x, o, w_out, mod, g, w_router, wg, wu, wd, *, seg_len, t_split, top_k=2):
    t, d = x.shape
    ne = w_router.shape[1]
    n_blocks = t * top_k // MOE_BLOCK + ne
    x, h, gates, rank, rank_t = _moe_route(x, o, w_out, mod, g, w_router, seg_len=seg_len)
    plan = _moe_plan(rank, ne, n_blocks=n_blocks)
    xs = _moe_dispatch(h, rank_t, plan, n_sub=n_blocks * (MOE_BLOCK // MOE_SUB))
    y = _moe_ffn(xs, wg, wu, wd, plan, n_blocks=n_blocks, tf=MOE_TF)
    return _moe_combine(x, mod, gates, rank, y, plan, ne=ne, seg_len=seg_len, t_split=t_split)


def _even_layer(x_lat, x_ctx, cond8, p, cache_ckv, cache_kr, *, nb, lat_seq, n_ctx, ctx_seq):
    d = x_lat.shape[1]
    t_lat = nb * lat_seq
    t = t_lat + x_ctx.shape[0]
    seg_len = lat_seq
    past = cache_ckv.shape[1]
    mod = _adaln(cond8, p['w_mod'], p['b_mod'])

    tm = 512
    tables = _rope_tables(lat_seq, MLA_ROPE, (MLA_NOPE,), tm)
    z_hy, kr, q, ckvn = _even_in(x_lat, x_ctx, mod, p, tables, tm=tm, seg_len=seg_len, t_lat=t_lat,
                                 lat_seq=lat_seq)

    u, x0 = _hy_pre(z_hy, p['hy_conv_w'], p['hy_conv_b'], t_lat=t_lat, lat_seq=lat_seq, ctx_seq=ctx_seq)
    k_lat, k_ctx, k_sum = _hyena_filters(p, lat_seq=lat_seq, ctx_seq=ctx_seq)
    dbias = p['hy_dbias'].reshape(1, HY_WIDTH)
    conv_lat = _hyena_lat(k_lat, k_sum[0:1], u, seq=lat_seq, nb=nb)
    conv_ctx = _hyena_ctx(k_ctx, k_sum[1:2], u, seq=ctx_seq, nseq=n_ctx, u_row0=t_lat)

    cache_kr_p = jnp.zeros((nb, past, LANES), F32).at[:, :, :MLA_ROPE].set(cache_kr)
    ckvn_rows = jnp.concatenate(
        [jnp.concatenate([cache_ckv, ckvn[:t_lat].reshape(nb, lat_seq, -1)], axis=1).reshape(nb * (past + lat_seq), -1),
         ckvn[t_lat:]], axis=0)
    kr_rows = jnp.concatenate(
        [jnp.concatenate([cache_kr_p, kr[:t_lat].reshape(nb, lat_seq, LANES)], axis=1).reshape(nb * (past + lat_seq), LANES),
         kr[t_lat:]], axis=0)
    k_all, v_all = _mla_kv(ckvn_rows, kr_rows, p, tables, tm=tm, nb=nb, past=past, lat_seq=lat_seq)
    o = jnp.zeros((t, MLA_HEADS * MLA_V), BF16)
    o = _mla_attn(o, q, k_all, v_all, tq=ATT_TQ, pairs=1, n_seq=nb, seq_q=lat_seq, seq_k=past + lat_seq,
                  q_row0=0, k_row0=0)
    o = _mla_attn(o, q, k_all, v_all, tq=ctx_seq, pairs=MLA_HEADS // 2, n_seq=n_ctx, seq_q=ctx_seq,
                  seq_k=ctx_seq, q_row0=t_lat, k_row0=nb * (past + lat_seq))

    w_out = p['w_out'].astype(BF16)
    x = _even_out(x_lat, x_ctx, mod, conv_lat, conv_ctx, u, x0, dbias, o, w_out[:HY_WIDTH], w_out[HY_WIDTH:],
                  tm=512, seg_len=seg_len)
    x = _ffn(x, mod, p['norm2'], p['ffn_w_gate'].astype(BF16), p['ffn_w_up'].astype(BF16),
             p['ffn_w_down'].astype(BF16), tm=512, tf=1408, seg_len=seg_len)
    new_ckv = ckvn[t_lat:].reshape(n_ctx, ctx_seq, -1)
    new_kr = kr[t_lat:, :MLA_ROPE].reshape(n_ctx, ctx_seq, MLA_ROPE)
    return x, new_ckv, new_kr


def _odd_layer(x, cond8, p, cache_k, cache_v, lambda_init, *, nb, lat_seq, n_ctx, ctx_seq):
    t, d = x.shape
    t_lat = nb * lat_seq
    seg_len = lat_seq
    mod = _adaln(cond8, p['w_mod'], p['b_mod'])
    tm = 512
    tables = _rope_tables(lat_seq, DIFF_DH, (0, DIFF_DH), tm)
    q, k, v, new_k, new_v = _qkv(x, mod, p, tables, tm=tm, seg_len=seg_len, t_lat=t_lat, lat_seq=lat_seq,
                                 n_ctx=n_ctx, ctx_seq=ctx_seq)
    lam_p = jnp.stack([p['lam_q1'], p['lam_k1'], p['lam_q2'], p['lam_k2']])
    o = jnp.zeros((t, DIFF_HEADS * 2 * DIFF_DH), BF16)
    o = _diff_attn(o, q, k, v, cache_k.astype(BF16), cache_v.astype(BF16), lam_p, p['subln'],
                   tq=ATT_TQ, heads=1, n_seq=nb, seq_q=lat_seq, q_row0=0, lambda_init=lambda_init)
    o = _diff_attn(o, q, k, v, None, None, lam_p, p['subln'],
                   tq=ctx_seq, heads=DIFF_HEADS, n_seq=n_ctx, seq_q=ctx_seq, q_row0=t_lat,
                   lambda_init=lambda_init)
    x_lat, x_ctx = _attn_out_moe(x, o, p['w_out'].astype(BF16), mod, p['norm2'], p['w_router'],
                                 p['moe_w_gate'], p['moe_w_up'], p['moe_w_down'], seg_len=seg_len, t_split=t_lat)
    return x_lat, x_ctx, new_k, new_v


def kernel(x_prompt, x_sample, cache_l0_ckv, cache_l0_krope, cache_l1_k, cache_l1_v, c, c_ctx,
           l0_w_mod, l0_b_mod, l0_norm1, l0_norm2, l0_w_in, l0_hy_conv_w, l0_hy_conv_b,
           l0_hy_fw1, l0_hy_fb1, l0_hy_freq1, l0_hy_fw2, l0_hy_fb2, l0_hy_freq2, l0_hy_fw3, l0_hy_dbias,
           l0_mla_qa_norm, l0_mla_w_uq, l0_mla_kva_norm, l0_mla_w_ukv, l0_mla_q_norm, l0_mla_k_norm,
           l0_w_out, l0_ffn_w_gate, l0_ffn_w_up, l0_ffn_w_down,
           l1_w_mod, l1_b_mod, l1_norm1, l1_norm2, l1_w_qkv, l1_q_norm, l1_k_norm,
           l1_lam_q1, l1_lam_k1, l1_lam_q2, l1_lam_k2, l1_subln, l1_w_out,
           l1_w_router, l1_moe_w_gate, l1_moe_w_up, l1_moe_w_down):
    even = {
        'w_mod': l0_w_mod, 'b_mod': l0_b_mod, 'norm1': l0_norm1, 'norm2': l0_norm2, 'w_in': l0_w_in,
        'hy_conv_w': l0_hy_conv_w, 'hy_conv_b': l0_hy_conv_b, 'hy_fw1': l0_hy_fw1, 'hy_fb1': l0_hy_fb1,
        'hy_freq1': l0_hy_freq1, 'hy_fw2': l0_hy_fw2, 'hy_fb2': l0_hy_fb2, 'hy_freq2': l0_hy_freq2,
        'hy_fw3': l0_hy_fw3, 'hy_dbias': l0_hy_dbias, 'qa_norm': l0_mla_qa_norm, 'w_uq': l0_mla_w_uq,
        'kva_norm': l0_mla_kva_norm, 'w_ukv': l0_mla_w_ukv, 'q_norm': l0_mla_q_norm, 'k_norm': l0_mla_k_norm,
        'w_out': l0_w_out, 'ffn_w_gate': l0_ffn_w_gate, 'ffn_w_up': l0_ffn_w_up, 'ffn_w_down': l0_ffn_w_down,
    }
    odd = {
        'w_mod': l1_w_mod, 'b_mod': l1_b_mod, 'norm1': l1_norm1, 'norm2': l1_norm2, 'w_qkv': l1_w_qkv,
        'q_norm': l1_q_norm, 'k_norm': l1_k_norm, 'lam_q1': l1_lam_q1, 'lam_k1': l1_lam_k1,
        'lam_q2': l1_lam_q2, 'lam_k2': l1_lam_k2, 'subln': l1_subln, 'w_out': l1_w_out,
        'w_router': l1_w_router, 'moe_w_gate': l1_moe_w_gate, 'moe_w_up': l1_moe_w_up,
        'moe_w_down': l1_moe_w_down,
    }
    n_ctx, ctx_seq, d = x_prompt.shape
    nb, lat_seq, _ = x_sample.shape
    assert n_ctx * ctx_seq == lat_seq, "segment layout needs equally sized modulation segments"
    dims = dict(nb=nb, lat_seq=lat_seq, n_ctx=n_ctx, ctx_seq=ctx_seq)
    t_lat = nb * lat_seq
    cond8 = jnp.zeros((SUBLANES, d), F32).at[:nb].set(c).at[nb].set(c_ctx)

    x, new_l0_ckv, new_l0_krope = _even_layer(x_sample.reshape(t_lat, d), x_prompt.reshape(n_ctx * ctx_seq, d),
                                              cond8, even, cache_l0_ckv, cache_l0_krope, **dims)
    lambda_init = 0.8 - 0.6 * math.exp(-0.3 * 1)
    x_lat, x_ctx, new_l1_k, new_l1_v = _odd_layer(x, cond8, odd, cache_l1_k, cache_l1_v, lambda_init, **dims)

    y_sample = x_lat.reshape(nb, lat_seq, d)
    y_prompt = x_ctx.reshape(n_ctx, ctx_seq, d)
    return (y_prompt, y_sample, new_l0_ckv, new_l0_krope, new_l1_k, new_l1_v)
```

```python
import functools
import math

import numpy as np
import jax
import jax.numpy as jnp
from jax import lax
from jax.experimental import pallas as pl
from jax.experimental.pallas import tpu as pltpu

F32 = jnp.float32
BF16 = jnp.bfloat16

VMEM_LIMIT_BYTES = 56 * 1024 * 1024
LANES = 128
SUBLANES = 8
LOG2_E = math.log2(math.e)

GRID_W = 64
ROPE_BASE = 10000.0
NORM_EPS = 1e-6
HY_WIDTH = 512
HY_BANDS = 16
HY_FAST_DECAY_PCT = 0.3
HY_SLOW_DECAY_PCT = 1.5
HY_DECAY_TARGET = 1e-2
MLA_HEADS = 8
MLA_NOPE = 64
MLA_ROPE = 32
MLA_QK = MLA_NOPE + MLA_ROPE
MLA_V = 64
MLA_Q_RANK = 768
MLA_KV_RANK = 256
DIFF_HEADS = 8
DIFF_DH = 64
N_EXPERTS = 8


def _params(*sem):
    return pltpu.CompilerParams(dimension_semantics=sem, vmem_limit_bytes=VMEM_LIMIT_BYTES)


def _dot(a, b):
    return jnp.dot(a, b, preferred_element_type=F32)


def _dot_nt(a, b):
    return lax.dot_general(a, b, (((1,), (1,)), ((), ())), preferred_element_type=F32)


def _split_bf16(a):
    hi = a.astype(BF16)
    lo = (a - hi.astype(F32)).astype(BF16)
    return hi, lo


def _dot_f32(a, b):
    ah, al = _split_bf16(a)
    bh, bl = _split_bf16(b)
    return _dot(ah, bh) + (_dot(al, bh) + _dot(ah, bl))


def _rms(x, g, n=None):
    n = x.shape[-1] if n is None else n
    ms = jnp.sum(x * x, axis=-1, keepdims=True) * (1.0 / n)
    return x * lax.rsqrt(ms + NORM_EPS) * g


def _norm_mod(x, g, shift, scale):
    return _rms(x, g) * (1.0 + scale) + shift


def _silu(x):
    return x / (1.0 + jnp.exp(-x))


def _adaln_body(c_ref, w_ref, b_ref, o_ref):
    o_ref[...] = _dot_f32(_silu(c_ref[...]), w_ref[...]) + b_ref[...]


def _adaln(cond8, w_mod, b_mod):
    d, n = w_mod.shape
    tn = n // 4
    out = pl.pallas_call(
        _adaln_body,
        grid=(n // tn,),
        in_specs=[pl.BlockSpec((SUBLANES, d), lambda j: (0, 0)),
                  pl.BlockSpec((d, tn), lambda j: (0, j)),
                  pl.BlockSpec((1, tn), lambda j: (0, j))],
        out_specs=pl.BlockSpec((SUBLANES, tn), lambda j: (0, j)),
        out_shape=jax.ShapeDtypeStruct((SUBLANES, n), F32),
        compiler_params=_params("arbitrary"),
        name="adaln",
    )(cond8, w_mod, b_mod.reshape(1, n))
    return out.reshape(SUBLANES, 6, d)


def _two_part_specs(a, b, tm):
    na = a.shape[0] // tm
    cols = a.shape[1]
    return (pl.BlockSpec((tm, cols), lambda i: (jnp.minimum(i, na - 1), 0)),
            pl.BlockSpec((tm, cols), lambda i: (jnp.maximum(i - na, 0), 0)), na)


def _hy_pre_body(z_ref, zp_ref, zn_ref, w_ref, b_ref, u_ref, x0_ref, *, tm, lat_tiles, tiles_per_seq):
    i = pl.program_id(0)
    z = z_ref[...]
    in_lat = i < lat_tiles
    has_prev = jnp.logical_and(in_lat, i % tiles_per_seq != 0)
    has_next = jnp.logical_and(in_lat, i % tiles_per_seq != tiles_per_seq - 1)
    prev_row = jnp.where(has_prev, zp_ref[SUBLANES - 1:SUBLANES, :], 0.0)
    next_row = jnp.where(has_next, zn_ref[0:1, :], 0.0)
    rows = lax.broadcasted_iota(jnp.int32, z.shape, 0)
    z_m = jnp.where(rows == 0, prev_row, pltpu.roll(z, 1, 0))
    z_p = jnp.where(rows == tm - 1, next_row, pltpu.roll(z, tm - 1, 0))
    zc = b_ref[...] + z_m * w_ref[0:1, :] + z * w_ref[1:2, :] + z_p * w_ref[2:3, :]
    c = HY_WIDTH
    x0_ref[...] = zc[:, :c]
    u_ref[...] = zc[:, 2 * c:] * zc[:, c:2 * c]


def _hy_pre(z, conv_w, conv_b, *, t_lat, lat_seq, ctx_seq):
    t = z.shape[0]
    tm = ctx_seq
    c3 = 3 * HY_WIDTH
    nb8 = t // SUBLANES
    body = functools.partial(_hy_pre_body, tm=tm, lat_tiles=t_lat // tm, tiles_per_seq=lat_seq // tm)
    return pl.pallas_call(
        body,
        grid=(t // tm,),
        in_specs=[pl.BlockSpec((tm, c3), lambda i: (i, 0)),
                  pl.BlockSpec((SUBLANES, c3), lambda i: (jnp.maximum(i * (tm // SUBLANES) - 1, 0), 0)),
                  pl.BlockSpec((SUBLANES, c3), lambda i: (jnp.minimum((i + 1) * (tm // SUBLANES), nb8 - 1), 0)),
                  pl.BlockSpec((3, c3), lambda i: (0, 0)),
                  pl.BlockSpec((1, c3), lambda i: (0, 0))],
        out_specs=[pl.BlockSpec((tm, HY_WIDTH), lambda i: (i, 0)),
                   pl.BlockSpec((tm, HY_WIDTH), lambda i: (i, 0))],
        out_shape=[jax.ShapeDtypeStruct((t, HY_WIDTH), F32),
                   jax.ShapeDtypeStruct((t, HY_WIDTH), F32)],
        compiler_params=_params("parallel"),
        name="hy_pre",
    )(z, z, z, conv_w, conv_b.reshape(1, c3))


def _filter_embedding(seq):
    t01 = np.linspace(0.0, 1.0, seq)[:, None]
    w = 2.0 * math.pi * np.arange(seq)[:, None] / seq
    f = np.linspace(1e-4, HY_BANDS - 1, HY_BANDS)[None, :]
    z = np.concatenate([t01, np.cos(f * w), -np.sin(f * w)], axis=-1)
    z_rev = np.concatenate([z[:1], z[:0:-1]], axis=0)
    zz = np.concatenate([z, z_rev], axis=0)
    out = np.zeros((2 * seq, LANES), np.float32)
    out[:, :zz.shape[1]] = zz
    return out


def _filter_body(zz_ref, dl_ref, w1_ref, b1_ref, f1_ref, w2_ref, b2_ref, f2_ref, w3_ref,
                 kl_ref, kc_ref, s_ref, *, tm, lat_tiles, ctx_tiles):
    i = pl.program_id(0)
    zz = zz_ref[...]
    h = jnp.sin(f1_ref[...] * (_dot_f32(zz, w1_ref[...]) + b1_ref[...]))
    h = jnp.sin(f2_ref[...] * (_dot_f32(h, w2_ref[...]) + b2_ref[...]))
    h = _dot_f32(h, w3_ref[...])
    is_bwd = jnp.logical_or(jnp.logical_and(i >= lat_tiles // 2, i < lat_tiles),
                            i >= lat_tiles + ctx_tiles // 2)
    first_bwd = jnp.logical_or(i == lat_tiles // 2, i == lat_tiles + ctx_tiles // 2)
    window = jnp.exp(-zz[:, 0:1] * dl_ref[...])
    k = jnp.where(is_bwd, h[:, HY_WIDTH:], h[:, :HY_WIDTH]) * window
    rows = lax.broadcasted_iota(jnp.int32, k.shape, 0)
    k = jnp.where(jnp.logical_and(first_bwd, rows == 0), 0.0, k)
    s = jnp.sum(jnp.abs(k), axis=0, keepdims=True)

    @pl.when(i == 0)
    def _():
        s_ref[...] = jnp.zeros_like(s_ref)

    @pl.when(i < lat_tiles)
    def _():
        kl_ref[...] = k
        s_ref[0:1, :] += s

    @pl.when(i >= lat_tiles)
    def _():
        kc_ref[...] = k
        s_ref[1:2, :] += s


def _hyena_filters(p, *, lat_seq, ctx_seq):
    tm = ctx_seq
    zz = jnp.asarray(np.concatenate([_filter_embedding(lat_seq), _filter_embedding(ctx_seq)], axis=0))
    rows = zz.shape[0]
    max_decay = math.log(HY_DECAY_TARGET) / HY_FAST_DECAY_PCT
    min_decay = math.log(HY_DECAY_TARGET) / HY_SLOW_DECAY_PCT
    deltas = jnp.asarray(np.abs(np.linspace(min_decay, max_decay, HY_WIDTH))[None, :].astype(np.float32))
    emb, hid = p['hy_fw1'].shape

    def pad2(a, r, c):
        return jnp.zeros((r, c), F32).at[:a.shape[0], :a.shape[1]].set(a)

    w1 = pad2(p['hy_fw1'], LANES, LANES)
    b1 = pad2(p['hy_fb1'][None, :], 1, LANES)
    f1 = pad2(p['hy_freq1'][None, :], 1, LANES)
    w2 = pad2(p['hy_fw2'], LANES, LANES)
    b2 = pad2(p['hy_fb2'][None, :], 1, LANES)
    f2 = pad2(p['hy_freq2'][None, :], 1, LANES)
    w3 = pad2(p['hy_fw3'], LANES, 2 * HY_WIDTH)
    lat_tiles = 2 * lat_seq // tm
    body = functools.partial(_filter_body, tm=tm, lat_tiles=lat_tiles, ctx_tiles=2 * ctx_seq // tm)
    full = lambda i: (0, 0)
    return pl.pallas_call(
        body,
        grid=(rows // tm,),
        in_specs=[pl.BlockSpec((tm, LANES), lambda i: (i, 0)),
                  pl.BlockSpec((1, HY_WIDTH), full),
                  pl.BlockSpec((LANES, LANES), full), pl.BlockSpec((1, LANES), full), pl.BlockSpec((1, LANES), full),
                  pl.BlockSpec((LANES, LANES), full), pl.BlockSpec((1, LANES), full), pl.BlockSpec((1, LANES), full),
                  pl.BlockSpec((LANES, 2 * HY_WIDTH), full)],
        out_specs=[pl.BlockSpec((tm, HY_WIDTH), lambda i: (jnp.minimum(i, lat_tiles - 1), 0)),
                   pl.BlockSpec((tm, HY_WIDTH), lambda i: (jnp.maximum(i - lat_tiles, 0), 0)),
                   pl.BlockSpec((SUBLANES, HY_WIDTH), full)],
        out_shape=[jax.ShapeDtypeStruct((2 * lat_seq, HY_WIDTH), F32),
                   jax.ShapeDtypeStruct((2 * ctx_seq, HY_WIDTH), F32),
                   jax.ShapeDtypeStruct((SUBLANES, HY_WIDTH), F32)],
        compiler_params=_params("arbitrary"),
        name="hy_filter",
    )(zz, deltas, w1, b1, f1, w2, b2, f2, w3)


def _stack_complex(z):
    return np.block([[z.real, -z.imag], [z.imag, z.real]])


def _dft_consts_two_level(seq, n1, n2):
    n = 2 * seq
    assert n1 * n2 == n
    a1 = np.arange(n1)
    f1_full = np.exp(-2j * np.pi * np.outer(a1, a1) / n1)
    f1_u = np.concatenate([f1_full.real, f1_full.imag], axis=0)[:, :n1 // 2]
    f1_k = np.concatenate([f1_full.real, f1_full.imag], axis=0)
    a2 = np.arange(n2)
    f = a1[:, None, None] + n1 * a2[None, :, None]
    z = np.exp(-2j * np.pi * (f * a2[None, None, :]) / n)
    mf = np.stack([_stack_complex(z[i]) for i in range(n1)])
    mi = np.stack([_stack_complex(np.conj(z[i]).T) for i in range(n1)])
    g = np.exp(2j * np.pi * np.outer(a1[:n1 // 2], a1) / n1) / n
    g1 = np.concatenate([_kron_rows(g.real), _kron_rows(-g.imag)], axis=1)
    as32 = lambda a: jnp.asarray(a.astype(np.float32))
    return as32(_kron_rows(f1_u)), as32(_kron_rows(f1_k)), as32(mf), as32(mi), as32(g1)


def _dft_consts_one_level(seq):
    n = 2 * seq
    a = np.arange(n)
    z = np.exp(-2j * np.pi * np.outer(a, a) / n)
    mf = np.concatenate([z.real, z.imag], axis=0)
    zi = np.exp(2j * np.pi * np.outer(a[:seq], a) / n) / n
    mi = np.concatenate([zi.real, -zi.imag], axis=1)
    as32 = lambda a: jnp.asarray(a.astype(np.float32))
    return as32(mf), as32(mi)


HY_ROWS = 16


def _kron_rows(m):
    return np.kron(m, np.eye(HY_ROWS))


def _dft1_body(m_ref, x_ref, sc_ref, or_ref, oi_ref):
    k, r, c = x_ref.shape
    x = (x_ref[...] * (1.0 / sc_ref[...])).reshape(k * r, c).astype(BF16)
    o = _dot(m_ref[...], x)
    h = o.shape[0] // 2
    or_ref[...] = o[:h].reshape(or_ref.shape).astype(BF16)
    oi_ref[...] = o[h:].reshape(oi_ref.shape).astype(BF16)


def _dft1(m, x, scale_row, *, n1, groups):
    _, k, n2, c = x.shape
    g = groups
    ospec = pl.BlockSpec((None, n1, HY_ROWS, c), lambda b, j: (b, 0, j, 0))
    return pl.pallas_call(
        _dft1_body,
        grid=(g, n2 // HY_ROWS),
        in_specs=[pl.BlockSpec(m.shape, lambda b, j: (0, 0)),
                  pl.BlockSpec((None, k, HY_ROWS, c), lambda b, j: (b, 0, j, 0)),
                  pl.BlockSpec((1, c), lambda b, j: (0, 0))],
        out_specs=[ospec, ospec],
        out_shape=[jax.ShapeDtypeStruct((g, n1, n2, c), BF16)] * 2,
        compiler_params=_params("parallel", "parallel"),
        name="hy_dft1",
    )(m, x, scale_row)


SPEC_GROUP = 16


def _spec_fwd_body(mf_ref, ar_ref, ai_ref, kr_ref, ki_ref):
    for g in range(SPEC_GROUP):
        a = jnp.concatenate([ar_ref[g], ai_ref[g]], axis=0)
        x = _dot(mf_ref[g], a)
        h = x.shape[0] // 2
        kr_ref[g] = x[:h]
        ki_ref[g] = x[h:]


def _spec_fwd(mf, ar, ai):
    n1, n2, c = ar.shape
    spec = pl.BlockSpec((SPEC_GROUP, n2, c), lambda i: (i, 0, 0))
    return pl.pallas_call(
        _spec_fwd_body,
        grid=(n1 // SPEC_GROUP,),
        in_specs=[pl.BlockSpec((SPEC_GROUP, 2 * n2, 2 * n2), lambda i: (i, 0, 0)), spec, spec],
        out_specs=[spec, spec],
        out_shape=[jax.ShapeDtypeStruct((n1, n2, c), F32)] * 2,
        compiler_params=_params("parallel"),
        name="hy_spec_filter",
    )(mf, ar, ai)


def _spec_mul_body(mf_ref, mi_ref, kr_ref, ki_ref, ar_ref, ai_ref, br_ref, bi_ref):
    for g in range(SPEC_GROUP):
        a = jnp.concatenate([ar_ref[g], ai_ref[g]], axis=0)
        x = _dot(mf_ref[g], a)
        h = x.shape[0] // 2
        xr, xi = x[:h], x[h:]
        kr, ki = kr_ref[g], ki_ref[g]
        y = jnp.concatenate([xr * kr - xi * ki, xr * ki + xi * kr], axis=0).astype(BF16)
        b = _dot(mi_ref[g], y)
        br_ref[g] = b[:h].astype(BF16)
        bi_ref[g] = b[h:].astype(BF16)


def _spec_mul(mf, mi, kr, ki, ar, ai):
    nb, n1, n2, c = ar.shape
    mspec = pl.BlockSpec((SPEC_GROUP, 2 * n2, 2 * n2), lambda i, b: (i, 0, 0))
    kspec = pl.BlockSpec((SPEC_GROUP, n2, c), lambda i, b: (i, 0, 0))
    aspec = pl.BlockSpec((None, SPEC_GROUP, n2, c), lambda i, b: (b, i, 0, 0))
    return pl.pallas_call(
        _spec_mul_body,
        grid=(n1 // SPEC_GROUP, nb),
        in_specs=[mspec, mspec, kspec, kspec, aspec, aspec],
        out_specs=[aspec, aspec],
        out_shape=[jax.ShapeDtypeStruct((nb, n1, n2, c), BF16)] * 2,
        compiler_params=_params("parallel", "arbitrary"),
        name="hy_spec_mul",
    )(mf, mi, kr, ki, ar, ai)


def _idft1_body(m_ref, br_ref, bi_ref, o_ref):
    n1, r, c = br_ref.shape
    b = jnp.concatenate([br_ref[...].reshape(n1 * r, c), bi_ref[...].reshape(n1 * r, c)], axis=0)
    o_ref[...] = _dot(m_ref[...], b).reshape(o_ref.shape)


def _idft1(m, br, bi, *, n_out):
    nb, n1, n2, c = br.shape
    bspec = pl.BlockSpec((None, n1, HY_ROWS, c), lambda b, j: (b, 0, j, 0))
    return pl.pallas_call(
        _idft1_body,
        grid=(nb, n2 // HY_ROWS),
        in_specs=[pl.BlockSpec(m.shape, lambda b, j: (0, 0)), bspec, bspec],
        out_specs=pl.BlockSpec((None, n_out, HY_ROWS, c), lambda b, j: (b, 0, j, 0)),
        out_shape=jax.ShapeDtypeStruct((nb, n_out, n2, c), F32),
        compiler_params=_params("parallel", "parallel"),
        name="hy_idft1",
    )(m, br, bi)


def _ctx_filter_body(mf_ref, k_ref, sc_ref, kf_ref):
    kf_ref[...] = _dot(mf_ref[...], (k_ref[...] * (1.0 / sc_ref[...])).astype(BF16))


def _ctx_conv_body(mf_ref, mi_ref, kf_ref, u_ref, o_ref):
    x = _dot(mf_ref[...], u_ref[...].astype(BF16))
    h = x.shape[0] // 2
    xr, xi = x[:h], x[h:]
    kr, ki = kf_ref[:h, :], kf_ref[h:, :]
    y = jnp.concatenate([xr * kr - xi * ki, xr * ki + xi * kr], axis=0).astype(BF16)
    o_ref[...] = _dot(mi_ref[...], y)


def _hyena_ctx(k_raw, k_norm1, u, *, seq, nseq, u_row0):
    mf, mi = _dft_consts_one_level(seq)
    n = 2 * seq
    c = u.shape[1]
    kf = pl.pallas_call(
        _ctx_filter_body,
        out_shape=jax.ShapeDtypeStruct((2 * n, c), F32),
        compiler_params=_params(),
        name="hy_ctx_filter",
    )(mf.astype(BF16), k_raw, k_norm1)
    full = lambda s: (0, 0)
    return pl.pallas_call(
        _ctx_conv_body,
        grid=(nseq,),
        in_specs=[pl.BlockSpec((2 * n, seq), full),
                  pl.BlockSpec((seq, 2 * n), full),
                  pl.BlockSpec((2 * n, c), full),
                  pl.BlockSpec((seq, c), lambda s: (u_row0 // seq + s, 0))],
        out_specs=pl.BlockSpec((seq, c), lambda s: (s, 0)),
        out_shape=jax.ShapeDtypeStruct((nseq * seq, c), F32),
        compiler_params=_params("parallel"),
        name="hy_ctx_conv",
    )(mf[:, :seq].astype(BF16), mi.astype(BF16), kf, u)


def _hyena_lat(k_raw, k_norm1, u, *, seq, nb):
    c = u.shape[1]
    n1, n2 = 64, 2 * seq // 64
    f1_u, f1_k, mf, mi, g1 = _dft_consts_two_level(seq, n1, n2)
    mf = mf.astype(BF16)
    mi = mi.astype(BF16)
    ones_row = jnp.ones((1, c), F32)
    akr, aki = _dft1(f1_k.astype(BF16), k_raw.reshape(-1, n1, n2, c), k_norm1, n1=n1, groups=1)
    kr, ki = _spec_fwd(mf, akr[0], aki[0])
    ar, ai = _dft1(f1_u.astype(BF16), u.reshape(-1, n1 // 2, n2, c), ones_row, n1=n1, groups=nb)
    br, bi = _spec_mul(mf, mi, kr, ki, ar, ai)
    y = _idft1(g1.astype(BF16), br, bi, n_out=n1 // 2)
    return y.reshape(nb * seq, c)


def _rope_tables(seq, rope_dims, lane_offsets, pad_rows):
    rows = seq // GRID_W
    rr, cc = np.meshgrid(np.arange(rows), np.arange(GRID_W), indexing='ij')
    pos = (rr.reshape(-1).astype(np.float64), cc.reshape(-1).astype(np.float64))
    half = rope_dims // 2
    q = half // 2
    inv_freq = ROPE_BASE ** (-np.arange(0, half, 2, dtype=np.float64) / half)
    cos_t = np.ones((seq + pad_rows, LANES), np.float64)
    sin_a = np.zeros((seq + pad_rows, LANES), np.float64)
    sin_b = np.zeros((seq + pad_rows, LANES), np.float64)
    for off in lane_offsets:
        for axis in range(2):
            ang = pos[axis][:, None] * inv_freq[None, :]
            base = off + axis * half
            cos_t[:seq, base:base + q] = np.cos(ang)
            cos_t[:seq, base + q:base + half] = np.cos(ang)
            sin_b[:seq, base:base + q] = -np.sin(ang)
            sin_a[:seq, base + q:base + half] = np.sin(ang)
    as32 = lambda a: jnp.asarray(a.astype(np.float32))
    return as32(cos_t), as32(sin_a), as32(sin_b)


def _rope(x, cos_t, sin_a, sin_b, shift):
    return x * cos_t + pltpu.roll(x, shift, 1) * sin_a + pltpu.roll(x, LANES - shift, 1) * sin_b


def _even_in_body(xa_ref, xb_ref, mod_ref, g_ref, w_ref, qa_ref, kva_ref, wuq_ref, qn_ref,
                  cos_ref, sa_ref, sb_ref, zhy_ref, kr_ref, q_ref, ckvn_ref, zq_ref, *, na, scale):
    i = pl.program_id(0)
    n = pl.num_programs(0) - 1

    @pl.when(i == 0)
    def _():
        zq_ref[1] = jnp.zeros(zq_ref.shape[1:], F32)

    c_q = MLA_Q_RANK
    zp = zq_ref[(i + 1) % 2]
    ckvn_ref[...] = _rms(zp[:, c_q:], kva_ref[...])
    cqn = _rms(zp[:, :c_q], qa_ref[...])
    q = _dot(cqn.astype(BF16), wuq_ref[...])
    cos_t, sin_a, sin_b = cos_ref[...], sa_ref[...], sb_ref[...]
    g = qn_ref[...]
    for hh in range(MLA_HEADS):
        sl = slice(hh * LANES, (hh + 1) * LANES)
        qh = _rope(_rms(q[:, sl], g, MLA_QK), cos_t, sin_a, sin_b, MLA_ROPE // 4)
        q_ref[:, sl] = (qh * scale).astype(BF16)

    x = jnp.where(jnp.minimum(i, n - 1) < na, xa_ref[...], xb_ref[...])
    h = _norm_mod(x, g_ref[...], mod_ref[0:1, :], mod_ref[1:2, :])
    z = _dot(h.astype(BF16), w_ref[...])
    c_hy = 3 * HY_WIDTH
    c_kv = c_hy + MLA_Q_RANK + MLA_KV_RANK
    zhy_ref[...] = z[:, :c_hy]
    kr_ref[...] = z[:, c_kv:]
    zq_ref[i % 2] = z[:, c_hy:c_kv]


def _pad_heads(w, heads, width):
    k = w.shape[0]
    w3 = w.reshape(k, heads, width)
    return jnp.zeros((k, heads, LANES), w.dtype).at[:, :, :width].set(w3).reshape(k, heads * LANES)


def _even_in(xa, xb, mod, p, tables, *, tm, seg_len, t_lat, lat_seq):
    d = xa.shape[1]
    t = xa.shape[0] + xb.shape[0]
    n_in = p['w_in'].shape[1]
    c_kv = 3 * HY_WIDTH + MLA_Q_RANK + MLA_KV_RANK
    assert c_kv % LANES == 0 and n_in - c_kv == MLA_ROPE
    n_pad = c_kv + LANES
    w_in = jnp.zeros((d, n_pad), BF16).at[:, :n_in].set(p['w_in'].astype(BF16))
    wuq = _pad_heads(p['w_uq'], MLA_HEADS, MLA_QK).astype(BF16)
    cos_t, sin_a, sin_b = tables
    qn = jnp.zeros((1, LANES), F32).at[0, :MLA_QK].set(p['q_norm'])
    pos_blocks = lat_seq // tm
    lat_tiles = t_lat // tm
    n = t // tm
    na = xa.shape[0] // tm
    cur = lambda i: jnp.minimum(i, n - 1)
    prev = lambda i: jnp.maximum(i - 1, 0)

    def tmap(i):
        j = prev(i)
        return (jnp.where(j < lat_tiles, j % pos_blocks, pos_blocks), 0)

    tspec = pl.BlockSpec((tm, LANES), tmap)
    full = lambda i: (0, 0)
    row_cur = lambda cols: pl.BlockSpec((tm, cols), lambda i: (cur(i), 0))
    row_prev = lambda cols: pl.BlockSpec((tm, cols), lambda i: (prev(i), 0))
    return pl.pallas_call(
        functools.partial(_even_in_body, na=na, scale=MLA_QK ** -0.5 * LOG2_E),
        grid=(n + 1,),
        in_specs=[pl.BlockSpec((tm, d), lambda i: (jnp.minimum(cur(i), na - 1), 0)),
                  pl.BlockSpec((tm, d), lambda i: (jnp.maximum(cur(i) - na, 0), 0)),
                  pl.BlockSpec((None, 6, d), lambda i: (cur(i) * tm // seg_len, 0, 0)),
                  pl.BlockSpec((1, d), full),
                  pl.BlockSpec((d, n_pad), full),
                  pl.BlockSpec((1, MLA_Q_RANK), full),
                  pl.BlockSpec((1, MLA_KV_RANK), full),
                  pl.BlockSpec((MLA_Q_RANK, MLA_HEADS * LANES), full),
                  pl.BlockSpec((1, LANES), full),
                  tspec, tspec, tspec],
        out_specs=[row_cur(3 * HY_WIDTH), row_cur(LANES), row_prev(MLA_HEADS * LANES), row_prev(MLA_KV_RANK)],
        out_shape=[jax.ShapeDtypeStruct((t, 3 * HY_WIDTH), F32),
                   jax.ShapeDtypeStruct((t, LANES), F32),
                   jax.ShapeDtypeStruct((t, MLA_HEADS * LANES), BF16),
                   jax.ShapeDtypeStruct((t, MLA_KV_RANK), F32)],
        scratch_shapes=[pltpu.VMEM((2, tm, MLA_Q_RANK + MLA_KV_RANK), F32)],
        compiler_params=_params("arbitrary"),
        name="even_in",
    )(xa, xb, mod, p['norm1'].reshape(1, d), w_in, p['qa_norm'].reshape(1, -1), p['kva_norm'].reshape(1, -1),
      wuq, qn, cos_t, sin_a, sin_b)


def _mla_kv_body(ckvn_ref, kr_ref, wk_ref, wv_ref, kn_ref, cos_ref, sa_ref, sb_ref, k_ref, v_ref):
    c = ckvn_ref[...].astype(BF16)
    k = _dot(c, wk_ref[...])
    v_ref[...] = _dot(c, wv_ref[...]).astype(BF16)
    kr = pltpu.roll(kr_ref[...], MLA_NOPE, 1)
    cos_t, sin_a, sin_b = cos_ref[...], sa_ref[...], sb_ref[...]
    g = kn_ref[...]
    for h in range(MLA_HEADS):
        sl = slice(h * LANES, (h + 1) * LANES)
        kh = _rope(_rms(k[:, sl] + kr, g, MLA_QK), cos_t, sin_a, sin_b, MLA_ROPE // 4)
        k_ref[:, sl] = kh.astype(BF16)


def _mla_kv(ckvn_rows, kr_rows, p, tables, *, tm, nb, past, lat_seq):
    r = ckvn_rows.shape[0]
    w = p['w_ukv'].reshape(MLA_KV_RANK, MLA_HEADS, MLA_NOPE + MLA_V)
    wk = _pad_heads(w[:, :, :MLA_NOPE].reshape(MLA_KV_RANK, -1), MLA_HEADS, MLA_NOPE).astype(BF16)
    wv = w[:, :, MLA_NOPE:].reshape(MLA_KV_RANK, MLA_HEADS * MLA_V).astype(BF16)
    cos_t, sin_a, sin_b = tables
    kn = jnp.zeros((1, LANES), F32).at[0, :MLA_QK].set(p['k_norm'])
    per_b = (past + lat_seq) // tm
    past_tiles = past // tm
    pos_blocks = lat_seq // tm
    lat_tiles = nb * per_b

    def tmap(i):
        j = i % per_b
        is_pos = jnp.logical_and(i < lat_tiles, j >= past_tiles)
        return (jnp.where(is_pos, j - past_tiles, pos_blocks), 0)

    tspec = pl.BlockSpec((tm, LANES), tmap)
    full = lambda i: (0, 0)
    return pl.pallas_call(
        _mla_kv_body,
        grid=(r // tm,),
        in_specs=[pl.BlockSpec((tm, MLA_KV_RANK), lambda i: (i, 0)),
                  pl.BlockSpec((tm, LANES), lambda i: (i, 0)),
                  pl.BlockSpec((MLA_KV_RANK, MLA_HEADS * LANES), full),
                  pl.BlockSpec((MLA_KV_RANK, MLA_HEADS * MLA_V), full),
                  pl.BlockSpec((1, LANES), full),
                  tspec, tspec, tspec],
        out_specs=[pl.BlockSpec((tm, MLA_HEADS * LANES), lambda i: (i, 0)),
                   pl.BlockSpec((tm, MLA_HEADS * MLA_V), lambda i: (i, 0))],
        out_shape=[jax.ShapeDtypeStruct((r, MLA_HEADS * LANES), BF16),
                   jax.ShapeDtypeStruct((r, MLA_HEADS * MLA_V), BF16)],
        compiler_params=_params("parallel"),
        name="mla_kv",
    )(ckvn_rows, kr_rows, wk, wv, kn, cos_t, sin_a, sin_b)


ATT_CHUNK = 512
ATT_UNIT_ROWS = 256
ATT_TQ = 1024


def _fill_vaug(vaug_ref, g, v_blocks):
    off = 0
    for v in v_blocks:
        n = v.shape[0]
        vaug_ref[g, off:off + n, :LANES] = v
        off += n
    vaug_ref[g, :, LANES:] = jnp.ones((vaug_ref.shape[1], LANES), BF16)


def _softmax_pv(units, s_ref, n_keys):
    chunk = min(ATT_CHUNK, n_keys)
    chunks = [slice(c * chunk, (c + 1) * chunk) for c in range(n_keys // chunk)]

    def scores(u, rows, m_lane):
        s = _dot_nt(units[u][0], units[u][1](rows))
        s_ref[u % 2, :, rows] = s
        for j in range(chunk // LANES):
            blk = s[:, j * LANES:(j + 1) * LANES]
            m_lane = blk if m_lane is None else jnp.maximum(m_lane, blk)
        return m_lane

    def values(u, rows, m, acc):
        p = jnp.exp2(s_ref[u % 2, :, rows] - m).astype(BF16)
        d = _dot(p, units[u][2](rows))
        return d if acc is None else acc + d

    outs = []
    m_lane = None
    for rows in chunks:
        m_lane = scores(0, rows, m_lane)
    for u in range(len(units)):
        m = jnp.max(m_lane, axis=-1, keepdims=True)
        acc, m_lane = None, None
        for rows in chunks:
            acc = values(u, rows, m, acc)
            if u + 1 < len(units):
                m_lane = scores(u + 1, rows, m_lane)
        outs.append(acc)
    return outs


def _mla_attn_body(prev_ref, q_ref, k_ref, v_ref, o_ref, vaug_ref, s_ref):
    pairs = vaug_ref.shape[0]

    @pl.when(pl.program_id(2) == 0)
    def _():
        for g in range(pairs):
            _fill_vaug(vaug_ref, g, [v_ref[:, g * LANES:(g + 1) * LANES]])

    n_keys = k_ref.shape[0]
    tq = q_ref.shape[0]
    ur = s_ref.shape[1]
    units, slots = [], []
    for r0 in range(0, tq, ur):
        for g in range(pairs):
            slots.append((r0, g))
            for hh in range(2):
                sl = slice((2 * g + hh) * LANES, (2 * g + hh + 1) * LANES)
                units.append((q_ref[r0:r0 + ur, sl], lambda rows, sl=sl: k_ref[rows, sl],
                              lambda rows, g=g: vaug_ref[g, rows, :]))
    res = [r[:, :LANES] / r[:, LANES:] for r in _softmax_pv(units, s_ref, n_keys)]
    lane = lax.broadcasted_iota(jnp.int32, res[0].shape, 1)
    for i, (r0, g) in enumerate(slots):
        o_ref[r0:r0 + ur, g * LANES:(g + 1) * LANES] = jnp.where(
            lane < MLA_V, res[2 * i], res[2 * i + 1]).astype(BF16)


def _mla_attn(prev, q, k, v, *, tq, pairs, n_seq, seq_q, seq_k, q_row0, k_row0):
    hp = MLA_HEADS // 2 // pairs
    nq = seq_q // tq
    qb0, kb0 = q_row0 // tq, k_row0 // seq_k
    return pl.pallas_call(
        _mla_attn_body,
        grid=(n_seq, hp, nq),
        in_specs=[pl.BlockSpec(memory_space=pl.ANY),
                  pl.BlockSpec((tq, pairs * 2 * LANES), lambda s, h, i: (qb0 + s * nq + i, h)),
                  pl.BlockSpec((seq_k, pairs * 2 * LANES), lambda s, h, i: (kb0 + s, h)),
                  pl.BlockSpec((seq_k, pairs * 2 * MLA_V), lambda s, h, i: (kb0 + s, h))],
        out_specs=pl.BlockSpec((tq, pairs * 2 * MLA_V), lambda s, h, i: (qb0 + s * nq + i, h)),
        out_shape=jax.ShapeDtypeStruct(prev.shape, prev.dtype),
        input_output_aliases={0: 0},
        scratch_shapes=[pltpu.VMEM((pairs, seq_k, 2 * LANES), BF16),
                        pltpu.VMEM((2, min(tq, ATT_UNIT_ROWS), seq_k), F32)],
        compiler_params=_params("arbitrary", "arbitrary", "arbitrary"),
        name="mla_attn",
    )(prev, q, k, v)


def _diff_attn_body(prev_ref, *refs, n_seg, lambda_init):
    q_ref, lam_ref, sub_ref = refs[0], refs[1], refs[2]
    k_refs = refs[3:3 + n_seg]
    v_refs = refs[3 + n_seg:3 + 2 * n_seg]
    o_ref, kcat_ref, vaug_ref, s_ref = refs[3 + 2 * n_seg:]
    heads = vaug_ref.shape[0]

    def head_block(ref, g):
        return ref[g] if len(ref.shape) == 3 else ref[:, g * LANES:(g + 1) * LANES]

    @pl.when(pl.program_id(2) == 0)
    def _():
        for g in range(heads):
            _fill_vaug(vaug_ref, g, [head_block(v, g) for v in v_refs])
            off = 0
            for k in k_refs:
                kcat_ref[g, off:off + k.shape[-2], :] = head_block(k, g)
                off += k.shape[-2]

    lp = lam_ref[...]
    lam = (jnp.exp(jnp.sum(lp[0:1] * lp[1:2], axis=-1, keepdims=True))
           - jnp.exp(jnp.sum(lp[2:3] * lp[3:4], axis=-1, keepdims=True)) + lambda_init)
    n_keys = kcat_ref.shape[1]
    tq = q_ref.shape[0]
    ur = s_ref.shape[1]
    units, slots = [], []
    for r0 in range(0, tq, ur):
        for g in range(heads):
            slots.append((r0, g))
            q = q_ref[r0:r0 + ur, g * LANES:(g + 1) * LANES].astype(F32)
            lane = lax.broadcasted_iota(jnp.int32, q.shape, 1)
            k_of = lambda rows, g=g: kcat_ref[g, rows, :]
            v_of = lambda rows, g=g: vaug_ref[g, rows, :]
            units.append((jnp.where(lane < DIFF_DH, q, 0.0).astype(BF16), k_of, v_of))
            units.append((jnp.where(lane < DIFF_DH, 0.0, q).astype(BF16), k_of, v_of))
    res = _softmax_pv(units, s_ref, n_keys)
    for i, (r0, g) in enumerate(slots):
        r1, r2 = res[2 * i], res[2 * i + 1]
        o = r1[:, :LANES] / r1[:, LANES:] - (lam / r2[:, LANES:]) * r2[:, :LANES]
        o_ref[r0:r0 + ur, g * LANES:(g + 1) * LANES] = (
            _rms(o, sub_ref[...]) * (1.0 - lambda_init)).astype(BF16)


def _diff_attn(prev, q, k_new, v_new, k_cache, v_cache, lam_p, subln, *, tq, heads, n_seq, seq_q, q_row0,
               lambda_init):
    nq = seq_q // tq
    qb0 = q_row0 // tq
    sb0 = q_row0 // seq_q
    d = 2 * DIFF_DH
    new_spec = pl.BlockSpec((seq_q, heads * d), lambda s, h, i: (sb0 + s, h))
    if k_cache is None:
        n_seg, k_args, v_args, k_specs, v_specs = 1, [k_new], [v_new], [new_spec], [new_spec]
        n_keys = seq_q
    else:
        past = k_cache.shape[2]
        c_spec = pl.BlockSpec((None, heads, past, d), lambda s, h, i: (s, h, 0, 0))
        n_seg, k_args, v_args = 2, [k_cache, k_new], [v_cache, v_new]
        k_specs, v_specs = [c_spec, new_spec], [c_spec, new_spec]
        n_keys = past + seq_q
    return pl.pallas_call(
        functools.partial(_diff_attn_body, n_seg=n_seg, lambda_init=lambda_init),
        grid=(n_seq, DIFF_HEADS // heads, nq),
        in_specs=[pl.BlockSpec(memory_space=pl.ANY),
                  pl.BlockSpec((tq, heads * d), lambda s, h, i: (qb0 + s * nq + i, h)),
                  pl.BlockSpec((4, DIFF_DH), lambda s, h, i: (0, 0)),
                  pl.BlockSpec((1, d), lambda s, h, i: (0, 0))] + k_specs + v_specs,
        out_specs=pl.BlockSpec((tq, heads * d), lambda s, h, i: (qb0 + s * nq + i, h)),
        out_shape=jax.ShapeDtypeStruct(prev.shape, prev.dtype),
        input_output_aliases={0: 0},
        scratch_shapes=[pltpu.VMEM((heads, n_keys, d), BF16), pltpu.VMEM((heads, n_keys, 2 * LANES), BF16),
                        pltpu.VMEM((2, min(tq, ATT_UNIT_ROWS), n_keys), F32)],
        compiler_params=_params("arbitrary", "arbitrary", "arbitrary"),
        name="diff_attn",
    )(prev, q, lam_p, subln.reshape(1, d), *k_args, *v_args)


def _even_out_body(xa_ref, xb_ref, mod_ref, ca_ref, cb_ref, u_ref, x0_ref, db_ref, o_ref, wa_ref, wb_ref,
                   out_ref, *, na):
    first = pl.program_id(0) < na
    x = jnp.where(first, xa_ref[...], xb_ref[...])
    conv = jnp.where(first, ca_ref[...], cb_ref[...])
    y_hy = (conv + u_ref[...] * db_ref[...]) * x0_ref[...]
    acc = _dot(y_hy.astype(BF16), wa_ref[...]) + _dot(o_ref[...], wb_ref[...])
    out_ref[...] = x + mod_ref[2:3, :] * acc


def _even_out(xa, xb, mod, conv_a, conv_b, u, x0, dbias, o, w_hy, w_att, *, tm, seg_len):
    d = xa.shape[1]
    t = xa.shape[0] + xb.shape[0]
    c = u.shape[1]
    xa_spec, xb_spec, na = _two_part_specs(xa, xb, tm)
    ca_spec, cb_spec, na_c = _two_part_specs(conv_a, conv_b, tm)
    assert na == na_c
    row = lambda cols: pl.BlockSpec((tm, cols), lambda i: (i, 0))
    full = lambda a: pl.BlockSpec(a.shape, lambda i: (0, 0))
    return pl.pallas_call(
        functools.partial(_even_out_body, na=na),
        grid=(t // tm,),
        in_specs=[xa_spec, xb_spec,
                  pl.BlockSpec((None, 6, d), lambda i: (i * tm // seg_len, 0, 0)),
                  ca_spec, cb_spec, row(c), row(c), full(dbias), row(o.shape[1]), full(w_hy), full(w_att)],
        out_specs=row(d),
        out_shape=jax.ShapeDtypeStruct((t, d), F32),
        compiler_params=_params("arbitrary"),
        name="even_out",
    )(xa, xb, mod, conv_a, conv_b, u, x0, dbias, o, w_hy, w_att)


def _ffn_body(x_ref, mod_ref, g_ref, wg_ref, wu_ref, wd_ref, o_ref, h_ref, acc_ref):
    f = pl.program_id(1)

    @pl.when(f == 0)
    def _():
        h_ref[...] = _norm_mod(x_ref[...], g_ref[...], mod_ref[3:4, :], mod_ref[4:5, :]).astype(BF16)
        acc_ref[...] = jnp.zeros_like(acc_ref)

    h = h_ref[...]
    a = _silu(_dot(h, wg_ref[...])) * _dot(h, wu_ref[...])
    acc_ref[...] += _dot(a.astype(BF16), wd_ref[...])

    @pl.when(f == pl.num_programs(1) - 1)
    def _():
        o_ref[...] = x_ref[...] + mod_ref[5:6, :] * acc_ref[...]


def _ffn(x, mod, g, wg, wu, wd, *, tm, tf, seg_len):
    t, d = x.shape
    ff = wg.shape[1]
    return pl.pallas_call(
        _ffn_body,
        grid=(t // tm, ff // tf),
        in_specs=[pl.BlockSpec((tm, d), lambda i, f: (i, 0)),
                  pl.BlockSpec((None, 6, d), lambda i, f: (i * tm // seg_len, 0, 0)),
                  pl.BlockSpec((1, d), lambda i, f: (0, 0)),
                  pl.BlockSpec((d, tf), lambda i, f: (0, f)),
                  pl.BlockSpec((d, tf), lambda i, f: (0, f)),
                  pl.BlockSpec((tf, d), lambda i, f: (f, 0))],
        out_specs=pl.BlockSpec((tm, d), lambda i, f: (i, 0)),
        out_shape=jax.ShapeDtypeStruct((t, d), F32),
        scratch_shapes=[pltpu.VMEM((tm, d), BF16), pltpu.VMEM((tm, d), F32)],
        compiler_params=_params("parallel", "arbitrary"),
        name="ffn",
    )(x, mod, g.reshape(1, d), wg, wu, wd)


def _group_ms(x, gmat):
    hi, lo = _split_bf16(x * x)
    return (_dot(hi, gmat) + _dot(lo, gmat)) * (1.0 / DIFF_DH)


def _qkv_body(x_ref, mod_ref, g_ref, w_ref, gm_ref, qn_ref, kn_ref, cos_ref, sa_ref, sb_ref,
              q_ref, k_ref, v_ref, kf_ref, vf_ref, *, scale, seq):
    h = _norm_mod(x_ref[...], g_ref[...], mod_ref[0:1, :], mod_ref[1:2, :]).astype(BF16)
    z = _dot(h, w_ref[...])
    hd = DIFF_HEADS * 2 * DIFF_DH
    tm = z.shape[0]
    cos_t, sin_a, sin_b = cos_ref[...], sa_ref[...], sb_ref[...]
    gm = gm_ref[...]
    shift = DIFF_DH // 4
    for hh in range(DIFF_HEADS):
        sl = slice(hh * LANES, (hh + 1) * LANES)
        qh = z[:, hh * LANES:(hh + 1) * LANES]
        qh = qh * lax.rsqrt(_group_ms(qh, gm) + NORM_EPS) * qn_ref[...]
        q_ref[:, sl] = (_rope(qh, cos_t, sin_a, sin_b, shift) * scale).astype(BF16)
        kh = z[:, hd + hh * LANES:hd + (hh + 1) * LANES]
        kh = kh * lax.rsqrt(_group_ms(kh, gm) + NORM_EPS) * kn_ref[...]
        k_ref[:, sl] = _rope(kh, cos_t, sin_a, sin_b, shift).astype(BF16)
        vh = z[:, 2 * hd + hh * LANES:2 * hd + (hh + 1) * LANES]
        for s in range(tm // seq):
            kf_ref[s, hh] = kh[s * seq:(s + 1) * seq, :]
            vf_ref[s, hh] = vh[s * seq:(s + 1) * seq, :]
    v_ref[...] = z[:, 2 * hd:].astype(BF16)


def _qkv(x, mod, p, tables, *, tm, seg_len, t_lat, lat_seq, n_ctx, ctx_seq):
    t, d = x.shape
    hd = DIFF_HEADS * 2 * DIFF_DH
    wqk = p['w_qkv'][:, :2 * hd].reshape(d, 2, 2, DIFF_HEADS, DIFF_DH)
    wqk = wqk.transpose(0, 1, 3, 2, 4).reshape(d, 2 * hd)
    w = jnp.concatenate([wqk, p['w_qkv'][:, 2 * hd:]], axis=1).astype(BF16)
    gi = np.arange(LANES) // DIFF_DH
    gmat = jnp.asarray((gi[:, None] == gi[None, :]).astype(np.float32)).astype(BF16)
    cos_t, sin_a, sin_b = tables
    qn = jnp.tile(p['q_norm'], 2).reshape(1, LANES)
    kn = jnp.tile(p['k_norm'], 2).reshape(1, LANES)
    pos_blocks = lat_seq // tm
    lat_tiles = t_lat // tm
    seq_per_tile = tm // ctx_seq
    tspec = pl.BlockSpec((tm, LANES), lambda i: (jnp.where(i < lat_tiles, i % pos_blocks, pos_blocks), 0))
    full = lambda i: (0, 0)
    row = pl.BlockSpec((tm, hd), lambda i: (i, 0))
    fspec = pl.BlockSpec((seq_per_tile, DIFF_HEADS, ctx_seq, LANES),
                         lambda i: (jnp.maximum(i - lat_tiles, 0), 0, 0, 0))
    fshape = jax.ShapeDtypeStruct((n_ctx, DIFF_HEADS, ctx_seq, LANES), F32)
    return pl.pallas_call(
        functools.partial(_qkv_body, scale=DIFF_DH ** -0.5 * LOG2_E, seq=ctx_seq),
        grid=(t // tm,),
        in_specs=[pl.BlockSpec((tm, d), lambda i: (i, 0)),
                  pl.BlockSpec((None, 6, d), lambda i: (i * tm // seg_len, 0, 0)),
                  pl.BlockSpec((1, d), full),
                  pl.BlockSpec((d, 3 * hd), full),
                  pl.BlockSpec((LANES, LANES), full),
                  pl.BlockSpec((1, LANES), full), pl.BlockSpec((1, LANES), full),
                  tspec, tspec, tspec],
        out_specs=[row, row, row, fspec, fspec],
        out_shape=[jax.ShapeDtypeStruct((t, hd), BF16)] * 3 + [fshape, fshape],
        compiler_params=_params("arbitrary"),
        name="qkv",
    )(x, mod, p['norm1'].reshape(1, d), w, gmat, qn, kn, cos_t, sin_a, sin_b)


def _route(logits):
    lane = lax.broadcasted_iota(jnp.int32, logits.shape, 1)
    neg = jnp.float32(-jnp.inf)
    lg = jnp.where(lane < N_EXPERTS, logits, neg)
    m1 = jnp.max(lg, axis=-1, keepdims=True)
    i1 = jnp.min(jnp.where(lg == m1, lane, LANES), axis=-1, keepdims=True)
    lg2 = jnp.where(lane == i1, neg, lg)
    m2 = jnp.max(lg2, axis=-1, keepdims=True)
    i2 = jnp.min(jnp.where(lg2 == m2, lane, LANES), axis=-1, keepdims=True)
    e = jnp.exp(m2 - m1)
    w1 = 1.0 / (1.0 + e)
    w2 = e / (1.0 + e)
    return jnp.where(lane == i1, w1, 0.0) + jnp.where(lane == i2, w2, 0.0)


MOE_BLOCK = 1024
MOE_SUB = 256
MOE_ROUTE_TM = 512
MOE_WINDOW = 6
MOE_TF = 512
MOE_CMB = 256
MOE_CMB_BLOCKS = MOE_SUB // MOE_CMB + 1


def _moe_route_body(x_ref, o_ref, wo_ref, mod_ref, g_ref, wr_ref, xo_ref, h_ref, gates_ref, rank_ref, rank_t_ref,
                    carry_row, carry_col, *, tm):
    i = pl.program_id(0)

    @pl.when(i == 0)
    def _():
        carry_row[...] = jnp.zeros_like(carry_row)
        carry_col[...] = jnp.zeros_like(carry_col)

    x = x_ref[...] + mod_ref[2:3, :] * _dot(o_ref[...], wo_ref[...])
    xo_ref[...] = x
    h = _norm_mod(x, g_ref[...], mod_ref[3:4, :], mod_ref[4:5, :])
    h_ref[...] = h.astype(BF16)
    gates = _route(_dot_f32(h, wr_ref[...]))
    gates_ref[...] = gates
    sel = jnp.where(gates != 0.0, 1.0, 0.0)
    sel_t = sel.T
    r = lax.broadcasted_iota(jnp.int32, (tm, tm), 0)
    c = lax.broadcasted_iota(jnp.int32, (tm, tm), 1)
    lower = jnp.where(c < r, 1.0, 0.0).astype(BF16)
    upper = jnp.where(r < c, 1.0, 0.0).astype(BF16)
    before = _dot(lower, sel.astype(BF16)) + carry_row[...]
    before_t = _dot(sel_t.astype(BF16), upper) + carry_col[...]
    rank_ref[...] = jnp.where(sel > 0.0, before, -1.0)
    rank_t = jnp.where(sel_t > 0.0, before_t, -1.0)
    for s in range(tm // MOE_SUB):
        rank_t_ref[s] = rank_t[:SUBLANES, s * MOE_SUB:(s + 1) * MOE_SUB]
    carry_row[...] += jnp.sum(sel, axis=0, keepdims=True)
    carry_col[...] += jnp.sum(sel_t, axis=1, keepdims=True)


def _moe_route(x, o, w_out, mod, g, w_router, *, seg_len):
    t, d = x.shape
    tm = MOE_ROUTE_TM
    ne = w_router.shape[1]
    assert ne <= SUBLANES
    wr = jnp.zeros((d, LANES), F32).at[:, :ne].set(w_router)
    sub = tm // MOE_SUB
    row = lambda cols: pl.BlockSpec((tm, cols), lambda i: (i, 0))
    full = lambda a: pl.BlockSpec(a.shape, lambda i: (0, 0))
    return pl.pallas_call(
        functools.partial(_moe_route_body, tm=tm),
        grid=(t // tm,),
        in_specs=[row(d), row(o.shape[1]), full(w_out),
                  pl.BlockSpec((None, 6, d), lambda i: (i * tm // seg_len, 0, 0)),
                  pl.BlockSpec((1, d), lambda i: (0, 0)),
                  full(wr)],
        out_specs=[row(d), row(d), row(LANES), row(LANES),
                   pl.BlockSpec((sub, SUBLANES, MOE_SUB), lambda i: (i, 0, 0))],
        out_shape=[jax.ShapeDtypeStruct((t, d), F32),
                   jax.ShapeDtypeStruct((t, d), BF16),
                   jax.ShapeDtypeStruct((t, LANES), F32),
                   jax.ShapeDtypeStruct((t, LANES), F32),
                   jax.ShapeDtypeStruct((t // MOE_SUB, SUBLANES, MOE_SUB), F32)],
        scratch_shapes=[pltpu.VMEM((1, LANES), F32), pltpu.VMEM((LANES, 1), F32)],
        compiler_params=_params("arbitrary"),
        name="moe_route",
    )(x, o, w_out, mod, g.reshape(1, d), wr)


def _moe_plan(rank, ne, *, n_blocks):
    t = rank.shape[0]
    n_tiles = t // MOE_SUB
    per_blk = MOE_BLOCK // MOE_SUB
    n_sub = n_blocks * per_blk
    sel = (rank[:, :ne] >= 0.0).astype(jnp.int32)
    tile_cnt = sel.reshape(n_tiles, MOE_SUB, ne).sum(axis=1)
    tile_end = jnp.cumsum(tile_cnt, axis=0)
    tile_start = tile_end - tile_cnt
    cnt = tile_end[-1]
    nblk = (cnt + MOE_BLOCK - 1) // MOE_BLOCK
    bend = jnp.cumsum(nblk)
    bstart = bend - nblk
    e_last = jnp.max(jnp.where(cnt > 0, jnp.arange(ne), 0))
    b = jnp.arange(n_blocks)
    blk_valid = b < bend[-1]
    blk_e = jnp.minimum(jnp.sum(bend[None, :] <= b[:, None], axis=1), e_last).astype(jnp.int32)
    blk_r0 = (b - bstart[blk_e]) * MOE_BLOCK
    blk_rows = jnp.where(blk_valid, jnp.clip(cnt[blk_e] - blk_r0, 0, MOE_BLOCK), 0).astype(jnp.int32)
    j = jnp.arange(n_sub)
    sub_e = blk_e[j // per_blk]
    sub_r0 = blk_r0[j // per_blk] + (j % per_blk) * MOE_SUB
    sub_valid = jnp.logical_and(blk_valid[j // per_blk], sub_r0 < cnt[sub_e])
    ends = tile_end[:, sub_e]
    r1 = jnp.minimum(sub_r0 + MOE_SUB, cnt[sub_e])
    c_lo = jnp.sum(ends <= sub_r0[None, :], axis=0)
    c_hi = jnp.sum(ends < r1[None, :], axis=0)
    c_lo = jnp.where(sub_valid, c_lo, 1).astype(jnp.int32)
    c_hi = jnp.where(sub_valid, jnp.minimum(c_hi, n_tiles - 1), 0).astype(jnp.int32)
    base = (bstart * MOE_BLOCK).astype(jnp.int32)
    n_cmb = n_sub * (MOE_SUB // MOE_CMB)
    j0 = jnp.minimum((base[None, :] + tile_start) // MOE_CMB, n_cmb - MOE_CMB_BLOCKS).astype(jnp.int32)
    return dict(blk_e=blk_e, blk_valid=blk_valid.astype(jnp.int32), blk_rows=blk_rows,
                sub_e=sub_e.astype(jnp.int32), sub_r0=sub_r0.astype(jnp.int32), c_lo=c_lo, c_hi=c_hi,
                base=base, j0=j0.reshape(-1))


def _moe_dispatch_body(e_ref, r0_ref, lo_ref, hi_ref, h_ref, rank_t_ref, xs_ref, acc_ref):
    j = pl.program_id(0)
    e = e_ref[j]
    rows = (r0_ref[j] + lax.broadcasted_iota(jnp.int32, (MOE_SUB, 1), 0)).astype(F32)
    sub = lax.broadcasted_iota(jnp.int32, (SUBLANES, MOE_SUB), 0)
    n_tiles = rank_t_ref.shape[0]
    lo, hi = lo_ref[j], hi_ref[j]
    acc_ref[...] = jnp.zeros_like(acc_ref)

    def step(w, carry):
        first = lo + w * MOE_WINDOW
        c0 = jnp.minimum(first, n_tiles - MOE_WINDOW)
        pieces = []
        for i in range(MOE_WINDOW):
            c = c0 + i
            rk = jnp.sum(jnp.where(sub == e, rank_t_ref[c], 0.0), axis=0, keepdims=True)
            rk = jnp.where(c >= first, rk, -1.0)
            pieces.append(jnp.where(rk == rows, 1.0, 0.0).astype(BF16))
        onehot = jnp.concatenate(pieces, axis=1)
        off = pl.multiple_of(c0 * MOE_SUB, MOE_SUB)
        acc_ref[...] += _dot(onehot, h_ref[pl.ds(off, MOE_WINDOW * MOE_SUB), :])
        return carry

    lax.fori_loop(0, (hi - lo + MOE_WINDOW) // MOE_WINDOW, step, 0)
    xs_ref[...] = acc_ref[...].astype(BF16)


def _moe_dispatch(h, rank_t, plan, *, n_sub):
    t, d = h.shape
    grid_spec = pltpu.PrefetchScalarGridSpec(
        num_scalar_prefetch=4,
        grid=(n_sub,),
        in_specs=[pl.BlockSpec((t, d), lambda j, *_: (0, 0), pipeline_mode=pl.Buffered(1)),
                  pl.BlockSpec(rank_t.shape, lambda j, *_: (0, 0, 0), pipeline_mode=pl.Buffered(1))],
        out_specs=pl.BlockSpec((MOE_SUB, d), lambda j, *_: (j, 0)),
        scratch_shapes=[pltpu.VMEM((MOE_SUB, d), F32)],
    )
    return pl.pallas_call(
        _moe_dispatch_body,
        grid_spec=grid_spec,
        out_shape=jax.ShapeDtypeStruct((n_sub * MOE_SUB, d), BF16),
        compiler_params=_params("arbitrary"),
        name="moe_dispatch",
    )(plan['sub_e'], plan['sub_r0'], plan['c_lo'], plan['c_hi'], h, rank_t)


def _moe_ffn_body(e_ref, valid_ref, rows_ref, xs_ref, wg_ref, wu_ref, wd_ref, y_ref, acc_ref):
    b = pl.program_id(0)
    f = pl.program_id(1)
    n_rows = rows_ref[b]
    last = f == pl.num_programs(1) - 1
    wg = wg_ref[...].astype(BF16)
    wu = wu_ref[...].astype(BF16)
    wd = wd_ref[...].astype(BF16)
    full = n_rows == MOE_BLOCK

    def swiglu(h):
        a = _silu(_dot(h, wg)) * _dot(h, wu)
        return _dot(a.astype(BF16), wd)

    @pl.when(jnp.logical_and(full, f == 0))
    def _():
        acc_ref[...] = swiglu(xs_ref[...])

    @pl.when(jnp.logical_and(full, f > 0))
    def _():
        acc_ref[...] += swiglu(xs_ref[...])

    @pl.when(jnp.logical_and(full, last))
    def _():
        y_ref[...] = acc_ref[...].astype(BF16)

    for s in range(MOE_BLOCK // MOE_SUB):
        sl = slice(s * MOE_SUB, (s + 1) * MOE_SUB)
        live = jnp.logical_and(jnp.logical_not(full), s * MOE_SUB < n_rows)
        dead = jnp.logical_and(jnp.logical_not(full), s * MOE_SUB >= n_rows)

        @pl.when(jnp.logical_and(live, f == 0))
        def _():
            acc_ref[sl, :] = jnp.zeros((MOE_SUB, acc_ref.shape[1]), F32)

        @pl.when(live)
        def _():
            acc_ref[sl, :] += swiglu(xs_ref[sl, :])

        @pl.when(jnp.logical_and(live, last))
        def _():
            y_ref[sl, :] = acc_ref[sl, :].astype(BF16)

        @pl.when(jnp.logical_and(dead, last))
        def _():
            y_ref[sl, :] = jnp.zeros((MOE_SUB, y_ref.shape[1]), BF16)


def _moe_ffn(xs, wg, wu, wd, plan, *, n_blocks, tf):
    _, d = xs.shape
    ne, _, ff = wg.shape
    nf = ff // tf

    def w_in(b, f, e_ref, valid_ref, rows_ref):
        return (e_ref[b], 0, jnp.where(valid_ref[b] > 0, f, nf - 1))

    def w_down(b, f, e_ref, valid_ref, rows_ref):
        return (e_ref[b], jnp.where(valid_ref[b] > 0, f, nf - 1), 0)

    grid_spec = pltpu.PrefetchScalarGridSpec(
        num_scalar_prefetch=3,
        grid=(n_blocks, nf),
        in_specs=[pl.BlockSpec((MOE_BLOCK, d), lambda b, f, *_: (b, 0)),
                  pl.BlockSpec((None, d, tf), w_in),
                  pl.BlockSpec((None, d, tf), w_in),
                  pl.BlockSpec((None, tf, d), w_down)],
        out_specs=pl.BlockSpec((MOE_BLOCK, d), lambda b, f, *_: (b, 0)),
        scratch_shapes=[pltpu.VMEM((MOE_BLOCK, d), F32)],
    )
    return pl.pallas_call(
        _moe_ffn_body,
        grid_spec=grid_spec,
        out_shape=jax.ShapeDtypeStruct((n_blocks * MOE_BLOCK, d), BF16),
        compiler_params=_params("arbitrary", "arbitrary"),
        name="moe_ffn",
    )(plan['blk_e'], plan['blk_valid'], plan['blk_rows'], xs, wg, wu, wd)


def _moe_combine_body(j0_ref, base_ref, x_ref, mod_ref, gates_ref, rank_ref, *rest, ne, split_tiles):
    nblk = MOE_CMB_BLOCKS
    y_refs, o_ref, o2_ref = rest[:nblk * ne], rest[nblk * ne], rest[nblk * ne + 1]
    c = pl.program_id(0)
    gates = gates_ref[...]
    rank = rank_ref[...]
    lane = lax.broadcasted_iota(jnp.int32, gates.shape, 1)
    col = lax.broadcasted_iota(jnp.int32, (1, MOE_CMB), 1).astype(F32)
    acc = None
    for e in range(ne):
        pick = lane == e
        g = jnp.sum(jnp.where(pick, gates, 0.0), axis=-1, keepdims=True)
        rk = jnp.sum(jnp.where(pick, rank, 0.0), axis=-1, keepdims=True)
        shift = (base_ref[e] - j0_ref[c * ne + e] * MOE_CMB).astype(F32)
        loc = jnp.where(rk >= 0.0, rk + shift, -1.0)
        picked = None
        for b in range(nblk):
            onehot = jnp.where(loc == col + float(b * MOE_CMB), 1.0, 0.0).astype(BF16)
            d = _dot(onehot, y_refs[nblk * e + b][...])
            picked = d if picked is None else picked + d
        contrib = g * picked
        acc = contrib if acc is None else acc + contrib
    out = x_ref[...] + mod_ref[5:6, :] * acc

    @pl.when(c < split_tiles)
    def _():
        o_ref[...] = out

    @pl.when(c >= split_tiles)
    def _():
        o2_ref[...] = out


def _moe_combine(x, mod, gates, rank, y, plan, *, ne, seg_len, t_split):
    t, d = x.shape
    tm = MOE_SUB
    split_tiles = t_split // tm
    y_specs = []
    for e in range(ne):
        for b in range(MOE_CMB_BLOCKS):
            y_specs.append(pl.BlockSpec((MOE_CMB, d), lambda c, j0, base, e=e, b=b: (j0[c * ne + e] + b, 0)))
    grid_spec = pltpu.PrefetchScalarGridSpec(
        num_scalar_prefetch=2,
        grid=(t // tm,),
        in_specs=[pl.BlockSpec((tm, d), lambda c, *_: (c, 0)),
                  pl.BlockSpec((None, 6, d), lambda c, *_: (c * tm // seg_len, 0, 0)),
                  pl.BlockSpec((tm, LANES), lambda c, *_: (c, 0)),
                  pl.BlockSpec((tm, LANES), lambda c, *_: (c, 0))] + y_specs,
        out_specs=[pl.BlockSpec((tm, d), lambda c, *_: (jnp.minimum(c, split_tiles - 1), 0)),
                   pl.BlockSpec((tm, d), lambda c, *_: (jnp.maximum(c - split_tiles, 0), 0))],
    )
    return pl.pallas_call(
        functools.partial(_moe_combine_body, ne=ne, split_tiles=split_tiles),
        grid_spec=grid_spec,
        out_shape=[jax.ShapeDtypeStruct((t_split, d), F32), jax.ShapeDtypeStruct((t - t_split, d), F32)],
        compiler_params=_params("arbitrary"),
        name="moe_combine",
    )(plan['j0'], plan['base'], x, mod, gates, rank, *([y] * (MOE_CMB_BLOCKS * ne)))


def _attn_out_moe(x, o, w_out, mod, g, w_router, wg, wu, wd, *, seg_len, t_split, top_k=2):
    t, d = x.shape
    ne = w_router.shape[1]
    n_blocks = t * top_k // MOE_BLOCK + ne
    x, h, gates, rank, rank_t = _moe_route(x, o, w_out, mod, g, w_router, seg_len=seg_len)
    plan = _moe_plan(rank, ne, n_blocks=n_blocks)
    xs = _moe_dispatch(h, rank_t, plan, n_sub=n_blocks * (MOE_BLOCK // MOE_SUB))
    y = _moe_ffn(xs, wg, wu, wd, plan, n_blocks=n_blocks, tf=MOE_TF)
    return _moe_combine(x, mod, gates, rank, y, plan, ne=ne, seg_len=seg_len, t_split=t_split)


def _even_layer(x_lat, x_ctx, cond8, p, cache_ckv, cache_kr, *, nb, lat_seq, n_ctx, ctx_seq):
    d = x_lat.shape[1]
    t_lat = nb * lat_seq
    t = t_lat + x_ctx.shape[0]
    seg_len = lat_seq
    past = cache_ckv.shape[1]
    mod = _adaln(cond8, p['w_mod'], p['b_mod'])

    tm = 512
    tables = _rope_tables(lat_seq, MLA_ROPE, (MLA_NOPE,), tm)
    z_hy, kr, q, ckvn = _even_in(x_lat, x_ctx, mod, p, tables, tm=tm, seg_len=seg_len, t_lat=t_lat,
                                 lat_seq=lat_seq)

    u, x0 = _hy_pre(z_hy, p['hy_conv_w'], p['hy_conv_b'], t_lat=t_lat, lat_seq=lat_seq, ctx_seq=ctx_seq)
    k_lat, k_ctx, k_sum = _hyena_filters(p, lat_seq=lat_seq, ctx_seq=ctx_seq)
    dbias = p['hy_dbias'].reshape(1, HY_WIDTH)
    conv_lat = _hyena_lat(k_lat, k_sum[0:1], u, seq=lat_seq, nb=nb)
    conv_ctx = _hyena_ctx(k_ctx, k_sum[1:2], u, seq=ctx_seq, nseq=n_ctx, u_row0=t_lat)

    cache_kr_p = jnp.zeros((nb, past, LANES), F32).at[:, :, :MLA_ROPE].set(cache_kr)
    ckvn_rows = jnp.concatenate(
        [jnp.concatenate([cache_ckv, ckvn[:t_lat].reshape(nb, lat_seq, -1)], axis=1).reshape(nb * (past + lat_seq), -1),
         ckvn[t_lat:]], axis=0)
    kr_rows = jnp.concatenate(
        [jnp.concatenate([cache_kr_p, kr[:t_lat].reshape(nb, lat_seq, LANES)], axis=1).reshape(nb * (past + lat_seq), LANES),
         kr[t_lat:]], axis=0)
    k_all, v_all = _mla_kv(ckvn_rows, kr_rows, p, tables, tm=tm, nb=nb, past=past, lat_seq=lat_seq)
    o = jnp.zeros((t, MLA_HEADS * MLA_V), BF16)
    o = _mla_attn(o, q, k_all, v_all, tq=ATT_TQ, pairs=1, n_seq=nb, seq_q=lat_seq, seq_k=past + lat_seq,
                  q_row0=0, k_row0=0)
    o = _mla_attn(o, q, k_all, v_all, tq=ctx_seq, pairs=MLA_HEADS // 2, n_seq=n_ctx, seq_q=ctx_seq,
                  seq_k=ctx_seq, q_row0=t_lat, k_row0=nb * (past + lat_seq))

    w_out = p['w_out'].astype(BF16)
    x = _even_out(x_lat, x_ctx, mod, conv_lat, conv_ctx, u, x0, dbias, o, w_out[:HY_WIDTH], w_out[HY_WIDTH:],
                  tm=512, seg_len=seg_len)
    x = _ffn(x, mod, p['norm2'], p['ffn_w_gate'].astype(BF16), p['ffn_w_up'].astype(BF16),
             p['ffn_w_down'].astype(BF16), tm=512, tf=1408, seg_len=seg_len)
    new_ckv = ckvn[t_lat:].reshape(n_ctx, ctx_seq, -1)
    new_kr = kr[t_lat:, :MLA_ROPE].reshape(n_ctx, ctx_seq, MLA_ROPE)
    return x, new_ckv, new_kr


def _odd_layer(x, cond8, p, cache_k, cache_v, lambda_init, *, nb, lat_seq, n_ctx, ctx_seq):
    t, d = x.shape
    t_lat = nb * lat_seq
    seg_len = lat_seq
    mod = _adaln(cond8, p['w_mod'], p['b_mod'])
    tm = 512
    tables = _rope_tables(lat_seq, DIFF_DH, (0, DIFF_DH), tm)
    q, k, v, new_k, new_v = _qkv(x, mod, p, tables, tm=tm, seg_len=seg_len, t_lat=t_lat, lat_seq=lat_seq,
                                 n_ctx=n_ctx, ctx_seq=ctx_seq)
    lam_p = jnp.stack([p['lam_q1'], p['lam_k1'], p['lam_q2'], p['lam_k2']])
    o = jnp.zeros((t, DIFF_HEADS * 2 * DIFF_DH), BF16)
    o = _diff_attn(o, q, k, v, cache_k.astype(BF16), cache_v.astype(BF16), lam_p, p['subln'],
                   tq=ATT_TQ, heads=1, n_seq=nb, seq_q=lat_seq, q_row0=0, lambda_init=lambda_init)
    o = _diff_attn(o, q, k, v, None, None, lam_p, p['subln'],
                   tq=ctx_seq, heads=DIFF_HEADS, n_seq=n_ctx, seq_q=ctx_seq, q_row0=t_lat,
                   lambda_init=lambda_init)
    x_lat, x_ctx = _attn_out_moe(x, o, p['w_out'].astype(BF16), mod, p['norm2'], p['w_router'],
                                 p['moe_w_gate'], p['moe_w_up'], p['moe_w_down'], seg_len=seg_len, t_split=t_lat)
    return x_lat, x_ctx, new_k, new_v


def kernel(x_prompt, x_sample, cache_l0_ckv, cache_l0_krope, cache_l1_k, cache_l1_v, c, c_ctx,
           l0_w_mod, l0_b_mod, l0_norm1, l0_norm2, l0_w_in, l0_hy_conv_w, l0_hy_conv_b,
           l0_hy_fw1, l0_hy_fb1, l0_hy_freq1, l0_hy_fw2, l0_hy_fb2, l0_hy_freq2, l0_hy_fw3, l0_hy_dbias,
           l0_mla_qa_norm, l0_mla_w_uq, l0_mla_kva_norm, l0_mla_w_ukv, l0_mla_q_norm, l0_mla_k_norm,
           l0_w_out, l0_ffn_w_gate, l0_ffn_w_up, l0_ffn_w_down,
           l1_w_mod, l1_b_mod, l1_norm1, l1_norm2, l1_w_qkv, l1_q_norm, l1_k_norm,
           l1_lam_q1, l1_lam_k1, l1_lam_q2, l1_lam_k2, l1_subln, l1_w_out,
           l1_w_router, l1_moe_w_gate, l1_moe_w_up, l1_moe_w_down):
    even = {
        'w_mod': l0_w_mod, 'b_mod': l0_b_mod, 'norm1': l0_norm1, 'norm2': l0_norm2, 'w_in': l0_w_in,
        'hy_conv_w': l0_hy_conv_w, 'hy_conv_b': l0_hy_conv_b, 'hy_fw1': l0_hy_fw1, 'hy_fb1': l0_hy_fb1,
        'hy_freq1': l0_hy_freq1, 'hy_fw2': l0_hy_fw2, 'hy_fb2': l0_hy_fb2, 'hy_freq2': l0_hy_freq2,
        'hy_fw3': l0_hy_fw3, 'hy_dbias': l0_hy_dbias, 'qa_norm': l0_mla_qa_norm, 'w_uq': l0_mla_w_uq,
        'kva_norm': l0_mla_kva_norm, 'w_ukv': l0_mla_w_ukv, 'q_norm': l0_mla_q_norm, 'k_norm': l0_mla_k_norm,
        'w_out': l0_w_out, 'ffn_w_gate': l0_ffn_w_gate, 'ffn_w_up': l0_ffn_w_up, 'ffn_w_down': l0_ffn_w_down,
    }
    odd = {
        'w_mod': l1_w_mod, 'b_mod': l1_b_mod, 'norm1': l1_norm1, 'norm2': l1_norm2, 'w_qkv': l1_w_qkv,
        'q_norm': l1_q_norm, 'k_norm': l1_k_norm, 'lam_q1': l1_lam_q1, 'lam_k1': l1_lam_k1,
        'lam_q2': l1_lam_q2, 'lam_k2': l1_lam_k2, 'subln': l1_subln, 'w_out': l1_w_out,
        'w_router': l1_w_router, 'moe_w_gate': l1_moe_w_gate, 'moe_w_up': l1_moe_w_up,
        'moe_w_down': l1_moe_w_down,
    }
    n_ctx, ctx_seq, d = x_prompt.shape
    nb, lat_seq, _ = x_sample.shape
    assert n_ctx * ctx_seq == lat_seq, "segment layout needs equally sized modulation segments"
    dims = dict(nb=nb, lat_seq=lat_seq, n_ctx=n_ctx, ctx_seq=ctx_seq)
    t_lat = nb * lat_seq
    cond8 = jnp.zeros((SUBLANES, d), F32).at[:nb].set(c).at[nb].set(c_ctx)

    x, new_l0_ckv, new_l0_krope = _even_layer(x_sample.reshape(t_lat, d), x_prompt.reshape(n_ctx * ctx_seq, d),
                                              cond8, even, cache_l0_ckv, cache_l0_krope, **dims)
    lambda_init = 0.8 - 0.6 * math.exp(-0.3 * 1)
    x_lat, x_ctx, new_l1_k, new_l1_v = _odd_layer(x, cond8, odd, cache_l1_k, cache_l1_v, lambda_init, **dims)

    y_sample = x_lat.reshape(nb, lat_seq, d)
    y_prompt = x_ctx.reshape(n_ctx, ctx_seq, d)
    return (y_prompt, y_sample, new_l0_ckv, new_l0_krope, new_l1_k, new_l1_v)
```

```python
import functools
import math

import numpy as np
import jax
import jax.numpy as jnp
from jax import lax
from jax.experimental import pallas as pl
from jax.experimental.pallas import tpu as pltpu

F32 = jnp.float32
BF16 = jnp.bfloat16

VMEM_LIMIT_BYTES = 56 * 1024 * 1024
LANES = 128
SUBLANES = 8
LOG2_E = math.log2(math.e)

GRID_W = 64
ROPE_BASE = 10000.0
NORM_EPS = 1e-6
HY_WIDTH = 512
HY_BANDS = 16
HY_FAST_DECAY_PCT = 0.3
HY_SLOW_DECAY_PCT = 1.5
HY_DECAY_TARGET = 1e-2
MLA_HEADS = 8
MLA_NOPE = 64
MLA_ROPE = 32
MLA_QK = MLA_NOPE + MLA_ROPE
MLA_V = 64
MLA_Q_RANK = 768
MLA_KV_RANK = 256
DIFF_HEADS = 8
DIFF_DH = 64
N_EXPERTS = 8


def _params(*sem):
    return pltpu.CompilerParams(dimension_semantics=sem, vmem_limit_bytes=VMEM_LIMIT_BYTES)


def _dot(a, b):
    return jnp.dot(a, b, preferred_element_type=F32)


def _dot_nt(a, b):
    return lax.dot_general(a, b, (((1,), (1,)), ((), ())), preferred_element_type=F32)


def _split_bf16(a):
    hi = a.astype(BF16)
    lo = (a - hi.astype(F32)).astype(BF16)
    return hi, lo


def _dot_f32(a, b):
    ah, al = _split_bf16(a)
    bh, bl = _split_bf16(b)
    return _dot(ah, bh) + (_dot(al, bh) + _dot(ah, bl))


def _rms(x, g, n=None):
    n = x.shape[-1] if n is None else n
    ms = jnp.sum(x * x, axis=-1, keepdims=True) * (1.0 / n)
    return x * lax.rsqrt(ms + NORM_EPS) * g


def _norm_mod(x, g, shift, scale):
    return _rms(x, g) * (1.0 + scale) + shift


def _silu(x):
    return x / (1.0 + jnp.exp(-x))


def _adaln_body(c_ref, w_ref, b_ref, o_ref):
    o_ref[...] = _dot_f32(_silu(c_ref[...]), w_ref[...]) + b_ref[...]


def _adaln(cond8, w_mod, b_mod):
    d, n = w_mod.shape
    tn = n // 4
    out = pl.pallas_call(
        _adaln_body,
        grid=(n // tn,),
        in_specs=[pl.BlockSpec((SUBLANES, d), lambda j: (0, 0)),
                  pl.BlockSpec((d, tn), lambda j: (0, j)),
                  pl.BlockSpec((1, tn), lambda j: (0, j))],
        out_specs=pl.BlockSpec((SUBLANES, tn), lambda j: (0, j)),
        out_shape=jax.ShapeDtypeStruct((SUBLANES, n), F32),
        compiler_params=_params("arbitrary"),
        name="adaln",
    )(cond8, w_mod, b_mod.reshape(1, n))
    return out.reshape(SUBLANES, 6, d)


def _two_part_specs(a, b, tm):
    na = a.shape[0] // tm
    cols = a.shape[1]
    return (pl.BlockSpec((tm, cols), lambda i: (jnp.minimum(i, na - 1), 0)),
            pl.BlockSpec((tm, cols), lambda i: (jnp.maximum(i - na, 0), 0)), na)


def _hy_pre_body(z_ref, zp_ref, zn_ref, w_ref, b_ref, u_ref, x0_ref, *, tm, lat_tiles, tiles_per_seq):
    i = pl.program_id(0)
    z = z_ref[...]
    in_lat = i < lat_tiles
    has_prev = jnp.logical_and(in_lat, i % tiles_per_seq != 0)
    has_next = jnp.logical_and(in_lat, i % tiles_per_seq != tiles_per_seq - 1)
    prev_row = jnp.where(has_prev, zp_ref[SUBLANES - 1:SUBLANES, :], 0.0)
    next_row = jnp.where(has_next, zn_ref[0:1, :], 0.0)
    rows = lax.broadcasted_iota(jnp.int32, z.shape, 0)
    z_m = jnp.where(rows == 0, prev_row, pltpu.roll(z, 1, 0))
    z_p = jnp.where(rows == tm - 1, next_row, pltpu.roll(z, tm - 1, 0))
    zc = b_ref[...] + z_m * w_ref[0:1, :] + z * w_ref[1:2, :] + z_p * w_ref[2:3, :]
    c = HY_WIDTH
    x0_ref[...] = zc[:, :c]
    u_ref[...] = zc[:, 2 * c:] * zc[:, c:2 * c]


def _hy_pre(z, conv_w, conv_b, *, t_lat, lat_seq, ctx_seq):
    t = z.shape[0]
    tm = ctx_seq
    c3 = 3 * HY_WIDTH
    nb8 = t // SUBLANES
    body = functools.partial(_hy_pre_body, tm=tm, lat_tiles=t_lat // tm, tiles_per_seq=lat_seq // tm)
    return pl.pallas_call(
        body,
        grid=(t // tm,),
        in_specs=[pl.BlockSpec((tm, c3), lambda i: (i, 0)),
                  pl.BlockSpec((SUBLANES, c3), lambda i: (jnp.maximum(i * (tm // SUBLANES) - 1, 0), 0)),
                  pl.BlockSpec((SUBLANES, c3), lambda i: (jnp.minimum((i + 1) * (tm // SUBLANES), nb8 - 1), 0)),
                  pl.BlockSpec((3, c3), lambda i: (0, 0)),
                  pl.BlockSpec((1, c3), lambda i: (0, 0))],
        out_specs=[pl.BlockSpec((tm, HY_WIDTH), lambda i: (i, 0)),
                   pl.BlockSpec((tm, HY_WIDTH), lambda i: (i, 0))],
        out_shape=[jax.ShapeDtypeStruct((t, HY_WIDTH), F32),
                   jax.ShapeDtypeStruct((t, HY_WIDTH), F32)],
        compiler_params=_params("parallel"),
        name="hy_pre",
    )(z, z, z, conv_w, conv_b.reshape(1, c3))


def _filter_embedding(seq):
    t01 = np.linspace(0.0, 1.0, seq)[:, None]
    w = 2.0 * math.pi * np.arange(seq)[:, None] / seq
    f = np.linspace(1e-4, HY_BANDS - 1, HY_BANDS)[None, :]
    z = np.concatenate([t01, np.cos(f * w), -np.sin(f * w)], axis=-1)
    z_rev = np.concatenate([z[:1], z[:0:-1]], axis=0)
    zz = np.concatenate([z, z_rev], axis=0)
    out = np.zeros((2 * seq, LANES), np.float32)
    out[:, :zz.shape[1]] = zz
    return out


def _filter_body(zz_ref, dl_ref, w1_ref, b1_ref, f1_ref, w2_ref, b2_ref, f2_ref, w3_ref,
                 kl_ref, kc_ref, s_ref, *, tm, lat_tiles, ctx_tiles):
    i = pl.program_id(0)
    zz = zz_ref[...]
    h = jnp.sin(f1_ref[...] * (_dot_f32(zz, w1_ref[...]) + b1_ref[...]))
    h = jnp.sin(f2_ref[...] * (_dot_f32(h, w2_ref[...]) + b2_ref[...]))
    h = _dot_f32(h, w3_ref[...])
    is_bwd = jnp.logical_or(jnp.logical_and(i >= lat_tiles // 2, i < lat_tiles),
                            i >= lat_tiles + ctx_tiles // 2)
    first_bwd = jnp.logical_or(i == lat_tiles // 2, i == lat_tiles + ctx_tiles // 2)
    window = jnp.exp(-zz[:, 0:1] * dl_ref[...])
    k = jnp.where(is_bwd, h[:, HY_WIDTH:], h[:, :HY_WIDTH]) * window
    rows = lax.broadcasted_iota(jnp.int32, k.shape, 0)
    k = jnp.where(jnp.logical_and(first_bwd, rows == 0), 0.0, k)
    s = jnp.sum(jnp.abs(k), axis=0, keepdims=True)

    @pl.when(i == 0)
    def _():
        s_ref[...] = jnp.zeros_like(s_ref)

    @pl.when(i < lat_tiles)
    def _():
        kl_ref[...] = k
        s_ref[0:1, :] += s

    @pl.when(i >= lat_tiles)
    def _():
        kc_ref[...] = k
        s_ref[1:2, :] += s


def _hyena_filters(p, *, lat_seq, ctx_seq):
    tm = ctx_seq
    zz = jnp.asarray(np.concatenate([_filter_embedding(lat_seq), _filter_embedding(ctx_seq)], axis=0))
    rows = zz.shape[0]
    max_decay = math.log(HY_DECAY_TARGET) / HY_FAST_DECAY_PCT
    min_decay = math.log(HY_DECAY_TARGET) / HY_SLOW_DECAY_PCT
    deltas = jnp.asarray(np.abs(np.linspace(min_decay, max_decay, HY_WIDTH))[None, :].astype(np.float32))
    emb, hid = p['hy_fw1'].shape

    def pad2(a, r, c):
        return jnp.zeros((r, c), F32).at[:a.shape[0], :a.shape[1]].set(a)

    w1 = pad2(p['hy_fw1'], LANES, LANES)
    b1 = pad2(p['hy_fb1'][None, :], 1, LANES)
    f1 = pad2(p['hy_freq1'][None, :], 1, LANES)
    w2 = pad2(p['hy_fw2'], LANES, LANES)
    b2 = pad2(p['hy_fb2'][None, :], 1, LANES)
    f2 = pad2(p['hy_freq2'][None, :], 1, LANES)
    w3 = pad2(p['hy_fw3'], LANES, 2 * HY_WIDTH)
    lat_tiles = 2 * lat_seq // tm
    body = functools.partial(_filter_body, tm=tm, lat_tiles=lat_tiles, ctx_tiles=2 * ctx_seq // tm)
    full = lambda i: (0, 0)
    return pl.pallas_call(
        body,
        grid=(rows // tm,),
        in_specs=[pl.BlockSpec((tm, LANES), lambda i: (i, 0)),
                  pl.BlockSpec((1, HY_WIDTH), full),
                  pl.BlockSpec((LANES, LANES), full), pl.BlockSpec((1, LANES), full), pl.BlockSpec((1, LANES), full),
                  pl.BlockSpec((LANES, LANES), full), pl.BlockSpec((1, LANES), full), pl.BlockSpec((1, LANES), full),
                  pl.BlockSpec((LANES, 2 * HY_WIDTH), full)],
        out_specs=[pl.BlockSpec((tm, HY_WIDTH), lambda i: (jnp.minimum(i, lat_tiles - 1), 0)),
                   pl.BlockSpec((tm, HY_WIDTH), lambda i: (jnp.maximum(i - lat_tiles, 0), 0)),
                   pl.BlockSpec((SUBLANES, HY_WIDTH), full)],
        out_shape=[jax.ShapeDtypeStruct((2 * lat_seq, HY_WIDTH), F32),
                   jax.ShapeDtypeStruct((2 * ctx_seq, HY_WIDTH), F32),
                   jax.ShapeDtypeStruct((SUBLANES, HY_WIDTH), F32)],
        compiler_params=_params("arbitrary"),
        name="hy_filter",
    )(zz, deltas, w1, b1, f1, w2, b2, f2, w3)


def _stack_complex(z):
    return np.block([[z.real, -z.imag], [z.imag, z.real]])


def _dft_consts_two_level(seq, n1, n2):
    n = 2 * seq
    assert n1 * n2 == n
    a1 = np.arange(n1)
    f1_full = np.exp(-2j * np.pi * np.outer(a1, a1) / n1)
    f1_u = np.concatenate([f1_full.real, f1_full.imag], axis=0)[:, :n1 // 2]
    f1_k = np.concatenate([f1_full.real, f1_full.imag], axis=0)
    a2 = np.arange(n2)
    f = a1[:, None, None] + n1 * a2[None, :, None]
    z = np.exp(-2j * np.pi * (f * a2[None, None, :]) / n)
    mf = np.stack([_stack_complex(z[i]) for i in range(n1)])
    mi = np.stack([_stack_complex(np.conj(z[i]).T) for i in range(n1)])
    g = np.exp(2j * np.pi * np.outer(a1[:n1 // 2], a1) / n1) / n
    g1 = np.concatenate([_kron_rows(g.real), _kron_rows(-g.imag)], axis=1)
    as32 = lambda a: jnp.asarray(a.astype(np.float32))
    return as32(_kron_rows(f1_u)), as32(_kron_rows(f1_k)), as32(mf), as32(mi), as32(g1)


def _dft_consts_one_level(seq):
    n = 2 * seq
    a = np.arange(n)
    z = np.exp(-2j * np.pi * np.outer(a, a) / n)
    mf = np.concatenate([z.real, z.imag], axis=0)
    zi = np.exp(2j * np.pi * np.outer(a[:seq], a) / n) / n
    mi = np.concatenate([zi.real, -zi.imag], axis=1)
    as32 = lambda a: jnp.asarray(a.astype(np.float32))
    return as32(mf), as32(mi)


HY_ROWS = 16


def _kron_rows(m):
    return np.kron(m, np.eye(HY_ROWS))


def _dft1_body(m_ref, x_ref, sc_ref, or_ref, oi_ref):
    k, r, c = x_ref.shape
    x = (x_ref[...] * (1.0 / sc_ref[...])).reshape(k * r, c).astype(BF16)
    o = _dot(m_ref[...], x)
    h = o.shape[0] // 2
    or_ref[...] = o[:h].reshape(or_ref.shape).astype(BF16)
    oi_ref[...] = o[h:].reshape(oi_ref.shape).astype(BF16)


def _dft1(m, x, scale_row, *, n1, groups):
    _, k, n2, c = x.shape
    g = groups
    ospec = pl.BlockSpec((None, n1, HY_ROWS, c), lambda b, j: (b, 0, j, 0))
    return pl.pallas_call(
        _dft1_body,
        grid=(g, n2 // HY_ROWS),
        in_specs=[pl.BlockSpec(m.shape, lambda b, j: (0, 0)),
                  pl.BlockSpec((None, k, HY_ROWS, c), lambda b, j: (b, 0, j, 0)),
                  pl.BlockSpec((1, c), lambda b, j: (0, 0))],
        out_specs=[ospec, ospec],
        out_shape=[jax.ShapeDtypeStruct((g, n1, n2, c), BF16)] * 2,
        compiler_params=_params("parallel", "parallel"),
        name="hy_dft1",
    )(m, x, scale_row)


SPEC_GROUP = 16


def _spec_fwd_body(mf_ref, ar_ref, ai_ref, kr_ref, ki_ref):
    for g in range(SPEC_GROUP):
        a = jnp.concatenate([ar_ref[g], ai_ref[g]], axis=0)
        x = _dot(mf_ref[g], a)
        h = x.shape[0] // 2
        kr_ref[g] = x[:h]
        ki_ref[g] = x[h:]


def _spec_fwd(mf, ar, ai):
    n1, n2, c = ar.shape
    spec = pl.BlockSpec((SPEC_GROUP, n2, c), lambda i: (i, 0, 0))
    return pl.pallas_call(
        _spec_fwd_body,
        grid=(n1 // SPEC_GROUP,),
        in_specs=[pl.BlockSpec((SPEC_GROUP, 2 * n2, 2 * n2), lambda i: (i, 0, 0)), spec, spec],
        out_specs=[spec, spec],
        out_shape=[jax.ShapeDtypeStruct((n1, n2, c), F32)] * 2,
        compiler_params=_params("parallel"),
        name="hy_spec_filter",
    )(mf, ar, ai)


def _spec_mul_body(mf_ref, mi_ref, kr_ref, ki_ref, ar_ref, ai_ref, br_ref, bi_ref):
    for g in range(SPEC_GROUP):
        a = jnp.concatenate([ar_ref[g], ai_ref[g]], axis=0)
        x = _dot(mf_ref[g], a)
        h = x.shape[0] // 2
        xr, xi = x[:h], x[h:]
        kr, ki = kr_ref[g], ki_ref[g]
        y = jnp.concatenate([xr * kr - xi * ki, xr * ki + xi * kr], axis=0).astype(BF16)
        b = _dot(mi_ref[g], y)
        br_ref[g] = b[:h].astype(BF16)
        bi_ref[g] = b[h:].astype(BF16)


def _spec_mul(mf, mi, kr, ki, ar, ai):
    nb, n1, n2, c = ar.shape
    mspec = pl.BlockSpec((SPEC_GROUP, 2 * n2, 2 * n2), lambda i, b: (i, 0, 0))
    kspec = pl.BlockSpec((SPEC_GROUP, n2, c), lambda i, b: (i, 0, 0))
    aspec = pl.BlockSpec((None, SPEC_GROUP, n2, c), lambda i, b: (b, i, 0, 0))
    return pl.pallas_call(
        _spec_mul_body,
        grid=(n1 // SPEC_GROUP, nb),
        in_specs=[mspec, mspec, kspec, kspec, aspec, aspec],
        out_specs=[aspec, aspec],
        out_shape=[jax.ShapeDtypeStruct((nb, n1, n2, c), BF16)] * 2,
        compiler_params=_params("parallel", "arbitrary"),
        name="hy_spec_mul",
    )(mf, mi, kr, ki, ar, ai)


def _idft1_body(m_ref, br_ref, bi_ref, o_ref):
    n1, r, c = br_ref.shape
    b = jnp.concatenate([br_ref[...].reshape(n1 * r, c), bi_ref[...].reshape(n1 * r, c)], axis=0)
    o_ref[...] = _dot(m_ref[...], b).reshape(o_ref.shape)


def _idft1(m, br, bi, *, n_out):
    nb, n1, n2, c = br.shape
    bspec = pl.BlockSpec((None, n1, HY_ROWS, c), lambda b, j: (b, 0, j, 0))
    return pl.pallas_call(
        _idft1_body,
        grid=(nb, n2 // HY_ROWS),
        in_specs=[pl.BlockSpec(m.shape, lambda b, j: (0, 0)), bspec, bspec],
        out_specs=pl.BlockSpec((None, n_out, HY_ROWS, c), lambda b, j: (b, 0, j, 0)),
        out_shape=jax.ShapeDtypeStruct((nb, n_out, n2, c), F32),
        compiler_params=_params("parallel", "parallel"),
        name="hy_idft1",
    )(m, br, bi)


def _ctx_filter_body(mf_ref, k_ref, sc_ref, kf_ref):
    kf_ref[...] = _dot(mf_ref[...], (k_ref[...] * (1.0 / sc_ref[...])).astype(BF16))


def _ctx_conv_body(mf_ref, mi_ref, kf_ref, u_ref, o_ref):
    x = _dot(mf_ref[...], u_ref[...].astype(BF16))
    h = x.shape[0] // 2
    xr, xi = x[:h], x[h:]
    kr, ki = kf_ref[:h, :], kf_ref[h:, :]
    y = jnp.concatenate([xr * kr - xi * ki, xr * ki + xi * kr], axis=0).astype(BF16)
    o_ref[...] = _dot(mi_ref[...], y)


def _hyena_ctx(k_raw, k_norm1, u, *, seq, nseq, u_row0):
    mf, mi = _dft_consts_one_level(seq)
    n = 2 * seq
    c = u.shape[1]
    kf = pl.pallas_call(
        _ctx_filter_body,
        out_shape=jax.ShapeDtypeStruct((2 * n, c), F32),
        compiler_params=_params(),
        name="hy_ctx_filter",
    )(mf.astype(BF16), k_raw, k_norm1)
    full = lambda s: (0, 0)
    return pl.pallas_call(
        _ctx_conv_body,
        grid=(nseq,),
        in_specs=[pl.BlockSpec((2 * n, seq), full),
                  pl.BlockSpec((seq, 2 * n), full),
                  pl.BlockSpec((2 * n, c), full),
                  pl.BlockSpec((seq, c), lambda s: (u_row0 // seq + s, 0))],
        out_specs=pl.BlockSpec((seq, c), lambda s: (s, 0)),
        out_shape=jax.ShapeDtypeStruct((nseq * seq, c), F32),
        compiler_params=_params("parallel"),
        name="hy_ctx_conv",
    )(mf[:, :seq].astype(BF16), mi.astype(BF16), kf, u)


def _hyena_lat(k_raw, k_norm1, u, *, seq, nb):
    c = u.shape[1]
    n1, n2 = 64, 2 * seq // 64
    f1_u, f1_k, mf, mi, g1 = _dft_consts_two_level(seq, n1, n2)
    mf = mf.astype(BF16)
    mi = mi.astype(BF16)
    ones_row = jnp.ones((1, c), F32)
    akr, aki = _dft1(f1_k.astype(BF16), k_raw.reshape(-1, n1, n2, c), k_norm1, n1=n1, groups=1)
    kr, ki = _spec_fwd(mf, akr[0], aki[0])
    ar, ai = _dft1(f1_u.astype(BF16), u.reshape(-1, n1 // 2, n2, c), ones_row, n1=n1, groups=nb)
    br, bi = _spec_mul(mf, mi, kr, ki, ar, ai)
    y = _idft1(g1.astype(BF16), br, bi, n_out=n1 // 2)
    return y.reshape(nb * seq, c)


def _rope_tables(seq, rope_dims, lane_offsets, pad_rows):
    rows = seq // GRID_W
    rr, cc = np.meshgrid(np.arange(rows), np.arange(GRID_W), indexing='ij')
    pos = (rr.reshape(-1).astype(np.float64), cc.reshape(-1).astype(np.float64))
    half = rope_dims // 2
    q = half // 2
    inv_freq = ROPE_BASE ** (-np.arange(0, half, 2, dtype=np.float64) / half)
    cos_t = np.ones((seq + pad_rows, LANES), np.float64)
    sin_a = np.zeros((seq + pad_rows, LANES), np.float64)
    sin_b = np.zeros((seq + pad_rows, LANES), np.float64)
    for off in lane_offsets:
        for axis in range(2):
            ang = pos[axis][:, None] * inv_freq[None, :]
            base = off + axis * half
            cos_t[:seq, base:base + q] = np.cos(ang)
            cos_t[:seq, base + q:base + half] = np.cos(ang)
            sin_b[:seq, base:base + q] = -np.sin(ang)
            sin_a[:seq, base + q:base + half] = np.sin(ang)
    as32 = lambda a: jnp.asarray(a.astype(np.float32))
    return as32(cos_t), as32(sin_a), as32(sin_b)


def _rope(x, cos_t, sin_a, sin_b, shift):
    return x * cos_t + pltpu.roll(x, shift, 1) * sin_a + pltpu.roll(x, LANES - shift, 1) * sin_b


def _even_in_body(xa_ref, xb_ref, mod_ref, g_ref, w_ref, qa_ref, kva_ref, wuq_ref, qn_ref,
                  cos_ref, sa_ref, sb_ref, zhy_ref, kr_ref, q_ref, ckvn_ref, zq_ref, *, na, scale):
    i = pl.program_id(0)
    n = pl.num_programs(0) - 1

    @pl.when(i == 0)
    def _():
        zq_ref[1] = jnp.zeros(zq_ref.shape[1:], F32)

    c_q = MLA_Q_RANK
    zp = zq_ref[(i + 1) % 2]
    ckvn_ref[...] = _rms(zp[:, c_q:], kva_ref[...])
    cqn = _rms(zp[:, :c_q], qa_ref[...])
    q = _dot(cqn.astype(BF16), wuq_ref[...])
    cos_t, sin_a, sin_b = cos_ref[...], sa_ref[...], sb_ref[...]
    g = qn_ref[...]
    for hh in range(MLA_HEADS):
        sl = slice(hh * LANES, (hh + 1) * LANES)
        qh = _rope(_rms(q[:, sl], g, MLA_QK), cos_t, sin_a, sin_b, MLA_ROPE // 4)
        q_ref[:, sl] = (qh * scale).astype(BF16)

    x = jnp.where(jnp.minimum(i, n - 1) < na, xa_ref[...], xb_ref[...])
    h = _norm_mod(x, g_ref[...], mod_ref[0:1, :], mod_ref[1:2, :])
    z = _dot(h.astype(BF16), w_ref[...])
    c_hy = 3 * HY_WIDTH
    c_kv = c_hy + MLA_Q_RANK + MLA_KV_RANK
    zhy_ref[...] = z[:, :c_hy]
    kr_ref[...] = z[:, c_kv:]
    zq_ref[i % 2] = z[:, c_hy:c_kv]


def _pad_heads(w, heads, width):
    k = w.shape[0]
    w3 = w.reshape(k, heads, width)
    return jnp.zeros((k, heads, LANES), w.dtype).at[:, :, :width].set(w3).reshape(k, heads * LANES)


def _even_in(xa, xb, mod, p, tables, *, tm, seg_len, t_lat, lat_seq):
    d = xa.shape[1]
    t = xa.shape[0] + xb.shape[0]
    n_in = p['w_in'].shape[1]
    c_kv = 3 * HY_WIDTH + MLA_Q_RANK + MLA_KV_RANK
    assert c_kv % LANES == 0 and n_in - c_kv == MLA_ROPE
    n_pad = c_kv + LANES
    w_in = jnp.zeros((d, n_pad), BF16).at[:, :n_in].set(p['w_in'].astype(BF16))
    wuq = _pad_heads(p['w_uq'], MLA_HEADS, MLA_QK).astype(BF16)
    cos_t, sin_a, sin_b = tables
    qn = jnp.zeros((1, LANES), F32).at[0, :MLA_QK].set(p['q_norm'])
    pos_blocks = lat_seq // tm
    lat_tiles = t_lat // tm
    n = t // tm
    na = xa.shape[0] // tm
    cur = lambda i: jnp.minimum(i, n - 1)
    prev = lambda i: jnp.maximum(i - 1, 0)

    def tmap(i):
        j = prev(i)
        return (jnp.where(j < lat_tiles, j % pos_blocks, pos_blocks), 0)

    tspec = pl.BlockSpec((tm, LANES), tmap)
    full = lambda i: (0, 0)
    row_cur = lambda cols: pl.BlockSpec((tm, cols), lambda i: (cur(i), 0))
    row_prev = lambda cols: pl.BlockSpec((tm, cols), lambda i: (prev(i), 0))
    return pl.pallas_call(
        functools.partial(_even_in_body, na=na, scale=MLA_QK ** -0.5 * LOG2_E),
        grid=(n + 1,),
        in_specs=[pl.BlockSpec((tm, d), lambda i: (jnp.minimum(cur(i), na - 1), 0)),
                  pl.BlockSpec((tm, d), lambda i: (jnp.maximum(cur(i) - na, 0), 0)),
                  pl.BlockSpec((None, 6, d), lambda i: (cur(i) * tm // seg_len, 0, 0)),
                  pl.BlockSpec((1, d), full),
                  pl.BlockSpec((d, n_pad), full),
                  pl.BlockSpec((1, MLA_Q_RANK), full),
                  pl.BlockSpec((1, MLA_KV_RANK), full),
                  pl.BlockSpec((MLA_Q_RANK, MLA_HEADS * LANES), full),
                  pl.BlockSpec((1, LANES), full),
                  tspec, tspec, tspec],
        out_specs=[row_cur(3 * HY_WIDTH), row_cur(LANES), row_prev(MLA_HEADS * LANES), row_prev(MLA_KV_RANK)],
        out_shape=[jax.ShapeDtypeStruct((t, 3 * HY_WIDTH), F32),
                   jax.ShapeDtypeStruct((t, LANES), F32),
                   jax.ShapeDtypeStruct((t, MLA_HEADS * LANES), BF16),
                   jax.ShapeDtypeStruct((t, MLA_KV_RANK), F32)],
        scratch_shapes=[pltpu.VMEM((2, tm, MLA_Q_RANK + MLA_KV_RANK), F32)],
        compiler_params=_params("arbitrary"),
        name="even_in",
    )(xa, xb, mod, p['norm1'].reshape(1, d), w_in, p['qa_norm'].reshape(1, -1), p['kva_norm'].reshape(1, -1),
      wuq, qn, cos_t, sin_a, sin_b)


def _mla_kv_body(ckvn_ref, kr_ref, wk_ref, wv_ref, kn_ref, cos_ref, sa_ref, sb_ref, k_ref, v_ref):
    c = ckvn_ref[...].astype(BF16)
    k = _dot(c, wk_ref[...])
    v_ref[...] = _dot(c, wv_ref[...]).astype(BF16)
    kr = pltpu.roll(kr_ref[...], MLA_NOPE, 1)
    cos_t, sin_a, sin_b = cos_ref[...], sa_ref[...], sb_ref[...]
    g = kn_ref[...]
    for h in range(MLA_HEADS):
        sl = slice(h * LANES, (h + 1) * LANES)
        kh = _rope(_rms(k[:, sl] + kr, g, MLA_QK), cos_t, sin_a, sin_b, MLA_ROPE // 4)
        k_ref[:, sl] = kh.astype(BF16)


def _mla_kv(ckvn_rows, kr_rows, p, tables, *, tm, nb, past, lat_seq):
    r = ckvn_rows.shape[0]
    w = p['w_ukv'].reshape(MLA_KV_RANK, MLA_HEADS, MLA_NOPE + MLA_V)
    wk = _pad_heads(w[:, :, :MLA_NOPE].reshape(MLA_KV_RANK, -1), MLA_HEADS, MLA_NOPE).astype(BF16)
    wv = w[:, :, MLA_NOPE:].reshape(MLA_KV_RANK, MLA_HEADS * MLA_V).astype(BF16)
    cos_t, sin_a, sin_b = tables
    kn = jnp.zeros((1, LANES), F32).at[0, :MLA_QK].set(p['k_norm'])
    per_b = (past + lat_seq) // tm
    past_tiles = past // tm
    pos_blocks = lat_seq // tm
    lat_tiles = nb * per_b

    def tmap(i):
        j = i % per_b
        is_pos = jnp.logical_and(i < lat_tiles, j >= past_tiles)
        return (jnp.where(is_pos, j - past_tiles, pos_blocks), 0)

    tspec = pl.BlockSpec((tm, LANES), tmap)
    full = lambda i: (0, 0)
    return pl.pallas_call(
        _mla_kv_body,
        grid=(r // tm,),
        in_specs=[pl.BlockSpec((tm, MLA_KV_RANK), lambda i: (i, 0)),
                  pl.BlockSpec((tm, LANES), lambda i: (i, 0)),
                  pl.BlockSpec((MLA_KV_RANK, MLA_HEADS * LANES), full),
                  pl.BlockSpec((MLA_KV_RANK, MLA_HEADS * MLA_V), full),
                  pl.BlockSpec((1, LANES), full),
                  tspec, tspec, tspec],
        out_specs=[pl.BlockSpec((tm, MLA_HEADS * LANES), lambda i: (i, 0)),
                   pl.BlockSpec((tm, MLA_HEADS * MLA_V), lambda i: (i, 0))],
        out_shape=[jax.ShapeDtypeStruct((r, MLA_HEADS * LANES), BF16),
                   jax.ShapeDtypeStruct((r, MLA_HEADS * MLA_V), BF16)],
        compiler_params=_params("parallel"),
        name="mla_kv",
    )(ckvn_rows, kr_rows, wk, wv, kn, cos_t, sin_a, sin_b)


ATT_CHUNK = 512
ATT_UNIT_ROWS = 256
ATT_TQ = 2048


def _fill_vaug(vaug_ref, g, v_blocks):
    off = 0
    for v in v_blocks:
        n = v.shape[0]
        vaug_ref[g, off:off + n, :LANES] = v
        off += n
    vaug_ref[g, :, LANES:] = jnp.ones((vaug_ref.shape[1], LANES), BF16)


def _softmax_pv(units, s_ref, n_keys):
    chunk = min(ATT_CHUNK, n_keys)
    chunks = [slice(c * chunk, (c + 1) * chunk) for c in range(n_keys // chunk)]

    def scores(u, rows, m_lane):
        s = _dot_nt(units[u][0], units[u][1](rows))
        s_ref[u % 2, :, rows] = s
        for j in range(chunk // LANES):
            blk = s[:, j * LANES:(j + 1) * LANES]
            m_lane = blk if m_lane is None else jnp.maximum(m_lane, blk)
        return m_lane

    def values(u, rows, m, acc):
        p = jnp.exp2(s_ref[u % 2, :, rows] - m).astype(BF16)
        d = _dot(p, units[u][2](rows))
        return d if acc is None else acc + d

    outs = []
    m_lane = None
    for rows in chunks:
        m_lane = scores(0, rows, m_lane)
    for u in range(len(units)):
        m = jnp.max(m_lane, axis=-1, keepdims=True)
        acc, m_lane = None, None
        for rows in chunks:
            acc = values(u, rows, m, acc)
            if u + 1 < len(units):
                m_lane = scores(u + 1, rows, m_lane)
        outs.append(acc)
    return outs


def _mla_attn_body(prev_ref, q_ref, k_ref, v_ref, o_ref, vaug_ref, s_ref):
    pairs = vaug_ref.shape[0]

    @pl.when(pl.program_id(2) == 0)
    def _():
        for g in range(pairs):
            _fill_vaug(vaug_ref, g, [v_ref[:, g * LANES:(g + 1) * LANES]])

    n_keys = k_ref.shape[0]
    tq = q_ref.shape[0]
    ur = s_ref.shape[1]
    units, slots = [], []
    for r0 in range(0, tq, ur):
        for g in range(pairs):
            slots.append((r0, g))
            for hh in range(2):
                sl = slice((2 * g + hh) * LANES, (2 * g + hh + 1) * LANES)
                units.append((q_ref[r0:r0 + ur, sl], lambda rows, sl=sl: k_ref[rows, sl],
                              lambda rows, g=g: vaug_ref[g, rows, :]))
    res = [r[:, :LANES] / r[:, LANES:] for r in _softmax_pv(units, s_ref, n_keys)]
    lane = lax.broadcasted_iota(jnp.int32, res[0].shape, 1)
    for i, (r0, g) in enumerate(slots):
        o_ref[r0:r0 + ur, g * LANES:(g + 1) * LANES] = jnp.where(
            lane < MLA_V, res[2 * i], res[2 * i + 1]).astype(BF16)


def _mla_attn(prev, q, k, v, *, tq, pairs, n_seq, seq_q, seq_k, q_row0, k_row0):
    hp = MLA_HEADS // 2 // pairs
    nq = seq_q // tq
    qb0, kb0 = q_row0 // tq, k_row0 // seq_k
    return pl.pallas_call(
        _mla_attn_body,
        grid=(n_seq, hp, nq),
        in_specs=[pl.BlockSpec(memory_space=pl.ANY),
                  pl.BlockSpec((tq, pairs * 2 * LANES), lambda s, h, i: (qb0 + s * nq + i, h)),
                  pl.BlockSpec((seq_k, pairs * 2 * LANES), lambda s, h, i: (kb0 + s, h)),
                  pl.BlockSpec((seq_k, pairs * 2 * MLA_V), lambda s, h, i: (kb0 + s, h))],
        out_specs=pl.BlockSpec((tq, pairs * 2 * MLA_V), lambda s, h, i: (qb0 + s * nq + i, h)),
        out_shape=jax.ShapeDtypeStruct(prev.shape, prev.dtype),
        input_output_aliases={0: 0},
        scratch_shapes=[pltpu.VMEM((pairs, seq_k, 2 * LANES), BF16),
                        pltpu.VMEM((2, min(tq, ATT_UNIT_ROWS), seq_k), F32)],
        compiler_params=_params("arbitrary", "arbitrary", "arbitrary"),
        name="mla_attn",
    )(prev, q, k, v)


def _diff_attn_body(prev_ref, *refs, n_seg, lambda_init):
    q_ref, lam_ref, sub_ref = refs[0], refs[1], refs[2]
    k_refs = refs[3:3 + n_seg]
    v_refs = refs[3 + n_seg:3 + 2 * n_seg]
    o_ref, kcat_ref, vaug_ref, s_ref = refs[3 + 2 * n_seg:]
    heads = vaug_ref.shape[0]

    def head_block(ref, g):
        return ref[g] if len(ref.shape) == 3 else ref[:, g * LANES:(g + 1) * LANES]

    @pl.when(pl.program_id(2) == 0)
    def _():
        for g in range(heads):
            _fill_vaug(vaug_ref, g, [head_block(v, g) for v in v_refs])
            off = 0
            for k in k_refs:
                kcat_ref[g, off:off + k.shape[-2], :] = head_block(k, g)
                off += k.shape[-2]

    lp = lam_ref[...]
    lam = (jnp.exp(jnp.sum(lp[0:1] * lp[1:2], axis=-1, keepdims=True))
           - jnp.exp(jnp.sum(lp[2:3] * lp[3:4], axis=-1, keepdims=True)) + lambda_init)
    n_keys = kcat_ref.shape[1]
    tq = q_ref.shape[0]
    ur = s_ref.shape[1]
    units, slots = [], []
    for r0 in range(0, tq, ur):
        for g in range(heads):
            slots.append((r0, g))
            q = q_ref[r0:r0 + ur, g * LANES:(g + 1) * LANES].astype(F32)
            lane = lax.broadcasted_iota(jnp.int32, q.shape, 1)
            k_of = lambda rows, g=g: kcat_ref[g, rows, :]
            v_of = lambda rows, g=g: vaug_ref[g, rows, :]
            units.append((jnp.where(lane < DIFF_DH, q, 0.0).astype(BF16), k_of, v_of))
            units.append((jnp.where(lane < DIFF_DH, 0.0, q).astype(BF16), k_of, v_of))
    res = _softmax_pv(units, s_ref, n_keys)
    for i, (r0, g) in enumerate(slots):
        r1, r2 = res[2 * i], res[2 * i + 1]
        o = r1[:, :LANES] / r1[:, LANES:] - (lam / r2[:, LANES:]) * r2[:, :LANES]
        o_ref[r0:r0 + ur, g * LANES:(g + 1) * LANES] = (
            _rms(o, sub_ref[...]) * (1.0 - lambda_init)).astype(BF16)


def _diff_attn(prev, q, k_new, v_new, k_cache, v_cache, lam_p, subln, *, tq, heads, n_seq, seq_q, q_row0,
               lambda_init):
    nq = seq_q // tq
    qb0 = q_row0 // tq
    sb0 = q_row0 // seq_q
    d = 2 * DIFF_DH
    new_spec = pl.BlockSpec((seq_q, heads * d), lambda s, h, i: (sb0 + s, h))
    if k_cache is None:
        n_seg, k_args, v_args, k_specs, v_specs = 1, [k_new], [v_new], [new_spec], [new_spec]
        n_keys = seq_q
    else:
        past = k_cache.shape[2]
        c_spec = pl.BlockSpec((None, heads, past, d), lambda s, h, i: (s, h, 0, 0))
        n_seg, k_args, v_args = 2, [k_cache, k_new], [v_cache, v_new]
        k_specs, v_specs = [c_spec, new_spec], [c_spec, new_spec]
        n_keys = past + seq_q
    return pl.pallas_call(
        functools.partial(_diff_attn_body, n_seg=n_seg, lambda_init=lambda_init),
        grid=(n_seq, DIFF_HEADS // heads, nq),
        in_specs=[pl.BlockSpec(memory_space=pl.ANY),
                  pl.BlockSpec((tq, heads * d), lambda s, h, i: (qb0 + s * nq + i, h)),
                  pl.BlockSpec((4, DIFF_DH), lambda s, h, i: (0, 0)),
                  pl.BlockSpec((1, d), lambda s, h, i: (0, 0))] + k_specs + v_specs,
        out_specs=pl.BlockSpec((tq, heads * d), lambda s, h, i: (qb0 + s * nq + i, h)),
        out_shape=jax.ShapeDtypeStruct(prev.shape, prev.dtype),
        input_output_aliases={0: 0},
        scratch_shapes=[pltpu.VMEM((heads, n_keys, d), BF16), pltpu.VMEM((heads, n_keys, 2 * LANES), BF16),
                        pltpu.VMEM((2, min(tq, ATT_UNIT_ROWS), n_keys), F32)],
        compiler_params=_params("arbitrary", "arbitrary", "arbitrary"),
        name="diff_attn",
    )(prev, q, lam_p, subln.reshape(1, d), *k_args, *v_args)


def _even_out_body(xa_ref, xb_ref, mod_ref, ca_ref, cb_ref, u_ref, x0_ref, db_ref, o_ref, wa_ref, wb_ref,
                   out_ref, *, na):
    first = pl.program_id(0) < na
    x = jnp.where(first, xa_ref[...], xb_ref[...])
    conv = jnp.where(first, ca_ref[...], cb_ref[...])
    y_hy = (conv + u_ref[...] * db_ref[...]) * x0_ref[...]
    acc = _dot(y_hy.astype(BF16), wa_ref[...]) + _dot(o_ref[...], wb_ref[...])
    out_ref[...] = x + mod_ref[2:3, :] * acc


def _even_out(xa, xb, mod, conv_a, conv_b, u, x0, dbias, o, w_hy, w_att, *, tm, seg_len):
    d = xa.shape[1]
    t = xa.shape[0] + xb.shape[0]
    c = u.shape[1]
    xa_spec, xb_spec, na = _two_part_specs(xa, xb, tm)
    ca_spec, cb_spec, na_c = _two_part_specs(conv_a, conv_b, tm)
    assert na == na_c
    row = lambda cols: pl.BlockSpec((tm, cols), lambda i: (i, 0))
    full = lambda a: pl.BlockSpec(a.shape, lambda i: (0, 0))
    return pl.pallas_call(
        functools.partial(_even_out_body, na=na),
        grid=(t // tm,),
        in_specs=[xa_spec, xb_spec,
                  pl.BlockSpec((None, 6, d), lambda i: (i * tm // seg_len, 0, 0)),
                  ca_spec, cb_spec, row(c), row(c), full(dbias), row(o.shape[1]), full(w_hy), full(w_att)],
        out_specs=row(d),
        out_shape=jax.ShapeDtypeStruct((t, d), F32),
        compiler_params=_params("arbitrary"),
        name="even_out",
    )(xa, xb, mod, conv_a, conv_b, u, x0, dbias, o, w_hy, w_att)


def _ffn_body(x_ref, mod_ref, g_ref, wg_ref, wu_ref, wd_ref, o_ref, h_ref, acc_ref):
    f = pl.program_id(1)

    @pl.when(f == 0)
    def _():
        h_ref[...] = _norm_mod(x_ref[...], g_ref[...], mod_ref[3:4, :], mod_ref[4:5, :]).astype(BF16)
        acc_ref[...] = jnp.zeros_like(acc_ref)

    h = h_ref[...]
    a = _silu(_dot(h, wg_ref[...])) * _dot(h, wu_ref[...])
    acc_ref[...] += _dot(a.astype(BF16), wd_ref[...])

    @pl.when(f == pl.num_programs(1) - 1)
    def _():
        o_ref[...] = x_ref[...] + mod_ref[5:6, :] * acc_ref[...]


def _ffn(x, mod, g, wg, wu, wd, *, tm, tf, seg_len):
    t, d = x.shape
    ff = wg.shape[1]
    return pl.pallas_call(
        _ffn_body,
        grid=(t // tm, ff // tf),
        in_specs=[pl.BlockSpec((tm, d), lambda i, f: (i, 0)),
                  pl.BlockSpec((None, 6, d), lambda i, f: (i * tm // seg_len, 0, 0)),
                  pl.BlockSpec((1, d), lambda i, f: (0, 0)),
                  pl.BlockSpec((d, tf), lambda i, f: (0, f)),
                  pl.BlockSpec((d, tf), lambda i, f: (0, f)),
                  pl.BlockSpec((tf, d), lambda i, f: (f, 0))],
        out_specs=pl.BlockSpec((tm, d), lambda i, f: (i, 0)),
        out_shape=jax.ShapeDtypeStruct((t, d), F32),
        scratch_shapes=[pltpu.VMEM((tm, d), BF16), pltpu.VMEM((tm, d), F32)],
        compiler_params=_params("parallel", "arbitrary"),
        name="ffn",
    )(x, mod, g.reshape(1, d), wg, wu, wd)


def _group_ms(x, gmat):
    hi, lo = _split_bf16(x * x)
    return (_dot(hi, gmat) + _dot(lo, gmat)) * (1.0 / DIFF_DH)


def _qkv_body(x_ref, mod_ref, g_ref, w_ref, gm_ref, qn_ref, kn_ref, cos_ref, sa_ref, sb_ref,
              q_ref, k_ref, v_ref, kf_ref, vf_ref, *, scale, seq):
    h = _norm_mod(x_ref[...], g_ref[...], mod_ref[0:1, :], mod_ref[1:2, :]).astype(BF16)
    z = _dot(h, w_ref[...])
    hd = DIFF_HEADS * 2 * DIFF_DH
    tm = z.shape[0]
    cos_t, sin_a, sin_b = cos_ref[...], sa_ref[...], sb_ref[...]
    gm = gm_ref[...]
    shift = DIFF_DH // 4
    for hh in range(DIFF_HEADS):
        sl = slice(hh * LANES, (hh + 1) * LANES)
        qh = z[:, hh * LANES:(hh + 1) * LANES]
        qh = qh * lax.rsqrt(_group_ms(qh, gm) + NORM_EPS) * qn_ref[...]
        q_ref[:, sl] = (_rope(qh, cos_t, sin_a, sin_b, shift) * scale).astype(BF16)
        kh = z[:, hd + hh * LANES:hd + (hh + 1) * LANES]
        kh = kh * lax.rsqrt(_group_ms(kh, gm) + NORM_EPS) * kn_ref[...]
        k_ref[:, sl] = _rope(kh, cos_t, sin_a, sin_b, shift).astype(BF16)
        vh = z[:, 2 * hd + hh * LANES:2 * hd + (hh + 1) * LANES]
        for s in range(tm // seq):
            kf_ref[s, hh] = kh[s * seq:(s + 1) * seq, :]
            vf_ref[s, hh] = vh[s * seq:(s + 1) * seq, :]
    v_ref[...] = z[:, 2 * hd:].astype(BF16)


def _qkv(x, mod, p, tables, *, tm, seg_len, t_lat, lat_seq, n_ctx, ctx_seq):
    t, d = x.shape
    hd = DIFF_HEADS * 2 * DIFF_DH
    wqk = p['w_qkv'][:, :2 * hd].reshape(d, 2, 2, DIFF_HEADS, DIFF_DH)
    wqk = wqk.transpose(0, 1, 3, 2, 4).reshape(d, 2 * hd)
    w = jnp.concatenate([wqk, p['w_qkv'][:, 2 * hd:]], axis=1).astype(BF16)
    gi = np.arange(LANES) // DIFF_DH
    gmat = jnp.asarray((gi[:, None] == gi[None, :]).astype(np.float32)).astype(BF16)
    cos_t, sin_a, sin_b = tables
    qn = jnp.tile(p['q_norm'], 2).reshape(1, LANES)
    kn = jnp.tile(p['k_norm'], 2).reshape(1, LANES)
    pos_blocks = lat_seq // tm
    lat_tiles = t_lat // tm
    seq_per_tile = tm // ctx_seq
    tspec = pl.BlockSpec((tm, LANES), lambda i: (jnp.where(i < lat_tiles, i % pos_blocks, pos_blocks), 0))
    full = lambda i: (0, 0)
    row = pl.BlockSpec((tm, hd), lambda i: (i, 0))
    fspec = pl.BlockSpec((seq_per_tile, DIFF_HEADS, ctx_seq, LANES),
                         lambda i: (jnp.maximum(i - lat_tiles, 0), 0, 0, 0))
    fshape = jax.ShapeDtypeStruct((n_ctx, DIFF_HEADS, ctx_seq, LANES), F32)
    return pl.pallas_call(
        functools.partial(_qkv_body, scale=DIFF_DH ** -0.5 * LOG2_E, seq=ctx_seq),
        grid=(t // tm,),
        in_specs=[pl.BlockSpec((tm, d), lambda i: (i, 0)),
                  pl.BlockSpec((None, 6, d), lambda i: (i * tm // seg_len, 0, 0)),
                  pl.BlockSpec((1, d), full),
                  pl.BlockSpec((d, 3 * hd), full),
                  pl.BlockSpec((LANES, LANES), full),
                  pl.BlockSpec((1, LANES), full), pl.BlockSpec((1, LANES), full),
                  tspec, tspec, tspec],
        out_specs=[row, row, row, fspec, fspec],
        out_shape=[jax.ShapeDtypeStruct((t, hd), BF16)] * 3 + [fshape, fshape],
        compiler_params=_params("arbitrary"),
        name="qkv",
    )(x, mod, p['norm1'].reshape(1, d), w, gmat, qn, kn, cos_t, sin_a, sin_b)


def _route(logits):
    lane = lax.broadcasted_iota(jnp.int32, logits.shape, 1)
    neg = jnp.float32(-jnp.inf)
    lg = jnp.where(lane < N_EXPERTS, logits, neg)
    m1 = jnp.max(lg, axis=-1, keepdims=True)
    i1 = jnp.min(jnp.where(lg == m1, lane, LANES), axis=-1, keepdims=True)
    lg2 = jnp.where(lane == i1, neg, lg)
    m2 = jnp.max(lg2, axis=-1, keepdims=True)
    i2 = jnp.min(jnp.where(lg2 == m2, lane, LANES), axis=-1, keepdims=True)
    e = jnp.exp(m2 - m1)
    w1 = 1.0 / (1.0 + e)
    w2 = e / (1.0 + e)
    return jnp.where(lane == i1, w1, 0.0) + jnp.where(lane == i2, w2, 0.0)


MOE_BLOCK = 1024
MOE_SUB = 256
MOE_ROUTE_TM = 512
MOE_WINDOW = 5
MOE_TF = 512
MOE_CMB = 256
MOE_CMB_BLOCKS = MOE_SUB // MOE_CMB + 1


def _moe_route_body(x_ref, o_ref, wo_ref, mod_ref, g_ref, wr_ref, xo_ref, h_ref, gates_ref, rank_ref, rank_t_ref,
                    carry_row, carry_col, *, tm):
    i = pl.program_id(0)

    @pl.when(i == 0)
    def _():
        carry_row[...] = jnp.zeros_like(carry_row)
        carry_col[...] = jnp.zeros_like(carry_col)

    x = x_ref[...] + mod_ref[2:3, :] * _dot(o_ref[...], wo_ref[...])
    xo_ref[...] = x
    h = _norm_mod(x, g_ref[...], mod_ref[3:4, :], mod_ref[4:5, :])
    h_ref[...] = h.astype(BF16)
    gates = _route(_dot_f32(h, wr_ref[...]))
    gates_ref[...] = gates
    sel = jnp.where(gates != 0.0, 1.0, 0.0)
    sel_t = sel.T
    r = lax.broadcasted_iota(jnp.int32, (tm, tm), 0)
    c = lax.broadcasted_iota(jnp.int32, (tm, tm), 1)
    lower = jnp.where(c < r, 1.0, 0.0).astype(BF16)
    upper = jnp.where(r < c, 1.0, 0.0).astype(BF16)
    before = _dot(lower, sel.astype(BF16)) + carry_row[...]
    before_t = _dot(sel_t.astype(BF16), upper) + carry_col[...]
    rank_ref[...] = jnp.where(sel > 0.0, before, -1.0)
    rank_t = jnp.where(sel_t > 0.0, before_t, -1.0)
    for s in range(tm // MOE_SUB):
        rank_t_ref[s] = rank_t[:SUBLANES, s * MOE_SUB:(s + 1) * MOE_SUB]
    carry_row[...] += jnp.sum(sel, axis=0, keepdims=True)
    carry_col[...] += jnp.sum(sel_t, axis=1, keepdims=True)


def _moe_route(x, o, w_out, mod, g, w_router, *, seg_len):
    t, d = x.shape
    tm = MOE_ROUTE_TM
    ne = w_router.shape[1]
    assert ne <= SUBLANES
    wr = jnp.zeros((d, LANES), F32).at[:, :ne].set(w_router)
    sub = tm // MOE_SUB
    row = lambda cols: pl.BlockSpec((tm, cols), lambda i: (i, 0))
    full = lambda a: pl.BlockSpec(a.shape, lambda i: (0, 0))
    return pl.pallas_call(
        functools.partial(_moe_route_body, tm=tm),
        grid=(t // tm,),
        in_specs=[row(d), row(o.shape[1]), full(w_out),
                  pl.BlockSpec((None, 6, d), lambda i: (i * tm // seg_len, 0, 0)),
                  pl.BlockSpec((1, d), lambda i: (0, 0)),
                  full(wr)],
        out_specs=[row(d), row(d), row(LANES), row(LANES),
                   pl.BlockSpec((sub, SUBLANES, MOE_SUB), lambda i: (i, 0, 0))],
        out_shape=[jax.ShapeDtypeStruct((t, d), F32),
                   jax.ShapeDtypeStruct((t, d), BF16),
                   jax.ShapeDtypeStruct((t, LANES), F32),
                   jax.ShapeDtypeStruct((t, LANES), F32),
                   jax.ShapeDtypeStruct((t // MOE_SUB, SUBLANES, MOE_SUB), F32)],
        scratch_shapes=[pltpu.VMEM((1, LANES), F32), pltpu.VMEM((LANES, 1), F32)],
        compiler_params=_params("arbitrary"),
        name="moe_route",
    )(x, o, w_out, mod, g.reshape(1, d), wr)


def _moe_plan(rank, ne, *, n_blocks):
    t = rank.shape[0]
    n_tiles = t // MOE_SUB
    per_blk = MOE_BLOCK // MOE_SUB
    n_sub = n_blocks * per_blk
    sel = (rank[:, :ne] >= 0.0).astype(jnp.int32)
    tile_cnt = sel.reshape(n_tiles, MOE_SUB, ne).sum(axis=1)
    tile_end = jnp.cumsum(tile_cnt, axis=0)
    tile_start = tile_end - tile_cnt
    cnt = tile_end[-1]
    nblk = (cnt + MOE_BLOCK - 1) // MOE_BLOCK
    bend = jnp.cumsum(nblk)
    bstart = bend - nblk
    e_last = jnp.max(jnp.where(cnt > 0, jnp.arange(ne), 0))
    b = jnp.arange(n_blocks)
    blk_valid = b < bend[-1]
    blk_e = jnp.minimum(jnp.sum(bend[None, :] <= b[:, None], axis=1), e_last).astype(jnp.int32)
    blk_r0 = (b - bstart[blk_e]) * MOE_BLOCK
    blk_rows = jnp.where(blk_valid, jnp.clip(cnt[blk_e] - blk_r0, 0, MOE_BLOCK), 0).astype(jnp.int32)
    j = jnp.arange(n_sub)
    sub_e = blk_e[j // per_blk]
    sub_r0 = blk_r0[j // per_blk] + (j % per_blk) * MOE_SUB
    sub_valid = jnp.logical_and(blk_valid[j // per_blk], sub_r0 < cnt[sub_e])
    ends = tile_end[:, sub_e]
    r1 = jnp.minimum(sub_r0 + MOE_SUB, cnt[sub_e])
    c_lo = jnp.sum(ends <= sub_r0[None, :], axis=0)
    c_hi = jnp.sum(ends < r1[None, :], axis=0)
    c_lo = jnp.where(sub_valid, c_lo, 1).astype(jnp.int32)
    c_hi = jnp.where(sub_valid, jnp.minimum(c_hi, n_tiles - 1), 0).astype(jnp.int32)
    base = (bstart * MOE_BLOCK).astype(jnp.int32)
    n_cmb = n_sub * (MOE_SUB // MOE_CMB)
    j0 = jnp.minimum((base[None, :] + tile_start) // MOE_CMB, n_cmb - MOE_CMB_BLOCKS).astype(jnp.int32)
    return dict(blk_e=blk_e, blk_valid=blk_valid.astype(jnp.int32), blk_rows=blk_rows,
                sub_e=sub_e.astype(jnp.int32), sub_r0=sub_r0.astype(jnp.int32), c_lo=c_lo, c_hi=c_hi,
                base=base, j0=j0.reshape(-1))


def _moe_dispatch_body(e_ref, r0_ref, lo_ref, hi_ref, h_ref, rank_t_ref, xs_ref, acc_ref):
    j = pl.program_id(0)
    e = e_ref[j]
    rows = (r0_ref[j] + lax.broadcasted_iota(jnp.int32, (MOE_SUB, 1), 0)).astype(F32)
    sub = lax.broadcasted_iota(jnp.int32, (SUBLANES, MOE_SUB), 0)
    n_tiles = rank_t_ref.shape[0]
    lo, hi = lo_ref[j], hi_ref[j]
    acc_ref[...] = jnp.zeros_like(acc_ref)

    def step(w, carry):
        first = lo + w * MOE_WINDOW
        c0 = jnp.minimum(first, n_tiles - MOE_WINDOW)
        pieces = []
        for i in range(MOE_WINDOW):
            c = c0 + i
            rk = jnp.sum(jnp.where(sub == e, rank_t_ref[c], 0.0), axis=0, keepdims=True)
            rk = jnp.where(c >= first, rk, -1.0)
            pieces.append(jnp.where(rk == rows, 1.0, 0.0).astype(BF16))
        onehot = jnp.concatenate(pieces, axis=1)
        off = pl.multiple_of(c0 * MOE_SUB, MOE_SUB)
        acc_ref[...] += _dot(onehot, h_ref[pl.ds(off, MOE_WINDOW * MOE_SUB), :])
        return carry

    lax.fori_loop(0, (hi - lo + MOE_WINDOW) // MOE_WINDOW, step, 0)
    xs_ref[...] = acc_ref[...].astype(BF16)


def _moe_dispatch(h, rank_t, plan, *, n_sub):
    t, d = h.shape
    grid_spec = pltpu.PrefetchScalarGridSpec(
        num_scalar_prefetch=4,
        grid=(n_sub,),
        in_specs=[pl.BlockSpec((t, d), lambda j, *_: (0, 0), pipeline_mode=pl.Buffered(1)),
                  pl.BlockSpec(rank_t.shape, lambda j, *_: (0, 0, 0), pipeline_mode=pl.Buffered(1))],
        out_specs=pl.BlockSpec((MOE_SUB, d), lambda j, *_: (j, 0)),
        scratch_shapes=[pltpu.VMEM((MOE_SUB, d), F32)],
    )
    return pl.pallas_call(
        _moe_dispatch_body,
        grid_spec=grid_spec,
        out_shape=jax.ShapeDtypeStruct((n_sub * MOE_SUB, d), BF16),
        compiler_params=_params("arbitrary"),
        name="moe_dispatch",
    )(plan['sub_e'], plan['sub_r0'], plan['c_lo'], plan['c_hi'], h, rank_t)


def _moe_ffn_body(e_ref, valid_ref, rows_ref, xs_ref, wg_ref, wu_ref, wd_ref, y_ref, acc_ref):
    b = pl.program_id(0)
    f = pl.program_id(1)
    n_rows = rows_ref[b]
    last = f == pl.num_programs(1) - 1
    wg = wg_ref[...].astype(BF16)
    wu = wu_ref[...].astype(BF16)
    wd = wd_ref[...].astype(BF16)
    full = n_rows == MOE_BLOCK

    def swiglu(h):
        a = _silu(_dot(h, wg)) * _dot(h, wu)
        return _dot(a.astype(BF16), wd)

    @pl.when(jnp.logical_and(full, f == 0))
    def _():
        acc_ref[...] = swiglu(xs_ref[...])

    @pl.when(jnp.logical_and(full, f > 0))
    def _():
        acc_ref[...] += swiglu(xs_ref[...])

    @pl.when(jnp.logical_and(full, last))
    def _():
        y_ref[...] = acc_ref[...].astype(BF16)

    for s in range(MOE_BLOCK // MOE_SUB):
        sl = slice(s * MOE_SUB, (s + 1) * MOE_SUB)
        live = jnp.logical_and(jnp.logical_not(full), s * MOE_SUB < n_rows)
        dead = jnp.logical_and(jnp.logical_not(full), s * MOE_SUB >= n_rows)

        @pl.when(jnp.logical_and(live, f == 0))
        def _():
            acc_ref[sl, :] = jnp.zeros((MOE_SUB, acc_ref.shape[1]), F32)

        @pl.when(live)
        def _():
            acc_ref[sl, :] += swiglu(xs_ref[sl, :])

        @pl.when(jnp.logical_and(live, last))
        def _():
            y_ref[sl, :] = acc_ref[sl, :].astype(BF16)

        @pl.when(jnp.logical_and(dead, last))
        def _():
            y_ref[sl, :] = jnp.zeros((MOE_SUB, y_ref.shape[1]), BF16)


def _moe_ffn(xs, wg, wu, wd, plan, *, n_blocks, tf):
    _, d = xs.shape
    ne, _, ff = wg.shape
    nf = ff // tf

    def w_in(b, f, e_ref, valid_ref, rows_ref):
        return (e_ref[b], 0, jnp.where(valid_ref[b] > 0, f, nf - 1))

    def w_down(b, f, e_ref, valid_ref, rows_ref):
        return (e_ref[b], jnp.where(valid_ref[b] > 0, f, nf - 1), 0)

    grid_spec = pltpu.PrefetchScalarGridSpec(
        num_scalar_prefetch=3,
        grid=(n_blocks, nf),
        in_specs=[pl.BlockSpec((MOE_BLOCK, d), lambda b, f, *_: (b, 0)),
                  pl.BlockSpec((None, d, tf), w_in),
                  pl.BlockSpec((None, d, tf), w_in),
                  pl.BlockSpec((None, tf, d), w_down)],
        out_specs=pl.BlockSpec((MOE_BLOCK, d), lambda b, f, *_: (b, 0)),
        scratch_shapes=[pltpu.VMEM((MOE_BLOCK, d), F32)],
    )
    return pl.pallas_call(
        _moe_ffn_body,
        grid_spec=grid_spec,
        out_shape=jax.ShapeDtypeStruct((n_blocks * MOE_BLOCK, d), BF16),
        compiler_params=_params("arbitrary", "arbitrary"),
        name="moe_ffn",
    )(plan['blk_e'], plan['blk_valid'], plan['blk_rows'], xs, wg, wu, wd)


def _moe_combine_body(j0_ref, base_ref, x_ref, mod_ref, gates_ref, rank_ref, *rest, ne, split_tiles):
    nblk = MOE_CMB_BLOCKS
    y_refs, o_ref, o2_ref = rest[:nblk * ne], rest[nblk * ne], rest[nblk * ne + 1]
    c = pl.program_id(0)
    gates = gates_ref[...]
    rank = rank_ref[...]
    lane = lax.broadcasted_iota(jnp.int32, gates.shape, 1)
    col = lax.broadcasted_iota(jnp.int32, (1, MOE_CMB), 1).astype(F32)
    acc = None
    for e in range(ne):
        pick = lane == e
        g = jnp.sum(jnp.where(pick, gates, 0.0), axis=-1, keepdims=True)
        rk = jnp.sum(jnp.where(pick, rank, 0.0), axis=-1, keepdims=True)
        shift = (base_ref[e] - j0_ref[c * ne + e] * MOE_CMB).astype(F32)
        loc = jnp.where(rk >= 0.0, rk + shift, -1.0)
        picked = None
        for b in range(nblk):
            onehot = jnp.where(loc == col + float(b * MOE_CMB), 1.0, 0.0).astype(BF16)
            d = _dot(onehot, y_refs[nblk * e + b][...])
            picked = d if picked is None else picked + d
        contrib = g * picked
        acc = contrib if acc is None else acc + contrib
    out = x_ref[...] + mod_ref[5:6, :] * acc

    @pl.when(c < split_tiles)
    def _():
        o_ref[...] = out

    @pl.when(c >= split_tiles)
    def _():
        o2_ref[...] = out


def _moe_combine(x, mod, gates, rank, y, plan, *, ne, seg_len, t_split):
    t, d = x.shape
    tm = MOE_SUB
    split_tiles = t_split // tm
    y_specs = []
    for e in range(ne):
        for b in range(MOE_CMB_BLOCKS):
            y_specs.append(pl.BlockSpec((MOE_CMB, d), lambda c, j0, base, e=e, b=b: (j0[c * ne + e] + b, 0)))
    grid_spec = pltpu.PrefetchScalarGridSpec(
        num_scalar_prefetch=2,
        grid=(t // tm,),
        in_specs=[pl.BlockSpec((tm, d), lambda c, *_: (c, 0)),
                  pl.BlockSpec((None, 6, d), lambda c, *_: (c * tm // seg_len, 0, 0)),
                  pl.BlockSpec((tm, LANES), lambda c, *_: (c, 0)),
                  pl.BlockSpec((tm, LANES), lambda c, *_: (c, 0))] + y_specs,
        out_specs=[pl.BlockSpec((tm, d), lambda c, *_: (jnp.minimum(c, split_tiles - 1), 0)),
                   pl.BlockSpec((tm, d), lambda c, *_: (jnp.maximum(c - split_tiles, 0), 0))],
    )
    return pl.pallas_call(
        functools.partial(_moe_combine_body, ne=ne, split_tiles=split_tiles),
        grid_spec=grid_spec,
        out_shape=[jax.ShapeDtypeStruct((t_split, d), F32), jax.ShapeDtypeStruct((t - t_split, d), F32)],
        compiler_params=_params("arbitrary"),
        name="moe_combine",
    )(plan['j0'], plan['base'], x, mod, gates, rank, *([y] * (MOE_CMB_BLOCKS * ne)))


def _attn_out_moe(x, o, w_out, mod, g, w_router, wg, wu, wd, *, seg_len, t_split, top_k=2):
    t, d = x.shape
    ne = w_router.shape[1]
    n_blocks = t * top_k // MOE_BLOCK + ne
    x, h, gates, rank, rank_t = _moe_route(x, o, w_out, mod, g, w_router, seg_len=seg_len)
    plan = _moe_plan(rank, ne, n_blocks=n_blocks)
    xs = _moe_dispatch(h, rank_t, plan, n_sub=n_blocks * (MOE_BLOCK // MOE_SUB))
    y = _moe_ffn(xs, wg, wu, wd, plan, n_blocks=n_blocks, tf=MOE_TF)
    return _moe_combine(x, mod, gates, rank, y, plan, ne=ne, seg_len=seg_len, t_split=t_split)


def _even_layer(x_lat, x_ctx, cond8, p, cache_ckv, cache_kr, *, nb, lat_seq, n_ctx, ctx_seq):
    d = x_lat.shape[1]
    t_lat = nb * lat_seq
    t = t_lat + x_ctx.shape[0]
    seg_len = lat_seq
    past = cache_ckv.shape[1]
    mod = _adaln(cond8, p['w_mod'], p['b_mod'])

    tm = 512
    tables = _rope_tables(lat_seq, MLA_ROPE, (MLA_NOPE,), tm)
    z_hy, kr, q, ckvn = _even_in(x_lat, x_ctx, mod, p, tables, tm=tm, seg_len=seg_len, t_lat=t_lat,
                                 lat_seq=lat_seq)

    u, x0 = _hy_pre(z_hy, p['hy_conv_w'], p['hy_conv_b'], t_lat=t_lat, lat_seq=lat_seq, ctx_seq=ctx_seq)
    k_lat, k_ctx, k_sum = _hyena_filters(p, lat_seq=lat_seq, ctx_seq=ctx_seq)
    dbias = p['hy_dbias'].reshape(1, HY_WIDTH)
    conv_lat = _hyena_lat(k_lat, k_sum[0:1], u, seq=lat_seq, nb=nb)
    conv_ctx = _hyena_ctx(k_ctx, k_sum[1:2], u, seq=ctx_seq, nseq=n_ctx, u_row0=t_lat)

    cache_kr_p = jnp.zeros((nb, past, LANES), F32).at[:, :, :MLA_ROPE].set(cache_kr)
    ckvn_rows = jnp.concatenate(
        [jnp.concatenate([cache_ckv, ckvn[:t_lat].reshape(nb, lat_seq, -1)], axis=1).reshape(nb * (past + lat_seq), -1),
         ckvn[t_lat:]], axis=0)
    kr_rows = jnp.concatenate(
        [jnp.concatenate([cache_kr_p, kr[:t_lat].reshape(nb, lat_seq, LANES)], axis=1).reshape(nb * (past + lat_seq), LANES),
         kr[t_lat:]], axis=0)
    k_all, v_all = _mla_kv(ckvn_rows, kr_rows, p, tables, tm=tm, nb=nb, past=past, lat_seq=lat_seq)
    o = jnp.zeros((t, MLA_HEADS * MLA_V), BF16)
    o = _mla_attn(o, q, k_all, v_all, tq=ATT_TQ, pairs=1, n_seq=nb, seq_q=lat_seq, seq_k=past + lat_seq,
                  q_row0=0, k_row0=0)
    o = _mla_attn(o, q, k_all, v_all, tq=ctx_seq, pairs=MLA_HEADS // 2, n_seq=n_ctx, seq_q=ctx_seq,
                  seq_k=ctx_seq, q_row0=t_lat, k_row0=nb * (past + lat_seq))

    w_out = p['w_out'].astype(BF16)
    x = _even_out(x_lat, x_ctx, mod, conv_lat, conv_ctx, u, x0, dbias, o, w_out[:HY_WIDTH], w_out[HY_WIDTH:],
                  tm=512, seg_len=seg_len)
    x = _ffn(x, mod, p['norm2'], p['ffn_w_gate'].astype(BF16), p['ffn_w_up'].astype(BF16),
             p['ffn_w_down'].astype(BF16), tm=512, tf=1408, seg_len=seg_len)
    new_ckv = ckvn[t_lat:].reshape(n_ctx, ctx_seq, -1)
    new_kr = kr[t_lat:, :MLA_ROPE].reshape(n_ctx, ctx_seq, MLA_ROPE)
    return x, new_ckv, new_kr


def _odd_layer(x, cond8, p, cache_k, cache_v, lambda_init, *, nb, lat_seq, n_ctx, ctx_seq):
    t, d = x.shape
    t_lat = nb * lat_seq
    seg_len = lat_seq
    mod = _adaln(cond8, p['w_mod'], p['b_mod'])
    tm = 512
    tables = _rope_tables(lat_seq, DIFF_DH, (0, DIFF_DH), tm)
    q, k, v, new_k, new_v = _qkv(x, mod, p, tables, tm=tm, seg_len=seg_len, t_lat=t_lat, lat_seq=lat_seq,
                                 n_ctx=n_ctx, ctx_seq=ctx_seq)
    lam_p = jnp.stack([p['lam_q1'], p['lam_k1'], p['lam_q2'], p['lam_k2']])
    o = jnp.zeros((t, DIFF_HEADS * 2 * DIFF_DH), BF16)
    o = _diff_attn(o, q, k, v, cache_k.astype(BF16), cache_v.astype(BF16), lam_p, p['subln'],
                   tq=ATT_TQ, heads=1, n_seq=nb, seq_q=lat_seq, q_row0=0, lambda_init=lambda_init)
    o = _diff_attn(o, q, k, v, None, None, lam_p, p['subln'],
                   tq=ctx_seq, heads=DIFF_HEADS, n_seq=n_ctx, seq_q=ctx_seq, q_row0=t_lat,
                   lambda_init=lambda_init)
    x_lat, x_ctx = _attn_out_moe(x, o, p['w_out'].astype(BF16), mod, p['norm2'], p['w_router'],
                                 p['moe_w_gate'], p['moe_w_up'], p['moe_w_down'], seg_len=seg_len, t_split=t_lat)
    return x_lat, x_ctx, new_k, new_v


def kernel(x_prompt, x_sample, cache_l0_ckv, cache_l0_krope, cache_l1_k, cache_l1_v, c, c_ctx,
           l0_w_mod, l0_b_mod, l0_norm1, l0_norm2, l0_w_in, l0_hy_conv_w, l0_hy_conv_b,
           l0_hy_fw1, l0_hy_fb1, l0_hy_freq1, l0_hy_fw2, l0_hy_fb2, l0_hy_freq2, l0_hy_fw3, l0_hy_dbias,
           l0_mla_qa_norm, l0_mla_w_uq, l0_mla_kva_norm, l0_mla_w_ukv, l0_mla_q_norm, l0_mla_k_norm,
           l0_w_out, l0_ffn_w_gate, l0_ffn_w_up, l0_ffn_w_down,
           l1_w_mod, l1_b_mod, l1_norm1, l1_norm2, l1_w_qkv, l1_q_norm, l1_k_norm,
           l1_lam_q1, l1_lam_k1, l1_lam_q2, l1_lam_k2, l1_subln, l1_w_out,
           l1_w_router, l1_moe_w_gate, l1_moe_w_up, l1_moe_w_down):
    even = {
        'w_mod': l0_w_mod, 'b_mod': l0_b_mod, 'norm1': l0_norm1, 'norm2': l0_norm2, 'w_in': l0_w_in,
        'hy_conv_w': l0_hy_conv_w, 'hy_conv_b': l0_hy_conv_b, 'hy_fw1': l0_hy_fw1, 'hy_fb1': l0_hy_fb1,
        'hy_freq1': l0_hy_freq1, 'hy_fw2': l0_hy_fw2, 'hy_fb2': l0_hy_fb2, 'hy_freq2': l0_hy_freq2,
        'hy_fw3': l0_hy_fw3, 'hy_dbias': l0_hy_dbias, 'qa_norm': l0_mla_qa_norm, 'w_uq': l0_mla_w_uq,
        'kva_norm': l0_mla_kva_norm, 'w_ukv': l0_mla_w_ukv, 'q_norm': l0_mla_q_norm, 'k_norm': l0_mla_k_norm,
        'w_out': l0_w_out, 'ffn_w_gate': l0_ffn_w_gate, 'ffn_w_up': l0_ffn_w_up, 'ffn_w_down': l0_ffn_w_down,
    }
    odd = {
        'w_mod': l1_w_mod, 'b_mod': l1_b_mod, 'norm1': l1_norm1, 'norm2': l1_norm2, 'w_qkv': l1_w_qkv,
        'q_norm': l1_q_norm, 'k_norm': l1_k_norm, 'lam_q1': l1_lam_q1, 'lam_k1': l1_lam_k1,
        'lam_q2': l1_lam_q2, 'lam_k2': l1_lam_k2, 'subln': l1_subln, 'w_out': l1_w_out,
        'w_router': l1_w_router, 'moe_w_gate': l1_moe_w_gate, 'moe_w_up': l1_moe_w_up,
        'moe_w_down': l1_moe_w_down,
    }
    n_ctx, ctx_seq, d = x_prompt.shape
    nb, lat_seq, _ = x_sample.shape
    assert n_ctx * ctx_seq == lat_seq, "segment layout needs equally sized modulation segments"
    dims = dict(nb=nb, lat_seq=lat_seq, n_ctx=n_ctx, ctx_seq=ctx_seq)
    t_lat = nb * lat_seq
    cond8 = jnp.zeros((SUBLANES, d), F32).at[:nb].set(c).at[nb].set(c_ctx)

    x, new_l0_ckv, new_l0_krope = _even_layer(x_sample.reshape(t_lat, d), x_prompt.reshape(n_ctx * ctx_seq, d),
                                              cond8, even, cache_l0_ckv, cache_l0_krope, **dims)
    lambda_init = 0.8 - 0.6 * math.exp(-0.3 * 1)
    x_lat, x_ctx, new_l1_k, new_l1_v = _odd_layer(x, cond8, odd, cache_l1_k, cache_l1_v, lambda_init, **dims)

    y_sample = x_lat.reshape(nb, lat_seq, d)
    y_prompt = x_ctx.reshape(n_ctx, ctx_seq, d)
    return (y_prompt, y_sample, new_l0_ckv, new_l0_krope, new_l1_k, new_l1_v)
```

```python
import functools
import math

import numpy as np
import jax
import jax.numpy as jnp
from jax import lax
from jax.experimental import pallas as pl
from jax.experimental.pallas import tpu as pltpu

F32 = jnp.float32
BF16 = jnp.bfloat16

VMEM_LIMIT_BYTES = 56 * 1024 * 1024
LANES = 128
SUBLANES = 8
LOG2_E = math.log2(math.e)

GRID_W = 64
ROPE_BASE = 10000.0
NORM_EPS = 1e-6
HY_WIDTH = 512
HY_BANDS = 16
HY_FAST_DECAY_PCT = 0.3
HY_SLOW_DECAY_PCT = 1.5
HY_DECAY_TARGET = 1e-2
MLA_HEADS = 8
MLA_NOPE = 64
MLA_ROPE = 32
MLA_QK = MLA_NOPE + MLA_ROPE
MLA_V = 64
MLA_Q_RANK = 768
MLA_KV_RANK = 256
DIFF_HEADS = 8
DIFF_DH = 64
N_EXPERTS = 8


def _params(*sem):
    return pltpu.CompilerParams(dimension_semantics=sem, vmem_limit_bytes=VMEM_LIMIT_BYTES)


def _dot(a, b):
    return jnp.dot(a, b, preferred_element_type=F32)


def _dot_nt(a, b):
    return lax.dot_general(a, b, (((1,), (1,)), ((), ())), preferred_element_type=F32)


def _split_bf16(a):
    hi = a.astype(BF16)
    lo = (a - hi.astype(F32)).astype(BF16)
    return hi, lo


def _dot_f32(a, b):
    ah, al = _split_bf16(a)
    bh, bl = _split_bf16(b)
    return _dot(ah, bh) + (_dot(al, bh) + _dot(ah, bl))


def _rms(x, g, n=None):
    n = x.shape[-1] if n is None else n
    ms = jnp.sum(x * x, axis=-1, keepdims=True) * (1.0 / n)
    return x * lax.rsqrt(ms + NORM_EPS) * g


def _norm_mod(x, g, shift, scale):
    return _rms(x, g) * (1.0 + scale) + shift


def _silu(x):
    return x / (1.0 + jnp.exp(-x))


def _adaln_body(c_ref, w_ref, b_ref, o_ref):
    o_ref[...] = _dot_f32(_silu(c_ref[...]), w_ref[...]) + b_ref[...]


def _adaln(cond8, w_mod, b_mod):
    d, n = w_mod.shape
    tn = n // 4
    out = pl.pallas_call(
        _adaln_body,
        grid=(n // tn,),
        in_specs=[pl.BlockSpec((SUBLANES, d), lambda j: (0, 0)),
                  pl.BlockSpec((d, tn), lambda j: (0, j)),
                  pl.BlockSpec((1, tn), lambda j: (0, j))],
        out_specs=pl.BlockSpec((SUBLANES, tn), lambda j: (0, j)),
        out_shape=jax.ShapeDtypeStruct((SUBLANES, n), F32),
        compiler_params=_params("arbitrary"),
        name="adaln",
    )(cond8, w_mod, b_mod.reshape(1, n))
    return out.reshape(SUBLANES, 6, d)


def _two_part_specs(a, b, tm):
    na = a.shape[0] // tm
    cols = a.shape[1]
    return (pl.BlockSpec((tm, cols), lambda i: (jnp.minimum(i, na - 1), 0)),
            pl.BlockSpec((tm, cols), lambda i: (jnp.maximum(i - na, 0), 0)), na)


def _hy_pre_body(z_ref, zp_ref, zn_ref, w_ref, b_ref, u_ref, x0_ref, *, tm, lat_tiles, tiles_per_seq):
    i = pl.program_id(0)
    z = z_ref[...]
    in_lat = i < lat_tiles
    has_prev = jnp.logical_and(in_lat, i % tiles_per_seq != 0)
    has_next = jnp.logical_and(in_lat, i % tiles_per_seq != tiles_per_seq - 1)
    prev_row = jnp.where(has_prev, zp_ref[SUBLANES - 1:SUBLANES, :], 0.0)
    next_row = jnp.where(has_next, zn_ref[0:1, :], 0.0)
    rows = lax.broadcasted_iota(jnp.int32, z.shape, 0)
    z_m = jnp.where(rows == 0, prev_row, pltpu.roll(z, 1, 0))
    z_p = jnp.where(rows == tm - 1, next_row, pltpu.roll(z, tm - 1, 0))
    zc = b_ref[...] + z_m * w_ref[0:1, :] + z * w_ref[1:2, :] + z_p * w_ref[2:3, :]
    c = HY_WIDTH
    x0_ref[...] = zc[:, :c]
    u_ref[...] = zc[:, 2 * c:] * zc[:, c:2 * c]


def _hy_pre(z, conv_w, conv_b, *, t_lat, lat_seq, ctx_seq):
    t = z.shape[0]
    tm = ctx_seq
    c3 = 3 * HY_WIDTH
    nb8 = t // SUBLANES
    body = functools.partial(_hy_pre_body, tm=tm, lat_tiles=t_lat // tm, tiles_per_seq=lat_seq // tm)
    return pl.pallas_call(
        body,
        grid=(t // tm,),
        in_specs=[pl.BlockSpec((tm, c3), lambda i: (i, 0)),
                  pl.BlockSpec((SUBLANES, c3), lambda i: (jnp.maximum(i * (tm // SUBLANES) - 1, 0), 0)),
                  pl.BlockSpec((SUBLANES, c3), lambda i: (jnp.minimum((i + 1) * (tm // SUBLANES), nb8 - 1), 0)),
                  pl.BlockSpec((3, c3), lambda i: (0, 0)),
                  pl.BlockSpec((1, c3), lambda i: (0, 0))],
        out_specs=[pl.BlockSpec((tm, HY_WIDTH), lambda i: (i, 0)),
                   pl.BlockSpec((tm, HY_WIDTH), lambda i: (i, 0))],
        out_shape=[jax.ShapeDtypeStruct((t, HY_WIDTH), F32),
                   jax.ShapeDtypeStruct((t, HY_WIDTH), F32)],
        compiler_params=_params("parallel"),
        name="hy_pre",
    )(z, z, z, conv_w, conv_b.reshape(1, c3))


def _filter_embedding(seq):
    t01 = np.linspace(0.0, 1.0, seq)[:, None]
    w = 2.0 * math.pi * np.arange(seq)[:, None] / seq
    f = np.linspace(1e-4, HY_BANDS - 1, HY_BANDS)[None, :]
    z = np.concatenate([t01, np.cos(f * w), -np.sin(f * w)], axis=-1)
    z_rev = np.concatenate([z[:1], z[:0:-1]], axis=0)
    zz = np.concatenate([z, z_rev], axis=0)
    out = np.zeros((2 * seq, LANES), np.float32)
    out[:, :zz.shape[1]] = zz
    return out


def _filter_body(zz_ref, dl_ref, w1_ref, b1_ref, f1_ref, w2_ref, b2_ref, f2_ref, w3_ref,
                 kl_ref, kc_ref, s_ref, *, tm, lat_tiles, ctx_tiles):
    i = pl.program_id(0)
    zz = zz_ref[...]
    h = jnp.sin(f1_ref[...] * (_dot_f32(zz, w1_ref[...]) + b1_ref[...]))
    h = jnp.sin(f2_ref[...] * (_dot_f32(h, w2_ref[...]) + b2_ref[...]))
    h = _dot_f32(h, w3_ref[...])
    is_bwd = jnp.logical_or(jnp.logical_and(i >= lat_tiles // 2, i < lat_tiles),
                            i >= lat_tiles + ctx_tiles // 2)
    first_bwd = jnp.logical_or(i == lat_tiles // 2, i == lat_tiles + ctx_tiles // 2)
    window = jnp.exp(-zz[:, 0:1] * dl_ref[...])
    k = jnp.where(is_bwd, h[:, HY_WIDTH:], h[:, :HY_WIDTH]) * window
    rows = lax.broadcasted_iota(jnp.int32, k.shape, 0)
    k = jnp.where(jnp.logical_and(first_bwd, rows == 0), 0.0, k)
    s = jnp.sum(jnp.abs(k), axis=0, keepdims=True)

    @pl.when(i == 0)
    def _():
        s_ref[...] = jnp.zeros_like(s_ref)

    @pl.when(i < lat_tiles)
    def _():
        kl_ref[...] = k
        s_ref[0:1, :] += s

    @pl.when(i >= lat_tiles)
    def _():
        kc_ref[...] = k
        s_ref[1:2, :] += s


def _hyena_filters(p, *, lat_seq, ctx_seq):
    tm = ctx_seq
    zz = jnp.asarray(np.concatenate([_filter_embedding(lat_seq), _filter_embedding(ctx_seq)], axis=0))
    rows = zz.shape[0]
    max_decay = math.log(HY_DECAY_TARGET) / HY_FAST_DECAY_PCT
    min_decay = math.log(HY_DECAY_TARGET) / HY_SLOW_DECAY_PCT
    deltas = jnp.asarray(np.abs(np.linspace(min_decay, max_decay, HY_WIDTH))[None, :].astype(np.float32))
    emb, hid = p['hy_fw1'].shape

    def pad2(a, r, c):
        return jnp.zeros((r, c), F32).at[:a.shape[0], :a.shape[1]].set(a)

    w1 = pad2(p['hy_fw1'], LANES, LANES)
    b1 = pad2(p['hy_fb1'][None, :], 1, LANES)
    f1 = pad2(p['hy_freq1'][None, :], 1, LANES)
    w2 = pad2(p['hy_fw2'], LANES, LANES)
    b2 = pad2(p['hy_fb2'][None, :], 1, LANES)
    f2 = pad2(p['hy_freq2'][None, :], 1, LANES)
    w3 = pad2(p['hy_fw3'], LANES, 2 * HY_WIDTH)
    lat_tiles = 2 * lat_seq // tm
    body = functools.partial(_filter_body, tm=tm, lat_tiles=lat_tiles, ctx_tiles=2 * ctx_seq // tm)
    full = lambda i: (0, 0)
    return pl.pallas_call(
        body,
        grid=(rows // tm,),
        in_specs=[pl.BlockSpec((tm, LANES), lambda i: (i, 0)),
                  pl.BlockSpec((1, HY_WIDTH), full),
                  pl.BlockSpec((LANES, LANES), full), pl.BlockSpec((1, LANES), full), pl.BlockSpec((1, LANES), full),
                  pl.BlockSpec((LANES, LANES), full), pl.BlockSpec((1, LANES), full), pl.BlockSpec((1, LANES), full),
                  pl.BlockSpec((LANES, 2 * HY_WIDTH), full)],
        out_specs=[pl.BlockSpec((tm, HY_WIDTH), lambda i: (jnp.minimum(i, lat_tiles - 1), 0)),
                   pl.BlockSpec((tm, HY_WIDTH), lambda i: (jnp.maximum(i - lat_tiles, 0), 0)),
                   pl.BlockSpec((SUBLANES, HY_WIDTH), full)],
        out_shape=[jax.ShapeDtypeStruct((2 * lat_seq, HY_WIDTH), F32),
                   jax.ShapeDtypeStruct((2 * ctx_seq, HY_WIDTH), F32),
                   jax.ShapeDtypeStruct((SUBLANES, HY_WIDTH), F32)],
        compiler_params=_params("arbitrary"),
        name="hy_filter",
    )(zz, deltas, w1, b1, f1, w2, b2, f2, w3)


def _stack_complex(z):
    return np.block([[z.real, -z.imag], [z.imag, z.real]])


def _dft_consts_two_level(seq, n1, n2):
    n = 2 * seq
    assert n1 * n2 == n
    a1 = np.arange(n1)
    f1_full = np.exp(-2j * np.pi * np.outer(a1, a1) / n1)
    f1_u = np.concatenate([f1_full.real, f1_full.imag], axis=0)[:, :n1 // 2]
    f1_k = np.concatenate([f1_full.real, f1_full.imag], axis=0)
    a2 = np.arange(n2)
    f = a1[:, None, None] + n1 * a2[None, :, None]
    z = np.exp(-2j * np.pi * (f * a2[None, None, :]) / n)
    mf = np.stack([_stack_complex(z[i]) for i in range(n1)])
    mi = np.stack([_stack_complex(np.conj(z[i]).T) for i in range(n1)])
    g = np.exp(2j * np.pi * np.outer(a1[:n1 // 2], a1) / n1) / n
    g1 = np.concatenate([_kron_rows(g.real), _kron_rows(-g.imag)], axis=1)
    as32 = lambda a: jnp.asarray(a.astype(np.float32))
    return as32(_kron_rows(f1_u)), as32(_kron_rows(f1_k)), as32(mf), as32(mi), as32(g1)


def _dft_consts_one_level(seq):
    n = 2 * seq
    a = np.arange(n)
    z = np.exp(-2j * np.pi * np.outer(a, a) / n)
    mf = np.concatenate([z.real, z.imag], axis=0)
    zi = np.exp(2j * np.pi * np.outer(a[:seq], a) / n) / n
    mi = np.concatenate([zi.real, -zi.imag], axis=1)
    as32 = lambda a: jnp.asarray(a.astype(np.float32))
    return as32(mf), as32(mi)


HY_ROWS = 16


def _kron_rows(m):
    return np.kron(m, np.eye(HY_ROWS))


def _dft1_body(m_ref, x_ref, sc_ref, or_ref, oi_ref):
    k, r, c = x_ref.shape
    x = (x_ref[...] * (1.0 / sc_ref[...])).reshape(k * r, c).astype(BF16)
    o = _dot(m_ref[...], x)
    h = o.shape[0] // 2
    or_ref[...] = o[:h].reshape(or_ref.shape).astype(BF16)
    oi_ref[...] = o[h:].reshape(oi_ref.shape).astype(BF16)


def _dft1(m, x, scale_row, *, n1, groups):
    _, k, n2, c = x.shape
    g = groups
    ospec = pl.BlockSpec((None, n1, HY_ROWS, c), lambda b, j: (b, 0, j, 0))
    return pl.pallas_call(
        _dft1_body,
        grid=(g, n2 // HY_ROWS),
        in_specs=[pl.BlockSpec(m.shape, lambda b, j: (0, 0)),
                  pl.BlockSpec((None, k, HY_ROWS, c), lambda b, j: (b, 0, j, 0)),
                  pl.BlockSpec((1, c), lambda b, j: (0, 0))],
        out_specs=[ospec, ospec],
        out_shape=[jax.ShapeDtypeStruct((g, n1, n2, c), BF16)] * 2,
        compiler_params=_params("parallel", "parallel"),
        name="hy_dft1",
    )(m, x, scale_row)


SPEC_GROUP = 16


def _spec_fwd_body(mf_ref, ar_ref, ai_ref, kr_ref, ki_ref):
    for g in range(SPEC_GROUP):
        a = jnp.concatenate([ar_ref[g], ai_ref[g]], axis=0)
        x = _dot(mf_ref[g], a)
        h = x.shape[0] // 2
        kr_ref[g] = x[:h]
        ki_ref[g] = x[h:]


def _spec_fwd(mf, ar, ai):
    n1, n2, c = ar.shape
    spec = pl.BlockSpec((SPEC_GROUP, n2, c), lambda i: (i, 0, 0))
    return pl.pallas_call(
        _spec_fwd_body,
        grid=(n1 // SPEC_GROUP,),
        in_specs=[pl.BlockSpec((SPEC_GROUP, 2 * n2, 2 * n2), lambda i: (i, 0, 0)), spec, spec],
        out_specs=[spec, spec],
        out_shape=[jax.ShapeDtypeStruct((n1, n2, c), F32)] * 2,
        compiler_params=_params("parallel"),
        name="hy_spec_filter",
    )(mf, ar, ai)


def _spec_mul_body(mf_ref, mi_ref, kr_ref, ki_ref, ar_ref, ai_ref, br_ref, bi_ref):
    for g in range(SPEC_GROUP):
        a = jnp.concatenate([ar_ref[g], ai_ref[g]], axis=0)
        x = _dot(mf_ref[g], a)
        h = x.shape[0] // 2
        xr, xi = x[:h], x[h:]
        kr, ki = kr_ref[g], ki_ref[g]
        y = jnp.concatenate([xr * kr - xi * ki, xr * ki + xi * kr], axis=0).astype(BF16)
        b = _dot(mi_ref[g], y)
        br_ref[g] = b[:h].astype(BF16)
        bi_ref[g] = b[h:].astype(BF16)


def _spec_mul(mf, mi, kr, ki, ar, ai):
    nb, n1, n2, c = ar.shape
    mspec = pl.BlockSpec((SPEC_GROUP, 2 * n2, 2 * n2), lambda i, b: (i, 0, 0))
    kspec = pl.BlockSpec((SPEC_GROUP, n2, c), lambda i, b: (i, 0, 0))
    aspec = pl.BlockSpec((None, SPEC_GROUP, n2, c), lambda i, b: (b, i, 0, 0))
    return pl.pallas_call(
        _spec_mul_body,
        grid=(n1 // SPEC_GROUP, nb),
        in_specs=[mspec, mspec, kspec, kspec, aspec, aspec],
        out_specs=[aspec, aspec],
        out_shape=[jax.ShapeDtypeStruct((nb, n1, n2, c), BF16)] * 2,
        compiler_params=_params("parallel", "arbitrary"),
        name="hy_spec_mul",
    )(mf, mi, kr, ki, ar, ai)


def _idft1_body(m_ref, br_ref, bi_ref, o_ref):
    n1, r, c = br_ref.shape
    b = jnp.concatenate([br_ref[...].reshape(n1 * r, c), bi_ref[...].reshape(n1 * r, c)], axis=0)
    o_ref[...] = _dot(m_ref[...], b).reshape(o_ref.shape)


def _idft1(m, br, bi, *, n_out):
    nb, n1, n2, c = br.shape
    bspec = pl.BlockSpec((None, n1, HY_ROWS, c), lambda b, j: (b, 0, j, 0))
    return pl.pallas_call(
        _idft1_body,
        grid=(nb, n2 // HY_ROWS),
        in_specs=[pl.BlockSpec(m.shape, lambda b, j: (0, 0)), bspec, bspec],
        out_specs=pl.BlockSpec((None, n_out, HY_ROWS, c), lambda b, j: (b, 0, j, 0)),
        out_shape=jax.ShapeDtypeStruct((nb, n_out, n2, c), F32),
        compiler_params=_params("parallel", "parallel"),
        name="hy_idft1",
    )(m, br, bi)


def _ctx_filter_body(mf_ref, k_ref, sc_ref, kf_ref):
    kf_ref[...] = _dot(mf_ref[...], (k_ref[...] * (1.0 / sc_ref[...])).astype(BF16))


def _ctx_conv_body(mf_ref, mi_ref, kf_ref, u_ref, o_ref):
    x = _dot(mf_ref[...], u_ref[...].astype(BF16))
    h = x.shape[0] // 2
    xr, xi = x[:h], x[h:]
    kr, ki = kf_ref[:h, :], kf_ref[h:, :]
    y = jnp.concatenate([xr * kr - xi * ki, xr * ki + xi * kr], axis=0).astype(BF16)
    o_ref[...] = _dot(mi_ref[...], y)


def _hyena_ctx(k_raw, k_norm1, u, *, seq, nseq, u_row0):
    mf, mi = _dft_consts_one_level(seq)
    n = 2 * seq
    c = u.shape[1]
    kf = pl.pallas_call(
        _ctx_filter_body,
        out_shape=jax.ShapeDtypeStruct((2 * n, c), F32),
        compiler_params=_params(),
        name="hy_ctx_filter",
    )(mf.astype(BF16), k_raw, k_norm1)
    full = lambda s: (0, 0)
    return pl.pallas_call(
        _ctx_conv_body,
        grid=(nseq,),
        in_specs=[pl.BlockSpec((2 * n, seq), full),
                  pl.BlockSpec((seq, 2 * n), full),
                  pl.BlockSpec((2 * n, c), full),
                  pl.BlockSpec((seq, c), lambda s: (u_row0 // seq + s, 0))],
        out_specs=pl.BlockSpec((seq, c), lambda s: (s, 0)),
        out_shape=jax.ShapeDtypeStruct((nseq * seq, c), F32),
        compiler_params=_params("parallel"),
        name="hy_ctx_conv",
    )(mf[:, :seq].astype(BF16), mi.astype(BF16), kf, u)


def _hyena_lat(k_raw, k_norm1, u, *, seq, nb):
    c = u.shape[1]
    n1, n2 = 64, 2 * seq // 64
    f1_u, f1_k, mf, mi, g1 = _dft_consts_two_level(seq, n1, n2)
    mf = mf.astype(BF16)
    mi = mi.astype(BF16)
    ones_row = jnp.ones((1, c), F32)
    akr, aki = _dft1(f1_k.astype(BF16), k_raw.reshape(-1, n1, n2, c), k_norm1, n1=n1, groups=1)
    kr, ki = _spec_fwd(mf, akr[0], aki[0])
    ar, ai = _dft1(f1_u.astype(BF16), u.reshape(-1, n1 // 2, n2, c), ones_row, n1=n1, groups=nb)
    br, bi = _spec_mul(mf, mi, kr, ki, ar, ai)
    y = _idft1(g1.astype(BF16), br, bi, n_out=n1 // 2)
    return y.reshape(nb * seq, c)


def _rope_tables(seq, rope_dims, lane_offsets, pad_rows):
    rows = seq // GRID_W
    rr, cc = np.meshgrid(np.arange(rows), np.arange(GRID_W), indexing='ij')
    pos = (rr.reshape(-1).astype(np.float64), cc.reshape(-1).astype(np.float64))
    half = rope_dims // 2
    q = half // 2
    inv_freq = ROPE_BASE ** (-np.arange(0, half, 2, dtype=np.float64) / half)
    cos_t = np.ones((seq + pad_rows, LANES), np.float64)
    sin_a = np.zeros((seq + pad_rows, LANES), np.float64)
    sin_b = np.zeros((seq + pad_rows, LANES), np.float64)
    for off in lane_offsets:
        for axis in range(2):
            ang = pos[axis][:, None] * inv_freq[None, :]
            base = off + axis * half
            cos_t[:seq, base:base + q] = np.cos(ang)
            cos_t[:seq, base + q:base + half] = np.cos(ang)
            sin_b[:seq, base:base + q] = -np.sin(ang)
            sin_a[:seq, base + q:base + half] = np.sin(ang)
    as32 = lambda a: jnp.asarray(a.astype(np.float32))
    return as32(cos_t), as32(sin_a), as32(sin_b)


def _rope(x, cos_t, sin_a, sin_b, shift):
    return x * cos_t + pltpu.roll(x, shift, 1) * sin_a + pltpu.roll(x, LANES - shift, 1) * sin_b


def _even_in_body(xa_ref, xb_ref, mod_ref, g_ref, w_ref, qa_ref, kva_ref, wuq_ref, qn_ref,
                  cos_ref, sa_ref, sb_ref, zhy_ref, kr_ref, q_ref, ckvn_ref, zq_ref, *, na, scale):
    i = pl.program_id(0)
    n = pl.num_programs(0) - 1

    @pl.when(i == 0)
    def _():
        zq_ref[1] = jnp.zeros(zq_ref.shape[1:], F32)

    c_q = MLA_Q_RANK
    zp = zq_ref[(i + 1) % 2]
    ckvn_ref[...] = _rms(zp[:, c_q:], kva_ref[...])
    cqn = _rms(zp[:, :c_q], qa_ref[...])
    q = _dot(cqn.astype(BF16), wuq_ref[...])
    cos_t, sin_a, sin_b = cos_ref[...], sa_ref[...], sb_ref[...]
    g = qn_ref[...]
    for hh in range(MLA_HEADS):
        sl = slice(hh * LANES, (hh + 1) * LANES)
        qh = _rope(_rms(q[:, sl], g, MLA_QK), cos_t, sin_a, sin_b, MLA_ROPE // 4)
        q_ref[:, sl] = (qh * scale).astype(BF16)

    x = jnp.where(jnp.minimum(i, n - 1) < na, xa_ref[...], xb_ref[...])
    h = _norm_mod(x, g_ref[...], mod_ref[0:1, :], mod_ref[1:2, :])
    z = _dot(h.astype(BF16), w_ref[...])
    c_hy = 3 * HY_WIDTH
    c_kv = c_hy + MLA_Q_RANK + MLA_KV_RANK
    zhy_ref[...] = z[:, :c_hy]
    kr_ref[...] = z[:, c_kv:]
    zq_ref[i % 2] = z[:, c_hy:c_kv]


def _pad_heads(w, heads, width):
    k = w.shape[0]
    w3 = w.reshape(k, heads, width)
    return jnp.zeros((k, heads, LANES), w.dtype).at[:, :, :width].set(w3).reshape(k, heads * LANES)


def _even_in(xa, xb, mod, p, tables, *, tm, seg_len, t_lat, lat_seq):
    d = xa.shape[1]
    t = xa.shape[0] + xb.shape[0]
    n_in = p['w_in'].shape[1]
    c_kv = 3 * HY_WIDTH + MLA_Q_RANK + MLA_KV_RANK
    assert c_kv % LANES == 0 and n_in - c_kv == MLA_ROPE
    n_pad = c_kv + LANES
    w_in = jnp.zeros((d, n_pad), BF16).at[:, :n_in].set(p['w_in'].astype(BF16))
    wuq = _pad_heads(p['w_uq'], MLA_HEADS, MLA_QK).astype(BF16)
    cos_t, sin_a, sin_b = tables
    qn = jnp.zeros((1, LANES), F32).at[0, :MLA_QK].set(p['q_norm'])
    pos_blocks = lat_seq // tm
    lat_tiles = t_lat // tm
    n = t // tm
    na = xa.shape[0] // tm
    cur = lambda i: jnp.minimum(i, n - 1)
    prev = lambda i: jnp.maximum(i - 1, 0)

    def tmap(i):
        j = prev(i)
        return (jnp.where(j < lat_tiles, j % pos_blocks, pos_blocks), 0)

    tspec = pl.BlockSpec((tm, LANES), tmap)
    full = lambda i: (0, 0)
    row_cur = lambda cols: pl.BlockSpec((tm, cols), lambda i: (cur(i), 0))
    row_prev = lambda cols: pl.BlockSpec((tm, cols), lambda i: (prev(i), 0))
    return pl.pallas_call(
        functools.partial(_even_in_body, na=na, scale=MLA_QK ** -0.5 * LOG2_E),
        grid=(n + 1,),
        in_specs=[pl.BlockSpec((tm, d), lambda i: (jnp.minimum(cur(i), na - 1), 0)),
                  pl.BlockSpec((tm, d), lambda i: (jnp.maximum(cur(i) - na, 0), 0)),
                  pl.BlockSpec((None, 6, d), lambda i: (cur(i) * tm // seg_len, 0, 0)),
                  pl.BlockSpec((1, d), full),
                  pl.BlockSpec((d, n_pad), full),
                  pl.BlockSpec((1, MLA_Q_RANK), full),
                  pl.BlockSpec((1, MLA_KV_RANK), full),
                  pl.BlockSpec((MLA_Q_RANK, MLA_HEADS * LANES), full),
                  pl.BlockSpec((1, LANES), full),
                  tspec, tspec, tspec],
        out_specs=[row_cur(3 * HY_WIDTH), row_cur(LANES), row_prev(MLA_HEADS * LANES), row_prev(MLA_KV_RANK)],
        out_shape=[jax.ShapeDtypeStruct((t, 3 * HY_WIDTH), F32),
                   jax.ShapeDtypeStruct((t, LANES), F32),
                   jax.ShapeDtypeStruct((t, MLA_HEADS * LANES), BF16),
                   jax.ShapeDtypeStruct((t, MLA_KV_RANK), F32)],
        scratch_shapes=[pltpu.VMEM((2, tm, MLA_Q_RANK + MLA_KV_RANK), F32)],
        compiler_params=_params("arbitrary"),
        name="even_in",
    )(xa, xb, mod, p['norm1'].reshape(1, d), w_in, p['qa_norm'].reshape(1, -1), p['kva_norm'].reshape(1, -1),
      wuq, qn, cos_t, sin_a, sin_b)


def _mla_kv_body(ckvn_ref, kr_ref, wk_ref, wv_ref, kn_ref, cos_ref, sa_ref, sb_ref, k_ref, v_ref):
    c = ckvn_ref[...].astype(BF16)
    k = _dot(c, wk_ref[...])
    v_ref[...] = _dot(c, wv_ref[...]).astype(BF16)
    kr = pltpu.roll(kr_ref[...], MLA_NOPE, 1)
    cos_t, sin_a, sin_b = cos_ref[...], sa_ref[...], sb_ref[...]
    g = kn_ref[...]
    for h in range(MLA_HEADS):
        sl = slice(h * LANES, (h + 1) * LANES)
        kh = _rope(_rms(k[:, sl] + kr, g, MLA_QK), cos_t, sin_a, sin_b, MLA_ROPE // 4)
        k_ref[:, sl] = kh.astype(BF16)


def _mla_kv(ckvn_rows, kr_rows, p, tables, *, tm, nb, past, lat_seq):
    r = ckvn_rows.shape[0]
    w = p['w_ukv'].reshape(MLA_KV_RANK, MLA_HEADS, MLA_NOPE + MLA_V)
    wk = _pad_heads(w[:, :, :MLA_NOPE].reshape(MLA_KV_RANK, -1), MLA_HEADS, MLA_NOPE).astype(BF16)
    wv = w[:, :, MLA_NOPE:].reshape(MLA_KV_RANK, MLA_HEADS * MLA_V).astype(BF16)
    cos_t, sin_a, sin_b = tables
    kn = jnp.zeros((1, LANES), F32).at[0, :MLA_QK].set(p['k_norm'])
    per_b = (past + lat_seq) // tm
    past_tiles = past // tm
    pos_blocks = lat_seq // tm
    lat_tiles = nb * per_b

    def tmap(i):
        j = i % per_b
        is_pos = jnp.logical_and(i < lat_tiles, j >= past_tiles)
        return (jnp.where(is_pos, j - past_tiles, pos_blocks), 0)

    tspec = pl.BlockSpec((tm, LANES), tmap)
    full = lambda i: (0, 0)
    return pl.pallas_call(
        _mla_kv_body,
        grid=(r // tm,),
        in_specs=[pl.BlockSpec((tm, MLA_KV_RANK), lambda i: (i, 0)),
                  pl.BlockSpec((tm, LANES), lambda i: (i, 0)),
                  pl.BlockSpec((MLA_KV_RANK, MLA_HEADS * LANES), full),
                  pl.BlockSpec((MLA_KV_RANK, MLA_HEADS * MLA_V), full),
                  pl.BlockSpec((1, LANES), full),
                  tspec, tspec, tspec],
        out_specs=[pl.BlockSpec((tm, MLA_HEADS * LANES), lambda i: (i, 0)),
                   pl.BlockSpec((tm, MLA_HEADS * MLA_V), lambda i: (i, 0))],
        out_shape=[jax.ShapeDtypeStruct((r, MLA_HEADS * LANES), BF16),
                   jax.ShapeDtypeStruct((r, MLA_HEADS * MLA_V), BF16)],
        compiler_params=_params("parallel"),
        name="mla_kv",
    )(ckvn_rows, kr_rows, wk, wv, kn, cos_t, sin_a, sin_b)


ATT_CHUNK = 512
ATT_UNIT_ROWS = 256
ATT_TQ = 4096


def _fill_vaug(vaug_ref, g, v_blocks):
    off = 0
    for v in v_blocks:
        n = v.shape[0]
        vaug_ref[g, off:off + n, :LANES] = v
        off += n
    vaug_ref[g, :, LANES:] = jnp.ones((vaug_ref.shape[1], LANES), BF16)


def _softmax_pv(units, s_ref, n_keys):
    chunk = min(ATT_CHUNK, n_keys)
    chunks = [slice(c * chunk, (c + 1) * chunk) for c in range(n_keys // chunk)]

    def scores(u, rows, m_lane):
        s = _dot_nt(units[u][0], units[u][1](rows))
        s_ref[u % 2, :, rows] = s
        for j in range(chunk // LANES):
            blk = s[:, j * LANES:(j + 1) * LANES]
            m_lane = blk if m_lane is None else jnp.maximum(m_lane, blk)
        return m_lane

    def values(u, rows, m, acc):
        p = jnp.exp2(s_ref[u % 2, :, rows] - m).astype(BF16)
        d = _dot(p, units[u][2](rows))
        return d if acc is None else acc + d

    outs = []
    m_lane = None
    for rows in chunks:
        m_lane = scores(0, rows, m_lane)
    for u in range(len(units)):
        m = jnp.max(m_lane, axis=-1, keepdims=True)
        acc, m_lane = None, None
        for rows in chunks:
            acc = values(u, rows, m, acc)
            if u + 1 < len(units):
                m_lane = scores(u + 1, rows, m_lane)
        outs.append(acc)
    return outs


def _mla_attn_body(prev_ref, q_ref, k_ref, v_ref, o_ref, vaug_ref, s_ref):
    pairs = vaug_ref.shape[0]

    @pl.when(pl.program_id(2) == 0)
    def _():
        for g in range(pairs):
            _fill_vaug(vaug_ref, g, [v_ref[:, g * LANES:(g + 1) * LANES]])

    n_keys = k_ref.shape[0]
    tq = q_ref.shape[0]
    ur = s_ref.shape[1]
    units, slots = [], []
    for r0 in range(0, tq, ur):
        for g in range(pairs):
            slots.append((r0, g))
            for hh in range(2):
                sl = slice((2 * g + hh) * LANES, (2 * g + hh + 1) * LANES)
                units.append((q_ref[r0:r0 + ur, sl], lambda rows, sl=sl: k_ref[rows, sl],
                              lambda rows, g=g: vaug_ref[g, rows, :]))
    res = [r[:, :LANES] / r[:, LANES:] for r in _softmax_pv(units, s_ref, n_keys)]
    lane = lax.broadcasted_iota(jnp.int32, res[0].shape, 1)
    for i, (r0, g) in enumerate(slots):
        o_ref[r0:r0 + ur, g * LANES:(g + 1) * LANES] = jnp.where(
            lane < MLA_V, res[2 * i], res[2 * i + 1]).astype(BF16)


def _mla_attn(prev, q, k, v, *, tq, pairs, n_seq, seq_q, seq_k, q_row0, k_row0):
    hp = MLA_HEADS // 2 // pairs
    nq = seq_q // tq
    qb0, kb0 = q_row0 // tq, k_row0 // seq_k
    return pl.pallas_call(
        _mla_attn_body,
        grid=(n_seq, hp, nq),
        in_specs=[pl.BlockSpec(memory_space=pl.ANY),
                  pl.BlockSpec((tq, pairs * 2 * LANES), lambda s, h, i: (qb0 + s * nq + i, h)),
                  pl.BlockSpec((seq_k, pairs * 2 * LANES), lambda s, h, i: (kb0 + s, h)),
                  pl.BlockSpec((seq_k, pairs * 2 * MLA_V), lambda s, h, i: (kb0 + s, h))],
        out_specs=pl.BlockSpec((tq, pairs * 2 * MLA_V), lambda s, h, i: (qb0 + s * nq + i, h)),
        out_shape=jax.ShapeDtypeStruct(prev.shape, prev.dtype),
        input_output_aliases={0: 0},
        scratch_shapes=[pltpu.VMEM((pairs, seq_k, 2 * LANES), BF16),
                        pltpu.VMEM((2, min(tq, ATT_UNIT_ROWS), seq_k), F32)],
        compiler_params=_params("arbitrary", "arbitrary", "arbitrary"),
        name="mla_attn",
    )(prev, q, k, v)


def _diff_attn_body(prev_ref, *refs, n_seg, lambda_init):
    q_ref, lam_ref, sub_ref = refs[0], refs[1], refs[2]
    k_refs = refs[3:3 + n_seg]
    v_refs = refs[3 + n_seg:3 + 2 * n_seg]
    o_ref, kcat_ref, vaug_ref, s_ref = refs[3 + 2 * n_seg:]
    heads = vaug_ref.shape[0]

    def head_block(ref, g):
        return ref[g] if len(ref.shape) == 3 else ref[:, g * LANES:(g + 1) * LANES]

    @pl.when(pl.program_id(2) == 0)
    def _():
        for g in range(heads):
            _fill_vaug(vaug_ref, g, [head_block(v, g) for v in v_refs])
            off = 0
            for k in k_refs:
                kcat_ref[g, off:off + k.shape[-2], :] = head_block(k, g)
                off += k.shape[-2]

    lp = lam_ref[...]
    lam = (jnp.exp(jnp.sum(lp[0:1] * lp[1:2], axis=-1, keepdims=True))
           - jnp.exp(jnp.sum(lp[2:3] * lp[3:4], axis=-1, keepdims=True)) + lambda_init)
    n_keys = kcat_ref.shape[1]
    tq = q_ref.shape[0]
    ur = s_ref.shape[1]
    units, slots = [], []
    for r0 in range(0, tq, ur):
        for g in range(heads):
            slots.append((r0, g))
            q = q_ref[r0:r0 + ur, g * LANES:(g + 1) * LANES].astype(F32)
            lane = lax.broadcasted_iota(jnp.int32, q.shape, 1)
            k_of = lambda rows, g=g: kcat_ref[g, rows, :]
            v_of = lambda rows, g=g: vaug_ref[g, rows, :]
            units.append((jnp.where(lane < DIFF_DH, q, 0.0).astype(BF16), k_of, v_of))
            units.append((jnp.where(lane < DIFF_DH, 0.0, q).astype(BF16), k_of, v_of))
    res = _softmax_pv(units, s_ref, n_keys)
    for i, (r0, g) in enumerate(slots):
        r1, r2 = res[2 * i], res[2 * i + 1]
        o = r1[:, :LANES] / r1[:, LANES:] - (lam / r2[:, LANES:]) * r2[:, :LANES]
        o_ref[r0:r0 + ur, g * LANES:(g + 1) * LANES] = (
            _rms(o, sub_ref[...]) * (1.0 - lambda_init)).astype(BF16)


def _diff_attn(prev, q, k_new, v_new, k_cache, v_cache, lam_p, subln, *, tq, heads, n_seq, seq_q, q_row0,
               lambda_init):
    nq = seq_q // tq
    qb0 = q_row0 // tq
    sb0 = q_row0 // seq_q
    d = 2 * DIFF_DH
    new_spec = pl.BlockSpec((seq_q, heads * d), lambda s, h, i: (sb0 + s, h))
    if k_cache is None:
        n_seg, k_args, v_args, k_specs, v_specs = 1, [k_new], [v_new], [new_spec], [new_spec]
        n_keys = seq_q
    else:
        past = k_cache.shape[2]
        c_spec = pl.BlockSpec((None, heads, past, d), lambda s, h, i: (s, h, 0, 0))
        n_seg, k_args, v_args = 2, [k_cache, k_new], [v_cache, v_new]
        k_specs, v_specs = [c_spec, new_spec], [c_spec, new_spec]
        n_keys = past + seq_q
    return pl.pallas_call(
        functools.partial(_diff_attn_body, n_seg=n_seg, lambda_init=lambda_init),
        grid=(n_seq, DIFF_HEADS // heads, nq),
        in_specs=[pl.BlockSpec(memory_space=pl.ANY),
                  pl.BlockSpec((tq, heads * d), lambda s, h, i: (qb0 + s * nq + i, h)),
                  pl.BlockSpec((4, DIFF_DH), lambda s, h, i: (0, 0)),
                  pl.BlockSpec((1, d), lambda s, h, i: (0, 0))] + k_specs + v_specs,
        out_specs=pl.BlockSpec((tq, heads * d), lambda s, h, i: (qb0 + s * nq + i, h)),
        out_shape=jax.ShapeDtypeStruct(prev.shape, prev.dtype),
        input_output_aliases={0: 0},
        scratch_shapes=[pltpu.VMEM((heads, n_keys, d), BF16), pltpu.VMEM((heads, n_keys, 2 * LANES), BF16),
                        pltpu.VMEM((2, min(tq, ATT_UNIT_ROWS), n_keys), F32)],
        compiler_params=_params("arbitrary", "arbitrary", "arbitrary"),
        name="diff_attn",
    )(prev, q, lam_p, subln.reshape(1, d), *k_args, *v_args)


def _even_out_body(xa_ref, xb_ref, mod_ref, ca_ref, cb_ref, u_ref, x0_ref, db_ref, o_ref, wa_ref, wb_ref,
                   out_ref, *, na):
    first = pl.program_id(0) < na
    x = jnp.where(first, xa_ref[...], xb_ref[...])
    conv = jnp.where(first, ca_ref[...], cb_ref[...])
    y_hy = (conv + u_ref[...] * db_ref[...]) * x0_ref[...]
    acc = _dot(y_hy.astype(BF16), wa_ref[...]) + _dot(o_ref[...], wb_ref[...])
    out_ref[...] = x + mod_ref[2:3, :] * acc


def _even_out(xa, xb, mod, conv_a, conv_b, u, x0, dbias, o, w_hy, w_att, *, tm, seg_len):
    d = xa.shape[1]
    t = xa.shape[0] + xb.shape[0]
    c = u.shape[1]
    xa_spec, xb_spec, na = _two_part_specs(xa, xb, tm)
    ca_spec, cb_spec, na_c = _two_part_specs(conv_a, conv_b, tm)
    assert na == na_c
    row = lambda cols: pl.BlockSpec((tm, cols), lambda i: (i, 0))
    full = lambda a: pl.BlockSpec(a.shape, lambda i: (0, 0))
    return pl.pallas_call(
        functools.partial(_even_out_body, na=na),
        grid=(t // tm,),
        in_specs=[xa_spec, xb_spec,
                  pl.BlockSpec((None, 6, d), lambda i: (i * tm // seg_len, 0, 0)),
                  ca_spec, cb_spec, row(c), row(c), full(dbias), row(o.shape[1]), full(w_hy), full(w_att)],
        out_specs=row(d),
        out_shape=jax.ShapeDtypeStruct((t, d), F32),
        compiler_params=_params("arbitrary"),
        name="even_out",
    )(xa, xb, mod, conv_a, conv_b, u, x0, dbias, o, w_hy, w_att)


def _ffn_body(x_ref, mod_ref, g_ref, wg_ref, wu_ref, wd_ref, o_ref, h_ref, acc_ref):
    f = pl.program_id(1)

    @pl.when(f == 0)
    def _():
        h_ref[...] = _norm_mod(x_ref[...], g_ref[...], mod_ref[3:4, :], mod_ref[4:5, :]).astype(BF16)
        acc_ref[...] = jnp.zeros_like(acc_ref)

    h = h_ref[...]
    a = _silu(_dot(h, wg_ref[...])) * _dot(h, wu_ref[...])
    acc_ref[...] += _dot(a.astype(BF16), wd_ref[...])

    @pl.when(f == pl.num_programs(1) - 1)
    def _():
        o_ref[...] = x_ref[...] + mod_ref[5:6, :] * acc_ref[...]


def _ffn(x, mod, g, wg, wu, wd, *, tm, tf, seg_len):
    t, d = x.shape
    ff = wg.shape[1]
    return pl.pallas_call(
        _ffn_body,
        grid=(t // tm, ff // tf),
        in_specs=[pl.BlockSpec((tm, d), lambda i, f: (i, 0)),
                  pl.BlockSpec((None, 6, d), lambda i, f: (i * tm // seg_len, 0, 0)),
                  pl.BlockSpec((1, d), lambda i, f: (0, 0)),
                  pl.BlockSpec((d, tf), lambda i, f: (0, f)),
                  pl.BlockSpec((d, tf), lambda i, f: (0, f)),
                  pl.BlockSpec((tf, d), lambda i, f: (f, 0))],
        out_specs=pl.BlockSpec((tm, d), lambda i, f: (i, 0)),
        out_shape=jax.ShapeDtypeStruct((t, d), F32),
        scratch_shapes=[pltpu.VMEM((tm, d), BF16), pltpu.VMEM((tm, d), F32)],
        compiler_params=_params("parallel", "arbitrary"),
        name="ffn",
    )(x, mod, g.reshape(1, d), wg, wu, wd)


def _group_ms(x, gmat):
    hi, lo = _split_bf16(x * x)
    return (_dot(hi, gmat) + _dot(lo, gmat)) * (1.0 / DIFF_DH)


def _qkv_body(x_ref, mod_ref, g_ref, w_ref, gm_ref, qn_ref, kn_ref, cos_ref, sa_ref, sb_ref,
              q_ref, k_ref, v_ref, kf_ref, vf_ref, *, scale, seq):
    h = _norm_mod(x_ref[...], g_ref[...], mod_ref[0:1, :], mod_ref[1:2, :]).astype(BF16)
    z = _dot(h, w_ref[...])
    hd = DIFF_HEADS * 2 * DIFF_DH
    tm = z.shape[0]
    cos_t, sin_a, sin_b = cos_ref[...], sa_ref[...], sb_ref[...]
    gm = gm_ref[...]
    shift = DIFF_DH // 4
    for hh in range(DIFF_HEADS):
        sl = slice(hh * LANES, (hh + 1) * LANES)
        qh = z[:, hh * LANES:(hh + 1) * LANES]
        qh = qh * lax.rsqrt(_group_ms(qh, gm) + NORM_EPS) * qn_ref[...]
        q_ref[:, sl] = (_rope(qh, cos_t, sin_a, sin_b, shift) * scale).astype(BF16)
        kh = z[:, hd + hh * LANES:hd + (hh + 1) * LANES]
        kh = kh * lax.rsqrt(_group_ms(kh, gm) + NORM_EPS) * kn_ref[...]
        k_ref[:, sl] = _rope(kh, cos_t, sin_a, sin_b, shift).astype(BF16)
        vh = z[:, 2 * hd + hh * LANES:2 * hd + (hh + 1) * LANES]
        for s in range(tm // seq):
            kf_ref[s, hh] = kh[s * seq:(s + 1) * seq, :]
            vf_ref[s, hh] = vh[s * seq:(s + 1) * seq, :]
    v_ref[...] = z[:, 2 * hd:].astype(BF16)


def _qkv(x, mod, p, tables, *, tm, seg_len, t_lat, lat_seq, n_ctx, ctx_seq):
    t, d = x.shape
    hd = DIFF_HEADS * 2 * DIFF_DH
    wqk = p['w_qkv'][:, :2 * hd].reshape(d, 2, 2, DIFF_HEADS, DIFF_DH)
    wqk = wqk.transpose(0, 1, 3, 2, 4).reshape(d, 2 * hd)
    w = jnp.concatenate([wqk, p['w_qkv'][:, 2 * hd:]], axis=1).astype(BF16)
    gi = np.arange(LANES) // DIFF_DH
    gmat = jnp.asarray((gi[:, None] == gi[None, :]).astype(np.float32)).astype(BF16)
    cos_t, sin_a, sin_b = tables
    qn = jnp.tile(p['q_norm'], 2).reshape(1, LANES)
    kn = jnp.tile(p['k_norm'], 2).reshape(1, LANES)
    pos_blocks = lat_seq // tm
    lat_tiles = t_lat // tm
    seq_per_tile = tm // ctx_seq
    tspec = pl.BlockSpec((tm, LANES), lambda i: (jnp.where(i < lat_tiles, i % pos_blocks, pos_blocks), 0))
    full = lambda i: (0, 0)
    row = pl.BlockSpec((tm, hd), lambda i: (i, 0))
    fspec = pl.BlockSpec((seq_per_tile, DIFF_HEADS, ctx_seq, LANES),
                         lambda i: (jnp.maximum(i - lat_tiles, 0), 0, 0, 0))
    fshape = jax.ShapeDtypeStruct((n_ctx, DIFF_HEADS, ctx_seq, LANES), F32)
    return pl.pallas_call(
        functools.partial(_qkv_body, scale=DIFF_DH ** -0.5 * LOG2_E, seq=ctx_seq),
        grid=(t // tm,),
        in_specs=[pl.BlockSpec((tm, d), lambda i: (i, 0)),
                  pl.BlockSpec((None, 6, d), lambda i: (i * tm // seg_len, 0, 0)),
                  pl.BlockSpec((1, d), full),
                  pl.BlockSpec((d, 3 * hd), full),
                  pl.BlockSpec((LANES, LANES), full),
                  pl.BlockSpec((1, LANES), full), pl.BlockSpec((1, LANES), full),
                  tspec, tspec, tspec],
        out_specs=[row, row, row, fspec, fspec],
        out_shape=[jax.ShapeDtypeStruct((t, hd), BF16)] * 3 + [fshape, fshape],
        compiler_params=_params("arbitrary"),
        name="qkv",
    )(x, mod, p['norm1'].reshape(1, d), w, gmat, qn, kn, cos_t, sin_a, sin_b)


def _route(logits):
    lane = lax.broadcasted_iota(jnp.int32, logits.shape, 1)
    neg = jnp.float32(-jnp.inf)
    lg = jnp.where(lane < N_EXPERTS, logits, neg)
    m1 = jnp.max(lg, axis=-1, keepdims=True)
    i1 = jnp.min(jnp.where(lg == m1, lane, LANES), axis=-1, keepdims=True)
    lg2 = jnp.where(lane == i1, neg, lg)
    m2 = jnp.max(lg2, axis=-1, keepdims=True)
    i2 = jnp.min(jnp.where(lg2 == m2, lane, LANES), axis=-1, keepdims=True)
    e = jnp.exp(m2 - m1)
    w1 = 1.0 / (1.0 + e)
    w2 = e / (1.0 + e)
    return jnp.where(lane == i1, w1, 0.0) + jnp.where(lane == i2, w2, 0.0)


MOE_BLOCK = 1024
MOE_SUB = 256
MOE_ROUTE_TM = 512
MOE_WINDOW = 5
MOE_TF = 512
MOE_CMB = 256
MOE_CMB_BLOCKS = MOE_SUB // MOE_CMB + 1


def _moe_route_body(x_ref, o_ref, wo_ref, mod_ref, g_ref, wr_ref, xo_ref, h_ref, gates_ref, rank_ref, rank_t_ref,
                    carry_row, carry_col, *, tm):
    i = pl.program_id(0)

    @pl.when(i == 0)
    def _():
        carry_row[...] = jnp.zeros_like(carry_row)
        carry_col[...] = jnp.zeros_like(carry_col)

    x = x_ref[...] + mod_ref[2:3, :] * _dot(o_ref[...], wo_ref[...])
    xo_ref[...] = x
    h = _norm_mod(x, g_ref[...], mod_ref[3:4, :], mod_ref[4:5, :])
    h_ref[...] = h.astype(BF16)
    gates = _route(_dot_f32(h, wr_ref[...]))
    gates_ref[...] = gates
    sel = jnp.where(gates != 0.0, 1.0, 0.0)
    sel_t = sel.T
    r = lax.broadcasted_iota(jnp.int32, (tm, tm), 0)
    c = lax.broadcasted_iota(jnp.int32, (tm, tm), 1)
    lower = jnp.where(c < r, 1.0, 0.0).astype(BF16)
    upper = jnp.where(r < c, 1.0, 0.0).astype(BF16)
    before = _dot(lower, sel.astype(BF16)) + carry_row[...]
    before_t = _dot(sel_t.astype(BF16), upper) + carry_col[...]
    rank_ref[...] = jnp.where(sel > 0.0, before, -1.0)
    rank_t = jnp.where(sel_t > 0.0, before_t, -1.0)
    for s in range(tm // MOE_SUB):
        rank_t_ref[s] = rank_t[:SUBLANES, s * MOE_SUB:(s + 1) * MOE_SUB]
    carry_row[...] += jnp.sum(sel, axis=0, keepdims=True)
    carry_col[...] += jnp.sum(sel_t, axis=1, keepdims=True)


def _moe_route(x, o, w_out, mod, g, w_router, *, seg_len):
    t, d = x.shape
    tm = MOE_ROUTE_TM
    ne = w_router.shape[1]
    assert ne <= SUBLANES
    wr = jnp.zeros((d, LANES), F32).at[:, :ne].set(w_router)
    sub = tm // MOE_SUB
    row = lambda cols: pl.BlockSpec((tm, cols), lambda i: (i, 0))
    full = lambda a: pl.BlockSpec(a.shape, lambda i: (0, 0))
    return pl.pallas_call(
        functools.partial(_moe_route_body, tm=tm),
        grid=(t // tm,),
        in_specs=[row(d), row(o.shape[1]), full(w_out),
                  pl.BlockSpec((None, 6, d), lambda i: (i * tm // seg_len, 0, 0)),
                  pl.BlockSpec((1, d), lambda i: (0, 0)),
                  full(wr)],
        out_specs=[row(d), row(d), row(LANES), row(LANES),
                   pl.BlockSpec((sub, SUBLANES, MOE_SUB), lambda i: (i, 0, 0))],
        out_shape=[jax.ShapeDtypeStruct((t, d), F32),
                   jax.ShapeDtypeStruct((t, d), BF16),
                   jax.ShapeDtypeStruct((t, LANES), F32),
                   jax.ShapeDtypeStruct((t, LANES), F32),
                   jax.ShapeDtypeStruct((t // MOE_SUB, SUBLANES, MOE_SUB), F32)],
        scratch_shapes=[pltpu.VMEM((1, LANES), F32), pltpu.VMEM((LANES, 1), F32)],
        compiler_params=_params("arbitrary"),
        name="moe_route",
    )(x, o, w_out, mod, g.reshape(1, d), wr)


def _moe_plan(rank, ne, *, n_blocks):
    t = rank.shape[0]
    n_tiles = t // MOE_SUB
    per_blk = MOE_BLOCK // MOE_SUB
    n_sub = n_blocks * per_blk
    sel = (rank[:, :ne] >= 0.0).astype(jnp.int32)
    tile_cnt = sel.reshape(n_tiles, MOE_SUB, ne).sum(axis=1)
    tile_end = jnp.cumsum(tile_cnt, axis=0)
    tile_start = tile_end - tile_cnt
    cnt = tile_end[-1]
    nblk = (cnt + MOE_BLOCK - 1) // MOE_BLOCK
    bend = jnp.cumsum(nblk)
    bstart = bend - nblk
    e_last = jnp.max(jnp.where(cnt > 0, jnp.arange(ne), 0))
    b = jnp.arange(n_blocks)
    blk_valid = b < bend[-1]
    blk_e = jnp.minimum(jnp.sum(bend[None, :] <= b[:, None], axis=1), e_last).astype(jnp.int32)
    blk_r0 = (b - bstart[blk_e]) * MOE_BLOCK
    blk_rows = jnp.where(blk_valid, jnp.clip(cnt[blk_e] - blk_r0, 0, MOE_BLOCK), 0).astype(jnp.int32)
    j = jnp.arange(n_sub)
    sub_e = blk_e[j // per_blk]
    sub_r0 = blk_r0[j // per_blk] + (j % per_blk) * MOE_SUB
    sub_valid = jnp.logical_and(blk_valid[j // per_blk], sub_r0 < cnt[sub_e])
    ends = tile_end[:, sub_e]
    r1 = jnp.minimum(sub_r0 + MOE_SUB, cnt[sub_e])
    c_lo = jnp.sum(ends <= sub_r0[None, :], axis=0)
    c_hi = jnp.sum(ends < r1[None, :], axis=0)
    c_lo = jnp.where(sub_valid, c_lo, 1).astype(jnp.int32)
    c_hi = jnp.where(sub_valid, jnp.minimum(c_hi, n_tiles - 1), 0).astype(jnp.int32)
    base = (bstart * MOE_BLOCK).astype(jnp.int32)
    n_cmb = n_sub * (MOE_SUB // MOE_CMB)
    j0 = jnp.minimum((base[None, :] + tile_start) // MOE_CMB, n_cmb - MOE_CMB_BLOCKS).astype(jnp.int32)
    return dict(blk_e=blk_e, blk_valid=blk_valid.astype(jnp.int32), blk_rows=blk_rows,
                sub_e=sub_e.astype(jnp.int32), sub_r0=sub_r0.astype(jnp.int32), c_lo=c_lo, c_hi=c_hi,
                base=base, j0=j0.reshape(-1))


def _moe_dispatch_body(e_ref, r0_ref, lo_ref, hi_ref, h_ref, rank_t_ref, xs_ref, acc_ref):
    j = pl.program_id(0)
    e = e_ref[j]
    rows = (r0_ref[j] + lax.broadcasted_iota(jnp.int32, (MOE_SUB, 1), 0)).astype(F32)
    sub = lax.broadcasted_iota(jnp.int32, (SUBLANES, MOE_SUB), 0)
    n_tiles = rank_t_ref.shape[0]
    lo, hi = lo_ref[j], hi_ref[j]
    acc_ref[...] = jnp.zeros_like(acc_ref)

    def step(w, carry):
        first = lo + w * MOE_WINDOW
        c0 = jnp.minimum(first, n_tiles - MOE_WINDOW)
        pieces = []
        for i in range(MOE_WINDOW):
            c = c0 + i
            rk = jnp.sum(jnp.where(sub == e, rank_t_ref[c], 0.0), axis=0, keepdims=True)
            rk = jnp.where(c >= first, rk, -1.0)
            pieces.append(jnp.where(rk == rows, 1.0, 0.0).astype(BF16))
        onehot = jnp.concatenate(pieces, axis=1)
        off = pl.multiple_of(c0 * MOE_SUB, MOE_SUB)
        acc_ref[...] += _dot(onehot, h_ref[pl.ds(off, MOE_WINDOW * MOE_SUB), :])
        return carry

    lax.fori_loop(0, (hi - lo + MOE_WINDOW) // MOE_WINDOW, step, 0)
    xs_ref[...] = acc_ref[...].astype(BF16)


def _moe_dispatch(h, rank_t, plan, *, n_sub):
    t, d = h.shape
    grid_spec = pltpu.PrefetchScalarGridSpec(
        num_scalar_prefetch=4,
        grid=(n_sub,),
        in_specs=[pl.BlockSpec((t, d), lambda j, *_: (0, 0), pipeline_mode=pl.Buffered(1)),
                  pl.BlockSpec(rank_t.shape, lambda j, *_: (0, 0, 0), pipeline_mode=pl.Buffered(1))],
        out_specs=pl.BlockSpec((MOE_SUB, d), lambda j, *_: (j, 0)),
        scratch_shapes=[pltpu.VMEM((MOE_SUB, d), F32)],
    )
    return pl.pallas_call(
        _moe_dispatch_body,
        grid_spec=grid_spec,
        out_shape=jax.ShapeDtypeStruct((n_sub * MOE_SUB, d), BF16),
        compiler_params=_params("arbitrary"),
        name="moe_dispatch",
    )(plan['sub_e'], plan['sub_r0'], plan['c_lo'], plan['c_hi'], h, rank_t)


def _moe_ffn_body(e_ref, valid_ref, rows_ref, xs_ref, wg_ref, wu_ref, wd_ref, y_ref, acc_ref):
    b = pl.program_id(0)
    f = pl.program_id(1)
    n_rows = rows_ref[b]
    last = f == pl.num_programs(1) - 1
    wg = wg_ref[...].astype(BF16)
    wu = wu_ref[...].astype(BF16)
    wd = wd_ref[...].astype(BF16)
    full = n_rows == MOE_BLOCK

    def swiglu(h):
        a = _silu(_dot(h, wg)) * _dot(h, wu)
        return _dot(a.astype(BF16), wd)

    @pl.when(jnp.logical_and(full, f == 0))
    def _():
        acc_ref[...] = swiglu(xs_ref[...])

    @pl.when(jnp.logical_and(full, f > 0))
    def _():
        acc_ref[...] += swiglu(xs_ref[...])

    @pl.when(jnp.logical_and(full, last))
    def _():
        y_ref[...] = acc_ref[...].astype(BF16)

    for s in range(MOE_BLOCK // MOE_SUB):
        sl = slice(s * MOE_SUB, (s + 1) * MOE_SUB)
        live = jnp.logical_and(jnp.logical_not(full), s * MOE_SUB < n_rows)
        dead = jnp.logical_and(jnp.logical_not(full), s * MOE_SUB >= n_rows)

        @pl.when(jnp.logical_and(live, f == 0))
        def _():
            acc_ref[sl, :] = jnp.zeros((MOE_SUB, acc_ref.shape[1]), F32)

        @pl.when(live)
        def _():
            acc_ref[sl, :] += swiglu(xs_ref[sl, :])

        @pl.when(jnp.logical_and(live, last))
        def _():
            y_ref[sl, :] = acc_ref[sl, :].astype(BF16)

        @pl.when(jnp.logical_and(dead, last))
        def _():
            y_ref[sl, :] = jnp.zeros((MOE_SUB, y_ref.shape[1]), BF16)


def _moe_ffn(xs, wg, wu, wd, plan, *, n_blocks, tf):
    _, d = xs.shape
    ne, _, ff = wg.shape
    nf = ff // tf

    def w_in(b, f, e_ref, valid_ref, rows_ref):
        return (e_ref[b], 0, jnp.where(valid_ref[b] > 0, f, nf - 1))

    def w_down(b, f, e_ref, valid_ref, rows_ref):
        return (e_ref[b], jnp.where(valid_ref[b] > 0, f, nf - 1), 0)

    grid_spec = pltpu.PrefetchScalarGridSpec(
        num_scalar_prefetch=3,
        grid=(n_blocks, nf),
        in_specs=[pl.BlockSpec((MOE_BLOCK, d), lambda b, f, *_: (b, 0)),
                  pl.BlockSpec((None, d, tf), w_in),
                  pl.BlockSpec((None, d, tf), w_in),
                  pl.BlockSpec((None, tf, d), w_down)],
        out_specs=pl.BlockSpec((MOE_BLOCK, d), lambda b, f, *_: (b, 0)),
        scratch_shapes=[pltpu.VMEM((MOE_BLOCK, d), F32)],
    )
    return pl.pallas_call(
        _moe_ffn_body,
        grid_spec=grid_spec,
        out_shape=jax.ShapeDtypeStruct((n_blocks * MOE_BLOCK, d), BF16),
        compiler_params=_params("arbitrary", "arbitrary"),
        name="moe_ffn",
    )(plan['blk_e'], plan['blk_valid'], plan['blk_rows'], xs, wg, wu, wd)


def _moe_combine_body(j0_ref, base_ref, x_ref, mod_ref, gates_ref, rank_ref, *rest, ne, split_tiles):
    nblk = MOE_CMB_BLOCKS
    y_refs, o_ref, o2_ref = rest[:nblk * ne], rest[nblk * ne], rest[nblk * ne + 1]
    c = pl.program_id(0)
    gates = gates_ref[...]
    rank = rank_ref[...]
    lane = lax.broadcasted_iota(jnp.int32, gates.shape, 1)
    col = lax.broadcasted_iota(jnp.int32, (1, MOE_CMB), 1).astype(F32)
    acc = None
    for e in range(ne):
        pick = lane == e
        g = jnp.sum(jnp.where(pick, gates, 0.0), axis=-1, keepdims=True)
        rk = jnp.sum(jnp.where(pick, rank, 0.0), axis=-1, keepdims=True)
        shift = (base_ref[e] - j0_ref[c * ne + e] * MOE_CMB).astype(F32)
        loc = jnp.where(rk >= 0.0, rk + shift, -1.0)
        picked = None
        for b in range(nblk):
            onehot = jnp.where(loc == col + float(b * MOE_CMB), 1.0, 0.0).astype(BF16)
            d = _dot(onehot, y_refs[nblk * e + b][...])
            picked = d if picked is None else picked + d
        contrib = g * picked
        acc = contrib if acc is None else acc + contrib
    out = x_ref[...] + mod_ref[5:6, :] * acc

    @pl.when(c < split_tiles)
    def _():
        o_ref[...] = out

    @pl.when(c >= split_tiles)
    def _():
        o2_ref[...] = out


def _moe_combine(x, mod, gates, rank, y, plan, *, ne, seg_len, t_split):
    t, d = x.shape
    tm = MOE_SUB
    split_tiles = t_split // tm
    y_specs = []
    for e in range(ne):
        for b in range(MOE_CMB_BLOCKS):
            y_specs.append(pl.BlockSpec((MOE_CMB, d), lambda c, j0, base, e=e, b=b: (j0[c * ne + e] + b, 0)))
    grid_spec = pltpu.PrefetchScalarGridSpec(
        num_scalar_prefetch=2,
        grid=(t // tm,),
        in_specs=[pl.BlockSpec((tm, d), lambda c, *_: (c, 0)),
                  pl.BlockSpec((None, 6, d), lambda c, *_: (c * tm // seg_len, 0, 0)),
                  pl.BlockSpec((tm, LANES), lambda c, *_: (c, 0)),
                  pl.BlockSpec((tm, LANES), lambda c, *_: (c, 0))] + y_specs,
        out_specs=[pl.BlockSpec((tm, d), lambda c, *_: (jnp.minimum(c, split_tiles - 1), 0)),
                   pl.BlockSpec((tm, d), lambda c, *_: (jnp.maximum(c - split_tiles, 0), 0))],
    )
    return pl.pallas_call(
        functools.partial(_moe_combine_body, ne=ne, split_tiles=split_tiles),
        grid_spec=grid_spec,
        out_shape=[jax.ShapeDtypeStruct((t_split, d), F32), jax.ShapeDtypeStruct((t - t_split, d), F32)],
        compiler_params=_params("arbitrary"),
        name="moe_combine",
    )(plan['j0'], plan['base'], x, mod, gates, rank, *([y] * (MOE_CMB_BLOCKS * ne)))


def _attn_out_moe(x, o, w_out, mod, g, w_router, wg, wu, wd, *, seg_len, t_split, top_k=2):
    t, d = x.shape
    ne = w_router.shape[1]
    n_blocks = t * top_k // MOE_BLOCK + ne
    x, h, gates, rank, rank_t = _moe_route(x, o, w_out, mod, g, w_router, seg_len=seg_len)
    plan = _moe_plan(rank, ne, n_blocks=n_blocks)
    xs = _moe_dispatch(h, rank_t, plan, n_sub=n_blocks * (MOE_BLOCK // MOE_SUB))
    y = _moe_ffn(xs, wg, wu, wd, plan, n_blocks=n_blocks, tf=MOE_TF)
    return _moe_combine(x, mod, gates, rank, y, plan, ne=ne, seg_len=seg_len, t_split=t_split)


def _even_layer(x_lat, x_ctx, cond8, p, cache_ckv, cache_kr, *, nb, lat_seq, n_ctx, ctx_seq):
    d = x_lat.shape[1]
    t_lat = nb * lat_seq
    t = t_lat + x_ctx.shape[0]
    seg_len = lat_seq
    past = cache_ckv.shape[1]
    mod = _adaln(cond8, p['w_mod'], p['b_mod'])

    tm = 512
    tables = _rope_tables(lat_seq, MLA_ROPE, (MLA_NOPE,), tm)
    z_hy, kr, q, ckvn = _even_in(x_lat, x_ctx, mod, p, tables, tm=tm, seg_len=seg_len, t_lat=t_lat,
                                 lat_seq=lat_seq)

    u, x0 = _hy_pre(z_hy, p['hy_conv_w'], p['hy_conv_b'], t_lat=t_lat, lat_seq=lat_seq, ctx_seq=ctx_seq)
    k_lat, k_ctx, k_sum = _hyena_filters(p, lat_seq=lat_seq, ctx_seq=ctx_seq)
    dbias = p['hy_dbias'].reshape(1, HY_WIDTH)
    conv_lat = _hyena_lat(k_lat, k_sum[0:1], u, seq=lat_seq, nb=nb)
    conv_ctx = _hyena_ctx(k_ctx, k_sum[1:2], u, seq=ctx_seq, nseq=n_ctx, u_row0=t_lat)

    cache_kr_p = jnp.zeros((nb, past, LANES), F32).at[:, :, :MLA_ROPE].set(cache_kr)
    ckvn_rows = jnp.concatenate(
        [jnp.concatenate([cache_ckv, ckvn[:t_lat].reshape(nb, lat_seq, -1)], axis=1).reshape(nb * (past + lat_seq), -1),
         ckvn[t_lat:]], axis=0)
    kr_rows = jnp.concatenate(
        [jnp.concatenate([cache_kr_p, kr[:t_lat].reshape(nb, lat_seq, LANES)], axis=1).reshape(nb * (past + lat_seq), LANES),
         kr[t_lat:]], axis=0)
    k_all, v_all = _mla_kv(ckvn_rows, kr_rows, p, tables, tm=tm, nb=nb, past=past, lat_seq=lat_seq)
    o = jnp.zeros((t, MLA_HEADS * MLA_V), BF16)
    o = _mla_attn(o, q, k_all, v_all, tq=ATT_TQ, pairs=1, n_seq=nb, seq_q=lat_seq, seq_k=past + lat_seq,
                  q_row0=0, k_row0=0)
    o = _mla_attn(o, q, k_all, v_all, tq=ctx_seq, pairs=MLA_HEADS // 2, n_seq=n_ctx, seq_q=ctx_seq,
                  seq_k=ctx_seq, q_row0=t_lat, k_row0=nb * (past + lat_seq))

    w_out = p['w_out'].astype(BF16)
    x = _even_out(x_lat, x_ctx, mod, conv_lat, conv_ctx, u, x0, dbias, o, w_out[:HY_WIDTH], w_out[HY_WIDTH:],
                  tm=512, seg_len=seg_len)
    x = _ffn(x, mod, p['norm2'], p['ffn_w_gate'].astype(BF16), p['ffn_w_up'].astype(BF16),
             p['ffn_w_down'].astype(BF16), tm=512, tf=1408, seg_len=seg_len)
    new_ckv = ckvn[t_lat:].reshape(n_ctx, ctx_seq, -1)
    new_kr = kr[t_lat:, :MLA_ROPE].reshape(n_ctx, ctx_seq, MLA_ROPE)
    return x, new_ckv, new_kr


def _odd_layer(x, cond8, p, cache_k, cache_v, lambda_init, *, nb, lat_seq, n_ctx, ctx_seq):
    t, d = x.shape
    t_lat = nb * lat_seq
    seg_len = lat_seq
    mod = _adaln(cond8, p['w_mod'], p['b_mod'])
    tm = 512
    tables = _rope_tables(lat_seq, DIFF_DH, (0, DIFF_DH), tm)
    q, k, v, new_k, new_v = _qkv(x, mod, p, tables, tm=tm, seg_len=seg_len, t_lat=t_lat, lat_seq=lat_seq,
                                 n_ctx=n_ctx, ctx_seq=ctx_seq)
    lam_p = jnp.stack([p['lam_q1'], p['lam_k1'], p['lam_q2'], p['lam_k2']])
    o = jnp.zeros((t, DIFF_HEADS * 2 * DIFF_DH), BF16)
    o = _diff_attn(o, q, k, v, cache_k.astype(BF16), cache_v.astype(BF16), lam_p, p['subln'],
                   tq=ATT_TQ, heads=1, n_seq=nb, seq_q=lat_seq, q_row0=0, lambda_init=lambda_init)
    o = _diff_attn(o, q, k, v, None, None, lam_p, p['subln'],
                   tq=ctx_seq, heads=DIFF_HEADS, n_seq=n_ctx, seq_q=ctx_seq, q_row0=t_lat,
                   lambda_init=lambda_init)
    x_lat, x_ctx = _attn_out_moe(x, o, p['w_out'].astype(BF16), mod, p['norm2'], p['w_router'],
                                 p['moe_w_gate'], p['moe_w_up'], p['moe_w_down'], seg_len=seg_len, t_split=t_lat)
    return x_lat, x_ctx, new_k, new_v


def kernel(x_prompt, x_sample, cache_l0_ckv, cache_l0_krope, cache_l1_k, cache_l1_v, c, c_ctx,
           l0_w_mod, l0_b_mod, l0_norm1, l0_norm2, l0_w_in, l0_hy_conv_w, l0_hy_conv_b,
           l0_hy_fw1, l0_hy_fb1, l0_hy_freq1, l0_hy_fw2, l0_hy_fb2, l0_hy_freq2, l0_hy_fw3, l0_hy_dbias,
           l0_mla_qa_norm, l0_mla_w_uq, l0_mla_kva_norm, l0_mla_w_ukv, l0_mla_q_norm, l0_mla_k_norm,
           l0_w_out, l0_ffn_w_gate, l0_ffn_w_up, l0_ffn_w_down,
           l1_w_mod, l1_b_mod, l1_norm1, l1_norm2, l1_w_qkv, l1_q_norm, l1_k_norm,
           l1_lam_q1, l1_lam_k1, l1_lam_q2, l1_lam_k2, l1_subln, l1_w_out,
           l1_w_router, l1_moe_w_gate, l1_moe_w_up, l1_moe_w_down):
    even = {
        'w_mod': l0_w_mod, 'b_mod': l0_b_mod, 'norm1': l0_norm1, 'norm2': l0_norm2, 'w_in': l0_w_in,
        'hy_conv_w': l0_hy_conv_w, 'hy_conv_b': l0_hy_conv_b, 'hy_fw1': l0_hy_fw1, 'hy_fb1': l0_hy_fb1,
        'hy_freq1': l0_hy_freq1, 'hy_fw2': l0_hy_fw2, 'hy_fb2': l0_hy_fb2, 'hy_freq2': l0_hy_freq2,
        'hy_fw3': l0_hy_fw3, 'hy_dbias': l0_hy_dbias, 'qa_norm': l0_mla_qa_norm, 'w_uq': l0_mla_w_uq,
        'kva_norm': l0_mla_kva_norm, 'w_ukv': l0_mla_w_ukv, 'q_norm': l0_mla_q_norm, 'k_norm': l0_mla_k_norm,
        'w_out': l0_w_out, 'ffn_w_gate': l0_ffn_w_gate, 'ffn_w_up': l0_ffn_w_up, 'ffn_w_down': l0_ffn_w_down,
    }
    odd = {
        'w_mod': l1_w_mod, 'b_mod': l1_b_mod, 'norm1': l1_norm1, 'norm2': l1_norm2, 'w_qkv': l1_w_qkv,
        'q_norm': l1_q_norm, 'k_norm': l1_k_norm, 'lam_q1': l1_lam_q1, 'lam_k1': l1_lam_k1,
        'lam_q2': l1_lam_q2, 'lam_k2': l1_lam_k2, 'subln': l1_subln, 'w_out': l1_w_out,
        'w_router': l1_w_router, 'moe_w_gate': l1_moe_w_gate, 'moe_w_up': l1_moe_w_up,
        'moe_w_down': l1_moe_w_down,
    }
    n_ctx, ctx_seq, d = x_prompt.shape
    nb, lat_seq, _ = x_sample.shape
    assert n_ctx * ctx_seq == lat_seq, "segment layout needs equally sized modulation segments"
    dims = dict(nb=nb, lat_seq=lat_seq, n_ctx=n_ctx, ctx_seq=ctx_seq)
    t_lat = nb * lat_seq
    cond8 = jnp.zeros((SUBLANES, d), F32).at[:nb].set(c).at[nb].set(c_ctx)

    x, new_l0_ckv, new_l0_krope = _even_layer(x_sample.reshape(t_lat, d), x_prompt.reshape(n_ctx * ctx_seq, d),
                                              cond8, even, cache_l0_ckv, cache_l0_krope, **dims)
    lambda_init = 0.8 - 0.6 * math.exp(-0.3 * 1)
    x_lat, x_ctx, new_l1_k, new_l1_v = _odd_layer(x, cond8, odd, cache_l1_k, cache_l1_v, lambda_init, **dims)

    y_sample = x_lat.reshape(nb, lat_seq, d)
    y_prompt = x_ctx.reshape(n_ctx, ctx_seq, d)
    return (y_prompt, y_sample, new_l0_ckv, new_l0_krope, new_l1_k, new_l1_v)
```
